```python
import jax, jax.numpy as jnp
from jax import lax
import numpy as np

D_MODEL = 1024
BATCH = 8
SEQ = 4096
DEPTH = 4

N_MIXERS = 2
N_SSD_LAYERS = (DEPTH + 1) // 2
N_SB_LAYERS = DEPTH // 2
NORM_EPS = 1e-6

SSD_EXPAND = 2
SSD_D_INNER = SSD_EXPAND * D_MODEL
SSD_HEAD_DIM = 64
SSD_HEADS = SSD_D_INNER // SSD_HEAD_DIM
SSD_GROUPS = 8
SSD_HEADS_PER_GROUP = SSD_HEADS // SSD_GROUPS
SSD_STATE = 128
SSD_CONV = 4
SSD_CHUNK = 128
SSD_CONV_DIM = SSD_D_INNER + 2 * SSD_GROUPS * SSD_STATE
SSD_IN_DIM = SSD_D_INNER + SSD_CONV_DIM + SSD_HEADS
SSD_DT_MIN = 1e-3
SSD_DT_MAX = 1e-1

SB_HEADS = 16
SB_HEAD_DIM = D_MODEL // SB_HEADS
SB_Q_BLOCK = 128

FFN_D_FF = 2816
FFN_CONV = 3

kernel_name = "hybrid_ssd_stickbreaking_convffn"


def rms_norm(x, g):
    xf = x.astype(jnp.float32)
    y = xf * lax.rsqrt(jnp.mean(xf * xf, axis=-1, keepdims=True) + NORM_EPS)
    return (y * g.astype(jnp.float32)).astype(x.dtype)


def causal_dwconv(u, w, b):
    width = w.shape[0]
    s = u.shape[1]
    up = jnp.pad(u, ((0, 0), (width - 1, 0), (0, 0)))
    return b + sum(w[k] * up[:, k:k + s] for k in range(width))


def ssd_chunked_scan(xdt, a, bm, cm):
    bsz, s = xdt.shape[:2]
    L, G, HG, P, N = SSD_CHUNK, SSD_GROUPS, SSD_HEADS_PER_GROUP, SSD_HEAD_DIM, SSD_STATE
    nc = s // L

    def to_chunks(t):
        t = t.reshape((bsz, nc, L) + t.shape[2:])
        return jnp.moveaxis(t, 1, 0)

    xc = to_chunks(xdt.reshape(bsz, s, G, HG, P))
    ac = to_chunks(a.reshape(bsz, s, G, HG))
    bc = to_chunks(bm)
    cc = to_chunks(cm)
    causal = jnp.tril(jnp.ones((L, L), dtype=bool))[None, :, :, None, None]

    def step(state, inp):
        x_c, a_c, b_c, c_c = inp
        acum = jnp.cumsum(a_c, axis=1)
        seg = acum[:, :, None] - acum[:, None, :]
        decay = jnp.exp(jnp.where(causal, seg, -jnp.inf))
        cb = jnp.einsum('btgn,bsgn->btsg', c_c, b_c)
        scores = cb[..., None] * decay
        y_diag = jnp.einsum('btsgh,bsghp->btghp', scores, x_c)
        y_off = jnp.einsum('btgn,bghpn->btghp', c_c, state) * jnp.exp(acum)[..., None]
        last = acum[:, -1]
        w_s = jnp.exp(last[:, None] - acum)
        new_state = (state * jnp.exp(last)[..., None, None]
                     + jnp.einsum('bsgn,bsghp->bghpn', b_c, x_c * w_s[..., None]))
        return new_state, (y_diag + y_off).astype(jnp.float32)

    state0 = jnp.zeros((bsz, G, HG, P, N), jnp.float32)
    _, ys = lax.scan(step, state0, (xc, ac, bc, cc))
    return jnp.moveaxis(ys, 0, 1).reshape(bsz, s, SSD_HEADS, P)


def ssd_mixer(h, w_in, conv_w, conv_b, dt_bias, a_log, d_skip, norm_g, w_out):
    bsz, s, _ = h.shape
    proj = h @ w_in
    z, xbc, dt = jnp.split(proj, [SSD_D_INNER, SSD_D_INNER + SSD_CONV_DIM], axis=-1)
    xbc = jax.nn.silu(causal_dwconv(xbc, conv_w, conv_b))
    xs, bm, cm = jnp.split(xbc, [SSD_D_INNER, SSD_D_INNER + SSD_GROUPS * SSD_STATE], axis=-1)
    xs = xs.reshape(bsz, s, SSD_HEADS, SSD_HEAD_DIM)
    bm = bm.reshape(bsz, s, SSD_GROUPS, SSD_STATE)
    cm = cm.reshape(bsz, s, SSD_GROUPS, SSD_STATE)
    dt = jax.nn.softplus(dt.astype(jnp.float32) + dt_bias.astype(jnp.float32))
    a = -jnp.exp(a_log.astype(jnp.float32)) * dt
    y = ssd_chunked_scan(xs * dt[..., None], a, bm, cm)
    y = y + d_skip[:, None] * xs
    y = y.reshape(bsz, s, SSD_D_INNER).astype(h.dtype)
    y = rms_norm(y * jax.nn.silu(z), norm_g)
    return y @ w_out


def stick_breaking_mixer(h, w_qkv, w_out):
    bsz, s, _ = h.shape
    qkv = (h @ w_qkv).reshape(bsz, s, 3, SB_HEADS, SB_HEAD_DIM)
    q = jnp.moveaxis(qkv[:, :, 0], 1, 2)
    k = jnp.moveaxis(qkv[:, :, 1], 1, 2)
    v = jnp.moveaxis(qkv[:, :, 2], 1, 2)
    scale = SB_HEAD_DIM ** -0.5
    outs = []
    for blk in range(s // SB_Q_BLOCK):
        q0 = blk * SB_Q_BLOCK
        kv_end = q0 + SB_Q_BLOCK
        logits = jnp.einsum('bhqd,bhkd->bhqk', q[:, :, q0:kv_end], k[:, :, :kv_end]).astype(jnp.float32) * scale
        qpos = q0 + jnp.arange(SB_Q_BLOCK)[:, None]
        kpos = jnp.arange(kv_end)[None, :]
        strict = kpos < qpos
        log_beta = jax.nn.log_sigmoid(logits)
        log_fail = jnp.where(strict, jax.nn.log_sigmoid(-logits), 0.0)
        suffix = lax.cumsum(log_fail, axis=3, reverse=True) - log_fail
        weights = jnp.where(strict, jnp.exp(log_beta + suffix), 0.0)
        outs.append(jnp.einsum('bhqk,bhkd->bhqd', weights.astype(v.dtype), v[:, :, :kv_end]))
    o = jnp.concatenate(outs, axis=2)
    o = jnp.moveaxis(o, 1, 2).reshape(bsz, s, D_MODEL)
    return o @ w_out


def conv_ffn(h, w_in, conv_w, conv_b, w_out):
    u = causal_dwconv(h @ w_in, conv_w, conv_b)
    gate, up = jnp.split(u, 2, axis=-1)
    return (jax.nn.silu(gate) * up) @ w_out


def _fwd_setup_inputs(seed: int = 0) -> dict:
    key = jax.random.key(seed)
    ks = jax.random.split(key, 24)
    f32 = jnp.float32
    out_scale = (2.0 * DEPTH) ** -0.5

    def nrm(k, shape, scale):
        return jax.random.normal(k, shape, f32) * scale

    x = jax.random.normal(ks[0], (BATCH, SEQ, D_MODEL), f32)
    mix_norm = 1.0 + nrm(ks[1], (DEPTH, D_MODEL), 0.02)
    ffn_norm = 1.0 + nrm(ks[2], (DEPTH, D_MODEL), 0.02)
    final_norm = 1.0 + nrm(ks[3], (D_MODEL,), 0.02)

    ssd_w_in = nrm(ks[4], (N_SSD_LAYERS, D_MODEL, SSD_IN_DIM), D_MODEL ** -0.5)
    ssd_conv_w = nrm(ks[5], (N_SSD_LAYERS, SSD_CONV, SSD_CONV_DIM), SSD_CONV ** -0.5)
    ssd_conv_b = nrm(ks[6], (N_SSD_LAYERS, SSD_CONV_DIM), 0.02)
    u = jax.random.uniform(ks[7], (N_SSD_LAYERS, SSD_HEADS), f32)
    dt0 = jnp.exp(u * (np.log(SSD_DT_MAX) - np.log(SSD_DT_MIN)) + np.log(SSD_DT_MIN))
    ssd_dt_bias = dt0 + jnp.log(-jnp.expm1(-dt0))
    ssd_a_log = jnp.log(jax.random.uniform(ks[8], (N_SSD_LAYERS, SSD_HEADS), f32, 1.0, 16.0))
    ssd_d = 1.0 + nrm(ks[9], (N_SSD_LAYERS, SSD_HEADS), 0.1)
    ssd_norm = 1.0 + nrm(ks[10], (N_SSD_LAYERS, SSD_D_INNER), 0.02)
    ssd_w_out = nrm(ks[11], (N_SSD_LAYERS, SSD_D_INNER, D_MODEL), SSD_D_INNER ** -0.5 * out_scale)

    sb_w_qkv = nrm(ks[12], (N_SB_LAYERS, D_MODEL, 3 * D_MODEL), D_MODEL ** -0.5)
    sb_w_out = nrm(ks[13], (N_SB_LAYERS, D_MODEL, D_MODEL), D_MODEL ** -0.5 * out_scale)

    ffn_w_in = nrm(ks[14], (DEPTH, D_MODEL, 2 * FFN_D_FF), D_MODEL ** -0.5)
    ffn_conv_w = nrm(ks[15], (DEPTH, FFN_CONV, 2 * FFN_D_FF), FFN_CONV ** -0.5)
    ffn_conv_b = nrm(ks[16], (DEPTH, 2 * FFN_D_FF), 0.02)
    ffn_w_out = nrm(ks[17], (DEPTH, FFN_D_FF, D_MODEL), FFN_D_FF ** -0.5 * out_scale)

    return {"x": x, "mix_norm": mix_norm, "ffn_norm": ffn_norm, "final_norm": final_norm,
            "ssd_w_in": ssd_w_in, "ssd_conv_w": ssd_conv_w, "ssd_conv_b": ssd_conv_b,
            "ssd_dt_bias": ssd_dt_bias, "ssd_a_log": ssd_a_log, "ssd_d": ssd_d,
            "ssd_norm": ssd_norm, "ssd_w_out": ssd_w_out,
            "sb_w_qkv": sb_w_qkv, "sb_w_out": sb_w_out,
            "ffn_w_in": ffn_w_in, "ffn_conv_w": ffn_conv_w, "ffn_conv_b": ffn_conv_b,
            "ffn_w_out": ffn_w_out}


def _fwd_reference(x, mix_norm, ffn_norm, final_norm,
              ssd_w_in, ssd_conv_w, ssd_conv_b, ssd_dt_bias, ssd_a_log, ssd_d, ssd_norm, ssd_w_out,
              sb_w_qkv, sb_w_out,
              ffn_w_in, ffn_conv_w, ffn_conv_b, ffn_w_out):
    for i in range(DEPTH):
        h = rms_norm(x, mix_norm[i])
        j = i // N_MIXERS
        if i % N_MIXERS == 0:
            x = x + ssd_mixer(h, ssd_w_in[j], ssd_conv_w[j], ssd_conv_b[j], ssd_dt_bias[j],
                              ssd_a_log[j], ssd_d[j], ssd_norm[j], ssd_w_out[j])
        else:
            x = x + stick_breaking_mixer(h, sb_w_qkv[j], sb_w_out[j])
        x = x + conv_ffn(rms_norm(x, ffn_norm[i]), ffn_w_in[i], ffn_conv_w[i], ffn_conv_b[i], ffn_w_out[i])
    return rms_norm(x, final_norm)


import jax as _jax
import jax.numpy as _jnp

TWIN_FORMAT = 'train_step'
FWD_PARAMS = ['x', 'mix_norm', 'ffn_norm', 'final_norm', 'ssd_w_in', 'ssd_conv_w', 'ssd_conv_b', 'ssd_dt_bias', 'ssd_a_log', 'ssd_d', 'ssd_norm', 'ssd_w_out', 'sb_w_qkv', 'sb_w_out', 'ffn_w_in', 'ffn_conv_w', 'ffn_conv_b', 'ffn_w_out']
TWIN_WEIGHTS = ['mix_norm', 'ffn_norm', 'final_norm', 'ssd_w_in', 'ssd_conv_w', 'ssd_conv_b', 'ssd_dt_bias', 'ssd_a_log', 'ssd_d', 'ssd_norm', 'ssd_w_out', 'sb_w_qkv', 'sb_w_out', 'ffn_w_in', 'ffn_conv_w', 'ffn_conv_b', 'ffn_w_out']
TWIN_DIFF_INPUT = 'x'
TWIN_INPUTS = ['x', 'mix_norm', 'ffn_norm', 'final_norm', 'ssd_w_in', 'ssd_conv_w', 'ssd_conv_b', 'ssd_dt_bias', 'ssd_a_log', 'ssd_d', 'ssd_norm', 'ssd_w_out', 'sb_w_qkv', 'sb_w_out', 'ffn_w_in', 'ffn_conv_w', 'ffn_conv_b', 'ffn_w_out', 'loss_target', 'm_mix_norm', 'm_ffn_norm', 'm_final_norm', 'm_ssd_w_in', 'm_ssd_conv_w', 'm_ssd_conv_b', 'm_ssd_dt_bias', 'm_ssd_a_log', 'm_ssd_d', 'm_ssd_norm', 'm_ssd_w_out', 'm_sb_w_qkv', 'm_sb_w_out', 'm_ffn_w_in', 'm_ffn_conv_w', 'm_ffn_conv_b', 'm_ffn_w_out', 'v_mix_norm', 'v_ffn_norm', 'v_final_norm', 'v_ssd_w_in', 'v_ssd_conv_w', 'v_ssd_conv_b', 'v_ssd_dt_bias', 'v_ssd_a_log', 'v_ssd_d', 'v_ssd_norm', 'v_ssd_w_out', 'v_sb_w_qkv', 'v_sb_w_out', 'v_ffn_w_in', 'v_ffn_conv_w', 'v_ffn_conv_b', 'v_ffn_w_out']
TWIN_OUTPUTS = ['loss', 'grad_x', 'grad_mix_norm', 'grad_ffn_norm', 'grad_final_norm', 'grad_ssd_w_in', 'grad_ssd_conv_w', 'grad_ssd_conv_b', 'grad_ssd_dt_bias', 'grad_ssd_a_log', 'grad_ssd_d', 'grad_ssd_norm', 'grad_ssd_w_out', 'grad_sb_w_qkv', 'grad_sb_w_out', 'grad_ffn_w_in', 'grad_ffn_conv_w', 'grad_ffn_conv_b', 'grad_ffn_w_out', 'delta_mix_norm', 'delta_ffn_norm', 'delta_final_norm', 'delta_ssd_w_in', 'delta_ssd_conv_w', 'delta_ssd_conv_b', 'delta_ssd_dt_bias', 'delta_ssd_a_log', 'delta_ssd_d', 'delta_ssd_norm', 'delta_ssd_w_out', 'delta_sb_w_qkv', 'delta_sb_w_out', 'delta_ffn_w_in', 'delta_ffn_conv_w', 'delta_ffn_conv_b', 'delta_ffn_w_out', 'new_m_mix_norm', 'new_m_ffn_norm', 'new_m_final_norm', 'new_m_ssd_w_in', 'new_m_ssd_conv_w', 'new_m_ssd_conv_b', 'new_m_ssd_dt_bias', 'new_m_ssd_a_log', 'new_m_ssd_d', 'new_m_ssd_norm', 'new_m_ssd_w_out', 'new_m_sb_w_qkv', 'new_m_sb_w_out', 'new_m_ffn_w_in', 'new_m_ffn_conv_w', 'new_m_ffn_conv_b', 'new_m_ffn_w_out', 'new_v_mix_norm', 'new_v_ffn_norm', 'new_v_final_norm', 'new_v_ssd_w_in', 'new_v_ssd_conv_w', 'new_v_ssd_conv_b', 'new_v_ssd_dt_bias', 'new_v_ssd_a_log', 'new_v_ssd_d', 'new_v_ssd_norm', 'new_v_ssd_w_out', 'new_v_sb_w_qkv', 'new_v_sb_w_out', 'new_v_ffn_w_in', 'new_v_ffn_conv_w', 'new_v_ffn_conv_b', 'new_v_ffn_w_out']
TWIN_LEAF_KINDS = {'loss': 'loss', 'grad_x': 'grad_x', 'grad_mix_norm': 'grad_w', 'grad_ffn_norm': 'grad_w', 'grad_final_norm': 'grad_w', 'grad_ssd_w_in': 'grad_w', 'grad_ssd_conv_w': 'grad_w', 'grad_ssd_conv_b': 'grad_w', 'grad_ssd_dt_bias': 'grad_w', 'grad_ssd_a_log': 'grad_w', 'grad_ssd_d': 'grad_w', 'grad_ssd_norm': 'grad_w', 'grad_ssd_w_out': 'grad_w', 'grad_sb_w_qkv': 'grad_w', 'grad_sb_w_out': 'grad_w', 'grad_ffn_w_in': 'grad_w', 'grad_ffn_conv_w': 'grad_w', 'grad_ffn_conv_b': 'grad_w', 'grad_ffn_w_out': 'grad_w', 'delta_mix_norm': 'delta_w', 'delta_ffn_norm': 'delta_w', 'delta_final_norm': 'delta_w', 'delta_ssd_w_in': 'delta_w', 'delta_ssd_conv_w': 'delta_w', 'delta_ssd_conv_b': 'delta_w', 'delta_ssd_dt_bias': 'delta_w', 'delta_ssd_a_log': 'delta_w', 'delta_ssd_d': 'delta_w', 'delta_ssd_norm': 'delta_w', 'delta_ssd_w_out': 'delta_w', 'delta_sb_w_qkv': 'delta_w', 'delta_sb_w_out': 'delta_w', 'delta_ffn_w_in': 'delta_w', 'delta_ffn_conv_w': 'delta_w', 'delta_ffn_conv_b': 'delta_w', 'delta_ffn_w_out': 'delta_w', 'new_m_mix_norm': 'new_m', 'new_m_ffn_norm': 'new_m', 'new_m_final_norm': 'new_m', 'new_m_ssd_w_in': 'new_m', 'new_m_ssd_conv_w': 'new_m', 'new_m_ssd_conv_b': 'new_m', 'new_m_ssd_dt_bias': 'new_m', 'new_m_ssd_a_log': 'new_m', 'new_m_ssd_d': 'new_m', 'new_m_ssd_norm': 'new_m', 'new_m_ssd_w_out': 'new_m', 'new_m_sb_w_qkv': 'new_m', 'new_m_sb_w_out': 'new_m', 'new_m_ffn_w_in': 'new_m', 'new_m_ffn_conv_w': 'new_m', 'new_m_ffn_conv_b': 'new_m', 'new_m_ffn_w_out': 'new_m', 'new_v_mix_norm': 'new_v', 'new_v_ffn_norm': 'new_v', 'new_v_final_norm': 'new_v', 'new_v_ssd_w_in': 'new_v', 'new_v_ssd_conv_w': 'new_v', 'new_v_ssd_conv_b': 'new_v', 'new_v_ssd_dt_bias': 'new_v', 'new_v_ssd_a_log': 'new_v', 'new_v_ssd_d': 'new_v', 'new_v_ssd_norm': 'new_v', 'new_v_ssd_w_out': 'new_v', 'new_v_sb_w_qkv': 'new_v', 'new_v_sb_w_out': 'new_v', 'new_v_ffn_w_in': 'new_v', 'new_v_ffn_conv_w': 'new_v', 'new_v_ffn_conv_b': 'new_v', 'new_v_ffn_w_out': 'new_v'}


def _forward(args):
    return _fwd_reference(*[args[k] for k in FWD_PARAMS])


def _output_shape():
    out = _jax.eval_shape(lambda: _forward(_fwd_setup_inputs(0)))
    return out.shape, out.dtype

N_MICROBATCH = 1
ADAM_LR = 0.001
ADAM_B1 = 0.9
ADAM_B2 = 0.999
ADAM_EPS = 1e-08
ADAM_WD = 0.01
ADAM_STEP = 10
PER_EXAMPLE_BATCH_AXIS = {'x': 0, 'loss_target': 0}
SHARED_INPUTS = []
_WEIGHT_DTYPES = {'mix_norm': _jnp.float32, 'ffn_norm': _jnp.float32, 'final_norm': _jnp.float32, 'ssd_w_in': _jnp.float32, 'ssd_conv_w': _jnp.float32, 'ssd_conv_b': _jnp.float32, 'ssd_dt_bias': _jnp.float32, 'ssd_a_log': _jnp.float32, 'ssd_d': _jnp.float32, 'ssd_norm': _jnp.float32, 'ssd_w_out': _jnp.float32, 'sb_w_qkv': _jnp.float32, 'sb_w_out': _jnp.float32, 'ffn_w_in': _jnp.float32, 'ffn_conv_w': _jnp.float32, 'ffn_conv_b': _jnp.float32, 'ffn_w_out': _jnp.float32}
MOMENT_SCALE = {'mix_norm': 7.301789e-02, 'ffn_norm': 5.119474e-02, 'final_norm': 3.198605e+01, 'ssd_w_in': 3.787479e-02, 'ssd_conv_w': 3.383311e-02, 'ssd_conv_b': 4.974052e-02, 'ssd_dt_bias': 1.216512e-01, 'ssd_a_log': 1.837912e-01, 'ssd_d': 2.747645e-01, 'ssd_norm': 4.757064e-02, 'ssd_w_out': 1.757450e-01, 'sb_w_qkv': 2.593151e-02, 'sb_w_out': 1.072591e-01, 'ffn_w_in': 2.156896e-02, 'ffn_conv_w': 2.164177e-02, 'ffn_conv_b': 2.164268e-02, 'ffn_w_out': 9.960636e-02}


def _to_microbatches(a, axis):
    t = _jnp.moveaxis(a, axis, 0)
    t = t.reshape((N_MICROBATCH, t.shape[0] // N_MICROBATCH) + t.shape[1:])
    return _jnp.moveaxis(t, 1, axis + 1)


def setup_inputs(seed: int = 0) -> dict:
    inp = _fwd_setup_inputs(seed)
    key = _jax.random.fold_in(_jax.random.key(seed), 7919)
    shape, _ = _output_shape()
    out = dict(inp)
    out["loss_target"] = _jax.random.normal(_jax.random.fold_in(key, 0), shape, _jnp.float32)
    for i, name in enumerate(TWIN_WEIGHTS):
        w = inp[name].astype(_jnp.float32)
        if MOMENT_SCALE is None:
            s = _jnp.sqrt(_jnp.mean(_jnp.square(w)) + 1e-30)
        else:
            s = MOMENT_SCALE[name]
        km, kv = _jax.random.split(_jax.random.fold_in(key, i + 1))
        out[name] = w
        out["m_" + name] = s * _jax.random.normal(km, w.shape, _jnp.float32)
        out["v_" + name] = (s * s) * _jax.random.uniform(kv, w.shape, _jnp.float32, 0.5, 1.5)
    if N_MICROBATCH > 1:
        for name, axis in PER_EXAMPLE_BATCH_AXIS.items():
            out[name] = _to_microbatches(out[name], axis)
    return {'x': out['x'], 'mix_norm': out['mix_norm'], 'ffn_norm': out['ffn_norm'], 'final_norm': out['final_norm'], 'ssd_w_in': out['ssd_w_in'], 'ssd_conv_w': out['ssd_conv_w'], 'ssd_conv_b': out['ssd_conv_b'], 'ssd_dt_bias': out['ssd_dt_bias'], 'ssd_a_log': out['ssd_a_log'], 'ssd_d': out['ssd_d'], 'ssd_norm': out['ssd_norm'], 'ssd_w_out': out['ssd_w_out'], 'sb_w_qkv': out['sb_w_qkv'], 'sb_w_out': out['sb_w_out'], 'ffn_w_in': out['ffn_w_in'], 'ffn_conv_w': out['ffn_conv_w'], 'ffn_conv_b': out['ffn_conv_b'], 'ffn_w_out': out['ffn_w_out'], 'loss_target': out['loss_target'], 'm_mix_norm': out['m_mix_norm'], 'm_ffn_norm': out['m_ffn_norm'], 'm_final_norm': out['m_final_norm'], 'm_ssd_w_in': out['m_ssd_w_in'], 'm_ssd_conv_w': out['m_ssd_conv_w'], 'm_ssd_conv_b': out['m_ssd_conv_b'], 'm_ssd_dt_bias': out['m_ssd_dt_bias'], 'm_ssd_a_log': out['m_ssd_a_log'], 'm_ssd_d': out['m_ssd_d'], 'm_ssd_norm': out['m_ssd_norm'], 'm_ssd_w_out': out['m_ssd_w_out'], 'm_sb_w_qkv': out['m_sb_w_qkv'], 'm_sb_w_out': out['m_sb_w_out'], 'm_ffn_w_in': out['m_ffn_w_in'], 'm_ffn_conv_w': out['m_ffn_conv_w'], 'm_ffn_conv_b': out['m_ffn_conv_b'], 'm_ffn_w_out': out['m_ffn_w_out'], 'v_mix_norm': out['v_mix_norm'], 'v_ffn_norm': out['v_ffn_norm'], 'v_final_norm': out['v_final_norm'], 'v_ssd_w_in': out['v_ssd_w_in'], 'v_ssd_conv_w': out['v_ssd_conv_w'], 'v_ssd_conv_b': out['v_ssd_conv_b'], 'v_ssd_dt_bias': out['v_ssd_dt_bias'], 'v_ssd_a_log': out['v_ssd_a_log'], 'v_ssd_d': out['v_ssd_d'], 'v_ssd_norm': out['v_ssd_norm'], 'v_ssd_w_out': out['v_ssd_w_out'], 'v_sb_w_qkv': out['v_sb_w_qkv'], 'v_sb_w_out': out['v_sb_w_out'], 'v_ffn_w_in': out['v_ffn_w_in'], 'v_ffn_conv_w': out['v_ffn_conv_w'], 'v_ffn_conv_b': out['v_ffn_conv_b'], 'v_ffn_w_out': out['v_ffn_w_out']}


def _loss(weights, diff, rest, loss_target):
    with _jax.named_scope("forward"):
        args = {**rest, TWIN_DIFF_INPUT: diff, **{k: w.astype(_WEIGHT_DTYPES[k]) for k, w in weights.items()}}
        y = _forward(args)
    with _jax.named_scope("loss_head"):
        err = _jnp.square(y.astype(_jnp.float32) - loss_target)
        return 0.5 * _jnp.sum(_jnp.mean(err, axis=-1)) if err.ndim else 0.5 * err


def _adamw(w, g, m, v):
    m = ADAM_B1 * m + (1.0 - ADAM_B1) * g
    v = ADAM_B2 * v + (1.0 - ADAM_B2) * _jnp.square(g)
    m_hat = m / (1.0 - ADAM_B1 ** ADAM_STEP)
    v_hat = v / (1.0 - ADAM_B2 ** ADAM_STEP)
    delta = -ADAM_LR * (m_hat / (_jnp.sqrt(v_hat) + ADAM_EPS) + ADAM_WD * w)
    return delta, m, v


def reference(x, mix_norm, ffn_norm, final_norm, ssd_w_in, ssd_conv_w, ssd_conv_b, ssd_dt_bias, ssd_a_log, ssd_d, ssd_norm, ssd_w_out, sb_w_qkv, sb_w_out, ffn_w_in, ffn_conv_w, ffn_conv_b, ffn_w_out, loss_target, m_mix_norm, m_ffn_norm, m_final_norm, m_ssd_w_in, m_ssd_conv_w, m_ssd_conv_b, m_ssd_dt_bias, m_ssd_a_log, m_ssd_d, m_ssd_norm, m_ssd_w_out, m_sb_w_qkv, m_sb_w_out, m_ffn_w_in, m_ffn_conv_w, m_ffn_conv_b, m_ffn_w_out, v_mix_norm, v_ffn_norm, v_final_norm, v_ssd_w_in, v_ssd_conv_w, v_ssd_conv_b, v_ssd_dt_bias, v_ssd_a_log, v_ssd_d, v_ssd_norm, v_ssd_w_out, v_sb_w_qkv, v_sb_w_out, v_ffn_w_in, v_ffn_conv_w, v_ffn_conv_b, v_ffn_w_out):
    given = dict(x=x, mix_norm=mix_norm, ffn_norm=ffn_norm, final_norm=final_norm, ssd_w_in=ssd_w_in, ssd_conv_w=ssd_conv_w, ssd_conv_b=ssd_conv_b, ssd_dt_bias=ssd_dt_bias, ssd_a_log=ssd_a_log, ssd_d=ssd_d, ssd_norm=ssd_norm, ssd_w_out=ssd_w_out, sb_w_qkv=sb_w_qkv, sb_w_out=sb_w_out, ffn_w_in=ffn_w_in, ffn_conv_w=ffn_conv_w, ffn_conv_b=ffn_conv_b, ffn_w_out=ffn_w_out, loss_target=loss_target, m_mix_norm=m_mix_norm, m_ffn_norm=m_ffn_norm, m_final_norm=m_final_norm, m_ssd_w_in=m_ssd_w_in, m_ssd_conv_w=m_ssd_conv_w, m_ssd_conv_b=m_ssd_conv_b, m_ssd_dt_bias=m_ssd_dt_bias, m_ssd_a_log=m_ssd_a_log, m_ssd_d=m_ssd_d, m_ssd_norm=m_ssd_norm, m_ssd_w_out=m_ssd_w_out, m_sb_w_qkv=m_sb_w_qkv, m_sb_w_out=m_sb_w_out, m_ffn_w_in=m_ffn_w_in, m_ffn_conv_w=m_ffn_conv_w, m_ffn_conv_b=m_ffn_conv_b, m_ffn_w_out=m_ffn_w_out, v_mix_norm=v_mix_norm, v_ffn_norm=v_ffn_norm, v_final_norm=v_final_norm, v_ssd_w_in=v_ssd_w_in, v_ssd_conv_w=v_ssd_conv_w, v_ssd_conv_b=v_ssd_conv_b, v_ssd_dt_bias=v_ssd_dt_bias, v_ssd_a_log=v_ssd_a_log, v_ssd_d=v_ssd_d, v_ssd_norm=v_ssd_norm, v_ssd_w_out=v_ssd_w_out, v_sb_w_qkv=v_sb_w_qkv, v_sb_w_out=v_sb_w_out, v_ffn_w_in=v_ffn_w_in, v_ffn_conv_w=v_ffn_conv_w, v_ffn_conv_b=v_ffn_conv_b, v_ffn_w_out=v_ffn_w_out)
    weights = {n: given[n] for n in TWIN_WEIGHTS}
    shared = {n: given[n] for n in SHARED_INPUTS}
    per_example = {n: given[n] for n in ['x']}
    grad_fn = _jax.value_and_grad(_loss, argnums=(0, 1))

    def one_microbatch(ex, loss_target):
        ex = dict(ex)
        diff = ex.pop(TWIN_DIFF_INPUT)
        return grad_fn(weights, diff, {**shared, **ex}, loss_target)

    if N_MICROBATCH == 1:
        loss, (grad_w, grad_x) = one_microbatch(per_example, given["loss_target"])
    else:
        def body(carry, xs):
            loss_sum, grad_sum = carry
            l_k, (gw_k, gx_k) = one_microbatch(xs[0], xs[1])
            with _jax.named_scope("update"):
                return (loss_sum + l_k, _jax.tree.map(_jnp.add, grad_sum, gw_k)), gx_k

        init = (_jnp.zeros((), _jnp.float32), _jax.tree.map(_jnp.zeros_like, weights))
        (loss, grad_w), grad_x = _jax.lax.scan(body, init, (per_example, given["loss_target"]))
    with _jax.named_scope("update"):
        delta_w, new_m, new_v = {}, {}, {}
        for n in TWIN_WEIGHTS:
            delta_w[n], new_m[n], new_v[n] = _adamw(weights[n], grad_w[n], given["m_" + n], given["v_" + n])
    return (loss, grad_x, *[grad_w[n] for n in TWIN_WEIGHTS], *[delta_w[n] for n in TWIN_WEIGHTS],
            *[new_m[n] for n in TWIN_WEIGHTS], *[new_v[n] for n in TWIN_WEIGHTS])
```

```python
import functools

import jax
import jax.numpy as jnp
import numpy as np
from jax import lax
from jax.experimental import pallas as pl
from jax.experimental.pallas import tpu as pltpu

F32 = jnp.float32
MXU_DTYPE = jnp.bfloat16
MESH_ID = pl.DeviceIdType.MESH
N_DEV = 8

NORM_EPS = 1e-6
D_MODEL = 1024
DEPTH = 4
SSD_D_INNER = 2048
SSD_HEADS = 32
SSD_HEAD_DIM = 64
SSD_GROUPS = 8
SSD_STATE = 128
SSD_CONV = 4
SSD_CHUNK = 128
SSD_CONV_DIM = SSD_D_INNER + 2 * SSD_GROUPS * SSD_STATE
SSD_IN_DIM = SSD_D_INNER + SSD_CONV_DIM + SSD_HEADS
LANES = 128
SSD_IN_PAD = SSD_D_INNER + SSD_CONV_DIM + LANES
SB_HEADS = 16
SB_HEAD_DIM = 64
SB_BLOCK = 128
SB_SCALE = SB_HEAD_DIM ** -0.5
FFN_D_FF = 2816
FFN_CONV = 3
PACK_W = 1024

ADAM_LR = 0.001
ADAM_B1 = 0.9
ADAM_B2 = 0.999
ADAM_EPS = 1e-08
ADAM_WD = 0.01
ADAM_STEP = 10

VMEM_LIMIT_BYTES = 56 * 1024 * 1024


def _cp(*sem):
    return pltpu.CompilerParams(dimension_semantics=sem, vmem_limit_bytes=VMEM_LIMIT_BYTES)


def _iota(shape, dim):
    return lax.broadcasted_iota(jnp.int32, shape, dim)


def _sigmoid(x):
    return 1.0 / (1.0 + jnp.exp(-x))


def _mm(a, b):
    return lax.dot_general(a, b, (((1,), (0,)), ((), ())), preferred_element_type=F32)


def _mm_nt(a, b):
    return lax.dot_general(a, b, (((1,), (1,)), ((), ())), preferred_element_type=F32)


def _mm_tn(a, b):
    return lax.dot_general(a, b, (((0,), (0,)), ((), ())), preferred_element_type=F32)


def _split(x):
    hi = x.astype(MXU_DTYPE)
    lo = (x - hi.astype(F32)).astype(MXU_DTYPE)
    return hi, lo


def _mm_exact_rhs(x, m):
    hi, lo = _split(x)
    return _mm(hi, m) + _mm(lo, m)


def _mm_exact_lhs(m, x):
    hi, lo = _split(x)
    return _mm(m, hi) + _mm(m, lo)


def _my_place():
    return lax.axis_index("x"), lax.axis_index("y"), lax.axis_index("c")


def _all_gather(shard, name):
    r, c_ = shard.shape

    def body(x_ref, out_ref, send_sems, recv_sems, local_sem):
        x, y, c = _my_place()
        me, sibling = (x, y, c), (x, y, 1 - c)
        chips = [(1 - x, y), (x, 1 - y), (1 - x, 1 - y)]

        def slot(px, py, pc):
            return out_ref.at[4 * px + 2 * py + pc]

        def copy(k, block, to, src=None):
            return pltpu.make_async_remote_copy(
                src_ref=slot(*block) if src is None else src, dst_ref=slot(*block),
                send_sem=send_sems.at[k], recv_sem=recv_sems.at[k], device_id=to, device_id_type=MESH_ID)

        mine = pltpu.make_async_copy(x_ref, slot(*me), local_sem)
        mine.start()
        first = [copy(0, me, sibling, src=x_ref)]
        first += [copy(1 + j, me, (*chip, c), src=x_ref) for j, chip in enumerate(chips)]
        for cp in first:
            cp.start()
        passed = [copy(4 + j, (*chip, c), sibling) for j, chip in enumerate(chips)]
        for j, chip in enumerate(chips):
            copy(1 + j, (*chip, c), me).wait_recv()
            passed[j].start()
        copy(0, sibling, me).wait_recv()
        for j, chip in enumerate(chips):
            copy(4 + j, (*chip, 1 - c), me).wait_recv()
        for cp in first + passed:
            cp.wait_send()
        mine.wait()

    return pl.pallas_call(
        body, name=name,
        out_shape=jax.ShapeDtypeStruct((N_DEV, r, c_), shard.dtype),
        in_specs=[pl.BlockSpec(memory_space=pl.ANY)],
        out_specs=pl.BlockSpec(memory_space=pl.ANY),
        scratch_shapes=[pltpu.SemaphoreType.DMA((7,)), pltpu.SemaphoreType.DMA((7,)), pltpu.SemaphoreType.DMA(())],
    )(shard)


def _swap_with_sibling(buf, name):
    def body(x_ref, out_ref, send_sem, recv_sem):
        x, y, c = _my_place()
        cp = pltpu.make_async_remote_copy(src_ref=x_ref, dst_ref=out_ref, send_sem=send_sem, recv_sem=recv_sem,
                                          device_id=(x, y, 1 - c), device_id_type=MESH_ID)
        cp.start()
        cp.wait()

    return pl.pallas_call(
        body, name=name, out_shape=jax.ShapeDtypeStruct(buf.shape, buf.dtype),
        in_specs=[pl.BlockSpec(memory_space=pl.ANY)], out_specs=pl.BlockSpec(memory_space=pl.ANY),
        scratch_shapes=[pltpu.SemaphoreType.DMA(()), pltpu.SemaphoreType.DMA(())],
    )(buf)


def _exchange_chips(parts, name):
    def body(p_ref, out_ref, send_sems, recv_sems, local_sem):
        x, y, c = _my_place()
        my_q = 2 * x + y
        chips = [(1 - x, y), (x, 1 - y), (1 - x, 1 - y)]
        local = pltpu.make_async_copy(p_ref.at[my_q], out_ref.at[my_q], local_sem)
        local.start()

        def copy(k, px, py):
            return pltpu.make_async_remote_copy(
                src_ref=p_ref.at[2 * px + py], dst_ref=out_ref.at[my_q],
                send_sem=send_sems.at[k], recv_sem=recv_sems.at[k], device_id=(px, py, c), device_id_type=MESH_ID)

        def landing(k, px, py):
            return pltpu.make_async_remote_copy(
                src_ref=p_ref.at[my_q], dst_ref=out_ref.at[2 * px + py],
                send_sem=send_sems.at[k], recv_sem=recv_sems.at[k], device_id=(px, py, c), device_id_type=MESH_ID)

        sends = [copy(k, px, py) for k, (px, py) in enumerate(chips)]
        for cp in sends:
            cp.start()
        for k, (px, py) in enumerate(chips):
            landing(k, px, py).wait_recv()
        for cp in sends:
            cp.wait_send()
        local.wait()

    return pl.pallas_call(
        body, name=name, out_shape=jax.ShapeDtypeStruct(parts.shape, parts.dtype),
        in_specs=[pl.BlockSpec(memory_space=pl.ANY)], out_specs=pl.BlockSpec(memory_space=pl.ANY),
        scratch_shapes=[pltpu.SemaphoreType.DMA((3,)), pltpu.SemaphoreType.DMA((3,)), pltpu.SemaphoreType.DMA(())],
    )(parts)


def _matmul(a, b, mode, out_dtype, name, add=None, tm=512, tn=512, tk=512):
    if mode == "nn":
        (m, k), (k2, n) = a.shape, b.shape
    elif mode == "nt":
        (m, k), (n, k2) = a.shape, b.shape
    else:
        (k, m), (k2, n) = a.shape, b.shape
    assert k == k2, (a.shape, b.shape, mode)
    tm, tn, tk = min(tm, m), min(tn, n), min(tk, k)
    assert m % tm == 0 and n % tn == 0 and k % tk == 0, (m, n, k, tm, tn, tk)
    nk = k // tk
    mm = {"nn": _mm, "nt": _mm_nt, "tn": _mm_tn}[mode]

    def body(*refs):
        if add is None:
            a_ref, b_ref, o_ref, acc_ref = refs
        else:
            a_ref, b_ref, add_ref, o_ref, acc_ref = refs
        kk = pl.program_id(2)

        @pl.when(kk == 0)
        def _():
            acc_ref[...] = jnp.zeros_like(acc_ref)

        acc_ref[...] += mm(a_ref[...].astype(MXU_DTYPE), b_ref[...].astype(MXU_DTYPE))

        @pl.when(kk == nk - 1)
        def _():
            res = acc_ref[...]
            if add is not None:
                res = res + add_ref[...]
            o_ref[...] = res.astype(o_ref.dtype)

    a_spec = {"nn": pl.BlockSpec((tm, tk), lambda i, j, kk: (i, kk)),
              "nt": pl.BlockSpec((tm, tk), lambda i, j, kk: (i, kk)),
              "tn": pl.BlockSpec((tk, tm), lambda i, j, kk: (kk, i))}[mode]
    b_spec = {"nn": pl.BlockSpec((tk, tn), lambda i, j, kk: (kk, j)),
              "nt": pl.BlockSpec((tn, tk), lambda i, j, kk: (j, kk)),
              "tn": pl.BlockSpec((tk, tn), lambda i, j, kk: (kk, j))}[mode]
    o_spec = pl.BlockSpec((tm, tn), lambda i, j, kk: (i, j))
    in_specs, args = [a_spec, b_spec], [a, b]
    if add is not None:
        in_specs.append(o_spec)
        args.append(add)
    return pl.pallas_call(
        body, name=name, grid=(m // tm, n // tn, nk), in_specs=in_specs, out_specs=o_spec,
        out_shape=jax.ShapeDtypeStruct((m, n), out_dtype),
        scratch_shapes=[pltpu.VMEM((tm, tn), F32)],
        compiler_params=_cp("parallel", "parallel", "arbitrary"),
    )(*args)


def _rmsnorm(x, g, name):
    t, d = x.shape
    tm = min(512, t)

    def body(x_ref, g_ref, o_ref):
        xv = x_ref[...]
        r = lax.rsqrt(jnp.mean(xv * xv, axis=-1, keepdims=True) + NORM_EPS)
        o_ref[...] = (xv * r * g_ref[...]).astype(o_ref.dtype)

    return pl.pallas_call(
        body, name=name, grid=(t // tm,),
        in_specs=[pl.BlockSpec((tm, d), lambda i: (i, 0)), pl.BlockSpec((1, d), lambda i: (0, 0))],
        out_specs=pl.BlockSpec((tm, d), lambda i: (i, 0)),
        out_shape=jax.ShapeDtypeStruct((t, d), MXU_DTYPE), compiler_params=_cp("parallel"),
    )(x, g.reshape(1, d))


def _rmsnorm_bwd(x, g, dh, dres, name):
    t, d = x.shape
    tm = min(512, t)

    def body(x_ref, g_ref, dh_ref, dres_ref, dx_ref, dg_ref):
        @pl.when(pl.program_id(0) == 0)
        def _():
            dg_ref[...] = jnp.zeros_like(dg_ref)

        xv = x_ref[...]
        r = lax.rsqrt(jnp.mean(xv * xv, axis=-1, keepdims=True) + NORM_EPS)
        xn = xv * r
        dhv = dh_ref[...]
        u = dhv * g_ref[...]
        dx_ref[...] = dres_ref[...] + r * (u - xn * jnp.mean(u * xn, axis=-1, keepdims=True))
        dg_ref[...] += jnp.sum(dhv * xn, axis=0, keepdims=True)

    row = pl.BlockSpec((tm, d), lambda i: (i, 0))
    vec = pl.BlockSpec((1, d), lambda i: (0, 0))
    return pl.pallas_call(
        body, name=name, grid=(t // tm,), in_specs=[row, vec, row, row], out_specs=[row, vec],
        out_shape=[jax.ShapeDtypeStruct((t, d), F32), jax.ShapeDtypeStruct((1, d), F32)],
        compiler_params=_cp("arbitrary"),
    )(x, g.reshape(1, d), dh, dres)


def _final_norm_loss(x, g, target, name):
    t, d = x.shape
    tm = min(512, t)

    def body(x_ref, g_ref, t_ref, dx_ref, dg_ref, loss_ref):
        @pl.when(pl.program_id(0) == 0)
        def _():
            dg_ref[...] = jnp.zeros_like(dg_ref)
            loss_ref[...] = jnp.zeros_like(loss_ref)

        xv = x_ref[...]
        gv = g_ref[...]
        r = lax.rsqrt(jnp.mean(xv * xv, axis=-1, keepdims=True) + NORM_EPS)
        xn = xv * r
        err = xn * gv - t_ref[...]
        per_tok = jnp.mean(err * err, axis=-1, keepdims=True)
        loss_ref[...] += jnp.broadcast_to(0.5 * jnp.sum(per_tok, axis=0, keepdims=True), loss_ref.shape)
        dy = err * (1.0 / d)
        u = dy * gv
        dx_ref[...] = r * (u - xn * jnp.mean(u * xn, axis=-1, keepdims=True))
        dg_ref[...] += jnp.sum(dy * xn, axis=0, keepdims=True)

    row = pl.BlockSpec((tm, d), lambda i: (i, 0))
    vec = pl.BlockSpec((1, d), lambda i: (0, 0))
    return pl.pallas_call(
        body, name=name, grid=(t // tm,), in_specs=[row, vec, row],
        out_specs=[row, vec, pl.BlockSpec((1, LANES), lambda i: (0, 0))],
        out_shape=[jax.ShapeDtypeStruct((t, d), F32), jax.ShapeDtypeStruct((1, d), F32),
                   jax.ShapeDtypeStruct((1, LANES), F32)],
        compiler_params=_cp("arbitrary"),
    )(x, g.reshape(1, d), target)


def _shift_down(p, s, row):
    return jnp.where(row >= s, pltpu.roll(p, s, axis=0), 0.0)


def _shift_up(p, s, row):
    n = p.shape[0]
    return jnp.where(row < n - s, pltpu.roll(p, n - s, axis=0), 0.0)


def _conv_pre(p, w_ref, b_ref, row):
    width = w_ref.shape[0]
    u = b_ref[...] + w_ref[width - 1:width, :] * p
    for s in range(1, width):
        u = u + w_ref[width - 1 - s:width - s, :] * _shift_down(p, s, row)
    return u


def _conv_transpose(du, w_ref, row):
    width = w_ref.shape[0]
    dp = w_ref[width - 1:width, :] * du
    for s in range(1, width):
        dp = dp + w_ref[width - 1 - s:width - s, :] * _shift_up(du, s, row)
    return dp


def _conv_wgrad(du, p, dw_ref, db_ref, row):
    width = dw_ref.shape[0]
    db_ref[...] = jnp.sum(du, axis=0, keepdims=True)
    dw_ref[width - 1:width, :] = jnp.sum(du * p, axis=0, keepdims=True)
    for s in range(1, width):
        dw_ref[width - 1 - s:width - s, :] = jnp.sum(du * _shift_down(p, s, row), axis=0, keepdims=True)


CONV_COLS = 128


def _ssd_conv_fwd(proj, w, b, name):
    t = proj.shape[0]
    cb = CONV_COLS
    off = SSD_D_INNER // cb

    def body(p_ref, w_ref, b_ref, o_ref):
        p = p_ref[...]
        row = _iota(p.shape, 0)
        u = _conv_pre(p, w_ref, b_ref, row)
        o_ref[...] = u * _sigmoid(u)

    return pl.pallas_call(
        body, name=name, grid=(SSD_CONV_DIM // cb,),
        in_specs=[pl.BlockSpec((t, cb), lambda j: (0, j + off)), pl.BlockSpec((SSD_CONV, cb), lambda j: (0, j)),
                  pl.BlockSpec((1, cb), lambda j: (0, j))],
        out_specs=pl.BlockSpec((t, cb), lambda j: (0, j)),
        out_shape=jax.ShapeDtypeStruct((t, SSD_CONV_DIM), F32), compiler_params=_cp("parallel"),
    )(proj, w, b.reshape(1, -1))


def _ssd_conv_bwd(proj, w, b, dact, name):
    t = proj.shape[0]
    cb = CONV_COLS
    off = SSD_D_INNER // cb

    def body(p_ref, w_ref, b_ref, da_ref, dp_ref, dw_ref, db_ref):
        p = p_ref[...]
        row = _iota(p.shape, 0)
        u = _conv_pre(p, w_ref, b_ref, row)
        sg = _sigmoid(u)
        du = da_ref[...] * (sg * (1.0 + u * (1.0 - sg)))
        dp_ref[...] = _conv_transpose(du, w_ref, row).astype(dp_ref.dtype)
        _conv_wgrad(du, p, dw_ref, db_ref, row)

    col = pl.BlockSpec((t, cb), lambda j: (0, j))
    wspec = pl.BlockSpec((SSD_CONV, cb), lambda j: (0, j))
    bspec = pl.BlockSpec((1, cb), lambda j: (0, j))
    return pl.pallas_call(
        body, name=name, grid=(SSD_CONV_DIM // cb,),
        in_specs=[pl.BlockSpec((t, cb), lambda j: (0, j + off)), wspec, bspec, col],
        out_specs=[col, wspec, bspec],
        out_shape=[jax.ShapeDtypeStruct((t, SSD_CONV_DIM), MXU_DTYPE), jax.ShapeDtypeStruct((SSD_CONV, SSD_CONV_DIM), F32),
                   jax.ShapeDtypeStruct((1, SSD_CONV_DIM), F32)],
        compiler_params=_cp("parallel"),
    )(proj, w, b.reshape(1, -1), dact)


def _ffn_conv_fwd(proj, w, b, name):
    t = proj.shape[0]
    cb = CONV_COLS
    nb = FFN_D_FF // cb

    def body(pg_ref, pu_ref, wg_ref, wu_ref, bg_ref, bu_ref, o_ref):
        row = _iota(pg_ref.shape, 0)
        ug = _conv_pre(pg_ref[...], wg_ref, bg_ref, row)
        uu = _conv_pre(pu_ref[...], wu_ref, bu_ref, row)
        o_ref[...] = (ug * _sigmoid(ug) * uu).astype(o_ref.dtype)

    gcol = pl.BlockSpec((t, cb), lambda j: (0, j))
    ucol = pl.BlockSpec((t, cb), lambda j: (0, j + nb))
    b2 = b.reshape(1, -1)
    return pl.pallas_call(
        body, name=name, grid=(nb,),
        in_specs=[gcol, ucol, pl.BlockSpec((FFN_CONV, cb), lambda j: (0, j)), pl.BlockSpec((FFN_CONV, cb), lambda j: (0, j + nb)),
                  pl.BlockSpec((1, cb), lambda j: (0, j)), pl.BlockSpec((1, cb), lambda j: (0, j + nb))],
        out_specs=gcol, out_shape=jax.ShapeDtypeStruct((t, FFN_D_FF), MXU_DTYPE), compiler_params=_cp("parallel"),
    )(proj, proj, w, w, b2, b2)


def _ffn_conv_bwd(proj, w, b, dact, name):
    t = proj.shape[0]
    cb = CONV_COLS
    nb = FFN_D_FF // cb

    def body(pg_ref, pu_ref, wg_ref, wu_ref, bg_ref, bu_ref, da_ref,
             dpg_ref, dpu_ref, dwg_ref, dwu_ref, dbg_ref, dbu_ref):
        pg, pu = pg_ref[...], pu_ref[...]
        row = _iota(pg.shape, 0)
        ug = _conv_pre(pg, wg_ref, bg_ref, row)
        uu = _conv_pre(pu, wu_ref, bu_ref, row)
        sg = _sigmoid(ug)
        da = da_ref[...]
        dug = da * uu * (sg * (1.0 + ug * (1.0 - sg)))
        duu = da * (ug * sg)
        dpg_ref[...] = _conv_transpose(dug, wg_ref, row).astype(dpg_ref.dtype)
        dpu_ref[...] = _conv_transpose(duu, wu_ref, row).astype(dpu_ref.dtype)
        _conv_wgrad(dug, pg, dwg_ref, dbg_ref, row)
        _conv_wgrad(duu, pu, dwu_ref, dbu_ref, row)

    gcol = pl.BlockSpec((t, cb), lambda j: (0, j))
    ucol = pl.BlockSpec((t, cb), lambda j: (0, j + nb))
    wg = pl.BlockSpec((FFN_CONV, cb), lambda j: (0, j))
    wu = pl.BlockSpec((FFN_CONV, cb), lambda j: (0, j + nb))
    bg = pl.BlockSpec((1, cb), lambda j: (0, j))
    bu = pl.BlockSpec((1, cb), lambda j: (0, j + nb))
    b2 = b.reshape(1, -1)
    half = jax.ShapeDtypeStruct((t, FFN_D_FF), MXU_DTYPE)
    return pl.pallas_call(
        body, name=name, grid=(nb,),
        in_specs=[gcol, ucol, wg, wu, bg, bu, gcol],
        out_specs=[gcol, gcol, wg, wg, bg, bg],
        out_shape=[half, half, jax.ShapeDtypeStruct((FFN_CONV, FFN_D_FF), F32), jax.ShapeDtypeStruct((FFN_CONV, FFN_D_FF), F32),
                   jax.ShapeDtypeStruct((1, FFN_D_FF), F32), jax.ShapeDtypeStruct((1, FFN_D_FF), F32)],
        compiler_params=_cp("parallel"),
    )(proj, proj, w, w, b2, b2, dact)


SSD_ROWS = 128
DT_COL = (SSD_D_INNER + SSD_CONV_DIM) // LANES


def _head_expand():
    h = np.arange(LANES)[:, None]
    col = np.arange(SSD_D_INNER)[None, :]
    return jnp.asarray((col // SSD_HEAD_DIM == h), MXU_DTYPE)


def _chunk_tri(n, lower):
    t = _iota((n, n), 0)
    s = _iota((n, n), 1)
    shift = SSD_CHUNK.bit_length() - 1
    same = jnp.right_shift(t, shift) == jnp.right_shift(s, shift)
    tri = (s <= t) if lower else (s >= t)
    return jnp.where(same & tri, 1.0, 0.0).astype(MXU_DTYPE)


def _softplus(x):
    return jnp.maximum(x, 0.0) + jnp.log(1.0 + jnp.exp(-jnp.abs(x)))


def _ssd_dt_fwd(proj, act, dt_bias, a_neg, expand, name):
    t = proj.shape[0]
    tm = min(SSD_ROWS, t)

    def body(raw_ref, xs_ref, bias_ref, a_ref, e_ref, xdt_ref, dt_ref, acum_ref):
        lane = _iota((tm, LANES), 1)
        dt = jnp.where(lane < SSD_HEADS, _softplus(raw_ref[...] + bias_ref[...]), 0.0)
        dt_ref[...] = dt
        xdt_ref[...] = xs_ref[...] * _mm_exact_rhs(dt, e_ref[...])
        acum_ref[...] = _mm_exact_lhs(_chunk_tri(tm, True), a_ref[...] * dt)

    vec = pl.BlockSpec((1, LANES), lambda i: (0, 0))
    return pl.pallas_call(
        body, name=name, grid=(t // tm,),
        in_specs=[pl.BlockSpec((tm, LANES), lambda i: (i, DT_COL)), pl.BlockSpec((tm, SSD_D_INNER), lambda i: (i, 0)),
                  vec, vec, pl.BlockSpec((LANES, SSD_D_INNER), lambda i: (0, 0))],
        out_specs=[pl.BlockSpec((tm, SSD_D_INNER), lambda i: (i, 0)), pl.BlockSpec((tm, LANES), lambda i: (i, 0)),
                   pl.BlockSpec((tm, LANES), lambda i: (i, 0))],
        out_shape=[jax.ShapeDtypeStruct((t, SSD_D_INNER), F32), jax.ShapeDtypeStruct((t, LANES), F32),
                   jax.ShapeDtypeStruct((t, LANES), F32)],
        compiler_params=_cp("parallel"),
    )(proj, act, dt_bias, a_neg, expand)


def _ssd_dt_bwd(proj, act, dt, dxdt, dyy, dacum, dt_bias, a_neg, d_exp, expand, expand_t, name):
    t = proj.shape[0]
    tm = min(SSD_ROWS, t)

    def body(raw_ref, xs_ref, dt_ref, dxdt_ref, dyy_ref, dac_ref, bias_ref, a_ref, dsk_ref, e_ref, et_ref,
             dxs_ref, draw_ref, da_ref, dbias_ref, dd_ref):
        @pl.when(pl.program_id(0) == 0)
        def _():
            da_ref[...] = jnp.zeros_like(da_ref)
            dbias_ref[...] = jnp.zeros_like(dbias_ref)
            dd_ref[...] = jnp.zeros_like(dd_ref)

        lane = _iota((tm, LANES), 1)
        xs, dt, dxdt, dyy = xs_ref[...], dt_ref[...], dxdt_ref[...], dyy_ref[...]
        dxs_ref[...] = dxdt * _mm_exact_rhs(dt, e_ref[...]) + dsk_ref[...] * dyy
        dd_ref[...] += jnp.sum(dyy * xs, axis=0, keepdims=True)
        ddt = _mm_exact_rhs(dxdt * xs, et_ref[...])
        da = _mm_exact_lhs(_chunk_tri(tm, False), dac_ref[...])
        ddt = ddt + da * a_ref[...]
        da_ref[...] += jnp.sum(da * dt, axis=0, keepdims=True)
        draw = jnp.where(lane < SSD_HEADS, ddt * _sigmoid(raw_ref[...] + bias_ref[...]), 0.0)
        dbias_ref[...] += jnp.sum(draw, axis=0, keepdims=True)
        draw_ref[...] = draw.astype(draw_ref.dtype)

    wide = pl.BlockSpec((tm, SSD_D_INNER), lambda i: (i, 0))
    thin = pl.BlockSpec((tm, LANES), lambda i: (i, 0))
    vec = pl.BlockSpec((1, LANES), lambda i: (0, 0))
    wvec = pl.BlockSpec((1, SSD_D_INNER), lambda i: (0, 0))
    return pl.pallas_call(
        body, name=name, grid=(t // tm,),
        in_specs=[pl.BlockSpec((tm, LANES), lambda i: (i, DT_COL)), wide, thin, wide, wide, thin, vec, vec, wvec,
                  pl.BlockSpec((LANES, SSD_D_INNER), lambda i: (0, 0)), pl.BlockSpec((SSD_D_INNER, LANES), lambda i: (0, 0))],
        out_specs=[wide, thin, vec, vec, wvec],
        out_shape=[jax.ShapeDtypeStruct((t, SSD_D_INNER), F32), jax.ShapeDtypeStruct((t, LANES), MXU_DTYPE),
                   jax.ShapeDtypeStruct((1, LANES), F32), jax.ShapeDtypeStruct((1, LANES), F32),
                   jax.ShapeDtypeStruct((1, SSD_D_INNER), F32)],
        compiler_params=_cp("arbitrary"),
    )(proj, act, dt, dxdt, dyy, dacum, dt_bias, a_neg, d_exp, expand, expand_t)


SSD_PAIR = 2 * SSD_HEAD_DIM
HEADS_PER_GROUP = SSD_HEADS // SSD_GROUPS
GROUP_COLS = HEADS_PER_GROUP * SSD_HEAD_DIM
B_COL0 = SSD_D_INNER // SSD_STATE
C_COL0 = (SSD_D_INNER + SSD_GROUPS * SSD_STATE) // SSD_STATE


def _pair_cols(vals, h0, lo_mask):
    return jnp.where(lo_mask, vals[:, h0:h0 + 1], vals[:, h0 + 1:h0 + 2])


def _ssd_scan_fwd(xdt, act, acum_g, acum_gt, name):
    t = xdt.shape[0]
    nc = t // SSD_CHUNK
    ln = SSD_CHUNK

    def body(x_ref, b_ref, c_ref, ac_ref, act_ref, y_ref, sst_ref, state):
        @pl.when(pl.program_id(1) == 0)
        def _():
            state[...] = jnp.zeros_like(state)

        bm = b_ref[...].astype(MXU_DTYPE)
        cm = c_ref[...].astype(MXU_DTYPE)
        cb = _mm_nt(cm, bm)
        ac, act_ = ac_ref[0], act_ref[0]
        causal = _iota((ln, ln), 1) <= _iota((ln, ln), 0)
        lo_mask = _iota((ln, SSD_PAIR), 1) < SSD_HEAD_DIM
        lo_rows = _iota((SSD_PAIR, SSD_STATE), 0) < SSD_HEAD_DIM
        sst_ref[0, 0] = state[...]
        last = ac[ln - 1:ln, :]
        e_ac = jnp.exp(ac)
        w_all = jnp.exp(last - ac)
        e_last = jnp.exp(last)
        for pr in range(2):
            cols = slice(pr * SSD_PAIR, (pr + 1) * SSD_PAIR)
            xp = x_ref[:, cols]
            sp = state[cols, :]
            ydiag = jnp.zeros((ln, SSD_PAIR), F32)
            for hh in range(2):
                h = 2 * pr + hh
                seg = ac[:, h:h + 1] - act_[h:h + 1, :]
                dec = jnp.exp(jnp.where(causal, seg, -1e30))
                mask = lo_mask if hh == 0 else jnp.logical_not(lo_mask)
                ydiag = ydiag + _mm((cb * dec).astype(MXU_DTYPE), jnp.where(mask, xp, 0.0).astype(MXU_DTYPE))
            yoff = _mm_nt(cm, sp.astype(MXU_DTYPE)) * _pair_cols(e_ac, 2 * pr, lo_mask)
            y_ref[:, cols] = ydiag + yoff
            xw = (xp * _pair_cols(w_all, 2 * pr, lo_mask)).astype(MXU_DTYPE)
            el = jnp.where(lo_rows, e_last[:, 2 * pr:2 * pr + 1], e_last[:, 2 * pr + 1:2 * pr + 2])
            state[cols, :] = sp * el + _mm_tn(xw, bm)

    return pl.pallas_call(
        body, name=name, grid=(SSD_GROUPS, nc),
        in_specs=[pl.BlockSpec((ln, GROUP_COLS), lambda g, c: (c, g)),
                  pl.BlockSpec((ln, SSD_STATE), lambda g, c: (c, B_COL0 + g)),
                  pl.BlockSpec((ln, SSD_STATE), lambda g, c: (c, C_COL0 + g)),
                  pl.BlockSpec((1, ln, HEADS_PER_GROUP), lambda g, c: (g, c, 0)),
                  pl.BlockSpec((1, HEADS_PER_GROUP, ln), lambda g, c: (g, 0, c))],
        out_specs=[pl.BlockSpec((ln, GROUP_COLS), lambda g, c: (c, g)),
                   pl.BlockSpec((1, 1, GROUP_COLS, SSD_STATE), lambda g, c: (c, g, 0, 0))],
        out_shape=[jax.ShapeDtypeStruct((t, SSD_D_INNER), F32),
                   jax.ShapeDtypeStruct((nc, SSD_GROUPS, GROUP_COLS, SSD_STATE), F32)],
        scratch_shapes=[pltpu.VMEM((GROUP_COLS, SSD_STATE), F32)],
        compiler_params=_cp("parallel", "arbitrary"),
    )(xdt, act, act, acum_g, acum_gt)


def _ssd_scan_bwd(xdt, act, acum_g, acum_gt, states, dy, name):
    t = xdt.shape[0]
    nc = t // SSD_CHUNK
    ln = SSD_CHUNK

    def body(x_ref, b_ref, c_ref, ac_ref, act_ref, sst_ref, dy_ref, dx_ref, db_ref, dc_ref, dacol_ref, darow_ref, dstate):
        @pl.when(pl.program_id(1) == 0)
        def _():
            dstate[...] = jnp.zeros_like(dstate)

        bm = b_ref[...].astype(MXU_DTYPE)
        cm = c_ref[...].astype(MXU_DTYPE)
        cb = _mm_nt(cm, bm)
        ac, act_ = ac_ref[0], act_ref[0]
        causal = _iota((ln, ln), 1) <= _iota((ln, ln), 0)
        lo_mask = _iota((ln, SSD_PAIR), 1) < SSD_HEAD_DIM
        lo_rows = _iota((SSD_PAIR, SSD_STATE), 0) < SSD_HEAD_DIM
        lane4 = _iota((ln, HEADS_PER_GROUP), 1)
        sub4 = _iota((HEADS_PER_GROUP, ln), 0)
        is_last = _iota((ln, 1), 0) == ln - 1
        last = ac[ln - 1:ln, :]
        e_ac = jnp.exp(ac)
        w_all = jnp.exp(last - ac)
        e_last = jnp.exp(last)
        dcb = jnp.zeros((ln, ln), F32)
        dc_acc = jnp.zeros((ln, SSD_STATE), F32)
        db_acc = jnp.zeros((ln, SSD_STATE), F32)
        dacol = jnp.zeros((ln, HEADS_PER_GROUP), F32)
        darow = jnp.zeros((HEADS_PER_GROUP, ln), F32)
        for pr in range(2):
            cols = slice(pr * SSD_PAIR, (pr + 1) * SSD_PAIR)
            xp = x_ref[:, cols]
            dyp = dy_ref[:, cols]
            sp = sst_ref[0, 0, cols, :]
            dsp = dstate[cols, :]
            ea = _pair_cols(e_ac, 2 * pr, lo_mask)
            w = _pair_cols(w_all, 2 * pr, lo_mask)
            dye = (dyp * ea).astype(MXU_DTYPE)
            dx_state = w * _mm_nt(bm, dsp.astype(MXU_DTYPE))
            yoff = _mm_nt(cm, sp.astype(MXU_DTYPE)) * ea
            dxp = dx_state
            for hh in range(2):
                h = 2 * pr + hh
                mask = lo_mask if hh == 0 else jnp.logical_not(lo_mask)
                rmask = lo_rows if hh == 0 else jnp.logical_not(lo_rows)
                seg = ac[:, h:h + 1] - act_[h:h + 1, :]
                dec = jnp.exp(jnp.where(causal, seg, -1e30))
                m = cb * dec
                dym = jnp.where(mask, dyp, 0.0).astype(MXU_DTYPE)
                xm = jnp.where(mask, xp, 0.0).astype(MXU_DTYPE)
                g = _mm_nt(dym, xm)
                dxp = dxp + _mm_tn(m.astype(MXU_DTYPE), dym)
                dcb = dcb + dec * g
                mg = m * g
                rs = jnp.sum(mg, axis=1, keepdims=True)
                cs = jnp.sum(mg, axis=0, keepdims=True)
                t_off = jnp.sum(jnp.where(mask, dyp * yoff, 0.0), axis=1, keepdims=True)
                q = jnp.sum(jnp.where(mask, xp * dx_state, 0.0), axis=1, keepdims=True)
                qsum = jnp.sum(q, axis=0, keepdims=True)
                ds_s = jnp.sum(jnp.sum(jnp.where(rmask, dsp * sp, 0.0), axis=1, keepdims=True), axis=0, keepdims=True)
                extra = qsum + e_last[:, h:h + 1] * ds_s
                col = rs + t_off - q + jnp.where(is_last, extra, 0.0)
                dacol = jnp.where(lane4 == h, col, dacol)
                darow = jnp.where(sub4 == h, -cs, darow)
            dx_ref[:, cols] = dxp
            dc_acc = dc_acc + _mm(dye, sp.astype(MXU_DTYPE))
            db_acc = db_acc + _mm((xp * w).astype(MXU_DTYPE), dsp.astype(MXU_DTYPE))
            el = jnp.where(lo_rows, e_last[:, 2 * pr:2 * pr + 1], e_last[:, 2 * pr + 1:2 * pr + 2])
            dstate[cols, :] = dsp * el + _mm_tn(dye, cm)
        dcbm = dcb.astype(MXU_DTYPE)
        dc_ref[...] = _mm(dcbm, bm) + dc_acc
        db_ref[...] = _mm_tn(dcbm, cm) + db_acc
        dacol_ref[0] = dacol
        darow_ref[0] = darow

    def rev(c):
        return nc - 1 - c

    grp = pl.BlockSpec((ln, GROUP_COLS), lambda g, c: (rev(c), g))
    return pl.pallas_call(
        body, name=name, grid=(SSD_GROUPS, nc),
        in_specs=[grp,
                  pl.BlockSpec((ln, SSD_STATE), lambda g, c: (rev(c), B_COL0 + g)),
                  pl.BlockSpec((ln, SSD_STATE), lambda g, c: (rev(c), C_COL0 + g)),
                  pl.BlockSpec((1, ln, HEADS_PER_GROUP), lambda g, c: (g, rev(c), 0)),
                  pl.BlockSpec((1, HEADS_PER_GROUP, ln), lambda g, c: (g, 0, rev(c))),
                  pl.BlockSpec((1, 1, GROUP_COLS, SSD_STATE), lambda g, c: (rev(c), g, 0, 0)),
                  grp],
        out_specs=[grp,
                   pl.BlockSpec((ln, SSD_STATE), lambda g, c: (rev(c), g)),
                   pl.BlockSpec((ln, SSD_STATE), lambda g, c: (rev(c), g)),
                   pl.BlockSpec((1, ln, HEADS_PER_GROUP), lambda g, c: (g, rev(c), 0)),
                   pl.BlockSpec((1, HEADS_PER_GROUP, ln), lambda g, c: (g, 0, rev(c)))],
        out_shape=[jax.ShapeDtypeStruct((t, SSD_D_INNER), F32),
                   jax.ShapeDtypeStruct((t, SSD_GROUPS * SSD_STATE), F32),
                   jax.ShapeDtypeStruct((t, SSD_GROUPS * SSD_STATE), F32),
                   jax.ShapeDtypeStruct((SSD_GROUPS, t, HEADS_PER_GROUP), F32),
                   jax.ShapeDtypeStruct((SSD_GROUPS, HEADS_PER_GROUP, t), F32)],
        scratch_shapes=[pltpu.VMEM((GROUP_COLS, SSD_STATE), F32)],
        compiler_params=_cp("parallel", "arbitrary"),
    )(xdt, act, act, acum_g, acum_gt, states, dy)


GN_ROWS = 128


def _gated_norm_parts(y_ref, xs_ref, z_ref, dsk_ref):
    yy = y_ref[...] + dsk_ref[...] * xs_ref[...]
    z = z_ref[...]
    sz = _sigmoid(z)
    silu = z * sz
    u = yy * silu
    r = lax.rsqrt(jnp.mean(u * u, axis=-1, keepdims=True) + NORM_EPS)
    return yy, z, sz, silu, u, r


def _gated_norm_fwd(y, act, proj, d_exp, g, name):
    t = y.shape[0]
    tm = min(GN_ROWS, t)

    def body(y_ref, xs_ref, z_ref, dsk_ref, g_ref, o_ref):
        _, _, _, _, u, r = _gated_norm_parts(y_ref, xs_ref, z_ref, dsk_ref)
        o_ref[...] = (u * r * g_ref[...]).astype(o_ref.dtype)

    wide = pl.BlockSpec((tm, SSD_D_INNER), lambda i: (i, 0))
    wvec = pl.BlockSpec((1, SSD_D_INNER), lambda i: (0, 0))
    return pl.pallas_call(
        body, name=name, grid=(t // tm,), in_specs=[wide, wide, wide, wvec, wvec], out_specs=wide,
        out_shape=jax.ShapeDtypeStruct((t, SSD_D_INNER), MXU_DTYPE), compiler_params=_cp("parallel"),
    )(y, act, proj, d_exp, g.reshape(1, -1))


def _gated_norm_bwd(y, act, proj, d_exp, g, dn, name):
    t = y.shape[0]
    tm = min(GN_ROWS, t)

    def body(y_ref, xs_ref, z_ref, dsk_ref, g_ref, dn_ref, dyy_ref, dz_ref, dg_ref):
        @pl.when(pl.program_id(0) == 0)
        def _():
            dg_ref[...] = jnp.zeros_like(dg_ref)

        yy, z, sz, silu, u, r = _gated_norm_parts(y_ref, xs_ref, z_ref, dsk_ref)
        un = u * r
        dn = dn_ref[...]
        v = dn * g_ref[...]
        du = r * (v - un * jnp.mean(v * un, axis=-1, keepdims=True))
        dg_ref[...] += jnp.sum(dn * un, axis=0, keepdims=True)
        dyy_ref[...] = du * silu
        dz_ref[...] = (du * yy * (sz * (1.0 + z * (1.0 - sz)))).astype(dz_ref.dtype)

    wide = pl.BlockSpec((tm, SSD_D_INNER), lambda i: (i, 0))
    wvec = pl.BlockSpec((1, SSD_D_INNER), lambda i: (0, 0))
    return pl.pallas_call(
        body, name=name, grid=(t // tm,), in_specs=[wide, wide, wide, wvec, wvec, wide], out_specs=[wide, wide, wvec],
        out_shape=[jax.ShapeDtypeStruct((t, SSD_D_INNER), F32), jax.ShapeDtypeStruct((t, SSD_D_INNER), MXU_DTYPE),
                   jax.ShapeDtypeStruct((1, SSD_D_INNER), F32)],
        compiler_params=_cp("arbitrary"),
    )(y, act, proj, d_exp, g.reshape(1, -1), dn)


SB_PAIRS = SB_HEADS // 2


def _log_sigmoids(x):
    sp = jnp.log(1.0 + jnp.exp(-jnp.abs(x)))
    return jnp.minimum(x, 0.0) - sp, jnp.minimum(-x, 0.0) - sp


def _kv_rows(j):
    return pl.ds(pl.multiple_of(j * SB_BLOCK, SB_BLOCK), SB_BLOCK)


def _mask_lanes(x, mask):
    return jnp.where(mask, x.astype(F32), 0.0).astype(x.dtype)


def _sb_attention_fwd(qkv, name):
    t = qkv.shape[0]
    nq = t // SB_BLOCK
    bl = SB_BLOCK

    def body(q_ref, k_ref, v_ref, o_ref):
        i = pl.program_id(1)
        qv = q_ref[...]
        lane = _iota((bl, bl), 1)
        rowi = _iota((bl, bl), 0)
        strict = lane < rowi
        later = (rowi > lane).astype(MXU_DTYPE)
        out = jnp.zeros((bl, bl), F32)
        for hh in range(2):
            mask = (lane < SB_HEAD_DIM) if hh == 0 else (lane >= SB_HEAD_DIM)
            qm = _mask_lanes(qv, mask)

            def block(j, carry, acc, diag):
                kb = k_ref[_kv_rows(j), :]
                vb = v_ref[_kv_rows(j), :]
                x = _mm_nt(qm, kb) * SB_SCALE
                lb, lf = _log_sigmoids(x)
                if diag:
                    lf = jnp.where(strict, lf, 0.0)
                w = jnp.exp(lb + _mm_exact_rhs(lf, later) + carry)
                if diag:
                    w = jnp.where(strict, w, 0.0)
                return carry + jnp.sum(lf, axis=1, keepdims=True), acc + _mm(w.astype(MXU_DTYPE), vb)

            carry, acc = block(i, jnp.zeros((bl, 1), F32), jnp.zeros((bl, bl), F32), True)

            def step(it, st):
                return block(i - 1 - it, st[0], st[1], False)

            carry, acc = lax.fori_loop(0, i, step, (carry, acc))
            out = jnp.where(mask, acc, out)
        o_ref[...] = out.astype(o_ref.dtype)

    return pl.pallas_call(
        body, name=name, grid=(SB_PAIRS, nq),
        in_specs=[pl.BlockSpec((bl, bl), lambda p, i: (i, p)),
                  pl.BlockSpec((t, bl), lambda p, i: (0, SB_PAIRS + p)),
                  pl.BlockSpec((t, bl), lambda p, i: (0, 2 * SB_PAIRS + p))],
        out_specs=pl.BlockSpec((bl, bl), lambda p, i: (i, p)),
        out_shape=jax.ShapeDtypeStruct((t, D_MODEL), MXU_DTYPE),
        compiler_params=_cp("parallel", "parallel"),
    )(qkv, qkv, qkv)


def _sb_attention_bwd(qkv, do, name):
    t = qkv.shape[0]
    nq = t // SB_BLOCK
    bl = SB_BLOCK

    def body(q_ref, k_ref, v_ref, do_ref, dq_ref, dk_ref, dv_ref, xbuf, ebuf, dk_acc, dv_acc):
        i = pl.program_id(1)

        @pl.when(i == 0)
        def _():
            dk_acc[...] = jnp.zeros_like(dk_acc)
            dv_acc[...] = jnp.zeros_like(dv_acc)

        qv = q_ref[...]
        dov = do_ref[...]
        lane = _iota((bl, bl), 1)
        rowi = _iota((bl, bl), 0)
        strict = lane < rowi
        later = (rowi > lane).astype(MXU_DTYPE)
        earlier = (rowi < lane).astype(MXU_DTYPE)
        dq_out = jnp.zeros((bl, bl), F32)
        for hh in range(2):
            mask = (lane < SB_HEAD_DIM) if hh == 0 else (lane >= SB_HEAD_DIM)
            qm = _mask_lanes(qv, mask)
            dom = _mask_lanes(dov, mask)

            def pass1(j, carry, diag):
                kb = k_ref[_kv_rows(j), :]
                vb = v_ref[_kv_rows(j), :]
                x = _mm_nt(qm, kb) * SB_SCALE
                lb, lf = _log_sigmoids(x)
                if diag:
                    lf = jnp.where(strict, lf, 0.0)
                w = jnp.exp(lb + _mm_exact_rhs(lf, later) + carry)
                if diag:
                    w = jnp.where(strict, w, 0.0)
                xbuf[:, _kv_rows(j)] = x
                ebuf[:, _kv_rows(j)] = w * _mm_nt(dom, vb)
                dv_acc[_kv_rows(j), :] += _mm_tn(w.astype(MXU_DTYPE), dom)
                return carry + jnp.sum(lf, axis=1, keepdims=True)

            carry = pass1(i, jnp.zeros((bl, 1), F32), True)
            lax.fori_loop(0, i, lambda it, cr: pass1(i - 1 - it, cr, False), carry)

            def pass2(j, pre, dq_acc, diag):
                kb = k_ref[_kv_rows(j), :]
                x = xbuf[:, _kv_rows(j)]
                e = ebuf[:, _kv_rows(j)]
                sg = _sigmoid(x)
                dx = e * (1.0 - sg) - sg * (_mm_exact_rhs(e, earlier) + pre)
                if diag:
                    dx = jnp.where(strict, dx, 0.0)
                dxm = (dx * SB_SCALE).astype(MXU_DTYPE)
                dq_acc = dq_acc + _mm(dxm, kb)
                dk_acc[_kv_rows(j), :] += _mm_tn(dxm, qm)
                return pre + jnp.sum(e, axis=1, keepdims=True), dq_acc

            pre, dq_acc = lax.fori_loop(0, i, lambda j, st: pass2(j, st[0], st[1], False),
                                        (jnp.zeros((bl, 1), F32), jnp.zeros((bl, bl), F32)))
            _, dq_acc = pass2(i, pre, dq_acc, True)
            dq_out = jnp.where(mask, dq_acc, dq_out)
        dq_ref[...] = dq_out.astype(dq_ref.dtype)

        @pl.when(i == nq - 1)
        def _():
            dk_ref[...] = dk_acc[...].astype(dk_ref.dtype)
            dv_ref[...] = dv_acc[...].astype(dv_ref.dtype)

    blk = pl.BlockSpec((bl, bl), lambda p, i: (i, p))
    whole = pl.BlockSpec((t, bl), lambda p, i: (0, p))
    out = jax.ShapeDtypeStruct((t, D_MODEL), MXU_DTYPE)
    return pl.pallas_call(
        body, name=name, grid=(SB_PAIRS, nq),
        in_specs=[blk, pl.BlockSpec((t, bl), lambda p, i: (0, SB_PAIRS + p)),
                  pl.BlockSpec((t, bl), lambda p, i: (0, 2 * SB_PAIRS + p)), blk],
        out_specs=[blk, whole, whole], out_shape=[out, out, out],
        scratch_shapes=[pltpu.VMEM((bl, t), F32), pltpu.VMEM((bl, t), F32), pltpu.VMEM((t, bl), F32), pltpu.VMEM((t, bl), F32)],
        compiler_params=_cp("parallel", "arbitrary"),
    )(qkv, qkv, qkv, do)


def _add_pair(a, b, name):
    s, r, c = a.shape
    tm = 256 if r % 256 == 0 else 8
    assert r % tm == 0

    def body(a_ref, b_ref, o_ref):
        o_ref[...] = (a_ref[...].astype(F32) + b_ref[...].astype(F32)).astype(o_ref.dtype)

    blk = pl.BlockSpec((1, tm, c), lambda q, i: (q, i, 0))
    return pl.pallas_call(body, name=name, grid=(s, r // tm), in_specs=[blk, blk], out_specs=blk,
                          out_shape=jax.ShapeDtypeStruct(a.shape, a.dtype), compiler_params=_cp("parallel", "parallel"))(a, b)


def _adamw(gslots, w, m, v, name):
    s, r, c = gslots.shape
    tm = 256 if r % 256 == 0 else 8
    assert r % tm == 0 and w.shape == (r, c)
    c1 = 1.0 - ADAM_B1 ** ADAM_STEP
    c2 = 1.0 - ADAM_B2 ** ADAM_STEP

    def body(g_ref, w_ref, m_ref, v_ref, go_ref, d_ref, mo_ref, vo_ref):
        g = g_ref[0].astype(F32)
        for q in range(1, s):
            g = g + g_ref[q].astype(F32)
        mn = ADAM_B1 * m_ref[...] + (1.0 - ADAM_B1) * g
        vn = ADAM_B2 * v_ref[...] + (1.0 - ADAM_B2) * (g * g)
        go_ref[...] = g
        mo_ref[...] = mn
        vo_ref[...] = vn
        d_ref[...] = -ADAM_LR * ((mn / c1) / (jnp.sqrt(vn / c2) + ADAM_EPS) + ADAM_WD * w_ref[...])

    row = pl.BlockSpec((tm, c), lambda i: (i, 0))
    out = jax.ShapeDtypeStruct((r, c), F32)
    return pl.pallas_call(
        body, name=name, grid=(r // tm,),
        in_specs=[pl.BlockSpec((s, tm, c), lambda i: (0, i, 0)), row, row, row],
        out_specs=[row, row, row, row], out_shape=[out, out, out, out], compiler_params=_cp("parallel"),
    )(gslots, w, m, v)


def _rows(a):
    flat = a.reshape(-1)
    pad = (-flat.shape[0]) % PACK_W
    if pad:
        flat = jnp.concatenate([flat, jnp.zeros((pad,), flat.dtype)])
    return flat.reshape(-1, PACK_W)


def _pack(arrays, row_multiple):
    parts, layout, off = [], [], 0
    for a in arrays:
        rw = _rows(a)
        parts.append(rw)
        layout.append((off, rw.shape[0], a.shape))
        off += rw.shape[0]
    pad = (-off) % row_multiple
    if pad:
        parts.append(jnp.zeros((pad, PACK_W), parts[0].dtype))
    return jnp.concatenate(parts, axis=0), layout


def _unpack(packed, layout):
    out = []
    for off, nrows, shape in layout:
        n = int(np.prod(shape))
        out.append(packed[off:off + nrows].reshape(-1)[:n].reshape(shape))
    return out


def _unshard_cols(g, lead):
    k = D_MODEL
    ns = g.shape[1] * PACK_W // (lead * k)
    return g.reshape(N_DEV, lead, k, ns).transpose(1, 2, 0, 3).reshape(lead, k, N_DEV * ns)


def _shard_cols(full):
    lead, k, n = full.shape
    return full.reshape(lead, k, N_DEV, n // N_DEV).transpose(2, 0, 1, 3)


def _unshard_rows(g, lead):
    ks = g.shape[1] // lead
    return g.reshape(N_DEV, lead, ks, PACK_W).transpose(1, 0, 2, 3).reshape(lead, N_DEV * ks, PACK_W)


def _shard_rows(full):
    lead, k, n = full.shape
    return full.reshape(lead, N_DEV, k // N_DEV, n).transpose(1, 0, 2, 3)


def _ssd_consts(dt_bias, a_log, d_skip):
    pad = LANES - SSD_HEADS
    bias = jnp.pad(dt_bias, (0, pad)).reshape(1, LANES)
    a_neg = jnp.pad(-jnp.exp(a_log), (0, pad)).reshape(1, LANES)
    d_exp = jnp.repeat(d_skip, SSD_HEAD_DIM).reshape(1, SSD_D_INNER)
    return bias, a_neg, d_exp


def _group_layouts(acum):
    t = acum.shape[0]
    a = acum[:, :SSD_HEADS].reshape(t, SSD_GROUPS, HEADS_PER_GROUP)
    return a.transpose(1, 0, 2), a.transpose(1, 2, 0)


def _ssd_fwd(x, p):
    hn = _rmsnorm(x, p["mix_norm"], "rmsnorm_fwd")
    proj = _matmul(hn, p["w_in"], "nn", F32, "ssd_in_fwd", tm=512, tn=896, tk=1024)
    act = _ssd_conv_fwd(proj, p["conv_w"], p["conv_b"], "ssd_conv_fwd")
    bias, a_neg, d_exp = _ssd_consts(p["dt_bias"], p["a_log"], p["d"])
    expand = _head_expand()
    xdt, dt, acum = _ssd_dt_fwd(proj, act, bias, a_neg, expand, "ssd_dt_fwd")
    acum_g, acum_gt = _group_layouts(acum)
    y, states = _ssd_scan_fwd(xdt, act, acum_g, acum_gt, "ssd_scan_fwd")
    yn = _gated_norm_fwd(y, act, proj, d_exp, p["norm"], "ssd_gnorm_fwd")
    x_new = _matmul(yn, p["w_out"], "nn", F32, "ssd_out_fwd", add=x, tm=512, tn=1024, tk=2048)
    saved = dict(x=x, hn=hn, proj=proj, act=act, xdt=xdt, dt=dt, acum_g=acum_g, acum_gt=acum_gt, y=y, states=states, yn=yn)
    return x_new, saved


def _ssd_bwd(dx, p, s):
    bias, a_neg, d_exp = _ssd_consts(p["dt_bias"], p["a_log"], p["d"])
    expand = _head_expand()
    dxb = dx.astype(MXU_DTYPE)
    dyn = _matmul(dxb, p["w_out"], "nt", F32, "ssd_out_dgrad", tm=512, tn=1024, tk=1024)
    g_w_out = _matmul(s["yn"], dxb, "tn", F32, "ssd_out_wgrad", tm=1024, tn=1024, tk=512)
    dyy, dz, g_norm = _gated_norm_bwd(s["y"], s["act"], s["proj"], d_exp, p["norm"], dyn, "ssd_gnorm_bwd")
    dxdt, dbm, dcm, dacol, darow = _ssd_scan_bwd(s["xdt"], s["act"], s["acum_g"], s["acum_gt"], s["states"], dyy, "ssd_scan_bwd")
    t = dx.shape[0]
    dacum = dacol.transpose(1, 0, 2).reshape(t, SSD_HEADS) + darow.transpose(2, 0, 1).reshape(t, SSD_HEADS)
    dacum = jnp.pad(dacum, ((0, 0), (0, LANES - SSD_HEADS)))
    dxs, draw, g_a, g_bias, g_dexp = _ssd_dt_bwd(s["proj"], s["act"], s["dt"], dxdt, dyy, dacum, bias, a_neg, d_exp,
                                                  expand, expand.T, "ssd_dt_bwd")
    dact = jnp.concatenate([dxs, dbm, dcm], axis=1)
    dxbc, g_conv_w, g_conv_b = _ssd_conv_bwd(s["proj"], p["conv_w"], p["conv_b"], dact, "ssd_conv_bwd")
    dproj = jnp.concatenate([dz, dxbc, draw], axis=1)
    dhn = _matmul(dproj, p["w_in"], "nt", F32, "ssd_in_dgrad", tm=512, tn=1024, tk=896)
    g_w_in = _matmul(s["hn"], dproj, "tn", F32, "ssd_in_wgrad", tm=1024, tn=896, tk=512)
    dx_new, g_mix = _rmsnorm_bwd(s["x"], p["mix_norm"], dhn, dx, "rmsnorm_bwd")
    grads = dict(w_in=g_w_in[:, :SSD_IN_DIM], w_out=g_w_out, conv_w=g_conv_w, conv_b=g_conv_b.reshape(-1),
                 dt_bias=g_bias[0, :SSD_HEADS], a_log=(g_a * a_neg)[0, :SSD_HEADS],
                 d=g_dexp.reshape(SSD_HEADS, SSD_HEAD_DIM).sum(axis=1), norm=g_norm.reshape(-1), mix_norm=g_mix.reshape(-1))
    return dx_new, grads


def _sb_fwd(x, p):
    hn = _rmsnorm(x, p["mix_norm"], "rmsnorm_fwd")
    qkv = _matmul(hn, p["w_qkv"], "nn", MXU_DTYPE, "sb_qkv_fwd", tm=512, tn=1024, tk=1024)
    o = _sb_attention_fwd(qkv, "sb_attn_fwd")
    x_new = _matmul(o, p["w_out"], "nn", F32, "sb_out_fwd", add=x, tm=512, tn=1024, tk=1024)
    return x_new, dict(x=x, hn=hn, qkv=qkv, o=o)


def _sb_bwd(dx, p, s):
    dxb = dx.astype(MXU_DTYPE)
    do = _matmul(dxb, p["w_out"], "nt", MXU_DTYPE, "sb_out_dgrad", tm=512, tn=1024, tk=1024)
    g_w_out = _matmul(s["o"], dxb, "tn", F32, "sb_out_wgrad", tm=1024, tn=1024, tk=512)
    dq, dk, dv = _sb_attention_bwd(s["qkv"], do, "sb_attn_bwd")
    dqkv = jnp.concatenate([dq, dk, dv], axis=1)
    dhn = _matmul(dqkv, p["w_qkv"], "nt", F32, "sb_qkv_dgrad", tm=512, tn=1024, tk=1024)
    g_w_qkv = _matmul(s["hn"], dqkv, "tn", F32, "sb_qkv_wgrad", tm=1024, tn=1024, tk=512)
    dx_new, g_mix = _rmsnorm_bwd(s["x"], p["mix_norm"], dhn, dx, "rmsnorm_bwd")
    return dx_new, dict(w_qkv=g_w_qkv, w_out=g_w_out, mix_norm=g_mix.reshape(-1))


def _ffn_fwd(x, p):
    hn = _rmsnorm(x, p["ffn_norm"], "rmsnorm_fwd")
    proj = _matmul(hn, p["w_in"], "nn", F32, "ffn_in_fwd", tm=512, tn=1408, tk=1024)
    act = _ffn_conv_fwd(proj, p["conv_w"], p["conv_b"], "ffn_conv_fwd")
    x_new = _matmul(act, p["w_out"], "nn", F32, "ffn_out_fwd", add=x, tm=512, tn=1024, tk=1408)
    return x_new, dict(x=x, hn=hn, proj=proj, act=act)


def _ffn_bwd(dx, p, s):
    dxb = dx.astype(MXU_DTYPE)
    dact = _matmul(dxb, p["w_out"], "nt", F32, "ffn_out_dgrad", tm=512, tn=1408, tk=1024)
    g_w_out = _matmul(s["act"], dxb, "tn", F32, "ffn_out_wgrad", tm=1408, tn=1024, tk=512)
    dpg, dpu, dwg, dwu, dbg, dbu = _ffn_conv_bwd(s["proj"], p["conv_w"], p["conv_b"], dact, "ffn_conv_bwd")
    dproj = jnp.concatenate([dpg, dpu], axis=1)
    dhn = _matmul(dproj, p["w_in"], "nt", F32, "ffn_in_dgrad", tm=512, tn=1024, tk=1408)
    g_w_in = _matmul(s["hn"], dproj, "tn", F32, "ffn_in_wgrad", tm=1024, tn=1408, tk=512)
    dx_new, g_norm = _rmsnorm_bwd(s["x"], p["ffn_norm"], dhn, dx, "rmsnorm_bwd")
    grads = dict(w_in=g_w_in, w_out=g_w_out, conv_w=jnp.concatenate([dwg, dwu], axis=1),
                 conv_b=jnp.concatenate([dbg, dbu], axis=1).reshape(-1), ffn_norm=g_norm.reshape(-1))
    return dx_new, grads


BIG_ROW_MULTIPLE = 256
BIG = ["ssd_w_in", "sb_w_qkv", "ffn_w_in", "ssd_w_out", "sb_w_out", "ffn_w_out"]
COL_SHARDED = {"ssd_w_in": 2, "sb_w_qkv": 2, "ffn_w_in": 4}
ROW_SHARDED = {"ssd_w_out": 2, "sb_w_out": 2, "ffn_w_out": 4}
CONV = ["ssd_conv_w", "ffn_conv_w"]
SMALL = ["mix_norm", "ffn_norm", "final_norm", "ssd_conv_b", "ssd_dt_bias", "ssd_a_log", "ssd_d", "ssd_norm", "ffn_conv_b"]
WEIGHTS = ["mix_norm", "ffn_norm", "final_norm", "ssd_w_in", "ssd_conv_w", "ssd_conv_b", "ssd_dt_bias", "ssd_a_log", "ssd_d",
           "ssd_norm", "ssd_w_out", "sb_w_qkv", "sb_w_out", "ffn_w_in", "ffn_conv_w", "ffn_conv_b", "ffn_w_out"]


def _step(x, loss_target, w, m, v):
    x = x.reshape(x.shape[-2], x.shape[-1])
    target = loss_target.reshape(x.shape)
    dev = 4 * lax.axis_index("x") + 2 * lax.axis_index("y") + lax.axis_index("c")
    core = lax.axis_index("c")

    big_pack, big_layout = _pack([w[n].astype(MXU_DTYPE) for n in BIG], BIG_ROW_MULTIPLE)
    big_all = _all_gather(big_pack, "gather_weights")
    full = {}
    for n, (off, nrows, _) in zip(BIG, big_layout):
        g = big_all[:, off:off + nrows]
        full[n] = _unshard_cols(g, COL_SHARDED[n]) if n in COL_SHARDED else _unshard_rows(g, ROW_SHARDED[n])
    full["ssd_w_in"] = jnp.pad(full["ssd_w_in"], ((0, 0), (0, 0), (0, SSD_IN_PAD - SSD_IN_DIM)))
    conv_pack, conv_layout = _pack([w[n] for n in CONV], 8)
    conv_all = _all_gather(conv_pack, "gather_conv_taps")
    for n, (off, nrows, shape) in zip(CONV, conv_layout):
        parts = [_unpack(conv_all[j], conv_layout)[CONV.index(n)] for j in range(N_DEV)]
        full[n] = jnp.concatenate(parts, axis=-1)

    def ssd_params(j):
        return dict(mix_norm=w["mix_norm"][2 * j], w_in=full["ssd_w_in"][j], conv_w=full["ssd_conv_w"][j],
                    conv_b=w["ssd_conv_b"][j], dt_bias=w["ssd_dt_bias"][j], a_log=w["ssd_a_log"][j], d=w["ssd_d"][j],
                    norm=w["ssd_norm"][j], w_out=full["ssd_w_out"][j])

    def sb_params(j):
        return dict(mix_norm=w["mix_norm"][2 * j + 1], w_qkv=full["sb_w_qkv"][j], w_out=full["sb_w_out"][j])

    def ffn_params(i):
        return dict(ffn_norm=w["ffn_norm"][i], w_in=full["ffn_w_in"][i], conv_w=full["ffn_conv_w"][i],
                    conv_b=w["ffn_conv_b"][i], w_out=full["ffn_w_out"][i])

    saved = []
    for i in range(DEPTH):
        if i % 2 == 0:
            x, s_mix = _ssd_fwd(x, ssd_params(i // 2))
        else:
            x, s_mix = _sb_fwd(x, sb_params(i // 2))
        x, s_ffn = _ffn_fwd(x, ffn_params(i))
        saved.append((s_mix, s_ffn))
    dx, g_final, loss_part = _final_norm_loss(x, w["final_norm"], target, "final_norm_loss")

    g_mix, g_ffn, g_ssd, g_sb = [None] * DEPTH, [None] * DEPTH, [None] * 2, [None] * 2
    for i in reversed(range(DEPTH)):
        s_mix, s_ffn = saved[i]
        dx, g_ffn[i] = _ffn_bwd(dx, ffn_params(i), s_ffn)
        if i % 2 == 0:
            dx, g_ssd[i // 2] = _ssd_bwd(dx, ssd_params(i // 2), s_mix)
            g_mix[i] = g_ssd[i // 2]["mix_norm"]
        else:
            dx, g_sb[i // 2] = _sb_bwd(dx, sb_params(i // 2), s_mix)
            g_mix[i] = g_sb[i // 2]["mix_norm"]
    grad_x = dx.reshape(1, *dx.shape)

    gfull = {
        "ssd_w_in": _shard_cols(jnp.stack([g["w_in"] for g in g_ssd])),
        "sb_w_qkv": _shard_cols(jnp.stack([g["w_qkv"] for g in g_sb])),
        "ffn_w_in": _shard_cols(jnp.stack([g["w_in"] for g in g_ffn])),
        "ssd_w_out": _shard_rows(jnp.stack([g["w_out"] for g in g_ssd])),
        "sb_w_out": _shard_rows(jnp.stack([g["w_out"] for g in g_sb])),
        "ffn_w_out": _shard_rows(jnp.stack([g["w_out"] for g in g_ffn])),
    }
    rtot = big_pack.shape[0]
    chunks = []
    for n, (off, nrows, _) in zip(BIG, big_layout):
        chunks.append(gfull[n].astype(MXU_DTYPE).reshape(N_DEV, nrows, PACK_W))
    pad_rows = rtot - sum(c_.shape[1] for c_ in chunks)
    if pad_rows:
        chunks.append(jnp.zeros((N_DEV, pad_rows, PACK_W), MXU_DTYPE))
    g8 = jnp.concatenate(chunks, axis=1).reshape(4, 2, rtot, PACK_W)
    keep = lax.dynamic_index_in_dim(g8, core, axis=1, keepdims=False)
    give = lax.dynamic_index_in_dim(g8, 1 - core, axis=1, keepdims=False)
    got = _swap_with_sibling(give, "grads_to_sibling")
    chip_part = _add_pair(keep, got, "grads_add_sibling")
    from_chips = _exchange_chips(chip_part, "grads_across_chips")
    w_pack, _ = _pack([w[n] for n in BIG], BIG_ROW_MULTIPLE)
    m_pack, _ = _pack([m[n] for n in BIG], BIG_ROW_MULTIPLE)
    v_pack, _ = _pack([v[n] for n in BIG], BIG_ROW_MULTIPLE)
    big_out = _adamw(from_chips, w_pack, m_pack, v_pack, "adamw_matmul_weights")
    big_res = [dict(zip(BIG, _unpack(o, big_layout))) for o in big_out]

    small_g = {
        "mix_norm": jnp.stack(g_mix), "ffn_norm": jnp.stack([g["ffn_norm"] for g in g_ffn]), "final_norm": g_final.reshape(-1),
        "ssd_conv_b": jnp.stack([g["conv_b"] for g in g_ssd]), "ssd_dt_bias": jnp.stack([g["dt_bias"] for g in g_ssd]),
        "ssd_a_log": jnp.stack([g["a_log"] for g in g_ssd]), "ssd_d": jnp.stack([g["d"] for g in g_ssd]),
        "ssd_norm": jnp.stack([g["norm"] for g in g_ssd]), "ffn_conv_b": jnp.stack([g["conv_b"] for g in g_ffn]),
    }
    conv_g = {"ssd_conv_w": jnp.stack([g["conv_w"] for g in g_ssd]), "ffn_conv_w": jnp.stack([g["conv_w"] for g in g_ffn])}
    extra = [conv_g[n] for n in CONV] + [loss_part]
    small_pack, small_layout = _pack([small_g[n] for n in SMALL] + extra, 8)
    small_all = _all_gather(small_pack, "gather_small_grads")
    zeros_like = [jnp.zeros(a.shape, F32) for a in extra]
    sw, _ = _pack([w[n] for n in SMALL] + zeros_like, 8)
    sm, _ = _pack([m[n] for n in SMALL] + zeros_like, 8)
    sv, _ = _pack([v[n] for n in SMALL] + [jnp.ones(a.shape, F32) for a in extra], 8)
    small_out = _adamw(small_all, sw, sm, sv, "adamw_replicated")
    small_res = [_unpack(o, small_layout) for o in small_out]
    summed = small_res[0]
    loss = summed[-1][0, 0]
    conv_shard_g = []
    for n, gsum in zip(CONV, summed[len(SMALL):len(SMALL) + len(CONV)]):
        ns = w[n].shape[-1]
        conv_shard_g.append(lax.dynamic_slice_in_dim(gsum, dev * ns, ns, axis=2))
    cg, conv_sh_layout = _pack(conv_shard_g, 8)
    cw, _ = _pack([w[n] for n in CONV], 8)
    cm_, _ = _pack([m[n] for n in CONV], 8)
    cv, _ = _pack([v[n] for n in CONV], 8)
    conv_out = _adamw(cg.reshape(1, *cg.shape), cw, cm_, cv, "adamw_conv_taps")
    conv_res = [dict(zip(CONV, _unpack(o, conv_sh_layout))) for o in conv_out]

    def pick(kind, n):
        if n in BIG:
            return big_res[kind][n]
        if n in CONV:
            return conv_res[kind][n]
        return small_res[kind][SMALL.index(n)]

    outs = [loss, grad_x]
    for kind in range(4):
        outs += [pick(kind, n) for n in WEIGHTS]
    return tuple(outs)


def kernel(x, mix_norm, ffn_norm, final_norm, ssd_w_in, ssd_conv_w, ssd_conv_b, ssd_dt_bias, ssd_a_log, ssd_d, ssd_norm, ssd_w_out, sb_w_qkv, sb_w_out, ffn_w_in, ffn_conv_w, ffn_conv_b, ffn_w_out, loss_target, m_mix_norm, m_ffn_norm, m_final_norm, m_ssd_w_in, m_ssd_conv_w, m_ssd_conv_b, m_ssd_dt_bias, m_ssd_a_log, m_ssd_d, m_ssd_norm, m_ssd_w_out, m_sb_w_qkv, m_sb_w_out, m_ffn_w_in, m_ffn_conv_w, m_ffn_conv_b, m_ffn_w_out, v_mix_norm, v_ffn_norm, v_final_norm, v_ssd_w_in, v_ssd_conv_w, v_ssd_conv_b, v_ssd_dt_bias, v_ssd_a_log, v_ssd_d, v_ssd_norm, v_ssd_w_out, v_sb_w_qkv, v_sb_w_out, v_ffn_w_in, v_ffn_conv_w, v_ffn_conv_b, v_ffn_w_out):
    w = dict(mix_norm=mix_norm, ffn_norm=ffn_norm, final_norm=final_norm, ssd_w_in=ssd_w_in, ssd_conv_w=ssd_conv_w,
             ssd_conv_b=ssd_conv_b, ssd_dt_bias=ssd_dt_bias, ssd_a_log=ssd_a_log, ssd_d=ssd_d, ssd_norm=ssd_norm,
             ssd_w_out=ssd_w_out, sb_w_qkv=sb_w_qkv, sb_w_out=sb_w_out, ffn_w_in=ffn_w_in, ffn_conv_w=ffn_conv_w,
             ffn_conv_b=ffn_conv_b, ffn_w_out=ffn_w_out)
    m = dict(mix_norm=m_mix_norm, ffn_norm=m_ffn_norm, final_norm=m_final_norm, ssd_w_in=m_ssd_w_in, ssd_conv_w=m_ssd_conv_w,
             ssd_conv_b=m_ssd_conv_b, ssd_dt_bias=m_ssd_dt_bias, ssd_a_log=m_ssd_a_log, ssd_d=m_ssd_d, ssd_norm=m_ssd_norm,
             ssd_w_out=m_ssd_w_out, sb_w_qkv=m_sb_w_qkv, sb_w_out=m_sb_w_out, ffn_w_in=m_ffn_w_in, ffn_conv_w=m_ffn_conv_w,
             ffn_conv_b=m_ffn_conv_b, ffn_w_out=m_ffn_w_out)
    v = dict(mix_norm=v_mix_norm, ffn_norm=v_ffn_norm, final_norm=v_final_norm, ssd_w_in=v_ssd_w_in, ssd_conv_w=v_ssd_conv_w,
             ssd_conv_b=v_ssd_conv_b, ssd_dt_bias=v_ssd_dt_bias, ssd_a_log=v_ssd_a_log, ssd_d=v_ssd_d, ssd_norm=v_ssd_norm,
             ssd_w_out=v_ssd_w_out, sb_w_qkv=v_sb_w_qkv, sb_w_out=v_sb_w_out, ffn_w_in=v_ffn_w_in, ffn_conv_w=v_ffn_conv_w,
             ffn_conv_b=v_ffn_conv_b, ffn_w_out=v_ffn_w_out)
    return _step(x, loss_target, w, m, v)
```

```python
import functools

import jax
import jax.numpy as jnp
import numpy as np
from jax import lax
from jax.experimental import pallas as pl
from jax.experimental.pallas import tpu as pltpu

F32 = jnp.float32
MXU_DTYPE = jnp.bfloat16
MESH_ID = pl.DeviceIdType.MESH
N_DEV = 8

NORM_EPS = 1e-6
D_MODEL = 1024
DEPTH = 4
SSD_D_INNER = 2048
SSD_HEADS = 32
SSD_HEAD_DIM = 64
SSD_GROUPS = 8
SSD_STATE = 128
SSD_CONV = 4
SSD_CHUNK = 128
SSD_CONV_DIM = SSD_D_INNER + 2 * SSD_GROUPS * SSD_STATE
SSD_IN_DIM = SSD_D_INNER + SSD_CONV_DIM + SSD_HEADS
LANES = 128
SSD_IN_PAD = SSD_D_INNER + SSD_CONV_DIM + LANES
SB_HEADS = 16
SB_HEAD_DIM = 64
SB_TILE = 256
SB_SCALE = SB_HEAD_DIM ** -0.5
FFN_D_FF = 2816
FFN_CONV = 3
PACK_W = 1024

ADAM_LR = 0.001
ADAM_B1 = 0.9
ADAM_B2 = 0.999
ADAM_EPS = 1e-08
ADAM_WD = 0.01
ADAM_STEP = 10

VMEM_LIMIT_BYTES = 56 * 1024 * 1024


def _cp(*sem):
    return pltpu.CompilerParams(dimension_semantics=sem, vmem_limit_bytes=VMEM_LIMIT_BYTES)


def _iota(shape, dim):
    return lax.broadcasted_iota(jnp.int32, shape, dim)


def _sigmoid(x):
    return 1.0 / (1.0 + jnp.exp(-x))


def _mm(a, b):
    return lax.dot_general(a, b, (((1,), (0,)), ((), ())), preferred_element_type=F32)


def _mm_nt(a, b):
    return lax.dot_general(a, b, (((1,), (1,)), ((), ())), preferred_element_type=F32)


def _mm_tn(a, b):
    return lax.dot_general(a, b, (((0,), (0,)), ((), ())), preferred_element_type=F32)


def _split(x):
    hi = x.astype(MXU_DTYPE)
    lo = (x - hi.astype(F32)).astype(MXU_DTYPE)
    return hi, lo


def _mm_exact_rhs(x, m):
    hi, lo = _split(x)
    return _mm(hi, m) + _mm(lo, m)


def _mm_exact_lhs(m, x):
    hi, lo = _split(x)
    return _mm(m, hi) + _mm(m, lo)


def _my_place():
    return lax.axis_index("x"), lax.axis_index("y"), lax.axis_index("c")


def _all_gather(shard, name):
    r, c_ = shard.shape

    def body(x_ref, out_ref, send_sems, recv_sems, local_sem):
        x, y, c = _my_place()
        me, sibling = (x, y, c), (x, y, 1 - c)
        chips = [(1 - x, y), (x, 1 - y), (1 - x, 1 - y)]

        def slot(px, py, pc):
            return out_ref.at[4 * px + 2 * py + pc]

        def copy(k, block, to, src=None):
            return pltpu.make_async_remote_copy(
                src_ref=slot(*block) if src is None else src, dst_ref=slot(*block),
                send_sem=send_sems.at[k], recv_sem=recv_sems.at[k], device_id=to, device_id_type=MESH_ID)

        mine = pltpu.make_async_copy(x_ref, slot(*me), local_sem)
        mine.start()
        first = [copy(0, me, sibling, src=x_ref)]
        first += [copy(1 + j, me, (*chip, c), src=x_ref) for j, chip in enumerate(chips)]
        for cp in first:
            cp.start()
        passed = [copy(4 + j, (*chip, c), sibling) for j, chip in enumerate(chips)]
        for j, chip in enumerate(chips):
            copy(1 + j, (*chip, c), me).wait_recv()
            passed[j].start()
        copy(0, sibling, me).wait_recv()
        for j, chip in enumerate(chips):
            copy(4 + j, (*chip, 1 - c), me).wait_recv()
        for cp in first + passed:
            cp.wait_send()
        mine.wait()

    return pl.pallas_call(
        body, name=name,
        out_shape=jax.ShapeDtypeStruct((N_DEV, r, c_), shard.dtype),
        in_specs=[pl.BlockSpec(memory_space=pl.ANY)],
        out_specs=pl.BlockSpec(memory_space=pl.ANY),
        scratch_shapes=[pltpu.SemaphoreType.DMA((7,)), pltpu.SemaphoreType.DMA((7,)), pltpu.SemaphoreType.DMA(())],
    )(shard)


def _swap_with_sibling(buf, name):
    def body(x_ref, out_ref, send_sem, recv_sem):
        x, y, c = _my_place()
        cp = pltpu.make_async_remote_copy(src_ref=x_ref, dst_ref=out_ref, send_sem=send_sem, recv_sem=recv_sem,
                                          device_id=(x, y, 1 - c), device_id_type=MESH_ID)
        cp.start()
        cp.wait()

    return pl.pallas_call(
        body, name=name, out_shape=jax.ShapeDtypeStruct(buf.shape, buf.dtype),
        in_specs=[pl.BlockSpec(memory_space=pl.ANY)], out_specs=pl.BlockSpec(memory_space=pl.ANY),
        scratch_shapes=[pltpu.SemaphoreType.DMA(()), pltpu.SemaphoreType.DMA(())],
    )(buf)


def _exchange_chips(parts, name):
    def body(p_ref, out_ref, send_sems, recv_sems, local_sem):
        x, y, c = _my_place()
        my_q = 2 * x + y
        chips = [(1 - x, y), (x, 1 - y), (1 - x, 1 - y)]
        local = pltpu.make_async_copy(p_ref.at[my_q], out_ref.at[my_q], local_sem)
        local.start()

        def copy(k, px, py):
            return pltpu.make_async_remote_copy(
                src_ref=p_ref.at[2 * px + py], dst_ref=out_ref.at[my_q],
                send_sem=send_sems.at[k], recv_sem=recv_sems.at[k], device_id=(px, py, c), device_id_type=MESH_ID)

        def landing(k, px, py):
            return pltpu.make_async_remote_copy(
                src_ref=p_ref.at[my_q], dst_ref=out_ref.at[2 * px + py],
                send_sem=send_sems.at[k], recv_sem=recv_sems.at[k], device_id=(px, py, c), device_id_type=MESH_ID)

        sends = [copy(k, px, py) for k, (px, py) in enumerate(chips)]
        for cp in sends:
            cp.start()
        for k, (px, py) in enumerate(chips):
            landing(k, px, py).wait_recv()
        for cp in sends:
            cp.wait_send()
        local.wait()

    return pl.pallas_call(
        body, name=name, out_shape=jax.ShapeDtypeStruct(parts.shape, parts.dtype),
        in_specs=[pl.BlockSpec(memory_space=pl.ANY)], out_specs=pl.BlockSpec(memory_space=pl.ANY),
        scratch_shapes=[pltpu.SemaphoreType.DMA((3,)), pltpu.SemaphoreType.DMA((3,)), pltpu.SemaphoreType.DMA(())],
    )(parts)


def _matmul(a, b, mode, out_dtype, name, add=None, tm=512, tn=512, tk=512):
    if mode == "nn":
        (m, k), (k2, n) = a.shape, b.shape
    elif mode == "nt":
        (m, k), (n, k2) = a.shape, b.shape
    else:
        (k, m), (k2, n) = a.shape, b.shape
    assert k == k2, (a.shape, b.shape, mode)
    tm, tn, tk = min(tm, m), min(tn, n), min(tk, k)
    assert m % tm == 0 and n % tn == 0 and k % tk == 0, (m, n, k, tm, tn, tk)
    nk = k // tk
    mm = {"nn": _mm, "nt": _mm_nt, "tn": _mm_tn}[mode]

    def body(*refs):
        if add is None:
            a_ref, b_ref, o_ref, acc_ref = refs
        else:
            a_ref, b_ref, add_ref, o_ref, acc_ref = refs
        kk = pl.program_id(2)

        @pl.when(kk == 0)
        def _():
            acc_ref[...] = jnp.zeros_like(acc_ref)

        acc_ref[...] += mm(a_ref[...].astype(MXU_DTYPE), b_ref[...].astype(MXU_DTYPE))

        @pl.when(kk == nk - 1)
        def _():
            res = acc_ref[...]
            if add is not None:
                res = res + add_ref[...]
            o_ref[...] = res.astype(o_ref.dtype)

    a_spec = {"nn": pl.BlockSpec((tm, tk), lambda i, j, kk: (i, kk)),
              "nt": pl.BlockSpec((tm, tk), lambda i, j, kk: (i, kk)),
              "tn": pl.BlockSpec((tk, tm), lambda i, j, kk: (kk, i))}[mode]
    b_spec = {"nn": pl.BlockSpec((tk, tn), lambda i, j, kk: (kk, j)),
              "nt": pl.BlockSpec((tn, tk), lambda i, j, kk: (j, kk)),
              "tn": pl.BlockSpec((tk, tn), lambda i, j, kk: (kk, j))}[mode]
    o_spec = pl.BlockSpec((tm, tn), lambda i, j, kk: (i, j))
    in_specs, args = [a_spec, b_spec], [a, b]
    if add is not None:
        in_specs.append(o_spec)
        args.append(add)
    return pl.pallas_call(
        body, name=name, grid=(m // tm, n // tn, nk), in_specs=in_specs, out_specs=o_spec,
        out_shape=jax.ShapeDtypeStruct((m, n), out_dtype),
        scratch_shapes=[pltpu.VMEM((tm, tn), F32)],
        compiler_params=_cp("parallel", "parallel", "arbitrary"),
    )(*args)


def _rmsnorm(x, g, name):
    t, d = x.shape
    tm = min(512, t)

    def body(x_ref, g_ref, o_ref):
        xv = x_ref[...]
        r = lax.rsqrt(jnp.mean(xv * xv, axis=-1, keepdims=True) + NORM_EPS)
        o_ref[...] = (xv * r * g_ref[...]).astype(o_ref.dtype)

    return pl.pallas_call(
        body, name=name, grid=(t // tm,),
        in_specs=[pl.BlockSpec((tm, d), lambda i: (i, 0)), pl.BlockSpec((1, d), lambda i: (0, 0))],
        out_specs=pl.BlockSpec((tm, d), lambda i: (i, 0)),
        out_shape=jax.ShapeDtypeStruct((t, d), MXU_DTYPE), compiler_params=_cp("parallel"),
    )(x, g.reshape(1, d))


def _rmsnorm_bwd(x, g, dh, dres, name):
    t, d = x.shape
    tm = min(512, t)

    def body(x_ref, g_ref, dh_ref, dres_ref, dx_ref, dg_ref):
        @pl.when(pl.program_id(0) == 0)
        def _():
            dg_ref[...] = jnp.zeros_like(dg_ref)

        xv = x_ref[...]
        r = lax.rsqrt(jnp.mean(xv * xv, axis=-1, keepdims=True) + NORM_EPS)
        xn = xv * r
        dhv = dh_ref[...]
        u = dhv * g_ref[...]
        dx_ref[...] = dres_ref[...] + r * (u - xn * jnp.mean(u * xn, axis=-1, keepdims=True))
        dg_ref[...] += jnp.sum(dhv * xn, axis=0, keepdims=True)

    row = pl.BlockSpec((tm, d), lambda i: (i, 0))
    vec = pl.BlockSpec((1, d), lambda i: (0, 0))
    return pl.pallas_call(
        body, name=name, grid=(t // tm,), in_specs=[row, vec, row, row], out_specs=[row, vec],
        out_shape=[jax.ShapeDtypeStruct((t, d), F32), jax.ShapeDtypeStruct((1, d), F32)],
        compiler_params=_cp("arbitrary"),
    )(x, g.reshape(1, d), dh, dres)


def _final_norm_loss(x, g, target, name):
    t, d = x.shape
    tm = min(512, t)

    def body(x_ref, g_ref, t_ref, dx_ref, dg_ref, loss_ref):
        @pl.when(pl.program_id(0) == 0)
        def _():
            dg_ref[...] = jnp.zeros_like(dg_ref)
            loss_ref[...] = jnp.zeros_like(loss_ref)

        xv = x_ref[...]
        gv = g_ref[...]
        r = lax.rsqrt(jnp.mean(xv * xv, axis=-1, keepdims=True) + NORM_EPS)
        xn = xv * r
        err = xn * gv - t_ref[...]
        per_tok = jnp.mean(err * err, axis=-1, keepdims=True)
        loss_ref[...] += jnp.broadcast_to(0.5 * jnp.sum(per_tok, axis=0, keepdims=True), loss_ref.shape)
        dy = err * (1.0 / d)
        u = dy * gv
        dx_ref[...] = r * (u - xn * jnp.mean(u * xn, axis=-1, keepdims=True))
        dg_ref[...] += jnp.sum(dy * xn, axis=0, keepdims=True)

    row = pl.BlockSpec((tm, d), lambda i: (i, 0))
    vec = pl.BlockSpec((1, d), lambda i: (0, 0))
    return pl.pallas_call(
        body, name=name, grid=(t // tm,), in_specs=[row, vec, row],
        out_specs=[row, vec, pl.BlockSpec((1, LANES), lambda i: (0, 0))],
        out_shape=[jax.ShapeDtypeStruct((t, d), F32), jax.ShapeDtypeStruct((1, d), F32),
                   jax.ShapeDtypeStruct((1, LANES), F32)],
        compiler_params=_cp("arbitrary"),
    )(x, g.reshape(1, d), target)


def _shift_down(p, s, row):
    return jnp.where(row >= s, pltpu.roll(p, s, axis=0), 0.0)


def _shift_up(p, s, row):
    n = p.shape[0]
    return jnp.where(row < n - s, pltpu.roll(p, n - s, axis=0), 0.0)


def _conv_pre(p, w_ref, b_ref, row):
    width = w_ref.shape[0]
    u = b_ref[...] + w_ref[width - 1:width, :] * p
    for s in range(1, width):
        u = u + w_ref[width - 1 - s:width - s, :] * _shift_down(p, s, row)
    return u


def _conv_transpose(du, w_ref, row):
    width = w_ref.shape[0]
    dp = w_ref[width - 1:width, :] * du
    for s in range(1, width):
        dp = dp + w_ref[width - 1 - s:width - s, :] * _shift_up(du, s, row)
    return dp


def _conv_wgrad(du, p, dw_ref, db_ref, row):
    width = dw_ref.shape[0]
    db_ref[...] = jnp.sum(du, axis=0, keepdims=True)
    dw_ref[width - 1:width, :] = jnp.sum(du * p, axis=0, keepdims=True)
    for s in range(1, width):
        dw_ref[width - 1 - s:width - s, :] = jnp.sum(du * _shift_down(p, s, row), axis=0, keepdims=True)


CONV_COLS = 128


def _ssd_conv_fwd(proj, w, b, name):
    t = proj.shape[0]
    cb = CONV_COLS
    off = SSD_D_INNER // cb

    def body(p_ref, w_ref, b_ref, o_ref):
        p = p_ref[...]
        row = _iota(p.shape, 0)
        u = _conv_pre(p, w_ref, b_ref, row)
        o_ref[...] = u * _sigmoid(u)

    return pl.pallas_call(
        body, name=name, grid=(SSD_CONV_DIM // cb,),
        in_specs=[pl.BlockSpec((t, cb), lambda j: (0, j + off)), pl.BlockSpec((SSD_CONV, cb), lambda j: (0, j)),
                  pl.BlockSpec((1, cb), lambda j: (0, j))],
        out_specs=pl.BlockSpec((t, cb), lambda j: (0, j)),
        out_shape=jax.ShapeDtypeStruct((t, SSD_CONV_DIM), F32), compiler_params=_cp("parallel"),
    )(proj, w, b.reshape(1, -1))


def _ssd_conv_bwd(proj, w, b, dact, name):
    t = proj.shape[0]
    cb = CONV_COLS
    off = SSD_D_INNER // cb

    def body(p_ref, w_ref, b_ref, da_ref, dp_ref, dw_ref, db_ref):
        p = p_ref[...]
        row = _iota(p.shape, 0)
        u = _conv_pre(p, w_ref, b_ref, row)
        sg = _sigmoid(u)
        du = da_ref[...] * (sg * (1.0 + u * (1.0 - sg)))
        dp_ref[...] = _conv_transpose(du, w_ref, row).astype(dp_ref.dtype)
        _conv_wgrad(du, p, dw_ref, db_ref, row)

    col = pl.BlockSpec((t, cb), lambda j: (0, j))
    wspec = pl.BlockSpec((SSD_CONV, cb), lambda j: (0, j))
    bspec = pl.BlockSpec((1, cb), lambda j: (0, j))
    return pl.pallas_call(
        body, name=name, grid=(SSD_CONV_DIM // cb,),
        in_specs=[pl.BlockSpec((t, cb), lambda j: (0, j + off)), wspec, bspec, col],
        out_specs=[col, wspec, bspec],
        out_shape=[jax.ShapeDtypeStruct((t, SSD_CONV_DIM), MXU_DTYPE), jax.ShapeDtypeStruct((SSD_CONV, SSD_CONV_DIM), F32),
                   jax.ShapeDtypeStruct((1, SSD_CONV_DIM), F32)],
        compiler_params=_cp("parallel"),
    )(proj, w, b.reshape(1, -1), dact)


def _ffn_conv_fwd(proj, w, b, name):
    t = proj.shape[0]
    cb = CONV_COLS
    nb = FFN_D_FF // cb

    def body(pg_ref, pu_ref, wg_ref, wu_ref, bg_ref, bu_ref, o_ref):
        row = _iota(pg_ref.shape, 0)
        ug = _conv_pre(pg_ref[...], wg_ref, bg_ref, row)
        uu = _conv_pre(pu_ref[...], wu_ref, bu_ref, row)
        o_ref[...] = (ug * _sigmoid(ug) * uu).astype(o_ref.dtype)

    gcol = pl.BlockSpec((t, cb), lambda j: (0, j))
    ucol = pl.BlockSpec((t, cb), lambda j: (0, j + nb))
    b2 = b.reshape(1, -1)
    return pl.pallas_call(
        body, name=name, grid=(nb,),
        in_specs=[gcol, ucol, pl.BlockSpec((FFN_CONV, cb), lambda j: (0, j)), pl.BlockSpec((FFN_CONV, cb), lambda j: (0, j + nb)),
                  pl.BlockSpec((1, cb), lambda j: (0, j)), pl.BlockSpec((1, cb), lambda j: (0, j + nb))],
        out_specs=gcol, out_shape=jax.ShapeDtypeStruct((t, FFN_D_FF), MXU_DTYPE), compiler_params=_cp("parallel"),
    )(proj, proj, w, w, b2, b2)


def _ffn_conv_bwd(proj, w, b, dact, name):
    t = proj.shape[0]
    cb = CONV_COLS
    nb = FFN_D_FF // cb

    def body(pg_ref, pu_ref, wg_ref, wu_ref, bg_ref, bu_ref, da_ref,
             dpg_ref, dpu_ref, dwg_ref, dwu_ref, dbg_ref, dbu_ref):
        pg, pu = pg_ref[...], pu_ref[...]
        row = _iota(pg.shape, 0)
        ug = _conv_pre(pg, wg_ref, bg_ref, row)
        uu = _conv_pre(pu, wu_ref, bu_ref, row)
        sg = _sigmoid(ug)
        da = da_ref[...]
        dug = da * uu * (sg * (1.0 + ug * (1.0 - sg)))
        duu = da * (ug * sg)
        dpg_ref[...] = _conv_transpose(dug, wg_ref, row).astype(dpg_ref.dtype)
        dpu_ref[...] = _conv_transpose(duu, wu_ref, row).astype(dpu_ref.dtype)
        _conv_wgrad(dug, pg, dwg_ref, dbg_ref, row)
        _conv_wgrad(duu, pu, dwu_ref, dbu_ref, row)

    gcol = pl.BlockSpec((t, cb), lambda j: (0, j))
    ucol = pl.BlockSpec((t, cb), lambda j: (0, j + nb))
    wg = pl.BlockSpec((FFN_CONV, cb), lambda j: (0, j))
    wu = pl.BlockSpec((FFN_CONV, cb), lambda j: (0, j + nb))
    bg = pl.BlockSpec((1, cb), lambda j: (0, j))
    bu = pl.BlockSpec((1, cb), lambda j: (0, j + nb))
    b2 = b.reshape(1, -1)
    half = jax.ShapeDtypeStruct((t, FFN_D_FF), MXU_DTYPE)
    return pl.pallas_call(
        body, name=name, grid=(nb,),
        in_specs=[gcol, ucol, wg, wu, bg, bu, gcol],
        out_specs=[gcol, gcol, wg, wg, bg, bg],
        out_shape=[half, half, jax.ShapeDtypeStruct((FFN_CONV, FFN_D_FF), F32), jax.ShapeDtypeStruct((FFN_CONV, FFN_D_FF), F32),
                   jax.ShapeDtypeStruct((1, FFN_D_FF), F32), jax.ShapeDtypeStruct((1, FFN_D_FF), F32)],
        compiler_params=_cp("parallel"),
    )(proj, proj, w, w, b2, b2, dact)


SSD_ROWS = 128
DT_COL = (SSD_D_INNER + SSD_CONV_DIM) // LANES


def _head_expand():
    h = np.arange(LANES)[:, None]
    col = np.arange(SSD_D_INNER)[None, :]
    return jnp.asarray((col // SSD_HEAD_DIM == h), MXU_DTYPE)


def _chunk_tri(n, lower):
    t = _iota((n, n), 0)
    s = _iota((n, n), 1)
    shift = SSD_CHUNK.bit_length() - 1
    same = jnp.right_shift(t, shift) == jnp.right_shift(s, shift)
    tri = (s <= t) if lower else (s >= t)
    return jnp.where(same & tri, 1.0, 0.0).astype(MXU_DTYPE)


def _softplus(x):
    return jnp.maximum(x, 0.0) + jnp.log(1.0 + jnp.exp(-jnp.abs(x)))


def _ssd_dt_fwd(proj, act, dt_bias, a_neg, expand, name):
    t = proj.shape[0]
    tm = min(SSD_ROWS, t)

    def body(raw_ref, xs_ref, bias_ref, a_ref, e_ref, xdt_ref, dt_ref, acum_ref):
        lane = _iota((tm, LANES), 1)
        dt = jnp.where(lane < SSD_HEADS, _softplus(raw_ref[...] + bias_ref[...]), 0.0)
        dt_ref[...] = dt
        xdt_ref[...] = xs_ref[...] * _mm_exact_rhs(dt, e_ref[...])
        acum_ref[...] = _mm_exact_lhs(_chunk_tri(tm, True), a_ref[...] * dt)

    vec = pl.BlockSpec((1, LANES), lambda i: (0, 0))
    return pl.pallas_call(
        body, name=name, grid=(t // tm,),
        in_specs=[pl.BlockSpec((tm, LANES), lambda i: (i, DT_COL)), pl.BlockSpec((tm, SSD_D_INNER), lambda i: (i, 0)),
                  vec, vec, pl.BlockSpec((LANES, SSD_D_INNER), lambda i: (0, 0))],
        out_specs=[pl.BlockSpec((tm, SSD_D_INNER), lambda i: (i, 0)), pl.BlockSpec((tm, LANES), lambda i: (i, 0)),
                   pl.BlockSpec((tm, LANES), lambda i: (i, 0))],
        out_shape=[jax.ShapeDtypeStruct((t, SSD_D_INNER), F32), jax.ShapeDtypeStruct((t, LANES), F32),
                   jax.ShapeDtypeStruct((t, LANES), F32)],
        compiler_params=_cp("parallel"),
    )(proj, act, dt_bias, a_neg, expand)


def _ssd_dt_bwd(proj, act, dt, dxdt, dyy, dacum, dt_bias, a_neg, d_exp, expand, expand_t, name):
    t = proj.shape[0]
    tm = min(SSD_ROWS, t)

    def body(raw_ref, xs_ref, dt_ref, dxdt_ref, dyy_ref, dac_ref, bias_ref, a_ref, dsk_ref, e_ref, et_ref,
             dxs_ref, draw_ref, da_ref, dbias_ref, dd_ref):
        @pl.when(pl.program_id(0) == 0)
        def _():
            da_ref[...] = jnp.zeros_like(da_ref)
            dbias_ref[...] = jnp.zeros_like(dbias_ref)
            dd_ref[...] = jnp.zeros_like(dd_ref)

        lane = _iota((tm, LANES), 1)
        xs, dt, dxdt, dyy = xs_ref[...], dt_ref[...], dxdt_ref[...], dyy_ref[...]
        dxs_ref[...] = dxdt * _mm_exact_rhs(dt, e_ref[...]) + dsk_ref[...] * dyy
        dd_ref[...] += jnp.sum(dyy * xs, axis=0, keepdims=True)
        ddt = _mm_exact_rhs(dxdt * xs, et_ref[...])
        da = _mm_exact_lhs(_chunk_tri(tm, False), dac_ref[...])
        ddt = ddt + da * a_ref[...]
        da_ref[...] += jnp.sum(da * dt, axis=0, keepdims=True)
        draw = jnp.where(lane < SSD_HEADS, ddt * _sigmoid(raw_ref[...] + bias_ref[...]), 0.0)
        dbias_ref[...] += jnp.sum(draw, axis=0, keepdims=True)
        draw_ref[...] = draw.astype(draw_ref.dtype)

    wide = pl.BlockSpec((tm, SSD_D_INNER), lambda i: (i, 0))
    thin = pl.BlockSpec((tm, LANES), lambda i: (i, 0))
    vec = pl.BlockSpec((1, LANES), lambda i: (0, 0))
    wvec = pl.BlockSpec((1, SSD_D_INNER), lambda i: (0, 0))
    return pl.pallas_call(
        body, name=name, grid=(t // tm,),
        in_specs=[pl.BlockSpec((tm, LANES), lambda i: (i, DT_COL)), wide, thin, wide, wide, thin, vec, vec, wvec,
                  pl.BlockSpec((LANES, SSD_D_INNER), lambda i: (0, 0)), pl.BlockSpec((SSD_D_INNER, LANES), lambda i: (0, 0))],
        out_specs=[wide, thin, vec, vec, wvec],
        out_shape=[jax.ShapeDtypeStruct((t, SSD_D_INNER), F32), jax.ShapeDtypeStruct((t, LANES), MXU_DTYPE),
                   jax.ShapeDtypeStruct((1, LANES), F32), jax.ShapeDtypeStruct((1, LANES), F32),
                   jax.ShapeDtypeStruct((1, SSD_D_INNER), F32)],
        compiler_params=_cp("arbitrary"),
    )(proj, act, dt, dxdt, dyy, dacum, dt_bias, a_neg, d_exp, expand, expand_t)


SSD_PAIR = 2 * SSD_HEAD_DIM
HEADS_PER_GROUP = SSD_HEADS // SSD_GROUPS
GROUP_COLS = HEADS_PER_GROUP * SSD_HEAD_DIM
B_COL0 = SSD_D_INNER // SSD_STATE
C_COL0 = (SSD_D_INNER + SSD_GROUPS * SSD_STATE) // SSD_STATE


def _pair_cols(vals, h0, lo_mask):
    return jnp.where(lo_mask, vals[:, h0:h0 + 1], vals[:, h0 + 1:h0 + 2])


def _ssd_scan_fwd(xdt, act, acum_g, acum_gt, name):
    t = xdt.shape[0]
    nc = t // SSD_CHUNK
    ln = SSD_CHUNK

    def body(x_ref, b_ref, c_ref, ac_ref, act_ref, y_ref, sst_ref, state):
        @pl.when(pl.program_id(1) == 0)
        def _():
            state[...] = jnp.zeros_like(state)

        bm = b_ref[...].astype(MXU_DTYPE)
        cm = c_ref[...].astype(MXU_DTYPE)
        cb = _mm_nt(cm, bm)
        ac, act_ = ac_ref[0], act_ref[0]
        causal = _iota((ln, ln), 1) <= _iota((ln, ln), 0)
        lo_mask = _iota((ln, SSD_PAIR), 1) < SSD_HEAD_DIM
        lo_rows = _iota((SSD_PAIR, SSD_STATE), 0) < SSD_HEAD_DIM
        sst_ref[0, 0] = state[...]
        last = ac[ln - 1:ln, :]
        e_ac = jnp.exp(ac)
        w_all = jnp.exp(last - ac)
        e_last = jnp.exp(last)
        for pr in range(2):
            cols = slice(pr * SSD_PAIR, (pr + 1) * SSD_PAIR)
            xp = x_ref[:, cols]
            sp = state[cols, :]
            ydiag = jnp.zeros((ln, SSD_PAIR), F32)
            for hh in range(2):
                h = 2 * pr + hh
                seg = ac[:, h:h + 1] - act_[h:h + 1, :]
                dec = jnp.exp(jnp.where(causal, seg, -1e30))
                mask = lo_mask if hh == 0 else jnp.logical_not(lo_mask)
                ydiag = ydiag + _mm((cb * dec).astype(MXU_DTYPE), jnp.where(mask, xp, 0.0).astype(MXU_DTYPE))
            yoff = _mm_nt(cm, sp.astype(MXU_DTYPE)) * _pair_cols(e_ac, 2 * pr, lo_mask)
            y_ref[:, cols] = ydiag + yoff
            xw = (xp * _pair_cols(w_all, 2 * pr, lo_mask)).astype(MXU_DTYPE)
            el = jnp.where(lo_rows, e_last[:, 2 * pr:2 * pr + 1], e_last[:, 2 * pr + 1:2 * pr + 2])
            state[cols, :] = sp * el + _mm_tn(xw, bm)

    return pl.pallas_call(
        body, name=name, grid=(SSD_GROUPS, nc),
        in_specs=[pl.BlockSpec((ln, GROUP_COLS), lambda g, c: (c, g)),
                  pl.BlockSpec((ln, SSD_STATE), lambda g, c: (c, B_COL0 + g)),
                  pl.BlockSpec((ln, SSD_STATE), lambda g, c: (c, C_COL0 + g)),
                  pl.BlockSpec((1, ln, HEADS_PER_GROUP), lambda g, c: (g, c, 0)),
                  pl.BlockSpec((1, HEADS_PER_GROUP, ln), lambda g, c: (g, 0, c))],
        out_specs=[pl.BlockSpec((ln, GROUP_COLS), lambda g, c: (c, g)),
                   pl.BlockSpec((1, 1, GROUP_COLS, SSD_STATE), lambda g, c: (c, g, 0, 0))],
        out_shape=[jax.ShapeDtypeStruct((t, SSD_D_INNER), F32),
                   jax.ShapeDtypeStruct((nc, SSD_GROUPS, GROUP_COLS, SSD_STATE), F32)],
        scratch_shapes=[pltpu.VMEM((GROUP_COLS, SSD_STATE), F32)],
        compiler_params=_cp("parallel", "arbitrary"),
    )(xdt, act, act, acum_g, acum_gt)


def _ssd_scan_bwd(xdt, act, acum_g, acum_gt, states, dy, name):
    t = xdt.shape[0]
    nc = t // SSD_CHUNK
    ln = SSD_CHUNK

    def body(x_ref, b_ref, c_ref, ac_ref, act_ref, sst_ref, dy_ref, dx_ref, db_ref, dc_ref, dacol_ref, darow_ref, dstate):
        @pl.when(pl.program_id(1) == 0)
        def _():
            dstate[...] = jnp.zeros_like(dstate)

        bm = b_ref[...].astype(MXU_DTYPE)
        cm = c_ref[...].astype(MXU_DTYPE)
        cb = _mm_nt(cm, bm)
        ac, act_ = ac_ref[0], act_ref[0]
        causal = _iota((ln, ln), 1) <= _iota((ln, ln), 0)
        lo_mask = _iota((ln, SSD_PAIR), 1) < SSD_HEAD_DIM
        lo_rows = _iota((SSD_PAIR, SSD_STATE), 0) < SSD_HEAD_DIM
        lane4 = _iota((ln, HEADS_PER_GROUP), 1)
        sub4 = _iota((HEADS_PER_GROUP, ln), 0)
        is_last = _iota((ln, 1), 0) == ln - 1
        last = ac[ln - 1:ln, :]
        e_ac = jnp.exp(ac)
        w_all = jnp.exp(last - ac)
        e_last = jnp.exp(last)
        dcb = jnp.zeros((ln, ln), F32)
        dc_acc = jnp.zeros((ln, SSD_STATE), F32)
        db_acc = jnp.zeros((ln, SSD_STATE), F32)
        dacol = jnp.zeros((ln, HEADS_PER_GROUP), F32)
        darow = jnp.zeros((HEADS_PER_GROUP, ln), F32)
        for pr in range(2):
            cols = slice(pr * SSD_PAIR, (pr + 1) * SSD_PAIR)
            xp = x_ref[:, cols]
            dyp = dy_ref[:, cols]
            sp = sst_ref[0, 0, cols, :]
            dsp = dstate[cols, :]
            ea = _pair_cols(e_ac, 2 * pr, lo_mask)
            w = _pair_cols(w_all, 2 * pr, lo_mask)
            dye = (dyp * ea).astype(MXU_DTYPE)
            dx_state = w * _mm_nt(bm, dsp.astype(MXU_DTYPE))
            yoff = _mm_nt(cm, sp.astype(MXU_DTYPE)) * ea
            dxp = dx_state
            for hh in range(2):
                h = 2 * pr + hh
                mask = lo_mask if hh == 0 else jnp.logical_not(lo_mask)
                rmask = lo_rows if hh == 0 else jnp.logical_not(lo_rows)
                seg = ac[:, h:h + 1] - act_[h:h + 1, :]
                dec = jnp.exp(jnp.where(causal, seg, -1e30))
                m = cb * dec
                dym = jnp.where(mask, dyp, 0.0).astype(MXU_DTYPE)
                xm = jnp.where(mask, xp, 0.0).astype(MXU_DTYPE)
                g = _mm_nt(dym, xm)
                dxp = dxp + _mm_tn(m.astype(MXU_DTYPE), dym)
                dcb = dcb + dec * g
                mg = m * g
                rs = jnp.sum(mg, axis=1, keepdims=True)
                cs = jnp.sum(mg, axis=0, keepdims=True)
                t_off = jnp.sum(jnp.where(mask, dyp * yoff, 0.0), axis=1, keepdims=True)
                q = jnp.sum(jnp.where(mask, xp * dx_state, 0.0), axis=1, keepdims=True)
                qsum = jnp.sum(q, axis=0, keepdims=True)
                ds_s = jnp.sum(jnp.sum(jnp.where(rmask, dsp * sp, 0.0), axis=1, keepdims=True), axis=0, keepdims=True)
                extra = qsum + e_last[:, h:h + 1] * ds_s
                col = rs + t_off - q + jnp.where(is_last, extra, 0.0)
                dacol = jnp.where(lane4 == h, col, dacol)
                darow = jnp.where(sub4 == h, -cs, darow)
            dx_ref[:, cols] = dxp
            dc_acc = dc_acc + _mm(dye, sp.astype(MXU_DTYPE))
            db_acc = db_acc + _mm((xp * w).astype(MXU_DTYPE), dsp.astype(MXU_DTYPE))
            el = jnp.where(lo_rows, e_last[:, 2 * pr:2 * pr + 1], e_last[:, 2 * pr + 1:2 * pr + 2])
            dstate[cols, :] = dsp * el + _mm_tn(dye, cm)
        dcbm = dcb.astype(MXU_DTYPE)
        dc_ref[...] = _mm(dcbm, bm) + dc_acc
        db_ref[...] = _mm_tn(dcbm, cm) + db_acc
        dacol_ref[0] = dacol
        darow_ref[0] = darow

    def rev(c):
        return nc - 1 - c

    grp = pl.BlockSpec((ln, GROUP_COLS), lambda g, c: (rev(c), g))
    return pl.pallas_call(
        body, name=name, grid=(SSD_GROUPS, nc),
        in_specs=[grp,
                  pl.BlockSpec((ln, SSD_STATE), lambda g, c: (rev(c), B_COL0 + g)),
                  pl.BlockSpec((ln, SSD_STATE), lambda g, c: (rev(c), C_COL0 + g)),
                  pl.BlockSpec((1, ln, HEADS_PER_GROUP), lambda g, c: (g, rev(c), 0)),
                  pl.BlockSpec((1, HEADS_PER_GROUP, ln), lambda g, c: (g, 0, rev(c))),
                  pl.BlockSpec((1, 1, GROUP_COLS, SSD_STATE), lambda g, c: (rev(c), g, 0, 0)),
                  grp],
        out_specs=[grp,
                   pl.BlockSpec((ln, SSD_STATE), lambda g, c: (rev(c), g)),
                   pl.BlockSpec((ln, SSD_STATE), lambda g, c: (rev(c), g)),
                   pl.BlockSpec((1, ln, HEADS_PER_GROUP), lambda g, c: (g, rev(c), 0)),
                   pl.BlockSpec((1, HEADS_PER_GROUP, ln), lambda g, c: (g, 0, rev(c)))],
        out_shape=[jax.ShapeDtypeStruct((t, SSD_D_INNER), F32),
                   jax.ShapeDtypeStruct((t, SSD_GROUPS * SSD_STATE), F32),
                   jax.ShapeDtypeStruct((t, SSD_GROUPS * SSD_STATE), F32),
                   jax.ShapeDtypeStruct((SSD_GROUPS, t, HEADS_PER_GROUP), F32),
                   jax.ShapeDtypeStruct((SSD_GROUPS, HEADS_PER_GROUP, t), F32)],
        scratch_shapes=[pltpu.VMEM((GROUP_COLS, SSD_STATE), F32)],
        compiler_params=_cp("parallel", "arbitrary"),
    )(xdt, act, act, acum_g, acum_gt, states, dy)


GN_ROWS = 128


def _gated_norm_parts(y_ref, xs_ref, z_ref, dsk_ref):
    yy = y_ref[...] + dsk_ref[...] * xs_ref[...]
    z = z_ref[...]
    sz = _sigmoid(z)
    silu = z * sz
    u = yy * silu
    r = lax.rsqrt(jnp.mean(u * u, axis=-1, keepdims=True) + NORM_EPS)
    return yy, z, sz, silu, u, r


def _gated_norm_fwd(y, act, proj, d_exp, g, name):
    t = y.shape[0]
    tm = min(GN_ROWS, t)

    def body(y_ref, xs_ref, z_ref, dsk_ref, g_ref, o_ref):
        _, _, _, _, u, r = _gated_norm_parts(y_ref, xs_ref, z_ref, dsk_ref)
        o_ref[...] = (u * r * g_ref[...]).astype(o_ref.dtype)

    wide = pl.BlockSpec((tm, SSD_D_INNER), lambda i: (i, 0))
    wvec = pl.BlockSpec((1, SSD_D_INNER), lambda i: (0, 0))
    return pl.pallas_call(
        body, name=name, grid=(t // tm,), in_specs=[wide, wide, wide, wvec, wvec], out_specs=wide,
        out_shape=jax.ShapeDtypeStruct((t, SSD_D_INNER), MXU_DTYPE), compiler_params=_cp("parallel"),
    )(y, act, proj, d_exp, g.reshape(1, -1))


def _gated_norm_bwd(y, act, proj, d_exp, g, dn, name):
    t = y.shape[0]
    tm = min(GN_ROWS, t)

    def body(y_ref, xs_ref, z_ref, dsk_ref, g_ref, dn_ref, dyy_ref, dz_ref, dg_ref):
        @pl.when(pl.program_id(0) == 0)
        def _():
            dg_ref[...] = jnp.zeros_like(dg_ref)

        yy, z, sz, silu, u, r = _gated_norm_parts(y_ref, xs_ref, z_ref, dsk_ref)
        un = u * r
        dn = dn_ref[...]
        v = dn * g_ref[...]
        du = r * (v - un * jnp.mean(v * un, axis=-1, keepdims=True))
        dg_ref[...] += jnp.sum(dn * un, axis=0, keepdims=True)
        dyy_ref[...] = du * silu
        dz_ref[...] = (du * yy * (sz * (1.0 + z * (1.0 - sz)))).astype(dz_ref.dtype)

    wide = pl.BlockSpec((tm, SSD_D_INNER), lambda i: (i, 0))
    wvec = pl.BlockSpec((1, SSD_D_INNER), lambda i: (0, 0))
    return pl.pallas_call(
        body, name=name, grid=(t // tm,), in_specs=[wide, wide, wide, wvec, wvec, wide], out_specs=[wide, wide, wvec],
        out_shape=[jax.ShapeDtypeStruct((t, SSD_D_INNER), F32), jax.ShapeDtypeStruct((t, SSD_D_INNER), MXU_DTYPE),
                   jax.ShapeDtypeStruct((1, SSD_D_INNER), F32)],
        compiler_params=_cp("arbitrary"),
    )(y, act, proj, d_exp, g.reshape(1, -1), dn)


SB_PAIRS = SB_HEADS // 2


def _kv_rows(j, bt):
    return pl.ds(pl.multiple_of(j * bt, bt), bt)


def _sb_tile_masks(bt):
    lane = _iota((bt, bt), 1)
    rowi = _iota((bt, bt), 0)
    return lane < rowi, (rowi > lane).astype(MXU_DTYPE), (rowi < lane).astype(MXU_DTYPE)


def _sb_scaled_heads(pair, scale):
    lane = _iota(pair.shape, 1)
    val = pair.astype(F32) * scale
    return [jnp.where(lane < SB_HEAD_DIM, val, 0.0).astype(pair.dtype), jnp.where(lane >= SB_HEAD_DIM, val, 0.0).astype(pair.dtype)]


def _sb_logits(q_head, kb, strict):
    x = _mm_nt(q_head, kb)
    nlf = jnp.maximum(x, 0.0) + jnp.log(1.0 + jnp.exp(-jnp.abs(x)))
    if strict is not None:
        nlf = jnp.where(strict, nlf, 0.0)
    return x, nlf


def _sb_attention_fwd(qkv, name):
    t = qkv.shape[0]
    bt = min(SB_TILE, t)
    nq = t // bt

    def body(q_ref, k_ref, v_ref, o_ref, acc_ref):
        i = pl.program_id(1)
        strict, later, _ = _sb_tile_masks(bt)
        qs = _sb_scaled_heads(q_ref[...], SB_SCALE)
        acc_ref[...] = jnp.zeros_like(acc_ref)

        def block(j, carries, diag):
            kb = k_ref[_kv_rows(j, bt), :]
            vb = v_ref[_kv_rows(j, bt), :]
            new = []
            for hh in range(2):
                x, nlf = _sb_logits(qs[hh], kb, strict if diag else None)
                w = jnp.exp(x - nlf - _mm_exact_rhs(nlf, later) - carries[hh])
                if diag:
                    w = jnp.where(strict, w, 0.0)
                acc_ref[hh] += _mm(w.astype(MXU_DTYPE), vb)
                new.append(carries[hh] + jnp.sum(nlf, axis=1, keepdims=True))
            return tuple(new)

        zero = jnp.zeros((bt, 1), F32)
        carries = block(i, (zero, zero), True)
        lax.fori_loop(0, i, lambda it, cr: block(i - 1 - it, cr, False), carries)
        low = _iota((bt, 2 * SB_HEAD_DIM), 1) < SB_HEAD_DIM
        o_ref[...] = jnp.where(low, acc_ref[0], acc_ref[1]).astype(o_ref.dtype)

    lanes = 2 * SB_HEAD_DIM
    return pl.pallas_call(
        body, name=name, grid=(SB_PAIRS, nq),
        in_specs=[pl.BlockSpec((bt, lanes), lambda p, i: (i, p)),
                  pl.BlockSpec((t, lanes), lambda p, i: (0, SB_PAIRS + p)),
                  pl.BlockSpec((t, lanes), lambda p, i: (0, 2 * SB_PAIRS + p))],
        out_specs=pl.BlockSpec((bt, lanes), lambda p, i: (i, p)),
        out_shape=jax.ShapeDtypeStruct((t, D_MODEL), MXU_DTYPE),
        scratch_shapes=[pltpu.VMEM((2, bt, lanes), F32)],
        compiler_params=_cp("parallel", "parallel"),
    )(qkv, qkv, qkv)


def _sb_attention_bwd(qkv, do, name):
    t = qkv.shape[0]
    bt = min(SB_TILE, t)
    nq = t // bt
    lanes = 2 * SB_HEAD_DIM

    def body(q_ref, k_ref, v_ref, do_ref, dq_ref, dk_ref, dv_ref, sbuf, ebuf, dq_acc, dk_acc, dv_acc):
        i = pl.program_id(1)

        @pl.when(i == 0)
        def _():
            dk_acc[...] = jnp.zeros_like(dk_acc)
            dv_acc[...] = jnp.zeros_like(dv_acc)

        strict, later, earlier = _sb_tile_masks(bt)
        qs = _sb_scaled_heads(q_ref[...], SB_SCALE)
        dos = _sb_scaled_heads(do_ref[...], 1.0)
        dq_acc[...] = jnp.zeros_like(dq_acc)

        def pass1(j, carries, diag):
            rows = _kv_rows(j, bt)
            kb = k_ref[rows, :]
            vb = v_ref[rows, :]
            new, dv = [], None
            for hh in range(2):
                x, nlf = _sb_logits(qs[hh], kb, strict if diag else None)
                lb = x - nlf
                w = jnp.exp(lb - _mm_exact_rhs(nlf, later) - carries[hh])
                if diag:
                    w = jnp.where(strict, w, 0.0)
                sbuf[hh, :, rows] = jnp.exp(lb)
                ebuf[hh, :, rows] = w * _mm_nt(dos[hh], vb)
                part = _mm_tn(w.astype(MXU_DTYPE), dos[hh])
                dv = part if dv is None else dv + part
                new.append(carries[hh] + jnp.sum(nlf, axis=1, keepdims=True))
            dv_acc[rows, :] += dv
            return tuple(new)

        zero = jnp.zeros((bt, 1), F32)
        carries = pass1(i, (zero, zero), True)
        lax.fori_loop(0, i, lambda it, cr: pass1(i - 1 - it, cr, False), carries)

        def pass2(j, pres, diag):
            rows = _kv_rows(j, bt)
            kb = k_ref[rows, :]
            new, dk = [], None
            for hh in range(2):
                sg = sbuf[hh, :, rows]
                e = ebuf[hh, :, rows]
                dx = e - sg * (e + _mm_exact_rhs(e, earlier) + pres[hh])
                if diag:
                    dx = jnp.where(strict, dx, 0.0)
                dxm = dx.astype(MXU_DTYPE)
                dq_acc[hh] += _mm(dxm, kb)
                part = _mm_tn(dxm, qs[hh])
                dk = part if dk is None else dk + part
                new.append(pres[hh] + jnp.sum(e, axis=1, keepdims=True))
            dk_acc[rows, :] += dk
            return tuple(new)

        pres = lax.fori_loop(0, i, lambda j, pr: pass2(j, pr, False), (zero, zero))
        pass2(i, pres, True)
        low = _iota((bt, lanes), 1) < SB_HEAD_DIM
        dq_ref[...] = (jnp.where(low, dq_acc[0], dq_acc[1]) * SB_SCALE).astype(dq_ref.dtype)

        @pl.when(i == nq - 1)
        def _():
            dk_ref[...] = dk_acc[...].astype(dk_ref.dtype)
            dv_ref[...] = dv_acc[...].astype(dv_ref.dtype)

    blk = pl.BlockSpec((bt, lanes), lambda p, i: (i, p))
    whole = pl.BlockSpec((t, lanes), lambda p, i: (0, p))
    out = jax.ShapeDtypeStruct((t, D_MODEL), MXU_DTYPE)
    return pl.pallas_call(
        body, name=name, grid=(SB_PAIRS, nq),
        in_specs=[blk, pl.BlockSpec((t, lanes), lambda p, i: (0, SB_PAIRS + p)),
                  pl.BlockSpec((t, lanes), lambda p, i: (0, 2 * SB_PAIRS + p)), blk],
        out_specs=[blk, whole, whole], out_shape=[out, out, out],
        scratch_shapes=[pltpu.VMEM((2, bt, t), F32), pltpu.VMEM((2, bt, t), F32), pltpu.VMEM((2, bt, lanes), F32),
                        pltpu.VMEM((t, lanes), F32), pltpu.VMEM((t, lanes), F32)],
        compiler_params=_cp("parallel", "arbitrary"),
    )(qkv, qkv, qkv, do)


def _add_pair(a, b, name):
    s, r, c = a.shape
    tm = 256 if r % 256 == 0 else 8
    assert r % tm == 0

    def body(a_ref, b_ref, o_ref):
        o_ref[...] = (a_ref[...].astype(F32) + b_ref[...].astype(F32)).astype(o_ref.dtype)

    blk = pl.BlockSpec((1, tm, c), lambda q, i: (q, i, 0))
    return pl.pallas_call(body, name=name, grid=(s, r // tm), in_specs=[blk, blk], out_specs=blk,
                          out_shape=jax.ShapeDtypeStruct(a.shape, a.dtype), compiler_params=_cp("parallel", "parallel"))(a, b)


def _adamw(gslots, w, m, v, name):
    s, r, c = gslots.shape
    tm = 256 if r % 256 == 0 else 8
    assert r % tm == 0 and w.shape == (r, c)
    c1 = 1.0 - ADAM_B1 ** ADAM_STEP
    c2 = 1.0 - ADAM_B2 ** ADAM_STEP

    def body(g_ref, w_ref, m_ref, v_ref, go_ref, d_ref, mo_ref, vo_ref):
        g = g_ref[0].astype(F32)
        for q in range(1, s):
            g = g + g_ref[q].astype(F32)
        mn = ADAM_B1 * m_ref[...] + (1.0 - ADAM_B1) * g
        vn = ADAM_B2 * v_ref[...] + (1.0 - ADAM_B2) * (g * g)
        go_ref[...] = g
        mo_ref[...] = mn
        vo_ref[...] = vn
        d_ref[...] = -ADAM_LR * ((mn / c1) / (jnp.sqrt(vn / c2) + ADAM_EPS) + ADAM_WD * w_ref[...])

    row = pl.BlockSpec((tm, c), lambda i: (i, 0))
    out = jax.ShapeDtypeStruct((r, c), F32)
    return pl.pallas_call(
        body, name=name, grid=(r // tm,),
        in_specs=[pl.BlockSpec((s, tm, c), lambda i: (0, i, 0)), row, row, row],
        out_specs=[row, row, row, row], out_shape=[out, out, out, out], compiler_params=_cp("parallel"),
    )(gslots, w, m, v)


def _rows(a):
    flat = a.reshape(-1)
    pad = (-flat.shape[0]) % PACK_W
    if pad:
        flat = jnp.concatenate([flat, jnp.zeros((pad,), flat.dtype)])
    return flat.reshape(-1, PACK_W)


def _pack(arrays, row_multiple):
    parts, layout, off = [], [], 0
    for a in arrays:
        rw = _rows(a)
        parts.append(rw)
        layout.append((off, rw.shape[0], a.shape))
        off += rw.shape[0]
    pad = (-off) % row_multiple
    if pad:
        parts.append(jnp.zeros((pad, PACK_W), parts[0].dtype))
    return jnp.concatenate(parts, axis=0), layout


def _unpack(packed, layout):
    out = []
    for off, nrows, shape in layout:
        n = int(np.prod(shape))
        out.append(packed[off:off + nrows].reshape(-1)[:n].reshape(shape))
    return out


def _unshard_cols(g, lead):
    k = D_MODEL
    ns = g.shape[1] * PACK_W // (lead * k)
    return g.reshape(N_DEV, lead, k, ns).transpose(1, 2, 0, 3).reshape(lead, k, N_DEV * ns)


def _shard_cols(full):
    lead, k, n = full.shape
    return full.reshape(lead, k, N_DEV, n // N_DEV).transpose(2, 0, 1, 3)


def _unshard_rows(g, lead):
    ks = g.shape[1] // lead
    return g.reshape(N_DEV, lead, ks, PACK_W).transpose(1, 0, 2, 3).reshape(lead, N_DEV * ks, PACK_W)


def _shard_rows(full):
    lead, k, n = full.shape
    return full.reshape(lead, N_DEV, k // N_DEV, n).transpose(1, 0, 2, 3)


def _ssd_consts(dt_bias, a_log, d_skip):
    pad = LANES - SSD_HEADS
    bias = jnp.pad(dt_bias, (0, pad)).reshape(1, LANES)
    a_neg = jnp.pad(-jnp.exp(a_log), (0, pad)).reshape(1, LANES)
    d_exp = jnp.repeat(d_skip, SSD_HEAD_DIM).reshape(1, SSD_D_INNER)
    return bias, a_neg, d_exp


def _group_layouts(acum):
    t = acum.shape[0]
    a = acum[:, :SSD_HEADS].reshape(t, SSD_GROUPS, HEADS_PER_GROUP)
    return a.transpose(1, 0, 2), a.transpose(1, 2, 0)


def _ssd_fwd(x, p):
    hn = _rmsnorm(x, p["mix_norm"], "rmsnorm_fwd")
    proj = _matmul(hn, p["w_in"], "nn", F32, "ssd_in_fwd", tm=512, tn=896, tk=1024)
    act = _ssd_conv_fwd(proj, p["conv_w"], p["conv_b"], "ssd_conv_fwd")
    bias, a_neg, d_exp = _ssd_consts(p["dt_bias"], p["a_log"], p["d"])
    expand = _head_expand()
    xdt, dt, acum = _ssd_dt_fwd(proj, act, bias, a_neg, expand, "ssd_dt_fwd")
    acum_g, acum_gt = _group_layouts(acum)
    y, states = _ssd_scan_fwd(xdt, act, acum_g, acum_gt, "ssd_scan_fwd")
    yn = _gated_norm_fwd(y, act, proj, d_exp, p["norm"], "ssd_gnorm_fwd")
    x_new = _matmul(yn, p["w_out"], "nn", F32, "ssd_out_fwd", add=x, tm=512, tn=1024, tk=2048)
    saved = dict(x=x, hn=hn, proj=proj, act=act, xdt=xdt, dt=dt, acum_g=acum_g, acum_gt=acum_gt, y=y, states=states, yn=yn)
    return x_new, saved


def _ssd_bwd(dx, p, s):
    bias, a_neg, d_exp = _ssd_consts(p["dt_bias"], p["a_log"], p["d"])
    expand = _head_expand()
    dxb = dx.astype(MXU_DTYPE)
    dyn = _matmul(dxb, p["w_out"], "nt", F32, "ssd_out_dgrad", tm=512, tn=1024, tk=1024)
    g_w_out = _matmul(s["yn"], dxb, "tn", F32, "ssd_out_wgrad", tm=1024, tn=1024, tk=512)
    dyy, dz, g_norm = _gated_norm_bwd(s["y"], s["act"], s["proj"], d_exp, p["norm"], dyn, "ssd_gnorm_bwd")
    dxdt, dbm, dcm, dacol, darow = _ssd_scan_bwd(s["xdt"], s["act"], s["acum_g"], s["acum_gt"], s["states"], dyy, "ssd_scan_bwd")
    t = dx.shape[0]
    dacum = dacol.transpose(1, 0, 2).reshape(t, SSD_HEADS) + darow.transpose(2, 0, 1).reshape(t, SSD_HEADS)
    dacum = jnp.pad(dacum, ((0, 0), (0, LANES - SSD_HEADS)))
    dxs, draw, g_a, g_bias, g_dexp = _ssd_dt_bwd(s["proj"], s["act"], s["dt"], dxdt, dyy, dacum, bias, a_neg, d_exp,
                                                  expand, expand.T, "ssd_dt_bwd")
    dact = jnp.concatenate([dxs, dbm, dcm], axis=1)
    dxbc, g_conv_w, g_conv_b = _ssd_conv_bwd(s["proj"], p["conv_w"], p["conv_b"], dact, "ssd_conv_bwd")
    dproj = jnp.concatenate([dz, dxbc, draw], axis=1)
    dhn = _matmul(dproj, p["w_in"], "nt", F32, "ssd_in_dgrad", tm=512, tn=1024, tk=896)
    g_w_in = _matmul(s["hn"], dproj, "tn", F32, "ssd_in_wgrad", tm=1024, tn=896, tk=512)
    dx_new, g_mix = _rmsnorm_bwd(s["x"], p["mix_norm"], dhn, dx, "rmsnorm_bwd")
    grads = dict(w_in=g_w_in[:, :SSD_IN_DIM], w_out=g_w_out, conv_w=g_conv_w, conv_b=g_conv_b.reshape(-1),
                 dt_bias=g_bias[0, :SSD_HEADS], a_log=(g_a * a_neg)[0, :SSD_HEADS],
                 d=g_dexp.reshape(SSD_HEADS, SSD_HEAD_DIM).sum(axis=1), norm=g_norm.reshape(-1), mix_norm=g_mix.reshape(-1))
    return dx_new, grads


def _sb_fwd(x, p):
    hn = _rmsnorm(x, p["mix_norm"], "rmsnorm_fwd")
    qkv = _matmul(hn, p["w_qkv"], "nn", MXU_DTYPE, "sb_qkv_fwd", tm=512, tn=1024, tk=1024)
    o = _sb_attention_fwd(qkv, "sb_attn_fwd")
    x_new = _matmul(o, p["w_out"], "nn", F32, "sb_out_fwd", add=x, tm=512, tn=1024, tk=1024)
    return x_new, dict(x=x, hn=hn, qkv=qkv, o=o)


def _sb_bwd(dx, p, s):
    dxb = dx.astype(MXU_DTYPE)
    do = _matmul(dxb, p["w_out"], "nt", MXU_DTYPE, "sb_out_dgrad", tm=512, tn=1024, tk=1024)
    g_w_out = _matmul(s["o"], dxb, "tn", F32, "sb_out_wgrad", tm=1024, tn=1024, tk=512)
    dq, dk, dv = _sb_attention_bwd(s["qkv"], do, "sb_attn_bwd")
    dqkv = jnp.concatenate([dq, dk, dv], axis=1)
    dhn = _matmul(dqkv, p["w_qkv"], "nt", F32, "sb_qkv_dgrad", tm=512, tn=1024, tk=1024)
    g_w_qkv = _matmul(s["hn"], dqkv, "tn", F32, "sb_qkv_wgrad", tm=1024, tn=1024, tk=512)
    dx_new, g_mix = _rmsnorm_bwd(s["x"], p["mix_norm"], dhn, dx, "rmsnorm_bwd")
    return dx_new, dict(w_qkv=g_w_qkv, w_out=g_w_out, mix_norm=g_mix.reshape(-1))


def _ffn_fwd(x, p):
    hn = _rmsnorm(x, p["ffn_norm"], "rmsnorm_fwd")
    proj = _matmul(hn, p["w_in"], "nn", F32, "ffn_in_fwd", tm=512, tn=1408, tk=1024)
    act = _ffn_conv_fwd(proj, p["conv_w"], p["conv_b"], "ffn_conv_fwd")
    x_new = _matmul(act, p["w_out"], "nn", F32, "ffn_out_fwd", add=x, tm=512, tn=1024, tk=1408)
    return x_new, dict(x=x, hn=hn, proj=proj, act=act)


def _ffn_bwd(dx, p, s):
    dxb = dx.astype(MXU_DTYPE)
    dact = _matmul(dxb, p["w_out"], "nt", F32, "ffn_out_dgrad", tm=512, tn=1408, tk=1024)
    g_w_out = _matmul(s["act"], dxb, "tn", F32, "ffn_out_wgrad", tm=1408, tn=1024, tk=512)
    dpg, dpu, dwg, dwu, dbg, dbu = _ffn_conv_bwd(s["proj"], p["conv_w"], p["conv_b"], dact, "ffn_conv_bwd")
    dproj = jnp.concatenate([dpg, dpu], axis=1)
    dhn = _matmul(dproj, p["w_in"], "nt", F32, "ffn_in_dgrad", tm=512, tn=1024, tk=1408)
    g_w_in = _matmul(s["hn"], dproj, "tn", F32, "ffn_in_wgrad", tm=1024, tn=1408, tk=512)
    dx_new, g_norm = _rmsnorm_bwd(s["x"], p["ffn_norm"], dhn, dx, "rmsnorm_bwd")
    grads = dict(w_in=g_w_in, w_out=g_w_out, conv_w=jnp.concatenate([dwg, dwu], axis=1),
                 conv_b=jnp.concatenate([dbg, dbu], axis=1).reshape(-1), ffn_norm=g_norm.reshape(-1))
    return dx_new, grads


BIG_ROW_MULTIPLE = 256
BIG = ["ssd_w_in", "sb_w_qkv", "ffn_w_in", "ssd_w_out", "sb_w_out", "ffn_w_out"]
COL_SHARDED = {"ssd_w_in": 2, "sb_w_qkv": 2, "ffn_w_in": 4}
ROW_SHARDED = {"ssd_w_out": 2, "sb_w_out": 2, "ffn_w_out": 4}
CONV = ["ssd_conv_w", "ffn_conv_w"]
SMALL = ["mix_norm", "ffn_norm", "final_norm", "ssd_conv_b", "ssd_dt_bias", "ssd_a_log", "ssd_d", "ssd_norm", "ffn_conv_b"]
WEIGHTS = ["mix_norm", "ffn_norm", "final_norm", "ssd_w_in", "ssd_conv_w", "ssd_conv_b", "ssd_dt_bias", "ssd_a_log", "ssd_d",
           "ssd_norm", "ssd_w_out", "sb_w_qkv", "sb_w_out", "ffn_w_in", "ffn_conv_w", "ffn_conv_b", "ffn_w_out"]


def _step(x, loss_target, w, m, v):
    x = x.reshape(x.shape[-2], x.shape[-1])
    target = loss_target.reshape(x.shape)
    dev = 4 * lax.axis_index("x") + 2 * lax.axis_index("y") + lax.axis_index("c")
    core = lax.axis_index("c")

    big_pack, big_layout = _pack([w[n].astype(MXU_DTYPE) for n in BIG], BIG_ROW_MULTIPLE)
    big_all = _all_gather(big_pack, "gather_weights")
    full = {}
    for n, (off, nrows, _) in zip(BIG, big_layout):
        g = big_all[:, off:off + nrows]
        full[n] = _unshard_cols(g, COL_SHARDED[n]) if n in COL_SHARDED else _unshard_rows(g, ROW_SHARDED[n])
    full["ssd_w_in"] = jnp.pad(full["ssd_w_in"], ((0, 0), (0, 0), (0, SSD_IN_PAD - SSD_IN_DIM)))
    conv_pack, conv_layout = _pack([w[n] for n in CONV], 8)
    conv_all = _all_gather(conv_pack, "gather_conv_taps")
    for n, (off, nrows, shape) in zip(CONV, conv_layout):
        parts = [_unpack(conv_all[j], conv_layout)[CONV.index(n)] for j in range(N_DEV)]
        full[n] = jnp.concatenate(parts, axis=-1)

    def ssd_params(j):
        return dict(mix_norm=w["mix_norm"][2 * j], w_in=full["ssd_w_in"][j], conv_w=full["ssd_conv_w"][j],
                    conv_b=w["ssd_conv_b"][j], dt_bias=w["ssd_dt_bias"][j], a_log=w["ssd_a_log"][j], d=w["ssd_d"][j],
                    norm=w["ssd_norm"][j], w_out=full["ssd_w_out"][j])

    def sb_params(j):
        return dict(mix_norm=w["mix_norm"][2 * j + 1], w_qkv=full["sb_w_qkv"][j], w_out=full["sb_w_out"][j])

    def ffn_params(i):
        return dict(ffn_norm=w["ffn_norm"][i], w_in=full["ffn_w_in"][i], conv_w=full["ffn_conv_w"][i],
                    conv_b=w["ffn_conv_b"][i], w_out=full["ffn_w_out"][i])

    saved = []
    for i in range(DEPTH):
        if i % 2 == 0:
            x, s_mix = _ssd_fwd(x, ssd_params(i // 2))
        else:
            x, s_mix = _sb_fwd(x, sb_params(i // 2))
        x, s_ffn = _ffn_fwd(x, ffn_params(i))
        saved.append((s_mix, s_ffn))
    dx, g_final, loss_part = _final_norm_loss(x, w["final_norm"], target, "final_norm_loss")

    g_mix, g_ffn, g_ssd, g_sb = [None] * DEPTH, [None] * DEPTH, [None] * 2, [None] * 2
    for i in reversed(range(DEPTH)):
        s_mix, s_ffn = saved[i]
        dx, g_ffn[i] = _ffn_bwd(dx, ffn_params(i), s_ffn)
        if i % 2 == 0:
            dx, g_ssd[i // 2] = _ssd_bwd(dx, ssd_params(i // 2), s_mix)
            g_mix[i] = g_ssd[i // 2]["mix_norm"]
        else:
            dx, g_sb[i // 2] = _sb_bwd(dx, sb_params(i // 2), s_mix)
            g_mix[i] = g_sb[i // 2]["mix_norm"]
    grad_x = dx.reshape(1, *dx.shape)

    gfull = {
        "ssd_w_in": _shard_cols(jnp.stack([g["w_in"] for g in g_ssd])),
        "sb_w_qkv": _shard_cols(jnp.stack([g["w_qkv"] for g in g_sb])),
        "ffn_w_in": _shard_cols(jnp.stack([g["w_in"] for g in g_ffn])),
        "ssd_w_out": _shard_rows(jnp.stack([g["w_out"] for g in g_ssd])),
        "sb_w_out": _shard_rows(jnp.stack([g["w_out"] for g in g_sb])),
        "ffn_w_out": _shard_rows(jnp.stack([g["w_out"] for g in g_ffn])),
    }
    rtot = big_pack.shape[0]
    chunks = []
    for n, (off, nrows, _) in zip(BIG, big_layout):
        chunks.append(gfull[n].astype(MXU_DTYPE).reshape(N_DEV, nrows, PACK_W))
    pad_rows = rtot - sum(c_.shape[1] for c_ in chunks)
    if pad_rows:
        chunks.append(jnp.zeros((N_DEV, pad_rows, PACK_W), MXU_DTYPE))
    g8 = jnp.concatenate(chunks, axis=1).reshape(4, 2, rtot, PACK_W)
    keep = lax.dynamic_index_in_dim(g8, core, axis=1, keepdims=False)
    give = lax.dynamic_index_in_dim(g8, 1 - core, axis=1, keepdims=False)
    got = _swap_with_sibling(give, "grads_to_sibling")
    chip_part = _add_pair(keep, got, "grads_add_sibling")
    from_chips = _exchange_chips(chip_part, "grads_across_chips")
    w_pack, _ = _pack([w[n] for n in BIG], BIG_ROW_MULTIPLE)
    m_pack, _ = _pack([m[n] for n in BIG], BIG_ROW_MULTIPLE)
    v_pack, _ = _pack([v[n] for n in BIG], BIG_ROW_MULTIPLE)
    big_out = _adamw(from_chips, w_pack, m_pack, v_pack, "adamw_matmul_weights")
    big_res = [dict(zip(BIG, _unpack(o, big_layout))) for o in big_out]

    small_g = {
        "mix_norm": jnp.stack(g_mix), "ffn_norm": jnp.stack([g["ffn_norm"] for g in g_ffn]), "final_norm": g_final.reshape(-1),
        "ssd_conv_b": jnp.stack([g["conv_b"] for g in g_ssd]), "ssd_dt_bias": jnp.stack([g["dt_bias"] for g in g_ssd]),
        "ssd_a_log": jnp.stack([g["a_log"] for g in g_ssd]), "ssd_d": jnp.stack([g["d"] for g in g_ssd]),
        "ssd_norm": jnp.stack([g["norm"] for g in g_ssd]), "ffn_conv_b": jnp.stack([g["conv_b"] for g in g_ffn]),
    }
    conv_g = {"ssd_conv_w": jnp.stack([g["conv_w"] for g in g_ssd]), "ffn_conv_w": jnp.stack([g["conv_w"] for g in g_ffn])}
    extra = [conv_g[n] for n in CONV] + [loss_part]
    small_pack, small_layout = _pack([small_g[n] for n in SMALL] + extra, 8)
    small_all = _all_gather(small_pack, "gather_small_grads")
    zeros_like = [jnp.zeros(a.shape, F32) for a in extra]
    sw, _ = _pack([w[n] for n in SMALL] + zeros_like, 8)
    sm, _ = _pack([m[n] for n in SMALL] + zeros_like, 8)
    sv, _ = _pack([v[n] for n in SMALL] + [jnp.ones(a.shape, F32) for a in extra], 8)
    small_out = _adamw(small_all, sw, sm, sv, "adamw_replicated")
    small_res = [_unpack(o, small_layout) for o in small_out]
    summed = small_res[0]
    loss = summed[-1][0, 0]
    conv_shard_g = []
    for n, gsum in zip(CONV, summed[len(SMALL):len(SMALL) + len(CONV)]):
        ns = w[n].shape[-1]
        conv_shard_g.append(lax.dynamic_slice_in_dim(gsum, dev * ns, ns, axis=2))
    cg, conv_sh_layout = _pack(conv_shard_g, 8)
    cw, _ = _pack([w[n] for n in CONV], 8)
    cm_, _ = _pack([m[n] for n in CONV], 8)
    cv, _ = _pack([v[n] for n in CONV], 8)
    conv_out = _adamw(cg.reshape(1, *cg.shape), cw, cm_, cv, "adamw_conv_taps")
    conv_res = [dict(zip(CONV, _unpack(o, conv_sh_layout))) for o in conv_out]

    def pick(kind, n):
        if n in BIG:
            return big_res[kind][n]
        if n in CONV:
            return conv_res[kind][n]
        return small_res[kind][SMALL.index(n)]

    outs = [loss, grad_x]
    for kind in range(4):
        outs += [pick(kind, n) for n in WEIGHTS]
    return tuple(outs)


def kernel(x, mix_norm, ffn_norm, final_norm, ssd_w_in, ssd_conv_w, ssd_conv_b, ssd_dt_bias, ssd_a_log, ssd_d, ssd_norm, ssd_w_out, sb_w_qkv, sb_w_out, ffn_w_in, ffn_conv_w, ffn_conv_b, ffn_w_out, loss_target, m_mix_norm, m_ffn_norm, m_final_norm, m_ssd_w_in, m_ssd_conv_w, m_ssd_conv_b, m_ssd_dt_bias, m_ssd_a_log, m_ssd_d, m_ssd_norm, m_ssd_w_out, m_sb_w_qkv, m_sb_w_out, m_ffn_w_in, m_ffn_conv_w, m_ffn_conv_b, m_ffn_w_out, v_mix_norm, v_ffn_norm, v_final_norm, v_ssd_w_in, v_ssd_conv_w, v_ssd_conv_b, v_ssd_dt_bias, v_ssd_a_log, v_ssd_d, v_ssd_norm, v_ssd_w_out, v_sb_w_qkv, v_sb_w_out, v_ffn_w_in, v_ffn_conv_w, v_ffn_conv_b, v_ffn_w_out):
    w = dict(mix_norm=mix_norm, ffn_norm=ffn_norm, final_norm=final_norm, ssd_w_in=ssd_w_in, ssd_conv_w=ssd_conv_w,
             ssd_conv_b=ssd_conv_b, ssd_dt_bias=ssd_dt_bias, ssd_a_log=ssd_a_log, ssd_d=ssd_d, ssd_norm=ssd_norm,
             ssd_w_out=ssd_w_out, sb_w_qkv=sb_w_qkv, sb_w_out=sb_w_out, ffn_w_in=ffn_w_in, ffn_conv_w=ffn_conv_w,
             ffn_conv_b=ffn_conv_b, ffn_w_out=ffn_w_out)
    m = dict(mix_norm=m_mix_norm, ffn_norm=m_ffn_norm, final_norm=m_final_norm, ssd_w_in=m_ssd_w_in, ssd_conv_w=m_ssd_conv_w,
             ssd_conv_b=m_ssd_conv_b, ssd_dt_bias=m_ssd_dt_bias, ssd_a_log=m_ssd_a_log, ssd_d=m_ssd_d, ssd_norm=m_ssd_norm,
             ssd_w_out=m_ssd_w_out, sb_w_qkv=m_sb_w_qkv, sb_w_out=m_sb_w_out, ffn_w_in=m_ffn_w_in, ffn_conv_w=m_ffn_conv_w,
             ffn_conv_b=m_ffn_conv_b, ffn_w_out=m_ffn_w_out)
    v = dict(mix_norm=v_mix_norm, ffn_norm=v_ffn_norm, final_norm=v_final_norm, ssd_w_in=v_ssd_w_in, ssd_conv_w=v_ssd_conv_w,
             ssd_conv_b=v_ssd_conv_b, ssd_dt_bias=v_ssd_dt_bias, ssd_a_log=v_ssd_a_log, ssd_d=v_ssd_d, ssd_norm=v_ssd_norm,
             ssd_w_out=v_ssd_w_out, sb_w_qkv=v_sb_w_qkv, sb_w_out=v_sb_w_out, ffn_w_in=v_ffn_w_in, ffn_conv_w=v_ffn_conv_w,
             ffn_conv_b=v_ffn_conv_b, ffn_w_out=v_ffn_w_out)
    return _step(x, loss_target, w, m, v)
```

```python
import functools

import jax
import jax.numpy as jnp
import numpy as np
from jax import lax
from jax.experimental import pallas as pl
from jax.experimental.pallas import tpu as pltpu

F32 = jnp.float32
MXU_DTYPE = jnp.bfloat16
MESH_ID = pl.DeviceIdType.MESH
N_DEV = 8

NORM_EPS = 1e-6
D_MODEL = 1024
DEPTH = 4
SSD_D_INNER = 2048
SSD_HEADS = 32
SSD_HEAD_DIM = 64
SSD_GROUPS = 8
SSD_STATE = 128
SSD_CONV = 4
SSD_CHUNK = 128
SSD_CONV_DIM = SSD_D_INNER + 2 * SSD_GROUPS * SSD_STATE
SSD_IN_DIM = SSD_D_INNER + SSD_CONV_DIM + SSD_HEADS
LANES = 128
SSD_IN_PAD = SSD_D_INNER + SSD_CONV_DIM + LANES
SB_HEADS = 16
SB_HEAD_DIM = 64
SB_TILE = 256
SB_SCALE = SB_HEAD_DIM ** -0.5
FFN_D_FF = 2816
FFN_CONV = 3
PACK_W = 1024

ADAM_LR = 0.001
ADAM_B1 = 0.9
ADAM_B2 = 0.999
ADAM_EPS = 1e-08
ADAM_WD = 0.01
ADAM_STEP = 10

VMEM_LIMIT_BYTES = 56 * 1024 * 1024


def _cp(*sem):
    return pltpu.CompilerParams(dimension_semantics=sem, vmem_limit_bytes=VMEM_LIMIT_BYTES)


def _iota(shape, dim):
    return lax.broadcasted_iota(jnp.int32, shape, dim)


def _sigmoid(x):
    return 1.0 / (1.0 + jnp.exp(-x))


def _mm(a, b):
    return lax.dot_general(a, b, (((1,), (0,)), ((), ())), preferred_element_type=F32)


def _mm_nt(a, b):
    return lax.dot_general(a, b, (((1,), (1,)), ((), ())), preferred_element_type=F32)


def _mm_tn(a, b):
    return lax.dot_general(a, b, (((0,), (0,)), ((), ())), preferred_element_type=F32)


def _split(x):
    hi = x.astype(MXU_DTYPE)
    lo = (x - hi.astype(F32)).astype(MXU_DTYPE)
    return hi, lo


def _mm_exact_rhs(x, m):
    hi, lo = _split(x)
    return _mm(jnp.concatenate([hi, lo], axis=1), jnp.concatenate([m, m], axis=0))


def _mm_exact_lhs(m, x):
    hi, lo = _split(x)
    return _mm(jnp.concatenate([m, m], axis=1), jnp.concatenate([hi, lo], axis=0))


def _my_place():
    return lax.axis_index("x"), lax.axis_index("y"), lax.axis_index("c")


def _all_gather(shard, name):
    r, c_ = shard.shape

    def body(x_ref, out_ref, send_sems, recv_sems, local_sem):
        x, y, c = _my_place()
        me, sibling = (x, y, c), (x, y, 1 - c)
        chips = [(1 - x, y), (x, 1 - y), (1 - x, 1 - y)]

        def slot(px, py, pc):
            return out_ref.at[4 * px + 2 * py + pc]

        def copy(k, block, to, src=None):
            return pltpu.make_async_remote_copy(
                src_ref=slot(*block) if src is None else src, dst_ref=slot(*block),
                send_sem=send_sems.at[k], recv_sem=recv_sems.at[k], device_id=to, device_id_type=MESH_ID)

        mine = pltpu.make_async_copy(x_ref, slot(*me), local_sem)
        mine.start()
        first = [copy(0, me, sibling, src=x_ref)]
        first += [copy(1 + j, me, (*chip, c), src=x_ref) for j, chip in enumerate(chips)]
        for cp in first:
            cp.start()
        passed = [copy(4 + j, (*chip, c), sibling) for j, chip in enumerate(chips)]
        for j, chip in enumerate(chips):
            copy(1 + j, (*chip, c), me).wait_recv()
            passed[j].start()
        copy(0, sibling, me).wait_recv()
        for j, chip in enumerate(chips):
            copy(4 + j, (*chip, 1 - c), me).wait_recv()
        for cp in first + passed:
            cp.wait_send()
        mine.wait()

    return pl.pallas_call(
        body, name=name,
        out_shape=jax.ShapeDtypeStruct((N_DEV, r, c_), shard.dtype),
        in_specs=[pl.BlockSpec(memory_space=pl.ANY)],
        out_specs=pl.BlockSpec(memory_space=pl.ANY),
        scratch_shapes=[pltpu.SemaphoreType.DMA((7,)), pltpu.SemaphoreType.DMA((7,)), pltpu.SemaphoreType.DMA(())],
    )(shard)


def _swap_with_sibling(buf, name):
    def body(x_ref, out_ref, send_sem, recv_sem):
        x, y, c = _my_place()
        cp = pltpu.make_async_remote_copy(src_ref=x_ref, dst_ref=out_ref, send_sem=send_sem, recv_sem=recv_sem,
                                          device_id=(x, y, 1 - c), device_id_type=MESH_ID)
        cp.start()
        cp.wait()

    return pl.pallas_call(
        body, name=name, out_shape=jax.ShapeDtypeStruct(buf.shape, buf.dtype),
        in_specs=[pl.BlockSpec(memory_space=pl.ANY)], out_specs=pl.BlockSpec(memory_space=pl.ANY),
        scratch_shapes=[pltpu.SemaphoreType.DMA(()), pltpu.SemaphoreType.DMA(())],
    )(buf)


def _exchange_chips(parts, name):
    def body(p_ref, out_ref, send_sems, recv_sems, local_sem):
        x, y, c = _my_place()
        my_q = 2 * x + y
        chips = [(1 - x, y), (x, 1 - y), (1 - x, 1 - y)]
        local = pltpu.make_async_copy(p_ref.at[my_q], out_ref.at[my_q], local_sem)
        local.start()

        def copy(k, px, py):
            return pltpu.make_async_remote_copy(
                src_ref=p_ref.at[2 * px + py], dst_ref=out_ref.at[my_q],
                send_sem=send_sems.at[k], recv_sem=recv_sems.at[k], device_id=(px, py, c), device_id_type=MESH_ID)

        def landing(k, px, py):
            return pltpu.make_async_remote_copy(
                src_ref=p_ref.at[my_q], dst_ref=out_ref.at[2 * px + py],
                send_sem=send_sems.at[k], recv_sem=recv_sems.at[k], device_id=(px, py, c), device_id_type=MESH_ID)

        sends = [copy(k, px, py) for k, (px, py) in enumerate(chips)]
        for cp in sends:
            cp.start()
        for k, (px, py) in enumerate(chips):
            landing(k, px, py).wait_recv()
        for cp in sends:
            cp.wait_send()
        local.wait()

    return pl.pallas_call(
        body, name=name, out_shape=jax.ShapeDtypeStruct(parts.shape, parts.dtype),
        in_specs=[pl.BlockSpec(memory_space=pl.ANY)], out_specs=pl.BlockSpec(memory_space=pl.ANY),
        scratch_shapes=[pltpu.SemaphoreType.DMA((3,)), pltpu.SemaphoreType.DMA((3,)), pltpu.SemaphoreType.DMA(())],
    )(parts)


def _matmul(a, b, mode, out_dtype, name, add=None, tm=512, tn=512, tk=512):
    if mode == "nn":
        (m, k), (k2, n) = a.shape, b.shape
    elif mode == "nt":
        (m, k), (n, k2) = a.shape, b.shape
    else:
        (k, m), (k2, n) = a.shape, b.shape
    assert k == k2, (a.shape, b.shape, mode)
    tm, tn, tk = min(tm, m), min(tn, n), min(tk, k)
    assert m % tm == 0 and n % tn == 0 and k % tk == 0, (m, n, k, tm, tn, tk)
    nk = k // tk
    mm = {"nn": _mm, "nt": _mm_nt, "tn": _mm_tn}[mode]

    def body(*refs):
        if add is None:
            a_ref, b_ref, o_ref, acc_ref = refs
        else:
            a_ref, b_ref, add_ref, o_ref, acc_ref = refs
        kk = pl.program_id(2)

        @pl.when(kk == 0)
        def _():
            acc_ref[...] = jnp.zeros_like(acc_ref)

        acc_ref[...] += mm(a_ref[...].astype(MXU_DTYPE), b_ref[...].astype(MXU_DTYPE))

        @pl.when(kk == nk - 1)
        def _():
            res = acc_ref[...]
            if add is not None:
                res = res + add_ref[...]
            o_ref[...] = res.astype(o_ref.dtype)

    a_spec = {"nn": pl.BlockSpec((tm, tk), lambda i, j, kk: (i, kk)),
              "nt": pl.BlockSpec((tm, tk), lambda i, j, kk: (i, kk)),
              "tn": pl.BlockSpec((tk, tm), lambda i, j, kk: (kk, i))}[mode]
    b_spec = {"nn": pl.BlockSpec((tk, tn), lambda i, j, kk: (kk, j)),
              "nt": pl.BlockSpec((tn, tk), lambda i, j, kk: (j, kk)),
              "tn": pl.BlockSpec((tk, tn), lambda i, j, kk: (kk, j))}[mode]
    o_spec = pl.BlockSpec((tm, tn), lambda i, j, kk: (i, j))
    in_specs, args = [a_spec, b_spec], [a, b]
    if add is not None:
        in_specs.append(o_spec)
        args.append(add)
    return pl.pallas_call(
        body, name=name, grid=(m // tm, n // tn, nk), in_specs=in_specs, out_specs=o_spec,
        out_shape=jax.ShapeDtypeStruct((m, n), out_dtype),
        scratch_shapes=[pltpu.VMEM((tm, tn), F32)],
        compiler_params=_cp("parallel", "parallel", "arbitrary"),
    )(*args)


def _rmsnorm(x, g, name):
    t, d = x.shape
    tm = min(512, t)

    def body(x_ref, g_ref, o_ref):
        xv = x_ref[...]
        r = lax.rsqrt(jnp.mean(xv * xv, axis=-1, keepdims=True) + NORM_EPS)
        o_ref[...] = (xv * r * g_ref[...]).astype(o_ref.dtype)

    return pl.pallas_call(
        body, name=name, grid=(t // tm,),
        in_specs=[pl.BlockSpec((tm, d), lambda i: (i, 0)), pl.BlockSpec((1, d), lambda i: (0, 0))],
        out_specs=pl.BlockSpec((tm, d), lambda i: (i, 0)),
        out_shape=jax.ShapeDtypeStruct((t, d), MXU_DTYPE), compiler_params=_cp("parallel"),
    )(x, g.reshape(1, d))


def _rmsnorm_bwd(x, g, dh, dres, name):
    t, d = x.shape
    tm = min(512, t)

    def body(x_ref, g_ref, dh_ref, dres_ref, dx_ref, dg_ref):
        @pl.when(pl.program_id(0) == 0)
        def _():
            dg_ref[...] = jnp.zeros_like(dg_ref)

        xv = x_ref[...]
        r = lax.rsqrt(jnp.mean(xv * xv, axis=-1, keepdims=True) + NORM_EPS)
        xn = xv * r
        dhv = dh_ref[...]
        u = dhv * g_ref[...]
        dx_ref[...] = dres_ref[...] + r * (u - xn * jnp.mean(u * xn, axis=-1, keepdims=True))
        dg_ref[...] += jnp.sum(dhv * xn, axis=0, keepdims=True)

    row = pl.BlockSpec((tm, d), lambda i: (i, 0))
    vec = pl.BlockSpec((1, d), lambda i: (0, 0))
    return pl.pallas_call(
        body, name=name, grid=(t // tm,), in_specs=[row, vec, row, row], out_specs=[row, vec],
        out_shape=[jax.ShapeDtypeStruct((t, d), F32), jax.ShapeDtypeStruct((1, d), F32)],
        compiler_params=_cp("arbitrary"),
    )(x, g.reshape(1, d), dh, dres)


def _final_norm_loss(x, g, target, name):
    t, d = x.shape
    tm = min(512, t)

    def body(x_ref, g_ref, t_ref, dx_ref, dg_ref, loss_ref):
        @pl.when(pl.program_id(0) == 0)
        def _():
            dg_ref[...] = jnp.zeros_like(dg_ref)
            loss_ref[...] = jnp.zeros_like(loss_ref)

        xv = x_ref[...]
        gv = g_ref[...]
        r = lax.rsqrt(jnp.mean(xv * xv, axis=-1, keepdims=True) + NORM_EPS)
        xn = xv * r
        err = xn * gv - t_ref[...]
        per_tok = jnp.mean(err * err, axis=-1, keepdims=True)
        loss_ref[...] += jnp.broadcast_to(0.5 * jnp.sum(per_tok, axis=0, keepdims=True), loss_ref.shape)
        dy = err * (1.0 / d)
        u = dy * gv
        dx_ref[...] = r * (u - xn * jnp.mean(u * xn, axis=-1, keepdims=True))
        dg_ref[...] += jnp.sum(dy * xn, axis=0, keepdims=True)

    row = pl.BlockSpec((tm, d), lambda i: (i, 0))
    vec = pl.BlockSpec((1, d), lambda i: (0, 0))
    return pl.pallas_call(
        body, name=name, grid=(t // tm,), in_specs=[row, vec, row],
        out_specs=[row, vec, pl.BlockSpec((1, LANES), lambda i: (0, 0))],
        out_shape=[jax.ShapeDtypeStruct((t, d), F32), jax.ShapeDtypeStruct((1, d), F32),
                   jax.ShapeDtypeStruct((1, LANES), F32)],
        compiler_params=_cp("arbitrary"),
    )(x, g.reshape(1, d), target)


def _shift_down(p, s, row):
    return jnp.where(row >= s, pltpu.roll(p, s, axis=0), 0.0)


def _shift_up(p, s, row):
    n = p.shape[0]
    return jnp.where(row < n - s, pltpu.roll(p, n - s, axis=0), 0.0)


def _conv_pre(p, w_ref, b_ref, row):
    width = w_ref.shape[0]
    u = b_ref[...] + w_ref[width - 1:width, :] * p
    for s in range(1, width):
        u = u + w_ref[width - 1 - s:width - s, :] * _shift_down(p, s, row)
    return u


def _conv_transpose(du, w_ref, row):
    width = w_ref.shape[0]
    dp = w_ref[width - 1:width, :] * du
    for s in range(1, width):
        dp = dp + w_ref[width - 1 - s:width - s, :] * _shift_up(du, s, row)
    return dp


def _conv_wgrad(du, p, dw_ref, db_ref, row):
    width = dw_ref.shape[0]
    db_ref[...] = jnp.sum(du, axis=0, keepdims=True)
    dw_ref[width - 1:width, :] = jnp.sum(du * p, axis=0, keepdims=True)
    for s in range(1, width):
        dw_ref[width - 1 - s:width - s, :] = jnp.sum(du * _shift_down(p, s, row), axis=0, keepdims=True)


CONV_COLS = 128


def _ssd_conv_fwd(proj, w, b, name):
    t = proj.shape[0]
    cb = CONV_COLS
    off = SSD_D_INNER // cb

    def body(p_ref, w_ref, b_ref, o_ref):
        p = p_ref[...]
        row = _iota(p.shape, 0)
        u = _conv_pre(p, w_ref, b_ref, row)
        o_ref[...] = u * _sigmoid(u)

    return pl.pallas_call(
        body, name=name, grid=(SSD_CONV_DIM // cb,),
        in_specs=[pl.BlockSpec((t, cb), lambda j: (0, j + off)), pl.BlockSpec((SSD_CONV, cb), lambda j: (0, j)),
                  pl.BlockSpec((1, cb), lambda j: (0, j))],
        out_specs=pl.BlockSpec((t, cb), lambda j: (0, j)),
        out_shape=jax.ShapeDtypeStruct((t, SSD_CONV_DIM), F32), compiler_params=_cp("parallel"),
    )(proj, w, b.reshape(1, -1))


def _ssd_conv_bwd(proj, w, b, dact, name):
    t = proj.shape[0]
    cb = CONV_COLS
    off = SSD_D_INNER // cb

    def body(p_ref, w_ref, b_ref, da_ref, dp_ref, dw_ref, db_ref):
        p = p_ref[...]
        row = _iota(p.shape, 0)
        u = _conv_pre(p, w_ref, b_ref, row)
        sg = _sigmoid(u)
        du = da_ref[...] * (sg * (1.0 + u * (1.0 - sg)))
        dp_ref[...] = _conv_transpose(du, w_ref, row).astype(dp_ref.dtype)
        _conv_wgrad(du, p, dw_ref, db_ref, row)

    col = pl.BlockSpec((t, cb), lambda j: (0, j))
    wspec = pl.BlockSpec((SSD_CONV, cb), lambda j: (0, j))
    bspec = pl.BlockSpec((1, cb), lambda j: (0, j))
    return pl.pallas_call(
        body, name=name, grid=(SSD_CONV_DIM // cb,),
        in_specs=[pl.BlockSpec((t, cb), lambda j: (0, j + off)), wspec, bspec, col],
        out_specs=[col, wspec, bspec],
        out_shape=[jax.ShapeDtypeStruct((t, SSD_CONV_DIM), MXU_DTYPE), jax.ShapeDtypeStruct((SSD_CONV, SSD_CONV_DIM), F32),
                   jax.ShapeDtypeStruct((1, SSD_CONV_DIM), F32)],
        compiler_params=_cp("parallel"),
    )(proj, w, b.reshape(1, -1), dact)


def _ffn_conv_fwd(proj, w, b, name):
    t = proj.shape[0]
    cb = CONV_COLS
    nb = FFN_D_FF // cb

    def body(pg_ref, pu_ref, wg_ref, wu_ref, bg_ref, bu_ref, o_ref):
        row = _iota(pg_ref.shape, 0)
        ug = _conv_pre(pg_ref[...], wg_ref, bg_ref, row)
        uu = _conv_pre(pu_ref[...], wu_ref, bu_ref, row)
        o_ref[...] = (ug * _sigmoid(ug) * uu).astype(o_ref.dtype)

    gcol = pl.BlockSpec((t, cb), lambda j: (0, j))
    ucol = pl.BlockSpec((t, cb), lambda j: (0, j + nb))
    b2 = b.reshape(1, -1)
    return pl.pallas_call(
        body, name=name, grid=(nb,),
        in_specs=[gcol, ucol, pl.BlockSpec((FFN_CONV, cb), lambda j: (0, j)), pl.BlockSpec((FFN_CONV, cb), lambda j: (0, j + nb)),
                  pl.BlockSpec((1, cb), lambda j: (0, j)), pl.BlockSpec((1, cb), lambda j: (0, j + nb))],
        out_specs=gcol, out_shape=jax.ShapeDtypeStruct((t, FFN_D_FF), MXU_DTYPE), compiler_params=_cp("parallel"),
    )(proj, proj, w, w, b2, b2)


def _ffn_conv_bwd(proj, w, b, dact, name):
    t = proj.shape[0]
    cb = CONV_COLS
    nb = FFN_D_FF // cb

    def body(pg_ref, pu_ref, wg_ref, wu_ref, bg_ref, bu_ref, da_ref,
             dpg_ref, dpu_ref, dwg_ref, dwu_ref, dbg_ref, dbu_ref):
        pg, pu = pg_ref[...], pu_ref[...]
        row = _iota(pg.shape, 0)
        ug = _conv_pre(pg, wg_ref, bg_ref, row)
        uu = _conv_pre(pu, wu_ref, bu_ref, row)
        sg = _sigmoid(ug)
        da = da_ref[...]
        dug = da * uu * (sg * (1.0 + ug * (1.0 - sg)))
        duu = da * (ug * sg)
        dpg_ref[...] = _conv_transpose(dug, wg_ref, row).astype(dpg_ref.dtype)
        dpu_ref[...] = _conv_transpose(duu, wu_ref, row).astype(dpu_ref.dtype)
        _conv_wgrad(dug, pg, dwg_ref, dbg_ref, row)
        _conv_wgrad(duu, pu, dwu_ref, dbu_ref, row)

    gcol = pl.BlockSpec((t, cb), lambda j: (0, j))
    ucol = pl.BlockSpec((t, cb), lambda j: (0, j + nb))
    wg = pl.BlockSpec((FFN_CONV, cb), lambda j: (0, j))
    wu = pl.BlockSpec((FFN_CONV, cb), lambda j: (0, j + nb))
    bg = pl.BlockSpec((1, cb), lambda j: (0, j))
    bu = pl.BlockSpec((1, cb), lambda j: (0, j + nb))
    b2 = b.reshape(1, -1)
    half = jax.ShapeDtypeStruct((t, FFN_D_FF), MXU_DTYPE)
    return pl.pallas_call(
        body, name=name, grid=(nb,),
        in_specs=[gcol, ucol, wg, wu, bg, bu, gcol],
        out_specs=[gcol, gcol, wg, wg, bg, bg],
        out_shape=[half, half, jax.ShapeDtypeStruct((FFN_CONV, FFN_D_FF), F32), jax.ShapeDtypeStruct((FFN_CONV, FFN_D_FF), F32),
                   jax.ShapeDtypeStruct((1, FFN_D_FF), F32), jax.ShapeDtypeStruct((1, FFN_D_FF), F32)],
        compiler_params=_cp("parallel"),
    )(proj, proj, w, w, b2, b2, dact)


SSD_ROWS = 128
DT_COL = (SSD_D_INNER + SSD_CONV_DIM) // LANES


def _head_expand():
    h = np.arange(LANES)[:, None]
    col = np.arange(SSD_D_INNER)[None, :]
    return jnp.asarray((col // SSD_HEAD_DIM == h), MXU_DTYPE)


def _chunk_tri(n, lower):
    t = _iota((n, n), 0)
    s = _iota((n, n), 1)
    shift = SSD_CHUNK.bit_length() - 1
    same = jnp.right_shift(t, shift) == jnp.right_shift(s, shift)
    tri = (s <= t) if lower else (s >= t)
    return jnp.where(same & tri, 1.0, 0.0).astype(MXU_DTYPE)


def _softplus(x):
    return jnp.maximum(x, 0.0) + jnp.log(1.0 + jnp.exp(-jnp.abs(x)))


def _ssd_dt_fwd(proj, act, dt_bias, a_neg, expand, name):
    t = proj.shape[0]
    tm = min(SSD_ROWS, t)

    def body(raw_ref, xs_ref, bias_ref, a_ref, e_ref, xdt_ref, dt_ref, acum_ref):
        lane = _iota((tm, LANES), 1)
        dt = jnp.where(lane < SSD_HEADS, _softplus(raw_ref[...] + bias_ref[...]), 0.0)
        dt_ref[...] = dt
        xdt_ref[...] = xs_ref[...] * _mm_exact_rhs(dt, e_ref[...])
        acum_ref[...] = _mm_exact_lhs(_chunk_tri(tm, True), a_ref[...] * dt)

    vec = pl.BlockSpec((1, LANES), lambda i: (0, 0))
    return pl.pallas_call(
        body, name=name, grid=(t // tm,),
        in_specs=[pl.BlockSpec((tm, LANES), lambda i: (i, DT_COL)), pl.BlockSpec((tm, SSD_D_INNER), lambda i: (i, 0)),
                  vec, vec, pl.BlockSpec((LANES, SSD_D_INNER), lambda i: (0, 0))],
        out_specs=[pl.BlockSpec((tm, SSD_D_INNER), lambda i: (i, 0)), pl.BlockSpec((tm, LANES), lambda i: (i, 0)),
                   pl.BlockSpec((tm, LANES), lambda i: (i, 0))],
        out_shape=[jax.ShapeDtypeStruct((t, SSD_D_INNER), F32), jax.ShapeDtypeStruct((t, LANES), F32),
                   jax.ShapeDtypeStruct((t, LANES), F32)],
        compiler_params=_cp("parallel"),
    )(proj, act, dt_bias, a_neg, expand)


def _ssd_dt_bwd(proj, act, dt, dxdt, dyy, dacum, dt_bias, a_neg, d_exp, expand, expand_t, name):
    t = proj.shape[0]
    tm = min(SSD_ROWS, t)

    def body(raw_ref, xs_ref, dt_ref, dxdt_ref, dyy_ref, dac_ref, bias_ref, a_ref, dsk_ref, e_ref, et_ref,
             dxs_ref, draw_ref, da_ref, dbias_ref, dd_ref):
        @pl.when(pl.program_id(0) == 0)
        def _():
            da_ref[...] = jnp.zeros_like(da_ref)
            dbias_ref[...] = jnp.zeros_like(dbias_ref)
            dd_ref[...] = jnp.zeros_like(dd_ref)

        lane = _iota((tm, LANES), 1)
        xs, dt, dxdt, dyy = xs_ref[...], dt_ref[...], dxdt_ref[...], dyy_ref[...]
        dxs_ref[...] = dxdt * _mm_exact_rhs(dt, e_ref[...]) + dsk_ref[...] * dyy
        dd_ref[...] += jnp.sum(dyy * xs, axis=0, keepdims=True)
        ddt = _mm_exact_rhs(dxdt * xs, et_ref[...])
        da = _mm_exact_lhs(_chunk_tri(tm, False), dac_ref[...])
        ddt = ddt + da * a_ref[...]
        da_ref[...] += jnp.sum(da * dt, axis=0, keepdims=True)
        draw = jnp.where(lane < SSD_HEADS, ddt * _sigmoid(raw_ref[...] + bias_ref[...]), 0.0)
        dbias_ref[...] += jnp.sum(draw, axis=0, keepdims=True)
        draw_ref[...] = draw.astype(draw_ref.dtype)

    wide = pl.BlockSpec((tm, SSD_D_INNER), lambda i: (i, 0))
    thin = pl.BlockSpec((tm, LANES), lambda i: (i, 0))
    vec = pl.BlockSpec((1, LANES), lambda i: (0, 0))
    wvec = pl.BlockSpec((1, SSD_D_INNER), lambda i: (0, 0))
    return pl.pallas_call(
        body, name=name, grid=(t // tm,),
        in_specs=[pl.BlockSpec((tm, LANES), lambda i: (i, DT_COL)), wide, thin, wide, wide, thin, vec, vec, wvec,
                  pl.BlockSpec((LANES, SSD_D_INNER), lambda i: (0, 0)), pl.BlockSpec((SSD_D_INNER, LANES), lambda i: (0, 0))],
        out_specs=[wide, thin, vec, vec, wvec],
        out_shape=[jax.ShapeDtypeStruct((t, SSD_D_INNER), F32), jax.ShapeDtypeStruct((t, LANES), MXU_DTYPE),
                   jax.ShapeDtypeStruct((1, LANES), F32), jax.ShapeDtypeStruct((1, LANES), F32),
                   jax.ShapeDtypeStruct((1, SSD_D_INNER), F32)],
        compiler_params=_cp("arbitrary"),
    )(proj, act, dt, dxdt, dyy, dacum, dt_bias, a_neg, d_exp, expand, expand_t)


SSD_PAIR = 2 * SSD_HEAD_DIM
HEADS_PER_GROUP = SSD_HEADS // SSD_GROUPS
GROUP_COLS = HEADS_PER_GROUP * SSD_HEAD_DIM
B_COL0 = SSD_D_INNER // SSD_STATE
C_COL0 = (SSD_D_INNER + SSD_GROUPS * SSD_STATE) // SSD_STATE


def _pair_cols(vals, h0, lo_mask):
    return jnp.where(lo_mask, vals[:, h0:h0 + 1], vals[:, h0 + 1:h0 + 2])


def _ssd_scan_fwd(xdt, act, acum_g, acum_gt, name):
    t = xdt.shape[0]
    nc = t // SSD_CHUNK
    ln = SSD_CHUNK

    def body(x_ref, b_ref, c_ref, ac_ref, act_ref, y_ref, sst_ref, state):
        @pl.when(pl.program_id(1) == 0)
        def _():
            state[...] = jnp.zeros_like(state)

        bm = b_ref[...].astype(MXU_DTYPE)
        cm = c_ref[...].astype(MXU_DTYPE)
        cb = _mm_nt(cm, bm)
        ac, act_ = ac_ref[0], act_ref[0]
        causal = _iota((ln, ln), 1) <= _iota((ln, ln), 0)
        lo_mask = _iota((ln, SSD_PAIR), 1) < SSD_HEAD_DIM
        lo_rows = _iota((SSD_PAIR, SSD_STATE), 0) < SSD_HEAD_DIM
        sst_ref[0, 0] = state[...]
        last = ac[ln - 1:ln, :]
        e_ac = jnp.exp(ac)
        w_all = jnp.exp(last - ac)
        e_last = jnp.exp(last)
        for pr in range(2):
            cols = slice(pr * SSD_PAIR, (pr + 1) * SSD_PAIR)
            xp = x_ref[:, cols]
            sp = state[cols, :]
            ydiag = jnp.zeros((ln, SSD_PAIR), F32)
            for hh in range(2):
                h = 2 * pr + hh
                seg = ac[:, h:h + 1] - act_[h:h + 1, :]
                dec = jnp.exp(jnp.where(causal, seg, -1e30))
                mask = lo_mask if hh == 0 else jnp.logical_not(lo_mask)
                ydiag = ydiag + _mm((cb * dec).astype(MXU_DTYPE), jnp.where(mask, xp, 0.0).astype(MXU_DTYPE))
            yoff = _mm_nt(cm, sp.astype(MXU_DTYPE)) * _pair_cols(e_ac, 2 * pr, lo_mask)
            y_ref[:, cols] = ydiag + yoff
            xw = (xp * _pair_cols(w_all, 2 * pr, lo_mask)).astype(MXU_DTYPE)
            el = jnp.where(lo_rows, e_last[:, 2 * pr:2 * pr + 1], e_last[:, 2 * pr + 1:2 * pr + 2])
            state[cols, :] = sp * el + _mm_tn(xw, bm)

    return pl.pallas_call(
        body, name=name, grid=(SSD_GROUPS, nc),
        in_specs=[pl.BlockSpec((ln, GROUP_COLS), lambda g, c: (c, g)),
                  pl.BlockSpec((ln, SSD_STATE), lambda g, c: (c, B_COL0 + g)),
                  pl.BlockSpec((ln, SSD_STATE), lambda g, c: (c, C_COL0 + g)),
                  pl.BlockSpec((1, ln, HEADS_PER_GROUP), lambda g, c: (g, c, 0)),
                  pl.BlockSpec((1, HEADS_PER_GROUP, ln), lambda g, c: (g, 0, c))],
        out_specs=[pl.BlockSpec((ln, GROUP_COLS), lambda g, c: (c, g)),
                   pl.BlockSpec((1, 1, GROUP_COLS, SSD_STATE), lambda g, c: (c, g, 0, 0))],
        out_shape=[jax.ShapeDtypeStruct((t, SSD_D_INNER), F32),
                   jax.ShapeDtypeStruct((nc, SSD_GROUPS, GROUP_COLS, SSD_STATE), F32)],
        scratch_shapes=[pltpu.VMEM((GROUP_COLS, SSD_STATE), F32)],
        compiler_params=_cp("parallel", "arbitrary"),
    )(xdt, act, act, acum_g, acum_gt)


def _ssd_scan_bwd(xdt, act, acum_g, acum_gt, states, dy, name):
    t = xdt.shape[0]
    nc = t // SSD_CHUNK
    ln = SSD_CHUNK

    def body(x_ref, b_ref, c_ref, ac_ref, act_ref, sst_ref, dy_ref, dx_ref, db_ref, dc_ref, dacol_ref, darow_ref, dstate):
        @pl.when(pl.program_id(1) == 0)
        def _():
            dstate[...] = jnp.zeros_like(dstate)

        bm = b_ref[...].astype(MXU_DTYPE)
        cm = c_ref[...].astype(MXU_DTYPE)
        cb = _mm_nt(cm, bm)
        ac, act_ = ac_ref[0], act_ref[0]
        causal = _iota((ln, ln), 1) <= _iota((ln, ln), 0)
        lo_mask = _iota((ln, SSD_PAIR), 1) < SSD_HEAD_DIM
        lo_rows = _iota((SSD_PAIR, SSD_STATE), 0) < SSD_HEAD_DIM
        lane4 = _iota((ln, HEADS_PER_GROUP), 1)
        sub4 = _iota((HEADS_PER_GROUP, ln), 0)
        is_last = _iota((ln, 1), 0) == ln - 1
        last = ac[ln - 1:ln, :]
        e_ac = jnp.exp(ac)
        w_all = jnp.exp(last - ac)
        e_last = jnp.exp(last)
        dcb = jnp.zeros((ln, ln), F32)
        dc_acc = jnp.zeros((ln, SSD_STATE), F32)
        db_acc = jnp.zeros((ln, SSD_STATE), F32)
        dacol = jnp.zeros((ln, HEADS_PER_GROUP), F32)
        darow = jnp.zeros((HEADS_PER_GROUP, ln), F32)
        for pr in range(2):
            cols = slice(pr * SSD_PAIR, (pr + 1) * SSD_PAIR)
            xp = x_ref[:, cols]
            dyp = dy_ref[:, cols]
            sp = sst_ref[0, 0, cols, :]
            dsp = dstate[cols, :]
            ea = _pair_cols(e_ac, 2 * pr, lo_mask)
            w = _pair_cols(w_all, 2 * pr, lo_mask)
            dye = (dyp * ea).astype(MXU_DTYPE)
            dx_state = w * _mm_nt(bm, dsp.astype(MXU_DTYPE))
            yoff = _mm_nt(cm, sp.astype(MXU_DTYPE)) * ea
            dxp = dx_state
            for hh in range(2):
                h = 2 * pr + hh
                mask = lo_mask if hh == 0 else jnp.logical_not(lo_mask)
                rmask = lo_rows if hh == 0 else jnp.logical_not(lo_rows)
                seg = ac[:, h:h + 1] - act_[h:h + 1, :]
                dec = jnp.exp(jnp.where(causal, seg, -1e30))
                m = cb * dec
                dym = jnp.where(mask, dyp, 0.0).astype(MXU_DTYPE)
                xm = jnp.where(mask, xp, 0.0).astype(MXU_DTYPE)
                g = _mm_nt(dym, xm)
                dxp = dxp + _mm_tn(m.astype(MXU_DTYPE), dym)
                dcb = dcb + dec * g
                mg = m * g
                rs = jnp.sum(mg, axis=1, keepdims=True)
                cs = jnp.sum(mg, axis=0, keepdims=True)
                t_off = jnp.sum(jnp.where(mask, dyp * yoff, 0.0), axis=1, keepdims=True)
                q = jnp.sum(jnp.where(mask, xp * dx_state, 0.0), axis=1, keepdims=True)
                qsum = jnp.sum(q, axis=0, keepdims=True)
                ds_s = jnp.sum(jnp.sum(jnp.where(rmask, dsp * sp, 0.0), axis=1, keepdims=True), axis=0, keepdims=True)
                extra = qsum + e_last[:, h:h + 1] * ds_s
                col = rs + t_off - q + jnp.where(is_last, extra, 0.0)
                dacol = jnp.where(lane4 == h, col, dacol)
                darow = jnp.where(sub4 == h, -cs, darow)
            dx_ref[:, cols] = dxp
            dc_acc = dc_acc + _mm(dye, sp.astype(MXU_DTYPE))
            db_acc = db_acc + _mm((xp * w).astype(MXU_DTYPE), dsp.astype(MXU_DTYPE))
            el = jnp.where(lo_rows, e_last[:, 2 * pr:2 * pr + 1], e_last[:, 2 * pr + 1:2 * pr + 2])
            dstate[cols, :] = dsp * el + _mm_tn(dye, cm)
        dcbm = dcb.astype(MXU_DTYPE)
        dc_ref[...] = _mm(dcbm, bm) + dc_acc
        db_ref[...] = _mm_tn(dcbm, cm) + db_acc
        dacol_ref[0] = dacol
        darow_ref[0] = darow

    def rev(c):
        return nc - 1 - c

    grp = pl.BlockSpec((ln, GROUP_COLS), lambda g, c: (rev(c), g))
    return pl.pallas_call(
        body, name=name, grid=(SSD_GROUPS, nc),
        in_specs=[grp,
                  pl.BlockSpec((ln, SSD_STATE), lambda g, c: (rev(c), B_COL0 + g)),
                  pl.BlockSpec((ln, SSD_STATE), lambda g, c: (rev(c), C_COL0 + g)),
                  pl.BlockSpec((1, ln, HEADS_PER_GROUP), lambda g, c: (g, rev(c), 0)),
                  pl.BlockSpec((1, HEADS_PER_GROUP, ln), lambda g, c: (g, 0, rev(c))),
                  pl.BlockSpec((1, 1, GROUP_COLS, SSD_STATE), lambda g, c: (rev(c), g, 0, 0)),
                  grp],
        out_specs=[grp,
                   pl.BlockSpec((ln, SSD_STATE), lambda g, c: (rev(c), g)),
                   pl.BlockSpec((ln, SSD_STATE), lambda g, c: (rev(c), g)),
                   pl.BlockSpec((1, ln, HEADS_PER_GROUP), lambda g, c: (g, rev(c), 0)),
                   pl.BlockSpec((1, HEADS_PER_GROUP, ln), lambda g, c: (g, 0, rev(c)))],
        out_shape=[jax.ShapeDtypeStruct((t, SSD_D_INNER), F32),
                   jax.ShapeDtypeStruct((t, SSD_GROUPS * SSD_STATE), F32),
                   jax.ShapeDtypeStruct((t, SSD_GROUPS * SSD_STATE), F32),
                   jax.ShapeDtypeStruct((SSD_GROUPS, t, HEADS_PER_GROUP), F32),
                   jax.ShapeDtypeStruct((SSD_GROUPS, HEADS_PER_GROUP, t), F32)],
        scratch_shapes=[pltpu.VMEM((GROUP_COLS, SSD_STATE), F32)],
        compiler_params=_cp("parallel", "arbitrary"),
    )(xdt, act, act, acum_g, acum_gt, states, dy)


GN_ROWS = 128


def _gated_norm_parts(y_ref, xs_ref, z_ref, dsk_ref):
    yy = y_ref[...] + dsk_ref[...] * xs_ref[...]
    z = z_ref[...]
    sz = _sigmoid(z)
    silu = z * sz
    u = yy * silu
    r = lax.rsqrt(jnp.mean(u * u, axis=-1, keepdims=True) + NORM_EPS)
    return yy, z, sz, silu, u, r


def _gated_norm_fwd(y, act, proj, d_exp, g, name):
    t = y.shape[0]
    tm = min(GN_ROWS, t)

    def body(y_ref, xs_ref, z_ref, dsk_ref, g_ref, o_ref):
        _, _, _, _, u, r = _gated_norm_parts(y_ref, xs_ref, z_ref, dsk_ref)
        o_ref[...] = (u * r * g_ref[...]).astype(o_ref.dtype)

    wide = pl.BlockSpec((tm, SSD_D_INNER), lambda i: (i, 0))
    wvec = pl.BlockSpec((1, SSD_D_INNER), lambda i: (0, 0))
    return pl.pallas_call(
        body, name=name, grid=(t // tm,), in_specs=[wide, wide, wide, wvec, wvec], out_specs=wide,
        out_shape=jax.ShapeDtypeStruct((t, SSD_D_INNER), MXU_DTYPE), compiler_params=_cp("parallel"),
    )(y, act, proj, d_exp, g.reshape(1, -1))


def _gated_norm_bwd(y, act, proj, d_exp, g, dn, name):
    t = y.shape[0]
    tm = min(GN_ROWS, t)

    def body(y_ref, xs_ref, z_ref, dsk_ref, g_ref, dn_ref, dyy_ref, dz_ref, dg_ref):
        @pl.when(pl.program_id(0) == 0)
        def _():
            dg_ref[...] = jnp.zeros_like(dg_ref)

        yy, z, sz, silu, u, r = _gated_norm_parts(y_ref, xs_ref, z_ref, dsk_ref)
        un = u * r
        dn = dn_ref[...]
        v = dn * g_ref[...]
        du = r * (v - un * jnp.mean(v * un, axis=-1, keepdims=True))
        dg_ref[...] += jnp.sum(dn * un, axis=0, keepdims=True)
        dyy_ref[...] = du * silu
        dz_ref[...] = (du * yy * (sz * (1.0 + z * (1.0 - sz)))).astype(dz_ref.dtype)

    wide = pl.BlockSpec((tm, SSD_D_INNER), lambda i: (i, 0))
    wvec = pl.BlockSpec((1, SSD_D_INNER), lambda i: (0, 0))
    return pl.pallas_call(
        body, name=name, grid=(t // tm,), in_specs=[wide, wide, wide, wvec, wvec, wide], out_specs=[wide, wide, wvec],
        out_shape=[jax.ShapeDtypeStruct((t, SSD_D_INNER), F32), jax.ShapeDtypeStruct((t, SSD_D_INNER), MXU_DTYPE),
                   jax.ShapeDtypeStruct((1, SSD_D_INNER), F32)],
        compiler_params=_cp("arbitrary"),
    )(y, act, proj, d_exp, g.reshape(1, -1), dn)


SB_PAIRS = SB_HEADS // 2


def _kv_rows(j, bt, nt=1):
    return pl.ds(pl.multiple_of(j * bt, bt), nt * bt)


def _sb_tile_masks(bt):
    lane = _iota((bt, bt), 1)
    rowi = _iota((bt, bt), 0)
    return lane < rowi, (rowi >= lane).astype(MXU_DTYPE), (rowi <= lane).astype(MXU_DTYPE)


def _sb_scaled_heads(pair, scale):
    lane = _iota(pair.shape, 1)
    val = pair.astype(F32) * scale
    return [jnp.where(lane < SB_HEAD_DIM, val, 0.0).astype(pair.dtype), jnp.where(lane >= SB_HEAD_DIM, val, 0.0).astype(pair.dtype)]


def _sb_logits(qs, kb, bt, strict):
    nt = kb.shape[0] // bt
    full = [_mm_nt(q_head, kb) for q_head in qs]
    xs, nlfs = [], []
    for x in full:
        nlf = jnp.maximum(x, 0.0) + jnp.log(1.0 + jnp.exp(-jnp.abs(x)))
        xs.append([x[:, tt * bt:(tt + 1) * bt] for tt in range(nt)])
        tiles = [nlf[:, tt * bt:(tt + 1) * bt] for tt in range(nt)]
        if strict is not None:
            tiles[-1] = jnp.where(strict, tiles[-1], 0.0)
        nlfs.append(tiles)
    return xs, nlfs


def _sb_tails(nlf_tiles, from_j):
    tails, run = [None] * len(nlf_tiles), None
    for tt in reversed(range(len(nlf_tiles))):
        tail = _mm_exact_rhs(nlf_tiles[tt], from_j)
        tails[tt] = tail if run is None else tail + run
        run = tails[tt][:, 0:1]
    return tails


def _sb_heads(e_tiles, upto_j, pre):
    sums, run = [], pre
    for e in e_tiles:
        sums.append(_mm_exact_rhs(e, upto_j) + run)
        run = sums[-1][:, e.shape[1] - 1:e.shape[1]]
    return sums


def _sb_attention_fwd(qkv, name):
    t = qkv.shape[0]
    bt = min(SB_TILE, t)
    nq = t // bt

    def body(q_ref, k_ref, v_ref, o_ref, acc_ref):
        i = pl.program_id(1)
        strict, from_j, _ = _sb_tile_masks(bt)
        qs = _sb_scaled_heads(q_ref[...], SB_SCALE)
        acc_ref[...] = jnp.zeros_like(acc_ref)

        def block(j, nt, carries, diag):
            rows = _kv_rows(j, bt, nt)
            kb, vb = k_ref[rows, :], v_ref[rows, :]
            xs, nlfs = _sb_logits(qs, kb, bt, strict if diag else None)
            tails = [_sb_tails(nlfs[hh], from_j) for hh in range(2)]
            for hh in range(2):
                ws = [jnp.exp(xs[hh][tt] - tails[hh][tt] - carries[hh]) for tt in range(nt)]
                if diag:
                    ws[-1] = jnp.where(strict, ws[-1], 0.0)
                acc_ref[hh] += _mm(jnp.concatenate([w.astype(MXU_DTYPE) for w in ws], axis=1), vb)
            return tuple(carries[hh] + tails[hh][0][:, 0:1] for hh in range(2))

        zero = jnp.zeros((bt, 1), F32)
        carries = block(i, 1, (zero, zero), True)
        carries = lax.fori_loop(0, i // 2, lambda it, cr: block(i - 2 - 2 * it, 2, cr, False), carries)

        @pl.when(i % 2 == 1)
        def _():
            block(0, 1, carries, False)

        low = _iota((bt, 2 * SB_HEAD_DIM), 1) < SB_HEAD_DIM
        o_ref[...] = jnp.where(low, acc_ref[0], acc_ref[1]).astype(o_ref.dtype)

    lanes = 2 * SB_HEAD_DIM
    return pl.pallas_call(
        body, name=name, grid=(SB_PAIRS, nq),
        in_specs=[pl.BlockSpec((bt, lanes), lambda p, i: (i, p)),
                  pl.BlockSpec((t, lanes), lambda p, i: (0, SB_PAIRS + p)),
                  pl.BlockSpec((t, lanes), lambda p, i: (0, 2 * SB_PAIRS + p))],
        out_specs=pl.BlockSpec((bt, lanes), lambda p, i: (i, p)),
        out_shape=jax.ShapeDtypeStruct((t, D_MODEL), MXU_DTYPE),
        scratch_shapes=[pltpu.VMEM((2, bt, lanes), F32)],
        compiler_params=_cp("parallel", "parallel"),
    )(qkv, qkv, qkv)


def _sb_attention_bwd(qkv, do, name):
    t = qkv.shape[0]
    bt = min(SB_TILE, t)
    nq = t // bt
    lanes = 2 * SB_HEAD_DIM

    def body(q_ref, k_ref, v_ref, do_ref, dq_ref, dk_ref, dv_ref, sbuf, ebuf, dq_acc, dk_acc, dv_acc):
        i = pl.program_id(1)

        @pl.when(i == 0)
        def _():
            dk_acc[...] = jnp.zeros_like(dk_acc)
            dv_acc[...] = jnp.zeros_like(dv_acc)

        strict, from_j, upto_j = _sb_tile_masks(bt)
        qs = _sb_scaled_heads(q_ref[...], SB_SCALE)
        dos = _sb_scaled_heads(do_ref[...], 1.0)
        q_both = jnp.concatenate(qs, axis=0)
        do_both = jnp.concatenate(dos, axis=0)
        dq_acc[...] = jnp.zeros_like(dq_acc)

        def pass1(j, nt, carries, diag):
            rows = _kv_rows(j, bt, nt)
            kb, vb = k_ref[rows, :], v_ref[rows, :]
            xs, nlfs = _sb_logits(qs, kb, bt, strict if diag else None)
            dws = [_mm_nt(dos[hh], vb) for hh in range(2)]
            tails = [_sb_tails(nlfs[hh], from_j) for hh in range(2)]
            wcat = []
            for hh in range(2):
                ws = [jnp.exp(xs[hh][tt] - tails[hh][tt] - carries[hh]) for tt in range(nt)]
                if diag:
                    ws[-1] = jnp.where(strict, ws[-1], 0.0)
                w_all = jnp.concatenate(ws, axis=1)
                sbuf[hh, :, rows] = jnp.exp(jnp.concatenate([xs[hh][tt] - nlfs[hh][tt] for tt in range(nt)], axis=1))
                ebuf[hh, :, rows] = w_all * dws[hh]
                wcat.append(w_all.astype(MXU_DTYPE))
            dv_acc[rows, :] += _mm_tn(jnp.concatenate(wcat, axis=0), do_both)
            return tuple(carries[hh] + tails[hh][0][:, 0:1] for hh in range(2))

        zero = jnp.zeros((bt, 1), F32)
        carries = pass1(i, 1, (zero, zero), True)
        carries = lax.fori_loop(0, i // 2, lambda it, cr: pass1(i - 2 - 2 * it, 2, cr, False), carries)

        @pl.when(i % 2 == 1)
        def _():
            pass1(0, 1, carries, False)

        def pass2(j, nt, pres, diag):
            rows = _kv_rows(j, bt, nt)
            kb = k_ref[rows, :]
            sums = [_sb_heads([ebuf[hh, :, _kv_rows(j + tt, bt)] for tt in range(nt)], upto_j, pres[hh]) for hh in range(2)]
            dxm = []
            for hh in range(2):
                dxs = [ebuf[hh, :, _kv_rows(j + tt, bt)] - sbuf[hh, :, _kv_rows(j + tt, bt)] * sums[hh][tt] for tt in range(nt)]
                if diag:
                    dxs[-1] = jnp.where(strict, dxs[-1], 0.0)
                dxm.append(jnp.concatenate(dxs, axis=1).astype(MXU_DTYPE))
                dq_acc[hh] += _mm(dxm[hh], kb)
            dk_acc[rows, :] += _mm_tn(jnp.concatenate(dxm, axis=0), q_both)
            return tuple(sums[hh][-1][:, bt - 1:bt] for hh in range(2))

        pres = lax.fori_loop(0, i // 2, lambda it, pr: pass2(2 * it, 2, pr, False), (zero, zero))

        @pl.when(i % 2 == 0)
        def _():
            pass2(i, 1, pres, True)

        @pl.when(i % 2 == 1)
        def _():
            pass2(i - 1, 2, pres, True)

        low = _iota((bt, lanes), 1) < SB_HEAD_DIM
        dq_ref[...] = (jnp.where(low, dq_acc[0], dq_acc[1]) * SB_SCALE).astype(dq_ref.dtype)

        @pl.when(i == nq - 1)
        def _():
            dk_ref[...] = dk_acc[...].astype(dk_ref.dtype)
            dv_ref[...] = dv_acc[...].astype(dv_ref.dtype)

    blk = pl.BlockSpec((bt, lanes), lambda p, i: (i, p))
    whole = pl.BlockSpec((t, lanes), lambda p, i: (0, p))
    out = jax.ShapeDtypeStruct((t, D_MODEL), MXU_DTYPE)
    return pl.pallas_call(
        body, name=name, grid=(SB_PAIRS, nq),
        in_specs=[blk, pl.BlockSpec((t, lanes), lambda p, i: (0, SB_PAIRS + p)),
                  pl.BlockSpec((t, lanes), lambda p, i: (0, 2 * SB_PAIRS + p)), blk],
        out_specs=[blk, whole, whole], out_shape=[out, out, out],
        scratch_shapes=[pltpu.VMEM((2, bt, t), F32), pltpu.VMEM((2, bt, t), F32), pltpu.VMEM((2, bt, lanes), F32),
                        pltpu.VMEM((t, lanes), F32), pltpu.VMEM((t, lanes), F32)],
        compiler_params=_cp("parallel", "arbitrary"),
    )(qkv, qkv, qkv, do)


def _add_pair(a, b, name):
    s, r, c = a.shape
    tm = 256 if r % 256 == 0 else 8
    assert r % tm == 0

    def body(a_ref, b_ref, o_ref):
        o_ref[...] = (a_ref[...].astype(F32) + b_ref[...].astype(F32)).astype(o_ref.dtype)

    blk = pl.BlockSpec((1, tm, c), lambda q, i: (q, i, 0))
    return pl.pallas_call(body, name=name, grid=(s, r // tm), in_specs=[blk, blk], out_specs=blk,
                          out_shape=jax.ShapeDtypeStruct(a.shape, a.dtype), compiler_params=_cp("parallel", "parallel"))(a, b)


def _adamw(gslots, w, m, v, name):
    s, r, c = gslots.shape
    tm = 256 if r % 256 == 0 else 8
    assert r % tm == 0 and w.shape == (r, c)
    c1 = 1.0 - ADAM_B1 ** ADAM_STEP
    c2 = 1.0 - ADAM_B2 ** ADAM_STEP

    def body(g_ref, w_ref, m_ref, v_ref, go_ref, d_ref, mo_ref, vo_ref):
        g = g_ref[0].astype(F32)
        for q in range(1, s):
            g = g + g_ref[q].astype(F32)
        mn = ADAM_B1 * m_ref[...] + (1.0 - ADAM_B1) * g
        vn = ADAM_B2 * v_ref[...] + (1.0 - ADAM_B2) * (g * g)
        go_ref[...] = g
        mo_ref[...] = mn
        vo_ref[...] = vn
        d_ref[...] = -ADAM_LR * ((mn / c1) / (jnp.sqrt(vn / c2) + ADAM_EPS) + ADAM_WD * w_ref[...])

    row = pl.BlockSpec((tm, c), lambda i: (i, 0))
    out = jax.ShapeDtypeStruct((r, c), F32)
    return pl.pallas_call(
        body, name=name, grid=(r // tm,),
        in_specs=[pl.BlockSpec((s, tm, c), lambda i: (0, i, 0)), row, row, row],
        out_specs=[row, row, row, row], out_shape=[out, out, out, out], compiler_params=_cp("parallel"),
    )(gslots, w, m, v)


def _rows(a):
    flat = a.reshape(-1)
    pad = (-flat.shape[0]) % PACK_W
    if pad:
        flat = jnp.concatenate([flat, jnp.zeros((pad,), flat.dtype)])
    return flat.reshape(-1, PACK_W)


def _pack(arrays, row_multiple):
    parts, layout, off = [], [], 0
    for a in arrays:
        rw = _rows(a)
        parts.append(rw)
        layout.append((off, rw.shape[0], a.shape))
        off += rw.shape[0]
    pad = (-off) % row_multiple
    if pad:
        parts.append(jnp.zeros((pad, PACK_W), parts[0].dtype))
    return jnp.concatenate(parts, axis=0), layout


def _unpack(packed, layout):
    out = []
    for off, nrows, shape in layout:
        n = int(np.prod(shape))
        out.append(packed[off:off + nrows].reshape(-1)[:n].reshape(shape))
    return out


def _unshard_cols(g, lead):
    k = D_MODEL
    ns = g.shape[1] * PACK_W // (lead * k)
    return g.reshape(N_DEV, lead, k, ns).transpose(1, 2, 0, 3).reshape(lead, k, N_DEV * ns)


def _shard_cols(full):
    lead, k, n = full.shape
    return full.reshape(lead, k, N_DEV, n // N_DEV).transpose(2, 0, 1, 3)


def _unshard_rows(g, lead):
    ks = g.shape[1] // lead
    return g.reshape(N_DEV, lead, ks, PACK_W).transpose(1, 0, 2, 3).reshape(lead, N_DEV * ks, PACK_W)


def _shard_rows(full):
    lead, k, n = full.shape
    return full.reshape(lead, N_DEV, k // N_DEV, n).transpose(1, 0, 2, 3)


def _ssd_consts(dt_bias, a_log, d_skip):
    pad = LANES - SSD_HEADS
    bias = jnp.pad(dt_bias, (0, pad)).reshape(1, LANES)
    a_neg = jnp.pad(-jnp.exp(a_log), (0, pad)).reshape(1, LANES)
    d_exp = jnp.repeat(d_skip, SSD_HEAD_DIM).reshape(1, SSD_D_INNER)
    return bias, a_neg, d_exp


def _group_layouts(acum):
    t = acum.shape[0]
    a = acum[:, :SSD_HEADS].reshape(t, SSD_GROUPS, HEADS_PER_GROUP)
    return a.transpose(1, 0, 2), a.transpose(1, 2, 0)


def _ssd_fwd(x, p):
    hn = _rmsnorm(x, p["mix_norm"], "rmsnorm_fwd")
    proj = _matmul(hn, p["w_in"], "nn", F32, "ssd_in_fwd", tm=512, tn=896, tk=1024)
    act = _ssd_conv_fwd(proj, p["conv_w"], p["conv_b"], "ssd_conv_fwd")
    bias, a_neg, d_exp = _ssd_consts(p["dt_bias"], p["a_log"], p["d"])
    expand = _head_expand()
    xdt, dt, acum = _ssd_dt_fwd(proj, act, bias, a_neg, expand, "ssd_dt_fwd")
    acum_g, acum_gt = _group_layouts(acum)
    y, states = _ssd_scan_fwd(xdt, act, acum_g, acum_gt, "ssd_scan_fwd")
    yn = _gated_norm_fwd(y, act, proj, d_exp, p["norm"], "ssd_gnorm_fwd")
    x_new = _matmul(yn, p["w_out"], "nn", F32, "ssd_out_fwd", add=x, tm=512, tn=1024, tk=2048)
    saved = dict(x=x, hn=hn, proj=proj, act=act, xdt=xdt, dt=dt, acum_g=acum_g, acum_gt=acum_gt, y=y, states=states, yn=yn)
    return x_new, saved


def _ssd_bwd(dx, p, s):
    bias, a_neg, d_exp = _ssd_consts(p["dt_bias"], p["a_log"], p["d"])
    expand = _head_expand()
    dxb = dx.astype(MXU_DTYPE)
    dyn = _matmul(dxb, p["w_out"], "nt", F32, "ssd_out_dgrad", tm=512, tn=1024, tk=1024)
    g_w_out = _matmul(s["yn"], dxb, "tn", F32, "ssd_out_wgrad", tm=1024, tn=1024, tk=512)
    dyy, dz, g_norm = _gated_norm_bwd(s["y"], s["act"], s["proj"], d_exp, p["norm"], dyn, "ssd_gnorm_bwd")
    dxdt, dbm, dcm, dacol, darow = _ssd_scan_bwd(s["xdt"], s["act"], s["acum_g"], s["acum_gt"], s["states"], dyy, "ssd_scan_bwd")
    t = dx.shape[0]
    dacum = dacol.transpose(1, 0, 2).reshape(t, SSD_HEADS) + darow.transpose(2, 0, 1).reshape(t, SSD_HEADS)
    dacum = jnp.pad(dacum, ((0, 0), (0, LANES - SSD_HEADS)))
    dxs, draw, g_a, g_bias, g_dexp = _ssd_dt_bwd(s["proj"], s["act"], s["dt"], dxdt, dyy, dacum, bias, a_neg, d_exp,
                                                  expand, expand.T, "ssd_dt_bwd")
    dact = jnp.concatenate([dxs, dbm, dcm], axis=1)
    dxbc, g_conv_w, g_conv_b = _ssd_conv_bwd(s["proj"], p["conv_w"], p["conv_b"], dact, "ssd_conv_bwd")
    dproj = jnp.concatenate([dz, dxbc, draw], axis=1)
    dhn = _matmul(dproj, p["w_in"], "nt", F32, "ssd_in_dgrad", tm=512, tn=1024, tk=896)
    g_w_in = _matmul(s["hn"], dproj, "tn", F32, "ssd_in_wgrad", tm=1024, tn=896, tk=512)
    dx_new, g_mix = _rmsnorm_bwd(s["x"], p["mix_norm"], dhn, dx, "rmsnorm_bwd")
    grads = dict(w_in=g_w_in[:, :SSD_IN_DIM], w_out=g_w_out, conv_w=g_conv_w, conv_b=g_conv_b.reshape(-1),
                 dt_bias=g_bias[0, :SSD_HEADS], a_log=(g_a * a_neg)[0, :SSD_HEADS],
                 d=g_dexp.reshape(SSD_HEADS, SSD_HEAD_DIM).sum(axis=1), norm=g_norm.reshape(-1), mix_norm=g_mix.reshape(-1))
    return dx_new, grads


def _sb_fwd(x, p):
    hn = _rmsnorm(x, p["mix_norm"], "rmsnorm_fwd")
    qkv = _matmul(hn, p["w_qkv"], "nn", MXU_DTYPE, "sb_qkv_fwd", tm=512, tn=1024, tk=1024)
    o = _sb_attention_fwd(qkv, "sb_attn_fwd")
    x_new = _matmul(o, p["w_out"], "nn", F32, "sb_out_fwd", add=x, tm=512, tn=1024, tk=1024)
    return x_new, dict(x=x, hn=hn, qkv=qkv, o=o)


def _sb_bwd(dx, p, s):
    dxb = dx.astype(MXU_DTYPE)
    do = _matmul(dxb, p["w_out"], "nt", MXU_DTYPE, "sb_out_dgrad", tm=512, tn=1024, tk=1024)
    g_w_out = _matmul(s["o"], dxb, "tn", F32, "sb_out_wgrad", tm=1024, tn=1024, tk=512)
    dq, dk, dv = _sb_attention_bwd(s["qkv"], do, "sb_attn_bwd")
    dqkv = jnp.concatenate([dq, dk, dv], axis=1)
    dhn = _matmul(dqkv, p["w_qkv"], "nt", F32, "sb_qkv_dgrad", tm=512, tn=1024, tk=1024)
    g_w_qkv = _matmul(s["hn"], dqkv, "tn", F32, "sb_qkv_wgrad", tm=1024, tn=1024, tk=512)
    dx_new, g_mix = _rmsnorm_bwd(s["x"], p["mix_norm"], dhn, dx, "rmsnorm_bwd")
    return dx_new, dict(w_qkv=g_w_qkv, w_out=g_w_out, mix_norm=g_mix.reshape(-1))


def _ffn_fwd(x, p):
    hn = _rmsnorm(x, p["ffn_norm"], "rmsnorm_fwd")
    proj = _matmul(hn, p["w_in"], "nn", F32, "ffn_in_fwd", tm=512, tn=1408, tk=1024)
    act = _ffn_conv_fwd(proj, p["conv_w"], p["conv_b"], "ffn_conv_fwd")
    x_new = _matmul(act, p["w_out"], "nn", F32, "ffn_out_fwd", add=x, tm=512, tn=1024, tk=1408)
    return x_new, dict(x=x, hn=hn, proj=proj, act=act)


def _ffn_bwd(dx, p, s):
    dxb = dx.astype(MXU_DTYPE)
    dact = _matmul(dxb, p["w_out"], "nt", F32, "ffn_out_dgrad", tm=512, tn=1408, tk=1024)
    g_w_out = _matmul(s["act"], dxb, "tn", F32, "ffn_out_wgrad", tm=1408, tn=1024, tk=512)
    dpg, dpu, dwg, dwu, dbg, dbu = _ffn_conv_bwd(s["proj"], p["conv_w"], p["conv_b"], dact, "ffn_conv_bwd")
    dproj = jnp.concatenate([dpg, dpu], axis=1)
    dhn = _matmul(dproj, p["w_in"], "nt", F32, "ffn_in_dgrad", tm=512, tn=1024, tk=1408)
    g_w_in = _matmul(s["hn"], dproj, "tn", F32, "ffn_in_wgrad", tm=1024, tn=1408, tk=512)
    dx_new, g_norm = _rmsnorm_bwd(s["x"], p["ffn_norm"], dhn, dx, "rmsnorm_bwd")
    grads = dict(w_in=g_w_in, w_out=g_w_out, conv_w=jnp.concatenate([dwg, dwu], axis=1),
                 conv_b=jnp.concatenate([dbg, dbu], axis=1).reshape(-1), ffn_norm=g_norm.reshape(-1))
    return dx_new, grads


BIG_ROW_MULTIPLE = 256
BIG = ["ssd_w_in", "sb_w_qkv", "ffn_w_in", "ssd_w_out", "sb_w_out", "ffn_w_out"]
COL_SHARDED = {"ssd_w_in": 2, "sb_w_qkv": 2, "ffn_w_in": 4}
ROW_SHARDED = {"ssd_w_out": 2, "sb_w_out": 2, "ffn_w_out": 4}
CONV = ["ssd_conv_w", "ffn_conv_w"]
SMALL = ["mix_norm", "ffn_norm", "final_norm", "ssd_conv_b", "ssd_dt_bias", "ssd_a_log", "ssd_d", "ssd_norm", "ffn_conv_b"]
WEIGHTS = ["mix_norm", "ffn_norm", "final_norm", "ssd_w_in", "ssd_conv_w", "ssd_conv_b", "ssd_dt_bias", "ssd_a_log", "ssd_d",
           "ssd_norm", "ssd_w_out", "sb_w_qkv", "sb_w_out", "ffn_w_in", "ffn_conv_w", "ffn_conv_b", "ffn_w_out"]


def _step(x, loss_target, w, m, v):
    x = x.reshape(x.shape[-2], x.shape[-1])
    target = loss_target.reshape(x.shape)
    dev = 4 * lax.axis_index("x") + 2 * lax.axis_index("y") + lax.axis_index("c")
    core = lax.axis_index("c")

    big_pack, big_layout = _pack([w[n].astype(MXU_DTYPE) for n in BIG], BIG_ROW_MULTIPLE)
    big_all = _all_gather(big_pack, "gather_weights")
    full = {}
    for n, (off, nrows, _) in zip(BIG, big_layout):
        g = big_all[:, off:off + nrows]
        full[n] = _unshard_cols(g, COL_SHARDED[n]) if n in COL_SHARDED else _unshard_rows(g, ROW_SHARDED[n])
    full["ssd_w_in"] = jnp.pad(full["ssd_w_in"], ((0, 0), (0, 0), (0, SSD_IN_PAD - SSD_IN_DIM)))
    conv_pack, conv_layout = _pack([w[n] for n in CONV], 8)
    conv_all = _all_gather(conv_pack, "gather_conv_taps")
    for n, (off, nrows, shape) in zip(CONV, conv_layout):
        parts = [_unpack(conv_all[j], conv_layout)[CONV.index(n)] for j in range(N_DEV)]
        full[n] = jnp.concatenate(parts, axis=-1)

    def ssd_params(j):
        return dict(mix_norm=w["mix_norm"][2 * j], w_in=full["ssd_w_in"][j], conv_w=full["ssd_conv_w"][j],
                    conv_b=w["ssd_conv_b"][j], dt_bias=w["ssd_dt_bias"][j], a_log=w["ssd_a_log"][j], d=w["ssd_d"][j],
                    norm=w["ssd_norm"][j], w_out=full["ssd_w_out"][j])

    def sb_params(j):
        return dict(mix_norm=w["mix_norm"][2 * j + 1], w_qkv=full["sb_w_qkv"][j], w_out=full["sb_w_out"][j])

    def ffn_params(i):
        return dict(ffn_norm=w["ffn_norm"][i], w_in=full["ffn_w_in"][i], conv_w=full["ffn_conv_w"][i],
                    conv_b=w["ffn_conv_b"][i], w_out=full["ffn_w_out"][i])

    saved = []
    for i in range(DEPTH):
        if i % 2 == 0:
            x, s_mix = _ssd_fwd(x, ssd_params(i // 2))
        else:
            x, s_mix = _sb_fwd(x, sb_params(i // 2))
        x, s_ffn = _ffn_fwd(x, ffn_params(i))
        saved.append((s_mix, s_ffn))
    dx, g_final, loss_part = _final_norm_loss(x, w["final_norm"], target, "final_norm_loss")

    g_mix, g_ffn, g_ssd, g_sb = [None] * DEPTH, [None] * DEPTH, [None] * 2, [None] * 2
    for i in reversed(range(DEPTH)):
        s_mix, s_ffn = saved[i]
        dx, g_ffn[i] = _ffn_bwd(dx, ffn_params(i), s_ffn)
        if i % 2 == 0:
            dx, g_ssd[i // 2] = _ssd_bwd(dx, ssd_params(i // 2), s_mix)
            g_mix[i] = g_ssd[i // 2]["mix_norm"]
        else:
            dx, g_sb[i // 2] = _sb_bwd(dx, sb_params(i // 2), s_mix)
            g_mix[i] = g_sb[i // 2]["mix_norm"]
    grad_x = dx.reshape(1, *dx.shape)

    gfull = {
        "ssd_w_in": _shard_cols(jnp.stack([g["w_in"] for g in g_ssd])),
        "sb_w_qkv": _shard_cols(jnp.stack([g["w_qkv"] for g in g_sb])),
        "ffn_w_in": _shard_cols(jnp.stack([g["w_in"] for g in g_ffn])),
        "ssd_w_out": _shard_rows(jnp.stack([g["w_out"] for g in g_ssd])),
        "sb_w_out": _shard_rows(jnp.stack([g["w_out"] for g in g_sb])),
        "ffn_w_out": _shard_rows(jnp.stack([g["w_out"] for g in g_ffn])),
    }
    rtot = big_pack.shape[0]
    chunks = []
    for n, (off, nrows, _) in zip(BIG, big_layout):
        chunks.append(gfull[n].astype(MXU_DTYPE).reshape(N_DEV, nrows, PACK_W))
    pad_rows = rtot - sum(c_.shape[1] for c_ in chunks)
    if pad_rows:
        chunks.append(jnp.zeros((N_DEV, pad_rows, PACK_W), MXU_DTYPE))
    g8 = jnp.concatenate(chunks, axis=1).reshape(4, 2, rtot, PACK_W)
    keep = lax.dynamic_index_in_dim(g8, core, axis=1, keepdims=False)
    give = lax.dynamic_index_in_dim(g8, 1 - core, axis=1, keepdims=False)
    got = _swap_with_sibling(give, "grads_to_sibling")
    chip_part = _add_pair(keep, got, "grads_add_sibling")
    from_chips = _exchange_chips(chip_part, "grads_across_chips")
    w_pack, _ = _pack([w[n] for n in BIG], BIG_ROW_MULTIPLE)
    m_pack, _ = _pack([m[n] for n in BIG], BIG_ROW_MULTIPLE)
    v_pack, _ = _pack([v[n] for n in BIG], BIG_ROW_MULTIPLE)
    big_out = _adamw(from_chips, w_pack, m_pack, v_pack, "adamw_matmul_weights")
    big_res = [dict(zip(BIG, _unpack(o, big_layout))) for o in big_out]

    small_g = {
        "mix_norm": jnp.stack(g_mix), "ffn_norm": jnp.stack([g["ffn_norm"] for g in g_ffn]), "final_norm": g_final.reshape(-1),
        "ssd_conv_b": jnp.stack([g["conv_b"] for g in g_ssd]), "ssd_dt_bias": jnp.stack([g["dt_bias"] for g in g_ssd]),
        "ssd_a_log": jnp.stack([g["a_log"] for g in g_ssd]), "ssd_d": jnp.stack([g["d"] for g in g_ssd]),
        "ssd_norm": jnp.stack([g["norm"] for g in g_ssd]), "ffn_conv_b": jnp.stack([g["conv_b"] for g in g_ffn]),
    }
    conv_g = {"ssd_conv_w": jnp.stack([g["conv_w"] for g in g_ssd]), "ffn_conv_w": jnp.stack([g["conv_w"] for g in g_ffn])}
    extra = [conv_g[n] for n in CONV] + [loss_part]
    small_pack, small_layout = _pack([small_g[n] for n in SMALL] + extra, 8)
    small_all = _all_gather(small_pack, "gather_small_grads")
    zeros_like = [jnp.zeros(a.shape, F32) for a in extra]
    sw, _ = _pack([w[n] for n in SMALL] + zeros_like, 8)
    sm, _ = _pack([m[n] for n in SMALL] + zeros_like, 8)
    sv, _ = _pack([v[n] for n in SMALL] + [jnp.ones(a.shape, F32) for a in extra], 8)
    small_out = _adamw(small_all, sw, sm, sv, "adamw_replicated")
    small_res = [_unpack(o, small_layout) for o in small_out]
    summed = small_res[0]
    loss = summed[-1][0, 0]
    conv_shard_g = []
    for n, gsum in zip(CONV, summed[len(SMALL):len(SMALL) + len(CONV)]):
        ns = w[n].shape[-1]
        conv_shard_g.append(lax.dynamic_slice_in_dim(gsum, dev * ns, ns, axis=2))
    cg, conv_sh_layout = _pack(conv_shard_g, 8)
    cw, _ = _pack([w[n] for n in CONV], 8)
    cm_, _ = _pack([m[n] for n in CONV], 8)
    cv, _ = _pack([v[n] for n in CONV], 8)
    conv_out = _adamw(cg.reshape(1, *cg.shape), cw, cm_, cv, "adamw_conv_taps")
    conv_res = [dict(zip(CONV, _unpack(o, conv_sh_layout))) for o in conv_out]

    def pick(kind, n):
        if n in BIG:
            return big_res[kind][n]
        if n in CONV:
            return conv_res[kind][n]
        return small_res[kind][SMALL.index(n)]

    outs = [loss, grad_x]
    for kind in range(4):
        outs += [pick(kind, n) for n in WEIGHTS]
    return tuple(outs)


def kernel(x, mix_norm, ffn_norm, final_norm, ssd_w_in, ssd_conv_w, ssd_conv_b, ssd_dt_bias, ssd_a_log, ssd_d, ssd_norm, ssd_w_out, sb_w_qkv, sb_w_out, ffn_w_in, ffn_conv_w, ffn_conv_b, ffn_w_out, loss_target, m_mix_norm, m_ffn_norm, m_final_norm, m_ssd_w_in, m_ssd_conv_w, m_ssd_conv_b, m_ssd_dt_bias, m_ssd_a_log, m_ssd_d, m_ssd_norm, m_ssd_w_out, m_sb_w_qkv, m_sb_w_out, m_ffn_w_in, m_ffn_conv_w, m_ffn_conv_b, m_ffn_w_out, v_mix_norm, v_ffn_norm, v_final_norm, v_ssd_w_in, v_ssd_conv_w, v_ssd_conv_b, v_ssd_dt_bias, v_ssd_a_log, v_ssd_d, v_ssd_norm, v_ssd_w_out, v_sb_w_qkv, v_sb_w_out, v_ffn_w_in, v_ffn_conv_w, v_ffn_conv_b, v_ffn_w_out):
    w = dict(mix_norm=mix_norm, ffn_norm=ffn_norm, final_norm=final_norm, ssd_w_in=ssd_w_in, ssd_conv_w=ssd_conv_w,
             ssd_conv_b=ssd_conv_b, ssd_dt_bias=ssd_dt_bias, ssd_a_log=ssd_a_log, ssd_d=ssd_d, ssd_norm=ssd_norm,
             ssd_w_out=ssd_w_out, sb_w_qkv=sb_w_qkv, sb_w_out=sb_w_out, ffn_w_in=ffn_w_in, ffn_conv_w=ffn_conv_w,
             ffn_conv_b=ffn_conv_b, ffn_w_out=ffn_w_out)
    m = dict(mix_norm=m_mix_norm, ffn_norm=m_ffn_norm, final_norm=m_final_norm, ssd_w_in=m_ssd_w_in, ssd_conv_w=m_ssd_conv_w,
             ssd_conv_b=m_ssd_conv_b, ssd_dt_bias=m_ssd_dt_bias, ssd_a_log=m_ssd_a_log, ssd_d=m_ssd_d, ssd_norm=m_ssd_norm,
             ssd_w_out=m_ssd_w_out, sb_w_qkv=m_sb_w_qkv, sb_w_out=m_sb_w_out, ffn_w_in=m_ffn_w_in, ffn_conv_w=m_ffn_conv_w,
             ffn_conv_b=m_ffn_conv_b, ffn_w_out=m_ffn_w_out)
    v = dict(mix_norm=v_mix_norm, ffn_norm=v_ffn_norm, final_norm=v_final_norm, ssd_w_in=v_ssd_w_in, ssd_conv_w=v_ssd_conv_w,
             ssd_conv_b=v_ssd_conv_b, ssd_dt_bias=v_ssd_dt_bias, ssd_a_log=v_ssd_a_log, ssd_d=v_ssd_d, ssd_norm=v_ssd_norm,
             ssd_w_out=v_ssd_w_out, sb_w_qkv=v_sb_w_qkv, sb_w_out=v_sb_w_out, ffn_w_in=v_ffn_w_in, ffn_conv_w=v_ffn_conv_w,
             ffn_conv_b=v_ffn_conv_b, ffn_w_out=v_ffn_w_out)
    return _step(x, loss_target, w, m, v)
```

```python
import functools

import jax
import jax.numpy as jnp
import numpy as np
from jax import lax
from jax.experimental import pallas as pl
from jax.experimental.pallas import tpu as pltpu

F32 = jnp.float32
MXU_DTYPE = jnp.bfloat16
MESH_ID = pl.DeviceIdType.MESH
N_DEV = 8

NORM_EPS = 1e-6
D_MODEL = 1024
DEPTH = 4
SSD_D_INNER = 2048
SSD_HEADS = 32
SSD_HEAD_DIM = 64
SSD_GROUPS = 8
SSD_STATE = 128
SSD_CONV = 4
SSD_CHUNK = 128
SSD_CONV_DIM = SSD_D_INNER + 2 * SSD_GROUPS * SSD_STATE
SSD_IN_DIM = SSD_D_INNER + SSD_CONV_DIM + SSD_HEADS
LANES = 128
SSD_IN_PAD = SSD_D_INNER + SSD_CONV_DIM + LANES
SB_HEADS = 16
SB_HEAD_DIM = 64
SB_TILE = 256
SB_SCALE = SB_HEAD_DIM ** -0.5
FFN_D_FF = 2816
FFN_CONV = 3
PACK_W = 1024

ADAM_LR = 0.001
ADAM_B1 = 0.9
ADAM_B2 = 0.999
ADAM_EPS = 1e-08
ADAM_WD = 0.01
ADAM_STEP = 10

VMEM_LIMIT_BYTES = 56 * 1024 * 1024


def _cp(*sem):
    return pltpu.CompilerParams(dimension_semantics=sem, vmem_limit_bytes=VMEM_LIMIT_BYTES)


def _iota(shape, dim):
    return lax.broadcasted_iota(jnp.int32, shape, dim)


def _sigmoid(x):
    return 1.0 / (1.0 + jnp.exp(-x))


def _mm(a, b):
    return lax.dot_general(a, b, (((1,), (0,)), ((), ())), preferred_element_type=F32)


def _mm_nt(a, b):
    return lax.dot_general(a, b, (((1,), (1,)), ((), ())), preferred_element_type=F32)


def _mm_tn(a, b):
    return lax.dot_general(a, b, (((0,), (0,)), ((), ())), preferred_element_type=F32)


def _split(x):
    hi = x.astype(MXU_DTYPE)
    lo = (x - hi.astype(F32)).astype(MXU_DTYPE)
    return hi, lo


def _mm_exact_rhs(x, m):
    hi, lo = _split(x)
    return _mm(jnp.concatenate([hi, lo], axis=1), jnp.concatenate([m, m], axis=0))


def _mm_exact_lhs(m, x):
    hi, lo = _split(x)
    return _mm(jnp.concatenate([m, m], axis=1), jnp.concatenate([hi, lo], axis=0))


def _my_place():
    return lax.axis_index("x"), lax.axis_index("y"), lax.axis_index("c")


def _all_gather(shard, name):
    r, c_ = shard.shape

    def body(x_ref, out_ref, send_sems, recv_sems, local_sem):
        x, y, c = _my_place()
        me, sibling = (x, y, c), (x, y, 1 - c)
        chips = [(1 - x, y), (x, 1 - y), (1 - x, 1 - y)]

        def slot(px, py, pc):
            return out_ref.at[4 * px + 2 * py + pc]

        def copy(k, block, to, src=None):
            return pltpu.make_async_remote_copy(
                src_ref=slot(*block) if src is None else src, dst_ref=slot(*block),
                send_sem=send_sems.at[k], recv_sem=recv_sems.at[k], device_id=to, device_id_type=MESH_ID)

        mine = pltpu.make_async_copy(x_ref, slot(*me), local_sem)
        mine.start()
        first = [copy(0, me, sibling, src=x_ref)]
        first += [copy(1 + j, me, (*chip, c), src=x_ref) for j, chip in enumerate(chips)]
        for cp in first:
            cp.start()
        passed = [copy(4 + j, (*chip, c), sibling) for j, chip in enumerate(chips)]
        for j, chip in enumerate(chips):
            copy(1 + j, (*chip, c), me).wait_recv()
            passed[j].start()
        copy(0, sibling, me).wait_recv()
        for j, chip in enumerate(chips):
            copy(4 + j, (*chip, 1 - c), me).wait_recv()
        for cp in first + passed:
            cp.wait_send()
        mine.wait()

    return pl.pallas_call(
        body, name=name,
        out_shape=jax.ShapeDtypeStruct((N_DEV, r, c_), shard.dtype),
        in_specs=[pl.BlockSpec(memory_space=pl.ANY)],
        out_specs=pl.BlockSpec(memory_space=pl.ANY),
        scratch_shapes=[pltpu.SemaphoreType.DMA((7,)), pltpu.SemaphoreType.DMA((7,)), pltpu.SemaphoreType.DMA(())],
    )(shard)


def _swap_with_sibling(buf, name):
    def body(x_ref, out_ref, send_sem, recv_sem):
        x, y, c = _my_place()
        cp = pltpu.make_async_remote_copy(src_ref=x_ref, dst_ref=out_ref, send_sem=send_sem, recv_sem=recv_sem,
                                          device_id=(x, y, 1 - c), device_id_type=MESH_ID)
        cp.start()
        cp.wait()

    return pl.pallas_call(
        body, name=name, out_shape=jax.ShapeDtypeStruct(buf.shape, buf.dtype),
        in_specs=[pl.BlockSpec(memory_space=pl.ANY)], out_specs=pl.BlockSpec(memory_space=pl.ANY),
        scratch_shapes=[pltpu.SemaphoreType.DMA(()), pltpu.SemaphoreType.DMA(())],
    )(buf)


def _exchange_chips(parts, name):
    def body(p_ref, out_ref, send_sems, recv_sems, local_sem):
        x, y, c = _my_place()
        my_q = 2 * x + y
        chips = [(1 - x, y), (x, 1 - y), (1 - x, 1 - y)]
        local = pltpu.make_async_copy(p_ref.at[my_q], out_ref.at[my_q], local_sem)
        local.start()

        def copy(k, px, py):
            return pltpu.make_async_remote_copy(
                src_ref=p_ref.at[2 * px + py], dst_ref=out_ref.at[my_q],
                send_sem=send_sems.at[k], recv_sem=recv_sems.at[k], device_id=(px, py, c), device_id_type=MESH_ID)

        def landing(k, px, py):
            return pltpu.make_async_remote_copy(
                src_ref=p_ref.at[my_q], dst_ref=out_ref.at[2 * px + py],
                send_sem=send_sems.at[k], recv_sem=recv_sems.at[k], device_id=(px, py, c), device_id_type=MESH_ID)

        sends = [copy(k, px, py) for k, (px, py) in enumerate(chips)]
        for cp in sends:
            cp.start()
        for k, (px, py) in enumerate(chips):
            landing(k, px, py).wait_recv()
        for cp in sends:
            cp.wait_send()
        local.wait()

    return pl.pallas_call(
        body, name=name, out_shape=jax.ShapeDtypeStruct(parts.shape, parts.dtype),
        in_specs=[pl.BlockSpec(memory_space=pl.ANY)], out_specs=pl.BlockSpec(memory_space=pl.ANY),
        scratch_shapes=[pltpu.SemaphoreType.DMA((3,)), pltpu.SemaphoreType.DMA((3,)), pltpu.SemaphoreType.DMA(())],
    )(parts)


def _matmul(a, b, mode, out_dtype, name, add=None, tm=512, tn=512, tk=512):
    if mode == "nn":
        (m, k), (k2, n) = a.shape, b.shape
    elif mode == "nt":
        (m, k), (n, k2) = a.shape, b.shape
    else:
        (k, m), (k2, n) = a.shape, b.shape
    assert k == k2, (a.shape, b.shape, mode)
    tm, tn, tk = min(tm, m), min(tn, n), min(tk, k)
    assert m % tm == 0 and n % tn == 0 and k % tk == 0, (m, n, k, tm, tn, tk)
    nk = k // tk
    mm = {"nn": _mm, "nt": _mm_nt, "tn": _mm_tn}[mode]

    def body(*refs):
        if add is None:
            a_ref, b_ref, o_ref, acc_ref = refs
        else:
            a_ref, b_ref, add_ref, o_ref, acc_ref = refs
        kk = pl.program_id(2)

        @pl.when(kk == 0)
        def _():
            acc_ref[...] = jnp.zeros_like(acc_ref)

        acc_ref[...] += mm(a_ref[...].astype(MXU_DTYPE), b_ref[...].astype(MXU_DTYPE))

        @pl.when(kk == nk - 1)
        def _():
            res = acc_ref[...]
            if add is not None:
                res = res + add_ref[...]
            o_ref[...] = res.astype(o_ref.dtype)

    a_spec = {"nn": pl.BlockSpec((tm, tk), lambda i, j, kk: (i, kk)),
              "nt": pl.BlockSpec((tm, tk), lambda i, j, kk: (i, kk)),
              "tn": pl.BlockSpec((tk, tm), lambda i, j, kk: (kk, i))}[mode]
    b_spec = {"nn": pl.BlockSpec((tk, tn), lambda i, j, kk: (kk, j)),
              "nt": pl.BlockSpec((tn, tk), lambda i, j, kk: (j, kk)),
              "tn": pl.BlockSpec((tk, tn), lambda i, j, kk: (kk, j))}[mode]
    o_spec = pl.BlockSpec((tm, tn), lambda i, j, kk: (i, j))
    in_specs, args = [a_spec, b_spec], [a, b]
    if add is not None:
        in_specs.append(o_spec)
        args.append(add)
    return pl.pallas_call(
        body, name=name, grid=(m // tm, n // tn, nk), in_specs=in_specs, out_specs=o_spec,
        out_shape=jax.ShapeDtypeStruct((m, n), out_dtype),
        scratch_shapes=[pltpu.VMEM((tm, tn), F32)],
        compiler_params=_cp("parallel", "parallel", "arbitrary"),
    )(*args)


def _rmsnorm(x, g, name):
    t, d = x.shape
    tm = min(512, t)

    def body(x_ref, g_ref, o_ref):
        xv = x_ref[...]
        r = lax.rsqrt(jnp.mean(xv * xv, axis=-1, keepdims=True) + NORM_EPS)
        o_ref[...] = (xv * r * g_ref[...]).astype(o_ref.dtype)

    return pl.pallas_call(
        body, name=name, grid=(t // tm,),
        in_specs=[pl.BlockSpec((tm, d), lambda i: (i, 0)), pl.BlockSpec((1, d), lambda i: (0, 0))],
        out_specs=pl.BlockSpec((tm, d), lambda i: (i, 0)),
        out_shape=jax.ShapeDtypeStruct((t, d), MXU_DTYPE), compiler_params=_cp("parallel"),
    )(x, g.reshape(1, d))


def _rmsnorm_bwd(x, g, dh, dres, name):
    t, d = x.shape
    tm = min(512, t)

    def body(x_ref, g_ref, dh_ref, dres_ref, dx_ref, dg_ref):
        @pl.when(pl.program_id(0) == 0)
        def _():
            dg_ref[...] = jnp.zeros_like(dg_ref)

        xv = x_ref[...]
        r = lax.rsqrt(jnp.mean(xv * xv, axis=-1, keepdims=True) + NORM_EPS)
        xn = xv * r
        dhv = dh_ref[...]
        u = dhv * g_ref[...]
        dx_ref[...] = dres_ref[...] + r * (u - xn * jnp.mean(u * xn, axis=-1, keepdims=True))
        dg_ref[...] += jnp.sum(dhv * xn, axis=0, keepdims=True)

    row = pl.BlockSpec((tm, d), lambda i: (i, 0))
    vec = pl.BlockSpec((1, d), lambda i: (0, 0))
    return pl.pallas_call(
        body, name=name, grid=(t // tm,), in_specs=[row, vec, row, row], out_specs=[row, vec],
        out_shape=[jax.ShapeDtypeStruct((t, d), F32), jax.ShapeDtypeStruct((1, d), F32)],
        compiler_params=_cp("arbitrary"),
    )(x, g.reshape(1, d), dh, dres)


def _final_norm_loss(x, g, target, name):
    t, d = x.shape
    tm = min(512, t)

    def body(x_ref, g_ref, t_ref, dx_ref, dg_ref, loss_ref):
        @pl.when(pl.program_id(0) == 0)
        def _():
            dg_ref[...] = jnp.zeros_like(dg_ref)
            loss_ref[...] = jnp.zeros_like(loss_ref)

        xv = x_ref[...]
        gv = g_ref[...]
        r = lax.rsqrt(jnp.mean(xv * xv, axis=-1, keepdims=True) + NORM_EPS)
        xn = xv * r
        err = xn * gv - t_ref[...]
        per_tok = jnp.mean(err * err, axis=-1, keepdims=True)
        loss_ref[...] += jnp.broadcast_to(0.5 * jnp.sum(per_tok, axis=0, keepdims=True), loss_ref.shape)
        dy = err * (1.0 / d)
        u = dy * gv
        dx_ref[...] = r * (u - xn * jnp.mean(u * xn, axis=-1, keepdims=True))
        dg_ref[...] += jnp.sum(dy * xn, axis=0, keepdims=True)

    row = pl.BlockSpec((tm, d), lambda i: (i, 0))
    vec = pl.BlockSpec((1, d), lambda i: (0, 0))
    return pl.pallas_call(
        body, name=name, grid=(t // tm,), in_specs=[row, vec, row],
        out_specs=[row, vec, pl.BlockSpec((1, LANES), lambda i: (0, 0))],
        out_shape=[jax.ShapeDtypeStruct((t, d), F32), jax.ShapeDtypeStruct((1, d), F32),
                   jax.ShapeDtypeStruct((1, LANES), F32)],
        compiler_params=_cp("arbitrary"),
    )(x, g.reshape(1, d), target)


CONV_COLS = 128


def _shifts_down(p, width):
    row = _iota(p.shape, 0)
    return [jnp.where(row >= s, pltpu.roll(p, s, axis=0), 0.0) for s in range(1, width)]


def _shifts_up(p, width):
    n = p.shape[0]
    row = _iota(p.shape, 0)
    return [jnp.where(row < n - s, pltpu.roll(p, n - s, axis=0), 0.0) for s in range(1, width)]


def _conv_pre(p, shifted, w_ref, b_ref):
    width = w_ref.shape[0]
    u = b_ref[...] + w_ref[width - 1:width, :] * p
    for s in range(1, width):
        u = u + w_ref[width - 1 - s:width - s, :] * shifted[s - 1]
    return u


def _conv_transpose(du, w_ref):
    width = w_ref.shape[0]
    shifted = _shifts_up(du, width)
    dp = w_ref[width - 1:width, :] * du
    for s in range(1, width):
        dp = dp + w_ref[width - 1 - s:width - s, :] * shifted[s - 1]
    return dp


def _conv_wgrad(du, p, shifted, dw_ref, db_ref):
    width = dw_ref.shape[0]
    db_ref[...] = jnp.sum(du, axis=0, keepdims=True)
    dw_ref[width - 1:width, :] = jnp.sum(du * p, axis=0, keepdims=True)
    for s in range(1, width):
        dw_ref[width - 1 - s:width - s, :] = jnp.sum(du * shifted[s - 1], axis=0, keepdims=True)


def _ssd_conv_fwd(proj, w, b, name):
    t = proj.shape[0]
    cb = CONV_COLS
    off = SSD_D_INNER // cb

    def body(p_ref, w_ref, b_ref, o_ref):
        p = p_ref[...]
        u = _conv_pre(p, _shifts_down(p, SSD_CONV), w_ref, b_ref)
        o_ref[...] = u * _sigmoid(u)

    return pl.pallas_call(
        body, name=name, grid=(SSD_CONV_DIM // cb,),
        in_specs=[pl.BlockSpec((t, cb), lambda j: (0, j + off)), pl.BlockSpec((SSD_CONV, cb), lambda j: (0, j)),
                  pl.BlockSpec((1, cb), lambda j: (0, j))],
        out_specs=pl.BlockSpec((t, cb), lambda j: (0, j)),
        out_shape=jax.ShapeDtypeStruct((t, SSD_CONV_DIM), F32), compiler_params=_cp("parallel"),
    )(proj, w, b.reshape(1, -1))


def _ssd_conv_bwd(proj, w, b, dact, name):
    t = proj.shape[0]
    cb = CONV_COLS
    off = SSD_D_INNER // cb

    def body(p_ref, w_ref, b_ref, da_ref, dp_ref, dw_ref, db_ref):
        p = p_ref[...]
        shifted = _shifts_down(p, SSD_CONV)
        u = _conv_pre(p, shifted, w_ref, b_ref)
        sg = _sigmoid(u)
        du = da_ref[...] * (sg * (1.0 + u * (1.0 - sg)))
        dp_ref[...] = _conv_transpose(du, w_ref).astype(dp_ref.dtype)
        _conv_wgrad(du, p, shifted, dw_ref, db_ref)

    col = pl.BlockSpec((t, cb), lambda j: (0, j))
    wspec = pl.BlockSpec((SSD_CONV, cb), lambda j: (0, j))
    bspec = pl.BlockSpec((1, cb), lambda j: (0, j))
    return pl.pallas_call(
        body, name=name, grid=(SSD_CONV_DIM // cb,),
        in_specs=[pl.BlockSpec((t, cb), lambda j: (0, j + off)), wspec, bspec, col],
        out_specs=[col, wspec, bspec],
        out_shape=[jax.ShapeDtypeStruct((t, SSD_CONV_DIM), MXU_DTYPE), jax.ShapeDtypeStruct((SSD_CONV, SSD_CONV_DIM), F32),
                   jax.ShapeDtypeStruct((1, SSD_CONV_DIM), F32)],
        compiler_params=_cp("parallel"),
    )(proj, w, b.reshape(1, -1), dact)


def _ffn_conv_fwd(proj, w, b, name):
    t = proj.shape[0]
    cb = CONV_COLS
    nb = FFN_D_FF // cb

    def body(pg_ref, pu_ref, wg_ref, wu_ref, bg_ref, bu_ref, o_ref):
        pg, pu = pg_ref[...], pu_ref[...]
        ug = _conv_pre(pg, _shifts_down(pg, FFN_CONV), wg_ref, bg_ref)
        uu = _conv_pre(pu, _shifts_down(pu, FFN_CONV), wu_ref, bu_ref)
        o_ref[...] = (ug * _sigmoid(ug) * uu).astype(o_ref.dtype)

    gcol = pl.BlockSpec((t, cb), lambda j: (0, j))
    ucol = pl.BlockSpec((t, cb), lambda j: (0, j + nb))
    b2 = b.reshape(1, -1)
    return pl.pallas_call(
        body, name=name, grid=(nb,),
        in_specs=[gcol, ucol, pl.BlockSpec((FFN_CONV, cb), lambda j: (0, j)), pl.BlockSpec((FFN_CONV, cb), lambda j: (0, j + nb)),
                  pl.BlockSpec((1, cb), lambda j: (0, j)), pl.BlockSpec((1, cb), lambda j: (0, j + nb))],
        out_specs=gcol, out_shape=jax.ShapeDtypeStruct((t, FFN_D_FF), MXU_DTYPE), compiler_params=_cp("parallel"),
    )(proj, proj, w, w, b2, b2)


def _ffn_conv_bwd(proj, w, b, dact, name):
    t = proj.shape[0]
    cb = CONV_COLS
    nb = FFN_D_FF // cb

    def body(pg_ref, pu_ref, wg_ref, wu_ref, bg_ref, bu_ref, da_ref,
             dpg_ref, dpu_ref, dwg_ref, dwu_ref, dbg_ref, dbu_ref):
        pg, pu = pg_ref[...], pu_ref[...]
        pg_shifted, pu_shifted = _shifts_down(pg, FFN_CONV), _shifts_down(pu, FFN_CONV)
        ug = _conv_pre(pg, pg_shifted, wg_ref, bg_ref)
        uu = _conv_pre(pu, pu_shifted, wu_ref, bu_ref)
        sg = _sigmoid(ug)
        da = da_ref[...]
        dug = da * uu * (sg * (1.0 + ug * (1.0 - sg)))
        duu = da * (ug * sg)
        dpg_ref[...] = _conv_transpose(dug, wg_ref).astype(dpg_ref.dtype)
        dpu_ref[...] = _conv_transpose(duu, wu_ref).astype(dpu_ref.dtype)
        _conv_wgrad(dug, pg, pg_shifted, dwg_ref, dbg_ref)
        _conv_wgrad(duu, pu, pu_shifted, dwu_ref, dbu_ref)

    gcol = pl.BlockSpec((t, cb), lambda j: (0, j))
    ucol = pl.BlockSpec((t, cb), lambda j: (0, j + nb))
    wg = pl.BlockSpec((FFN_CONV, cb), lambda j: (0, j))
    wu = pl.BlockSpec((FFN_CONV, cb), lambda j: (0, j + nb))
    bg = pl.BlockSpec((1, cb), lambda j: (0, j))
    bu = pl.BlockSpec((1, cb), lambda j: (0, j + nb))
    b2 = b.reshape(1, -1)
    half = jax.ShapeDtypeStruct((t, FFN_D_FF), MXU_DTYPE)
    return pl.pallas_call(
        body, name=name, grid=(nb,),
        in_specs=[gcol, ucol, wg, wu, bg, bu, gcol],
        out_specs=[gcol, gcol, wg, wg, bg, bg],
        out_shape=[half, half, jax.ShapeDtypeStruct((FFN_CONV, FFN_D_FF), F32), jax.ShapeDtypeStruct((FFN_CONV, FFN_D_FF), F32),
                   jax.ShapeDtypeStruct((1, FFN_D_FF), F32), jax.ShapeDtypeStruct((1, FFN_D_FF), F32)],
        compiler_params=_cp("parallel"),
    )(proj, proj, w, w, b2, b2, dact)


SSD_ROWS = 128
DT_COL = (SSD_D_INNER + SSD_CONV_DIM) // LANES


def _head_expand():
    h = np.arange(LANES)[:, None]
    col = np.arange(SSD_D_INNER)[None, :]
    return jnp.asarray((col // SSD_HEAD_DIM == h), MXU_DTYPE)


def _chunk_tri(n, lower):
    t = _iota((n, n), 0)
    s = _iota((n, n), 1)
    shift = SSD_CHUNK.bit_length() - 1
    same = jnp.right_shift(t, shift) == jnp.right_shift(s, shift)
    tri = (s <= t) if lower else (s >= t)
    return jnp.where(same & tri, 1.0, 0.0).astype(MXU_DTYPE)


def _softplus(x):
    return jnp.maximum(x, 0.0) + jnp.log(1.0 + jnp.exp(-jnp.abs(x)))


def _ssd_dt_fwd(proj, act, dt_bias, a_neg, expand, name):
    t = proj.shape[0]
    tm = min(SSD_ROWS, t)

    def body(raw_ref, xs_ref, bias_ref, a_ref, e_ref, xdt_ref, dt_ref, acum_ref):
        lane = _iota((tm, LANES), 1)
        dt = jnp.where(lane < SSD_HEADS, _softplus(raw_ref[...] + bias_ref[...]), 0.0)
        dt_ref[...] = dt
        xdt_ref[...] = xs_ref[...] * _mm_exact_rhs(dt, e_ref[...])
        acum_ref[...] = _mm_exact_lhs(_chunk_tri(tm, True), a_ref[...] * dt)

    vec = pl.BlockSpec((1, LANES), lambda i: (0, 0))
    return pl.pallas_call(
        body, name=name, grid=(t // tm,),
        in_specs=[pl.BlockSpec((tm, LANES), lambda i: (i, DT_COL)), pl.BlockSpec((tm, SSD_D_INNER), lambda i: (i, 0)),
                  vec, vec, pl.BlockSpec((LANES, SSD_D_INNER), lambda i: (0, 0))],
        out_specs=[pl.BlockSpec((tm, SSD_D_INNER), lambda i: (i, 0)), pl.BlockSpec((tm, LANES), lambda i: (i, 0)),
                   pl.BlockSpec((tm, LANES), lambda i: (i, 0))],
        out_shape=[jax.ShapeDtypeStruct((t, SSD_D_INNER), F32), jax.ShapeDtypeStruct((t, LANES), F32),
                   jax.ShapeDtypeStruct((t, LANES), F32)],
        compiler_params=_cp("parallel"),
    )(proj, act, dt_bias, a_neg, expand)


def _ssd_dt_bwd(proj, act, dt, dxdt, dyy, dacum, dt_bias, a_neg, d_exp, expand, expand_t, name):
    t = proj.shape[0]
    tm = min(SSD_ROWS, t)

    def body(raw_ref, xs_ref, dt_ref, dxdt_ref, dyy_ref, dac_ref, bias_ref, a_ref, dsk_ref, e_ref, et_ref,
             dxs_ref, draw_ref, da_ref, dbias_ref, dd_ref):
        @pl.when(pl.program_id(0) == 0)
        def _():
            da_ref[...] = jnp.zeros_like(da_ref)
            dbias_ref[...] = jnp.zeros_like(dbias_ref)
            dd_ref[...] = jnp.zeros_like(dd_ref)

        lane = _iota((tm, LANES), 1)
        xs, dt, dxdt, dyy = xs_ref[...], dt_ref[...], dxdt_ref[...], dyy_ref[...]
        dxs_ref[...] = dxdt * _mm_exact_rhs(dt, e_ref[...]) + dsk_ref[...] * dyy
        dd_ref[...] += jnp.sum(dyy * xs, axis=0, keepdims=True)
        ddt = _mm_exact_rhs(dxdt * xs, et_ref[...])
        da = _mm_exact_lhs(_chunk_tri(tm, False), dac_ref[...])
        ddt = ddt + da * a_ref[...]
        da_ref[...] += jnp.sum(da * dt, axis=0, keepdims=True)
        draw = jnp.where(lane < SSD_HEADS, ddt * _sigmoid(raw_ref[...] + bias_ref[...]), 0.0)
        dbias_ref[...] += jnp.sum(draw, axis=0, keepdims=True)
        draw_ref[...] = draw.astype(draw_ref.dtype)

    wide = pl.BlockSpec((tm, SSD_D_INNER), lambda i: (i, 0))
    thin = pl.BlockSpec((tm, LANES), lambda i: (i, 0))
    vec = pl.BlockSpec((1, LANES), lambda i: (0, 0))
    wvec = pl.BlockSpec((1, SSD_D_INNER), lambda i: (0, 0))
    return pl.pallas_call(
        body, name=name, grid=(t // tm,),
        in_specs=[pl.BlockSpec((tm, LANES), lambda i: (i, DT_COL)), wide, thin, wide, wide, thin, vec, vec, wvec,
                  pl.BlockSpec((LANES, SSD_D_INNER), lambda i: (0, 0)), pl.BlockSpec((SSD_D_INNER, LANES), lambda i: (0, 0))],
        out_specs=[wide, thin, vec, vec, wvec],
        out_shape=[jax.ShapeDtypeStruct((t, SSD_D_INNER), F32), jax.ShapeDtypeStruct((t, LANES), MXU_DTYPE),
                   jax.ShapeDtypeStruct((1, LANES), F32), jax.ShapeDtypeStruct((1, LANES), F32),
                   jax.ShapeDtypeStruct((1, SSD_D_INNER), F32)],
        compiler_params=_cp("arbitrary"),
    )(proj, act, dt, dxdt, dyy, dacum, dt_bias, a_neg, d_exp, expand, expand_t)


SSD_PAIR = 2 * SSD_HEAD_DIM
HEADS_PER_GROUP = SSD_HEADS // SSD_GROUPS
GROUP_COLS = HEADS_PER_GROUP * SSD_HEAD_DIM
B_COL0 = SSD_D_INNER // SSD_STATE
C_COL0 = (SSD_D_INNER + SSD_GROUPS * SSD_STATE) // SSD_STATE


def _pair_cols(vals, h0, lo_mask):
    return jnp.where(lo_mask, vals[:, h0:h0 + 1], vals[:, h0 + 1:h0 + 2])


def _ssd_scan_fwd(xdt, act, acum_g, acum_gt, name):
    t = xdt.shape[0]
    nc = t // SSD_CHUNK
    ln = SSD_CHUNK

    def body(x_ref, b_ref, c_ref, ac_ref, act_ref, y_ref, sst_ref, state):
        @pl.when(pl.program_id(1) == 0)
        def _():
            state[...] = jnp.zeros_like(state)

        bm = b_ref[...].astype(MXU_DTYPE)
        cm = c_ref[...].astype(MXU_DTYPE)
        cb = _mm_nt(cm, bm)
        ac, act_ = ac_ref[0], act_ref[0]
        causal = _iota((ln, ln), 1) <= _iota((ln, ln), 0)
        lo_mask = _iota((ln, SSD_PAIR), 1) < SSD_HEAD_DIM
        lo_rows = _iota((SSD_PAIR, SSD_STATE), 0) < SSD_HEAD_DIM
        sst_ref[0, 0] = state[...]
        last = ac[ln - 1:ln, :]
        e_ac = jnp.exp(ac)
        w_all = jnp.exp(last - ac)
        e_last = jnp.exp(last)
        for pr in range(2):
            cols = slice(pr * SSD_PAIR, (pr + 1) * SSD_PAIR)
            xp = x_ref[:, cols]
            sp = state[cols, :]
            ydiag = jnp.zeros((ln, SSD_PAIR), F32)
            for hh in range(2):
                h = 2 * pr + hh
                seg = ac[:, h:h + 1] - act_[h:h + 1, :]
                dec = jnp.exp(jnp.where(causal, seg, -1e30))
                mask = lo_mask if hh == 0 else jnp.logical_not(lo_mask)
                ydiag = ydiag + _mm((cb * dec).astype(MXU_DTYPE), jnp.where(mask, xp, 0.0).astype(MXU_DTYPE))
            yoff = _mm_nt(cm, sp.astype(MXU_DTYPE)) * _pair_cols(e_ac, 2 * pr, lo_mask)
            y_ref[:, cols] = ydiag + yoff
            xw = (xp * _pair_cols(w_all, 2 * pr, lo_mask)).astype(MXU_DTYPE)
            el = jnp.where(lo_rows, e_last[:, 2 * pr:2 * pr + 1], e_last[:, 2 * pr + 1:2 * pr + 2])
            state[cols, :] = sp * el + _mm_tn(xw, bm)

    return pl.pallas_call(
        body, name=name, grid=(SSD_GROUPS, nc),
        in_specs=[pl.BlockSpec((ln, GROUP_COLS), lambda g, c: (c, g)),
                  pl.BlockSpec((ln, SSD_STATE), lambda g, c: (c, B_COL0 + g)),
                  pl.BlockSpec((ln, SSD_STATE), lambda g, c: (c, C_COL0 + g)),
                  pl.BlockSpec((1, ln, HEADS_PER_GROUP), lambda g, c: (g, c, 0)),
                  pl.BlockSpec((1, HEADS_PER_GROUP, ln), lambda g, c: (g, 0, c))],
        out_specs=[pl.BlockSpec((ln, GROUP_COLS), lambda g, c: (c, g)),
                   pl.BlockSpec((1, 1, GROUP_COLS, SSD_STATE), lambda g, c: (c, g, 0, 0))],
        out_shape=[jax.ShapeDtypeStruct((t, SSD_D_INNER), F32),
                   jax.ShapeDtypeStruct((nc, SSD_GROUPS, GROUP_COLS, SSD_STATE), F32)],
        scratch_shapes=[pltpu.VMEM((GROUP_COLS, SSD_STATE), F32)],
        compiler_params=_cp("parallel", "arbitrary"),
    )(xdt, act, act, acum_g, acum_gt)


def _ssd_scan_bwd(xdt, act, acum_g, acum_gt, states, dy, name):
    t = xdt.shape[0]
    nc = t // SSD_CHUNK
    ln = SSD_CHUNK

    def body(x_ref, b_ref, c_ref, ac_ref, act_ref, sst_ref, dy_ref, dx_ref, db_ref, dc_ref, dacol_ref, darow_ref, dstate):
        @pl.when(pl.program_id(1) == 0)
        def _():
            dstate[...] = jnp.zeros_like(dstate)

        bm = b_ref[...].astype(MXU_DTYPE)
        cm = c_ref[...].astype(MXU_DTYPE)
        cb = _mm_nt(cm, bm)
        ac, act_ = ac_ref[0], act_ref[0]
        causal = _iota((ln, ln), 1) <= _iota((ln, ln), 0)
        lo_mask = _iota((ln, SSD_PAIR), 1) < SSD_HEAD_DIM
        lo_rows = _iota((SSD_PAIR, SSD_STATE), 0) < SSD_HEAD_DIM
        lane4 = _iota((ln, HEADS_PER_GROUP), 1)
        sub4 = _iota((HEADS_PER_GROUP, ln), 0)
        is_last = _iota((ln, 1), 0) == ln - 1
        last = ac[ln - 1:ln, :]
        e_ac = jnp.exp(ac)
        w_all = jnp.exp(last - ac)
        e_last = jnp.exp(last)
        dcb = jnp.zeros((ln, ln), F32)
        dc_acc = jnp.zeros((ln, SSD_STATE), F32)
        db_acc = jnp.zeros((ln, SSD_STATE), F32)
        dacol = jnp.zeros((ln, HEADS_PER_GROUP), F32)
        darow = jnp.zeros((HEADS_PER_GROUP, ln), F32)
        for pr in range(2):
            cols = slice(pr * SSD_PAIR, (pr + 1) * SSD_PAIR)
            xp = x_ref[:, cols]
            dyp = dy_ref[:, cols]
            sp = sst_ref[0, 0, cols, :]
            dsp = dstate[cols, :]
            ea = _pair_cols(e_ac, 2 * pr, lo_mask)
            w = _pair_cols(w_all, 2 * pr, lo_mask)
            dye = (dyp * ea).astype(MXU_DTYPE)
            dx_state = w * _mm_nt(bm, dsp.astype(MXU_DTYPE))
            yoff = _mm_nt(cm, sp.astype(MXU_DTYPE)) * ea
            dxp = dx_state
            for hh in range(2):
                h = 2 * pr + hh
                mask = lo_mask if hh == 0 else jnp.logical_not(lo_mask)
                rmask = lo_rows if hh == 0 else jnp.logical_not(lo_rows)
                seg = ac[:, h:h + 1] - act_[h:h + 1, :]
                dec = jnp.exp(jnp.where(causal, seg, -1e30))
                m = cb * dec
                dym = jnp.where(mask, dyp, 0.0).astype(MXU_DTYPE)
                xm = jnp.where(mask, xp, 0.0).astype(MXU_DTYPE)
                g = _mm_nt(dym, xm)
                dxp = dxp + _mm_tn(m.astype(MXU_DTYPE), dym)
                dcb = dcb + dec * g
                mg = m * g
                rs = jnp.sum(mg, axis=1, keepdims=True)
                cs = jnp.sum(mg, axis=0, keepdims=True)
                t_off = jnp.sum(jnp.where(mask, dyp * yoff, 0.0), axis=1, keepdims=True)
                q = jnp.sum(jnp.where(mask, xp * dx_state, 0.0), axis=1, keepdims=True)
                qsum = jnp.sum(q, axis=0, keepdims=True)
                ds_s = jnp.sum(jnp.sum(jnp.where(rmask, dsp * sp, 0.0), axis=1, keepdims=True), axis=0, keepdims=True)
                extra = qsum + e_last[:, h:h + 1] * ds_s
                col = rs + t_off - q + jnp.where(is_last, extra, 0.0)
                dacol = jnp.where(lane4 == h, col, dacol)
                darow = jnp.where(sub4 == h, -cs, darow)
            dx_ref[:, cols] = dxp
            dc_acc = dc_acc + _mm(dye, sp.astype(MXU_DTYPE))
            db_acc = db_acc + _mm((xp * w).astype(MXU_DTYPE), dsp.astype(MXU_DTYPE))
            el = jnp.where(lo_rows, e_last[:, 2 * pr:2 * pr + 1], e_last[:, 2 * pr + 1:2 * pr + 2])
            dstate[cols, :] = dsp * el + _mm_tn(dye, cm)
        dcbm = dcb.astype(MXU_DTYPE)
        dc_ref[...] = _mm(dcbm, bm) + dc_acc
        db_ref[...] = _mm_tn(dcbm, cm) + db_acc
        dacol_ref[0] = dacol
        darow_ref[0] = darow

    def rev(c):
        return nc - 1 - c

    grp = pl.BlockSpec((ln, GROUP_COLS), lambda g, c: (rev(c), g))
    return pl.pallas_call(
        body, name=name, grid=(SSD_GROUPS, nc),
        in_specs=[grp,
                  pl.BlockSpec((ln, SSD_STATE), lambda g, c: (rev(c), B_COL0 + g)),
                  pl.BlockSpec((ln, SSD_STATE), lambda g, c: (rev(c), C_COL0 + g)),
                  pl.BlockSpec((1, ln, HEADS_PER_GROUP), lambda g, c: (g, rev(c), 0)),
                  pl.BlockSpec((1, HEADS_PER_GROUP, ln), lambda g, c: (g, 0, rev(c))),
                  pl.BlockSpec((1, 1, GROUP_COLS, SSD_STATE), lambda g, c: (rev(c), g, 0, 0)),
                  grp],
        out_specs=[grp,
                   pl.BlockSpec((ln, SSD_STATE), lambda g, c: (rev(c), g)),
                   pl.BlockSpec((ln, SSD_STATE), lambda g, c: (rev(c), g)),
                   pl.BlockSpec((1, ln, HEADS_PER_GROUP), lambda g, c: (g, rev(c), 0)),
                   pl.BlockSpec((1, HEADS_PER_GROUP, ln), lambda g, c: (g, 0, rev(c)))],
        out_shape=[jax.ShapeDtypeStruct((t, SSD_D_INNER), F32),
                   jax.ShapeDtypeStruct((t, SSD_GROUPS * SSD_STATE), F32),
                   jax.ShapeDtypeStruct((t, SSD_GROUPS * SSD_STATE), F32),
                   jax.ShapeDtypeStruct((SSD_GROUPS, t, HEADS_PER_GROUP), F32),
                   jax.ShapeDtypeStruct((SSD_GROUPS, HEADS_PER_GROUP, t), F32)],
        scratch_shapes=[pltpu.VMEM((GROUP_COLS, SSD_STATE), F32)],
        compiler_params=_cp("parallel", "arbitrary"),
    )(xdt, act, act, acum_g, acum_gt, states, dy)


GN_ROWS = 128


def _gated_norm_parts(y_ref, xs_ref, z_ref, dsk_ref):
    yy = y_ref[...] + dsk_ref[...] * xs_ref[...]
    z = z_ref[...]
    sz = _sigmoid(z)
    silu = z * sz
    u = yy * silu
    r = lax.rsqrt(jnp.mean(u * u, axis=-1, keepdims=True) + NORM_EPS)
    return yy, z, sz, silu, u, r


def _gated_norm_fwd(y, act, proj, d_exp, g, name):
    t = y.shape[0]
    tm = min(GN_ROWS, t)

    def body(y_ref, xs_ref, z_ref, dsk_ref, g_ref, o_ref):
        _, _, _, _, u, r = _gated_norm_parts(y_ref, xs_ref, z_ref, dsk_ref)
        o_ref[...] = (u * r * g_ref[...]).astype(o_ref.dtype)

    wide = pl.BlockSpec((tm, SSD_D_INNER), lambda i: (i, 0))
    wvec = pl.BlockSpec((1, SSD_D_INNER), lambda i: (0, 0))
    return pl.pallas_call(
        body, name=name, grid=(t // tm,), in_specs=[wide, wide, wide, wvec, wvec], out_specs=wide,
        out_shape=jax.ShapeDtypeStruct((t, SSD_D_INNER), MXU_DTYPE), compiler_params=_cp("parallel"),
    )(y, act, proj, d_exp, g.reshape(1, -1))


def _gated_norm_bwd(y, act, proj, d_exp, g, dn, name):
    t = y.shape[0]
    tm = min(GN_ROWS, t)

    def body(y_ref, xs_ref, z_ref, dsk_ref, g_ref, dn_ref, dyy_ref, dz_ref, dg_ref):
        @pl.when(pl.program_id(0) == 0)
        def _():
            dg_ref[...] = jnp.zeros_like(dg_ref)

        yy, z, sz, silu, u, r = _gated_norm_parts(y_ref, xs_ref, z_ref, dsk_ref)
        un = u * r
        dn = dn_ref[...]
        v = dn * g_ref[...]
        du = r * (v - un * jnp.mean(v * un, axis=-1, keepdims=True))
        dg_ref[...] += jnp.sum(dn * un, axis=0, keepdims=True)
        dyy_ref[...] = du * silu
        dz_ref[...] = (du * yy * (sz * (1.0 + z * (1.0 - sz)))).astype(dz_ref.dtype)

    wide = pl.BlockSpec((tm, SSD_D_INNER), lambda i: (i, 0))
    wvec = pl.BlockSpec((1, SSD_D_INNER), lambda i: (0, 0))
    return pl.pallas_call(
        body, name=name, grid=(t // tm,), in_specs=[wide, wide, wide, wvec, wvec, wide], out_specs=[wide, wide, wvec],
        out_shape=[jax.ShapeDtypeStruct((t, SSD_D_INNER), F32), jax.ShapeDtypeStruct((t, SSD_D_INNER), MXU_DTYPE),
                   jax.ShapeDtypeStruct((1, SSD_D_INNER), F32)],
        compiler_params=_cp("arbitrary"),
    )(y, act, proj, d_exp, g.reshape(1, -1), dn)


SB_PAIRS = SB_HEADS // 2


def _kv_rows(j, bt, nt=1):
    return pl.ds(pl.multiple_of(j * bt, bt), nt * bt)


def _sb_tile_masks(bt):
    lane = _iota((bt, bt), 1)
    rowi = _iota((bt, bt), 0)
    return lane < rowi, (rowi >= lane).astype(MXU_DTYPE), (rowi <= lane).astype(MXU_DTYPE)


def _sb_scaled_heads(pair, scale):
    lane = _iota(pair.shape, 1)
    val = pair.astype(F32) * scale
    return [jnp.where(lane < SB_HEAD_DIM, val, 0.0).astype(pair.dtype), jnp.where(lane >= SB_HEAD_DIM, val, 0.0).astype(pair.dtype)]


def _sb_logits(qs, kb, bt, strict):
    nt = kb.shape[0] // bt
    full = [_mm_nt(q_head, kb) for q_head in qs]
    xs, nlfs = [], []
    for x in full:
        nlf = jnp.maximum(x, 0.0) + jnp.log(1.0 + jnp.exp(-jnp.abs(x)))
        xs.append([x[:, tt * bt:(tt + 1) * bt] for tt in range(nt)])
        tiles = [nlf[:, tt * bt:(tt + 1) * bt] for tt in range(nt)]
        if strict is not None:
            tiles[-1] = jnp.where(strict, tiles[-1], 0.0)
        nlfs.append(tiles)
    return xs, nlfs


def _sb_tails(nlf_tiles, from_j):
    tails, run = [None] * len(nlf_tiles), None
    for tt in reversed(range(len(nlf_tiles))):
        tail = _mm_exact_rhs(nlf_tiles[tt], from_j)
        tails[tt] = tail if run is None else tail + run
        run = tails[tt][:, 0:1]
    return tails


def _sb_heads(e_tiles, upto_j, pre):
    sums, run = [], pre
    for e in e_tiles:
        sums.append(_mm_exact_rhs(e, upto_j) + run)
        run = sums[-1][:, e.shape[1] - 1:e.shape[1]]
    return sums


def _sb_attention_fwd(qkv, name):
    t = qkv.shape[0]
    bt = min(SB_TILE, t)
    nq = t // bt

    def body(q_ref, k_ref, v_ref, o_ref, acc_ref):
        i = pl.program_id(1)
        strict, from_j, _ = _sb_tile_masks(bt)
        qs = _sb_scaled_heads(q_ref[...], SB_SCALE)
        acc_ref[...] = jnp.zeros_like(acc_ref)

        def block(j, nt, carries, diag):
            rows = _kv_rows(j, bt, nt)
            kb, vb = k_ref[rows, :], v_ref[rows, :]
            xs, nlfs = _sb_logits(qs, kb, bt, strict if diag else None)
            tails = [_sb_tails(nlfs[hh], from_j) for hh in range(2)]
            for hh in range(2):
                ws = [jnp.exp(xs[hh][tt] - tails[hh][tt] - carries[hh]) for tt in range(nt)]
                if diag:
                    ws[-1] = jnp.where(strict, ws[-1], 0.0)
                acc_ref[hh] += _mm(jnp.concatenate([w.astype(MXU_DTYPE) for w in ws], axis=1), vb)
            return tuple(carries[hh] + tails[hh][0][:, 0:1] for hh in range(2))

        zero = jnp.zeros((bt, 1), F32)
        carries = block(i, 1, (zero, zero), True)
        carries = lax.fori_loop(0, i // 2, lambda it, cr: block(i - 2 - 2 * it, 2, cr, False), carries)

        @pl.when(i % 2 == 1)
        def _():
            block(0, 1, carries, False)

        low = _iota((bt, 2 * SB_HEAD_DIM), 1) < SB_HEAD_DIM
        o_ref[...] = jnp.where(low, acc_ref[0], acc_ref[1]).astype(o_ref.dtype)

    lanes = 2 * SB_HEAD_DIM
    return pl.pallas_call(
        body, name=name, grid=(SB_PAIRS, nq),
        in_specs=[pl.BlockSpec((bt, lanes), lambda p, i: (i, p)),
                  pl.BlockSpec((t, lanes), lambda p, i: (0, SB_PAIRS + p)),
                  pl.BlockSpec((t, lanes), lambda p, i: (0, 2 * SB_PAIRS + p))],
        out_specs=pl.BlockSpec((bt, lanes), lambda p, i: (i, p)),
        out_shape=jax.ShapeDtypeStruct((t, D_MODEL), MXU_DTYPE),
        scratch_shapes=[pltpu.VMEM((2, bt, lanes), F32)],
        compiler_params=_cp("parallel", "parallel"),
    )(qkv, qkv, qkv)


def _sb_attention_bwd(qkv, do, name):
    t = qkv.shape[0]
    bt = min(SB_TILE, t)
    nq = t // bt
    lanes = 2 * SB_HEAD_DIM

    def body(q_ref, k_ref, v_ref, do_ref, dq_ref, dk_ref, dv_ref, sbuf, ebuf, dq_acc, dk_acc, dv_acc):
        i = pl.program_id(1)

        @pl.when(i == 0)
        def _():
            dk_acc[...] = jnp.zeros_like(dk_acc)
            dv_acc[...] = jnp.zeros_like(dv_acc)

        strict, from_j, upto_j = _sb_tile_masks(bt)
        qs = _sb_scaled_heads(q_ref[...], SB_SCALE)
        dos = _sb_scaled_heads(do_ref[...], 1.0)
        q_both = jnp.concatenate(qs, axis=0)
        do_both = jnp.concatenate(dos, axis=0)
        dq_acc[...] = jnp.zeros_like(dq_acc)

        def pass1(j, nt, carries, diag):
            rows = _kv_rows(j, bt, nt)
            kb, vb = k_ref[rows, :], v_ref[rows, :]
            xs, nlfs = _sb_logits(qs, kb, bt, strict if diag else None)
            dws = [_mm_nt(dos[hh], vb) for hh in range(2)]
            tails = [_sb_tails(nlfs[hh], from_j) for hh in range(2)]
            wcat = []
            for hh in range(2):
                ws = [jnp.exp(xs[hh][tt] - tails[hh][tt] - carries[hh]) for tt in range(nt)]
                if diag:
                    ws[-1] = jnp.where(strict, ws[-1], 0.0)
                w_all = jnp.concatenate(ws, axis=1)
                sbuf[hh, :, rows] = jnp.exp(jnp.concatenate([xs[hh][tt] - nlfs[hh][tt] for tt in range(nt)], axis=1))
                ebuf[hh, :, rows] = w_all * dws[hh]
                wcat.append(w_all.astype(MXU_DTYPE))
            dv_acc[rows, :] += _mm_tn(jnp.concatenate(wcat, axis=0), do_both)
            return tuple(carries[hh] + tails[hh][0][:, 0:1] for hh in range(2))

        zero = jnp.zeros((bt, 1), F32)
        carries = pass1(i, 1, (zero, zero), True)
        carries = lax.fori_loop(0, i // 2, lambda it, cr: pass1(i - 2 - 2 * it, 2, cr, False), carries)

        @pl.when(i % 2 == 1)
        def _():
            pass1(0, 1, carries, False)

        def pass2(j, nt, pres, diag):
            rows = _kv_rows(j, bt, nt)
            kb = k_ref[rows, :]
            sums = [_sb_heads([ebuf[hh, :, _kv_rows(j + tt, bt)] for tt in range(nt)], upto_j, pres[hh]) for hh in range(2)]
            dxm = []
            for hh in range(2):
                dxs = [ebuf[hh, :, _kv_rows(j + tt, bt)] - sbuf[hh, :, _kv_rows(j + tt, bt)] * sums[hh][tt] for tt in range(nt)]
                if diag:
                    dxs[-1] = jnp.where(strict, dxs[-1], 0.0)
                dxm.append(jnp.concatenate(dxs, axis=1).astype(MXU_DTYPE))
                dq_acc[hh] += _mm(dxm[hh], kb)
            dk_acc[rows, :] += _mm_tn(jnp.concatenate(dxm, axis=0), q_both)
            return tuple(sums[hh][-1][:, bt - 1:bt] for hh in range(2))

        pres = lax.fori_loop(0, i // 2, lambda it, pr: pass2(2 * it, 2, pr, False), (zero, zero))

        @pl.when(i % 2 == 0)
        def _():
            pass2(i, 1, pres, True)

        @pl.when(i % 2 == 1)
        def _():
            pass2(i - 1, 2, pres, True)

        low = _iota((bt, lanes), 1) < SB_HEAD_DIM
        dq_ref[...] = (jnp.where(low, dq_acc[0], dq_acc[1]) * SB_SCALE).astype(dq_ref.dtype)

        @pl.when(i == nq - 1)
        def _():
            dk_ref[...] = dk_acc[...].astype(dk_ref.dtype)
            dv_ref[...] = dv_acc[...].astype(dv_ref.dtype)

    blk = pl.BlockSpec((bt, lanes), lambda p, i: (i, p))
    whole = pl.BlockSpec((t, lanes), lambda p, i: (0, p))
    out = jax.ShapeDtypeStruct((t, D_MODEL), MXU_DTYPE)
    return pl.pallas_call(
        body, name=name, grid=(SB_PAIRS, nq),
        in_specs=[blk, pl.BlockSpec((t, lanes), lambda p, i: (0, SB_PAIRS + p)),
                  pl.BlockSpec((t, lanes), lambda p, i: (0, 2 * SB_PAIRS + p)), blk],
        out_specs=[blk, whole, whole], out_shape=[out, out, out],
        scratch_shapes=[pltpu.VMEM((2, bt, t), F32), pltpu.VMEM((2, bt, t), F32), pltpu.VMEM((2, bt, lanes), F32),
                        pltpu.VMEM((t, lanes), F32), pltpu.VMEM((t, lanes), F32)],
        compiler_params=_cp("parallel", "arbitrary"),
    )(qkv, qkv, qkv, do)


def _add_pair(a, b, name):
    s, r, c = a.shape
    tm = _row_tile(r)

    def body(a_ref, b_ref, o_ref):
        o_ref[...] = (a_ref[...].astype(F32) + b_ref[...].astype(F32)).astype(o_ref.dtype)

    blk = pl.BlockSpec((1, tm, c), lambda q, i: (q, i, 0))
    return pl.pallas_call(body, name=name, grid=(s, r // tm), in_specs=[blk, blk], out_specs=blk,
                          out_shape=jax.ShapeDtypeStruct(a.shape, a.dtype), compiler_params=_cp("parallel", "parallel"))(a, b)


def _sum_slots(gslots, name):
    s, r, c = gslots.shape

    def body(g_ref, o_ref):
        g = g_ref[0].astype(F32)
        for q in range(1, s):
            g = g + g_ref[q].astype(F32)
        o_ref[...] = g

    return pl.pallas_call(
        body, name=name, grid=(c // LANES,),
        in_specs=[pl.BlockSpec((s, r, LANES), lambda j: (0, 0, j))], out_specs=pl.BlockSpec((r, LANES), lambda j: (0, j)),
        out_shape=jax.ShapeDtypeStruct((r, c), F32), compiler_params=_cp("parallel"),
    )(gslots)


def _adamw(gslots, w, m, v, name):
    s, r, c = gslots.shape
    tm = _row_tile(r)
    assert w.shape == (r, c), (w.shape, gslots.shape)
    c1 = 1.0 - ADAM_B1 ** ADAM_STEP
    c2 = 1.0 - ADAM_B2 ** ADAM_STEP

    def body(g_ref, w_ref, m_ref, v_ref, go_ref, d_ref, mo_ref, vo_ref):
        g = g_ref[0].astype(F32)
        for q in range(1, s):
            g = g + g_ref[q].astype(F32)
        mn = ADAM_B1 * m_ref[...] + (1.0 - ADAM_B1) * g
        vn = ADAM_B2 * v_ref[...] + (1.0 - ADAM_B2) * (g * g)
        go_ref[...] = g
        mo_ref[...] = mn
        vo_ref[...] = vn
        d_ref[...] = -ADAM_LR * ((mn / c1) / (jnp.sqrt(vn / c2) + ADAM_EPS) + ADAM_WD * w_ref[...])

    row = pl.BlockSpec((tm, c), lambda i: (i, 0))
    out = jax.ShapeDtypeStruct((r, c), F32)
    return pl.pallas_call(
        body, name=name, grid=(r // tm,),
        in_specs=[pl.BlockSpec((s, tm, c), lambda i: (0, i, 0)), row, row, row],
        out_specs=[row, row, row, row], out_shape=[out, out, out, out], compiler_params=_cp("parallel"),
    )(gslots, w, m, v)


def _rows(a):
    flat = a.reshape(-1)
    pad = (-flat.shape[0]) % PACK_W
    if pad:
        flat = jnp.concatenate([flat, jnp.zeros((pad,), flat.dtype)])
    return flat.reshape(-1, PACK_W)


def _pack(arrays, row_multiple):
    parts, layout, off = [], [], 0
    for a in arrays:
        rw = _rows(a)
        parts.append(rw)
        layout.append((off, rw.shape[0], a.shape))
        off += rw.shape[0]
    pad = (-off) % row_multiple
    if pad:
        parts.append(jnp.zeros((pad, PACK_W), parts[0].dtype))
    return jnp.concatenate(parts, axis=0), layout


def _unpack(packed, layout):
    out = []
    for off, nrows, shape in layout:
        n = int(np.prod(shape))
        out.append(packed[off:off + nrows].reshape(-1)[:n].reshape(shape))
    return out


def _shard_as_rows(name, shard):
    if name in COL_SHARDED:
        shard = shard.transpose(0, 2, 1)
    return shard.reshape(-1, PACK_W)


def _rows_as_shard(name, rows, shape):
    if name in COL_SHARDED:
        lead, k, ns = shape
        return rows.reshape(lead, ns, k).transpose(0, 2, 1)
    return rows.reshape(shape)


def _row_tile(r):
    return next(tm for tm in (256, 128, 64, 32, 16, 8) if r % tm == 0)


def _unshard_rows(g, lead):
    ks = g.shape[1] // lead
    return g.reshape(N_DEV, lead, ks, PACK_W).transpose(1, 0, 2, 3).reshape(lead, N_DEV * ks, PACK_W)


def _shard_rows(full):
    lead, k, n = full.shape
    return full.reshape(lead, N_DEV, k // N_DEV, n).transpose(1, 0, 2, 3)


def _ssd_consts(dt_bias, a_log, d_skip):
    pad = LANES - SSD_HEADS
    bias = jnp.pad(dt_bias, (0, pad)).reshape(1, LANES)
    a_neg = jnp.pad(-jnp.exp(a_log), (0, pad)).reshape(1, LANES)
    d_exp = jnp.repeat(d_skip, SSD_HEAD_DIM).reshape(1, SSD_D_INNER)
    return bias, a_neg, d_exp


def _group_layouts(acum):
    t = acum.shape[0]
    a = acum[:, :SSD_HEADS].reshape(t, SSD_GROUPS, HEADS_PER_GROUP)
    return a.transpose(1, 0, 2), a.transpose(1, 2, 0)


def _ssd_fwd(x, p):
    hn = _rmsnorm(x, p["mix_norm"], "rmsnorm_fwd")
    proj = _matmul(hn, p["w_in"], "nt", F32, "ssd_in_fwd", tm=512, tn=896, tk=1024)
    act = _ssd_conv_fwd(proj, p["conv_w"], p["conv_b"], "ssd_conv_fwd")
    bias, a_neg, d_exp = _ssd_consts(p["dt_bias"], p["a_log"], p["d"])
    expand = _head_expand()
    xdt, dt, acum = _ssd_dt_fwd(proj, act, bias, a_neg, expand, "ssd_dt_fwd")
    acum_g, acum_gt = _group_layouts(acum)
    y, states = _ssd_scan_fwd(xdt, act, acum_g, acum_gt, "ssd_scan_fwd")
    yn = _gated_norm_fwd(y, act, proj, d_exp, p["norm"], "ssd_gnorm_fwd")
    x_new = _matmul(yn, p["w_out"], "nn", F32, "ssd_out_fwd", add=x, tm=512, tn=1024, tk=2048)
    saved = dict(x=x, hn=hn, proj=proj, act=act, xdt=xdt, dt=dt, acum_g=acum_g, acum_gt=acum_gt, y=y, states=states, yn=yn)
    return x_new, saved


def _ssd_bwd(dx, p, s):
    bias, a_neg, d_exp = _ssd_consts(p["dt_bias"], p["a_log"], p["d"])
    expand = _head_expand()
    dyn = _matmul(dx, p["w_out"], "nt", F32, "ssd_out_dgrad", tm=512, tn=1024, tk=1024)
    g_w_out = _matmul(s["yn"], dx, "tn", MXU_DTYPE, "ssd_out_wgrad", tm=1024, tn=1024, tk=512)
    dyy, dz, g_norm = _gated_norm_bwd(s["y"], s["act"], s["proj"], d_exp, p["norm"], dyn, "ssd_gnorm_bwd")
    dxdt, dbm, dcm, dacol, darow = _ssd_scan_bwd(s["xdt"], s["act"], s["acum_g"], s["acum_gt"], s["states"], dyy, "ssd_scan_bwd")
    t = dx.shape[0]
    dacum = dacol.transpose(1, 0, 2).reshape(t, SSD_HEADS) + darow.transpose(2, 0, 1).reshape(t, SSD_HEADS)
    dacum = jnp.pad(dacum, ((0, 0), (0, LANES - SSD_HEADS)))
    dxs, draw, g_a, g_bias, g_dexp = _ssd_dt_bwd(s["proj"], s["act"], s["dt"], dxdt, dyy, dacum, bias, a_neg, d_exp,
                                                  expand, expand.T, "ssd_dt_bwd")
    dact = jnp.concatenate([dxs, dbm, dcm], axis=1)
    dxbc, g_conv_w, g_conv_b = _ssd_conv_bwd(s["proj"], p["conv_w"], p["conv_b"], dact, "ssd_conv_bwd")
    dproj = jnp.concatenate([dz, dxbc, draw], axis=1)
    dhn = _matmul(dproj, p["w_in"], "nn", F32, "ssd_in_dgrad", tm=512, tn=1024, tk=896)
    g_w_in = _matmul(dproj, s["hn"], "tn", MXU_DTYPE, "ssd_in_wgrad", tm=896, tn=1024, tk=512)
    dx_new, g_mix = _rmsnorm_bwd(s["x"], p["mix_norm"], dhn, dx, "rmsnorm_bwd")
    grads = dict(w_in=g_w_in[:SSD_IN_DIM], w_out=g_w_out, conv_w=g_conv_w, conv_b=g_conv_b.reshape(-1),
                 dt_bias=g_bias[0, :SSD_HEADS], a_log=(g_a * a_neg)[0, :SSD_HEADS],
                 d=g_dexp.reshape(SSD_HEADS, SSD_HEAD_DIM).sum(axis=1), norm=g_norm.reshape(-1), mix_norm=g_mix.reshape(-1))
    return dx_new, grads


def _sb_fwd(x, p):
    hn = _rmsnorm(x, p["mix_norm"], "rmsnorm_fwd")
    qkv = _matmul(hn, p["w_qkv"], "nt", MXU_DTYPE, "sb_qkv_fwd", tm=512, tn=1024, tk=1024)
    o = _sb_attention_fwd(qkv, "sb_attn_fwd")
    x_new = _matmul(o, p["w_out"], "nn", F32, "sb_out_fwd", add=x, tm=512, tn=1024, tk=1024)
    return x_new, dict(x=x, hn=hn, qkv=qkv, o=o)


def _sb_bwd(dx, p, s):
    do = _matmul(dx, p["w_out"], "nt", MXU_DTYPE, "sb_out_dgrad", tm=512, tn=1024, tk=1024)
    g_w_out = _matmul(s["o"], dx, "tn", MXU_DTYPE, "sb_out_wgrad", tm=1024, tn=1024, tk=512)
    dq, dk, dv = _sb_attention_bwd(s["qkv"], do, "sb_attn_bwd")
    dqkv = jnp.concatenate([dq, dk, dv], axis=1)
    dhn = _matmul(dqkv, p["w_qkv"], "nn", F32, "sb_qkv_dgrad", tm=512, tn=1024, tk=1024)
    g_w_qkv = _matmul(dqkv, s["hn"], "tn", MXU_DTYPE, "sb_qkv_wgrad", tm=1024, tn=1024, tk=512)
    dx_new, g_mix = _rmsnorm_bwd(s["x"], p["mix_norm"], dhn, dx, "rmsnorm_bwd")
    return dx_new, dict(w_qkv=g_w_qkv, w_out=g_w_out, mix_norm=g_mix.reshape(-1))


def _ffn_fwd(x, p):
    hn = _rmsnorm(x, p["ffn_norm"], "rmsnorm_fwd")
    proj = _matmul(hn, p["w_in"], "nt", F32, "ffn_in_fwd", tm=512, tn=1408, tk=1024)
    act = _ffn_conv_fwd(proj, p["conv_w"], p["conv_b"], "ffn_conv_fwd")
    x_new = _matmul(act, p["w_out"], "nn", F32, "ffn_out_fwd", add=x, tm=512, tn=1024, tk=1408)
    return x_new, dict(x=x, hn=hn, proj=proj, act=act)


def _ffn_bwd(dx, p, s):
    dact = _matmul(dx, p["w_out"], "nt", F32, "ffn_out_dgrad", tm=512, tn=1408, tk=1024)
    g_w_out = _matmul(s["act"], dx, "tn", MXU_DTYPE, "ffn_out_wgrad", tm=1408, tn=1024, tk=512)
    dpg, dpu, dwg, dwu, dbg, dbu = _ffn_conv_bwd(s["proj"], p["conv_w"], p["conv_b"], dact, "ffn_conv_bwd")
    dproj = jnp.concatenate([dpg, dpu], axis=1)
    dhn = _matmul(dproj, p["w_in"], "nn", F32, "ffn_in_dgrad", tm=512, tn=1024, tk=1408)
    g_w_in = _matmul(dproj, s["hn"], "tn", MXU_DTYPE, "ffn_in_wgrad", tm=1408, tn=1024, tk=512)
    dx_new, g_norm = _rmsnorm_bwd(s["x"], p["ffn_norm"], dhn, dx, "rmsnorm_bwd")
    grads = dict(w_in=g_w_in, w_out=g_w_out, conv_w=jnp.concatenate([dwg, dwu], axis=1),
                 conv_b=jnp.concatenate([dbg, dbu], axis=1).reshape(-1), ffn_norm=g_norm.reshape(-1))
    return dx_new, grads


BIG_ROW_MULTIPLE = 256
BIG = ["ssd_w_in", "sb_w_qkv", "ffn_w_in", "ssd_w_out", "sb_w_out", "ffn_w_out"]
COL_SHARDED = {"ssd_w_in": 2, "sb_w_qkv": 2, "ffn_w_in": 4}
ROW_SHARDED = {"ssd_w_out": 2, "sb_w_out": 2, "ffn_w_out": 4}
CONV = ["ssd_conv_w", "ffn_conv_w"]
SMALL = ["mix_norm", "ffn_norm", "final_norm", "ssd_conv_b", "ssd_dt_bias", "ssd_a_log", "ssd_d", "ssd_norm", "ffn_conv_b"]
WEIGHTS = ["mix_norm", "ffn_norm", "final_norm", "ssd_w_in", "ssd_conv_w", "ssd_conv_b", "ssd_dt_bias", "ssd_a_log", "ssd_d",
           "ssd_norm", "ssd_w_out", "sb_w_qkv", "sb_w_out", "ffn_w_in", "ffn_conv_w", "ffn_conv_b", "ffn_w_out"]


def _step(x, loss_target, w, m, v):
    x = x.reshape(x.shape[-2], x.shape[-1])
    target = loss_target.reshape(x.shape)
    dev = 4 * lax.axis_index("x") + 2 * lax.axis_index("y") + lax.axis_index("c")
    core = lax.axis_index("c")

    big_pack, big_layout = _pack([_shard_as_rows(n, w[n].astype(MXU_DTYPE)) for n in BIG], BIG_ROW_MULTIPLE)
    big_all = _all_gather(big_pack, "gather_weights")
    full = {}
    for n, (off, nrows, _) in zip(BIG, big_layout):
        full[n] = _unshard_rows(big_all[:, off:off + nrows], w[n].shape[0])
    full["ssd_w_in"] = jnp.pad(full["ssd_w_in"], ((0, 0), (0, SSD_IN_PAD - SSD_IN_DIM), (0, 0)))
    conv_pack, conv_layout = _pack([w[n] for n in CONV], 8)
    conv_all = _all_gather(conv_pack, "gather_conv_taps")
    for n, (off, nrows, shape) in zip(CONV, conv_layout):
        parts = [_unpack(conv_all[j], conv_layout)[CONV.index(n)] for j in range(N_DEV)]
        full[n] = jnp.concatenate(parts, axis=-1)

    def ssd_params(j):
        return dict(mix_norm=w["mix_norm"][2 * j], w_in=full["ssd_w_in"][j], conv_w=full["ssd_conv_w"][j],
                    conv_b=w["ssd_conv_b"][j], dt_bias=w["ssd_dt_bias"][j], a_log=w["ssd_a_log"][j], d=w["ssd_d"][j],
                    norm=w["ssd_norm"][j], w_out=full["ssd_w_out"][j])

    def sb_params(j):
        return dict(mix_norm=w["mix_norm"][2 * j + 1], w_qkv=full["sb_w_qkv"][j], w_out=full["sb_w_out"][j])

    def ffn_params(i):
        return dict(ffn_norm=w["ffn_norm"][i], w_in=full["ffn_w_in"][i], conv_w=full["ffn_conv_w"][i],
                    conv_b=w["ffn_conv_b"][i], w_out=full["ffn_w_out"][i])

    saved = []
    for i in range(DEPTH):
        if i % 2 == 0:
            x, s_mix = _ssd_fwd(x, ssd_params(i // 2))
        else:
            x, s_mix = _sb_fwd(x, sb_params(i // 2))
        x, s_ffn = _ffn_fwd(x, ffn_params(i))
        saved.append((s_mix, s_ffn))
    dx, g_final, loss_part = _final_norm_loss(x, w["final_norm"], target, "final_norm_loss")

    g_mix, g_ffn, g_ssd, g_sb = [None] * DEPTH, [None] * DEPTH, [None] * 2, [None] * 2
    for i in reversed(range(DEPTH)):
        s_mix, s_ffn = saved[i]
        dx, g_ffn[i] = _ffn_bwd(dx, ffn_params(i), s_ffn)
        if i % 2 == 0:
            dx, g_ssd[i // 2] = _ssd_bwd(dx, ssd_params(i // 2), s_mix)
            g_mix[i] = g_ssd[i // 2]["mix_norm"]
        else:
            dx, g_sb[i // 2] = _sb_bwd(dx, sb_params(i // 2), s_mix)
            g_mix[i] = g_sb[i // 2]["mix_norm"]
    grad_x = dx.reshape(1, *dx.shape)

    gfull = {
        "ssd_w_in": jnp.stack([g["w_in"] for g in g_ssd]), "sb_w_qkv": jnp.stack([g["w_qkv"] for g in g_sb]),
        "ffn_w_in": jnp.stack([g["w_in"] for g in g_ffn]), "ssd_w_out": jnp.stack([g["w_out"] for g in g_ssd]),
        "sb_w_out": jnp.stack([g["w_out"] for g in g_sb]), "ffn_w_out": jnp.stack([g["w_out"] for g in g_ffn]),
    }
    rtot = big_pack.shape[0]
    chunks = [_shard_rows(gfull[n]).reshape(N_DEV, nrows, PACK_W) for n, (off, nrows, _) in zip(BIG, big_layout)]
    pad_rows = rtot - sum(c_.shape[1] for c_ in chunks)
    if pad_rows:
        chunks.append(jnp.zeros((N_DEV, pad_rows, PACK_W), MXU_DTYPE))
    g8 = jnp.concatenate(chunks, axis=1).reshape(4, 2, rtot, PACK_W)
    keep = lax.dynamic_index_in_dim(g8, core, axis=1, keepdims=False)
    give = lax.dynamic_index_in_dim(g8, 1 - core, axis=1, keepdims=False)
    got = _swap_with_sibling(give, "grads_to_sibling")
    chip_part = _add_pair(keep, got, "grads_add_sibling")
    from_chips = _exchange_chips(chip_part, "grads_across_chips")
    layout_of = dict(zip(BIG, big_layout))
    col_rows = sum(layout_of[n][1] for n in COL_SHARDED)
    row_rows = sum(layout_of[n][1] for n in ROW_SHARDED)
    big_res = [dict() for _ in range(4)]
    col_sum = _sum_slots(from_chips[:, :col_rows], "grads_sum_chips")
    for n in COL_SHARDED:
        off, nrows, _ = layout_of[n]
        lead, k, ns = w[n].shape
        g_nat = _rows_as_shard(n, col_sum[off:off + nrows], w[n].shape).reshape(1, lead * k, ns)
        outs = _adamw(g_nat, w[n].reshape(lead * k, ns), m[n].reshape(lead * k, ns), v[n].reshape(lead * k, ns), "adamw_" + n)
        for kind in range(4):
            big_res[kind][n] = outs[kind].reshape(w[n].shape)
    row_names = list(ROW_SHARDED)
    w_pack, row_layout = _pack([w[n] for n in row_names], 8)
    m_pack, _ = _pack([m[n] for n in row_names], 8)
    v_pack, _ = _pack([v[n] for n in row_names], 8)
    row_out = _adamw(from_chips[:, col_rows:col_rows + row_rows], w_pack, m_pack, v_pack, "adamw_down_projections")
    for kind in range(4):
        big_res[kind].update(zip(row_names, _unpack(row_out[kind], row_layout)))

    small_g = {
        "mix_norm": jnp.stack(g_mix), "ffn_norm": jnp.stack([g["ffn_norm"] for g in g_ffn]), "final_norm": g_final.reshape(-1),
        "ssd_conv_b": jnp.stack([g["conv_b"] for g in g_ssd]), "ssd_dt_bias": jnp.stack([g["dt_bias"] for g in g_ssd]),
        "ssd_a_log": jnp.stack([g["a_log"] for g in g_ssd]), "ssd_d": jnp.stack([g["d"] for g in g_ssd]),
        "ssd_norm": jnp.stack([g["norm"] for g in g_ssd]), "ffn_conv_b": jnp.stack([g["conv_b"] for g in g_ffn]),
    }
    conv_g = {"ssd_conv_w": jnp.stack([g["conv_w"] for g in g_ssd]), "ffn_conv_w": jnp.stack([g["conv_w"] for g in g_ffn])}
    extra = [conv_g[n] for n in CONV] + [loss_part]
    small_pack, small_layout = _pack([small_g[n] for n in SMALL] + extra, 8)
    small_all = _all_gather(small_pack, "gather_small_grads")
    zeros_like = [jnp.zeros(a.shape, F32) for a in extra]
    sw, _ = _pack([w[n] for n in SMALL] + zeros_like, 8)
    sm, _ = _pack([m[n] for n in SMALL] + zeros_like, 8)
    sv, _ = _pack([v[n] for n in SMALL] + [jnp.ones(a.shape, F32) for a in extra], 8)
    small_out = _adamw(small_all, sw, sm, sv, "adamw_replicated")
    small_res = [_unpack(o, small_layout) for o in small_out]
    summed = small_res[0]
    loss = summed[-1][0, 0]
    conv_shard_g = []
    for n, gsum in zip(CONV, summed[len(SMALL):len(SMALL) + len(CONV)]):
        ns = w[n].shape[-1]
        conv_shard_g.append(lax.dynamic_slice_in_dim(gsum, dev * ns, ns, axis=2))
    cg, conv_sh_layout = _pack(conv_shard_g, 8)
    cw, _ = _pack([w[n] for n in CONV], 8)
    cm_, _ = _pack([m[n] for n in CONV], 8)
    cv, _ = _pack([v[n] for n in CONV], 8)
    conv_out = _adamw(cg.reshape(1, *cg.shape), cw, cm_, cv, "adamw_conv_taps")
    conv_res = [dict(zip(CONV, _unpack(o, conv_sh_layout))) for o in conv_out]

    def pick(kind, n):
        if n in BIG:
            return big_res[kind][n]
        if n in CONV:
            return conv_res[kind][n]
        return small_res[kind][SMALL.index(n)]

    outs = [loss, grad_x]
    for kind in range(4):
        outs += [pick(kind, n) for n in WEIGHTS]
    return tuple(outs)


def kernel(x, mix_norm, ffn_norm, final_norm, ssd_w_in, ssd_conv_w, ssd_conv_b, ssd_dt_bias, ssd_a_log, ssd_d, ssd_norm, ssd_w_out, sb_w_qkv, sb_w_out, ffn_w_in, ffn_conv_w, ffn_conv_b, ffn_w_out, loss_target, m_mix_norm, m_ffn_norm, m_final_norm, m_ssd_w_in, m_ssd_conv_w, m_ssd_conv_b, m_ssd_dt_bias, m_ssd_a_log, m_ssd_d, m_ssd_norm, m_ssd_w_out, m_sb_w_qkv, m_sb_w_out, m_ffn_w_in, m_ffn_conv_w, m_ffn_conv_b, m_ffn_w_out, v_mix_norm, v_ffn_norm, v_final_norm, v_ssd_w_in, v_ssd_conv_w, v_ssd_conv_b, v_ssd_dt_bias, v_ssd_a_log, v_ssd_d, v_ssd_norm, v_ssd_w_out, v_sb_w_qkv, v_sb_w_out, v_ffn_w_in, v_ffn_conv_w, v_ffn_conv_b, v_ffn_w_out):
    w = dict(mix_norm=mix_norm, ffn_norm=ffn_norm, final_norm=final_norm, ssd_w_in=ssd_w_in, ssd_conv_w=ssd_conv_w,
             ssd_conv_b=ssd_conv_b, ssd_dt_bias=ssd_dt_bias, ssd_a_log=ssd_a_log, ssd_d=ssd_d, ssd_norm=ssd_norm,
             ssd_w_out=ssd_w_out, sb_w_qkv=sb_w_qkv, sb_w_out=sb_w_out, ffn_w_in=ffn_w_in, ffn_conv_w=ffn_conv_w,
             ffn_conv_b=ffn_conv_b, ffn_w_out=ffn_w_out)
    m = dict(mix_norm=m_mix_norm, ffn_norm=m_ffn_norm, final_norm=m_final_norm, ssd_w_in=m_ssd_w_in, ssd_conv_w=m_ssd_conv_w,
             ssd_conv_b=m_ssd_conv_b, ssd_dt_bias=m_ssd_dt_bias, ssd_a_log=m_ssd_a_log, ssd_d=m_ssd_d, ssd_norm=m_ssd_norm,
             ssd_w_out=m_ssd_w_out, sb_w_qkv=m_sb_w_qkv, sb_w_out=m_sb_w_out, ffn_w_in=m_ffn_w_in, ffn_conv_w=m_ffn_conv_w,
             ffn_conv_b=m_ffn_conv_b, ffn_w_out=m_ffn_w_out)
    v = dict(mix_norm=v_mix_norm, ffn_norm=v_ffn_norm, final_norm=v_final_norm, ssd_w_in=v_ssd_w_in, ssd_conv_w=v_ssd_conv_w,
             ssd_conv_b=v_ssd_conv_b, ssd_dt_bias=v_ssd_dt_bias, ssd_a_log=v_ssd_a_log, ssd_d=v_ssd_d, ssd_norm=v_ssd_norm,
             ssd_w_out=v_ssd_w_out, sb_w_qkv=v_sb_w_qkv, sb_w_out=v_sb_w_out, ffn_w_in=v_ffn_w_in, ffn_conv_w=v_ffn_conv_w,
             ffn_conv_b=v_ffn_conv_b, ffn_w_out=v_ffn_w_out)
    return _step(x, loss_target, w, m, v)
```

```python
import functools

import jax
import jax.numpy as jnp
import numpy as np
from jax import lax
from jax.experimental import pallas as pl
from jax.experimental.pallas import tpu as pltpu

F32 = jnp.float32
MXU_DTYPE = jnp.bfloat16
MESH_ID = pl.DeviceIdType.MESH
N_DEV = 8

NORM_EPS = 1e-6
D_MODEL = 1024
DEPTH = 4
SSD_D_INNER = 2048
SSD_HEADS = 32
SSD_HEAD_DIM = 64
SSD_GROUPS = 8
SSD_STATE = 128
SSD_CONV = 4
SSD_CHUNK = 128
SSD_CONV_DIM = SSD_D_INNER + 2 * SSD_GROUPS * SSD_STATE
SSD_IN_DIM = SSD_D_INNER + SSD_CONV_DIM + SSD_HEADS
LANES = 128
SSD_IN_PAD = SSD_D_INNER + SSD_CONV_DIM + LANES
SB_HEADS = 16
SB_HEAD_DIM = 64
SB_TILE = 256
SB_SCALE = SB_HEAD_DIM ** -0.5
FFN_D_FF = 2816
FFN_CONV = 3
PACK_W = 1024

ADAM_LR = 0.001
ADAM_B1 = 0.9
ADAM_B2 = 0.999
ADAM_EPS = 1e-08
ADAM_WD = 0.01
ADAM_STEP = 10

VMEM_LIMIT_BYTES = 56 * 1024 * 1024


def _cp(*sem):
    return pltpu.CompilerParams(dimension_semantics=sem, vmem_limit_bytes=VMEM_LIMIT_BYTES)


def _iota(shape, dim):
    return lax.broadcasted_iota(jnp.int32, shape, dim)


def _sigmoid(x):
    return 1.0 / (1.0 + jnp.exp(-x))


def _mm(a, b):
    return lax.dot_general(a, b, (((1,), (0,)), ((), ())), preferred_element_type=F32)


def _mm_nt(a, b):
    return lax.dot_general(a, b, (((1,), (1,)), ((), ())), preferred_element_type=F32)


def _mm_tn(a, b):
    return lax.dot_general(a, b, (((0,), (0,)), ((), ())), preferred_element_type=F32)


def _split(x):
    hi = x.astype(MXU_DTYPE)
    lo = (x - hi.astype(F32)).astype(MXU_DTYPE)
    return hi, lo


def _mm_exact_rhs(x, m):
    hi, lo = _split(x)
    return _mm(jnp.concatenate([hi, lo], axis=1), jnp.concatenate([m, m], axis=0))


def _mm_exact_lhs(m, x):
    hi, lo = _split(x)
    return _mm(jnp.concatenate([m, m], axis=1), jnp.concatenate([hi, lo], axis=0))


def _my_place():
    return lax.axis_index("x"), lax.axis_index("y"), lax.axis_index("c")


def _all_gather(shard, name):
    r, c_ = shard.shape

    def body(x_ref, out_ref, send_sems, recv_sems, local_sem):
        x, y, c = _my_place()
        me, sibling = (x, y, c), (x, y, 1 - c)
        chips = [(1 - x, y), (x, 1 - y), (1 - x, 1 - y)]

        def slot(px, py, pc):
            return out_ref.at[4 * px + 2 * py + pc]

        def copy(k, block, to, src=None):
            return pltpu.make_async_remote_copy(
                src_ref=slot(*block) if src is None else src, dst_ref=slot(*block),
                send_sem=send_sems.at[k], recv_sem=recv_sems.at[k], device_id=to, device_id_type=MESH_ID)

        mine = pltpu.make_async_copy(x_ref, slot(*me), local_sem)
        mine.start()
        first = [copy(0, me, sibling, src=x_ref)]
        first += [copy(1 + j, me, (*chip, c), src=x_ref) for j, chip in enumerate(chips)]
        for cp in first:
            cp.start()
        passed = [copy(4 + j, (*chip, c), sibling) for j, chip in enumerate(chips)]
        for j, chip in enumerate(chips):
            copy(1 + j, (*chip, c), me).wait_recv()
            passed[j].start()
        copy(0, sibling, me).wait_recv()
        for j, chip in enumerate(chips):
            copy(4 + j, (*chip, 1 - c), me).wait_recv()
        for cp in first + passed:
            cp.wait_send()
        mine.wait()

    return pl.pallas_call(
        body, name=name,
        out_shape=jax.ShapeDtypeStruct((N_DEV, r, c_), shard.dtype),
        in_specs=[pl.BlockSpec(memory_space=pl.ANY)],
        out_specs=pl.BlockSpec(memory_space=pl.ANY),
        scratch_shapes=[pltpu.SemaphoreType.DMA((7,)), pltpu.SemaphoreType.DMA((7,)), pltpu.SemaphoreType.DMA(())],
    )(shard)


def _swap_with_sibling(buf, name):
    def body(x_ref, out_ref, send_sem, recv_sem):
        x, y, c = _my_place()
        cp = pltpu.make_async_remote_copy(src_ref=x_ref, dst_ref=out_ref, send_sem=send_sem, recv_sem=recv_sem,
                                          device_id=(x, y, 1 - c), device_id_type=MESH_ID)
        cp.start()
        cp.wait()

    return pl.pallas_call(
        body, name=name, out_shape=jax.ShapeDtypeStruct(buf.shape, buf.dtype),
        in_specs=[pl.BlockSpec(memory_space=pl.ANY)], out_specs=pl.BlockSpec(memory_space=pl.ANY),
        scratch_shapes=[pltpu.SemaphoreType.DMA(()), pltpu.SemaphoreType.DMA(())],
    )(buf)


def _exchange_chips(parts, name):
    def body(p_ref, out_ref, send_sems, recv_sems, local_sem):
        x, y, c = _my_place()
        my_q = 2 * x + y
        chips = [(1 - x, y), (x, 1 - y), (1 - x, 1 - y)]
        local = pltpu.make_async_copy(p_ref.at[my_q], out_ref.at[my_q], local_sem)
        local.start()

        def copy(k, px, py):
            return pltpu.make_async_remote_copy(
                src_ref=p_ref.at[2 * px + py], dst_ref=out_ref.at[my_q],
                send_sem=send_sems.at[k], recv_sem=recv_sems.at[k], device_id=(px, py, c), device_id_type=MESH_ID)

        def landing(k, px, py):
            return pltpu.make_async_remote_copy(
                src_ref=p_ref.at[my_q], dst_ref=out_ref.at[2 * px + py],
                send_sem=send_sems.at[k], recv_sem=recv_sems.at[k], device_id=(px, py, c), device_id_type=MESH_ID)

        sends = [copy(k, px, py) for k, (px, py) in enumerate(chips)]
        for cp in sends:
            cp.start()
        for k, (px, py) in enumerate(chips):
            landing(k, px, py).wait_recv()
        for cp in sends:
            cp.wait_send()
        local.wait()

    return pl.pallas_call(
        body, name=name, out_shape=jax.ShapeDtypeStruct(parts.shape, parts.dtype),
        in_specs=[pl.BlockSpec(memory_space=pl.ANY)], out_specs=pl.BlockSpec(memory_space=pl.ANY),
        scratch_shapes=[pltpu.SemaphoreType.DMA((3,)), pltpu.SemaphoreType.DMA((3,)), pltpu.SemaphoreType.DMA(())],
    )(parts)


RELATIONS = [(0, 0, 1), (1, 0, 0), (0, 1, 0), (1, 1, 0), (1, 0, 1), (0, 1, 1), (1, 1, 1)]
EXCHANGE_SCRATCH = [pltpu.SemaphoreType.DMA((len(RELATIONS),)), pltpu.SemaphoreType.DMA((len(RELATIONS),)),
                    pltpu.SemaphoreType.DMA(())]


def _exchange_copies(src_ref, land_ref, send_sems, recv_sems, local_sem, per_peer):
    x, y, c = _my_place()
    me = 4 * x + 2 * y + c

    def src(j):
        return src_ref.at[j] if per_peer else src_ref

    local = pltpu.make_async_copy(src(me), land_ref.at[me], local_sem)
    pairs = []
    for k, (dx, dy, dc) in enumerate(RELATIONS):
        peer = (1 - x if dx else x, 1 - y if dy else y, 1 - c if dc else c)
        j = 4 * peer[0] + 2 * peer[1] + peer[2]
        sems = dict(send_sem=send_sems.at[k], recv_sem=recv_sems.at[k], device_id=peer, device_id_type=MESH_ID)
        pairs.append((pltpu.make_async_remote_copy(src_ref=src(j), dst_ref=land_ref.at[me], **sems),
                      pltpu.make_async_remote_copy(src_ref=src(me), dst_ref=land_ref.at[j], **sems)))
    return local, pairs


def _exchange_start(*refs, per_peer):
    local, pairs = _exchange_copies(*refs, per_peer)
    local.start()
    for outgoing, _ in pairs:
        outgoing.start()


def _exchange_finish(*refs, per_peer):
    local, pairs = _exchange_copies(*refs, per_peer)
    for _, incoming in pairs:
        incoming.wait_recv()
    for outgoing, _ in pairs:
        outgoing.wait_send()
    local.wait()


def _matmul(a, b, mode, out_dtype, name, add=None, tm=512, tn=512, tk=512):
    if mode == "nn":
        (m, k), (k2, n) = a.shape, b.shape
    elif mode == "nt":
        (m, k), (n, k2) = a.shape, b.shape
    else:
        (k, m), (k2, n) = a.shape, b.shape
    assert k == k2, (a.shape, b.shape, mode)
    tm, tn, tk = min(tm, m), min(tn, n), min(tk, k)
    assert m % tm == 0 and n % tn == 0 and k % tk == 0, (m, n, k, tm, tn, tk)
    nk = k // tk
    mm = {"nn": _mm, "nt": _mm_nt, "tn": _mm_tn}[mode]

    def body(*refs):
        if add is None:
            a_ref, b_ref, o_ref, acc_ref = refs
        else:
            a_ref, b_ref, add_ref, o_ref, acc_ref = refs
        kk = pl.program_id(2)

        @pl.when(kk == 0)
        def _():
            acc_ref[...] = jnp.zeros_like(acc_ref)

        acc_ref[...] += mm(a_ref[...].astype(MXU_DTYPE), b_ref[...].astype(MXU_DTYPE))

        @pl.when(kk == nk - 1)
        def _():
            res = acc_ref[...]
            if add is not None:
                res = res + add_ref[...]
            o_ref[...] = res.astype(o_ref.dtype)

    a_spec = {"nn": pl.BlockSpec((tm, tk), lambda i, j, kk: (i, kk)),
              "nt": pl.BlockSpec((tm, tk), lambda i, j, kk: (i, kk)),
              "tn": pl.BlockSpec((tk, tm), lambda i, j, kk: (kk, i))}[mode]
    b_spec = {"nn": pl.BlockSpec((tk, tn), lambda i, j, kk: (kk, j)),
              "nt": pl.BlockSpec((tn, tk), lambda i, j, kk: (j, kk)),
              "tn": pl.BlockSpec((tk, tn), lambda i, j, kk: (kk, j))}[mode]
    o_spec = pl.BlockSpec((tm, tn), lambda i, j, kk: (i, j))
    in_specs, args = [a_spec, b_spec], [a, b]
    if add is not None:
        in_specs.append(o_spec)
        args.append(add)
    return pl.pallas_call(
        body, name=name, grid=(m // tm, n // tn, nk), in_specs=in_specs, out_specs=o_spec,
        out_shape=jax.ShapeDtypeStruct((m, n), out_dtype),
        scratch_shapes=[pltpu.VMEM((tm, tn), F32)],
        compiler_params=_cp("parallel", "parallel", "arbitrary"),
    )(*args)


def _rmsnorm(x, g, name):
    t, d = x.shape
    tm = min(512, t)

    def body(x_ref, g_ref, o_ref):
        xv = x_ref[...]
        r = lax.rsqrt(jnp.mean(xv * xv, axis=-1, keepdims=True) + NORM_EPS)
        o_ref[...] = (xv * r * g_ref[...]).astype(o_ref.dtype)

    return pl.pallas_call(
        body, name=name, grid=(t // tm,),
        in_specs=[pl.BlockSpec((tm, d), lambda i: (i, 0)), pl.BlockSpec((1, d), lambda i: (0, 0))],
        out_specs=pl.BlockSpec((tm, d), lambda i: (i, 0)),
        out_shape=jax.ShapeDtypeStruct((t, d), MXU_DTYPE), compiler_params=_cp("parallel"),
    )(x, g.reshape(1, d))


def _rmsnorm_bwd(x, g, dh, dres, name):
    t, d = x.shape
    tm = min(512, t)

    def body(x_ref, g_ref, dh_ref, dres_ref, dx_ref, dg_ref):
        @pl.when(pl.program_id(0) == 0)
        def _():
            dg_ref[...] = jnp.zeros_like(dg_ref)

        xv = x_ref[...]
        r = lax.rsqrt(jnp.mean(xv * xv, axis=-1, keepdims=True) + NORM_EPS)
        xn = xv * r
        dhv = dh_ref[...]
        u = dhv * g_ref[...]
        dx_ref[...] = dres_ref[...] + r * (u - xn * jnp.mean(u * xn, axis=-1, keepdims=True))
        dg_ref[...] += jnp.sum(dhv * xn, axis=0, keepdims=True)

    row = pl.BlockSpec((tm, d), lambda i: (i, 0))
    vec = pl.BlockSpec((1, d), lambda i: (0, 0))
    return pl.pallas_call(
        body, name=name, grid=(t // tm,), in_specs=[row, vec, row, row], out_specs=[row, vec],
        out_shape=[jax.ShapeDtypeStruct((t, d), F32), jax.ShapeDtypeStruct((1, d), F32)],
        compiler_params=_cp("arbitrary"),
    )(x, g.reshape(1, d), dh, dres)


def _final_norm_loss(x, g, target, name):
    t, d = x.shape
    tm = min(512, t)

    def body(x_ref, g_ref, t_ref, dx_ref, dg_ref, loss_ref):
        @pl.when(pl.program_id(0) == 0)
        def _():
            dg_ref[...] = jnp.zeros_like(dg_ref)
            loss_ref[...] = jnp.zeros_like(loss_ref)

        xv = x_ref[...]
        gv = g_ref[...]
        r = lax.rsqrt(jnp.mean(xv * xv, axis=-1, keepdims=True) + NORM_EPS)
        xn = xv * r
        err = xn * gv - t_ref[...]
        per_tok = jnp.mean(err * err, axis=-1, keepdims=True)
        loss_ref[...] += jnp.broadcast_to(0.5 * jnp.sum(per_tok, axis=0, keepdims=True), loss_ref.shape)
        dy = err * (1.0 / d)
        u = dy * gv
        dx_ref[...] = r * (u - xn * jnp.mean(u * xn, axis=-1, keepdims=True))
        dg_ref[...] += jnp.sum(dy * xn, axis=0, keepdims=True)

    row = pl.BlockSpec((tm, d), lambda i: (i, 0))
    vec = pl.BlockSpec((1, d), lambda i: (0, 0))
    return pl.pallas_call(
        body, name=name, grid=(t // tm,), in_specs=[row, vec, row],
        out_specs=[row, vec, pl.BlockSpec((1, LANES), lambda i: (0, 0))],
        out_shape=[jax.ShapeDtypeStruct((t, d), F32), jax.ShapeDtypeStruct((1, d), F32),
                   jax.ShapeDtypeStruct((1, LANES), F32)],
        compiler_params=_cp("arbitrary"),
    )(x, g.reshape(1, d), target)


CONV_COLS = 128


def _shifts_down(p, width):
    row = _iota(p.shape, 0)
    return [jnp.where(row >= s, pltpu.roll(p, s, axis=0), 0.0) for s in range(1, width)]


def _shifts_up(p, width):
    n = p.shape[0]
    row = _iota(p.shape, 0)
    return [jnp.where(row < n - s, pltpu.roll(p, n - s, axis=0), 0.0) for s in range(1, width)]


def _conv_pre(p, shifted, w_ref, b_ref):
    width = w_ref.shape[0]
    u = b_ref[...] + w_ref[width - 1:width, :] * p
    for s in range(1, width):
        u = u + w_ref[width - 1 - s:width - s, :] * shifted[s - 1]
    return u


def _conv_transpose(du, w_ref):
    width = w_ref.shape[0]
    shifted = _shifts_up(du, width)
    dp = w_ref[width - 1:width, :] * du
    for s in range(1, width):
        dp = dp + w_ref[width - 1 - s:width - s, :] * shifted[s - 1]
    return dp


def _conv_wgrad(du, p, shifted, dw_ref, db_ref):
    width = dw_ref.shape[0]
    db_ref[...] = jnp.sum(du, axis=0, keepdims=True)
    dw_ref[width - 1:width, :] = jnp.sum(du * p, axis=0, keepdims=True)
    for s in range(1, width):
        dw_ref[width - 1 - s:width - s, :] = jnp.sum(du * shifted[s - 1], axis=0, keepdims=True)


def _ssd_conv_fwd(proj, w, b, name):
    t = proj.shape[0]
    cb = CONV_COLS
    off = SSD_D_INNER // cb

    def body(p_ref, w_ref, b_ref, o_ref):
        p = p_ref[...]
        u = _conv_pre(p, _shifts_down(p, SSD_CONV), w_ref, b_ref)
        o_ref[...] = u * _sigmoid(u)

    return pl.pallas_call(
        body, name=name, grid=(SSD_CONV_DIM // cb,),
        in_specs=[pl.BlockSpec((t, cb), lambda j: (0, j + off)), pl.BlockSpec((SSD_CONV, cb), lambda j: (0, j)),
                  pl.BlockSpec((1, cb), lambda j: (0, j))],
        out_specs=pl.BlockSpec((t, cb), lambda j: (0, j)),
        out_shape=jax.ShapeDtypeStruct((t, SSD_CONV_DIM), F32), compiler_params=_cp("parallel"),
    )(proj, w, b.reshape(1, -1))


def _ssd_conv_bwd(proj, w, b, dact, name):
    t = proj.shape[0]
    cb = CONV_COLS
    off = SSD_D_INNER // cb

    def body(p_ref, w_ref, b_ref, da_ref, dp_ref, dw_ref, db_ref):
        p = p_ref[...]
        shifted = _shifts_down(p, SSD_CONV)
        u = _conv_pre(p, shifted, w_ref, b_ref)
        sg = _sigmoid(u)
        du = da_ref[...] * (sg * (1.0 + u * (1.0 - sg)))
        dp_ref[...] = _conv_transpose(du, w_ref).astype(dp_ref.dtype)
        _conv_wgrad(du, p, shifted, dw_ref, db_ref)

    col = pl.BlockSpec((t, cb), lambda j: (0, j))
    wspec = pl.BlockSpec((SSD_CONV, cb), lambda j: (0, j))
    bspec = pl.BlockSpec((1, cb), lambda j: (0, j))
    return pl.pallas_call(
        body, name=name, grid=(SSD_CONV_DIM // cb,),
        in_specs=[pl.BlockSpec((t, cb), lambda j: (0, j + off)), wspec, bspec, col],
        out_specs=[col, wspec, bspec],
        out_shape=[jax.ShapeDtypeStruct((t, SSD_CONV_DIM), MXU_DTYPE), jax.ShapeDtypeStruct((SSD_CONV, SSD_CONV_DIM), F32),
                   jax.ShapeDtypeStruct((1, SSD_CONV_DIM), F32)],
        compiler_params=_cp("parallel"),
    )(proj, w, b.reshape(1, -1), dact)


def _ffn_conv_fwd(proj, w, b, name):
    t = proj.shape[0]
    cb = CONV_COLS
    nb = FFN_D_FF // cb

    def body(pg_ref, pu_ref, wg_ref, wu_ref, bg_ref, bu_ref, o_ref):
        pg, pu = pg_ref[...], pu_ref[...]
        ug = _conv_pre(pg, _shifts_down(pg, FFN_CONV), wg_ref, bg_ref)
        uu = _conv_pre(pu, _shifts_down(pu, FFN_CONV), wu_ref, bu_ref)
        o_ref[...] = (ug * _sigmoid(ug) * uu).astype(o_ref.dtype)

    gcol = pl.BlockSpec((t, cb), lambda j: (0, j))
    ucol = pl.BlockSpec((t, cb), lambda j: (0, j + nb))
    b2 = b.reshape(1, -1)
    return pl.pallas_call(
        body, name=name, grid=(nb,),
        in_specs=[gcol, ucol, pl.BlockSpec((FFN_CONV, cb), lambda j: (0, j)), pl.BlockSpec((FFN_CONV, cb), lambda j: (0, j + nb)),
                  pl.BlockSpec((1, cb), lambda j: (0, j)), pl.BlockSpec((1, cb), lambda j: (0, j + nb))],
        out_specs=gcol, out_shape=jax.ShapeDtypeStruct((t, FFN_D_FF), MXU_DTYPE), compiler_params=_cp("parallel"),
    )(proj, proj, w, w, b2, b2)


def _ffn_conv_bwd(proj, w, b, dact, name):
    t = proj.shape[0]
    cb = CONV_COLS
    nb = FFN_D_FF // cb

    def body(pg_ref, pu_ref, wg_ref, wu_ref, bg_ref, bu_ref, da_ref,
             dpg_ref, dpu_ref, dwg_ref, dwu_ref, dbg_ref, dbu_ref):
        pg, pu = pg_ref[...], pu_ref[...]
        pg_shifted, pu_shifted = _shifts_down(pg, FFN_CONV), _shifts_down(pu, FFN_CONV)
        ug = _conv_pre(pg, pg_shifted, wg_ref, bg_ref)
        uu = _conv_pre(pu, pu_shifted, wu_ref, bu_ref)
        sg = _sigmoid(ug)
        da = da_ref[...]
        dug = da * uu * (sg * (1.0 + ug * (1.0 - sg)))
        duu = da * (ug * sg)
        dpg_ref[...] = _conv_transpose(dug, wg_ref).astype(dpg_ref.dtype)
        dpu_ref[...] = _conv_transpose(duu, wu_ref).astype(dpu_ref.dtype)
        _conv_wgrad(dug, pg, pg_shifted, dwg_ref, dbg_ref)
        _conv_wgrad(duu, pu, pu_shifted, dwu_ref, dbu_ref)

    gcol = pl.BlockSpec((t, cb), lambda j: (0, j))
    ucol = pl.BlockSpec((t, cb), lambda j: (0, j + nb))
    wg = pl.BlockSpec((FFN_CONV, cb), lambda j: (0, j))
    wu = pl.BlockSpec((FFN_CONV, cb), lambda j: (0, j + nb))
    bg = pl.BlockSpec((1, cb), lambda j: (0, j))
    bu = pl.BlockSpec((1, cb), lambda j: (0, j + nb))
    b2 = b.reshape(1, -1)
    half = jax.ShapeDtypeStruct((t, FFN_D_FF), MXU_DTYPE)
    return pl.pallas_call(
        body, name=name, grid=(nb,),
        in_specs=[gcol, ucol, wg, wu, bg, bu, gcol],
        out_specs=[gcol, gcol, wg, wg, bg, bg],
        out_shape=[half, half, jax.ShapeDtypeStruct((FFN_CONV, FFN_D_FF), F32), jax.ShapeDtypeStruct((FFN_CONV, FFN_D_FF), F32),
                   jax.ShapeDtypeStruct((1, FFN_D_FF), F32), jax.ShapeDtypeStruct((1, FFN_D_FF), F32)],
        compiler_params=_cp("parallel"),
    )(proj, proj, w, w, b2, b2, dact)


SSD_ROWS = 128
DT_COL = (SSD_D_INNER + SSD_CONV_DIM) // LANES


def _head_expand():
    h = np.arange(LANES)[:, None]
    col = np.arange(SSD_D_INNER)[None, :]
    return jnp.asarray((col // SSD_HEAD_DIM == h), MXU_DTYPE)


def _chunk_tri(n, lower):
    t = _iota((n, n), 0)
    s = _iota((n, n), 1)
    shift = SSD_CHUNK.bit_length() - 1
    same = jnp.right_shift(t, shift) == jnp.right_shift(s, shift)
    tri = (s <= t) if lower else (s >= t)
    return jnp.where(same & tri, 1.0, 0.0).astype(MXU_DTYPE)


def _softplus(x):
    return jnp.maximum(x, 0.0) + jnp.log(1.0 + jnp.exp(-jnp.abs(x)))


def _ssd_dt_fwd(proj, act, dt_bias, a_neg, expand, name):
    t = proj.shape[0]
    tm = min(SSD_ROWS, t)

    def body(raw_ref, xs_ref, bias_ref, a_ref, e_ref, xdt_ref, dt_ref, acum_ref):
        lane = _iota((tm, LANES), 1)
        dt = jnp.where(lane < SSD_HEADS, _softplus(raw_ref[...] + bias_ref[...]), 0.0)
        dt_ref[...] = dt
        xdt_ref[...] = xs_ref[...] * _mm_exact_rhs(dt, e_ref[...])
        acum_ref[...] = _mm_exact_lhs(_chunk_tri(tm, True), a_ref[...] * dt)

    vec = pl.BlockSpec((1, LANES), lambda i: (0, 0))
    return pl.pallas_call(
        body, name=name, grid=(t // tm,),
        in_specs=[pl.BlockSpec((tm, LANES), lambda i: (i, DT_COL)), pl.BlockSpec((tm, SSD_D_INNER), lambda i: (i, 0)),
                  vec, vec, pl.BlockSpec((LANES, SSD_D_INNER), lambda i: (0, 0))],
        out_specs=[pl.BlockSpec((tm, SSD_D_INNER), lambda i: (i, 0)), pl.BlockSpec((tm, LANES), lambda i: (i, 0)),
                   pl.BlockSpec((tm, LANES), lambda i: (i, 0))],
        out_shape=[jax.ShapeDtypeStruct((t, SSD_D_INNER), F32), jax.ShapeDtypeStruct((t, LANES), F32),
                   jax.ShapeDtypeStruct((t, LANES), F32)],
        compiler_params=_cp("parallel"),
    )(proj, act, dt_bias, a_neg, expand)


def _ssd_dt_bwd(proj, act, dt, dxdt, dyy, dacum, dt_bias, a_neg, d_exp, expand, expand_t, name):
    t = proj.shape[0]
    tm = min(SSD_ROWS, t)

    def body(raw_ref, xs_ref, dt_ref, dxdt_ref, dyy_ref, dac_ref, bias_ref, a_ref, dsk_ref, e_ref, et_ref,
             dxs_ref, draw_ref, da_ref, dbias_ref, dd_ref):
        @pl.when(pl.program_id(0) == 0)
        def _():
            da_ref[...] = jnp.zeros_like(da_ref)
            dbias_ref[...] = jnp.zeros_like(dbias_ref)
            dd_ref[...] = jnp.zeros_like(dd_ref)

        lane = _iota((tm, LANES), 1)
        xs, dt, dxdt, dyy = xs_ref[...], dt_ref[...], dxdt_ref[...], dyy_ref[...]
        dxs_ref[...] = dxdt * _mm_exact_rhs(dt, e_ref[...]) + dsk_ref[...] * dyy
        dd_ref[...] += jnp.sum(dyy * xs, axis=0, keepdims=True)
        ddt = _mm_exact_rhs(dxdt * xs, et_ref[...])
        da = _mm_exact_lhs(_chunk_tri(tm, False), dac_ref[...])
        ddt = ddt + da * a_ref[...]
        da_ref[...] += jnp.sum(da * dt, axis=0, keepdims=True)
        draw = jnp.where(lane < SSD_HEADS, ddt * _sigmoid(raw_ref[...] + bias_ref[...]), 0.0)
        dbias_ref[...] += jnp.sum(draw, axis=0, keepdims=True)
        draw_ref[...] = draw.astype(draw_ref.dtype)

    wide = pl.BlockSpec((tm, SSD_D_INNER), lambda i: (i, 0))
    thin = pl.BlockSpec((tm, LANES), lambda i: (i, 0))
    vec = pl.BlockSpec((1, LANES), lambda i: (0, 0))
    wvec = pl.BlockSpec((1, SSD_D_INNER), lambda i: (0, 0))
    return pl.pallas_call(
        body, name=name, grid=(t // tm,),
        in_specs=[pl.BlockSpec((tm, LANES), lambda i: (i, DT_COL)), wide, thin, wide, wide, thin, vec, vec, wvec,
                  pl.BlockSpec((LANES, SSD_D_INNER), lambda i: (0, 0)), pl.BlockSpec((SSD_D_INNER, LANES), lambda i: (0, 0))],
        out_specs=[wide, thin, vec, vec, wvec],
        out_shape=[jax.ShapeDtypeStruct((t, SSD_D_INNER), F32), jax.ShapeDtypeStruct((t, LANES), MXU_DTYPE),
                   jax.ShapeDtypeStruct((1, LANES), F32), jax.ShapeDtypeStruct((1, LANES), F32),
                   jax.ShapeDtypeStruct((1, SSD_D_INNER), F32)],
        compiler_params=_cp("arbitrary"),
    )(proj, act, dt, dxdt, dyy, dacum, dt_bias, a_neg, d_exp, expand, expand_t)


SSD_PAIR = 2 * SSD_HEAD_DIM
HEADS_PER_GROUP = SSD_HEADS // SSD_GROUPS
GROUP_COLS = HEADS_PER_GROUP * SSD_HEAD_DIM
B_COL0 = SSD_D_INNER // SSD_STATE
C_COL0 = (SSD_D_INNER + SSD_GROUPS * SSD_STATE) // SSD_STATE


def _pair_cols(vals, h0, lo_mask):
    return jnp.where(lo_mask, vals[:, h0:h0 + 1], vals[:, h0 + 1:h0 + 2])


def _ssd_scan_fwd(xdt, act, acum_g, acum_gt, name):
    t = xdt.shape[0]
    nc = t // SSD_CHUNK
    ln = SSD_CHUNK

    def body(x_ref, b_ref, c_ref, ac_ref, act_ref, y_ref, sst_ref, state):
        @pl.when(pl.program_id(1) == 0)
        def _():
            state[...] = jnp.zeros_like(state)

        bm = b_ref[...].astype(MXU_DTYPE)
        cm = c_ref[...].astype(MXU_DTYPE)
        cb = _mm_nt(cm, bm)
        ac, act_ = ac_ref[0], act_ref[0]
        causal = _iota((ln, ln), 1) <= _iota((ln, ln), 0)
        lo_mask = _iota((ln, SSD_PAIR), 1) < SSD_HEAD_DIM
        lo_rows = _iota((SSD_PAIR, SSD_STATE), 0) < SSD_HEAD_DIM
        sst_ref[0, 0] = state[...]
        last = ac[ln - 1:ln, :]
        e_ac = jnp.exp(ac)
        w_all = jnp.exp(last - ac)
        e_last = jnp.exp(last)
        for pr in range(2):
            cols = slice(pr * SSD_PAIR, (pr + 1) * SSD_PAIR)
            xp = x_ref[:, cols]
            sp = state[cols, :]
            ydiag = jnp.zeros((ln, SSD_PAIR), F32)
            for hh in range(2):
                h = 2 * pr + hh
                seg = ac[:, h:h + 1] - act_[h:h + 1, :]
                dec = jnp.exp(jnp.where(causal, seg, -1e30))
                mask = lo_mask if hh == 0 else jnp.logical_not(lo_mask)
                ydiag = ydiag + _mm((cb * dec).astype(MXU_DTYPE), jnp.where(mask, xp, 0.0).astype(MXU_DTYPE))
            yoff = _mm_nt(cm, sp.astype(MXU_DTYPE)) * _pair_cols(e_ac, 2 * pr, lo_mask)
            y_ref[:, cols] = ydiag + yoff
            xw = (xp * _pair_cols(w_all, 2 * pr, lo_mask)).astype(MXU_DTYPE)
            el = jnp.where(lo_rows, e_last[:, 2 * pr:2 * pr + 1], e_last[:, 2 * pr + 1:2 * pr + 2])
            state[cols, :] = sp * el + _mm_tn(xw, bm)

    return pl.pallas_call(
        body, name=name, grid=(SSD_GROUPS, nc),
        in_specs=[pl.BlockSpec((ln, GROUP_COLS), lambda g, c: (c, g)),
                  pl.BlockSpec((ln, SSD_STATE), lambda g, c: (c, B_COL0 + g)),
                  pl.BlockSpec((ln, SSD_STATE), lambda g, c: (c, C_COL0 + g)),
                  pl.BlockSpec((1, ln, HEADS_PER_GROUP), lambda g, c: (g, c, 0)),
                  pl.BlockSpec((1, HEADS_PER_GROUP, ln), lambda g, c: (g, 0, c))],
        out_specs=[pl.BlockSpec((ln, GROUP_COLS), lambda g, c: (c, g)),
                   pl.BlockSpec((1, 1, GROUP_COLS, SSD_STATE), lambda g, c: (c, g, 0, 0))],
        out_shape=[jax.ShapeDtypeStruct((t, SSD_D_INNER), F32),
                   jax.ShapeDtypeStruct((nc, SSD_GROUPS, GROUP_COLS, SSD_STATE), F32)],
        scratch_shapes=[pltpu.VMEM((GROUP_COLS, SSD_STATE), F32)],
        compiler_params=_cp("parallel", "arbitrary"),
    )(xdt, act, act, acum_g, acum_gt)


def _ssd_scan_bwd(xdt, act, acum_g, acum_gt, states, dy, name):
    t = xdt.shape[0]
    nc = t // SSD_CHUNK
    ln = SSD_CHUNK

    def body(x_ref, b_ref, c_ref, ac_ref, act_ref, sst_ref, dy_ref, dx_ref, db_ref, dc_ref, dacol_ref, darow_ref, dstate):
        @pl.when(pl.program_id(1) == 0)
        def _():
            dstate[...] = jnp.zeros_like(dstate)

        bm = b_ref[...].astype(MXU_DTYPE)
        cm = c_ref[...].astype(MXU_DTYPE)
        cb = _mm_nt(cm, bm)
        ac, act_ = ac_ref[0], act_ref[0]
        causal = _iota((ln, ln), 1) <= _iota((ln, ln), 0)
        lo_mask = _iota((ln, SSD_PAIR), 1) < SSD_HEAD_DIM
        lo_rows = _iota((SSD_PAIR, SSD_STATE), 0) < SSD_HEAD_DIM
        lane4 = _iota((ln, HEADS_PER_GROUP), 1)
        sub4 = _iota((HEADS_PER_GROUP, ln), 0)
        is_last = _iota((ln, 1), 0) == ln - 1
        last = ac[ln - 1:ln, :]
        e_ac = jnp.exp(ac)
        w_all = jnp.exp(last - ac)
        e_last = jnp.exp(last)
        dcb = jnp.zeros((ln, ln), F32)
        dc_acc = jnp.zeros((ln, SSD_STATE), F32)
        db_acc = jnp.zeros((ln, SSD_STATE), F32)
        dacol = jnp.zeros((ln, HEADS_PER_GROUP), F32)
        darow = jnp.zeros((HEADS_PER_GROUP, ln), F32)
        for pr in range(2):
            cols = slice(pr * SSD_PAIR, (pr + 1) * SSD_PAIR)
            xp = x_ref[:, cols]
            dyp = dy_ref[:, cols]
            sp = sst_ref[0, 0, cols, :]
            dsp = dstate[cols, :]
            ea = _pair_cols(e_ac, 2 * pr, lo_mask)
            w = _pair_cols(w_all, 2 * pr, lo_mask)
            dye = (dyp * ea).astype(MXU_DTYPE)
            dx_state = w * _mm_nt(bm, dsp.astype(MXU_DTYPE))
            yoff = _mm_nt(cm, sp.astype(MXU_DTYPE)) * ea
            dxp = dx_state
            for hh in range(2):
                h = 2 * pr + hh
                mask = lo_mask if hh == 0 else jnp.logical_not(lo_mask)
                rmask = lo_rows if hh == 0 else jnp.logical_not(lo_rows)
                seg = ac[:, h:h + 1] - act_[h:h + 1, :]
                dec = jnp.exp(jnp.where(causal, seg, -1e30))
                m = cb * dec
                dym = jnp.where(mask, dyp, 0.0).astype(MXU_DTYPE)
                xm = jnp.where(mask, xp, 0.0).astype(MXU_DTYPE)
                g = _mm_nt(dym, xm)
                dxp = dxp + _mm_tn(m.astype(MXU_DTYPE), dym)
                dcb = dcb + dec * g
                mg = m * g
                rs = jnp.sum(mg, axis=1, keepdims=True)
                cs = jnp.sum(mg, axis=0, keepdims=True)
                t_off = jnp.sum(jnp.where(mask, dyp * yoff, 0.0), axis=1, keepdims=True)
                q = jnp.sum(jnp.where(mask, xp * dx_state, 0.0), axis=1, keepdims=True)
                qsum = jnp.sum(q, axis=0, keepdims=True)
                ds_s = jnp.sum(jnp.sum(jnp.where(rmask, dsp * sp, 0.0), axis=1, keepdims=True), axis=0, keepdims=True)
                extra = qsum + e_last[:, h:h + 1] * ds_s
                col = rs + t_off - q + jnp.where(is_last, extra, 0.0)
                dacol = jnp.where(lane4 == h, col, dacol)
                darow = jnp.where(sub4 == h, -cs, darow)
            dx_ref[:, cols] = dxp
            dc_acc = dc_acc + _mm(dye, sp.astype(MXU_DTYPE))
            db_acc = db_acc + _mm((xp * w).astype(MXU_DTYPE), dsp.astype(MXU_DTYPE))
            el = jnp.where(lo_rows, e_last[:, 2 * pr:2 * pr + 1], e_last[:, 2 * pr + 1:2 * pr + 2])
            dstate[cols, :] = dsp * el + _mm_tn(dye, cm)
        dcbm = dcb.astype(MXU_DTYPE)
        dc_ref[...] = _mm(dcbm, bm) + dc_acc
        db_ref[...] = _mm_tn(dcbm, cm) + db_acc
        dacol_ref[0] = dacol
        darow_ref[0] = darow

    def rev(c):
        return nc - 1 - c

    grp = pl.BlockSpec((ln, GROUP_COLS), lambda g, c: (rev(c), g))
    return pl.pallas_call(
        body, name=name, grid=(SSD_GROUPS, nc),
        in_specs=[grp,
                  pl.BlockSpec((ln, SSD_STATE), lambda g, c: (rev(c), B_COL0 + g)),
                  pl.BlockSpec((ln, SSD_STATE), lambda g, c: (rev(c), C_COL0 + g)),
                  pl.BlockSpec((1, ln, HEADS_PER_GROUP), lambda g, c: (g, rev(c), 0)),
                  pl.BlockSpec((1, HEADS_PER_GROUP, ln), lambda g, c: (g, 0, rev(c))),
                  pl.BlockSpec((1, 1, GROUP_COLS, SSD_STATE), lambda g, c: (rev(c), g, 0, 0)),
                  grp],
        out_specs=[grp,
                   pl.BlockSpec((ln, SSD_STATE), lambda g, c: (rev(c), g)),
                   pl.BlockSpec((ln, SSD_STATE), lambda g, c: (rev(c), g)),
                   pl.BlockSpec((1, ln, HEADS_PER_GROUP), lambda g, c: (g, rev(c), 0)),
                   pl.BlockSpec((1, HEADS_PER_GROUP, ln), lambda g, c: (g, 0, rev(c)))],
        out_shape=[jax.ShapeDtypeStruct((t, SSD_D_INNER), F32),
                   jax.ShapeDtypeStruct((t, SSD_GROUPS * SSD_STATE), F32),
                   jax.ShapeDtypeStruct((t, SSD_GROUPS * SSD_STATE), F32),
                   jax.ShapeDtypeStruct((SSD_GROUPS, t, HEADS_PER_GROUP), F32),
                   jax.ShapeDtypeStruct((SSD_GROUPS, HEADS_PER_GROUP, t), F32)],
        scratch_shapes=[pltpu.VMEM((GROUP_COLS, SSD_STATE), F32)],
        compiler_params=_cp("parallel", "arbitrary"),
    )(xdt, act, act, acum_g, acum_gt, states, dy)


GN_ROWS = 128


def _gated_norm_parts(y_ref, xs_ref, z_ref, dsk_ref):
    yy = y_ref[...] + dsk_ref[...] * xs_ref[...]
    z = z_ref[...]
    sz = _sigmoid(z)
    silu = z * sz
    u = yy * silu
    r = lax.rsqrt(jnp.mean(u * u, axis=-1, keepdims=True) + NORM_EPS)
    return yy, z, sz, silu, u, r


def _gated_norm_fwd(y, act, proj, d_exp, g, name):
    t = y.shape[0]
    tm = min(GN_ROWS, t)

    def body(y_ref, xs_ref, z_ref, dsk_ref, g_ref, o_ref):
        _, _, _, _, u, r = _gated_norm_parts(y_ref, xs_ref, z_ref, dsk_ref)
        o_ref[...] = (u * r * g_ref[...]).astype(o_ref.dtype)

    wide = pl.BlockSpec((tm, SSD_D_INNER), lambda i: (i, 0))
    wvec = pl.BlockSpec((1, SSD_D_INNER), lambda i: (0, 0))
    return pl.pallas_call(
        body, name=name, grid=(t // tm,), in_specs=[wide, wide, wide, wvec, wvec], out_specs=wide,
        out_shape=jax.ShapeDtypeStruct((t, SSD_D_INNER), MXU_DTYPE), compiler_params=_cp("parallel"),
    )(y, act, proj, d_exp, g.reshape(1, -1))


def _gated_norm_bwd(y, act, proj, d_exp, g, dn, name):
    t = y.shape[0]
    tm = min(GN_ROWS, t)

    def body(y_ref, xs_ref, z_ref, dsk_ref, g_ref, dn_ref, dyy_ref, dz_ref, dg_ref):
        @pl.when(pl.program_id(0) == 0)
        def _():
            dg_ref[...] = jnp.zeros_like(dg_ref)

        yy, z, sz, silu, u, r = _gated_norm_parts(y_ref, xs_ref, z_ref, dsk_ref)
        un = u * r
        dn = dn_ref[...]
        v = dn * g_ref[...]
        du = r * (v - un * jnp.mean(v * un, axis=-1, keepdims=True))
        dg_ref[...] += jnp.sum(dn * un, axis=0, keepdims=True)
        dyy_ref[...] = du * silu
        dz_ref[...] = (du * yy * (sz * (1.0 + z * (1.0 - sz)))).astype(dz_ref.dtype)

    wide = pl.BlockSpec((tm, SSD_D_INNER), lambda i: (i, 0))
    wvec = pl.BlockSpec((1, SSD_D_INNER), lambda i: (0, 0))
    return pl.pallas_call(
        body, name=name, grid=(t // tm,), in_specs=[wide, wide, wide, wvec, wvec, wide], out_specs=[wide, wide, wvec],
        out_shape=[jax.ShapeDtypeStruct((t, SSD_D_INNER), F32), jax.ShapeDtypeStruct((t, SSD_D_INNER), MXU_DTYPE),
                   jax.ShapeDtypeStruct((1, SSD_D_INNER), F32)],
        compiler_params=_cp("arbitrary"),
    )(y, act, proj, d_exp, g.reshape(1, -1), dn)


SB_PAIRS = SB_HEADS // 2


def _kv_rows(j, bt, nt=1):
    return pl.ds(pl.multiple_of(j * bt, bt), nt * bt)


def _sb_tile_masks(bt):
    lane = _iota((bt, bt), 1)
    rowi = _iota((bt, bt), 0)
    return lane < rowi, (rowi >= lane).astype(MXU_DTYPE), (rowi <= lane).astype(MXU_DTYPE)


def _sb_scaled_heads(pair, scale):
    lane = _iota(pair.shape, 1)
    val = pair.astype(F32) * scale
    return [jnp.where(lane < SB_HEAD_DIM, val, 0.0).astype(pair.dtype), jnp.where(lane >= SB_HEAD_DIM, val, 0.0).astype(pair.dtype)]


def _sb_logits(qs, kb, bt, strict):
    nt = kb.shape[0] // bt
    full = [_mm_nt(q_head, kb) for q_head in qs]
    xs, nlfs = [], []
    for x in full:
        nlf = jnp.maximum(x, 0.0) + jnp.log(1.0 + jnp.exp(-jnp.abs(x)))
        xs.append([x[:, tt * bt:(tt + 1) * bt] for tt in range(nt)])
        tiles = [nlf[:, tt * bt:(tt + 1) * bt] for tt in range(nt)]
        if strict is not None:
            tiles[-1] = jnp.where(strict, tiles[-1], 0.0)
        nlfs.append(tiles)
    return xs, nlfs


def _sb_tails(nlf_tiles, from_j):
    tails, run = [None] * len(nlf_tiles), None
    for tt in reversed(range(len(nlf_tiles))):
        tail = _mm_exact_rhs(nlf_tiles[tt], from_j)
        tails[tt] = tail if run is None else tail + run
        run = tails[tt][:, 0:1]
    return tails


def _sb_heads(e_tiles, upto_j, pre):
    sums, run = [], pre
    for e in e_tiles:
        sums.append(_mm_exact_rhs(e, upto_j) + run)
        run = sums[-1][:, e.shape[1] - 1:e.shape[1]]
    return sums


def _carried_specs(carried):
    if carried is None:
        return [], [], [], []
    src, per_peer = carried
    rows = src.shape[1:] if per_peer else src.shape
    anywhere = pl.BlockSpec(memory_space=pl.ANY)
    return [anywhere], [src], [anywhere], [jax.ShapeDtypeStruct((N_DEV, *rows), src.dtype)]


def _carried_hooks(carried, comm_refs, first, last):
    if carried is None:
        return lambda: None

    @pl.when(first)
    def _():
        _exchange_start(*comm_refs, per_peer=carried[1])

    def finish():
        @pl.when(last)
        def _():
            _exchange_finish(*comm_refs, per_peer=carried[1])

    return finish


def _sb_attention_fwd(qkv, name, carried=None):
    t = qkv.shape[0]
    bt = min(SB_TILE, t)
    nq = t // bt
    c_in_specs, c_in, c_out_specs, c_out = _carried_specs(carried)

    def body(*refs):
        if carried is None:
            q_ref, k_ref, v_ref, o_ref, acc_ref = refs
            comm_refs = None
        else:
            q_ref, k_ref, v_ref, src_ref, o_ref, land_ref, acc_ref, send_sems, recv_sems, local_sem = refs
            comm_refs = (src_ref, land_ref, send_sems, recv_sems, local_sem)
        i = pl.program_id(1)
        finish = _carried_hooks(carried, comm_refs, (pl.program_id(0) == 0) & (i == 0),
                                (pl.program_id(0) == SB_PAIRS - 1) & (i == nq - 1))
        strict, from_j, _ = _sb_tile_masks(bt)
        qs = _sb_scaled_heads(q_ref[...], SB_SCALE)
        acc_ref[...] = jnp.zeros_like(acc_ref)

        def block(j, nt, carries, diag):
            rows = _kv_rows(j, bt, nt)
            kb, vb = k_ref[rows, :], v_ref[rows, :]
            xs, nlfs = _sb_logits(qs, kb, bt, strict if diag else None)
            tails = [_sb_tails(nlfs[hh], from_j) for hh in range(2)]
            for hh in range(2):
                ws = [jnp.exp(xs[hh][tt] - tails[hh][tt] - carries[hh]) for tt in range(nt)]
                if diag:
                    ws[-1] = jnp.where(strict, ws[-1], 0.0)
                acc_ref[hh] += _mm(jnp.concatenate([w.astype(MXU_DTYPE) for w in ws], axis=1), vb)
            return tuple(carries[hh] + tails[hh][0][:, 0:1] for hh in range(2))

        zero = jnp.zeros((bt, 1), F32)
        carries = block(i, 1, (zero, zero), True)
        carries = lax.fori_loop(0, i // 2, lambda it, cr: block(i - 2 - 2 * it, 2, cr, False), carries)

        @pl.when(i % 2 == 1)
        def _():
            block(0, 1, carries, False)

        low = _iota((bt, 2 * SB_HEAD_DIM), 1) < SB_HEAD_DIM
        o_ref[...] = jnp.where(low, acc_ref[0], acc_ref[1]).astype(o_ref.dtype)
        finish()

    lanes = 2 * SB_HEAD_DIM
    res = pl.pallas_call(
        body, name=name, grid=(SB_PAIRS, nq),
        in_specs=[pl.BlockSpec((bt, lanes), lambda p, i: (i, p)),
                  pl.BlockSpec((t, lanes), lambda p, i: (0, SB_PAIRS + p)),
                  pl.BlockSpec((t, lanes), lambda p, i: (0, 2 * SB_PAIRS + p))] + c_in_specs,
        out_specs=[pl.BlockSpec((bt, lanes), lambda p, i: (i, p))] + c_out_specs,
        out_shape=[jax.ShapeDtypeStruct((t, D_MODEL), MXU_DTYPE)] + c_out,
        scratch_shapes=[pltpu.VMEM((2, bt, lanes), F32)] + (EXCHANGE_SCRATCH if carried else []),
        compiler_params=_cp("arbitrary", "arbitrary"),
    )(qkv, qkv, qkv, *c_in)
    return res[0] if carried is None else res


def _sb_attention_bwd(qkv, do, name, carried=None):
    t = qkv.shape[0]
    bt = min(SB_TILE, t)
    nq = t // bt
    lanes = 2 * SB_HEAD_DIM
    c_in_specs, c_in, c_out_specs, c_out = _carried_specs(carried)

    def body(*refs):
        if carried is None:
            q_ref, k_ref, v_ref, do_ref, dq_ref, dk_ref, dv_ref, sbuf, ebuf, dq_acc, dk_acc, dv_acc = refs
            comm_refs = None
        else:
            (q_ref, k_ref, v_ref, do_ref, src_ref, dq_ref, dk_ref, dv_ref, land_ref,
             sbuf, ebuf, dq_acc, dk_acc, dv_acc, send_sems, recv_sems, local_sem) = refs
            comm_refs = (src_ref, land_ref, send_sems, recv_sems, local_sem)
        i = pl.program_id(1)
        finish = _carried_hooks(carried, comm_refs, (pl.program_id(0) == 0) & (i == 0),
                                (pl.program_id(0) == SB_PAIRS - 1) & (i == nq - 1))

        @pl.when(i == 0)
        def _():
            dk_acc[...] = jnp.zeros_like(dk_acc)
            dv_acc[...] = jnp.zeros_like(dv_acc)

        strict, from_j, upto_j = _sb_tile_masks(bt)
        qs = _sb_scaled_heads(q_ref[...], SB_SCALE)
        dos = _sb_scaled_heads(do_ref[...], 1.0)
        q_both = jnp.concatenate(qs, axis=0)
        do_both = jnp.concatenate(dos, axis=0)
        dq_acc[...] = jnp.zeros_like(dq_acc)

        def pass1(j, nt, carries, diag):
            rows = _kv_rows(j, bt, nt)
            kb, vb = k_ref[rows, :], v_ref[rows, :]
            xs, nlfs = _sb_logits(qs, kb, bt, strict if diag else None)
            dws = [_mm_nt(dos[hh], vb) for hh in range(2)]
            tails = [_sb_tails(nlfs[hh], from_j) for hh in range(2)]
            wcat = []
            for hh in range(2):
                ws = [jnp.exp(xs[hh][tt] - tails[hh][tt] - carries[hh]) for tt in range(nt)]
                if diag:
                    ws[-1] = jnp.where(strict, ws[-1], 0.0)
                w_all = jnp.concatenate(ws, axis=1)
                sbuf[hh, :, rows] = jnp.exp(jnp.concatenate([xs[hh][tt] - nlfs[hh][tt] for tt in range(nt)], axis=1))
                ebuf[hh, :, rows] = w_all * dws[hh]
                wcat.append(w_all.astype(MXU_DTYPE))
            dv_acc[rows, :] += _mm_tn(jnp.concatenate(wcat, axis=0), do_both)
            return tuple(carries[hh] + tails[hh][0][:, 0:1] for hh in range(2))

        zero = jnp.zeros((bt, 1), F32)
        carries = pass1(i, 1, (zero, zero), True)
        carries = lax.fori_loop(0, i // 2, lambda it, cr: pass1(i - 2 - 2 * it, 2, cr, False), carries)

        @pl.when(i % 2 == 1)
        def _():
            pass1(0, 1, carries, False)

        def pass2(j, nt, pres, diag):
            rows = _kv_rows(j, bt, nt)
            kb = k_ref[rows, :]
            sums = [_sb_heads([ebuf[hh, :, _kv_rows(j + tt, bt)] for tt in range(nt)], upto_j, pres[hh]) for hh in range(2)]
            dxm = []
            for hh in range(2):
                dxs = [ebuf[hh, :, _kv_rows(j + tt, bt)] - sbuf[hh, :, _kv_rows(j + tt, bt)] * sums[hh][tt] for tt in range(nt)]
                if diag:
                    dxs[-1] = jnp.where(strict, dxs[-1], 0.0)
                dxm.append(jnp.concatenate(dxs, axis=1).astype(MXU_DTYPE))
                dq_acc[hh] += _mm(dxm[hh], kb)
            dk_acc[rows, :] += _mm_tn(jnp.concatenate(dxm, axis=0), q_both)
            return tuple(sums[hh][-1][:, bt - 1:bt] for hh in range(2))

        pres = lax.fori_loop(0, i // 2, lambda it, pr: pass2(2 * it, 2, pr, False), (zero, zero))

        @pl.when(i % 2 == 0)
        def _():
            pass2(i, 1, pres, True)

        @pl.when(i % 2 == 1)
        def _():
            pass2(i - 1, 2, pres, True)

        low = _iota((bt, lanes), 1) < SB_HEAD_DIM
        dq_ref[...] = (jnp.where(low, dq_acc[0], dq_acc[1]) * SB_SCALE).astype(dq_ref.dtype)

        @pl.when(i == nq - 1)
        def _():
            dk_ref[...] = dk_acc[...].astype(dk_ref.dtype)
            dv_ref[...] = dv_acc[...].astype(dv_ref.dtype)

        finish()

    blk = pl.BlockSpec((bt, lanes), lambda p, i: (i, p))
    whole = pl.BlockSpec((t, lanes), lambda p, i: (0, p))
    out = jax.ShapeDtypeStruct((t, D_MODEL), MXU_DTYPE)
    return pl.pallas_call(
        body, name=name, grid=(SB_PAIRS, nq),
        in_specs=[blk, pl.BlockSpec((t, lanes), lambda p, i: (0, SB_PAIRS + p)),
                  pl.BlockSpec((t, lanes), lambda p, i: (0, 2 * SB_PAIRS + p)), blk] + c_in_specs,
        out_specs=[blk, whole, whole] + c_out_specs, out_shape=[out, out, out] + c_out,
        scratch_shapes=[pltpu.VMEM((2, bt, t), F32), pltpu.VMEM((2, bt, t), F32), pltpu.VMEM((2, bt, lanes), F32),
                        pltpu.VMEM((t, lanes), F32), pltpu.VMEM((t, lanes), F32)] + (EXCHANGE_SCRATCH if carried else []),
        compiler_params=_cp("arbitrary", "arbitrary"),
    )(qkv, qkv, qkv, do, *c_in)


def _add_pair(a, b, name):
    s, r, c = a.shape
    tm = _row_tile(r)

    def body(a_ref, b_ref, o_ref):
        o_ref[...] = (a_ref[...].astype(F32) + b_ref[...].astype(F32)).astype(o_ref.dtype)

    blk = pl.BlockSpec((1, tm, c), lambda q, i: (q, i, 0))
    return pl.pallas_call(body, name=name, grid=(s, r // tm), in_specs=[blk, blk], out_specs=blk,
                          out_shape=jax.ShapeDtypeStruct(a.shape, a.dtype), compiler_params=_cp("parallel", "parallel"))(a, b)


def _sum_slots(gslots, name):
    s, r, c = gslots.shape

    def body(g_ref, o_ref):
        g = g_ref[0].astype(F32)
        for q in range(1, s):
            g = g + g_ref[q].astype(F32)
        o_ref[...] = g

    return pl.pallas_call(
        body, name=name, grid=(c // LANES,),
        in_specs=[pl.BlockSpec((s, r, LANES), lambda j: (0, 0, j))], out_specs=pl.BlockSpec((r, LANES), lambda j: (0, j)),
        out_shape=jax.ShapeDtypeStruct((r, c), F32), compiler_params=_cp("parallel"),
    )(gslots)


def _adamw(gslots, w, m, v, name):
    s, r, c = gslots.shape
    tm = _row_tile(r)
    assert w.shape == (r, c), (w.shape, gslots.shape)
    c1 = 1.0 - ADAM_B1 ** ADAM_STEP
    c2 = 1.0 - ADAM_B2 ** ADAM_STEP

    def body(g_ref, w_ref, m_ref, v_ref, go_ref, d_ref, mo_ref, vo_ref):
        g = g_ref[0].astype(F32)
        for q in range(1, s):
            g = g + g_ref[q].astype(F32)
        mn = ADAM_B1 * m_ref[...] + (1.0 - ADAM_B1) * g
        vn = ADAM_B2 * v_ref[...] + (1.0 - ADAM_B2) * (g * g)
        go_ref[...] = g
        mo_ref[...] = mn
        vo_ref[...] = vn
        d_ref[...] = -ADAM_LR * ((mn / c1) / (jnp.sqrt(vn / c2) + ADAM_EPS) + ADAM_WD * w_ref[...])

    row = pl.BlockSpec((tm, c), lambda i: (i, 0))
    out = jax.ShapeDtypeStruct((r, c), F32)
    return pl.pallas_call(
        body, name=name, grid=(r // tm,),
        in_specs=[pl.BlockSpec((s, tm, c), lambda i: (0, i, 0)), row, row, row],
        out_specs=[row, row, row, row], out_shape=[out, out, out, out], compiler_params=_cp("parallel"),
    )(gslots, w, m, v)


def _rows(a):
    flat = a.reshape(-1)
    pad = (-flat.shape[0]) % PACK_W
    if pad:
        flat = jnp.concatenate([flat, jnp.zeros((pad,), flat.dtype)])
    return flat.reshape(-1, PACK_W)


def _pack(arrays, row_multiple):
    parts, layout, off = [], [], 0
    for a in arrays:
        rw = _rows(a)
        parts.append(rw)
        layout.append((off, rw.shape[0], a.shape))
        off += rw.shape[0]
    pad = (-off) % row_multiple
    if pad:
        parts.append(jnp.zeros((pad, PACK_W), parts[0].dtype))
    return jnp.concatenate(parts, axis=0), layout


def _unpack(packed, layout):
    out = []
    for off, nrows, shape in layout:
        n = int(np.prod(shape))
        out.append(packed[off:off + nrows].reshape(-1)[:n].reshape(shape))
    return out


def _shard_as_rows(name, shard):
    if name in COL_SHARDED:
        shard = shard.transpose(0, 2, 1)
    return shard.reshape(-1, PACK_W)


def _rows_as_shard(name, rows, shape):
    if name in COL_SHARDED:
        lead, k, ns = shape
        return rows.reshape(lead, ns, k).transpose(0, 2, 1)
    return rows.reshape(shape)


def _row_tile(r):
    return next(tm for tm in (256, 128, 64, 32, 16, 8) if r % tm == 0)


def _unshard_rows(g, lead):
    ks = g.shape[1] // lead
    return g.reshape(N_DEV, lead, ks, PACK_W).transpose(1, 0, 2, 3).reshape(lead, N_DEV * ks, PACK_W)


def _shard_rows(full):
    lead, k, n = full.shape
    return full.reshape(lead, N_DEV, k // N_DEV, n).transpose(1, 0, 2, 3)


def _ssd_consts(dt_bias, a_log, d_skip):
    pad = LANES - SSD_HEADS
    bias = jnp.pad(dt_bias, (0, pad)).reshape(1, LANES)
    a_neg = jnp.pad(-jnp.exp(a_log), (0, pad)).reshape(1, LANES)
    d_exp = jnp.repeat(d_skip, SSD_HEAD_DIM).reshape(1, SSD_D_INNER)
    return bias, a_neg, d_exp


def _group_layouts(acum):
    t = acum.shape[0]
    a = acum[:, :SSD_HEADS].reshape(t, SSD_GROUPS, HEADS_PER_GROUP)
    return a.transpose(1, 0, 2), a.transpose(1, 2, 0)


def _ssd_fwd(x, p):
    hn = _rmsnorm(x, p["mix_norm"], "rmsnorm_fwd")
    proj = _matmul(hn, p["w_in"], "nt", F32, "ssd_in_fwd", tm=512, tn=896, tk=1024)
    act = _ssd_conv_fwd(proj, p["conv_w"], p["conv_b"], "ssd_conv_fwd")
    bias, a_neg, d_exp = _ssd_consts(p["dt_bias"], p["a_log"], p["d"])
    expand = _head_expand()
    xdt, dt, acum = _ssd_dt_fwd(proj, act, bias, a_neg, expand, "ssd_dt_fwd")
    acum_g, acum_gt = _group_layouts(acum)
    y, states = _ssd_scan_fwd(xdt, act, acum_g, acum_gt, "ssd_scan_fwd")
    yn = _gated_norm_fwd(y, act, proj, d_exp, p["norm"], "ssd_gnorm_fwd")
    x_new = _matmul(yn, p["w_out"], "nn", F32, "ssd_out_fwd", add=x, tm=512, tn=1024, tk=2048)
    saved = dict(x=x, hn=hn, proj=proj, act=act, xdt=xdt, dt=dt, acum_g=acum_g, acum_gt=acum_gt, y=y, states=states, yn=yn)
    return x_new, saved


def _ssd_bwd(dx, p, s):
    bias, a_neg, d_exp = _ssd_consts(p["dt_bias"], p["a_log"], p["d"])
    expand = _head_expand()
    dyn = _matmul(dx, p["w_out"], "nt", F32, "ssd_out_dgrad", tm=512, tn=1024, tk=1024)
    g_w_out = _matmul(s["yn"], dx, "tn", MXU_DTYPE, "ssd_out_wgrad", tm=1024, tn=1024, tk=512)
    dyy, dz, g_norm = _gated_norm_bwd(s["y"], s["act"], s["proj"], d_exp, p["norm"], dyn, "ssd_gnorm_bwd")
    dxdt, dbm, dcm, dacol, darow = _ssd_scan_bwd(s["xdt"], s["act"], s["acum_g"], s["acum_gt"], s["states"], dyy, "ssd_scan_bwd")
    t = dx.shape[0]
    dacum = dacol.transpose(1, 0, 2).reshape(t, SSD_HEADS) + darow.transpose(2, 0, 1).reshape(t, SSD_HEADS)
    dacum = jnp.pad(dacum, ((0, 0), (0, LANES - SSD_HEADS)))
    dxs, draw, g_a, g_bias, g_dexp = _ssd_dt_bwd(s["proj"], s["act"], s["dt"], dxdt, dyy, dacum, bias, a_neg, d_exp,
                                                  expand, expand.T, "ssd_dt_bwd")
    dact = jnp.concatenate([dxs, dbm, dcm], axis=1)
    dxbc, g_conv_w, g_conv_b = _ssd_conv_bwd(s["proj"], p["conv_w"], p["conv_b"], dact, "ssd_conv_bwd")
    dproj = jnp.concatenate([dz, dxbc, draw], axis=1)
    dhn = _matmul(dproj, p["w_in"], "nn", F32, "ssd_in_dgrad", tm=512, tn=1024, tk=896)
    g_w_in = _matmul(dproj, s["hn"], "tn", MXU_DTYPE, "ssd_in_wgrad", tm=896, tn=1024, tk=512)
    dx_new, g_mix = _rmsnorm_bwd(s["x"], p["mix_norm"], dhn, dx, "rmsnorm_bwd")
    grads = dict(w_in=g_w_in[:SSD_IN_DIM], w_out=g_w_out, conv_w=g_conv_w, conv_b=g_conv_b.reshape(-1),
                 dt_bias=g_bias[0, :SSD_HEADS], a_log=(g_a * a_neg)[0, :SSD_HEADS],
                 d=g_dexp.reshape(SSD_HEADS, SSD_HEAD_DIM).sum(axis=1), norm=g_norm.reshape(-1), mix_norm=g_mix.reshape(-1))
    return dx_new, grads


def _sb_fwd(x, p, carried=None):
    hn = _rmsnorm(x, p["mix_norm"], "rmsnorm_fwd")
    qkv = _matmul(hn, p["w_qkv"], "nt", MXU_DTYPE, "sb_qkv_fwd", tm=512, tn=1024, tk=1024)
    if carried is None:
        o, landed = _sb_attention_fwd(qkv, "sb_attn_fwd"), None
    else:
        o, landed = _sb_attention_fwd(qkv, "sb_attn_fwd_carrying_gather", carried)
    x_new = _matmul(o, p["w_out"], "nn", F32, "sb_out_fwd", add=x, tm=512, tn=1024, tk=1024)
    return x_new, dict(x=x, hn=hn, qkv=qkv, o=o), landed


def _sb_bwd(dx, p, s, carried_of=None):
    do = _matmul(dx, p["w_out"], "nt", MXU_DTYPE, "sb_out_dgrad", tm=512, tn=1024, tk=1024)
    g_w_out = _matmul(s["o"], dx, "tn", MXU_DTYPE, "sb_out_wgrad", tm=1024, tn=1024, tk=512)
    if carried_of is None:
        (dq, dk, dv), landed = _sb_attention_bwd(s["qkv"], do, "sb_attn_bwd"), None
    else:
        dq, dk, dv, landed = _sb_attention_bwd(s["qkv"], do, "sb_attn_bwd_carrying_grads", carried_of(g_w_out))
    dqkv = jnp.concatenate([dq, dk, dv], axis=1)
    dhn = _matmul(dqkv, p["w_qkv"], "nn", F32, "sb_qkv_dgrad", tm=512, tn=1024, tk=1024)
    g_w_qkv = _matmul(dqkv, s["hn"], "tn", MXU_DTYPE, "sb_qkv_wgrad", tm=1024, tn=1024, tk=512)
    dx_new, g_mix = _rmsnorm_bwd(s["x"], p["mix_norm"], dhn, dx, "rmsnorm_bwd")
    return dx_new, dict(w_qkv=g_w_qkv, w_out=g_w_out, mix_norm=g_mix.reshape(-1)), landed


def _ffn_fwd(x, p):
    hn = _rmsnorm(x, p["ffn_norm"], "rmsnorm_fwd")
    proj = _matmul(hn, p["w_in"], "nt", F32, "ffn_in_fwd", tm=512, tn=1408, tk=1024)
    act = _ffn_conv_fwd(proj, p["conv_w"], p["conv_b"], "ffn_conv_fwd")
    x_new = _matmul(act, p["w_out"], "nn", F32, "ffn_out_fwd", add=x, tm=512, tn=1024, tk=1408)
    return x_new, dict(x=x, hn=hn, proj=proj, act=act)


def _ffn_bwd(dx, p, s):
    dact = _matmul(dx, p["w_out"], "nt", F32, "ffn_out_dgrad", tm=512, tn=1408, tk=1024)
    g_w_out = _matmul(s["act"], dx, "tn", MXU_DTYPE, "ffn_out_wgrad", tm=1408, tn=1024, tk=512)
    dpg, dpu, dwg, dwu, dbg, dbu = _ffn_conv_bwd(s["proj"], p["conv_w"], p["conv_b"], dact, "ffn_conv_bwd")
    dproj = jnp.concatenate([dpg, dpu], axis=1)
    dhn = _matmul(dproj, p["w_in"], "nn", F32, "ffn_in_dgrad", tm=512, tn=1024, tk=1408)
    g_w_in = _matmul(dproj, s["hn"], "tn", MXU_DTYPE, "ffn_in_wgrad", tm=1408, tn=1024, tk=512)
    dx_new, g_norm = _rmsnorm_bwd(s["x"], p["ffn_norm"], dhn, dx, "rmsnorm_bwd")
    grads = dict(w_in=g_w_in, w_out=g_w_out, conv_w=jnp.concatenate([dwg, dwu], axis=1),
                 conv_b=jnp.concatenate([dbg, dbu], axis=1).reshape(-1), ffn_norm=g_norm.reshape(-1))
    return dx_new, grads


ADD_ROWS = 256
BIG = ["ssd_w_in", "sb_w_qkv", "ffn_w_in", "ssd_w_out", "sb_w_out", "ffn_w_out"]
LAYER_PIECES = [[("ssd_w_in", 0), ("ssd_w_out", 0), ("ffn_w_in", 0), ("ffn_w_out", 0)],
                [("sb_w_qkv", 0), ("sb_w_out", 0), ("ffn_w_in", 1), ("ffn_w_out", 1)],
                [("ssd_w_in", 1), ("ssd_w_out", 1), ("ffn_w_in", 2), ("ffn_w_out", 2)],
                [("sb_w_qkv", 1), ("sb_w_out", 1), ("ffn_w_in", 3), ("ffn_w_out", 3)]]
GRAD_SETS = {3: [("ffn_w_in", 3), ("ffn_w_out", 3), ("sb_w_out", 1)],
             1: [("sb_w_qkv", 1)] + LAYER_PIECES[2] + [("ffn_w_in", 1), ("ffn_w_out", 1), ("sb_w_out", 0)],
             "end": [("sb_w_qkv", 0)] + LAYER_PIECES[0]}
COL_SHARDED = {"ssd_w_in": 2, "sb_w_qkv": 2, "ffn_w_in": 4}
ROW_SHARDED = {"ssd_w_out": 2, "sb_w_out": 2, "ffn_w_out": 4}
CONV = ["ssd_conv_w", "ffn_conv_w"]
SMALL = ["mix_norm", "ffn_norm", "final_norm", "ssd_conv_b", "ssd_dt_bias", "ssd_a_log", "ssd_d", "ssd_norm", "ffn_conv_b"]
WEIGHTS = ["mix_norm", "ffn_norm", "final_norm", "ssd_w_in", "ssd_conv_w", "ssd_conv_b", "ssd_dt_bias", "ssd_a_log", "ssd_d",
           "ssd_norm", "ssd_w_out", "sb_w_qkv", "sb_w_out", "ffn_w_in", "ffn_conv_w", "ffn_conv_b", "ffn_w_out"]


def _step(x, loss_target, w, m, v):
    x = x.reshape(x.shape[-2], x.shape[-1])
    target = loss_target.reshape(x.shape)
    dev = 4 * lax.axis_index("x") + 2 * lax.axis_index("y") + lax.axis_index("c")
    core = lax.axis_index("c")

    shard_rows = {n: _shard_as_rows(n, w[n].astype(MXU_DTYPE)) for n in BIG}
    per_shard = {n: shard_rows[n].shape[0] // w[n].shape[0] for n in BIG}

    def layout(pieces):
        where, off = {}, 0
        for n, l in pieces:
            where[(n, l)] = (off, per_shard[n])
            off += per_shard[n]
        return where

    def pack_pieces(pieces, rows_of):
        return jnp.concatenate([rows_of(piece) for piece in pieces], axis=-2)

    def shard_piece(piece):
        n, l = piece
        return shard_rows[n][l * per_shard[n]:(l + 1) * per_shard[n]]

    full = {}

    def unpack_weights(gathered, where):
        for (n, l), (off, rows) in where.items():
            mat = gathered[:, off:off + rows].reshape(N_DEV * rows, PACK_W)
            if n == "ssd_w_in":
                mat = jnp.pad(mat, ((0, SSD_IN_PAD - SSD_IN_DIM), (0, 0)))
            full[(n, l)] = mat

    early = [p for i in (0, 1) for p in LAYER_PIECES[i]]
    late = [p for i in (2, 3) for p in LAYER_PIECES[i]]
    late_pack, late_where = pack_pieces(late, shard_piece), layout(late)
    unpack_weights(_all_gather(pack_pieces(early, shard_piece), "gather_weights_early"), layout(early))
    conv_pack, conv_layout = _pack([w[n] for n in CONV], 8)
    conv_all = _all_gather(conv_pack, "gather_conv_taps")
    for n, (off, nrows, shape) in zip(CONV, conv_layout):
        parts = [_unpack(conv_all[j], conv_layout)[CONV.index(n)] for j in range(N_DEV)]
        full[n] = jnp.concatenate(parts, axis=-1)

    def ssd_params(j):
        return dict(mix_norm=w["mix_norm"][2 * j], w_in=full[("ssd_w_in", j)], conv_w=full["ssd_conv_w"][j],
                    conv_b=w["ssd_conv_b"][j], dt_bias=w["ssd_dt_bias"][j], a_log=w["ssd_a_log"][j], d=w["ssd_d"][j],
                    norm=w["ssd_norm"][j], w_out=full[("ssd_w_out", j)])

    def sb_params(j):
        return dict(mix_norm=w["mix_norm"][2 * j + 1], w_qkv=full[("sb_w_qkv", j)], w_out=full[("sb_w_out", j)])

    def ffn_params(i):
        return dict(ffn_norm=w["ffn_norm"][i], w_in=full[("ffn_w_in", i)], conv_w=full["ffn_conv_w"][i],
                    conv_b=w["ffn_conv_b"][i], w_out=full[("ffn_w_out", i)])

    saved = []
    for i in range(DEPTH):
        if i % 2 == 0:
            x, s_mix = _ssd_fwd(x, ssd_params(i // 2))
        elif i == 1:
            x, s_mix, late_all = _sb_fwd(x, sb_params(0), carried=(late_pack, False))
            unpack_weights(late_all, late_where)
        else:
            x, s_mix, _ = _sb_fwd(x, sb_params(i // 2))
        x, s_ffn = _ffn_fwd(x, ffn_params(i))
        saved.append((s_mix, s_ffn))
    dx, g_final, loss_part = _final_norm_loss(x, w["final_norm"], target, "final_norm_loss")

    piece_grad = {}

    def grad_piece(piece):
        g = piece_grad[piece]
        return g.reshape(N_DEV, g.shape[0] // N_DEV, PACK_W)

    def carried_set(pieces, own_piece):
        def make(g_w_out):
            piece_grad[own_piece] = g_w_out
            return pack_pieces(pieces, grad_piece), True
        return make

    g_mix, g_ffn, g_ssd, g_sb = [None] * DEPTH, [None] * DEPTH, [None] * 2, [None] * 2
    landed = {}
    for i in reversed(range(DEPTH)):
        s_mix, s_ffn = saved[i]
        dx, g_ffn[i] = _ffn_bwd(dx, ffn_params(i), s_ffn)
        piece_grad[("ffn_w_in", i)], piece_grad[("ffn_w_out", i)] = g_ffn[i]["w_in"], g_ffn[i]["w_out"]
        j = i // 2
        if i % 2 == 0:
            dx, g_ssd[j] = _ssd_bwd(dx, ssd_params(j), s_mix)
            g_mix[i] = g_ssd[j]["mix_norm"]
            piece_grad[("ssd_w_in", j)], piece_grad[("ssd_w_out", j)] = g_ssd[j]["w_in"], g_ssd[j]["w_out"]
        else:
            dx, g_sb[j], landed[i] = _sb_bwd(dx, sb_params(j), s_mix, carried_set(GRAD_SETS[i], ("sb_w_out", j)))
            g_mix[i] = g_sb[j]["mix_norm"]
            piece_grad[("sb_w_qkv", j)] = g_sb[j]["w_qkv"]
    grad_x = dx.reshape(1, *dx.shape)

    g8 = pack_pieces(GRAD_SETS["end"], grad_piece)
    g8 = jnp.pad(g8, ((0, 0), (0, (-g8.shape[1]) % ADD_ROWS), (0, 0)))
    g8 = g8.reshape(4, 2, *g8.shape[1:])
    keep = lax.dynamic_index_in_dim(g8, core, axis=1, keepdims=False)
    give = lax.dynamic_index_in_dim(g8, 1 - core, axis=1, keepdims=False)
    got = _swap_with_sibling(give, "grads_to_sibling")
    chip_part = _add_pair(keep, got, "grads_add_sibling")
    landed["end"] = _exchange_chips(chip_part, "grads_across_chips")

    summed = {}
    for key, pieces in GRAD_SETS.items():
        total = _sum_slots(landed[key], "grads_sum_landed")
        for piece, (off, rows) in layout(pieces).items():
            summed[piece] = total[off:off + rows]
    big_res = [dict() for _ in range(4)]
    for n in BIG:
        lead, rows, cols = w[n].shape
        g_rows = jnp.concatenate([summed[(n, l)] for l in range(lead)], axis=0)
        g_nat = _rows_as_shard(n, g_rows, w[n].shape).reshape(1, lead * rows, cols)
        two_d = (lead * rows, cols)
        outs = _adamw(g_nat, w[n].reshape(two_d), m[n].reshape(two_d), v[n].reshape(two_d), "adamw_" + n)
        for kind in range(4):
            big_res[kind][n] = outs[kind].reshape(w[n].shape)

    small_g = {
        "mix_norm": jnp.stack(g_mix), "ffn_norm": jnp.stack([g["ffn_norm"] for g in g_ffn]), "final_norm": g_final.reshape(-1),
        "ssd_conv_b": jnp.stack([g["conv_b"] for g in g_ssd]), "ssd_dt_bias": jnp.stack([g["dt_bias"] for g in g_ssd]),
        "ssd_a_log": jnp.stack([g["a_log"] for g in g_ssd]), "ssd_d": jnp.stack([g["d"] for g in g_ssd]),
        "ssd_norm": jnp.stack([g["norm"] for g in g_ssd]), "ffn_conv_b": jnp.stack([g["conv_b"] for g in g_ffn]),
    }
    conv_g = {"ssd_conv_w": jnp.stack([g["conv_w"] for g in g_ssd]), "ffn_conv_w": jnp.stack([g["conv_w"] for g in g_ffn])}
    extra = [conv_g[n] for n in CONV] + [loss_part]
    small_pack, small_layout = _pack([small_g[n] for n in SMALL] + extra, 8)
    small_all = _all_gather(small_pack, "gather_small_grads")
    zeros_like = [jnp.zeros(a.shape, F32) for a in extra]
    sw, _ = _pack([w[n] for n in SMALL] + zeros_like, 8)
    sm, _ = _pack([m[n] for n in SMALL] + zeros_like, 8)
    sv, _ = _pack([v[n] for n in SMALL] + [jnp.ones(a.shape, F32) for a in extra], 8)
    small_out = _adamw(small_all, sw, sm, sv, "adamw_replicated")
    small_res = [_unpack(o, small_layout) for o in small_out]
    summed = small_res[0]
    loss = summed[-1][0, 0]
    conv_shard_g = []
    for n, gsum in zip(CONV, summed[len(SMALL):len(SMALL) + len(CONV)]):
        ns = w[n].shape[-1]
        conv_shard_g.append(lax.dynamic_slice_in_dim(gsum, dev * ns, ns, axis=2))
    cg, conv_sh_layout = _pack(conv_shard_g, 8)
    cw, _ = _pack([w[n] for n in CONV], 8)
    cm_, _ = _pack([m[n] for n in CONV], 8)
    cv, _ = _pack([v[n] for n in CONV], 8)
    conv_out = _adamw(cg.reshape(1, *cg.shape), cw, cm_, cv, "adamw_conv_taps")
    conv_res = [dict(zip(CONV, _unpack(o, conv_sh_layout))) for o in conv_out]

    def pick(kind, n):
        if n in BIG:
            return big_res[kind][n]
        if n in CONV:
            return conv_res[kind][n]
        return small_res[kind][SMALL.index(n)]

    outs = [loss, grad_x]
    for kind in range(4):
        outs += [pick(kind, n) for n in WEIGHTS]
    return tuple(outs)


def kernel(x, mix_norm, ffn_norm, final_norm, ssd_w_in, ssd_conv_w, ssd_conv_b, ssd_dt_bias, ssd_a_log, ssd_d, ssd_norm, ssd_w_out, sb_w_qkv, sb_w_out, ffn_w_in, ffn_conv_w, ffn_conv_b, ffn_w_out, loss_target, m_mix_norm, m_ffn_norm, m_final_norm, m_ssd_w_in, m_ssd_conv_w, m_ssd_conv_b, m_ssd_dt_bias, m_ssd_a_log, m_ssd_d, m_ssd_norm, m_ssd_w_out, m_sb_w_qkv, m_sb_w_out, m_ffn_w_in, m_ffn_conv_w, m_ffn_conv_b, m_ffn_w_out, v_mix_norm, v_ffn_norm, v_final_norm, v_ssd_w_in, v_ssd_conv_w, v_ssd_conv_b, v_ssd_dt_bias, v_ssd_a_log, v_ssd_d, v_ssd_norm, v_ssd_w_out, v_sb_w_qkv, v_sb_w_out, v_ffn_w_in, v_ffn_conv_w, v_ffn_conv_b, v_ffn_w_out):
    w = dict(mix_norm=mix_norm, ffn_norm=ffn_norm, final_norm=final_norm, ssd_w_in=ssd_w_in, ssd_conv_w=ssd_conv_w,
             ssd_conv_b=ssd_conv_b, ssd_dt_bias=ssd_dt_bias, ssd_a_log=ssd_a_log, ssd_d=ssd_d, ssd_norm=ssd_norm,
             ssd_w_out=ssd_w_out, sb_w_qkv=sb_w_qkv, sb_w_out=sb_w_out, ffn_w_in=ffn_w_in, ffn_conv_w=ffn_conv_w,
             ffn_conv_b=ffn_conv_b, ffn_w_out=ffn_w_out)
    m = dict(mix_norm=m_mix_norm, ffn_norm=m_ffn_norm, final_norm=m_final_norm, ssd_w_in=m_ssd_w_in, ssd_conv_w=m_ssd_conv_w,
             ssd_conv_b=m_ssd_conv_b, ssd_dt_bias=m_ssd_dt_bias, ssd_a_log=m_ssd_a_log, ssd_d=m_ssd_d, ssd_norm=m_ssd_norm,
             ssd_w_out=m_ssd_w_out, sb_w_qkv=m_sb_w_qkv, sb_w_out=m_sb_w_out, ffn_w_in=m_ffn_w_in, ffn_conv_w=m_ffn_conv_w,
             ffn_conv_b=m_ffn_conv_b, ffn_w_out=m_ffn_w_out)
    v = dict(mix_norm=v_mix_norm, ffn_norm=v_ffn_norm, final_norm=v_final_norm, ssd_w_in=v_ssd_w_in, ssd_conv_w=v_ssd_conv_w,
             ssd_conv_b=v_ssd_conv_b, ssd_dt_bias=v_ssd_dt_bias, ssd_a_log=v_ssd_a_log, ssd_d=v_ssd_d, ssd_norm=v_ssd_norm,
             ssd_w_out=v_ssd_w_out, sb_w_qkv=v_sb_w_qkv, sb_w_out=v_sb_w_out, ffn_w_in=v_ffn_w_in, ffn_conv_w=v_ffn_conv_w,
             ffn_conv_b=v_ffn_conv_b, ffn_w_out=v_ffn_w_out)
    return _step(x, loss_target, w, m, v)
```

```python
import jax
import jax.numpy as jnp
import numpy as np
from jax import lax
from jax.experimental import pallas as pl
from jax.experimental.pallas import tpu as pltpu

F32 = jnp.float32
MXU_DTYPE = jnp.bfloat16
MESH_ID = pl.DeviceIdType.MESH
N_DEV = 8

NORM_EPS = 1e-6
D_MODEL = 1024
DEPTH = 4
SSD_D_INNER = 2048
SSD_HEADS = 32
SSD_HEAD_DIM = 64
SSD_GROUPS = 8
SSD_STATE = 128
SSD_CONV = 4
SSD_CHUNK = 128
SSD_CONV_DIM = SSD_D_INNER + 2 * SSD_GROUPS * SSD_STATE
SSD_IN_DIM = SSD_D_INNER + SSD_CONV_DIM + SSD_HEADS
LANES = 128
SSD_IN_PAD = SSD_D_INNER + SSD_CONV_DIM + LANES
SB_HEADS = 16
SB_HEAD_DIM = 64
SB_TILE = 256
SB_SCALE = SB_HEAD_DIM ** -0.5
FFN_D_FF = 2816
FFN_CONV = 3
PACK_W = 1024

ADAM_LR = 0.001
ADAM_B1 = 0.9
ADAM_B2 = 0.999
ADAM_EPS = 1e-08
ADAM_WD = 0.01
ADAM_STEP = 10

VMEM_LIMIT_BYTES = 56 * 1024 * 1024


def _cp(*sem):
    return pltpu.CompilerParams(dimension_semantics=sem, vmem_limit_bytes=VMEM_LIMIT_BYTES)


def _iota(shape, dim):
    return lax.broadcasted_iota(jnp.int32, shape, dim)


def _sigmoid(x):
    return 1.0 / (1.0 + jnp.exp(-x))


def _mm(a, b):
    return lax.dot_general(a, b, (((1,), (0,)), ((), ())), preferred_element_type=F32)


def _mm_nt(a, b):
    return lax.dot_general(a, b, (((1,), (1,)), ((), ())), preferred_element_type=F32)


def _mm_tn(a, b):
    return lax.dot_general(a, b, (((0,), (0,)), ((), ())), preferred_element_type=F32)


def _split(x):
    hi = x.astype(MXU_DTYPE)
    lo = (x - hi.astype(F32)).astype(MXU_DTYPE)
    return hi, lo


def _mm_exact_rhs(x, m):
    hi, lo = _split(x)
    return _mm(jnp.concatenate([hi, lo], axis=1), jnp.concatenate([m, m], axis=0))


def _mm_exact_lhs(m, x):
    hi, lo = _split(x)
    return _mm(jnp.concatenate([m, m], axis=1), jnp.concatenate([hi, lo], axis=0))


def _my_place():
    return lax.axis_index("x"), lax.axis_index("y"), lax.axis_index("c")


def _all_gather(shard, name):
    r, c_ = shard.shape

    def body(x_ref, out_ref, send_sems, recv_sems, local_sem):
        x, y, c = _my_place()
        me, sibling = (x, y, c), (x, y, 1 - c)
        chips = [(1 - x, y), (x, 1 - y), (1 - x, 1 - y)]

        def slot(px, py, pc):
            return out_ref.at[4 * px + 2 * py + pc]

        def copy(k, block, to, src=None):
            return pltpu.make_async_remote_copy(
                src_ref=slot(*block) if src is None else src, dst_ref=slot(*block),
                send_sem=send_sems.at[k], recv_sem=recv_sems.at[k], device_id=to, device_id_type=MESH_ID)

        mine = pltpu.make_async_copy(x_ref, slot(*me), local_sem)
        mine.start()
        first = [copy(0, me, sibling, src=x_ref)]
        first += [copy(1 + j, me, (*chip, c), src=x_ref) for j, chip in enumerate(chips)]
        for cp in first:
            cp.start()
        passed = [copy(4 + j, (*chip, c), sibling) for j, chip in enumerate(chips)]
        for j, chip in enumerate(chips):
            copy(1 + j, (*chip, c), me).wait_recv()
            passed[j].start()
        copy(0, sibling, me).wait_recv()
        for j, chip in enumerate(chips):
            copy(4 + j, (*chip, 1 - c), me).wait_recv()
        for cp in first + passed:
            cp.wait_send()
        mine.wait()

    return pl.pallas_call(
        body, name=name,
        out_shape=jax.ShapeDtypeStruct((N_DEV, r, c_), shard.dtype),
        in_specs=[pl.BlockSpec(memory_space=pl.ANY)],
        out_specs=pl.BlockSpec(memory_space=pl.ANY),
        scratch_shapes=[pltpu.SemaphoreType.DMA((7,)), pltpu.SemaphoreType.DMA((7,)), pltpu.SemaphoreType.DMA(())],
    )(shard)


def _swap_with_sibling(buf, name):
    def body(x_ref, out_ref, send_sem, recv_sem):
        x, y, c = _my_place()
        cp = pltpu.make_async_remote_copy(src_ref=x_ref, dst_ref=out_ref, send_sem=send_sem, recv_sem=recv_sem,
                                          device_id=(x, y, 1 - c), device_id_type=MESH_ID)
        cp.start()
        cp.wait()

    return pl.pallas_call(
        body, name=name, out_shape=jax.ShapeDtypeStruct(buf.shape, buf.dtype),
        in_specs=[pl.BlockSpec(memory_space=pl.ANY)], out_specs=pl.BlockSpec(memory_space=pl.ANY),
        scratch_shapes=[pltpu.SemaphoreType.DMA(()), pltpu.SemaphoreType.DMA(())],
    )(buf)


def _exchange_chips(parts, name):
    def body(p_ref, out_ref, send_sems, recv_sems, local_sem):
        x, y, c = _my_place()
        my_q = 2 * x + y
        chips = [(1 - x, y), (x, 1 - y), (1 - x, 1 - y)]
        local = pltpu.make_async_copy(p_ref.at[my_q], out_ref.at[my_q], local_sem)
        local.start()

        def copy(k, px, py):
            return pltpu.make_async_remote_copy(
                src_ref=p_ref.at[2 * px + py], dst_ref=out_ref.at[my_q],
                send_sem=send_sems.at[k], recv_sem=recv_sems.at[k], device_id=(px, py, c), device_id_type=MESH_ID)

        def landing(k, px, py):
            return pltpu.make_async_remote_copy(
                src_ref=p_ref.at[my_q], dst_ref=out_ref.at[2 * px + py],
                send_sem=send_sems.at[k], recv_sem=recv_sems.at[k], device_id=(px, py, c), device_id_type=MESH_ID)

        sends = [copy(k, px, py) for k, (px, py) in enumerate(chips)]
        for cp in sends:
            cp.start()
        for k, (px, py) in enumerate(chips):
            landing(k, px, py).wait_recv()
        for cp in sends:
            cp.wait_send()
        local.wait()

    return pl.pallas_call(
        body, name=name, out_shape=jax.ShapeDtypeStruct(parts.shape, parts.dtype),
        in_specs=[pl.BlockSpec(memory_space=pl.ANY)], out_specs=pl.BlockSpec(memory_space=pl.ANY),
        scratch_shapes=[pltpu.SemaphoreType.DMA((3,)), pltpu.SemaphoreType.DMA((3,)), pltpu.SemaphoreType.DMA(())],
    )(parts)


RELATIONS = [(0, 0, 1), (1, 0, 0), (0, 1, 0), (1, 1, 0), (1, 0, 1), (0, 1, 1), (1, 1, 1)]
EXCHANGE_SCRATCH = [pltpu.SemaphoreType.DMA((len(RELATIONS),)), pltpu.SemaphoreType.DMA((len(RELATIONS),)),
                    pltpu.SemaphoreType.DMA(())]


def _exchange_copies(src_ref, land_ref, send_sems, recv_sems, local_sem, per_peer, incoming=True):
    x, y, c = _my_place()
    me = 4 * x + 2 * y + c

    def src(j):
        return src_ref.at[j] if per_peer else src_ref

    local = pltpu.make_async_copy(src(me), land_ref.at[me], local_sem)
    pairs = []
    for k, (dx, dy, dc) in enumerate(RELATIONS):
        peer = (1 - x if dx else x, 1 - y if dy else y, 1 - c if dc else c)
        j = 4 * peer[0] + 2 * peer[1] + peer[2]
        sems = dict(send_sem=send_sems.at[k], recv_sem=recv_sems.at[k], device_id=peer, device_id_type=MESH_ID)
        pairs.append((pltpu.make_async_remote_copy(src_ref=src(j), dst_ref=land_ref.at[me], **sems),
                      pltpu.make_async_remote_copy(src_ref=src(me), dst_ref=land_ref.at[j], **sems) if incoming else None))
    return local, pairs


def _exchange_start(*refs, per_peer):
    local, pairs = _exchange_copies(*refs, per_peer, incoming=False)
    local.start()
    for outgoing, _ in pairs:
        outgoing.start()


def _exchange_finish(*refs, per_peer):
    local, pairs = _exchange_copies(*refs, per_peer)
    for _, incoming in pairs:
        incoming.wait_recv()
    for outgoing, _ in pairs:
        outgoing.wait_send()
    local.wait()


def _matmul(a, b, mode, out_dtype, name, add=None, tm=512, tn=512, tk=512):
    if mode == "nn":
        (m, k), (k2, n) = a.shape, b.shape
    elif mode == "nt":
        (m, k), (n, k2) = a.shape, b.shape
    else:
        (k, m), (k2, n) = a.shape, b.shape
    assert k == k2, (a.shape, b.shape, mode)
    tm, tn, tk = min(tm, m), min(tn, n), min(tk, k)
    assert m % tm == 0 and n % tn == 0 and k % tk == 0, (m, n, k, tm, tn, tk)
    nk = k // tk
    mm = {"nn": _mm, "nt": _mm_nt, "tn": _mm_tn}[mode]

    def body(*refs):
        if add is None:
            a_ref, b_ref, o_ref, acc_ref = refs
        else:
            a_ref, b_ref, add_ref, o_ref, acc_ref = refs
        kk = pl.program_id(2)

        @pl.when(kk == 0)
        def _():
            acc_ref[...] = jnp.zeros_like(acc_ref)

        acc_ref[...] += mm(a_ref[...].astype(MXU_DTYPE), b_ref[...].astype(MXU_DTYPE))

        @pl.when(kk == nk - 1)
        def _():
            res = acc_ref[...]
            if add is not None:
                res = res + add_ref[...]
            o_ref[...] = res.astype(o_ref.dtype)

    a_spec = {"nn": pl.BlockSpec((tm, tk), lambda i, j, kk: (i, kk)),
              "nt": pl.BlockSpec((tm, tk), lambda i, j, kk: (i, kk)),
              "tn": pl.BlockSpec((tk, tm), lambda i, j, kk: (kk, i))}[mode]
    b_spec = {"nn": pl.BlockSpec((tk, tn), lambda i, j, kk: (kk, j)),
              "nt": pl.BlockSpec((tn, tk), lambda i, j, kk: (j, kk)),
              "tn": pl.BlockSpec((tk, tn), lambda i, j, kk: (kk, j))}[mode]
    o_spec = pl.BlockSpec((tm, tn), lambda i, j, kk: (i, j))
    in_specs, args = [a_spec, b_spec], [a, b]
    if add is not None:
        in_specs.append(o_spec)
        args.append(add)
    return pl.pallas_call(
        body, name=name, grid=(m // tm, n // tn, nk), in_specs=in_specs, out_specs=o_spec,
        out_shape=jax.ShapeDtypeStruct((m, n), out_dtype),
        scratch_shapes=[pltpu.VMEM((tm, tn), F32)],
        compiler_params=_cp("parallel", "parallel", "arbitrary"),
    )(*args)


def _rmsnorm(x, g, name):
    t, d = x.shape
    tm = min(512, t)

    def body(x_ref, g_ref, o_ref):
        xv = x_ref[...]
        r = lax.rsqrt(jnp.mean(xv * xv, axis=-1, keepdims=True) + NORM_EPS)
        o_ref[...] = (xv * r * g_ref[...]).astype(o_ref.dtype)

    return pl.pallas_call(
        body, name=name, grid=(t // tm,),
        in_specs=[pl.BlockSpec((tm, d), lambda i: (i, 0)), pl.BlockSpec((1, d), lambda i: (0, 0))],
        out_specs=pl.BlockSpec((tm, d), lambda i: (i, 0)),
        out_shape=jax.ShapeDtypeStruct((t, d), MXU_DTYPE), compiler_params=_cp("parallel"),
    )(x, g.reshape(1, d))


def _rmsnorm_bwd(x, g, dh, dres, name):
    t, d = x.shape
    tm = min(512, t)

    def body(x_ref, g_ref, dh_ref, dres_ref, dx_ref, dg_ref):
        @pl.when(pl.program_id(0) == 0)
        def _():
            dg_ref[...] = jnp.zeros_like(dg_ref)

        xv = x_ref[...]
        r = lax.rsqrt(jnp.mean(xv * xv, axis=-1, keepdims=True) + NORM_EPS)
        xn = xv * r
        dhv = dh_ref[...]
        u = dhv * g_ref[...]
        dx_ref[...] = dres_ref[...] + r * (u - xn * jnp.mean(u * xn, axis=-1, keepdims=True))
        dg_ref[...] += jnp.sum(dhv * xn, axis=0, keepdims=True)

    row = pl.BlockSpec((tm, d), lambda i: (i, 0))
    vec = pl.BlockSpec((1, d), lambda i: (0, 0))
    return pl.pallas_call(
        body, name=name, grid=(t // tm,), in_specs=[row, vec, row, row], out_specs=[row, vec],
        out_shape=[jax.ShapeDtypeStruct((t, d), F32), jax.ShapeDtypeStruct((1, d), F32)],
        compiler_params=_cp("arbitrary"),
    )(x, g.reshape(1, d), dh, dres)


def _final_norm_loss(x, g, target, name):
    t, d = x.shape
    tm = min(512, t)

    def body(x_ref, g_ref, t_ref, dx_ref, dg_ref, loss_ref):
        @pl.when(pl.program_id(0) == 0)
        def _():
            dg_ref[...] = jnp.zeros_like(dg_ref)
            loss_ref[...] = jnp.zeros_like(loss_ref)

        xv = x_ref[...]
        gv = g_ref[...]
        r = lax.rsqrt(jnp.mean(xv * xv, axis=-1, keepdims=True) + NORM_EPS)
        xn = xv * r
        err = xn * gv - t_ref[...]
        per_tok = jnp.mean(err * err, axis=-1, keepdims=True)
        loss_ref[...] += jnp.broadcast_to(0.5 * jnp.sum(per_tok, axis=0, keepdims=True), loss_ref.shape)
        dy = err * (1.0 / d)
        u = dy * gv
        dx_ref[...] = r * (u - xn * jnp.mean(u * xn, axis=-1, keepdims=True))
        dg_ref[...] += jnp.sum(dy * xn, axis=0, keepdims=True)

    row = pl.BlockSpec((tm, d), lambda i: (i, 0))
    vec = pl.BlockSpec((1, d), lambda i: (0, 0))
    return pl.pallas_call(
        body, name=name, grid=(t // tm,), in_specs=[row, vec, row],
        out_specs=[row, vec, pl.BlockSpec((1, LANES), lambda i: (0, 0))],
        out_shape=[jax.ShapeDtypeStruct((t, d), F32), jax.ShapeDtypeStruct((1, d), F32),
                   jax.ShapeDtypeStruct((1, LANES), F32)],
        compiler_params=_cp("arbitrary"),
    )(x, g.reshape(1, d), target)


CONV_COLS = 128


def _shifts_down(p, width):
    row = _iota(p.shape, 0)
    return [jnp.where(row >= s, pltpu.roll(p, s, axis=0), 0.0) for s in range(1, width)]


def _shifts_up(p, width):
    n = p.shape[0]
    row = _iota(p.shape, 0)
    return [jnp.where(row < n - s, pltpu.roll(p, n - s, axis=0), 0.0) for s in range(1, width)]


def _conv_pre(p, shifted, w_ref, b_ref):
    width = w_ref.shape[0]
    u = b_ref[...] + w_ref[width - 1:width, :] * p
    for s in range(1, width):
        u = u + w_ref[width - 1 - s:width - s, :] * shifted[s - 1]
    return u


def _conv_transpose(du, w_ref):
    width = w_ref.shape[0]
    shifted = _shifts_up(du, width)
    dp = w_ref[width - 1:width, :] * du
    for s in range(1, width):
        dp = dp + w_ref[width - 1 - s:width - s, :] * shifted[s - 1]
    return dp


def _conv_wgrad(du, p, shifted, dw_ref, db_ref):
    width = dw_ref.shape[0]
    db_ref[...] = jnp.sum(du, axis=0, keepdims=True)
    dw_ref[width - 1:width, :] = jnp.sum(du * p, axis=0, keepdims=True)
    for s in range(1, width):
        dw_ref[width - 1 - s:width - s, :] = jnp.sum(du * shifted[s - 1], axis=0, keepdims=True)


def _ssd_conv_fwd(proj, w, b, name):
    t = proj.shape[0]
    cb = CONV_COLS
    off = SSD_D_INNER // cb

    def body(p_ref, w_ref, b_ref, o_ref):
        p = p_ref[...]
        u = _conv_pre(p, _shifts_down(p, SSD_CONV), w_ref, b_ref)
        o_ref[...] = u * _sigmoid(u)

    return pl.pallas_call(
        body, name=name, grid=(SSD_CONV_DIM // cb,),
        in_specs=[pl.BlockSpec((t, cb), lambda j: (0, j + off)), pl.BlockSpec((SSD_CONV, cb), lambda j: (0, j)),
                  pl.BlockSpec((1, cb), lambda j: (0, j))],
        out_specs=pl.BlockSpec((t, cb), lambda j: (0, j)),
        out_shape=jax.ShapeDtypeStruct((t, SSD_CONV_DIM), F32), compiler_params=_cp("parallel"),
    )(proj, w, b.reshape(1, -1))


def _ssd_conv_bwd(proj, w, b, dact, name):
    t = proj.shape[0]
    cb = CONV_COLS
    off = SSD_D_INNER // cb

    def body(p_ref, w_ref, b_ref, da_ref, dp_ref, dw_ref, db_ref):
        p = p_ref[...]
        shifted = _shifts_down(p, SSD_CONV)
        u = _conv_pre(p, shifted, w_ref, b_ref)
        sg = _sigmoid(u)
        du = da_ref[...] * (sg * (1.0 + u * (1.0 - sg)))
        dp_ref[...] = _conv_transpose(du, w_ref).astype(dp_ref.dtype)
        _conv_wgrad(du, p, shifted, dw_ref, db_ref)

    col = pl.BlockSpec((t, cb), lambda j: (0, j))
    wspec = pl.BlockSpec((SSD_CONV, cb), lambda j: (0, j))
    bspec = pl.BlockSpec((1, cb), lambda j: (0, j))
    return pl.pallas_call(
        body, name=name, grid=(SSD_CONV_DIM // cb,),
        in_specs=[pl.BlockSpec((t, cb), lambda j: (0, j + off)), wspec, bspec, col],
        out_specs=[col, wspec, bspec],
        out_shape=[jax.ShapeDtypeStruct((t, SSD_CONV_DIM), MXU_DTYPE), jax.ShapeDtypeStruct((SSD_CONV, SSD_CONV_DIM), F32),
                   jax.ShapeDtypeStruct((1, SSD_CONV_DIM), F32)],
        compiler_params=_cp("parallel"),
    )(proj, w, b.reshape(1, -1), dact)


def _ffn_conv_fwd(proj, w, b, name):
    t = proj.shape[0]
    cb = CONV_COLS
    nb = FFN_D_FF // cb

    def body(pg_ref, pu_ref, wg_ref, wu_ref, bg_ref, bu_ref, o_ref):
        pg, pu = pg_ref[...], pu_ref[...]
        ug = _conv_pre(pg, _shifts_down(pg, FFN_CONV), wg_ref, bg_ref)
        uu = _conv_pre(pu, _shifts_down(pu, FFN_CONV), wu_ref, bu_ref)
        o_ref[...] = (ug * _sigmoid(ug) * uu).astype(o_ref.dtype)

    gcol = pl.BlockSpec((t, cb), lambda j: (0, j))
    ucol = pl.BlockSpec((t, cb), lambda j: (0, j + nb))
    b2 = b.reshape(1, -1)
    return pl.pallas_call(
        body, name=name, grid=(nb,),
        in_specs=[gcol, ucol, pl.BlockSpec((FFN_CONV, cb), lambda j: (0, j)), pl.BlockSpec((FFN_CONV, cb), lambda j: (0, j + nb)),
                  pl.BlockSpec((1, cb), lambda j: (0, j)), pl.BlockSpec((1, cb), lambda j: (0, j + nb))],
        out_specs=gcol, out_shape=jax.ShapeDtypeStruct((t, FFN_D_FF), MXU_DTYPE), compiler_params=_cp("parallel"),
    )(proj, proj, w, w, b2, b2)


def _ffn_conv_bwd(proj, w, b, dact, name):
    t = proj.shape[0]
    cb = CONV_COLS
    nb = FFN_D_FF // cb

    def body(pg_ref, pu_ref, wg_ref, wu_ref, bg_ref, bu_ref, da_ref,
             dpg_ref, dpu_ref, dwg_ref, dwu_ref, dbg_ref, dbu_ref):
        pg, pu = pg_ref[...], pu_ref[...]
        pg_shifted, pu_shifted = _shifts_down(pg, FFN_CONV), _shifts_down(pu, FFN_CONV)
        ug = _conv_pre(pg, pg_shifted, wg_ref, bg_ref)
        uu = _conv_pre(pu, pu_shifted, wu_ref, bu_ref)
        sg = _sigmoid(ug)
        da = da_ref[...]
        dug = da * uu * (sg * (1.0 + ug * (1.0 - sg)))
        duu = da * (ug * sg)
        dpg_ref[...] = _conv_transpose(dug, wg_ref).astype(dpg_ref.dtype)
        dpu_ref[...] = _conv_transpose(duu, wu_ref).astype(dpu_ref.dtype)
        _conv_wgrad(dug, pg, pg_shifted, dwg_ref, dbg_ref)
        _conv_wgrad(duu, pu, pu_shifted, dwu_ref, dbu_ref)

    gcol = pl.BlockSpec((t, cb), lambda j: (0, j))
    ucol = pl.BlockSpec((t, cb), lambda j: (0, j + nb))
    wg = pl.BlockSpec((FFN_CONV, cb), lambda j: (0, j))
    wu = pl.BlockSpec((FFN_CONV, cb), lambda j: (0, j + nb))
    bg = pl.BlockSpec((1, cb), lambda j: (0, j))
    bu = pl.BlockSpec((1, cb), lambda j: (0, j + nb))
    b2 = b.reshape(1, -1)
    half = jax.ShapeDtypeStruct((t, FFN_D_FF), MXU_DTYPE)
    return pl.pallas_call(
        body, name=name, grid=(nb,),
        in_specs=[gcol, ucol, wg, wu, bg, bu, gcol],
        out_specs=[gcol, gcol, wg, wg, bg, bg],
        out_shape=[half, half, jax.ShapeDtypeStruct((FFN_CONV, FFN_D_FF), F32), jax.ShapeDtypeStruct((FFN_CONV, FFN_D_FF), F32),
                   jax.ShapeDtypeStruct((1, FFN_D_FF), F32), jax.ShapeDtypeStruct((1, FFN_D_FF), F32)],
        compiler_params=_cp("parallel"),
    )(proj, proj, w, w, b2, b2, dact)


SSD_ROWS = 128
DT_COL = (SSD_D_INNER + SSD_CONV_DIM) // LANES


def _head_expand():
    h = np.arange(LANES)[:, None]
    col = np.arange(SSD_D_INNER)[None, :]
    return jnp.asarray((col // SSD_HEAD_DIM == h), MXU_DTYPE)


def _chunk_tri(n, lower):
    t = _iota((n, n), 0)
    s = _iota((n, n), 1)
    shift = SSD_CHUNK.bit_length() - 1
    same = jnp.right_shift(t, shift) == jnp.right_shift(s, shift)
    tri = (s <= t) if lower else (s >= t)
    return jnp.where(same & tri, 1.0, 0.0).astype(MXU_DTYPE)


def _softplus(x):
    return jnp.maximum(x, 0.0) + jnp.log(1.0 + jnp.exp(-jnp.abs(x)))


def _ssd_dt_fwd(proj, act, dt_bias, a_neg, expand, name):
    t = proj.shape[0]
    tm = min(SSD_ROWS, t)

    def body(raw_ref, xs_ref, bias_ref, a_ref, e_ref, xdt_ref, dt_ref, acum_ref):
        lane = _iota((tm, LANES), 1)
        dt = jnp.where(lane < SSD_HEADS, _softplus(raw_ref[...] + bias_ref[...]), 0.0)
        dt_ref[...] = dt
        xdt_ref[...] = xs_ref[...] * _mm_exact_rhs(dt, e_ref[...])
        acum_ref[...] = _mm_exact_lhs(_chunk_tri(tm, True), a_ref[...] * dt)

    vec = pl.BlockSpec((1, LANES), lambda i: (0, 0))
    return pl.pallas_call(
        body, name=name, grid=(t // tm,),
        in_specs=[pl.BlockSpec((tm, LANES), lambda i: (i, DT_COL)), pl.BlockSpec((tm, SSD_D_INNER), lambda i: (i, 0)),
                  vec, vec, pl.BlockSpec((LANES, SSD_D_INNER), lambda i: (0, 0))],
        out_specs=[pl.BlockSpec((tm, SSD_D_INNER), lambda i: (i, 0)), pl.BlockSpec((tm, LANES), lambda i: (i, 0)),
                   pl.BlockSpec((tm, LANES), lambda i: (i, 0))],
        out_shape=[jax.ShapeDtypeStruct((t, SSD_D_INNER), F32), jax.ShapeDtypeStruct((t, LANES), F32),
                   jax.ShapeDtypeStruct((t, LANES), F32)],
        compiler_params=_cp("parallel"),
    )(proj, act, dt_bias, a_neg, expand)


def _ssd_dt_bwd(proj, act, dt, dxdt, dyy, dacum, dt_bias, a_neg, d_exp, expand, expand_t, name):
    t = proj.shape[0]
    tm = min(SSD_ROWS, t)

    def body(raw_ref, xs_ref, dt_ref, dxdt_ref, dyy_ref, dac_ref, bias_ref, a_ref, dsk_ref, e_ref, et_ref,
             dxs_ref, draw_ref, da_ref, dbias_ref, dd_ref):
        @pl.when(pl.program_id(0) == 0)
        def _():
            da_ref[...] = jnp.zeros_like(da_ref)
            dbias_ref[...] = jnp.zeros_like(dbias_ref)
            dd_ref[...] = jnp.zeros_like(dd_ref)

        lane = _iota((tm, LANES), 1)
        xs, dt, dxdt, dyy = xs_ref[...], dt_ref[...], dxdt_ref[...], dyy_ref[...]
        dxs_ref[...] = dxdt * _mm_exact_rhs(dt, e_ref[...]) + dsk_ref[...] * dyy
        dd_ref[...] += jnp.sum(dyy * xs, axis=0, keepdims=True)
        ddt = _mm_exact_rhs(dxdt * xs, et_ref[...])
        da = _mm_exact_lhs(_chunk_tri(tm, False), dac_ref[...])
        ddt = ddt + da * a_ref[...]
        da_ref[...] += jnp.sum(da * dt, axis=0, keepdims=True)
        draw = jnp.where(lane < SSD_HEADS, ddt * _sigmoid(raw_ref[...] + bias_ref[...]), 0.0)
        dbias_ref[...] += jnp.sum(draw, axis=0, keepdims=True)
        draw_ref[...] = draw.astype(draw_ref.dtype)

    wide = pl.BlockSpec((tm, SSD_D_INNER), lambda i: (i, 0))
    thin = pl.BlockSpec((tm, LANES), lambda i: (i, 0))
    vec = pl.BlockSpec((1, LANES), lambda i: (0, 0))
    wvec = pl.BlockSpec((1, SSD_D_INNER), lambda i: (0, 0))
    return pl.pallas_call(
        body, name=name, grid=(t // tm,),
        in_specs=[pl.BlockSpec((tm, LANES), lambda i: (i, DT_COL)), wide, thin, wide, wide, thin, vec, vec, wvec,
                  pl.BlockSpec((LANES, SSD_D_INNER), lambda i: (0, 0)), pl.BlockSpec((SSD_D_INNER, LANES), lambda i: (0, 0))],
        out_specs=[wide, thin, vec, vec, wvec],
        out_shape=[jax.ShapeDtypeStruct((t, SSD_D_INNER), F32), jax.ShapeDtypeStruct((t, LANES), MXU_DTYPE),
                   jax.ShapeDtypeStruct((1, LANES), F32), jax.ShapeDtypeStruct((1, LANES), F32),
                   jax.ShapeDtypeStruct((1, SSD_D_INNER), F32)],
        compiler_params=_cp("arbitrary"),
    )(proj, act, dt, dxdt, dyy, dacum, dt_bias, a_neg, d_exp, expand, expand_t)


SSD_PAIR = 2 * SSD_HEAD_DIM
HEADS_PER_GROUP = SSD_HEADS // SSD_GROUPS
GROUP_COLS = HEADS_PER_GROUP * SSD_HEAD_DIM
B_COL0 = SSD_D_INNER // SSD_STATE
C_COL0 = (SSD_D_INNER + SSD_GROUPS * SSD_STATE) // SSD_STATE


def _pair_cols(vals, h0, lo_mask):
    return jnp.where(lo_mask, vals[:, h0:h0 + 1], vals[:, h0 + 1:h0 + 2])


def _scan_step_is(g, c):
    return (pl.program_id(0) == g) & (pl.program_id(1) == c)


def _ssd_scan_fwd(xdt, act, acum_g, acum_gt, name, carried=None):
    t = xdt.shape[0]
    nc = t // SSD_CHUNK
    ln = SSD_CHUNK
    c_in_specs, c_in, c_out_specs, c_out = _carried_specs(carried)

    def body(*refs):
        if carried is None:
            x_ref, b_ref, c_ref, ac_ref, act_ref, y_ref, sst_ref, state = refs
            comm_refs = None
        else:
            x_ref, b_ref, c_ref, ac_ref, act_ref, src_ref, y_ref, sst_ref, land_ref, state, send_sems, recv_sems, local_sem = refs
            comm_refs = (src_ref, land_ref, send_sems, recv_sems, local_sem)
        finish = _carried_hooks(carried, comm_refs, _scan_step_is(0, 0), _scan_step_is(SSD_GROUPS - 1, nc - 1))

        @pl.when(pl.program_id(1) == 0)
        def _():
            state[...] = jnp.zeros_like(state)

        bm = b_ref[...].astype(MXU_DTYPE)
        cm = c_ref[...].astype(MXU_DTYPE)
        cb = _mm_nt(cm, bm)
        ac, act_ = ac_ref[0], act_ref[0]
        causal = _iota((ln, ln), 1) <= _iota((ln, ln), 0)
        lo_mask = _iota((ln, SSD_PAIR), 1) < SSD_HEAD_DIM
        lo_rows = _iota((SSD_PAIR, SSD_STATE), 0) < SSD_HEAD_DIM
        sst_ref[0, 0] = state[...]
        last = ac[ln - 1:ln, :]
        e_ac = jnp.exp(ac)
        w_all = jnp.exp(last - ac)
        e_last = jnp.exp(last)
        for pr in range(2):
            cols = slice(pr * SSD_PAIR, (pr + 1) * SSD_PAIR)
            xp = x_ref[:, cols]
            sp = state[cols, :]
            ydiag = jnp.zeros((ln, SSD_PAIR), F32)
            for hh in range(2):
                h = 2 * pr + hh
                seg = ac[:, h:h + 1] - act_[h:h + 1, :]
                dec = jnp.exp(jnp.where(causal, seg, -1e30))
                mask = lo_mask if hh == 0 else jnp.logical_not(lo_mask)
                ydiag = ydiag + _mm((cb * dec).astype(MXU_DTYPE), jnp.where(mask, xp, 0.0).astype(MXU_DTYPE))
            yoff = _mm_nt(cm, sp.astype(MXU_DTYPE)) * _pair_cols(e_ac, 2 * pr, lo_mask)
            y_ref[:, cols] = ydiag + yoff
            xw = (xp * _pair_cols(w_all, 2 * pr, lo_mask)).astype(MXU_DTYPE)
            el = jnp.where(lo_rows, e_last[:, 2 * pr:2 * pr + 1], e_last[:, 2 * pr + 1:2 * pr + 2])
            state[cols, :] = sp * el + _mm_tn(xw, bm)
        finish()

    return pl.pallas_call(
        body, name=name, grid=(SSD_GROUPS, nc),
        in_specs=[pl.BlockSpec((ln, GROUP_COLS), lambda g, c: (c, g)),
                  pl.BlockSpec((ln, SSD_STATE), lambda g, c: (c, B_COL0 + g)),
                  pl.BlockSpec((ln, SSD_STATE), lambda g, c: (c, C_COL0 + g)),
                  pl.BlockSpec((1, ln, HEADS_PER_GROUP), lambda g, c: (g, c, 0)),
                  pl.BlockSpec((1, HEADS_PER_GROUP, ln), lambda g, c: (g, 0, c))] + c_in_specs,
        out_specs=[pl.BlockSpec((ln, GROUP_COLS), lambda g, c: (c, g)),
                   pl.BlockSpec((1, 1, GROUP_COLS, SSD_STATE), lambda g, c: (c, g, 0, 0))] + c_out_specs,
        out_shape=[jax.ShapeDtypeStruct((t, SSD_D_INNER), F32),
                   jax.ShapeDtypeStruct((nc, SSD_GROUPS, GROUP_COLS, SSD_STATE), F32)] + c_out,
        scratch_shapes=[pltpu.VMEM((GROUP_COLS, SSD_STATE), F32)] + (EXCHANGE_SCRATCH if carried else []),
        compiler_params=_cp("arbitrary", "arbitrary"),
    )(xdt, act, act, acum_g, acum_gt, *c_in)


def _ssd_scan_bwd(xdt, act, acum_g, acum_gt, states, dy, name, carried=None):
    t = xdt.shape[0]
    nc = t // SSD_CHUNK
    ln = SSD_CHUNK
    c_in_specs, c_in, c_out_specs, c_out = _carried_specs(carried)

    def body(*refs):
        if carried is None:
            x_ref, b_ref, c_ref, ac_ref, act_ref, sst_ref, dy_ref, dx_ref, db_ref, dc_ref, dacol_ref, darow_ref, dstate = refs
            comm_refs = None
        else:
            (x_ref, b_ref, c_ref, ac_ref, act_ref, sst_ref, dy_ref, src_ref, dx_ref, db_ref, dc_ref, dacol_ref, darow_ref,
             land_ref, dstate, send_sems, recv_sems, local_sem) = refs
            comm_refs = (src_ref, land_ref, send_sems, recv_sems, local_sem)
        finish = _carried_hooks(carried, comm_refs, _scan_step_is(0, 0), _scan_step_is(SSD_GROUPS - 1, nc - 1))

        @pl.when(pl.program_id(1) == 0)
        def _():
            dstate[...] = jnp.zeros_like(dstate)

        bm = b_ref[...].astype(MXU_DTYPE)
        cm = c_ref[...].astype(MXU_DTYPE)
        cb = _mm_nt(cm, bm)
        ac, act_ = ac_ref[0], act_ref[0]
        causal = _iota((ln, ln), 1) <= _iota((ln, ln), 0)
        lo_mask = _iota((ln, SSD_PAIR), 1) < SSD_HEAD_DIM
        lo_rows = _iota((SSD_PAIR, SSD_STATE), 0) < SSD_HEAD_DIM
        lane4 = _iota((ln, HEADS_PER_GROUP), 1)
        sub4 = _iota((HEADS_PER_GROUP, ln), 0)
        is_last = _iota((ln, 1), 0) == ln - 1
        last = ac[ln - 1:ln, :]
        e_ac = jnp.exp(ac)
        w_all = jnp.exp(last - ac)
        e_last = jnp.exp(last)
        dcb = jnp.zeros((ln, ln), F32)
        dc_acc = jnp.zeros((ln, SSD_STATE), F32)
        db_acc = jnp.zeros((ln, SSD_STATE), F32)
        dacol = jnp.zeros((ln, HEADS_PER_GROUP), F32)
        darow = jnp.zeros((HEADS_PER_GROUP, ln), F32)
        for pr in range(2):
            cols = slice(pr * SSD_PAIR, (pr + 1) * SSD_PAIR)
            xp = x_ref[:, cols]
            dyp = dy_ref[:, cols]
            sp = sst_ref[0, 0, cols, :]
            dsp = dstate[cols, :]
            ea = _pair_cols(e_ac, 2 * pr, lo_mask)
            w = _pair_cols(w_all, 2 * pr, lo_mask)
            dye = (dyp * ea).astype(MXU_DTYPE)
            dx_state = w * _mm_nt(bm, dsp.astype(MXU_DTYPE))
            yoff = _mm_nt(cm, sp.astype(MXU_DTYPE)) * ea
            dxp = dx_state
            for hh in range(2):
                h = 2 * pr + hh
                mask = lo_mask if hh == 0 else jnp.logical_not(lo_mask)
                rmask = lo_rows if hh == 0 else jnp.logical_not(lo_rows)
                seg = ac[:, h:h + 1] - act_[h:h + 1, :]
                dec = jnp.exp(jnp.where(causal, seg, -1e30))
                m = cb * dec
                dym = jnp.where(mask, dyp, 0.0).astype(MXU_DTYPE)
                xm = jnp.where(mask, xp, 0.0).astype(MXU_DTYPE)
                g = _mm_nt(dym, xm)
                dxp = dxp + _mm_tn(m.astype(MXU_DTYPE), dym)
                dcb = dcb + dec * g
                mg = m * g
                rs = jnp.sum(mg, axis=1, keepdims=True)
                cs = jnp.sum(mg, axis=0, keepdims=True)
                t_off = jnp.sum(jnp.where(mask, dyp * yoff, 0.0), axis=1, keepdims=True)
                q = jnp.sum(jnp.where(mask, xp * dx_state, 0.0), axis=1, keepdims=True)
                qsum = jnp.sum(q, axis=0, keepdims=True)
                ds_s = jnp.sum(jnp.sum(jnp.where(rmask, dsp * sp, 0.0), axis=1, keepdims=True), axis=0, keepdims=True)
                extra = qsum + e_last[:, h:h + 1] * ds_s
                col = rs + t_off - q + jnp.where(is_last, extra, 0.0)
                dacol = jnp.where(lane4 == h, col, dacol)
                darow = jnp.where(sub4 == h, -cs, darow)
            dx_ref[:, cols] = dxp
            dc_acc = dc_acc + _mm(dye, sp.astype(MXU_DTYPE))
            db_acc = db_acc + _mm((xp * w).astype(MXU_DTYPE), dsp.astype(MXU_DTYPE))
            el = jnp.where(lo_rows, e_last[:, 2 * pr:2 * pr + 1], e_last[:, 2 * pr + 1:2 * pr + 2])
            dstate[cols, :] = dsp * el + _mm_tn(dye, cm)
        dcbm = dcb.astype(MXU_DTYPE)
        dc_ref[...] = _mm(dcbm, bm) + dc_acc
        db_ref[...] = _mm_tn(dcbm, cm) + db_acc
        dacol_ref[0] = dacol
        darow_ref[0] = darow
        finish()

    def rev(c):
        return nc - 1 - c

    grp = pl.BlockSpec((ln, GROUP_COLS), lambda g, c: (rev(c), g))
    return pl.pallas_call(
        body, name=name, grid=(SSD_GROUPS, nc),
        in_specs=[grp,
                  pl.BlockSpec((ln, SSD_STATE), lambda g, c: (rev(c), B_COL0 + g)),
                  pl.BlockSpec((ln, SSD_STATE), lambda g, c: (rev(c), C_COL0 + g)),
                  pl.BlockSpec((1, ln, HEADS_PER_GROUP), lambda g, c: (g, rev(c), 0)),
                  pl.BlockSpec((1, HEADS_PER_GROUP, ln), lambda g, c: (g, 0, rev(c))),
                  pl.BlockSpec((1, 1, GROUP_COLS, SSD_STATE), lambda g, c: (rev(c), g, 0, 0)),
                  grp] + c_in_specs,
        out_specs=[grp,
                   pl.BlockSpec((ln, SSD_STATE), lambda g, c: (rev(c), g)),
                   pl.BlockSpec((ln, SSD_STATE), lambda g, c: (rev(c), g)),
                   pl.BlockSpec((1, ln, HEADS_PER_GROUP), lambda g, c: (g, rev(c), 0)),
                   pl.BlockSpec((1, HEADS_PER_GROUP, ln), lambda g, c: (g, 0, rev(c)))] + c_out_specs,
        out_shape=[jax.ShapeDtypeStruct((t, SSD_D_INNER), F32),
                   jax.ShapeDtypeStruct((t, SSD_GROUPS * SSD_STATE), F32),
                   jax.ShapeDtypeStruct((t, SSD_GROUPS * SSD_STATE), F32),
                   jax.ShapeDtypeStruct((SSD_GROUPS, t, HEADS_PER_GROUP), F32),
                   jax.ShapeDtypeStruct((SSD_GROUPS, HEADS_PER_GROUP, t), F32)] + c_out,
        scratch_shapes=[pltpu.VMEM((GROUP_COLS, SSD_STATE), F32)] + (EXCHANGE_SCRATCH if carried else []),
        compiler_params=_cp("arbitrary", "arbitrary"),
    )(xdt, act, act, acum_g, acum_gt, states, dy, *c_in)


GN_ROWS = 128


def _gated_norm_parts(y_ref, xs_ref, z_ref, dsk_ref):
    yy = y_ref[...] + dsk_ref[...] * xs_ref[...]
    z = z_ref[...]
    sz = _sigmoid(z)
    silu = z * sz
    u = yy * silu
    r = lax.rsqrt(jnp.mean(u * u, axis=-1, keepdims=True) + NORM_EPS)
    return yy, z, sz, silu, u, r


def _gated_norm_fwd(y, act, proj, d_exp, g, name):
    t = y.shape[0]
    tm = min(GN_ROWS, t)

    def body(y_ref, xs_ref, z_ref, dsk_ref, g_ref, o_ref):
        _, _, _, _, u, r = _gated_norm_parts(y_ref, xs_ref, z_ref, dsk_ref)
        o_ref[...] = (u * r * g_ref[...]).astype(o_ref.dtype)

    wide = pl.BlockSpec((tm, SSD_D_INNER), lambda i: (i, 0))
    wvec = pl.BlockSpec((1, SSD_D_INNER), lambda i: (0, 0))
    return pl.pallas_call(
        body, name=name, grid=(t // tm,), in_specs=[wide, wide, wide, wvec, wvec], out_specs=wide,
        out_shape=jax.ShapeDtypeStruct((t, SSD_D_INNER), MXU_DTYPE), compiler_params=_cp("parallel"),
    )(y, act, proj, d_exp, g.reshape(1, -1))


def _gated_norm_bwd(y, act, proj, d_exp, g, dn, name):
    t = y.shape[0]
    tm = min(GN_ROWS, t)

    def body(y_ref, xs_ref, z_ref, dsk_ref, g_ref, dn_ref, dyy_ref, dz_ref, dg_ref):
        @pl.when(pl.program_id(0) == 0)
        def _():
            dg_ref[...] = jnp.zeros_like(dg_ref)

        yy, z, sz, silu, u, r = _gated_norm_parts(y_ref, xs_ref, z_ref, dsk_ref)
        un = u * r
        dn = dn_ref[...]
        v = dn * g_ref[...]
        du = r * (v - un * jnp.mean(v * un, axis=-1, keepdims=True))
        dg_ref[...] += jnp.sum(dn * un, axis=0, keepdims=True)
        dyy_ref[...] = du * silu
        dz_ref[...] = (du * yy * (sz * (1.0 + z * (1.0 - sz)))).astype(dz_ref.dtype)

    wide = pl.BlockSpec((tm, SSD_D_INNER), lambda i: (i, 0))
    wvec = pl.BlockSpec((1, SSD_D_INNER), lambda i: (0, 0))
    return pl.pallas_call(
        body, name=name, grid=(t // tm,), in_specs=[wide, wide, wide, wvec, wvec, wide], out_specs=[wide, wide, wvec],
        out_shape=[jax.ShapeDtypeStruct((t, SSD_D_INNER), F32), jax.ShapeDtypeStruct((t, SSD_D_INNER), MXU_DTYPE),
                   jax.ShapeDtypeStruct((1, SSD_D_INNER), F32)],
        compiler_params=_cp("arbitrary"),
    )(y, act, proj, d_exp, g.reshape(1, -1), dn)


SB_PAIRS = SB_HEADS // 2


def _kv_rows(j, bt, nt=1):
    return pl.ds(pl.multiple_of(j * bt, bt), nt * bt)


def _sb_tile_masks(bt):
    lane = _iota((bt, bt), 1)
    rowi = _iota((bt, bt), 0)
    return lane < rowi, (rowi >= lane).astype(MXU_DTYPE), (rowi <= lane).astype(MXU_DTYPE)


def _sb_scaled_heads(pair, scale):
    lane = _iota(pair.shape, 1)
    val = pair.astype(F32) * scale
    return [jnp.where(lane < SB_HEAD_DIM, val, 0.0).astype(pair.dtype), jnp.where(lane >= SB_HEAD_DIM, val, 0.0).astype(pair.dtype)]


def _sb_logits(qs, kb, bt, strict):
    nt = kb.shape[0] // bt
    full = [_mm_nt(q_head, kb) for q_head in qs]
    xs, nlfs = [], []
    for x in full:
        nlf = jnp.maximum(x, 0.0) + jnp.log(1.0 + jnp.exp(-jnp.abs(x)))
        xs.append([x[:, tt * bt:(tt + 1) * bt] for tt in range(nt)])
        tiles = [nlf[:, tt * bt:(tt + 1) * bt] for tt in range(nt)]
        if strict is not None:
            tiles[-1] = jnp.where(strict, tiles[-1], 0.0)
        nlfs.append(tiles)
    return xs, nlfs


def _sb_tails(nlf_tiles, from_j):
    tails, run = [None] * len(nlf_tiles), None
    for tt in reversed(range(len(nlf_tiles))):
        tail = _mm_exact_rhs(nlf_tiles[tt], from_j)
        tails[tt] = tail if run is None else tail + run
        run = tails[tt][:, 0:1]
    return tails


def _sb_heads(e_tiles, upto_j, pre):
    sums, run = [], pre
    for e in e_tiles:
        sums.append(_mm_exact_rhs(e, upto_j) + run)
        run = sums[-1][:, e.shape[1] - 1:e.shape[1]]
    return sums


def _carried_specs(carried):
    if carried is None:
        return [], [], [], []
    src, per_peer = carried
    rows = src.shape[1:] if per_peer else src.shape
    anywhere = pl.BlockSpec(memory_space=pl.ANY)
    return [anywhere], [src], [anywhere], [jax.ShapeDtypeStruct((N_DEV, *rows), src.dtype)]


def _carried_hooks(carried, comm_refs, first, last):
    if carried is None:
        return lambda: None

    @pl.when(first)
    def _():
        _exchange_start(*comm_refs, per_peer=carried[1])

    def finish():
        @pl.when(last)
        def _():
            _exchange_finish(*comm_refs, per_peer=carried[1])

    return finish


def _sb_attention_fwd(qkv, name, carried=None):
    t = qkv.shape[0]
    bt = min(SB_TILE, t)
    nq = t // bt
    c_in_specs, c_in, c_out_specs, c_out = _carried_specs(carried)

    def body(*refs):
        if carried is None:
            q_ref, k_ref, v_ref, o_ref, acc_ref = refs
            comm_refs = None
        else:
            q_ref, k_ref, v_ref, src_ref, o_ref, land_ref, acc_ref, send_sems, recv_sems, local_sem = refs
            comm_refs = (src_ref, land_ref, send_sems, recv_sems, local_sem)
        i = pl.program_id(1)
        finish = _carried_hooks(carried, comm_refs, (pl.program_id(0) == 0) & (i == 0),
                                (pl.program_id(0) == SB_PAIRS - 1) & (i == nq - 1))
        strict, from_j, _ = _sb_tile_masks(bt)
        qs = _sb_scaled_heads(q_ref[...], SB_SCALE)
        acc_ref[...] = jnp.zeros_like(acc_ref)

        def block(j, nt, carries, diag):
            rows = _kv_rows(j, bt, nt)
            kb, vb = k_ref[rows, :], v_ref[rows, :]
            xs, nlfs = _sb_logits(qs, kb, bt, strict if diag else None)
            tails = [_sb_tails(nlfs[hh], from_j) for hh in range(2)]
            for hh in range(2):
                ws = [jnp.exp(xs[hh][tt] - tails[hh][tt] - carries[hh]) for tt in range(nt)]
                if diag:
                    ws[-1] = jnp.where(strict, ws[-1], 0.0)
                acc_ref[hh] += _mm(jnp.concatenate([w.astype(MXU_DTYPE) for w in ws], axis=1), vb)
            return tuple(carries[hh] + tails[hh][0][:, 0:1] for hh in range(2))

        zero = jnp.zeros((bt, 1), F32)
        carries = block(i, 1, (zero, zero), True)
        carries = lax.fori_loop(0, i // 2, lambda it, cr: block(i - 2 - 2 * it, 2, cr, False), carries)

        @pl.when(i % 2 == 1)
        def _():
            block(0, 1, carries, False)

        low = _iota((bt, 2 * SB_HEAD_DIM), 1) < SB_HEAD_DIM
        o_ref[...] = jnp.where(low, acc_ref[0], acc_ref[1]).astype(o_ref.dtype)
        finish()

    lanes = 2 * SB_HEAD_DIM
    res = pl.pallas_call(
        body, name=name, grid=(SB_PAIRS, nq),
        in_specs=[pl.BlockSpec((bt, lanes), lambda p, i: (i, p)),
                  pl.BlockSpec((t, lanes), lambda p, i: (0, SB_PAIRS + p)),
                  pl.BlockSpec((t, lanes), lambda p, i: (0, 2 * SB_PAIRS + p))] + c_in_specs,
        out_specs=[pl.BlockSpec((bt, lanes), lambda p, i: (i, p))] + c_out_specs,
        out_shape=[jax.ShapeDtypeStruct((t, D_MODEL), MXU_DTYPE)] + c_out,
        scratch_shapes=[pltpu.VMEM((2, bt, lanes), F32)] + (EXCHANGE_SCRATCH if carried else []),
        compiler_params=_cp("arbitrary", "arbitrary"),
    )(qkv, qkv, qkv, *c_in)
    return res[0] if carried is None else res


def _sb_attention_bwd(qkv, do, name, carried=None):
    t = qkv.shape[0]
    bt = min(SB_TILE, t)
    nq = t // bt
    lanes = 2 * SB_HEAD_DIM
    c_in_specs, c_in, c_out_specs, c_out = _carried_specs(carried)

    def body(*refs):
        if carried is None:
            q_ref, k_ref, v_ref, do_ref, dq_ref, dk_ref, dv_ref, sbuf, ebuf, dq_acc, dk_acc, dv_acc = refs
            comm_refs = None
        else:
            (q_ref, k_ref, v_ref, do_ref, src_ref, dq_ref, dk_ref, dv_ref, land_ref,
             sbuf, ebuf, dq_acc, dk_acc, dv_acc, send_sems, recv_sems, local_sem) = refs
            comm_refs = (src_ref, land_ref, send_sems, recv_sems, local_sem)
        i = pl.program_id(1)
        finish = _carried_hooks(carried, comm_refs, (pl.program_id(0) == 0) & (i == 0),
                                (pl.program_id(0) == SB_PAIRS - 1) & (i == nq - 1))

        @pl.when(i == 0)
        def _():
            dk_acc[...] = jnp.zeros_like(dk_acc)
            dv_acc[...] = jnp.zeros_like(dv_acc)

        strict, from_j, upto_j = _sb_tile_masks(bt)
        qs = _sb_scaled_heads(q_ref[...], SB_SCALE)
        dos = _sb_scaled_heads(do_ref[...], 1.0)
        q_both = jnp.concatenate(qs, axis=0)
        do_both = jnp.concatenate(dos, axis=0)
        dq_acc[...] = jnp.zeros_like(dq_acc)

        def pass1(j, nt, carries, diag):
            rows = _kv_rows(j, bt, nt)
            kb, vb = k_ref[rows, :], v_ref[rows, :]
            xs, nlfs = _sb_logits(qs, kb, bt, strict if diag else None)
            dws = [_mm_nt(dos[hh], vb) for hh in range(2)]
            tails = [_sb_tails(nlfs[hh], from_j) for hh in range(2)]
            wcat = []
            for hh in range(2):
                ws = [jnp.exp(xs[hh][tt] - tails[hh][tt] - carries[hh]) for tt in range(nt)]
                if diag:
                    ws[-1] = jnp.where(strict, ws[-1], 0.0)
                w_all = jnp.concatenate(ws, axis=1)
                sbuf[hh, :, rows] = jnp.exp(jnp.concatenate([xs[hh][tt] - nlfs[hh][tt] for tt in range(nt)], axis=1))
                ebuf[hh, :, rows] = w_all * dws[hh]
                wcat.append(w_all.astype(MXU_DTYPE))
            dv_acc[rows, :] += _mm_tn(jnp.concatenate(wcat, axis=0), do_both)
            return tuple(carries[hh] + tails[hh][0][:, 0:1] for hh in range(2))

        zero = jnp.zeros((bt, 1), F32)
        carries = pass1(i, 1, (zero, zero), True)
        carries = lax.fori_loop(0, i // 2, lambda it, cr: pass1(i - 2 - 2 * it, 2, cr, False), carries)

        @pl.when(i % 2 == 1)
        def _():
            pass1(0, 1, carries, False)

        def pass2(j, nt, pres, diag):
            rows = _kv_rows(j, bt, nt)
            kb = k_ref[rows, :]
            sums = [_sb_heads([ebuf[hh, :, _kv_rows(j + tt, bt)] for tt in range(nt)], upto_j, pres[hh]) for hh in range(2)]
            dxm = []
            for hh in range(2):
                dxs = [ebuf[hh, :, _kv_rows(j + tt, bt)] - sbuf[hh, :, _kv_rows(j + tt, bt)] * sums[hh][tt] for tt in range(nt)]
                if diag:
                    dxs[-1] = jnp.where(strict, dxs[-1], 0.0)
                dxm.append(jnp.concatenate(dxs, axis=1).astype(MXU_DTYPE))
                dq_acc[hh] += _mm(dxm[hh], kb)
            dk_acc[rows, :] += _mm_tn(jnp.concatenate(dxm, axis=0), q_both)
            return tuple(sums[hh][-1][:, bt - 1:bt] for hh in range(2))

        pres = lax.fori_loop(0, i // 2, lambda it, pr: pass2(2 * it, 2, pr, False), (zero, zero))

        @pl.when(i % 2 == 0)
        def _():
            pass2(i, 1, pres, True)

        @pl.when(i % 2 == 1)
        def _():
            pass2(i - 1, 2, pres, True)

        low = _iota((bt, lanes), 1) < SB_HEAD_DIM
        dq_ref[...] = (jnp.where(low, dq_acc[0], dq_acc[1]) * SB_SCALE).astype(dq_ref.dtype)

        @pl.when(i == nq - 1)
        def _():
            dk_ref[...] = dk_acc[...].astype(dk_ref.dtype)
            dv_ref[...] = dv_acc[...].astype(dv_ref.dtype)

        finish()

    blk = pl.BlockSpec((bt, lanes), lambda p, i: (i, p))
    whole = pl.BlockSpec((t, lanes), lambda p, i: (0, p))
    out = jax.ShapeDtypeStruct((t, D_MODEL), MXU_DTYPE)
    return pl.pallas_call(
        body, name=name, grid=(SB_PAIRS, nq),
        in_specs=[blk, pl.BlockSpec((t, lanes), lambda p, i: (0, SB_PAIRS + p)),
                  pl.BlockSpec((t, lanes), lambda p, i: (0, 2 * SB_PAIRS + p)), blk] + c_in_specs,
        out_specs=[blk, whole, whole] + c_out_specs, out_shape=[out, out, out] + c_out,
        scratch_shapes=[pltpu.VMEM((2, bt, t), F32), pltpu.VMEM((2, bt, t), F32), pltpu.VMEM((2, bt, lanes), F32),
                        pltpu.VMEM((t, lanes), F32), pltpu.VMEM((t, lanes), F32)] + (EXCHANGE_SCRATCH if carried else []),
        compiler_params=_cp("arbitrary", "arbitrary"),
    )(qkv, qkv, qkv, do, *c_in)


def _add_pair(a, b, name):
    s, r, c = a.shape
    tm = _row_tile(r)

    def body(a_ref, b_ref, o_ref):
        o_ref[...] = (a_ref[...].astype(F32) + b_ref[...].astype(F32)).astype(o_ref.dtype)

    blk = pl.BlockSpec((1, tm, c), lambda q, i: (q, i, 0))
    return pl.pallas_call(body, name=name, grid=(s, r // tm), in_specs=[blk, blk], out_specs=blk,
                          out_shape=jax.ShapeDtypeStruct(a.shape, a.dtype), compiler_params=_cp("parallel", "parallel"))(a, b)


def _sum_slots(gslots, name):
    s, r, c = gslots.shape

    def body(g_ref, o_ref):
        g = g_ref[0].astype(F32)
        for q in range(1, s):
            g = g + g_ref[q].astype(F32)
        o_ref[...] = g

    return pl.pallas_call(
        body, name=name, grid=(c // LANES,),
        in_specs=[pl.BlockSpec((s, r, LANES), lambda j: (0, 0, j))], out_specs=pl.BlockSpec((r, LANES), lambda j: (0, j)),
        out_shape=jax.ShapeDtypeStruct((r, c), F32), compiler_params=_cp("parallel"),
    )(gslots)


def _adamw(gslots, w, m, v, name):
    s, r, c = gslots.shape
    tm = _row_tile(r)
    assert w.shape == (r, c), (w.shape, gslots.shape)
    c1 = 1.0 - ADAM_B1 ** ADAM_STEP
    c2 = 1.0 - ADAM_B2 ** ADAM_STEP

    def body(g_ref, w_ref, m_ref, v_ref, go_ref, d_ref, mo_ref, vo_ref):
        g = g_ref[0].astype(F32)
        for q in range(1, s):
            g = g + g_ref[q].astype(F32)
        mn = ADAM_B1 * m_ref[...] + (1.0 - ADAM_B1) * g
        vn = ADAM_B2 * v_ref[...] + (1.0 - ADAM_B2) * (g * g)
        go_ref[...] = g
        mo_ref[...] = mn
        vo_ref[...] = vn
        d_ref[...] = -ADAM_LR * ((mn / c1) / (jnp.sqrt(vn / c2) + ADAM_EPS) + ADAM_WD * w_ref[...])

    row = pl.BlockSpec((tm, c), lambda i: (i, 0))
    out = jax.ShapeDtypeStruct((r, c), F32)
    return pl.pallas_call(
        body, name=name, grid=(r // tm,),
        in_specs=[pl.BlockSpec((s, tm, c), lambda i: (0, i, 0)), row, row, row],
        out_specs=[row, row, row, row], out_shape=[out, out, out, out], compiler_params=_cp("parallel"),
    )(gslots, w, m, v)


def _rows(a):
    flat = a.reshape(-1)
    pad = (-flat.shape[0]) % PACK_W
    if pad:
        flat = jnp.concatenate([flat, jnp.zeros((pad,), flat.dtype)])
    return flat.reshape(-1, PACK_W)


def _pack(arrays, row_multiple):
    parts, layout, off = [], [], 0
    for a in arrays:
        rw = _rows(a)
        parts.append(rw)
        layout.append((off, rw.shape[0], a.shape))
        off += rw.shape[0]
    pad = (-off) % row_multiple
    if pad:
        parts.append(jnp.zeros((pad, PACK_W), parts[0].dtype))
    return jnp.concatenate(parts, axis=0), layout


def _unpack(packed, layout):
    out = []
    for off, nrows, shape in layout:
        n = int(np.prod(shape))
        out.append(packed[off:off + nrows].reshape(-1)[:n].reshape(shape))
    return out


def _shard_as_rows(name, shard):
    if name in COL_SHARDED:
        shard = shard.transpose(0, 2, 1)
    return shard.reshape(-1, PACK_W)


def _rows_as_shard(name, rows, shape):
    if name in COL_SHARDED:
        lead, k, ns = shape
        return rows.reshape(lead, ns, k).transpose(0, 2, 1)
    return rows.reshape(shape)


def _row_tile(r):
    return next(tm for tm in (256, 128, 64, 32, 16, 8) if r % tm == 0)


def _ssd_consts(dt_bias, a_log, d_skip):
    pad = LANES - SSD_HEADS
    bias = jnp.pad(dt_bias, (0, pad)).reshape(1, LANES)
    a_neg = jnp.pad(-jnp.exp(a_log), (0, pad)).reshape(1, LANES)
    d_exp = jnp.repeat(d_skip, SSD_HEAD_DIM).reshape(1, SSD_D_INNER)
    return bias, a_neg, d_exp


def _group_layouts(acum):
    t = acum.shape[0]
    a = acum[:, :SSD_HEADS].reshape(t, SSD_GROUPS, HEADS_PER_GROUP)
    return a.transpose(1, 0, 2), a.transpose(1, 2, 0)


def _ssd_fwd(x, p, carried=None):
    hn = _rmsnorm(x, p["mix_norm"], "rmsnorm_fwd")
    proj = _matmul(hn, p["w_in"], "nt", F32, "ssd_in_fwd", tm=512, tn=896, tk=1024)
    act = _ssd_conv_fwd(proj, p["conv_w"], p["conv_b"], "ssd_conv_fwd")
    bias, a_neg, d_exp = _ssd_consts(p["dt_bias"], p["a_log"], p["d"])
    expand = _head_expand()
    xdt, dt, acum = _ssd_dt_fwd(proj, act, bias, a_neg, expand, "ssd_dt_fwd")
    acum_g, acum_gt = _group_layouts(acum)
    if carried is None:
        (y, states), landed = _ssd_scan_fwd(xdt, act, acum_g, acum_gt, "ssd_scan_fwd"), None
    else:
        y, states, landed = _ssd_scan_fwd(xdt, act, acum_g, acum_gt, "ssd_scan_fwd_carrying_gather", carried)
    yn = _gated_norm_fwd(y, act, proj, d_exp, p["norm"], "ssd_gnorm_fwd")
    x_new = _matmul(yn, p["w_out"], "nn", F32, "ssd_out_fwd", add=x, tm=512, tn=1024, tk=2048)
    saved = dict(x=x, hn=hn, proj=proj, act=act, xdt=xdt, dt=dt, acum_g=acum_g, acum_gt=acum_gt, y=y, states=states, yn=yn)
    return x_new, saved, landed


def _ssd_bwd(dx, p, s, carried=None):
    bias, a_neg, d_exp = _ssd_consts(p["dt_bias"], p["a_log"], p["d"])
    expand = _head_expand()
    dyn = _matmul(dx, p["w_out"], "nt", F32, "ssd_out_dgrad", tm=512, tn=1024, tk=1024)
    g_w_out = _matmul(s["yn"], dx, "tn", MXU_DTYPE, "ssd_out_wgrad", tm=1024, tn=1024, tk=512)
    dyy, dz, g_norm = _gated_norm_bwd(s["y"], s["act"], s["proj"], d_exp, p["norm"], dyn, "ssd_gnorm_bwd")
    scan_args = (s["xdt"], s["act"], s["acum_g"], s["acum_gt"], s["states"], dyy)
    if carried is None:
        (dxdt, dbm, dcm, dacol, darow), landed = _ssd_scan_bwd(*scan_args, "ssd_scan_bwd"), None
    else:
        dxdt, dbm, dcm, dacol, darow, landed = _ssd_scan_bwd(*scan_args, "ssd_scan_bwd_carrying_grads", carried)
    t = dx.shape[0]
    dacum = dacol.transpose(1, 0, 2).reshape(t, SSD_HEADS) + darow.transpose(2, 0, 1).reshape(t, SSD_HEADS)
    dacum = jnp.pad(dacum, ((0, 0), (0, LANES - SSD_HEADS)))
    dxs, draw, g_a, g_bias, g_dexp = _ssd_dt_bwd(s["proj"], s["act"], s["dt"], dxdt, dyy, dacum, bias, a_neg, d_exp,
                                                  expand, expand.T, "ssd_dt_bwd")
    dact = jnp.concatenate([dxs, dbm, dcm], axis=1)
    dxbc, g_conv_w, g_conv_b = _ssd_conv_bwd(s["proj"], p["conv_w"], p["conv_b"], dact, "ssd_conv_bwd")
    dproj = jnp.concatenate([dz, dxbc, draw], axis=1)
    dhn = _matmul(dproj, p["w_in"], "nn", F32, "ssd_in_dgrad", tm=512, tn=1024, tk=896)
    g_w_in = _matmul(dproj, s["hn"], "tn", MXU_DTYPE, "ssd_in_wgrad", tm=896, tn=1024, tk=512)
    dx_new, g_mix = _rmsnorm_bwd(s["x"], p["mix_norm"], dhn, dx, "rmsnorm_bwd")
    grads = dict(w_in=g_w_in[:SSD_IN_DIM], w_out=g_w_out, conv_w=g_conv_w, conv_b=g_conv_b.reshape(-1),
                 dt_bias=g_bias[0, :SSD_HEADS], a_log=(g_a * a_neg)[0, :SSD_HEADS],
                 d=g_dexp.reshape(SSD_HEADS, SSD_HEAD_DIM).sum(axis=1), norm=g_norm.reshape(-1), mix_norm=g_mix.reshape(-1))
    return dx_new, grads, landed


def _sb_fwd(x, p, carried=None):
    hn = _rmsnorm(x, p["mix_norm"], "rmsnorm_fwd")
    qkv = _matmul(hn, p["w_qkv"], "nt", MXU_DTYPE, "sb_qkv_fwd", tm=512, tn=1024, tk=1024)
    if carried is None:
        o, landed = _sb_attention_fwd(qkv, "sb_attn_fwd"), None
    else:
        o, landed = _sb_attention_fwd(qkv, "sb_attn_fwd_carrying_gather", carried)
    x_new = _matmul(o, p["w_out"], "nn", F32, "sb_out_fwd", add=x, tm=512, tn=1024, tk=1024)
    return x_new, dict(x=x, hn=hn, qkv=qkv, o=o), landed


def _sb_bwd(dx, p, s, carried_of=None):
    do = _matmul(dx, p["w_out"], "nt", MXU_DTYPE, "sb_out_dgrad", tm=512, tn=1024, tk=1024)
    g_w_out = _matmul(s["o"], dx, "tn", MXU_DTYPE, "sb_out_wgrad", tm=1024, tn=1024, tk=512)
    if carried_of is None:
        (dq, dk, dv), landed = _sb_attention_bwd(s["qkv"], do, "sb_attn_bwd"), None
    else:
        dq, dk, dv, landed = _sb_attention_bwd(s["qkv"], do, "sb_attn_bwd_carrying_grads", carried_of(g_w_out))
    dqkv = jnp.concatenate([dq, dk, dv], axis=1)
    dhn = _matmul(dqkv, p["w_qkv"], "nn", F32, "sb_qkv_dgrad", tm=512, tn=1024, tk=1024)
    g_w_qkv = _matmul(dqkv, s["hn"], "tn", MXU_DTYPE, "sb_qkv_wgrad", tm=1024, tn=1024, tk=512)
    dx_new, g_mix = _rmsnorm_bwd(s["x"], p["mix_norm"], dhn, dx, "rmsnorm_bwd")
    return dx_new, dict(w_qkv=g_w_qkv, w_out=g_w_out, mix_norm=g_mix.reshape(-1)), landed


def _ffn_fwd(x, p):
    hn = _rmsnorm(x, p["ffn_norm"], "rmsnorm_fwd")
    proj = _matmul(hn, p["w_in"], "nt", F32, "ffn_in_fwd", tm=512, tn=1408, tk=1024)
    act = _ffn_conv_fwd(proj, p["conv_w"], p["conv_b"], "ffn_conv_fwd")
    x_new = _matmul(act, p["w_out"], "nn", F32, "ffn_out_fwd", add=x, tm=512, tn=1024, tk=1408)
    return x_new, dict(x=x, hn=hn, proj=proj, act=act)


def _ffn_bwd(dx, p, s):
    dact = _matmul(dx, p["w_out"], "nt", F32, "ffn_out_dgrad", tm=512, tn=1408, tk=1024)
    g_w_out = _matmul(s["act"], dx, "tn", MXU_DTYPE, "ffn_out_wgrad", tm=1408, tn=1024, tk=512)
    dpg, dpu, dwg, dwu, dbg, dbu = _ffn_conv_bwd(s["proj"], p["conv_w"], p["conv_b"], dact, "ffn_conv_bwd")
    dproj = jnp.concatenate([dpg, dpu], axis=1)
    dhn = _matmul(dproj, p["w_in"], "nn", F32, "ffn_in_dgrad", tm=512, tn=1024, tk=1408)
    g_w_in = _matmul(dproj, s["hn"], "tn", MXU_DTYPE, "ffn_in_wgrad", tm=1408, tn=1024, tk=512)
    dx_new, g_norm = _rmsnorm_bwd(s["x"], p["ffn_norm"], dhn, dx, "rmsnorm_bwd")
    grads = dict(w_in=g_w_in, w_out=g_w_out, conv_w=jnp.concatenate([dwg, dwu], axis=1),
                 conv_b=jnp.concatenate([dbg, dbu], axis=1).reshape(-1), ffn_norm=g_norm.reshape(-1))
    return dx_new, grads


ADD_ROWS = 256
BIG = ["ssd_w_in", "sb_w_qkv", "ffn_w_in", "ssd_w_out", "sb_w_out", "ffn_w_out"]
LAYER_PIECES = [[("ssd_w_in", 0), ("ssd_w_out", 0), ("ffn_w_in", 0), ("ffn_w_out", 0)],
                [("sb_w_qkv", 0), ("sb_w_out", 0), ("ffn_w_in", 1), ("ffn_w_out", 1)],
                [("ssd_w_in", 1), ("ssd_w_out", 1), ("ffn_w_in", 2), ("ffn_w_out", 2)],
                [("sb_w_qkv", 1), ("sb_w_out", 1), ("ffn_w_in", 3), ("ffn_w_out", 3)]]
GRAD_SETS = {3: [("ffn_w_in", 3), ("ffn_w_out", 3), ("sb_w_out", 1)],
             1: [("sb_w_qkv", 1), ("ssd_w_out", 1), ("ffn_w_in", 2), ("ffn_w_out", 2), ("ffn_w_in", 1), ("ffn_w_out", 1),
                 ("sb_w_out", 0), ("ssd_w_in", 1)],
             0: [("sb_w_qkv", 0), ("ffn_w_in", 0), ("ffn_w_out", 0)],
             "end": [("ssd_w_out", 0), ("ssd_w_in", 0)]}
GATHER_SETS = {"early": [("ssd_w_out", 0), ("ffn_w_in", 0), ("ffn_w_out", 0), ("ssd_w_in", 0)],
               0: LAYER_PIECES[1],
               1: [("ssd_w_out", 1), ("ffn_w_in", 2), ("ffn_w_out", 2)] + LAYER_PIECES[3] + [("ssd_w_in", 1)]}
COL_SHARDED = {"ssd_w_in": 2, "sb_w_qkv": 2, "ffn_w_in": 4}
CONV = ["ssd_conv_w", "ffn_conv_w"]
SMALL = ["mix_norm", "ffn_norm", "final_norm", "ssd_conv_b", "ssd_dt_bias", "ssd_a_log", "ssd_d", "ssd_norm", "ffn_conv_b"]
WEIGHTS = ["mix_norm", "ffn_norm", "final_norm", "ssd_w_in", "ssd_conv_w", "ssd_conv_b", "ssd_dt_bias", "ssd_a_log", "ssd_d",
           "ssd_norm", "ssd_w_out", "sb_w_qkv", "sb_w_out", "ffn_w_in", "ffn_conv_w", "ffn_conv_b", "ffn_w_out"]


def _step(x, loss_target, w, m, v):
    x = x.reshape(x.shape[-2], x.shape[-1])
    target = loss_target.reshape(x.shape)
    dev = 4 * lax.axis_index("x") + 2 * lax.axis_index("y") + lax.axis_index("c")
    core = lax.axis_index("c")

    shard_rows = {n: _shard_as_rows(n, w[n].astype(MXU_DTYPE)) for n in BIG}
    per_shard = {n: shard_rows[n].shape[0] // w[n].shape[0] for n in BIG}

    def layout(pieces):
        where, off = {}, 0
        for n, l in pieces:
            where[(n, l)] = (off, per_shard[n])
            off += per_shard[n]
        return where

    def pack_pieces(pieces, rows_of):
        return jnp.concatenate([rows_of(piece) for piece in pieces], axis=-2)

    def shard_piece(piece):
        n, l = piece
        return shard_rows[n][l * per_shard[n]:(l + 1) * per_shard[n]]

    full = {}

    def unpack_weights(gathered, where):
        for (n, l), (off, rows) in where.items():
            mat = gathered[:, off:off + rows].reshape(N_DEV * rows, PACK_W)
            if n == "ssd_w_in":
                mat = jnp.pad(mat, ((0, SSD_IN_PAD - SSD_IN_DIM), (0, 0)))
            full[(n, l)] = mat

    unpack_weights(_all_gather(pack_pieces(GATHER_SETS["early"], shard_piece), "gather_weights_early"), layout(GATHER_SETS["early"]))
    conv_pack, conv_layout = _pack([w[n] for n in CONV], 8)
    conv_all = _all_gather(conv_pack, "gather_conv_taps")
    for n, (off, nrows, shape) in zip(CONV, conv_layout):
        parts = [_unpack(conv_all[j], conv_layout)[CONV.index(n)] for j in range(N_DEV)]
        full[n] = jnp.concatenate(parts, axis=-1)

    def ssd_params(j):
        return dict(mix_norm=w["mix_norm"][2 * j], w_in=full[("ssd_w_in", j)], conv_w=full["ssd_conv_w"][j],
                    conv_b=w["ssd_conv_b"][j], dt_bias=w["ssd_dt_bias"][j], a_log=w["ssd_a_log"][j], d=w["ssd_d"][j],
                    norm=w["ssd_norm"][j], w_out=full[("ssd_w_out", j)])

    def sb_params(j):
        return dict(mix_norm=w["mix_norm"][2 * j + 1], w_qkv=full[("sb_w_qkv", j)], w_out=full[("sb_w_out", j)])

    def ffn_params(i):
        return dict(ffn_norm=w["ffn_norm"][i], w_in=full[("ffn_w_in", i)], conv_w=full["ffn_conv_w"][i],
                    conv_b=w["ffn_conv_b"][i], w_out=full[("ffn_w_out", i)])

    saved = []
    for i in range(DEPTH):
        mixer_fwd, params = (_ssd_fwd, ssd_params) if i % 2 == 0 else (_sb_fwd, sb_params)
        if i in GATHER_SETS:
            x, s_mix, arrived = mixer_fwd(x, params(i // 2), carried=(pack_pieces(GATHER_SETS[i], shard_piece), False))
            unpack_weights(arrived, layout(GATHER_SETS[i]))
        else:
            x, s_mix, _ = mixer_fwd(x, params(i // 2))
        x, s_ffn = _ffn_fwd(x, ffn_params(i))
        saved.append((s_mix, s_ffn))
    dx, g_final, loss_part = _final_norm_loss(x, w["final_norm"], target, "final_norm_loss")

    piece_grad = {}

    def grad_piece(piece):
        g = piece_grad[piece]
        return g.reshape(N_DEV, g.shape[0] // N_DEV, PACK_W)

    def carried_set(pieces, own_piece):
        def make(g_w_out):
            piece_grad[own_piece] = g_w_out
            return pack_pieces(pieces, grad_piece), True
        return make

    g_mix, g_ffn, g_ssd, g_sb = [None] * DEPTH, [None] * DEPTH, [None] * 2, [None] * 2
    landed = {}
    for i in reversed(range(DEPTH)):
        s_mix, s_ffn = saved[i]
        dx, g_ffn[i] = _ffn_bwd(dx, ffn_params(i), s_ffn)
        piece_grad[("ffn_w_in", i)], piece_grad[("ffn_w_out", i)] = g_ffn[i]["w_in"], g_ffn[i]["w_out"]
        j = i // 2
        if i % 2 == 0:
            carried = (pack_pieces(GRAD_SETS[i], grad_piece), True) if i in GRAD_SETS else None
            dx, g_ssd[j], landed[i] = _ssd_bwd(dx, ssd_params(j), s_mix, carried)
            g_mix[i] = g_ssd[j]["mix_norm"]
            piece_grad[("ssd_w_in", j)], piece_grad[("ssd_w_out", j)] = g_ssd[j]["w_in"], g_ssd[j]["w_out"]
        else:
            dx, g_sb[j], landed[i] = _sb_bwd(dx, sb_params(j), s_mix, carried_set(GRAD_SETS[i], ("sb_w_out", j)))
            g_mix[i] = g_sb[j]["mix_norm"]
            piece_grad[("sb_w_qkv", j)] = g_sb[j]["w_qkv"]
    grad_x = dx.reshape(1, *dx.shape)

    g8 = pack_pieces(GRAD_SETS["end"], grad_piece)
    g8 = jnp.pad(g8, ((0, 0), (0, (-g8.shape[1]) % ADD_ROWS), (0, 0)))
    g8 = g8.reshape(4, 2, *g8.shape[1:])
    keep = lax.dynamic_index_in_dim(g8, core, axis=1, keepdims=False)
    give = lax.dynamic_index_in_dim(g8, 1 - core, axis=1, keepdims=False)
    got = _swap_with_sibling(give, "grads_to_sibling")
    chip_part = _add_pair(keep, got, "grads_add_sibling")
    landed["end"] = _exchange_chips(chip_part, "grads_across_chips")

    summed = {}
    for key, pieces in GRAD_SETS.items():
        total = _sum_slots(landed[key], "grads_sum_landed")
        for piece, (off, rows) in layout(pieces).items():
            summed[piece] = total[off:off + rows]
    big_res = [dict() for _ in range(4)]
    for n in BIG:
        lead, rows, cols = w[n].shape
        g_rows = jnp.concatenate([summed[(n, l)] for l in range(lead)], axis=0)
        g_nat = _rows_as_shard(n, g_rows, w[n].shape).reshape(1, lead * rows, cols)
        two_d = (lead * rows, cols)
        outs = _adamw(g_nat, w[n].reshape(two_d), m[n].reshape(two_d), v[n].reshape(two_d), "adamw_" + n)
        for kind in range(4):
            big_res[kind][n] = outs[kind].reshape(w[n].shape)

    small_g = {
        "mix_norm": jnp.stack(g_mix), "ffn_norm": jnp.stack([g["ffn_norm"] for g in g_ffn]), "final_norm": g_final.reshape(-1),
        "ssd_conv_b": jnp.stack([g["conv_b"] for g in g_ssd]), "ssd_dt_bias": jnp.stack([g["dt_bias"] for g in g_ssd]),
        "ssd_a_log": jnp.stack([g["a_log"] for g in g_ssd]), "ssd_d": jnp.stack([g["d"] for g in g_ssd]),
        "ssd_norm": jnp.stack([g["norm"] for g in g_ssd]), "ffn_conv_b": jnp.stack([g["conv_b"] for g in g_ffn]),
    }
    conv_g = {"ssd_conv_w": jnp.stack([g["conv_w"] for g in g_ssd]), "ffn_conv_w": jnp.stack([g["conv_w"] for g in g_ffn])}
    extra = [conv_g[n] for n in CONV] + [loss_part]
    small_pack, small_layout = _pack([small_g[n] for n in SMALL] + extra, 8)
    small_all = _all_gather(small_pack, "gather_small_grads")
    zeros_like = [jnp.zeros(a.shape, F32) for a in extra]
    sw, _ = _pack([w[n] for n in SMALL] + zeros_like, 8)
    sm, _ = _pack([m[n] for n in SMALL] + zeros_like, 8)
    sv, _ = _pack([v[n] for n in SMALL] + [jnp.ones(a.shape, F32) for a in extra], 8)
    small_out = _adamw(small_all, sw, sm, sv, "adamw_replicated")
    small_res = [_unpack(o, small_layout) for o in small_out]
    summed = small_res[0]
    loss = summed[-1][0, 0]
    conv_shard_g = []
    for n, gsum in zip(CONV, summed[len(SMALL):len(SMALL) + len(CONV)]):
        ns = w[n].shape[-1]
        conv_shard_g.append(lax.dynamic_slice_in_dim(gsum, dev * ns, ns, axis=2))
    cg, conv_sh_layout = _pack(conv_shard_g, 8)
    cw, _ = _pack([w[n] for n in CONV], 8)
    cm_, _ = _pack([m[n] for n in CONV], 8)
    cv, _ = _pack([v[n] for n in CONV], 8)
    conv_out = _adamw(cg.reshape(1, *cg.shape), cw, cm_, cv, "adamw_conv_taps")
    conv_res = [dict(zip(CONV, _unpack(o, conv_sh_layout))) for o in conv_out]

    def pick(kind, n):
        if n in BIG:
            return big_res[kind][n]
        if n in CONV:
            return conv_res[kind][n]
        return small_res[kind][SMALL.index(n)]

    outs = [loss, grad_x]
    for kind in range(4):
        outs += [pick(kind, n) for n in WEIGHTS]
    return tuple(outs)


def kernel(x, mix_norm, ffn_norm, final_norm, ssd_w_in, ssd_conv_w, ssd_conv_b, ssd_dt_bias, ssd_a_log, ssd_d, ssd_norm, ssd_w_out, sb_w_qkv, sb_w_out, ffn_w_in, ffn_conv_w, ffn_conv_b, ffn_w_out, loss_target, m_mix_norm, m_ffn_norm, m_final_norm, m_ssd_w_in, m_ssd_conv_w, m_ssd_conv_b, m_ssd_dt_bias, m_ssd_a_log, m_ssd_d, m_ssd_norm, m_ssd_w_out, m_sb_w_qkv, m_sb_w_out, m_ffn_w_in, m_ffn_conv_w, m_ffn_conv_b, m_ffn_w_out, v_mix_norm, v_ffn_norm, v_final_norm, v_ssd_w_in, v_ssd_conv_w, v_ssd_conv_b, v_ssd_dt_bias, v_ssd_a_log, v_ssd_d, v_ssd_norm, v_ssd_w_out, v_sb_w_qkv, v_sb_w_out, v_ffn_w_in, v_ffn_conv_w, v_ffn_conv_b, v_ffn_w_out):
    w = dict(mix_norm=mix_norm, ffn_norm=ffn_norm, final_norm=final_norm, ssd_w_in=ssd_w_in, ssd_conv_w=ssd_conv_w,
             ssd_conv_b=ssd_conv_b, ssd_dt_bias=ssd_dt_bias, ssd_a_log=ssd_a_log, ssd_d=ssd_d, ssd_norm=ssd_norm,
             ssd_w_out=ssd_w_out, sb_w_qkv=sb_w_qkv, sb_w_out=sb_w_out, ffn_w_in=ffn_w_in, ffn_conv_w=ffn_conv_w,
             ffn_conv_b=ffn_conv_b, ffn_w_out=ffn_w_out)
    m = dict(mix_norm=m_mix_norm, ffn_norm=m_ffn_norm, final_norm=m_final_norm, ssd_w_in=m_ssd_w_in, ssd_conv_w=m_ssd_conv_w,
             ssd_conv_b=m_ssd_conv_b, ssd_dt_bias=m_ssd_dt_bias, ssd_a_log=m_ssd_a_log, ssd_d=m_ssd_d, ssd_norm=m_ssd_norm,
             ssd_w_out=m_ssd_w_out, sb_w_qkv=m_sb_w_qkv, sb_w_out=m_sb_w_out, ffn_w_in=m_ffn_w_in, ffn_conv_w=m_ffn_conv_w,
             ffn_conv_b=m_ffn_conv_b, ffn_w_out=m_ffn_w_out)
    v = dict(mix_norm=v_mix_norm, ffn_norm=v_ffn_norm, final_norm=v_final_norm, ssd_w_in=v_ssd_w_in, ssd_conv_w=v_ssd_conv_w,
             ssd_conv_b=v_ssd_conv_b, ssd_dt_bias=v_ssd_dt_bias, ssd_a_log=v_ssd_a_log, ssd_d=v_ssd_d, ssd_norm=v_ssd_norm,
             ssd_w_out=v_ssd_w_out, sb_w_qkv=v_sb_w_qkv, sb_w_out=v_sb_w_out, ffn_w_in=v_ffn_w_in, ffn_conv_w=v_ffn_conv_w,
             ffn_conv_b=v_ffn_conv_b, ffn_w_out=v_ffn_w_out)
    return _step(x, loss_target, w, m, v)
```

```python
import jax
import jax.numpy as jnp
import numpy as np
from jax import lax
from jax.experimental import pallas as pl
from jax.experimental.pallas import tpu as pltpu

F32 = jnp.float32
MXU_DTYPE = jnp.bfloat16
MESH_ID = pl.DeviceIdType.MESH
N_DEV = 8

NORM_EPS = 1e-6
D_MODEL = 1024
DEPTH = 4
SSD_D_INNER = 2048
SSD_HEADS = 32
SSD_HEAD_DIM = 64
SSD_GROUPS = 8
SSD_STATE = 128
SSD_CONV = 4
SSD_CHUNK = 128
SSD_CONV_DIM = SSD_D_INNER + 2 * SSD_GROUPS * SSD_STATE
SSD_IN_DIM = SSD_D_INNER + SSD_CONV_DIM + SSD_HEADS
LANES = 128
SSD_IN_PAD = SSD_D_INNER + SSD_CONV_DIM + LANES
SB_HEADS = 16
SB_HEAD_DIM = 64
SB_TILE = 256
SB_SCALE = SB_HEAD_DIM ** -0.5
FFN_D_FF = 2816
FFN_CONV = 3
PACK_W = 1024

ADAM_LR = 0.001
ADAM_B1 = 0.9
ADAM_B2 = 0.999
ADAM_EPS = 1e-08
ADAM_WD = 0.01
ADAM_STEP = 10

VMEM_LIMIT_BYTES = 56 * 1024 * 1024


def _cp(*sem):
    return pltpu.CompilerParams(dimension_semantics=sem, vmem_limit_bytes=VMEM_LIMIT_BYTES)


def _iota(shape, dim):
    return lax.broadcasted_iota(jnp.int32, shape, dim)


def _sigmoid(x):
    return 1.0 / (1.0 + jnp.exp(-x))


def _mm(a, b):
    return lax.dot_general(a, b, (((1,), (0,)), ((), ())), preferred_element_type=F32)


def _mm_nt(a, b):
    return lax.dot_general(a, b, (((1,), (1,)), ((), ())), preferred_element_type=F32)


def _mm_tn(a, b):
    return lax.dot_general(a, b, (((0,), (0,)), ((), ())), preferred_element_type=F32)


def _split(x):
    hi = x.astype(MXU_DTYPE)
    lo = (x - hi.astype(F32)).astype(MXU_DTYPE)
    return hi, lo


def _mm_exact_rhs(x, m):
    hi, lo = _split(x)
    return _mm(jnp.concatenate([hi, lo], axis=1), jnp.concatenate([m, m], axis=0))


def _mm_exact_lhs(m, x):
    hi, lo = _split(x)
    return _mm(jnp.concatenate([m, m], axis=1), jnp.concatenate([hi, lo], axis=0))


def _my_place():
    return lax.axis_index("x"), lax.axis_index("y"), lax.axis_index("c")


def _all_gather(shard, name):
    r, c_ = shard.shape

    def body(x_ref, out_ref, send_sems, recv_sems, local_sem):
        x, y, c = _my_place()
        me, sibling = (x, y, c), (x, y, 1 - c)
        chips = [(1 - x, y), (x, 1 - y), (1 - x, 1 - y)]

        def slot(px, py, pc):
            return out_ref.at[4 * px + 2 * py + pc]

        def copy(k, block, to, src=None):
            return pltpu.make_async_remote_copy(
                src_ref=slot(*block) if src is None else src, dst_ref=slot(*block),
                send_sem=send_sems.at[k], recv_sem=recv_sems.at[k], device_id=to, device_id_type=MESH_ID)

        mine = pltpu.make_async_copy(x_ref, slot(*me), local_sem)
        mine.start()
        first = [copy(0, me, sibling, src=x_ref)]
        first += [copy(1 + j, me, (*chip, c), src=x_ref) for j, chip in enumerate(chips)]
        for cp in first:
            cp.start()
        passed = [copy(4 + j, (*chip, c), sibling) for j, chip in enumerate(chips)]
        for j, chip in enumerate(chips):
            copy(1 + j, (*chip, c), me).wait_recv()
            passed[j].start()
        copy(0, sibling, me).wait_recv()
        for j, chip in enumerate(chips):
            copy(4 + j, (*chip, 1 - c), me).wait_recv()
        for cp in first + passed:
            cp.wait_send()
        mine.wait()

    return pl.pallas_call(
        body, name=name,
        out_shape=jax.ShapeDtypeStruct((N_DEV, r, c_), shard.dtype),
        in_specs=[pl.BlockSpec(memory_space=pl.ANY)],
        out_specs=pl.BlockSpec(memory_space=pl.ANY),
        scratch_shapes=[pltpu.SemaphoreType.DMA((7,)), pltpu.SemaphoreType.DMA((7,)), pltpu.SemaphoreType.DMA(())],
    )(shard)


def _swap_with_sibling(buf, name):
    def body(x_ref, out_ref, send_sem, recv_sem):
        x, y, c = _my_place()
        cp = pltpu.make_async_remote_copy(src_ref=x_ref, dst_ref=out_ref, send_sem=send_sem, recv_sem=recv_sem,
                                          device_id=(x, y, 1 - c), device_id_type=MESH_ID)
        cp.start()
        cp.wait()

    return pl.pallas_call(
        body, name=name, out_shape=jax.ShapeDtypeStruct(buf.shape, buf.dtype),
        in_specs=[pl.BlockSpec(memory_space=pl.ANY)], out_specs=pl.BlockSpec(memory_space=pl.ANY),
        scratch_shapes=[pltpu.SemaphoreType.DMA(()), pltpu.SemaphoreType.DMA(())],
    )(buf)


def _exchange_chips(parts, name):
    def body(p_ref, out_ref, send_sems, recv_sems, local_sem):
        x, y, c = _my_place()
        my_q = 2 * x + y
        chips = [(1 - x, y), (x, 1 - y), (1 - x, 1 - y)]
        local = pltpu.make_async_copy(p_ref.at[my_q], out_ref.at[my_q], local_sem)
        local.start()

        def copy(k, px, py):
            return pltpu.make_async_remote_copy(
                src_ref=p_ref.at[2 * px + py], dst_ref=out_ref.at[my_q],
                send_sem=send_sems.at[k], recv_sem=recv_sems.at[k], device_id=(px, py, c), device_id_type=MESH_ID)

        def landing(k, px, py):
            return pltpu.make_async_remote_copy(
                src_ref=p_ref.at[my_q], dst_ref=out_ref.at[2 * px + py],
                send_sem=send_sems.at[k], recv_sem=recv_sems.at[k], device_id=(px, py, c), device_id_type=MESH_ID)

        sends = [copy(k, px, py) for k, (px, py) in enumerate(chips)]
        for cp in sends:
            cp.start()
        for k, (px, py) in enumerate(chips):
            landing(k, px, py).wait_recv()
        for cp in sends:
            cp.wait_send()
        local.wait()

    return pl.pallas_call(
        body, name=name, out_shape=jax.ShapeDtypeStruct(parts.shape, parts.dtype),
        in_specs=[pl.BlockSpec(memory_space=pl.ANY)], out_specs=pl.BlockSpec(memory_space=pl.ANY),
        scratch_shapes=[pltpu.SemaphoreType.DMA((3,)), pltpu.SemaphoreType.DMA((3,)), pltpu.SemaphoreType.DMA(())],
    )(parts)


RELATIONS = [(0, 0, 1), (1, 0, 0), (0, 1, 0), (1, 1, 0), (1, 0, 1), (0, 1, 1), (1, 1, 1)]
EXCHANGE_SCRATCH = [pltpu.SemaphoreType.DMA((len(RELATIONS),)), pltpu.SemaphoreType.DMA((len(RELATIONS),)),
                    pltpu.SemaphoreType.DMA(())]


def _exchange_copies(src_ref, land_ref, send_sems, recv_sems, local_sem, per_peer, incoming=True):
    x, y, c = _my_place()
    me = 4 * x + 2 * y + c

    def src(j):
        return src_ref.at[j] if per_peer else src_ref

    local = pltpu.make_async_copy(src(me), land_ref.at[me], local_sem)
    pairs = []
    for k, (dx, dy, dc) in enumerate(RELATIONS):
        peer = (1 - x if dx else x, 1 - y if dy else y, 1 - c if dc else c)
        j = 4 * peer[0] + 2 * peer[1] + peer[2]
        sems = dict(send_sem=send_sems.at[k], recv_sem=recv_sems.at[k], device_id=peer, device_id_type=MESH_ID)
        pairs.append((pltpu.make_async_remote_copy(src_ref=src(j), dst_ref=land_ref.at[me], **sems),
                      pltpu.make_async_remote_copy(src_ref=src(me), dst_ref=land_ref.at[j], **sems) if incoming else None))
    return local, pairs


def _exchange_start(*refs, per_peer):
    local, pairs = _exchange_copies(*refs, per_peer, incoming=False)
    local.start()
    for outgoing, _ in pairs:
        outgoing.start()


def _exchange_finish(*refs, per_peer):
    local, pairs = _exchange_copies(*refs, per_peer)
    for _, incoming in pairs:
        incoming.wait_recv()
    for outgoing, _ in pairs:
        outgoing.wait_send()
    local.wait()


TOKEN_ROWS = 1024


def _matmul(a, b, mode, out_dtype, name, add=None, tm=512, tn=512, tk=512):
    if mode == "nn":
        (m, k), (k2, n) = a.shape, b.shape
    elif mode == "nt":
        (m, k), (n, k2) = a.shape, b.shape
    else:
        (k, m), (k2, n) = a.shape, b.shape
    assert k == k2, (a.shape, b.shape, mode)
    tm, tn, tk = min(tm, m), min(tn, n), min(tk, k)
    assert m % tm == 0 and n % tn == 0 and k % tk == 0, (m, n, k, tm, tn, tk)
    nk = k // tk
    mm = {"nn": _mm, "nt": _mm_nt, "tn": _mm_tn}[mode]

    def body(*refs):
        if add is None:
            a_ref, b_ref, o_ref, acc_ref = refs
        else:
            a_ref, b_ref, add_ref, o_ref, acc_ref = refs
        kk = pl.program_id(2)

        @pl.when(kk == 0)
        def _():
            acc_ref[...] = jnp.zeros_like(acc_ref)

        acc_ref[...] += mm(a_ref[...].astype(MXU_DTYPE), b_ref[...].astype(MXU_DTYPE))

        @pl.when(kk == nk - 1)
        def _():
            res = acc_ref[...]
            if add is not None:
                res = res + add_ref[...]
            o_ref[...] = res.astype(o_ref.dtype)

    a_spec = {"nn": pl.BlockSpec((tm, tk), lambda i, j, kk: (i, kk)),
              "nt": pl.BlockSpec((tm, tk), lambda i, j, kk: (i, kk)),
              "tn": pl.BlockSpec((tk, tm), lambda i, j, kk: (kk, i))}[mode]
    b_spec = {"nn": pl.BlockSpec((tk, tn), lambda i, j, kk: (kk, j)),
              "nt": pl.BlockSpec((tn, tk), lambda i, j, kk: (j, kk)),
              "tn": pl.BlockSpec((tk, tn), lambda i, j, kk: (kk, j))}[mode]
    o_spec = pl.BlockSpec((tm, tn), lambda i, j, kk: (i, j))
    in_specs, args = [a_spec, b_spec], [a, b]
    if add is not None:
        in_specs.append(o_spec)
        args.append(add)
    return pl.pallas_call(
        body, name=name, grid=(m // tm, n // tn, nk), in_specs=in_specs, out_specs=o_spec,
        out_shape=jax.ShapeDtypeStruct((m, n), out_dtype),
        scratch_shapes=[pltpu.VMEM((tm, tn), F32)],
        compiler_params=_cp("parallel", "parallel", "arbitrary"),
    )(*args)


def _rmsnorm(x, g, name):
    t, d = x.shape
    tm = min(512, t)

    def body(x_ref, g_ref, o_ref):
        xv = x_ref[...]
        r = lax.rsqrt(jnp.mean(xv * xv, axis=-1, keepdims=True) + NORM_EPS)
        o_ref[...] = (xv * r * g_ref[...]).astype(o_ref.dtype)

    return pl.pallas_call(
        body, name=name, grid=(t // tm,),
        in_specs=[pl.BlockSpec((tm, d), lambda i: (i, 0)), pl.BlockSpec((1, d), lambda i: (0, 0))],
        out_specs=pl.BlockSpec((tm, d), lambda i: (i, 0)),
        out_shape=jax.ShapeDtypeStruct((t, d), MXU_DTYPE), compiler_params=_cp("parallel"),
    )(x, g.reshape(1, d))


def _rmsnorm_bwd(x, g, dh, dres, name):
    t, d = x.shape
    tm = min(512, t)

    def body(x_ref, g_ref, dh_ref, dres_ref, dx_ref, dg_ref):
        @pl.when(pl.program_id(0) == 0)
        def _():
            dg_ref[...] = jnp.zeros_like(dg_ref)

        xv = x_ref[...]
        r = lax.rsqrt(jnp.mean(xv * xv, axis=-1, keepdims=True) + NORM_EPS)
        xn = xv * r
        dhv = dh_ref[...]
        u = dhv * g_ref[...]
        dx_ref[...] = dres_ref[...] + r * (u - xn * jnp.mean(u * xn, axis=-1, keepdims=True))
        dg_ref[...] += jnp.sum(dhv * xn, axis=0, keepdims=True)

    row = pl.BlockSpec((tm, d), lambda i: (i, 0))
    vec = pl.BlockSpec((1, d), lambda i: (0, 0))
    return pl.pallas_call(
        body, name=name, grid=(t // tm,), in_specs=[row, vec, row, row], out_specs=[row, vec],
        out_shape=[jax.ShapeDtypeStruct((t, d), F32), jax.ShapeDtypeStruct((1, d), F32)],
        compiler_params=_cp("arbitrary"),
    )(x, g.reshape(1, d), dh, dres)


def _final_norm_loss(x, g, target, name):
    t, d = x.shape
    tm = min(512, t)

    def body(x_ref, g_ref, t_ref, dx_ref, dg_ref, loss_ref):
        @pl.when(pl.program_id(0) == 0)
        def _():
            dg_ref[...] = jnp.zeros_like(dg_ref)
            loss_ref[...] = jnp.zeros_like(loss_ref)

        xv = x_ref[...]
        gv = g_ref[...]
        r = lax.rsqrt(jnp.mean(xv * xv, axis=-1, keepdims=True) + NORM_EPS)
        xn = xv * r
        err = xn * gv - t_ref[...]
        per_tok = jnp.mean(err * err, axis=-1, keepdims=True)
        loss_ref[...] += jnp.broadcast_to(0.5 * jnp.sum(per_tok, axis=0, keepdims=True), loss_ref.shape)
        dy = err * (1.0 / d)
        u = dy * gv
        dx_ref[...] = r * (u - xn * jnp.mean(u * xn, axis=-1, keepdims=True))
        dg_ref[...] += jnp.sum(dy * xn, axis=0, keepdims=True)

    row = pl.BlockSpec((tm, d), lambda i: (i, 0))
    vec = pl.BlockSpec((1, d), lambda i: (0, 0))
    return pl.pallas_call(
        body, name=name, grid=(t // tm,), in_specs=[row, vec, row],
        out_specs=[row, vec, pl.BlockSpec((1, LANES), lambda i: (0, 0))],
        out_shape=[jax.ShapeDtypeStruct((t, d), F32), jax.ShapeDtypeStruct((1, d), F32),
                   jax.ShapeDtypeStruct((1, LANES), F32)],
        compiler_params=_cp("arbitrary"),
    )(x, g.reshape(1, d), target)


CONV_COLS = 128


def _shifts_down(p, width):
    row = _iota(p.shape, 0)
    return [jnp.where(row >= s, pltpu.roll(p, s, axis=0), 0.0) for s in range(1, width)]


def _shifts_up(p, width):
    n = p.shape[0]
    row = _iota(p.shape, 0)
    return [jnp.where(row < n - s, pltpu.roll(p, n - s, axis=0), 0.0) for s in range(1, width)]


def _conv_pre(p, shifted, w_ref, b_ref):
    width = w_ref.shape[0]
    u = b_ref[...] + w_ref[width - 1:width, :] * p
    for s in range(1, width):
        u = u + w_ref[width - 1 - s:width - s, :] * shifted[s - 1]
    return u


def _conv_transpose(du, w_ref):
    width = w_ref.shape[0]
    shifted = _shifts_up(du, width)
    dp = w_ref[width - 1:width, :] * du
    for s in range(1, width):
        dp = dp + w_ref[width - 1 - s:width - s, :] * shifted[s - 1]
    return dp


def _conv_wgrad(du, p, shifted, dw_ref, db_ref):
    width = dw_ref.shape[0]
    db_ref[...] = jnp.sum(du, axis=0, keepdims=True)
    dw_ref[width - 1:width, :] = jnp.sum(du * p, axis=0, keepdims=True)
    for s in range(1, width):
        dw_ref[width - 1 - s:width - s, :] = jnp.sum(du * shifted[s - 1], axis=0, keepdims=True)


def _ssd_conv_fwd(proj, w, b, name):
    t = proj.shape[0]
    cb = CONV_COLS
    off = SSD_D_INNER // cb

    def body(p_ref, w_ref, b_ref, o_ref):
        p = p_ref[...]
        u = _conv_pre(p, _shifts_down(p, SSD_CONV), w_ref, b_ref)
        o_ref[...] = u * _sigmoid(u)

    return pl.pallas_call(
        body, name=name, grid=(SSD_CONV_DIM // cb,),
        in_specs=[pl.BlockSpec((t, cb), lambda j: (0, j + off)), pl.BlockSpec((SSD_CONV, cb), lambda j: (0, j)),
                  pl.BlockSpec((1, cb), lambda j: (0, j))],
        out_specs=pl.BlockSpec((t, cb), lambda j: (0, j)),
        out_shape=jax.ShapeDtypeStruct((t, SSD_CONV_DIM), F32), compiler_params=_cp("parallel"),
    )(proj, w, b.reshape(1, -1))


def _ssd_conv_bwd(proj, w, b, dact, name):
    t = proj.shape[0]
    cb = CONV_COLS
    off = SSD_D_INNER // cb

    def body(p_ref, w_ref, b_ref, da_ref, dp_ref, dw_ref, db_ref):
        p = p_ref[...]
        shifted = _shifts_down(p, SSD_CONV)
        u = _conv_pre(p, shifted, w_ref, b_ref)
        sg = _sigmoid(u)
        du = da_ref[...] * (sg * (1.0 + u * (1.0 - sg)))
        dp_ref[...] = _conv_transpose(du, w_ref).astype(dp_ref.dtype)
        _conv_wgrad(du, p, shifted, dw_ref, db_ref)

    col = pl.BlockSpec((t, cb), lambda j: (0, j))
    wspec = pl.BlockSpec((SSD_CONV, cb), lambda j: (0, j))
    bspec = pl.BlockSpec((1, cb), lambda j: (0, j))
    return pl.pallas_call(
        body, name=name, grid=(SSD_CONV_DIM // cb,),
        in_specs=[pl.BlockSpec((t, cb), lambda j: (0, j + off)), wspec, bspec, col],
        out_specs=[col, wspec, bspec],
        out_shape=[jax.ShapeDtypeStruct((t, SSD_CONV_DIM), MXU_DTYPE), jax.ShapeDtypeStruct((SSD_CONV, SSD_CONV_DIM), F32),
                   jax.ShapeDtypeStruct((1, SSD_CONV_DIM), F32)],
        compiler_params=_cp("parallel"),
    )(proj, w, b.reshape(1, -1), dact)


def _ffn_conv_fwd(proj, w, b, name):
    t = proj.shape[0]
    cb = CONV_COLS
    nb = FFN_D_FF // cb

    def body(pg_ref, pu_ref, wg_ref, wu_ref, bg_ref, bu_ref, o_ref):
        pg, pu = pg_ref[...], pu_ref[...]
        ug = _conv_pre(pg, _shifts_down(pg, FFN_CONV), wg_ref, bg_ref)
        uu = _conv_pre(pu, _shifts_down(pu, FFN_CONV), wu_ref, bu_ref)
        o_ref[...] = (ug * _sigmoid(ug) * uu).astype(o_ref.dtype)

    gcol = pl.BlockSpec((t, cb), lambda j: (0, j))
    ucol = pl.BlockSpec((t, cb), lambda j: (0, j + nb))
    b2 = b.reshape(1, -1)
    return pl.pallas_call(
        body, name=name, grid=(nb,),
        in_specs=[gcol, ucol, pl.BlockSpec((FFN_CONV, cb), lambda j: (0, j)), pl.BlockSpec((FFN_CONV, cb), lambda j: (0, j + nb)),
                  pl.BlockSpec((1, cb), lambda j: (0, j)), pl.BlockSpec((1, cb), lambda j: (0, j + nb))],
        out_specs=gcol, out_shape=jax.ShapeDtypeStruct((t, FFN_D_FF), MXU_DTYPE), compiler_params=_cp("parallel"),
    )(proj, proj, w, w, b2, b2)


def _ffn_conv_bwd(proj, w, b, dact, name):
    t = proj.shape[0]
    cb = CONV_COLS
    nb = FFN_D_FF // cb

    def body(pg_ref, pu_ref, wg_ref, wu_ref, bg_ref, bu_ref, da_ref,
             dpg_ref, dpu_ref, dwg_ref, dwu_ref, dbg_ref, dbu_ref):
        pg, pu = pg_ref[...], pu_ref[...]
        pg_shifted, pu_shifted = _shifts_down(pg, FFN_CONV), _shifts_down(pu, FFN_CONV)
        ug = _conv_pre(pg, pg_shifted, wg_ref, bg_ref)
        uu = _conv_pre(pu, pu_shifted, wu_ref, bu_ref)
        sg = _sigmoid(ug)
        da = da_ref[...]
        dug = da * uu * (sg * (1.0 + ug * (1.0 - sg)))
        duu = da * (ug * sg)
        dpg_ref[...] = _conv_transpose(dug, wg_ref).astype(dpg_ref.dtype)
        dpu_ref[...] = _conv_transpose(duu, wu_ref).astype(dpu_ref.dtype)
        _conv_wgrad(dug, pg, pg_shifted, dwg_ref, dbg_ref)
        _conv_wgrad(duu, pu, pu_shifted, dwu_ref, dbu_ref)

    gcol = pl.BlockSpec((t, cb), lambda j: (0, j))
    ucol = pl.BlockSpec((t, cb), lambda j: (0, j + nb))
    wg = pl.BlockSpec((FFN_CONV, cb), lambda j: (0, j))
    wu = pl.BlockSpec((FFN_CONV, cb), lambda j: (0, j + nb))
    bg = pl.BlockSpec((1, cb), lambda j: (0, j))
    bu = pl.BlockSpec((1, cb), lambda j: (0, j + nb))
    b2 = b.reshape(1, -1)
    half = jax.ShapeDtypeStruct((t, FFN_D_FF), MXU_DTYPE)
    return pl.pallas_call(
        body, name=name, grid=(nb,),
        in_specs=[gcol, ucol, wg, wu, bg, bu, gcol],
        out_specs=[gcol, gcol, wg, wg, bg, bg],
        out_shape=[half, half, jax.ShapeDtypeStruct((FFN_CONV, FFN_D_FF), F32), jax.ShapeDtypeStruct((FFN_CONV, FFN_D_FF), F32),
                   jax.ShapeDtypeStruct((1, FFN_D_FF), F32), jax.ShapeDtypeStruct((1, FFN_D_FF), F32)],
        compiler_params=_cp("parallel"),
    )(proj, proj, w, w, b2, b2, dact)


SSD_ROWS = 128
DT_COL = (SSD_D_INNER + SSD_CONV_DIM) // LANES


def _head_expand():
    h = np.arange(LANES)[:, None]
    col = np.arange(SSD_D_INNER)[None, :]
    return jnp.asarray((col // SSD_HEAD_DIM == h), MXU_DTYPE)


def _chunk_tri(n, lower):
    t = _iota((n, n), 0)
    s = _iota((n, n), 1)
    shift = SSD_CHUNK.bit_length() - 1
    same = jnp.right_shift(t, shift) == jnp.right_shift(s, shift)
    tri = (s <= t) if lower else (s >= t)
    return jnp.where(same & tri, 1.0, 0.0).astype(MXU_DTYPE)


def _softplus(x):
    return jnp.maximum(x, 0.0) + jnp.log(1.0 + jnp.exp(-jnp.abs(x)))


def _ssd_dt_fwd(proj, act, dt_bias, a_neg, expand, name):
    t = proj.shape[0]
    tm = min(SSD_ROWS, t)

    def body(raw_ref, xs_ref, bias_ref, a_ref, e_ref, xdt_ref, dt_ref, acum_ref):
        lane = _iota((tm, LANES), 1)
        dt = jnp.where(lane < SSD_HEADS, _softplus(raw_ref[...] + bias_ref[...]), 0.0)
        dt_ref[...] = dt
        xdt_ref[...] = xs_ref[...] * _mm_exact_rhs(dt, e_ref[...])
        acum_ref[...] = _mm_exact_lhs(_chunk_tri(tm, True), a_ref[...] * dt)

    vec = pl.BlockSpec((1, LANES), lambda i: (0, 0))
    return pl.pallas_call(
        body, name=name, grid=(t // tm,),
        in_specs=[pl.BlockSpec((tm, LANES), lambda i: (i, DT_COL)), pl.BlockSpec((tm, SSD_D_INNER), lambda i: (i, 0)),
                  vec, vec, pl.BlockSpec((LANES, SSD_D_INNER), lambda i: (0, 0))],
        out_specs=[pl.BlockSpec((tm, SSD_D_INNER), lambda i: (i, 0)), pl.BlockSpec((tm, LANES), lambda i: (i, 0)),
                   pl.BlockSpec((tm, LANES), lambda i: (i, 0))],
        out_shape=[jax.ShapeDtypeStruct((t, SSD_D_INNER), F32), jax.ShapeDtypeStruct((t, LANES), F32),
                   jax.ShapeDtypeStruct((t, LANES), F32)],
        compiler_params=_cp("parallel"),
    )(proj, act, dt_bias, a_neg, expand)


def _ssd_dt_bwd(proj, act, dt, dxdt, dyy, dacum, dt_bias, a_neg, d_exp, expand, expand_t, name):
    t = proj.shape[0]
    tm = min(SSD_ROWS, t)

    def body(raw_ref, xs_ref, dt_ref, dxdt_ref, dyy_ref, dac_ref, bias_ref, a_ref, dsk_ref, e_ref, et_ref,
             dxs_ref, draw_ref, da_ref, dbias_ref, dd_ref):
        @pl.when(pl.program_id(0) == 0)
        def _():
            da_ref[...] = jnp.zeros_like(da_ref)
            dbias_ref[...] = jnp.zeros_like(dbias_ref)
            dd_ref[...] = jnp.zeros_like(dd_ref)

        lane = _iota((tm, LANES), 1)
        xs, dt, dxdt, dyy = xs_ref[...], dt_ref[...], dxdt_ref[...], dyy_ref[...]
        dxs_ref[...] = dxdt * _mm_exact_rhs(dt, e_ref[...]) + dsk_ref[...] * dyy
        dd_ref[...] += jnp.sum(dyy * xs, axis=0, keepdims=True)
        ddt = _mm_exact_rhs(dxdt * xs, et_ref[...])
        da = _mm_exact_lhs(_chunk_tri(tm, False), dac_ref[...])
        ddt = ddt + da * a_ref[...]
        da_ref[...] += jnp.sum(da * dt, axis=0, keepdims=True)
        draw = jnp.where(lane < SSD_HEADS, ddt * _sigmoid(raw_ref[...] + bias_ref[...]), 0.0)
        dbias_ref[...] += jnp.sum(draw, axis=0, keepdims=True)
        draw_ref[...] = draw.astype(draw_ref.dtype)

    wide = pl.BlockSpec((tm, SSD_D_INNER), lambda i: (i, 0))
    thin = pl.BlockSpec((tm, LANES), lambda i: (i, 0))
    vec = pl.BlockSpec((1, LANES), lambda i: (0, 0))
    wvec = pl.BlockSpec((1, SSD_D_INNER), lambda i: (0, 0))
    return pl.pallas_call(
        body, name=name, grid=(t // tm,),
        in_specs=[pl.BlockSpec((tm, LANES), lambda i: (i, DT_COL)), wide, thin, wide, wide, thin, vec, vec, wvec,
                  pl.BlockSpec((LANES, SSD_D_INNER), lambda i: (0, 0)), pl.BlockSpec((SSD_D_INNER, LANES), lambda i: (0, 0))],
        out_specs=[wide, thin, vec, vec, wvec],
        out_shape=[jax.ShapeDtypeStruct((t, SSD_D_INNER), F32), jax.ShapeDtypeStruct((t, LANES), MXU_DTYPE),
                   jax.ShapeDtypeStruct((1, LANES), F32), jax.ShapeDtypeStruct((1, LANES), F32),
                   jax.ShapeDtypeStruct((1, SSD_D_INNER), F32)],
        compiler_params=_cp("arbitrary"),
    )(proj, act, dt, dxdt, dyy, dacum, dt_bias, a_neg, d_exp, expand, expand_t)


SSD_PAIR = 2 * SSD_HEAD_DIM
HEADS_PER_GROUP = SSD_HEADS // SSD_GROUPS
GROUP_COLS = HEADS_PER_GROUP * SSD_HEAD_DIM
B_COL0 = SSD_D_INNER // SSD_STATE
C_COL0 = (SSD_D_INNER + SSD_GROUPS * SSD_STATE) // SSD_STATE


def _pair_cols(vals, h0, lo_mask):
    return jnp.where(lo_mask, vals[:, h0:h0 + 1], vals[:, h0 + 1:h0 + 2])


SCAN_GROUPS = 2
SCAN_STEPS = SSD_GROUPS // SCAN_GROUPS


def _scan_step_is(g, c):
    return (pl.program_id(0) == g) & (pl.program_id(1) == c)


def _ssd_scan_fwd(xdt, act, acum_g, acum_gt, name, carried=None):
    t = xdt.shape[0]
    nc = t // SSD_CHUNK
    ln = SSD_CHUNK
    c_in_specs, c_in, c_out_specs, c_out = _carried_specs(carried)

    def body(*refs):
        if carried is None:
            x_ref, b_ref, c_ref, ac_ref, act_ref, y_ref, sst_ref, state = refs
            comm_refs = None
        else:
            x_ref, b_ref, c_ref, ac_ref, act_ref, src_ref, y_ref, sst_ref, land_ref, state, send_sems, recv_sems, local_sem = refs
            comm_refs = (src_ref, land_ref, send_sems, recv_sems, local_sem)
        finish = _carried_hooks(carried, comm_refs, _scan_step_is(0, 0), _scan_step_is(SCAN_STEPS - 1, nc - 1))

        @pl.when(pl.program_id(1) == 0)
        def _():
            state[...] = jnp.zeros_like(state)

        causal = _iota((ln, ln), 1) <= _iota((ln, ln), 0)
        lo_mask = _iota((ln, SSD_PAIR), 1) < SSD_HEAD_DIM
        lo_rows = _iota((SSD_PAIR, SSD_STATE), 0) < SSD_HEAD_DIM
        for gg in range(SCAN_GROUPS):
            sst_ref[0, gg] = state[gg * GROUP_COLS:(gg + 1) * GROUP_COLS, :]
            bm = b_ref[:, gg * SSD_STATE:(gg + 1) * SSD_STATE].astype(MXU_DTYPE)
            cm = c_ref[:, gg * SSD_STATE:(gg + 1) * SSD_STATE].astype(MXU_DTYPE)
            cb = _mm_nt(cm, bm)
            ac, act_ = ac_ref[gg], act_ref[gg]
            last = ac[ln - 1:ln, :]
            e_ac = jnp.exp(ac)
            w_all = jnp.exp(last - ac)
            e_last = jnp.exp(last)
            for pr in range(2):
                first = gg * GROUP_COLS + pr * SSD_PAIR
                cols = slice(first, first + SSD_PAIR)
                xp = x_ref[:, cols]
                sp = state[cols, :]
                ydiag = jnp.zeros((ln, SSD_PAIR), F32)
                for hh in range(2):
                    h = 2 * pr + hh
                    seg = ac[:, h:h + 1] - act_[h:h + 1, :]
                    dec = jnp.exp(jnp.where(causal, seg, -1e30))
                    mask = lo_mask if hh == 0 else jnp.logical_not(lo_mask)
                    ydiag = ydiag + _mm((cb * dec).astype(MXU_DTYPE), jnp.where(mask, xp, 0.0).astype(MXU_DTYPE))
                yoff = _mm_nt(cm, sp.astype(MXU_DTYPE)) * _pair_cols(e_ac, 2 * pr, lo_mask)
                y_ref[:, cols] = ydiag + yoff
                xw = (xp * _pair_cols(w_all, 2 * pr, lo_mask)).astype(MXU_DTYPE)
                el = jnp.where(lo_rows, e_last[:, 2 * pr:2 * pr + 1], e_last[:, 2 * pr + 1:2 * pr + 2])
                state[cols, :] = sp * el + _mm_tn(xw, bm)
        finish()

    sg = SCAN_GROUPS
    return pl.pallas_call(
        body, name=name, grid=(SCAN_STEPS, nc),
        in_specs=[pl.BlockSpec((ln, sg * GROUP_COLS), lambda g, c: (c, g)),
                  pl.BlockSpec((ln, sg * SSD_STATE), lambda g, c: (c, B_COL0 // sg + g)),
                  pl.BlockSpec((ln, sg * SSD_STATE), lambda g, c: (c, C_COL0 // sg + g)),
                  pl.BlockSpec((sg, ln, HEADS_PER_GROUP), lambda g, c: (g, c, 0)),
                  pl.BlockSpec((sg, HEADS_PER_GROUP, ln), lambda g, c: (g, 0, c))] + c_in_specs,
        out_specs=[pl.BlockSpec((ln, sg * GROUP_COLS), lambda g, c: (c, g)),
                   pl.BlockSpec((1, sg, GROUP_COLS, SSD_STATE), lambda g, c: (c, g, 0, 0))] + c_out_specs,
        out_shape=[jax.ShapeDtypeStruct((t, SSD_D_INNER), F32),
                   jax.ShapeDtypeStruct((nc, SSD_GROUPS, GROUP_COLS, SSD_STATE), F32)] + c_out,
        scratch_shapes=[pltpu.VMEM((sg * GROUP_COLS, SSD_STATE), F32)] + (EXCHANGE_SCRATCH if carried else []),
        compiler_params=_cp("arbitrary", "arbitrary"),
    )(xdt, act, act, acum_g, acum_gt, *c_in)


def _ssd_scan_bwd(xdt, act, acum_g, acum_gt, states, dy, name, carried=None):
    t = xdt.shape[0]
    nc = t // SSD_CHUNK
    ln = SSD_CHUNK
    c_in_specs, c_in, c_out_specs, c_out = _carried_specs(carried)

    def body(*refs):
        if carried is None:
            x_ref, b_ref, c_ref, ac_ref, act_ref, sst_ref, dy_ref, dx_ref, db_ref, dc_ref, dacol_ref, darow_ref, dstate = refs
            comm_refs = None
        else:
            (x_ref, b_ref, c_ref, ac_ref, act_ref, sst_ref, dy_ref, src_ref, dx_ref, db_ref, dc_ref, dacol_ref, darow_ref,
             land_ref, dstate, send_sems, recv_sems, local_sem) = refs
            comm_refs = (src_ref, land_ref, send_sems, recv_sems, local_sem)
        finish = _carried_hooks(carried, comm_refs, _scan_step_is(0, 0), _scan_step_is(SCAN_STEPS - 1, nc - 1))

        @pl.when(pl.program_id(1) == 0)
        def _():
            dstate[...] = jnp.zeros_like(dstate)

        for gg in range(SCAN_GROUPS):
            group_bwd(gg, x_ref, b_ref, c_ref, ac_ref, act_ref, sst_ref, dy_ref, dx_ref, db_ref, dc_ref, dacol_ref, darow_ref, dstate)
        finish()

    def group_bwd(gg, x_ref, b_ref, c_ref, ac_ref, act_ref, sst_ref, dy_ref, dx_ref, db_ref, dc_ref, dacol_ref, darow_ref, dstate):
        bc_cols = slice(gg * SSD_STATE, (gg + 1) * SSD_STATE)
        bm = b_ref[:, bc_cols].astype(MXU_DTYPE)
        cm = c_ref[:, bc_cols].astype(MXU_DTYPE)
        cb = _mm_nt(cm, bm)
        ac, act_ = ac_ref[gg], act_ref[gg]
        causal = _iota((ln, ln), 1) <= _iota((ln, ln), 0)
        lo_mask = _iota((ln, SSD_PAIR), 1) < SSD_HEAD_DIM
        lo_rows = _iota((SSD_PAIR, SSD_STATE), 0) < SSD_HEAD_DIM
        lane4 = _iota((ln, HEADS_PER_GROUP), 1)
        sub4 = _iota((HEADS_PER_GROUP, ln), 0)
        is_last = _iota((ln, 1), 0) == ln - 1
        last = ac[ln - 1:ln, :]
        e_ac = jnp.exp(ac)
        w_all = jnp.exp(last - ac)
        e_last = jnp.exp(last)
        dcb = jnp.zeros((ln, ln), F32)
        dc_acc = jnp.zeros((ln, SSD_STATE), F32)
        db_acc = jnp.zeros((ln, SSD_STATE), F32)
        dacol = jnp.zeros((ln, HEADS_PER_GROUP), F32)
        darow = jnp.zeros((HEADS_PER_GROUP, ln), F32)
        for pr in range(2):
            in_group = slice(pr * SSD_PAIR, (pr + 1) * SSD_PAIR)
            cols = slice(gg * GROUP_COLS + pr * SSD_PAIR, gg * GROUP_COLS + (pr + 1) * SSD_PAIR)
            xp = x_ref[:, cols]
            dyp = dy_ref[:, cols]
            sp = sst_ref[0, gg, in_group, :]
            dsp = dstate[cols, :]
            ea = _pair_cols(e_ac, 2 * pr, lo_mask)
            w = _pair_cols(w_all, 2 * pr, lo_mask)
            dye = (dyp * ea).astype(MXU_DTYPE)
            dx_state = w * _mm_nt(bm, dsp.astype(MXU_DTYPE))
            yoff = _mm_nt(cm, sp.astype(MXU_DTYPE)) * ea
            dxp = dx_state
            for hh in range(2):
                h = 2 * pr + hh
                mask = lo_mask if hh == 0 else jnp.logical_not(lo_mask)
                rmask = lo_rows if hh == 0 else jnp.logical_not(lo_rows)
                seg = ac[:, h:h + 1] - act_[h:h + 1, :]
                dec = jnp.exp(jnp.where(causal, seg, -1e30))
                m = cb * dec
                dym = jnp.where(mask, dyp, 0.0).astype(MXU_DTYPE)
                xm = jnp.where(mask, xp, 0.0).astype(MXU_DTYPE)
                g = _mm_nt(dym, xm)
                dxp = dxp + _mm_tn(m.astype(MXU_DTYPE), dym)
                dcb = dcb + dec * g
                mg = m * g
                rs = jnp.sum(mg, axis=1, keepdims=True)
                cs = jnp.sum(mg, axis=0, keepdims=True)
                t_off = jnp.sum(jnp.where(mask, dyp * yoff, 0.0), axis=1, keepdims=True)
                q = jnp.sum(jnp.where(mask, xp * dx_state, 0.0), axis=1, keepdims=True)
                qsum = jnp.sum(q, axis=0, keepdims=True)
                ds_s = jnp.sum(jnp.sum(jnp.where(rmask, dsp * sp, 0.0), axis=1, keepdims=True), axis=0, keepdims=True)
                extra = qsum + e_last[:, h:h + 1] * ds_s
                col = rs + t_off - q + jnp.where(is_last, extra, 0.0)
                dacol = jnp.where(lane4 == h, col, dacol)
                darow = jnp.where(sub4 == h, -cs, darow)
            dx_ref[:, cols] = dxp
            dc_acc = dc_acc + _mm(dye, sp.astype(MXU_DTYPE))
            db_acc = db_acc + _mm((xp * w).astype(MXU_DTYPE), dsp.astype(MXU_DTYPE))
            el = jnp.where(lo_rows, e_last[:, 2 * pr:2 * pr + 1], e_last[:, 2 * pr + 1:2 * pr + 2])
            dstate[cols, :] = dsp * el + _mm_tn(dye, cm)
        dcbm = dcb.astype(MXU_DTYPE)
        dc_ref[:, bc_cols] = _mm(dcbm, bm) + dc_acc
        db_ref[:, bc_cols] = _mm_tn(dcbm, cm) + db_acc
        dacol_ref[gg] = dacol
        darow_ref[gg] = darow

    def rev(c):
        return nc - 1 - c

    sg = SCAN_GROUPS
    grp = pl.BlockSpec((ln, sg * GROUP_COLS), lambda g, c: (rev(c), g))
    return pl.pallas_call(
        body, name=name, grid=(SCAN_STEPS, nc),
        in_specs=[grp,
                  pl.BlockSpec((ln, sg * SSD_STATE), lambda g, c: (rev(c), B_COL0 // sg + g)),
                  pl.BlockSpec((ln, sg * SSD_STATE), lambda g, c: (rev(c), C_COL0 // sg + g)),
                  pl.BlockSpec((sg, ln, HEADS_PER_GROUP), lambda g, c: (g, rev(c), 0)),
                  pl.BlockSpec((sg, HEADS_PER_GROUP, ln), lambda g, c: (g, 0, rev(c))),
                  pl.BlockSpec((1, sg, GROUP_COLS, SSD_STATE), lambda g, c: (rev(c), g, 0, 0)),
                  grp] + c_in_specs,
        out_specs=[grp,
                   pl.BlockSpec((ln, sg * SSD_STATE), lambda g, c: (rev(c), g)),
                   pl.BlockSpec((ln, sg * SSD_STATE), lambda g, c: (rev(c), g)),
                   pl.BlockSpec((sg, ln, HEADS_PER_GROUP), lambda g, c: (g, rev(c), 0)),
                   pl.BlockSpec((sg, HEADS_PER_GROUP, ln), lambda g, c: (g, 0, rev(c)))] + c_out_specs,
        out_shape=[jax.ShapeDtypeStruct((t, SSD_D_INNER), F32),
                   jax.ShapeDtypeStruct((t, SSD_GROUPS * SSD_STATE), F32),
                   jax.ShapeDtypeStruct((t, SSD_GROUPS * SSD_STATE), F32),
                   jax.ShapeDtypeStruct((SSD_GROUPS, t, HEADS_PER_GROUP), F32),
                   jax.ShapeDtypeStruct((SSD_GROUPS, HEADS_PER_GROUP, t), F32)] + c_out,
        scratch_shapes=[pltpu.VMEM((sg * GROUP_COLS, SSD_STATE), F32)] + (EXCHANGE_SCRATCH if carried else []),
        compiler_params=_cp("arbitrary", "arbitrary"),
    )(xdt, act, act, acum_g, acum_gt, states, dy, *c_in)


GN_ROWS = 128


def _gated_norm_parts(y_ref, xs_ref, z_ref, dsk_ref):
    yy = y_ref[...] + dsk_ref[...] * xs_ref[...]
    z = z_ref[...]
    sz = _sigmoid(z)
    silu = z * sz
    u = yy * silu
    r = lax.rsqrt(jnp.mean(u * u, axis=-1, keepdims=True) + NORM_EPS)
    return yy, z, sz, silu, u, r


def _gated_norm_fwd(y, act, proj, d_exp, g, name):
    t = y.shape[0]
    tm = min(GN_ROWS, t)

    def body(y_ref, xs_ref, z_ref, dsk_ref, g_ref, o_ref):
        _, _, _, _, u, r = _gated_norm_parts(y_ref, xs_ref, z_ref, dsk_ref)
        o_ref[...] = (u * r * g_ref[...]).astype(o_ref.dtype)

    wide = pl.BlockSpec((tm, SSD_D_INNER), lambda i: (i, 0))
    wvec = pl.BlockSpec((1, SSD_D_INNER), lambda i: (0, 0))
    return pl.pallas_call(
        body, name=name, grid=(t // tm,), in_specs=[wide, wide, wide, wvec, wvec], out_specs=wide,
        out_shape=jax.ShapeDtypeStruct((t, SSD_D_INNER), MXU_DTYPE), compiler_params=_cp("parallel"),
    )(y, act, proj, d_exp, g.reshape(1, -1))


def _gated_norm_bwd(y, act, proj, d_exp, g, dn, name):
    t = y.shape[0]
    tm = min(GN_ROWS, t)

    def body(y_ref, xs_ref, z_ref, dsk_ref, g_ref, dn_ref, dyy_ref, dz_ref, dg_ref):
        @pl.when(pl.program_id(0) == 0)
        def _():
            dg_ref[...] = jnp.zeros_like(dg_ref)

        yy, z, sz, silu, u, r = _gated_norm_parts(y_ref, xs_ref, z_ref, dsk_ref)
        un = u * r
        dn = dn_ref[...]
        v = dn * g_ref[...]
        du = r * (v - un * jnp.mean(v * un, axis=-1, keepdims=True))
        dg_ref[...] += jnp.sum(dn * un, axis=0, keepdims=True)
        dyy_ref[...] = du * silu
        dz_ref[...] = (du * yy * (sz * (1.0 + z * (1.0 - sz)))).astype(dz_ref.dtype)

    wide = pl.BlockSpec((tm, SSD_D_INNER), lambda i: (i, 0))
    wvec = pl.BlockSpec((1, SSD_D_INNER), lambda i: (0, 0))
    return pl.pallas_call(
        body, name=name, grid=(t // tm,), in_specs=[wide, wide, wide, wvec, wvec, wide], out_specs=[wide, wide, wvec],
        out_shape=[jax.ShapeDtypeStruct((t, SSD_D_INNER), F32), jax.ShapeDtypeStruct((t, SSD_D_INNER), MXU_DTYPE),
                   jax.ShapeDtypeStruct((1, SSD_D_INNER), F32)],
        compiler_params=_cp("arbitrary"),
    )(y, act, proj, d_exp, g.reshape(1, -1), dn)


SB_PAIRS = SB_HEADS // 2


def _kv_rows(j, bt, nt=1):
    return pl.ds(pl.multiple_of(j * bt, bt), nt * bt)


def _sb_tile_masks(bt):
    lane = _iota((bt, bt), 1)
    rowi = _iota((bt, bt), 0)
    return lane < rowi, (rowi >= lane).astype(MXU_DTYPE), (rowi <= lane).astype(MXU_DTYPE)


def _sb_scaled_heads(pair, scale):
    lane = _iota(pair.shape, 1)
    val = pair.astype(F32) * scale
    return [jnp.where(lane < SB_HEAD_DIM, val, 0.0).astype(pair.dtype), jnp.where(lane >= SB_HEAD_DIM, val, 0.0).astype(pair.dtype)]


def _sb_logits(qs, kb, bt, strict):
    nt = kb.shape[0] // bt
    full = [_mm_nt(q_head, kb) for q_head in qs]
    xs, nlfs = [], []
    for x in full:
        nlf = jnp.maximum(x, 0.0) + jnp.log(1.0 + jnp.exp(-jnp.abs(x)))
        xs.append([x[:, tt * bt:(tt + 1) * bt] for tt in range(nt)])
        tiles = [nlf[:, tt * bt:(tt + 1) * bt] for tt in range(nt)]
        if strict is not None:
            tiles[-1] = jnp.where(strict, tiles[-1], 0.0)
        nlfs.append(tiles)
    return xs, nlfs


def _sb_tails(nlf_tiles, from_j):
    tails, run = [None] * len(nlf_tiles), None
    for tt in reversed(range(len(nlf_tiles))):
        tail = _mm_exact_rhs(nlf_tiles[tt], from_j)
        tails[tt] = tail if run is None else tail + run
        run = tails[tt][:, 0:1]
    return tails


def _sb_heads(e_tiles, upto_j, pre):
    sums, run = [], pre
    for e in e_tiles:
        sums.append(_mm_exact_rhs(e, upto_j) + run)
        run = sums[-1][:, e.shape[1] - 1:e.shape[1]]
    return sums


def _carried_specs(carried):
    if carried is None:
        return [], [], [], []
    src, per_peer = carried
    rows = src.shape[1:] if per_peer else src.shape
    anywhere = pl.BlockSpec(memory_space=pl.ANY)
    return [anywhere], [src], [anywhere], [jax.ShapeDtypeStruct((N_DEV, *rows), src.dtype)]


def _carried_hooks(carried, comm_refs, first, last):
    if carried is None:
        return lambda: None

    @pl.when(first)
    def _():
        _exchange_start(*comm_refs, per_peer=carried[1])

    def finish():
        @pl.when(last)
        def _():
            _exchange_finish(*comm_refs, per_peer=carried[1])

    return finish


def _sb_attention_fwd(qkv, name, carried=None):
    t = qkv.shape[0]
    bt = min(SB_TILE, t)
    nq = t // bt
    c_in_specs, c_in, c_out_specs, c_out = _carried_specs(carried)

    def body(*refs):
        if carried is None:
            q_ref, k_ref, v_ref, o_ref, acc_ref = refs
            comm_refs = None
        else:
            q_ref, k_ref, v_ref, src_ref, o_ref, land_ref, acc_ref, send_sems, recv_sems, local_sem = refs
            comm_refs = (src_ref, land_ref, send_sems, recv_sems, local_sem)
        i = pl.program_id(1)
        finish = _carried_hooks(carried, comm_refs, (pl.program_id(0) == 0) & (i == 0),
                                (pl.program_id(0) == SB_PAIRS - 1) & (i == nq - 1))
        strict, from_j, _ = _sb_tile_masks(bt)
        qs = _sb_scaled_heads(q_ref[...], SB_SCALE)
        acc_ref[...] = jnp.zeros_like(acc_ref)

        def block(j, nt, carries, diag):
            rows = _kv_rows(j, bt, nt)
            kb, vb = k_ref[rows, :], v_ref[rows, :]
            xs, nlfs = _sb_logits(qs, kb, bt, strict if diag else None)
            tails = [_sb_tails(nlfs[hh], from_j) for hh in range(2)]
            for hh in range(2):
                ws = [jnp.exp(xs[hh][tt] - tails[hh][tt] - carries[hh]) for tt in range(nt)]
                if diag:
                    ws[-1] = jnp.where(strict, ws[-1], 0.0)
                acc_ref[hh] += _mm(jnp.concatenate([w.astype(MXU_DTYPE) for w in ws], axis=1), vb)
            return tuple(carries[hh] + tails[hh][0][:, 0:1] for hh in range(2))

        zero = jnp.zeros((bt, 1), F32)
        carries = block(i, 1, (zero, zero), True)
        carries = lax.fori_loop(0, i // 2, lambda it, cr: block(i - 2 - 2 * it, 2, cr, False), carries)

        @pl.when(i % 2 == 1)
        def _():
            block(0, 1, carries, False)

        low = _iota((bt, 2 * SB_HEAD_DIM), 1) < SB_HEAD_DIM
        o_ref[...] = jnp.where(low, acc_ref[0], acc_ref[1]).astype(o_ref.dtype)
        finish()

    lanes = 2 * SB_HEAD_DIM
    res = pl.pallas_call(
        body, name=name, grid=(SB_PAIRS, nq),
        in_specs=[pl.BlockSpec((bt, lanes), lambda p, i: (i, p)),
                  pl.BlockSpec((t, lanes), lambda p, i: (0, SB_PAIRS + p)),
                  pl.BlockSpec((t, lanes), lambda p, i: (0, 2 * SB_PAIRS + p))] + c_in_specs,
        out_specs=[pl.BlockSpec((bt, lanes), lambda p, i: (i, p))] + c_out_specs,
        out_shape=[jax.ShapeDtypeStruct((t, D_MODEL), MXU_DTYPE)] + c_out,
        scratch_shapes=[pltpu.VMEM((2, bt, lanes), F32)] + (EXCHANGE_SCRATCH if carried else []),
        compiler_params=_cp("arbitrary", "arbitrary"),
    )(qkv, qkv, qkv, *c_in)
    return res[0] if carried is None else res


def _sb_attention_bwd(qkv, do, name, carried=None):
    t = qkv.shape[0]
    bt = min(SB_TILE, t)
    nq = t // bt
    lanes = 2 * SB_HEAD_DIM
    c_in_specs, c_in, c_out_specs, c_out = _carried_specs(carried)

    def body(*refs):
        if carried is None:
            q_ref, k_ref, v_ref, do_ref, dq_ref, dk_ref, dv_ref, sbuf, ebuf, dq_acc, dk_acc, dv_acc = refs
            comm_refs = None
        else:
            (q_ref, k_ref, v_ref, do_ref, src_ref, dq_ref, dk_ref, dv_ref, land_ref,
             sbuf, ebuf, dq_acc, dk_acc, dv_acc, send_sems, recv_sems, local_sem) = refs
            comm_refs = (src_ref, land_ref, send_sems, recv_sems, local_sem)
        i = pl.program_id(1)
        finish = _carried_hooks(carried, comm_refs, (pl.program_id(0) == 0) & (i == 0),
                                (pl.program_id(0) == SB_PAIRS - 1) & (i == nq - 1))

        @pl.when(i == 0)
        def _():
            dk_acc[...] = jnp.zeros_like(dk_acc)
            dv_acc[...] = jnp.zeros_like(dv_acc)

        strict, from_j, upto_j = _sb_tile_masks(bt)
        qs = _sb_scaled_heads(q_ref[...], SB_SCALE)
        dos = _sb_scaled_heads(do_ref[...], 1.0)
        q_both = jnp.concatenate(qs, axis=0)
        do_both = jnp.concatenate(dos, axis=0)
        dq_acc[...] = jnp.zeros_like(dq_acc)

        def pass1(j, nt, carries, diag):
            rows = _kv_rows(j, bt, nt)
            kb, vb = k_ref[rows, :], v_ref[rows, :]
            xs, nlfs = _sb_logits(qs, kb, bt, strict if diag else None)
            dws = [_mm_nt(dos[hh], vb) for hh in range(2)]
            tails = [_sb_tails(nlfs[hh], from_j) for hh in range(2)]
            wcat = []
            for hh in range(2):
                ws = [jnp.exp(xs[hh][tt] - tails[hh][tt] - carries[hh]) for tt in range(nt)]
                if diag:
                    ws[-1] = jnp.where(strict, ws[-1], 0.0)
                w_all = jnp.concatenate(ws, axis=1)
                sbuf[hh, :, rows] = jnp.exp(jnp.concatenate([xs[hh][tt] - nlfs[hh][tt] for tt in range(nt)], axis=1))
                ebuf[hh, :, rows] = w_all * dws[hh]
                wcat.append(w_all.astype(MXU_DTYPE))
            dv_acc[rows, :] += _mm_tn(jnp.concatenate(wcat, axis=0), do_both)
            return tuple(carries[hh] + tails[hh][0][:, 0:1] for hh in range(2))

        zero = jnp.zeros((bt, 1), F32)
        carries = pass1(i, 1, (zero, zero), True)
        carries = lax.fori_loop(0, i // 2, lambda it, cr: pass1(i - 2 - 2 * it, 2, cr, False), carries)

        @pl.when(i % 2 == 1)
        def _():
            pass1(0, 1, carries, False)

        def pass2(j, nt, pres, diag):
            rows = _kv_rows(j, bt, nt)
            kb = k_ref[rows, :]
            sums = [_sb_heads([ebuf[hh, :, _kv_rows(j + tt, bt)] for tt in range(nt)], upto_j, pres[hh]) for hh in range(2)]
            dxm = []
            for hh in range(2):
                dxs = [ebuf[hh, :, _kv_rows(j + tt, bt)] - sbuf[hh, :, _kv_rows(j + tt, bt)] * sums[hh][tt] for tt in range(nt)]
                if diag:
                    dxs[-1] = jnp.where(strict, dxs[-1], 0.0)
                dxm.append(jnp.concatenate(dxs, axis=1).astype(MXU_DTYPE))
                dq_acc[hh] += _mm(dxm[hh], kb)
            dk_acc[rows, :] += _mm_tn(jnp.concatenate(dxm, axis=0), q_both)
            return tuple(sums[hh][-1][:, bt - 1:bt] for hh in range(2))

        pres = lax.fori_loop(0, i // 2, lambda it, pr: pass2(2 * it, 2, pr, False), (zero, zero))

        @pl.when(i % 2 == 0)
        def _():
            pass2(i, 1, pres, True)

        @pl.when(i % 2 == 1)
        def _():
            pass2(i - 1, 2, pres, True)

        low = _iota((bt, lanes), 1) < SB_HEAD_DIM
        dq_ref[...] = (jnp.where(low, dq_acc[0], dq_acc[1]) * SB_SCALE).astype(dq_ref.dtype)

        @pl.when(i == nq - 1)
        def _():
            dk_ref[...] = dk_acc[...].astype(dk_ref.dtype)
            dv_ref[...] = dv_acc[...].astype(dv_ref.dtype)

        finish()

    blk = pl.BlockSpec((bt, lanes), lambda p, i: (i, p))
    whole = pl.BlockSpec((t, lanes), lambda p, i: (0, p))
    out = jax.ShapeDtypeStruct((t, D_MODEL), MXU_DTYPE)
    return pl.pallas_call(
        body, name=name, grid=(SB_PAIRS, nq),
        in_specs=[blk, pl.BlockSpec((t, lanes), lambda p, i: (0, SB_PAIRS + p)),
                  pl.BlockSpec((t, lanes), lambda p, i: (0, 2 * SB_PAIRS + p)), blk] + c_in_specs,
        out_specs=[blk, whole, whole] + c_out_specs, out_shape=[out, out, out] + c_out,
        scratch_shapes=[pltpu.VMEM((2, bt, t), F32), pltpu.VMEM((2, bt, t), F32), pltpu.VMEM((2, bt, lanes), F32),
                        pltpu.VMEM((t, lanes), F32), pltpu.VMEM((t, lanes), F32)] + (EXCHANGE_SCRATCH if carried else []),
        compiler_params=_cp("arbitrary", "arbitrary"),
    )(qkv, qkv, qkv, do, *c_in)


def _add_pair(a, b, name):
    s, r, c = a.shape
    tm = _row_tile(r)

    def body(a_ref, b_ref, o_ref):
        o_ref[...] = (a_ref[...].astype(F32) + b_ref[...].astype(F32)).astype(o_ref.dtype)

    blk = pl.BlockSpec((1, tm, c), lambda q, i: (q, i, 0))
    return pl.pallas_call(body, name=name, grid=(s, r // tm), in_specs=[blk, blk], out_specs=blk,
                          out_shape=jax.ShapeDtypeStruct(a.shape, a.dtype), compiler_params=_cp("parallel", "parallel"))(a, b)


def _sum_slots(gslots, name):
    s, r, c = gslots.shape

    def body(g_ref, o_ref):
        g = g_ref[0].astype(F32)
        for q in range(1, s):
            g = g + g_ref[q].astype(F32)
        o_ref[...] = g

    return pl.pallas_call(
        body, name=name, grid=(c // LANES,),
        in_specs=[pl.BlockSpec((s, r, LANES), lambda j: (0, 0, j))], out_specs=pl.BlockSpec((r, LANES), lambda j: (0, j)),
        out_shape=jax.ShapeDtypeStruct((r, c), F32), compiler_params=_cp("parallel"),
    )(gslots)


def _adamw(gslots, w, m, v, name):
    s, r, c = gslots.shape
    tm = _row_tile(r)
    assert w.shape == (r, c), (w.shape, gslots.shape)
    c1 = 1.0 - ADAM_B1 ** ADAM_STEP
    c2 = 1.0 - ADAM_B2 ** ADAM_STEP

    def body(g_ref, w_ref, m_ref, v_ref, go_ref, d_ref, mo_ref, vo_ref):
        g = g_ref[0].astype(F32)
        for q in range(1, s):
            g = g + g_ref[q].astype(F32)
        mn = ADAM_B1 * m_ref[...] + (1.0 - ADAM_B1) * g
        vn = ADAM_B2 * v_ref[...] + (1.0 - ADAM_B2) * (g * g)
        go_ref[...] = g
        mo_ref[...] = mn
        vo_ref[...] = vn
        d_ref[...] = -ADAM_LR * ((mn / c1) / (jnp.sqrt(vn / c2) + ADAM_EPS) + ADAM_WD * w_ref[...])

    row = pl.BlockSpec((tm, c), lambda i: (i, 0))
    out = jax.ShapeDtypeStruct((r, c), F32)
    return pl.pallas_call(
        body, name=name, grid=(r // tm,),
        in_specs=[pl.BlockSpec((s, tm, c), lambda i: (0, i, 0)), row, row, row],
        out_specs=[row, row, row, row], out_shape=[out, out, out, out], compiler_params=_cp("parallel"),
    )(gslots, w, m, v)


def _rows(a):
    flat = a.reshape(-1)
    pad = (-flat.shape[0]) % PACK_W
    if pad:
        flat = jnp.concatenate([flat, jnp.zeros((pad,), flat.dtype)])
    return flat.reshape(-1, PACK_W)


def _pack(arrays, row_multiple):
    parts, layout, off = [], [], 0
    for a in arrays:
        rw = _rows(a)
        parts.append(rw)
        layout.append((off, rw.shape[0], a.shape))
        off += rw.shape[0]
    pad = (-off) % row_multiple
    if pad:
        parts.append(jnp.zeros((pad, PACK_W), parts[0].dtype))
    return jnp.concatenate(parts, axis=0), layout


def _unpack(packed, layout):
    out = []
    for off, nrows, shape in layout:
        n = int(np.prod(shape))
        out.append(packed[off:off + nrows].reshape(-1)[:n].reshape(shape))
    return out


def _shard_as_rows(name, shard):
    if name in COL_SHARDED:
        shard = shard.transpose(0, 2, 1)
    return shard.reshape(-1, PACK_W)


def _rows_as_shard(name, rows, shape):
    if name in COL_SHARDED:
        lead, k, ns = shape
        return rows.reshape(lead, ns, k).transpose(0, 2, 1)
    return rows.reshape(shape)


def _row_tile(r):
    return next(tm for tm in (256, 128, 64, 32, 16, 8) if r % tm == 0)


def _ssd_consts(dt_bias, a_log, d_skip):
    pad = LANES - SSD_HEADS
    bias = jnp.pad(dt_bias, (0, pad)).reshape(1, LANES)
    a_neg = jnp.pad(-jnp.exp(a_log), (0, pad)).reshape(1, LANES)
    d_exp = jnp.repeat(d_skip, SSD_HEAD_DIM).reshape(1, SSD_D_INNER)
    return bias, a_neg, d_exp


def _group_layouts(acum):
    t = acum.shape[0]
    a = acum[:, :SSD_HEADS].reshape(t, SSD_GROUPS, HEADS_PER_GROUP)
    return a.transpose(1, 0, 2), a.transpose(1, 2, 0)


def _ssd_fwd(x, p, carried=None):
    hn = _rmsnorm(x, p["mix_norm"], "rmsnorm_fwd")
    proj = _matmul(hn, p["w_in"], "nt", F32, "ssd_in_fwd", tm=TOKEN_ROWS, tn=896, tk=1024)
    act = _ssd_conv_fwd(proj, p["conv_w"], p["conv_b"], "ssd_conv_fwd")
    bias, a_neg, d_exp = _ssd_consts(p["dt_bias"], p["a_log"], p["d"])
    expand = _head_expand()
    xdt, dt, acum = _ssd_dt_fwd(proj, act, bias, a_neg, expand, "ssd_dt_fwd")
    acum_g, acum_gt = _group_layouts(acum)
    if carried is None:
        (y, states), landed = _ssd_scan_fwd(xdt, act, acum_g, acum_gt, "ssd_scan_fwd"), None
    else:
        y, states, landed = _ssd_scan_fwd(xdt, act, acum_g, acum_gt, "ssd_scan_fwd_carrying_gather", carried)
    yn = _gated_norm_fwd(y, act, proj, d_exp, p["norm"], "ssd_gnorm_fwd")
    x_new = _matmul(yn, p["w_out"], "nn", F32, "ssd_out_fwd", add=x, tm=TOKEN_ROWS, tn=1024, tk=2048)
    saved = dict(x=x, hn=hn, proj=proj, act=act, xdt=xdt, dt=dt, acum_g=acum_g, acum_gt=acum_gt, y=y, states=states, yn=yn)
    return x_new, saved, landed


def _ssd_bwd(dx, p, s, carried_of=None):
    bias, a_neg, d_exp = _ssd_consts(p["dt_bias"], p["a_log"], p["d"])
    expand = _head_expand()
    dyn = _matmul(dx, p["w_out"], "nt", F32, "ssd_out_dgrad", tm=TOKEN_ROWS, tn=1024, tk=1024)
    g_w_out = _matmul(s["yn"], dx, "tn", MXU_DTYPE, "ssd_out_wgrad", tm=1024, tn=1024, tk=512)
    dyy, dz, g_norm = _gated_norm_bwd(s["y"], s["act"], s["proj"], d_exp, p["norm"], dyn, "ssd_gnorm_bwd")
    scan_args = (s["xdt"], s["act"], s["acum_g"], s["acum_gt"], s["states"], dyy)
    if carried_of is None:
        (dxdt, dbm, dcm, dacol, darow), landed = _ssd_scan_bwd(*scan_args, "ssd_scan_bwd"), None
    else:
        dxdt, dbm, dcm, dacol, darow, landed = _ssd_scan_bwd(*scan_args, "ssd_scan_bwd_carrying_grads", carried_of(g_w_out))
    t = dx.shape[0]
    dacum = dacol.transpose(1, 0, 2).reshape(t, SSD_HEADS) + darow.transpose(2, 0, 1).reshape(t, SSD_HEADS)
    dacum = jnp.pad(dacum, ((0, 0), (0, LANES - SSD_HEADS)))
    dxs, draw, g_a, g_bias, g_dexp = _ssd_dt_bwd(s["proj"], s["act"], s["dt"], dxdt, dyy, dacum, bias, a_neg, d_exp,
                                                  expand, expand.T, "ssd_dt_bwd")
    dact = jnp.concatenate([dxs, dbm, dcm], axis=1)
    dxbc, g_conv_w, g_conv_b = _ssd_conv_bwd(s["proj"], p["conv_w"], p["conv_b"], dact, "ssd_conv_bwd")
    dproj = jnp.concatenate([dz, dxbc, draw], axis=1)
    dhn = _matmul(dproj, p["w_in"], "nn", F32, "ssd_in_dgrad", tm=TOKEN_ROWS, tn=1024, tk=896)
    g_w_in = _matmul(dproj, s["hn"], "tn", MXU_DTYPE, "ssd_in_wgrad", tm=896, tn=1024, tk=512)
    dx_new, g_mix = _rmsnorm_bwd(s["x"], p["mix_norm"], dhn, dx, "rmsnorm_bwd")
    grads = dict(w_in=g_w_in[:SSD_IN_DIM], w_out=g_w_out, conv_w=g_conv_w, conv_b=g_conv_b.reshape(-1),
                 dt_bias=g_bias[0, :SSD_HEADS], a_log=(g_a * a_neg)[0, :SSD_HEADS],
                 d=g_dexp.reshape(SSD_HEADS, SSD_HEAD_DIM).sum(axis=1), norm=g_norm.reshape(-1), mix_norm=g_mix.reshape(-1))
    return dx_new, grads, landed


def _sb_fwd(x, p, carried=None):
    hn = _rmsnorm(x, p["mix_norm"], "rmsnorm_fwd")
    qkv = _matmul(hn, p["w_qkv"], "nt", MXU_DTYPE, "sb_qkv_fwd", tm=TOKEN_ROWS, tn=1024, tk=1024)
    if carried is None:
        o, landed = _sb_attention_fwd(qkv, "sb_attn_fwd"), None
    else:
        o, landed = _sb_attention_fwd(qkv, "sb_attn_fwd_carrying_gather", carried)
    x_new = _matmul(o, p["w_out"], "nn", F32, "sb_out_fwd", add=x, tm=TOKEN_ROWS, tn=1024, tk=1024)
    return x_new, dict(x=x, hn=hn, qkv=qkv, o=o), landed


def _sb_bwd(dx, p, s, carried_of=None):
    do = _matmul(dx, p["w_out"], "nt", MXU_DTYPE, "sb_out_dgrad", tm=TOKEN_ROWS, tn=1024, tk=1024)
    g_w_out = _matmul(s["o"], dx, "tn", MXU_DTYPE, "sb_out_wgrad", tm=1024, tn=1024, tk=512)
    if carried_of is None:
        (dq, dk, dv), landed = _sb_attention_bwd(s["qkv"], do, "sb_attn_bwd"), None
    else:
        dq, dk, dv, landed = _sb_attention_bwd(s["qkv"], do, "sb_attn_bwd_carrying_grads", carried_of(g_w_out))
    dqkv = jnp.concatenate([dq, dk, dv], axis=1)
    dhn = _matmul(dqkv, p["w_qkv"], "nn", F32, "sb_qkv_dgrad", tm=TOKEN_ROWS, tn=1024, tk=1024)
    g_w_qkv = _matmul(dqkv, s["hn"], "tn", MXU_DTYPE, "sb_qkv_wgrad", tm=1024, tn=1024, tk=512)
    dx_new, g_mix = _rmsnorm_bwd(s["x"], p["mix_norm"], dhn, dx, "rmsnorm_bwd")
    return dx_new, dict(w_qkv=g_w_qkv, w_out=g_w_out, mix_norm=g_mix.reshape(-1)), landed


def _ffn_fwd(x, p):
    hn = _rmsnorm(x, p["ffn_norm"], "rmsnorm_fwd")
    proj = _matmul(hn, p["w_in"], "nt", F32, "ffn_in_fwd", tm=TOKEN_ROWS, tn=1408, tk=1024)
    act = _ffn_conv_fwd(proj, p["conv_w"], p["conv_b"], "ffn_conv_fwd")
    x_new = _matmul(act, p["w_out"], "nn", F32, "ffn_out_fwd", add=x, tm=TOKEN_ROWS, tn=1024, tk=1408)
    return x_new, dict(x=x, hn=hn, proj=proj, act=act)


def _ffn_bwd(dx, p, s):
    dact = _matmul(dx, p["w_out"], "nt", F32, "ffn_out_dgrad", tm=TOKEN_ROWS, tn=1408, tk=1024)
    g_w_out = _matmul(s["act"], dx, "tn", MXU_DTYPE, "ffn_out_wgrad", tm=1408, tn=1024, tk=512)
    dpg, dpu, dwg, dwu, dbg, dbu = _ffn_conv_bwd(s["proj"], p["conv_w"], p["conv_b"], dact, "ffn_conv_bwd")
    dproj = jnp.concatenate([dpg, dpu], axis=1)
    dhn = _matmul(dproj, p["w_in"], "nn", F32, "ffn_in_dgrad", tm=TOKEN_ROWS, tn=1024, tk=1408)
    g_w_in = _matmul(dproj, s["hn"], "tn", MXU_DTYPE, "ffn_in_wgrad", tm=1408, tn=1024, tk=512)
    dx_new, g_norm = _rmsnorm_bwd(s["x"], p["ffn_norm"], dhn, dx, "rmsnorm_bwd")
    grads = dict(w_in=g_w_in, w_out=g_w_out, conv_w=jnp.concatenate([dwg, dwu], axis=1),
                 conv_b=jnp.concatenate([dbg, dbu], axis=1).reshape(-1), ffn_norm=g_norm.reshape(-1))
    return dx_new, grads


ADD_ROWS = 256
BIG = ["ssd_w_in", "sb_w_qkv", "ffn_w_in", "ssd_w_out", "sb_w_out", "ffn_w_out"]
LAYER_PIECES = [[("ssd_w_in", 0), ("ssd_w_out", 0), ("ffn_w_in", 0), ("ffn_w_out", 0)],
                [("sb_w_qkv", 0), ("sb_w_out", 0), ("ffn_w_in", 1), ("ffn_w_out", 1)],
                [("ssd_w_in", 1), ("ssd_w_out", 1), ("ffn_w_in", 2), ("ffn_w_out", 2)],
                [("sb_w_qkv", 1), ("sb_w_out", 1), ("ffn_w_in", 3), ("ffn_w_out", 3)]]
GRAD_SETS = {3: [("ffn_w_in", 3), ("ffn_w_out", 3), ("sb_w_out", 1)],
             1: [("sb_w_qkv", 1), ("ssd_w_out", 1), ("ffn_w_in", 2), ("ffn_w_out", 2), ("ffn_w_in", 1), ("ffn_w_out", 1),
                 ("sb_w_out", 0), ("ssd_w_in", 1)],
             0: [("sb_w_qkv", 0), ("ffn_w_in", 0), ("ffn_w_out", 0), ("ssd_w_out", 0)],
             "end": [("ssd_w_in", 0)]}
GATHER_SETS = {"early": [("ssd_w_out", 0), ("ffn_w_in", 0), ("ffn_w_out", 0), ("ssd_w_in", 0)],
               0: LAYER_PIECES[1],
               1: [("ssd_w_out", 1), ("ffn_w_in", 2), ("ffn_w_out", 2)] + LAYER_PIECES[3] + [("ssd_w_in", 1)]}
COL_SHARDED = {"ssd_w_in": 2, "sb_w_qkv": 2, "ffn_w_in": 4}
CONV = ["ssd_conv_w", "ffn_conv_w"]
SMALL = ["mix_norm", "ffn_norm", "final_norm", "ssd_conv_b", "ssd_dt_bias", "ssd_a_log", "ssd_d", "ssd_norm", "ffn_conv_b"]
WEIGHTS = ["mix_norm", "ffn_norm", "final_norm", "ssd_w_in", "ssd_conv_w", "ssd_conv_b", "ssd_dt_bias", "ssd_a_log", "ssd_d",
           "ssd_norm", "ssd_w_out", "sb_w_qkv", "sb_w_out", "ffn_w_in", "ffn_conv_w", "ffn_conv_b", "ffn_w_out"]


def _step(x, loss_target, w, m, v):
    x = x.reshape(x.shape[-2], x.shape[-1])
    target = loss_target.reshape(x.shape)
    dev = 4 * lax.axis_index("x") + 2 * lax.axis_index("y") + lax.axis_index("c")
    core = lax.axis_index("c")

    shard_rows = {n: _shard_as_rows(n, w[n].astype(MXU_DTYPE)) for n in BIG}
    per_shard = {n: shard_rows[n].shape[0] // w[n].shape[0] for n in BIG}

    def layout(pieces):
        where, off = {}, 0
        for n, l in pieces:
            where[(n, l)] = (off, per_shard[n])
            off += per_shard[n]
        return where

    def pack_pieces(pieces, rows_of):
        return jnp.concatenate([rows_of(piece) for piece in pieces], axis=-2)

    def shard_piece(piece):
        n, l = piece
        return shard_rows[n][l * per_shard[n]:(l + 1) * per_shard[n]]

    full = {}

    def unpack_weights(gathered, where):
        for (n, l), (off, rows) in where.items():
            mat = gathered[:, off:off + rows].reshape(N_DEV * rows, PACK_W)
            if n == "ssd_w_in":
                mat = jnp.pad(mat, ((0, SSD_IN_PAD - SSD_IN_DIM), (0, 0)))
            full[(n, l)] = mat

    unpack_weights(_all_gather(pack_pieces(GATHER_SETS["early"], shard_piece), "gather_weights_early"), layout(GATHER_SETS["early"]))
    conv_pack, conv_layout = _pack([w[n] for n in CONV], 8)
    conv_all = _all_gather(conv_pack, "gather_conv_taps")
    for n, (off, nrows, shape) in zip(CONV, conv_layout):
        parts = [_unpack(conv_all[j], conv_layout)[CONV.index(n)] for j in range(N_DEV)]
        full[n] = jnp.concatenate(parts, axis=-1)

    def ssd_params(j):
        return dict(mix_norm=w["mix_norm"][2 * j], w_in=full[("ssd_w_in", j)], conv_w=full["ssd_conv_w"][j],
                    conv_b=w["ssd_conv_b"][j], dt_bias=w["ssd_dt_bias"][j], a_log=w["ssd_a_log"][j], d=w["ssd_d"][j],
                    norm=w["ssd_norm"][j], w_out=full[("ssd_w_out", j)])

    def sb_params(j):
        return dict(mix_norm=w["mix_norm"][2 * j + 1], w_qkv=full[("sb_w_qkv", j)], w_out=full[("sb_w_out", j)])

    def ffn_params(i):
        return dict(ffn_norm=w["ffn_norm"][i], w_in=full[("ffn_w_in", i)], conv_w=full["ffn_conv_w"][i],
                    conv_b=w["ffn_conv_b"][i], w_out=full[("ffn_w_out", i)])

    saved = []
    for i in range(DEPTH):
        mixer_fwd, params = (_ssd_fwd, ssd_params) if i % 2 == 0 else (_sb_fwd, sb_params)
        if i in GATHER_SETS:
            x, s_mix, arrived = mixer_fwd(x, params(i // 2), carried=(pack_pieces(GATHER_SETS[i], shard_piece), False))
            unpack_weights(arrived, layout(GATHER_SETS[i]))
        else:
            x, s_mix, _ = mixer_fwd(x, params(i // 2))
        x, s_ffn = _ffn_fwd(x, ffn_params(i))
        saved.append((s_mix, s_ffn))
    dx, g_final, loss_part = _final_norm_loss(x, w["final_norm"], target, "final_norm_loss")

    piece_grad = {}

    def grad_piece(piece):
        g = piece_grad[piece]
        return g.reshape(N_DEV, g.shape[0] // N_DEV, PACK_W)

    def carried_set(pieces, own_piece):
        def make(g_w_out):
            piece_grad[own_piece] = g_w_out
            return pack_pieces(pieces, grad_piece), True
        return make

    g_mix, g_ffn, g_ssd, g_sb = [None] * DEPTH, [None] * DEPTH, [None] * 2, [None] * 2
    landed = {}
    for i in reversed(range(DEPTH)):
        s_mix, s_ffn = saved[i]
        dx, g_ffn[i] = _ffn_bwd(dx, ffn_params(i), s_ffn)
        piece_grad[("ffn_w_in", i)], piece_grad[("ffn_w_out", i)] = g_ffn[i]["w_in"], g_ffn[i]["w_out"]
        j = i // 2
        if i % 2 == 0:
            carried_of = carried_set(GRAD_SETS[i], ("ssd_w_out", j)) if i in GRAD_SETS else None
            dx, g_ssd[j], landed[i] = _ssd_bwd(dx, ssd_params(j), s_mix, carried_of)
            g_mix[i] = g_ssd[j]["mix_norm"]
            piece_grad[("ssd_w_in", j)], piece_grad[("ssd_w_out", j)] = g_ssd[j]["w_in"], g_ssd[j]["w_out"]
        else:
            dx, g_sb[j], landed[i] = _sb_bwd(dx, sb_params(j), s_mix, carried_set(GRAD_SETS[i], ("sb_w_out", j)))
            g_mix[i] = g_sb[j]["mix_norm"]
            piece_grad[("sb_w_qkv", j)] = g_sb[j]["w_qkv"]
    grad_x = dx.reshape(1, *dx.shape)

    g8 = pack_pieces(GRAD_SETS["end"], grad_piece)
    g8 = jnp.pad(g8, ((0, 0), (0, (-g8.shape[1]) % ADD_ROWS), (0, 0)))
    g8 = g8.reshape(4, 2, *g8.shape[1:])
    keep = lax.dynamic_index_in_dim(g8, core, axis=1, keepdims=False)
    give = lax.dynamic_index_in_dim(g8, 1 - core, axis=1, keepdims=False)
    got = _swap_with_sibling(give, "grads_to_sibling")
    chip_part = _add_pair(keep, got, "grads_add_sibling")
    landed["end"] = _exchange_chips(chip_part, "grads_across_chips")

    summed = {}
    for key, pieces in GRAD_SETS.items():
        total = _sum_slots(landed[key], "grads_sum_landed")
        for piece, (off, rows) in layout(pieces).items():
            summed[piece] = total[off:off + rows]
    big_res = [dict() for _ in range(4)]
    for n in BIG:
        lead, rows, cols = w[n].shape
        g_rows = jnp.concatenate([summed[(n, l)] for l in range(lead)], axis=0)
        g_nat = _rows_as_shard(n, g_rows, w[n].shape).reshape(1, lead * rows, cols)
        two_d = (lead * rows, cols)
        outs = _adamw(g_nat, w[n].reshape(two_d), m[n].reshape(two_d), v[n].reshape(two_d), "adamw_" + n)
        for kind in range(4):
            big_res[kind][n] = outs[kind].reshape(w[n].shape)

    small_g = {
        "mix_norm": jnp.stack(g_mix), "ffn_norm": jnp.stack([g["ffn_norm"] for g in g_ffn]), "final_norm": g_final.reshape(-1),
        "ssd_conv_b": jnp.stack([g["conv_b"] for g in g_ssd]), "ssd_dt_bias": jnp.stack([g["dt_bias"] for g in g_ssd]),
        "ssd_a_log": jnp.stack([g["a_log"] for g in g_ssd]), "ssd_d": jnp.stack([g["d"] for g in g_ssd]),
        "ssd_norm": jnp.stack([g["norm"] for g in g_ssd]), "ffn_conv_b": jnp.stack([g["conv_b"] for g in g_ffn]),
    }
    conv_g = {"ssd_conv_w": jnp.stack([g["conv_w"] for g in g_ssd]), "ffn_conv_w": jnp.stack([g["conv_w"] for g in g_ffn])}
    extra = [conv_g[n] for n in CONV] + [loss_part]
    small_pack, small_layout = _pack([small_g[n] for n in SMALL] + extra, 8)
    small_all = _all_gather(small_pack, "gather_small_grads")
    zeros_like = [jnp.zeros(a.shape, F32) for a in extra]
    sw, _ = _pack([w[n] for n in SMALL] + zeros_like, 8)
    sm, _ = _pack([m[n] for n in SMALL] + zeros_like, 8)
    sv, _ = _pack([v[n] for n in SMALL] + [jnp.ones(a.shape, F32) for a in extra], 8)
    small_out = _adamw(small_all, sw, sm, sv, "adamw_replicated")
    small_res = [_unpack(o, small_layout) for o in small_out]
    summed = small_res[0]
    loss = summed[-1][0, 0]
    conv_shard_g = []
    for n, gsum in zip(CONV, summed[len(SMALL):len(SMALL) + len(CONV)]):
        ns = w[n].shape[-1]
        conv_shard_g.append(lax.dynamic_slice_in_dim(gsum, dev * ns, ns, axis=2))
    cg, conv_sh_layout = _pack(conv_shard_g, 8)
    cw, _ = _pack([w[n] for n in CONV], 8)
    cm_, _ = _pack([m[n] for n in CONV], 8)
    cv, _ = _pack([v[n] for n in CONV], 8)
    conv_out = _adamw(cg.reshape(1, *cg.shape), cw, cm_, cv, "adamw_conv_taps")
    conv_res = [dict(zip(CONV, _unpack(o, conv_sh_layout))) for o in conv_out]

    def pick(kind, n):
        if n in BIG:
            return big_res[kind][n]
        if n in CONV:
            return conv_res[kind][n]
        return small_res[kind][SMALL.index(n)]

    outs = [loss, grad_x]
    for kind in range(4):
        outs += [pick(kind, n) for n in WEIGHTS]
    return tuple(outs)


def kernel(x, mix_norm, ffn_norm, final_norm, ssd_w_in, ssd_conv_w, ssd_conv_b, ssd_dt_bias, ssd_a_log, ssd_d, ssd_norm, ssd_w_out, sb_w_qkv, sb_w_out, ffn_w_in, ffn_conv_w, ffn_conv_b, ffn_w_out, loss_target, m_mix_norm, m_ffn_norm, m_final_norm, m_ssd_w_in, m_ssd_conv_w, m_ssd_conv_b, m_ssd_dt_bias, m_ssd_a_log, m_ssd_d, m_ssd_norm, m_ssd_w_out, m_sb_w_qkv, m_sb_w_out, m_ffn_w_in, m_ffn_conv_w, m_ffn_conv_b, m_ffn_w_out, v_mix_norm, v_ffn_norm, v_final_norm, v_ssd_w_in, v_ssd_conv_w, v_ssd_conv_b, v_ssd_dt_bias, v_ssd_a_log, v_ssd_d, v_ssd_norm, v_ssd_w_out, v_sb_w_qkv, v_sb_w_out, v_ffn_w_in, v_ffn_conv_w, v_ffn_conv_b, v_ffn_w_out):
    w = dict(mix_norm=mix_norm, ffn_norm=ffn_norm, final_norm=final_norm, ssd_w_in=ssd_w_in, ssd_conv_w=ssd_conv_w,
             ssd_conv_b=ssd_conv_b, ssd_dt_bias=ssd_dt_bias, ssd_a_log=ssd_a_log, ssd_d=ssd_d, ssd_norm=ssd_norm,
             ssd_w_out=ssd_w_out, sb_w_qkv=sb_w_qkv, sb_w_out=sb_w_out, ffn_w_in=ffn_w_in, ffn_conv_w=ffn_conv_w,
             ffn_conv_b=ffn_conv_b, ffn_w_out=ffn_w_out)
    m = dict(mix_norm=m_mix_norm, ffn_norm=m_ffn_norm, final_norm=m_final_norm, ssd_w_in=m_ssd_w_in, ssd_conv_w=m_ssd_conv_w,
             ssd_conv_b=m_ssd_conv_b, ssd_dt_bias=m_ssd_dt_bias, ssd_a_log=m_ssd_a_log, ssd_d=m_ssd_d, ssd_norm=m_ssd_norm,
             ssd_w_out=m_ssd_w_out, sb_w_qkv=m_sb_w_qkv, sb_w_out=m_sb_w_out, ffn_w_in=m_ffn_w_in, ffn_conv_w=m_ffn_conv_w,
             ffn_conv_b=m_ffn_conv_b, ffn_w_out=m_ffn_w_out)
    v = dict(mix_norm=v_mix_norm, ffn_norm=v_ffn_norm, final_norm=v_final_norm, ssd_w_in=v_ssd_w_in, ssd_conv_w=v_ssd_conv_w,
             ssd_conv_b=v_ssd_conv_b, ssd_dt_bias=v_ssd_dt_bias, ssd_a_log=v_ssd_a_log, ssd_d=v_ssd_d, ssd_norm=v_ssd_norm,
             ssd_w_out=v_ssd_w_out, sb_w_qkv=v_sb_w_qkv, sb_w_out=v_sb_w_out, ffn_w_in=v_ffn_w_in, ffn_conv_w=v_ffn_conv_w,
             ffn_conv_b=v_ffn_conv_b, ffn_w_out=v_ffn_w_out)
    return _step(x, loss_target, w, m, v)
```

```python
import jax
import jax.numpy as jnp
import numpy as np
from jax import lax
from jax.experimental import pallas as pl
from jax.experimental.pallas import tpu as pltpu

F32 = jnp.float32
MXU_DTYPE = jnp.bfloat16
MESH_ID = pl.DeviceIdType.MESH
N_DEV = 8

NORM_EPS = 1e-6
D_MODEL = 1024
DEPTH = 4
SSD_D_INNER = 2048
SSD_HEADS = 32
SSD_HEAD_DIM = 64
SSD_GROUPS = 8
SSD_STATE = 128
SSD_CONV = 4
SSD_CHUNK = 128
SSD_CONV_DIM = SSD_D_INNER + 2 * SSD_GROUPS * SSD_STATE
SSD_IN_DIM = SSD_D_INNER + SSD_CONV_DIM + SSD_HEADS
LANES = 128
SSD_IN_PAD = SSD_D_INNER + SSD_CONV_DIM + LANES
SB_HEADS = 16
SB_HEAD_DIM = 64
SB_TILE = 256
SB_SCALE = SB_HEAD_DIM ** -0.5
FFN_D_FF = 2816
FFN_CONV = 3
PACK_W = 1024

ADAM_LR = 0.001
ADAM_B1 = 0.9
ADAM_B2 = 0.999
ADAM_EPS = 1e-08
ADAM_WD = 0.01
ADAM_STEP = 10

VMEM_LIMIT_BYTES = 56 * 1024 * 1024


def _cp(*sem):
    return pltpu.CompilerParams(dimension_semantics=sem, vmem_limit_bytes=VMEM_LIMIT_BYTES)


def _iota(shape, dim):
    return lax.broadcasted_iota(jnp.int32, shape, dim)


def _sigmoid(x):
    return 1.0 / (1.0 + jnp.exp(-x))


def _mm(a, b):
    return lax.dot_general(a, b, (((1,), (0,)), ((), ())), preferred_element_type=F32)


def _mm_nt(a, b):
    return lax.dot_general(a, b, (((1,), (1,)), ((), ())), preferred_element_type=F32)


def _mm_tn(a, b):
    return lax.dot_general(a, b, (((0,), (0,)), ((), ())), preferred_element_type=F32)


def _split(x):
    hi = x.astype(MXU_DTYPE)
    lo = (x - hi.astype(F32)).astype(MXU_DTYPE)
    return hi, lo


def _mm_exact_rhs(x, m):
    hi, lo = _split(x)
    return _mm(jnp.concatenate([hi, lo], axis=1), jnp.concatenate([m, m], axis=0))


def _mm_exact_lhs(m, x):
    hi, lo = _split(x)
    return _mm(jnp.concatenate([m, m], axis=1), jnp.concatenate([hi, lo], axis=0))


def _my_place():
    return lax.axis_index("x"), lax.axis_index("y"), lax.axis_index("c")


def _all_gather(shard, name):
    r, c_ = shard.shape

    def body(x_ref, out_ref, send_sems, recv_sems, local_sem):
        x, y, c = _my_place()
        me, sibling = (x, y, c), (x, y, 1 - c)
        chips = [(1 - x, y), (x, 1 - y), (1 - x, 1 - y)]

        def slot(px, py, pc):
            return out_ref.at[4 * px + 2 * py + pc]

        def copy(k, block, to, src=None):
            return pltpu.make_async_remote_copy(
                src_ref=slot(*block) if src is None else src, dst_ref=slot(*block),
                send_sem=send_sems.at[k], recv_sem=recv_sems.at[k], device_id=to, device_id_type=MESH_ID)

        mine = pltpu.make_async_copy(x_ref, slot(*me), local_sem)
        mine.start()
        first = [copy(0, me, sibling, src=x_ref)]
        first += [copy(1 + j, me, (*chip, c), src=x_ref) for j, chip in enumerate(chips)]
        for cp in first:
            cp.start()
        passed = [copy(4 + j, (*chip, c), sibling) for j, chip in enumerate(chips)]
        for j, chip in enumerate(chips):
            copy(1 + j, (*chip, c), me).wait_recv()
            passed[j].start()
        copy(0, sibling, me).wait_recv()
        for j, chip in enumerate(chips):
            copy(4 + j, (*chip, 1 - c), me).wait_recv()
        for cp in first + passed:
            cp.wait_send()
        mine.wait()

    return pl.pallas_call(
        body, name=name,
        out_shape=jax.ShapeDtypeStruct((N_DEV, r, c_), shard.dtype),
        in_specs=[pl.BlockSpec(memory_space=pl.ANY)],
        out_specs=pl.BlockSpec(memory_space=pl.ANY),
        scratch_shapes=[pltpu.SemaphoreType.DMA((7,)), pltpu.SemaphoreType.DMA((7,)), pltpu.SemaphoreType.DMA(())],
    )(shard)


def _swap_with_sibling(buf, name):
    def body(x_ref, out_ref, send_sem, recv_sem):
        x, y, c = _my_place()
        cp = pltpu.make_async_remote_copy(src_ref=x_ref, dst_ref=out_ref, send_sem=send_sem, recv_sem=recv_sem,
                                          device_id=(x, y, 1 - c), device_id_type=MESH_ID)
        cp.start()
        cp.wait()

    return pl.pallas_call(
        body, name=name, out_shape=jax.ShapeDtypeStruct(buf.shape, buf.dtype),
        in_specs=[pl.BlockSpec(memory_space=pl.ANY)], out_specs=pl.BlockSpec(memory_space=pl.ANY),
        scratch_shapes=[pltpu.SemaphoreType.DMA(()), pltpu.SemaphoreType.DMA(())],
    )(buf)


def _exchange_chips(parts, name):
    def body(p_ref, out_ref, send_sems, recv_sems, local_sem):
        x, y, c = _my_place()
        my_q = 2 * x + y
        chips = [(1 - x, y), (x, 1 - y), (1 - x, 1 - y)]
        local = pltpu.make_async_copy(p_ref.at[my_q], out_ref.at[my_q], local_sem)
        local.start()

        def copy(k, px, py):
            return pltpu.make_async_remote_copy(
                src_ref=p_ref.at[2 * px + py], dst_ref=out_ref.at[my_q],
                send_sem=send_sems.at[k], recv_sem=recv_sems.at[k], device_id=(px, py, c), device_id_type=MESH_ID)

        def landing(k, px, py):
            return pltpu.make_async_remote_copy(
                src_ref=p_ref.at[my_q], dst_ref=out_ref.at[2 * px + py],
                send_sem=send_sems.at[k], recv_sem=recv_sems.at[k], device_id=(px, py, c), device_id_type=MESH_ID)

        sends = [copy(k, px, py) for k, (px, py) in enumerate(chips)]
        for cp in sends:
            cp.start()
        for k, (px, py) in enumerate(chips):
            landing(k, px, py).wait_recv()
        for cp in sends:
            cp.wait_send()
        local.wait()

    return pl.pallas_call(
        body, name=name, out_shape=jax.ShapeDtypeStruct(parts.shape, parts.dtype),
        in_specs=[pl.BlockSpec(memory_space=pl.ANY)], out_specs=pl.BlockSpec(memory_space=pl.ANY),
        scratch_shapes=[pltpu.SemaphoreType.DMA((3,)), pltpu.SemaphoreType.DMA((3,)), pltpu.SemaphoreType.DMA(())],
    )(parts)


RELATIONS = [(0, 0, 1), (1, 0, 0), (0, 1, 0), (1, 1, 0), (1, 0, 1), (0, 1, 1), (1, 1, 1)]
EXCHANGE_SCRATCH = [pltpu.SemaphoreType.DMA((len(RELATIONS),)), pltpu.SemaphoreType.DMA((len(RELATIONS),)),
                    pltpu.SemaphoreType.DMA(())]


def _exchange_copies(src_ref, land_ref, send_sems, recv_sems, local_sem, per_peer, incoming=True):
    x, y, c = _my_place()
    me = 4 * x + 2 * y + c

    def src(j):
        return src_ref.at[j] if per_peer else src_ref

    local = pltpu.make_async_copy(src(me), land_ref.at[me], local_sem)
    pairs = []
    for k, (dx, dy, dc) in enumerate(RELATIONS):
        peer = (1 - x if dx else x, 1 - y if dy else y, 1 - c if dc else c)
        j = 4 * peer[0] + 2 * peer[1] + peer[2]
        sems = dict(send_sem=send_sems.at[k], recv_sem=recv_sems.at[k], device_id=peer, device_id_type=MESH_ID)
        pairs.append((pltpu.make_async_remote_copy(src_ref=src(j), dst_ref=land_ref.at[me], **sems),
                      pltpu.make_async_remote_copy(src_ref=src(me), dst_ref=land_ref.at[j], **sems) if incoming else None))
    return local, pairs


def _exchange_start(*refs, per_peer):
    local, pairs = _exchange_copies(*refs, per_peer, incoming=False)
    local.start()
    for outgoing, _ in pairs:
        outgoing.start()


def _exchange_finish(*refs, per_peer):
    local, pairs = _exchange_copies(*refs, per_peer)
    for _, incoming in pairs:
        incoming.wait_recv()
    for outgoing, _ in pairs:
        outgoing.wait_send()
    local.wait()


TOKEN_ROWS = 1024


def _matmul(a, b, mode, out_dtype, name, add=None, tm=512, tn=512, tk=512):
    if mode == "nn":
        (m, k), (k2, n) = a.shape, b.shape
    elif mode == "nt":
        (m, k), (n, k2) = a.shape, b.shape
    else:
        (k, m), (k2, n) = a.shape, b.shape
    assert k == k2, (a.shape, b.shape, mode)
    tm, tn, tk = min(tm, m), min(tn, n), min(tk, k)
    assert m % tm == 0 and n % tn == 0 and k % tk == 0, (m, n, k, tm, tn, tk)
    nk = k // tk
    mm = {"nn": _mm, "nt": _mm_nt, "tn": _mm_tn}[mode]

    def body(*refs):
        if add is None:
            a_ref, b_ref, o_ref, acc_ref = refs
        else:
            a_ref, b_ref, add_ref, o_ref, acc_ref = refs
        kk = pl.program_id(2)

        @pl.when(kk == 0)
        def _():
            acc_ref[...] = jnp.zeros_like(acc_ref)

        acc_ref[...] += mm(a_ref[...].astype(MXU_DTYPE), b_ref[...].astype(MXU_DTYPE))

        @pl.when(kk == nk - 1)
        def _():
            res = acc_ref[...]
            if add is not None:
                res = res + add_ref[...]
            o_ref[...] = res.astype(o_ref.dtype)

    a_spec = {"nn": pl.BlockSpec((tm, tk), lambda i, j, kk: (i, kk)),
              "nt": pl.BlockSpec((tm, tk), lambda i, j, kk: (i, kk)),
              "tn": pl.BlockSpec((tk, tm), lambda i, j, kk: (kk, i))}[mode]
    b_spec = {"nn": pl.BlockSpec((tk, tn), lambda i, j, kk: (kk, j)),
              "nt": pl.BlockSpec((tn, tk), lambda i, j, kk: (j, kk)),
              "tn": pl.BlockSpec((tk, tn), lambda i, j, kk: (kk, j))}[mode]
    o_spec = pl.BlockSpec((tm, tn), lambda i, j, kk: (i, j))
    in_specs, args = [a_spec, b_spec], [a, b]
    if add is not None:
        in_specs.append(o_spec)
        args.append(add)
    return pl.pallas_call(
        body, name=name, grid=(m // tm, n // tn, nk), in_specs=in_specs, out_specs=o_spec,
        out_shape=jax.ShapeDtypeStruct((m, n), out_dtype),
        scratch_shapes=[pltpu.VMEM((tm, tn), F32)],
        compiler_params=_cp("parallel", "parallel", "arbitrary"),
    )(*args)


def _rmsnorm(x, g, name):
    t, d = x.shape
    tm = min(512, t)

    def body(x_ref, g_ref, o_ref):
        xv = x_ref[...]
        r = lax.rsqrt(jnp.mean(xv * xv, axis=-1, keepdims=True) + NORM_EPS)
        o_ref[...] = (xv * r * g_ref[...]).astype(o_ref.dtype)

    return pl.pallas_call(
        body, name=name, grid=(t // tm,),
        in_specs=[pl.BlockSpec((tm, d), lambda i: (i, 0)), pl.BlockSpec((1, d), lambda i: (0, 0))],
        out_specs=pl.BlockSpec((tm, d), lambda i: (i, 0)),
        out_shape=jax.ShapeDtypeStruct((t, d), MXU_DTYPE), compiler_params=_cp("parallel"),
    )(x, g.reshape(1, d))


def _rmsnorm_bwd(x, g, dh, dres, name):
    t, d = x.shape
    tm = min(512, t)

    def body(x_ref, g_ref, dh_ref, dres_ref, dx_ref, dg_ref):
        @pl.when(pl.program_id(0) == 0)
        def _():
            dg_ref[...] = jnp.zeros_like(dg_ref)

        xv = x_ref[...]
        r = lax.rsqrt(jnp.mean(xv * xv, axis=-1, keepdims=True) + NORM_EPS)
        xn = xv * r
        dhv = dh_ref[...]
        u = dhv * g_ref[...]
        dx_ref[...] = dres_ref[...] + r * (u - xn * jnp.mean(u * xn, axis=-1, keepdims=True))
        dg_ref[...] += jnp.sum(dhv * xn, axis=0, keepdims=True)

    row = pl.BlockSpec((tm, d), lambda i: (i, 0))
    vec = pl.BlockSpec((1, d), lambda i: (0, 0))
    return pl.pallas_call(
        body, name=name, grid=(t // tm,), in_specs=[row, vec, row, row], out_specs=[row, vec],
        out_shape=[jax.ShapeDtypeStruct((t, d), F32), jax.ShapeDtypeStruct((1, d), F32)],
        compiler_params=_cp("arbitrary"),
    )(x, g.reshape(1, d), dh, dres)


def _final_norm_loss(x, g, target, name):
    t, d = x.shape
    tm = min(512, t)

    def body(x_ref, g_ref, t_ref, dx_ref, dg_ref, loss_ref):
        @pl.when(pl.program_id(0) == 0)
        def _():
            dg_ref[...] = jnp.zeros_like(dg_ref)
            loss_ref[...] = jnp.zeros_like(loss_ref)

        xv = x_ref[...]
        gv = g_ref[...]
        r = lax.rsqrt(jnp.mean(xv * xv, axis=-1, keepdims=True) + NORM_EPS)
        xn = xv * r
        err = xn * gv - t_ref[...]
        per_tok = jnp.mean(err * err, axis=-1, keepdims=True)
        loss_ref[...] += jnp.broadcast_to(0.5 * jnp.sum(per_tok, axis=0, keepdims=True), loss_ref.shape)
        dy = err * (1.0 / d)
        u = dy * gv
        dx_ref[...] = r * (u - xn * jnp.mean(u * xn, axis=-1, keepdims=True))
        dg_ref[...] += jnp.sum(dy * xn, axis=0, keepdims=True)

    row = pl.BlockSpec((tm, d), lambda i: (i, 0))
    vec = pl.BlockSpec((1, d), lambda i: (0, 0))
    return pl.pallas_call(
        body, name=name, grid=(t // tm,), in_specs=[row, vec, row],
        out_specs=[row, vec, pl.BlockSpec((1, LANES), lambda i: (0, 0))],
        out_shape=[jax.ShapeDtypeStruct((t, d), F32), jax.ShapeDtypeStruct((1, d), F32),
                   jax.ShapeDtypeStruct((1, LANES), F32)],
        compiler_params=_cp("arbitrary"),
    )(x, g.reshape(1, d), target)


CONV_COLS = 128


def _shifts_down(p, width):
    row = _iota(p.shape, 0)
    return [jnp.where(row >= s, pltpu.roll(p, s, axis=0), 0.0) for s in range(1, width)]


def _shifts_up(p, width):
    n = p.shape[0]
    row = _iota(p.shape, 0)
    return [jnp.where(row < n - s, pltpu.roll(p, n - s, axis=0), 0.0) for s in range(1, width)]


def _conv_pre(p, shifted, w_ref, b_ref):
    width = w_ref.shape[0]
    u = b_ref[...] + w_ref[width - 1:width, :] * p
    for s in range(1, width):
        u = u + w_ref[width - 1 - s:width - s, :] * shifted[s - 1]
    return u


def _conv_transpose(du, w_ref):
    width = w_ref.shape[0]
    shifted = _shifts_up(du, width)
    dp = w_ref[width - 1:width, :] * du
    for s in range(1, width):
        dp = dp + w_ref[width - 1 - s:width - s, :] * shifted[s - 1]
    return dp


def _conv_wgrad(du, p, shifted, dw_ref, db_ref):
    width = dw_ref.shape[0]
    db_ref[...] = jnp.sum(du, axis=0, keepdims=True)
    dw_ref[width - 1:width, :] = jnp.sum(du * p, axis=0, keepdims=True)
    for s in range(1, width):
        dw_ref[width - 1 - s:width - s, :] = jnp.sum(du * shifted[s - 1], axis=0, keepdims=True)


def _ssd_conv_fwd(proj, w, b, name):
    t = proj.shape[0]
    cb = CONV_COLS
    off = SSD_D_INNER // cb

    def body(p_ref, w_ref, b_ref, o_ref):
        p = p_ref[...]
        u = _conv_pre(p, _shifts_down(p, SSD_CONV), w_ref, b_ref)
        o_ref[...] = u * _sigmoid(u)

    return pl.pallas_call(
        body, name=name, grid=(SSD_CONV_DIM // cb,),
        in_specs=[pl.BlockSpec((t, cb), lambda j: (0, j + off)), pl.BlockSpec((SSD_CONV, cb), lambda j: (0, j)),
                  pl.BlockSpec((1, cb), lambda j: (0, j))],
        out_specs=pl.BlockSpec((t, cb), lambda j: (0, j)),
        out_shape=jax.ShapeDtypeStruct((t, SSD_CONV_DIM), F32), compiler_params=_cp("parallel"),
    )(proj, w, b.reshape(1, -1))


def _ssd_conv_bwd(proj, w, b, dact, name):
    t = proj.shape[0]
    cb = CONV_COLS
    off = SSD_D_INNER // cb

    def body(p_ref, w_ref, b_ref, da_ref, dp_ref, dw_ref, db_ref):
        p = p_ref[...]
        shifted = _shifts_down(p, SSD_CONV)
        u = _conv_pre(p, shifted, w_ref, b_ref)
        sg = _sigmoid(u)
        du = da_ref[...] * (sg * (1.0 + u * (1.0 - sg)))
        dp_ref[...] = _conv_transpose(du, w_ref).astype(dp_ref.dtype)
        _conv_wgrad(du, p, shifted, dw_ref, db_ref)

    col = pl.BlockSpec((t, cb), lambda j: (0, j))
    wspec = pl.BlockSpec((SSD_CONV, cb), lambda j: (0, j))
    bspec = pl.BlockSpec((1, cb), lambda j: (0, j))
    return pl.pallas_call(
        body, name=name, grid=(SSD_CONV_DIM // cb,),
        in_specs=[pl.BlockSpec((t, cb), lambda j: (0, j + off)), wspec, bspec, col],
        out_specs=[col, wspec, bspec],
        out_shape=[jax.ShapeDtypeStruct((t, SSD_CONV_DIM), MXU_DTYPE), jax.ShapeDtypeStruct((SSD_CONV, SSD_CONV_DIM), F32),
                   jax.ShapeDtypeStruct((1, SSD_CONV_DIM), F32)],
        compiler_params=_cp("parallel"),
    )(proj, w, b.reshape(1, -1), dact)


def _ffn_conv_fwd(proj, w, b, name):
    t = proj.shape[0]
    cb = CONV_COLS
    nb = FFN_D_FF // cb

    def body(pg_ref, pu_ref, wg_ref, wu_ref, bg_ref, bu_ref, o_ref):
        pg, pu = pg_ref[...], pu_ref[...]
        ug = _conv_pre(pg, _shifts_down(pg, FFN_CONV), wg_ref, bg_ref)
        uu = _conv_pre(pu, _shifts_down(pu, FFN_CONV), wu_ref, bu_ref)
        o_ref[...] = (ug * _sigmoid(ug) * uu).astype(o_ref.dtype)

    gcol = pl.BlockSpec((t, cb), lambda j: (0, j))
    ucol = pl.BlockSpec((t, cb), lambda j: (0, j + nb))
    b2 = b.reshape(1, -1)
    return pl.pallas_call(
        body, name=name, grid=(nb,),
        in_specs=[gcol, ucol, pl.BlockSpec((FFN_CONV, cb), lambda j: (0, j)), pl.BlockSpec((FFN_CONV, cb), lambda j: (0, j + nb)),
                  pl.BlockSpec((1, cb), lambda j: (0, j)), pl.BlockSpec((1, cb), lambda j: (0, j + nb))],
        out_specs=gcol, out_shape=jax.ShapeDtypeStruct((t, FFN_D_FF), MXU_DTYPE), compiler_params=_cp("parallel"),
    )(proj, proj, w, w, b2, b2)


def _ffn_conv_bwd(proj, w, b, dact, name):
    t = proj.shape[0]
    cb = CONV_COLS
    nb = FFN_D_FF // cb

    def body(pg_ref, pu_ref, wg_ref, wu_ref, bg_ref, bu_ref, da_ref,
             dpg_ref, dpu_ref, dwg_ref, dwu_ref, dbg_ref, dbu_ref):
        pg, pu = pg_ref[...], pu_ref[...]
        pg_shifted, pu_shifted = _shifts_down(pg, FFN_CONV), _shifts_down(pu, FFN_CONV)
        ug = _conv_pre(pg, pg_shifted, wg_ref, bg_ref)
        uu = _conv_pre(pu, pu_shifted, wu_ref, bu_ref)
        sg = _sigmoid(ug)
        da = da_ref[...]
        dug = da * uu * (sg * (1.0 + ug * (1.0 - sg)))
        duu = da * (ug * sg)
        dpg_ref[...] = _conv_transpose(dug, wg_ref).astype(dpg_ref.dtype)
        dpu_ref[...] = _conv_transpose(duu, wu_ref).astype(dpu_ref.dtype)
        _conv_wgrad(dug, pg, pg_shifted, dwg_ref, dbg_ref)
        _conv_wgrad(duu, pu, pu_shifted, dwu_ref, dbu_ref)

    gcol = pl.BlockSpec((t, cb), lambda j: (0, j))
    ucol = pl.BlockSpec((t, cb), lambda j: (0, j + nb))
    wg = pl.BlockSpec((FFN_CONV, cb), lambda j: (0, j))
    wu = pl.BlockSpec((FFN_CONV, cb), lambda j: (0, j + nb))
    bg = pl.BlockSpec((1, cb), lambda j: (0, j))
    bu = pl.BlockSpec((1, cb), lambda j: (0, j + nb))
    b2 = b.reshape(1, -1)
    half = jax.ShapeDtypeStruct((t, FFN_D_FF), MXU_DTYPE)
    return pl.pallas_call(
        body, name=name, grid=(nb,),
        in_specs=[gcol, ucol, wg, wu, bg, bu, gcol],
        out_specs=[gcol, gcol, wg, wg, bg, bg],
        out_shape=[half, half, jax.ShapeDtypeStruct((FFN_CONV, FFN_D_FF), F32), jax.ShapeDtypeStruct((FFN_CONV, FFN_D_FF), F32),
                   jax.ShapeDtypeStruct((1, FFN_D_FF), F32), jax.ShapeDtypeStruct((1, FFN_D_FF), F32)],
        compiler_params=_cp("parallel"),
    )(proj, proj, w, w, b2, b2, dact)


SSD_ROWS = 128
DT_COL = (SSD_D_INNER + SSD_CONV_DIM) // LANES


def _head_expand():
    h = np.arange(LANES)[:, None]
    col = np.arange(SSD_D_INNER)[None, :]
    return jnp.asarray((col // SSD_HEAD_DIM == h), MXU_DTYPE)


def _chunk_tri(n, lower):
    t = _iota((n, n), 0)
    s = _iota((n, n), 1)
    shift = SSD_CHUNK.bit_length() - 1
    same = jnp.right_shift(t, shift) == jnp.right_shift(s, shift)
    tri = (s <= t) if lower else (s >= t)
    return jnp.where(same & tri, 1.0, 0.0).astype(MXU_DTYPE)


def _softplus(x):
    return jnp.maximum(x, 0.0) + jnp.log(1.0 + jnp.exp(-jnp.abs(x)))


def _ssd_dt_fwd(proj, act, dt_bias, a_neg, expand, name):
    t = proj.shape[0]
    tm = min(SSD_ROWS, t)

    def body(raw_ref, xs_ref, bias_ref, a_ref, e_ref, xdt_ref, dt_ref, acum_ref):
        lane = _iota((tm, LANES), 1)
        dt = jnp.where(lane < SSD_HEADS, _softplus(raw_ref[...] + bias_ref[...]), 0.0)
        dt_ref[...] = dt
        xdt_ref[...] = xs_ref[...] * _mm_exact_rhs(dt, e_ref[...])
        acum_ref[...] = _mm_exact_lhs(_chunk_tri(tm, True), a_ref[...] * dt)

    vec = pl.BlockSpec((1, LANES), lambda i: (0, 0))
    return pl.pallas_call(
        body, name=name, grid=(t // tm,),
        in_specs=[pl.BlockSpec((tm, LANES), lambda i: (i, DT_COL)), pl.BlockSpec((tm, SSD_D_INNER), lambda i: (i, 0)),
                  vec, vec, pl.BlockSpec((LANES, SSD_D_INNER), lambda i: (0, 0))],
        out_specs=[pl.BlockSpec((tm, SSD_D_INNER), lambda i: (i, 0)), pl.BlockSpec((tm, LANES), lambda i: (i, 0)),
                   pl.BlockSpec((tm, LANES), lambda i: (i, 0))],
        out_shape=[jax.ShapeDtypeStruct((t, SSD_D_INNER), F32), jax.ShapeDtypeStruct((t, LANES), F32),
                   jax.ShapeDtypeStruct((t, LANES), F32)],
        compiler_params=_cp("parallel"),
    )(proj, act, dt_bias, a_neg, expand)


def _ssd_dt_bwd(proj, act, dt, dxdt, dyy, dacum, dt_bias, a_neg, d_exp, expand, expand_t, name):
    t = proj.shape[0]
    tm = min(SSD_ROWS, t)

    def body(raw_ref, xs_ref, dt_ref, dxdt_ref, dyy_ref, dac_ref, bias_ref, a_ref, dsk_ref, e_ref, et_ref,
             dxs_ref, draw_ref, da_ref, dbias_ref, dd_ref):
        @pl.when(pl.program_id(0) == 0)
        def _():
            da_ref[...] = jnp.zeros_like(da_ref)
            dbias_ref[...] = jnp.zeros_like(dbias_ref)
            dd_ref[...] = jnp.zeros_like(dd_ref)

        lane = _iota((tm, LANES), 1)
        xs, dt, dxdt, dyy = xs_ref[...], dt_ref[...], dxdt_ref[...], dyy_ref[...]
        dxs_ref[...] = dxdt * _mm_exact_rhs(dt, e_ref[...]) + dsk_ref[...] * dyy
        dd_ref[...] += jnp.sum(dyy * xs, axis=0, keepdims=True)
        ddt = _mm_exact_rhs(dxdt * xs, et_ref[...])
        da = _mm_exact_lhs(_chunk_tri(tm, False), dac_ref[...])
        ddt = ddt + da * a_ref[...]
        da_ref[...] += jnp.sum(da * dt, axis=0, keepdims=True)
        draw = jnp.where(lane < SSD_HEADS, ddt * _sigmoid(raw_ref[...] + bias_ref[...]), 0.0)
        dbias_ref[...] += jnp.sum(draw, axis=0, keepdims=True)
        draw_ref[...] = draw.astype(draw_ref.dtype)

    wide = pl.BlockSpec((tm, SSD_D_INNER), lambda i: (i, 0))
    thin = pl.BlockSpec((tm, LANES), lambda i: (i, 0))
    vec = pl.BlockSpec((1, LANES), lambda i: (0, 0))
    wvec = pl.BlockSpec((1, SSD_D_INNER), lambda i: (0, 0))
    return pl.pallas_call(
        body, name=name, grid=(t // tm,),
        in_specs=[pl.BlockSpec((tm, LANES), lambda i: (i, DT_COL)), wide, thin, wide, wide, thin, vec, vec, wvec,
                  pl.BlockSpec((LANES, SSD_D_INNER), lambda i: (0, 0)), pl.BlockSpec((SSD_D_INNER, LANES), lambda i: (0, 0))],
        out_specs=[wide, thin, vec, vec, wvec],
        out_shape=[jax.ShapeDtypeStruct((t, SSD_D_INNER), F32), jax.ShapeDtypeStruct((t, LANES), MXU_DTYPE),
                   jax.ShapeDtypeStruct((1, LANES), F32), jax.ShapeDtypeStruct((1, LANES), F32),
                   jax.ShapeDtypeStruct((1, SSD_D_INNER), F32)],
        compiler_params=_cp("arbitrary"),
    )(proj, act, dt, dxdt, dyy, dacum, dt_bias, a_neg, d_exp, expand, expand_t)


SSD_PAIR = 2 * SSD_HEAD_DIM
HEADS_PER_GROUP = SSD_HEADS // SSD_GROUPS
GROUP_COLS = HEADS_PER_GROUP * SSD_HEAD_DIM
B_COL0 = SSD_D_INNER // SSD_STATE
C_COL0 = (SSD_D_INNER + SSD_GROUPS * SSD_STATE) // SSD_STATE


def _pair_cols(vals, h0, lo_mask):
    return jnp.where(lo_mask, vals[:, h0:h0 + 1], vals[:, h0 + 1:h0 + 2])


SCAN_GROUPS = 2
SCAN_STEPS = SSD_GROUPS // SCAN_GROUPS


def _scan_step_is(g, c):
    return (pl.program_id(0) == g) & (pl.program_id(1) == c)


def _ssd_scan_fwd(xdt, act, acum_g, acum_gt, name, carried=None):
    t = xdt.shape[0]
    nc = t // SSD_CHUNK
    ln = SSD_CHUNK
    c_in_specs, c_in, c_out_specs, c_out = _carried_specs(carried)

    def body(*refs):
        if carried is None:
            x_ref, b_ref, c_ref, ac_ref, act_ref, y_ref, sst_ref, state = refs
            comm_refs = None
        else:
            x_ref, b_ref, c_ref, ac_ref, act_ref, src_ref, y_ref, sst_ref, land_ref, state, send_sems, recv_sems, local_sem = refs
            comm_refs = (src_ref, land_ref, send_sems, recv_sems, local_sem)
        finish = _carried_hooks(carried, comm_refs, _scan_step_is(0, 0), _scan_step_is(SCAN_STEPS - 1, nc - 1))

        @pl.when(pl.program_id(1) == 0)
        def _():
            state[...] = jnp.zeros_like(state)

        causal = _iota((ln, ln), 1) <= _iota((ln, ln), 0)
        lo_mask = _iota((ln, SSD_PAIR), 1) < SSD_HEAD_DIM
        lo_rows = _iota((SSD_PAIR, SSD_STATE), 0) < SSD_HEAD_DIM
        for gg in range(SCAN_GROUPS):
            sst_ref[0, gg] = state[gg * GROUP_COLS:(gg + 1) * GROUP_COLS, :]
            bm = b_ref[:, gg * SSD_STATE:(gg + 1) * SSD_STATE].astype(MXU_DTYPE)
            cm = c_ref[:, gg * SSD_STATE:(gg + 1) * SSD_STATE].astype(MXU_DTYPE)
            cb = _mm_nt(cm, bm)
            ac, act_ = ac_ref[gg], act_ref[gg]
            last = ac[ln - 1:ln, :]
            e_ac = jnp.exp(ac)
            w_all = jnp.exp(last - ac)
            e_last = jnp.exp(last)
            for pr in range(2):
                first = gg * GROUP_COLS + pr * SSD_PAIR
                cols = slice(first, first + SSD_PAIR)
                xp = x_ref[:, cols]
                sp = state[cols, :]
                ydiag = jnp.zeros((ln, SSD_PAIR), F32)
                for hh in range(2):
                    h = 2 * pr + hh
                    seg = ac[:, h:h + 1] - act_[h:h + 1, :]
                    dec = jnp.exp(jnp.where(causal, seg, -1e30))
                    mask = lo_mask if hh == 0 else jnp.logical_not(lo_mask)
                    ydiag = ydiag + _mm((cb * dec).astype(MXU_DTYPE), jnp.where(mask, xp, 0.0).astype(MXU_DTYPE))
                yoff = _mm_nt(cm, sp.astype(MXU_DTYPE)) * _pair_cols(e_ac, 2 * pr, lo_mask)
                y_ref[:, cols] = ydiag + yoff
                xw = (xp * _pair_cols(w_all, 2 * pr, lo_mask)).astype(MXU_DTYPE)
                el = jnp.where(lo_rows, e_last[:, 2 * pr:2 * pr + 1], e_last[:, 2 * pr + 1:2 * pr + 2])
                state[cols, :] = sp * el + _mm_tn(xw, bm)
        finish()

    sg = SCAN_GROUPS
    return pl.pallas_call(
        body, name=name, grid=(SCAN_STEPS, nc),
        in_specs=[pl.BlockSpec((ln, sg * GROUP_COLS), lambda g, c: (c, g)),
                  pl.BlockSpec((ln, sg * SSD_STATE), lambda g, c: (c, B_COL0 // sg + g)),
                  pl.BlockSpec((ln, sg * SSD_STATE), lambda g, c: (c, C_COL0 // sg + g)),
                  pl.BlockSpec((sg, ln, HEADS_PER_GROUP), lambda g, c: (g, c, 0)),
                  pl.BlockSpec((sg, HEADS_PER_GROUP, ln), lambda g, c: (g, 0, c))] + c_in_specs,
        out_specs=[pl.BlockSpec((ln, sg * GROUP_COLS), lambda g, c: (c, g)),
                   pl.BlockSpec((1, sg, GROUP_COLS, SSD_STATE), lambda g, c: (c, g, 0, 0))] + c_out_specs,
        out_shape=[jax.ShapeDtypeStruct((t, SSD_D_INNER), F32),
                   jax.ShapeDtypeStruct((nc, SSD_GROUPS, GROUP_COLS, SSD_STATE), F32)] + c_out,
        scratch_shapes=[pltpu.VMEM((sg * GROUP_COLS, SSD_STATE), F32)] + (EXCHANGE_SCRATCH if carried else []),
        compiler_params=_cp("arbitrary", "arbitrary"),
    )(xdt, act, act, acum_g, acum_gt, *c_in)


def _ssd_scan_bwd(xdt, act, acum_g, acum_gt, states, dy, name, carried=None):
    t = xdt.shape[0]
    nc = t // SSD_CHUNK
    ln = SSD_CHUNK
    c_in_specs, c_in, c_out_specs, c_out = _carried_specs(carried)

    def body(*refs):
        if carried is None:
            x_ref, b_ref, c_ref, ac_ref, act_ref, sst_ref, dy_ref, dx_ref, db_ref, dc_ref, dacol_ref, darow_ref, dstate = refs
            comm_refs = None
        else:
            (x_ref, b_ref, c_ref, ac_ref, act_ref, sst_ref, dy_ref, src_ref, dx_ref, db_ref, dc_ref, dacol_ref, darow_ref,
             land_ref, dstate, send_sems, recv_sems, local_sem) = refs
            comm_refs = (src_ref, land_ref, send_sems, recv_sems, local_sem)
        finish = _carried_hooks(carried, comm_refs, _scan_step_is(0, 0), _scan_step_is(SCAN_STEPS - 1, nc - 1))

        @pl.when(pl.program_id(1) == 0)
        def _():
            dstate[...] = jnp.zeros_like(dstate)

        for gg in range(SCAN_GROUPS):
            group_bwd(gg, x_ref, b_ref, c_ref, ac_ref, act_ref, sst_ref, dy_ref, dx_ref, db_ref, dc_ref, dacol_ref, darow_ref, dstate)
        finish()

    def group_bwd(gg, x_ref, b_ref, c_ref, ac_ref, act_ref, sst_ref, dy_ref, dx_ref, db_ref, dc_ref, dacol_ref, darow_ref, dstate):
        bc_cols = slice(gg * SSD_STATE, (gg + 1) * SSD_STATE)
        bm = b_ref[:, bc_cols].astype(MXU_DTYPE)
        cm = c_ref[:, bc_cols].astype(MXU_DTYPE)
        cb = _mm_nt(cm, bm)
        ac, act_ = ac_ref[gg], act_ref[gg]
        causal = _iota((ln, ln), 1) <= _iota((ln, ln), 0)
        lo_mask = _iota((ln, SSD_PAIR), 1) < SSD_HEAD_DIM
        lo_rows = _iota((SSD_PAIR, SSD_STATE), 0) < SSD_HEAD_DIM
        lane4 = _iota((ln, HEADS_PER_GROUP), 1)
        sub4 = _iota((HEADS_PER_GROUP, ln), 0)
        is_last = _iota((ln, 1), 0) == ln - 1
        last = ac[ln - 1:ln, :]
        e_ac = jnp.exp(ac)
        w_all = jnp.exp(last - ac)
        e_last = jnp.exp(last)
        dcb = jnp.zeros((ln, ln), F32)
        dc_acc = jnp.zeros((ln, SSD_STATE), F32)
        db_acc = jnp.zeros((ln, SSD_STATE), F32)
        dacol = jnp.zeros((ln, HEADS_PER_GROUP), F32)
        darow = jnp.zeros((HEADS_PER_GROUP, ln), F32)
        for pr in range(2):
            in_group = slice(pr * SSD_PAIR, (pr + 1) * SSD_PAIR)
            cols = slice(gg * GROUP_COLS + pr * SSD_PAIR, gg * GROUP_COLS + (pr + 1) * SSD_PAIR)
            xp = x_ref[:, cols]
            dyp = dy_ref[:, cols]
            sp = sst_ref[0, gg, in_group, :]
            dsp = dstate[cols, :]
            ea = _pair_cols(e_ac, 2 * pr, lo_mask)
            w = _pair_cols(w_all, 2 * pr, lo_mask)
            dye = (dyp * ea).astype(MXU_DTYPE)
            dx_state = w * _mm_nt(bm, dsp.astype(MXU_DTYPE))
            yoff = _mm_nt(cm, sp.astype(MXU_DTYPE)) * ea
            dxp = dx_state
            for hh in range(2):
                h = 2 * pr + hh
                mask = lo_mask if hh == 0 else jnp.logical_not(lo_mask)
                rmask = lo_rows if hh == 0 else jnp.logical_not(lo_rows)
                seg = ac[:, h:h + 1] - act_[h:h + 1, :]
                dec = jnp.exp(jnp.where(causal, seg, -1e30))
                m = cb * dec
                dym = jnp.where(mask, dyp, 0.0).astype(MXU_DTYPE)
                xm = jnp.where(mask, xp, 0.0).astype(MXU_DTYPE)
                g = _mm_nt(dym, xm)
                dxp = dxp + _mm_tn(m.astype(MXU_DTYPE), dym)
                dcb = dcb + dec * g
                mg = m * g
                rs = jnp.sum(mg, axis=1, keepdims=True)
                cs = jnp.sum(mg, axis=0, keepdims=True)
                t_off = jnp.sum(jnp.where(mask, dyp * yoff, 0.0), axis=1, keepdims=True)
                q = jnp.sum(jnp.where(mask, xp * dx_state, 0.0), axis=1, keepdims=True)
                qsum = jnp.sum(q, axis=0, keepdims=True)
                ds_s = jnp.sum(jnp.sum(jnp.where(rmask, dsp * sp, 0.0), axis=1, keepdims=True), axis=0, keepdims=True)
                extra = qsum + e_last[:, h:h + 1] * ds_s
                col = rs + t_off - q + jnp.where(is_last, extra, 0.0)
                dacol = jnp.where(lane4 == h, col, dacol)
                darow = jnp.where(sub4 == h, -cs, darow)
            dx_ref[:, cols] = dxp
            dc_acc = dc_acc + _mm(dye, sp.astype(MXU_DTYPE))
            db_acc = db_acc + _mm((xp * w).astype(MXU_DTYPE), dsp.astype(MXU_DTYPE))
            el = jnp.where(lo_rows, e_last[:, 2 * pr:2 * pr + 1], e_last[:, 2 * pr + 1:2 * pr + 2])
            dstate[cols, :] = dsp * el + _mm_tn(dye, cm)
        dcbm = dcb.astype(MXU_DTYPE)
        dc_ref[:, bc_cols] = _mm(dcbm, bm) + dc_acc
        db_ref[:, bc_cols] = _mm_tn(dcbm, cm) + db_acc
        dacol_ref[gg] = dacol
        darow_ref[gg] = darow

    def rev(c):
        return nc - 1 - c

    sg = SCAN_GROUPS
    grp = pl.BlockSpec((ln, sg * GROUP_COLS), lambda g, c: (rev(c), g))
    return pl.pallas_call(
        body, name=name, grid=(SCAN_STEPS, nc),
        in_specs=[grp,
                  pl.BlockSpec((ln, sg * SSD_STATE), lambda g, c: (rev(c), B_COL0 // sg + g)),
                  pl.BlockSpec((ln, sg * SSD_STATE), lambda g, c: (rev(c), C_COL0 // sg + g)),
                  pl.BlockSpec((sg, ln, HEADS_PER_GROUP), lambda g, c: (g, rev(c), 0)),
                  pl.BlockSpec((sg, HEADS_PER_GROUP, ln), lambda g, c: (g, 0, rev(c))),
                  pl.BlockSpec((1, sg, GROUP_COLS, SSD_STATE), lambda g, c: (rev(c), g, 0, 0)),
                  grp] + c_in_specs,
        out_specs=[grp,
                   pl.BlockSpec((ln, sg * SSD_STATE), lambda g, c: (rev(c), g)),
                   pl.BlockSpec((ln, sg * SSD_STATE), lambda g, c: (rev(c), g)),
                   pl.BlockSpec((sg, ln, HEADS_PER_GROUP), lambda g, c: (g, rev(c), 0)),
                   pl.BlockSpec((sg, HEADS_PER_GROUP, ln), lambda g, c: (g, 0, rev(c)))] + c_out_specs,
        out_shape=[jax.ShapeDtypeStruct((t, SSD_D_INNER), F32),
                   jax.ShapeDtypeStruct((t, SSD_GROUPS * SSD_STATE), F32),
                   jax.ShapeDtypeStruct((t, SSD_GROUPS * SSD_STATE), F32),
                   jax.ShapeDtypeStruct((SSD_GROUPS, t, HEADS_PER_GROUP), F32),
                   jax.ShapeDtypeStruct((SSD_GROUPS, HEADS_PER_GROUP, t), F32)] + c_out,
        scratch_shapes=[pltpu.VMEM((sg * GROUP_COLS, SSD_STATE), F32)] + (EXCHANGE_SCRATCH if carried else []),
        compiler_params=_cp("arbitrary", "arbitrary"),
    )(xdt, act, act, acum_g, acum_gt, states, dy, *c_in)


GN_ROWS = 128


def _gated_norm_parts(y_ref, xs_ref, z_ref, dsk_ref):
    yy = y_ref[...] + dsk_ref[...] * xs_ref[...]
    z = z_ref[...]
    sz = _sigmoid(z)
    silu = z * sz
    u = yy * silu
    r = lax.rsqrt(jnp.mean(u * u, axis=-1, keepdims=True) + NORM_EPS)
    return yy, z, sz, silu, u, r


def _gated_norm_fwd(y, act, proj, d_exp, g, name):
    t = y.shape[0]
    tm = min(GN_ROWS, t)

    def body(y_ref, xs_ref, z_ref, dsk_ref, g_ref, o_ref):
        _, _, _, _, u, r = _gated_norm_parts(y_ref, xs_ref, z_ref, dsk_ref)
        o_ref[...] = (u * r * g_ref[...]).astype(o_ref.dtype)

    wide = pl.BlockSpec((tm, SSD_D_INNER), lambda i: (i, 0))
    wvec = pl.BlockSpec((1, SSD_D_INNER), lambda i: (0, 0))
    return pl.pallas_call(
        body, name=name, grid=(t // tm,), in_specs=[wide, wide, wide, wvec, wvec], out_specs=wide,
        out_shape=jax.ShapeDtypeStruct((t, SSD_D_INNER), MXU_DTYPE), compiler_params=_cp("parallel"),
    )(y, act, proj, d_exp, g.reshape(1, -1))


def _gated_norm_bwd(y, act, proj, d_exp, g, dn, name):
    t = y.shape[0]
    tm = min(GN_ROWS, t)

    def body(y_ref, xs_ref, z_ref, dsk_ref, g_ref, dn_ref, dyy_ref, dz_ref, dg_ref):
        @pl.when(pl.program_id(0) == 0)
        def _():
            dg_ref[...] = jnp.zeros_like(dg_ref)

        yy, z, sz, silu, u, r = _gated_norm_parts(y_ref, xs_ref, z_ref, dsk_ref)
        un = u * r
        dn = dn_ref[...]
        v = dn * g_ref[...]
        du = r * (v - un * jnp.mean(v * un, axis=-1, keepdims=True))
        dg_ref[...] += jnp.sum(dn * un, axis=0, keepdims=True)
        dyy_ref[...] = du * silu
        dz_ref[...] = (du * yy * (sz * (1.0 + z * (1.0 - sz)))).astype(dz_ref.dtype)

    wide = pl.BlockSpec((tm, SSD_D_INNER), lambda i: (i, 0))
    wvec = pl.BlockSpec((1, SSD_D_INNER), lambda i: (0, 0))
    return pl.pallas_call(
        body, name=name, grid=(t // tm,), in_specs=[wide, wide, wide, wvec, wvec, wide], out_specs=[wide, wide, wvec],
        out_shape=[jax.ShapeDtypeStruct((t, SSD_D_INNER), F32), jax.ShapeDtypeStruct((t, SSD_D_INNER), MXU_DTYPE),
                   jax.ShapeDtypeStruct((1, SSD_D_INNER), F32)],
        compiler_params=_cp("arbitrary"),
    )(y, act, proj, d_exp, g.reshape(1, -1), dn)


SB_PAIRS = SB_HEADS // 2


def _kv_rows(j, bt, nt=1):
    return pl.ds(pl.multiple_of(j * bt, bt), nt * bt)


def _sb_tile_masks(bt):
    lane = _iota((bt, bt), 1)
    rowi = _iota((bt, bt), 0)
    return lane < rowi, (rowi >= lane).astype(MXU_DTYPE), (rowi <= lane).astype(MXU_DTYPE)


def _sb_scaled_heads(pair, scale):
    lane = _iota(pair.shape, 1)
    val = pair.astype(F32) * scale
    return [jnp.where(lane < SB_HEAD_DIM, val, 0.0).astype(pair.dtype), jnp.where(lane >= SB_HEAD_DIM, val, 0.0).astype(pair.dtype)]


def _sb_logits(qs, kb, bt, strict):
    nt = kb.shape[0] // bt
    full = [_mm_nt(q_head, kb) for q_head in qs]
    xs, nlfs = [], []
    for x in full:
        nlf = jnp.maximum(x, 0.0) + jnp.log(1.0 + jnp.exp(-jnp.abs(x)))
        xs.append([x[:, tt * bt:(tt + 1) * bt] for tt in range(nt)])
        tiles = [nlf[:, tt * bt:(tt + 1) * bt] for tt in range(nt)]
        if strict is not None:
            tiles[-1] = jnp.where(strict, tiles[-1], 0.0)
        nlfs.append(tiles)
    return xs, nlfs


def _sb_tails(nlf_tiles, from_j):
    tails, run = [None] * len(nlf_tiles), None
    for tt in reversed(range(len(nlf_tiles))):
        tail = _mm_exact_rhs(nlf_tiles[tt], from_j)
        tails[tt] = tail if run is None else tail + run
        run = tails[tt][:, 0:1]
    return tails


def _sb_heads(e_tiles, upto_j, pre):
    sums, run = [], pre
    for e in e_tiles:
        sums.append(_mm_exact_rhs(e, upto_j) + run)
        run = sums[-1][:, e.shape[1] - 1:e.shape[1]]
    return sums


def _carried_specs(carried):
    if carried is None:
        return [], [], [], []
    src, per_peer = carried
    rows = src.shape[1:] if per_peer else src.shape
    anywhere = pl.BlockSpec(memory_space=pl.ANY)
    return [anywhere], [src], [anywhere], [jax.ShapeDtypeStruct((N_DEV, *rows), src.dtype)]


def _carried_hooks(carried, comm_refs, first, last):
    if carried is None:
        return lambda: None

    @pl.when(first)
    def _():
        _exchange_start(*comm_refs, per_peer=carried[1])

    def finish():
        @pl.when(last)
        def _():
            _exchange_finish(*comm_refs, per_peer=carried[1])

    return finish


def _sb_attention_fwd(qkv, name, carried=None):
    t = qkv.shape[0]
    bt = min(SB_TILE, t)
    nq = t // bt
    c_in_specs, c_in, c_out_specs, c_out = _carried_specs(carried)

    def body(*refs):
        if carried is None:
            q_ref, k_ref, v_ref, o_ref, acc_ref = refs
            comm_refs = None
        else:
            q_ref, k_ref, v_ref, src_ref, o_ref, land_ref, acc_ref, send_sems, recv_sems, local_sem = refs
            comm_refs = (src_ref, land_ref, send_sems, recv_sems, local_sem)
        i = pl.program_id(1)
        finish = _carried_hooks(carried, comm_refs, (pl.program_id(0) == 0) & (i == 0),
                                (pl.program_id(0) == SB_PAIRS - 1) & (i == nq - 1))
        strict, from_j, _ = _sb_tile_masks(bt)
        qs = _sb_scaled_heads(q_ref[...], SB_SCALE)
        acc_ref[...] = jnp.zeros_like(acc_ref)

        def block(j, nt, carries, diag):
            rows = _kv_rows(j, bt, nt)
            kb, vb = k_ref[rows, :], v_ref[rows, :]
            xs, nlfs = _sb_logits(qs, kb, bt, strict if diag else None)
            tails = [_sb_tails(nlfs[hh], from_j) for hh in range(2)]
            for hh in range(2):
                ws = [jnp.exp(xs[hh][tt] - tails[hh][tt] - carries[hh]) for tt in range(nt)]
                if diag:
                    ws[-1] = jnp.where(strict, ws[-1], 0.0)
                acc_ref[hh] += _mm(jnp.concatenate([w.astype(MXU_DTYPE) for w in ws], axis=1), vb)
            return tuple(carries[hh] + tails[hh][0][:, 0:1] for hh in range(2))

        zero = jnp.zeros((bt, 1), F32)
        carries = block(i, 1, (zero, zero), True)
        carries = lax.fori_loop(0, i // 2, lambda it, cr: block(i - 2 - 2 * it, 2, cr, False), carries)

        @pl.when(i % 2 == 1)
        def _():
            block(0, 1, carries, False)

        low = _iota((bt, 2 * SB_HEAD_DIM), 1) < SB_HEAD_DIM
        o_ref[...] = jnp.where(low, acc_ref[0], acc_ref[1]).astype(o_ref.dtype)
        finish()

    lanes = 2 * SB_HEAD_DIM
    res = pl.pallas_call(
        body, name=name, grid=(SB_PAIRS, nq),
        in_specs=[pl.BlockSpec((bt, lanes), lambda p, i: (i, p)),
                  pl.BlockSpec((t, lanes), lambda p, i: (0, SB_PAIRS + p)),
                  pl.BlockSpec((t, lanes), lambda p, i: (0, 2 * SB_PAIRS + p))] + c_in_specs,
        out_specs=[pl.BlockSpec((bt, lanes), lambda p, i: (i, p))] + c_out_specs,
        out_shape=[jax.ShapeDtypeStruct((t, D_MODEL), MXU_DTYPE)] + c_out,
        scratch_shapes=[pltpu.VMEM((2, bt, lanes), F32)] + (EXCHANGE_SCRATCH if carried else []),
        compiler_params=_cp("arbitrary", "arbitrary"),
    )(qkv, qkv, qkv, *c_in)
    return res[0] if carried is None else res


def _sb_attention_bwd(qkv, do, name, carried=None):
    t = qkv.shape[0]
    bt = min(SB_TILE, t)
    nq = t // bt
    lanes = 2 * SB_HEAD_DIM
    c_in_specs, c_in, c_out_specs, c_out = _carried_specs(carried)

    def body(*refs):
        if carried is None:
            q_ref, k_ref, v_ref, do_ref, dq_ref, dk_ref, dv_ref, sbuf, ebuf, dq_acc, dk_acc, dv_acc = refs
            comm_refs = None
        else:
            (q_ref, k_ref, v_ref, do_ref, src_ref, dq_ref, dk_ref, dv_ref, land_ref,
             sbuf, ebuf, dq_acc, dk_acc, dv_acc, send_sems, recv_sems, local_sem) = refs
            comm_refs = (src_ref, land_ref, send_sems, recv_sems, local_sem)
        i = pl.program_id(1)
        finish = _carried_hooks(carried, comm_refs, (pl.program_id(0) == 0) & (i == 0),
                                (pl.program_id(0) == SB_PAIRS - 1) & (i == nq - 1))

        @pl.when(i == 0)
        def _():
            dk_acc[...] = jnp.zeros_like(dk_acc)
            dv_acc[...] = jnp.zeros_like(dv_acc)

        strict, from_j, upto_j = _sb_tile_masks(bt)
        qs = _sb_scaled_heads(q_ref[...], SB_SCALE)
        dos = _sb_scaled_heads(do_ref[...], 1.0)
        q_both = jnp.concatenate(qs, axis=0)
        do_both = jnp.concatenate(dos, axis=0)
        dq_acc[...] = jnp.zeros_like(dq_acc)

        def pass1(j, nt, carries, diag):
            rows = _kv_rows(j, bt, nt)
            kb, vb = k_ref[rows, :], v_ref[rows, :]
            xs, nlfs = _sb_logits(qs, kb, bt, strict if diag else None)
            dws = [_mm_nt(dos[hh], vb) for hh in range(2)]
            tails = [_sb_tails(nlfs[hh], from_j) for hh in range(2)]
            wcat = []
            for hh in range(2):
                ws = [jnp.exp(xs[hh][tt] - tails[hh][tt] - carries[hh]) for tt in range(nt)]
                if diag:
                    ws[-1] = jnp.where(strict, ws[-1], 0.0)
                w_all = jnp.concatenate(ws, axis=1)
                sbuf[hh, :, rows] = jnp.exp(jnp.concatenate([xs[hh][tt] - nlfs[hh][tt] for tt in range(nt)], axis=1))
                ebuf[hh, :, rows] = w_all * dws[hh]
                wcat.append(w_all.astype(MXU_DTYPE))
            dv_acc[rows, :] += _mm_tn(jnp.concatenate(wcat, axis=0), do_both)
            return tuple(carries[hh] + tails[hh][0][:, 0:1] for hh in range(2))

        zero = jnp.zeros((bt, 1), F32)
        carries = pass1(i, 1, (zero, zero), True)
        carries = lax.fori_loop(0, i // 2, lambda it, cr: pass1(i - 2 - 2 * it, 2, cr, False), carries)

        @pl.when(i % 2 == 1)
        def _():
            pass1(0, 1, carries, False)

        def pass2(j, nt, pres, diag):
            rows = _kv_rows(j, bt, nt)
            kb = k_ref[rows, :]
            sums = [_sb_heads([ebuf[hh, :, _kv_rows(j + tt, bt)] for tt in range(nt)], upto_j, pres[hh]) for hh in range(2)]
            dxm = []
            for hh in range(2):
                dxs = [ebuf[hh, :, _kv_rows(j + tt, bt)] - sbuf[hh, :, _kv_rows(j + tt, bt)] * sums[hh][tt] for tt in range(nt)]
                if diag:
                    dxs[-1] = jnp.where(strict, dxs[-1], 0.0)
                dxm.append(jnp.concatenate(dxs, axis=1).astype(MXU_DTYPE))
                dq_acc[hh] += _mm(dxm[hh], kb)
            dk_acc[rows, :] += _mm_tn(jnp.concatenate(dxm, axis=0), q_both)
            return tuple(sums[hh][-1][:, bt - 1:bt] for hh in range(2))

        pres = lax.fori_loop(0, i // 2, lambda it, pr: pass2(2 * it, 2, pr, False), (zero, zero))

        @pl.when(i % 2 == 0)
        def _():
            pass2(i, 1, pres, True)

        @pl.when(i % 2 == 1)
        def _():
            pass2(i - 1, 2, pres, True)

        low = _iota((bt, lanes), 1) < SB_HEAD_DIM
        dq_ref[...] = (jnp.where(low, dq_acc[0], dq_acc[1]) * SB_SCALE).astype(dq_ref.dtype)

        @pl.when(i == nq - 1)
        def _():
            dk_ref[...] = dk_acc[...].astype(dk_ref.dtype)
            dv_ref[...] = dv_acc[...].astype(dv_ref.dtype)

        finish()

    blk = pl.BlockSpec((bt, lanes), lambda p, i: (i, p))
    whole = pl.BlockSpec((t, lanes), lambda p, i: (0, p))
    out = jax.ShapeDtypeStruct((t, D_MODEL), MXU_DTYPE)
    return pl.pallas_call(
        body, name=name, grid=(SB_PAIRS, nq),
        in_specs=[blk, pl.BlockSpec((t, lanes), lambda p, i: (0, SB_PAIRS + p)),
                  pl.BlockSpec((t, lanes), lambda p, i: (0, 2 * SB_PAIRS + p)), blk] + c_in_specs,
        out_specs=[blk, whole, whole] + c_out_specs, out_shape=[out, out, out] + c_out,
        scratch_shapes=[pltpu.VMEM((2, bt, t), F32), pltpu.VMEM((2, bt, t), F32), pltpu.VMEM((2, bt, lanes), F32),
                        pltpu.VMEM((t, lanes), F32), pltpu.VMEM((t, lanes), F32)] + (EXCHANGE_SCRATCH if carried else []),
        compiler_params=_cp("arbitrary", "arbitrary"),
    )(qkv, qkv, qkv, do, *c_in)


def _add_pair(a, b, name):
    s, r, c = a.shape
    tm = _row_tile(r)

    def body(a_ref, b_ref, o_ref):
        o_ref[...] = (a_ref[...].astype(F32) + b_ref[...].astype(F32)).astype(o_ref.dtype)

    blk = pl.BlockSpec((1, tm, c), lambda q, i: (q, i, 0))
    return pl.pallas_call(body, name=name, grid=(s, r // tm), in_specs=[blk, blk], out_specs=blk,
                          out_shape=jax.ShapeDtypeStruct(a.shape, a.dtype), compiler_params=_cp("parallel", "parallel"))(a, b)


def _sum_slots(gslots, name):
    s, r, c = gslots.shape

    def body(g_ref, o_ref):
        g = g_ref[0].astype(F32)
        for q in range(1, s):
            g = g + g_ref[q].astype(F32)
        o_ref[...] = g

    return pl.pallas_call(
        body, name=name, grid=(c // LANES,),
        in_specs=[pl.BlockSpec((s, r, LANES), lambda j: (0, 0, j))], out_specs=pl.BlockSpec((r, LANES), lambda j: (0, j)),
        out_shape=jax.ShapeDtypeStruct((r, c), F32), compiler_params=_cp("parallel"),
    )(gslots)


def _adamw(gslots, w, m, v, name):
    s, r, c = gslots.shape
    tm = _row_tile(r)
    assert w.shape == (r, c), (w.shape, gslots.shape)
    c1 = 1.0 - ADAM_B1 ** ADAM_STEP
    c2 = 1.0 - ADAM_B2 ** ADAM_STEP

    def body(g_ref, w_ref, m_ref, v_ref, go_ref, d_ref, mo_ref, vo_ref):
        g = g_ref[0].astype(F32)
        for q in range(1, s):
            g = g + g_ref[q].astype(F32)
        mn = ADAM_B1 * m_ref[...] + (1.0 - ADAM_B1) * g
        vn = ADAM_B2 * v_ref[...] + (1.0 - ADAM_B2) * (g * g)
        go_ref[...] = g
        mo_ref[...] = mn
        vo_ref[...] = vn
        d_ref[...] = -ADAM_LR * ((mn / c1) / (jnp.sqrt(vn / c2) + ADAM_EPS) + ADAM_WD * w_ref[...])

    row = pl.BlockSpec((tm, c), lambda i: (i, 0))
    out = jax.ShapeDtypeStruct((r, c), F32)
    return pl.pallas_call(
        body, name=name, grid=(r // tm,),
        in_specs=[pl.BlockSpec((s, tm, c), lambda i: (0, i, 0)), row, row, row],
        out_specs=[row, row, row, row], out_shape=[out, out, out, out], compiler_params=_cp("parallel"),
    )(gslots, w, m, v)


def _rows(a):
    flat = a.reshape(-1)
    pad = (-flat.shape[0]) % PACK_W
    if pad:
        flat = jnp.concatenate([flat, jnp.zeros((pad,), flat.dtype)])
    return flat.reshape(-1, PACK_W)


def _pack(arrays, row_multiple):
    parts, layout, off = [], [], 0
    for a in arrays:
        rw = _rows(a)
        parts.append(rw)
        layout.append((off, rw.shape[0], a.shape))
        off += rw.shape[0]
    pad = (-off) % row_multiple
    if pad:
        parts.append(jnp.zeros((pad, PACK_W), parts[0].dtype))
    return jnp.concatenate(parts, axis=0), layout


def _unpack(packed, layout):
    out = []
    for off, nrows, shape in layout:
        n = int(np.prod(shape))
        out.append(packed[off:off + nrows].reshape(-1)[:n].reshape(shape))
    return out


def _shard_as_rows(name, shard):
    if name in COL_SHARDED:
        shard = shard.transpose(0, 2, 1)
    return shard.reshape(-1, PACK_W)


def _rows_as_shard(name, rows, shape):
    if name in COL_SHARDED:
        lead, k, ns = shape
        return rows.reshape(lead, ns, k).transpose(0, 2, 1)
    return rows.reshape(shape)


def _row_tile(r):
    return next(tm for tm in (256, 128, 64, 32, 16, 8) if r % tm == 0)


def _ssd_consts(dt_bias, a_log, d_skip):
    pad = LANES - SSD_HEADS
    bias = jnp.pad(dt_bias, (0, pad)).reshape(1, LANES)
    a_neg = jnp.pad(-jnp.exp(a_log), (0, pad)).reshape(1, LANES)
    d_exp = jnp.repeat(d_skip, SSD_HEAD_DIM).reshape(1, SSD_D_INNER)
    return bias, a_neg, d_exp


def _group_layouts(acum):
    t = acum.shape[0]
    a = acum[:, :SSD_HEADS].reshape(t, SSD_GROUPS, HEADS_PER_GROUP)
    return a.transpose(1, 0, 2), a.transpose(1, 2, 0)


def _ssd_fwd(x, p, carried=None):
    hn = _rmsnorm(x, p["mix_norm"], "rmsnorm_fwd")
    proj = _matmul(hn, p["w_in"], "nt", F32, "ssd_in_fwd", tm=TOKEN_ROWS, tn=896, tk=1024)
    act = _ssd_conv_fwd(proj, p["conv_w"], p["conv_b"], "ssd_conv_fwd")
    bias, a_neg, d_exp = _ssd_consts(p["dt_bias"], p["a_log"], p["d"])
    expand = _head_expand()
    xdt, dt, acum = _ssd_dt_fwd(proj, act, bias, a_neg, expand, "ssd_dt_fwd")
    acum_g, acum_gt = _group_layouts(acum)
    if carried is None:
        (y, states), landed = _ssd_scan_fwd(xdt, act, acum_g, acum_gt, "ssd_scan_fwd"), None
    else:
        y, states, landed = _ssd_scan_fwd(xdt, act, acum_g, acum_gt, "ssd_scan_fwd_carrying_gather", carried)
    yn = _gated_norm_fwd(y, act, proj, d_exp, p["norm"], "ssd_gnorm_fwd")
    x_new = _matmul(yn, p["w_out"], "nn", F32, "ssd_out_fwd", add=x, tm=TOKEN_ROWS, tn=1024, tk=2048)
    saved = dict(x=x, hn=hn, proj=proj, act=act, xdt=xdt, dt=dt, acum_g=acum_g, acum_gt=acum_gt, y=y, states=states, yn=yn)
    return x_new, saved, landed


def _ssd_bwd(dx, p, s, carried_of=None):
    bias, a_neg, d_exp = _ssd_consts(p["dt_bias"], p["a_log"], p["d"])
    expand = _head_expand()
    dyn = _matmul(dx, p["w_out"], "nt", F32, "ssd_out_dgrad", tm=TOKEN_ROWS, tn=1024, tk=1024)
    g_w_out = _matmul(s["yn"], dx, "tn", MXU_DTYPE, "ssd_out_wgrad", tm=1024, tn=1024, tk=TOKEN_ROWS)
    dyy, dz, g_norm = _gated_norm_bwd(s["y"], s["act"], s["proj"], d_exp, p["norm"], dyn, "ssd_gnorm_bwd")
    scan_args = (s["xdt"], s["act"], s["acum_g"], s["acum_gt"], s["states"], dyy)
    if carried_of is None:
        (dxdt, dbm, dcm, dacol, darow), landed = _ssd_scan_bwd(*scan_args, "ssd_scan_bwd"), None
    else:
        dxdt, dbm, dcm, dacol, darow, landed = _ssd_scan_bwd(*scan_args, "ssd_scan_bwd_carrying_grads", carried_of(g_w_out))
    t = dx.shape[0]
    dacum = dacol.transpose(1, 0, 2).reshape(t, SSD_HEADS) + darow.transpose(2, 0, 1).reshape(t, SSD_HEADS)
    dacum = jnp.pad(dacum, ((0, 0), (0, LANES - SSD_HEADS)))
    dxs, draw, g_a, g_bias, g_dexp = _ssd_dt_bwd(s["proj"], s["act"], s["dt"], dxdt, dyy, dacum, bias, a_neg, d_exp,
                                                  expand, expand.T, "ssd_dt_bwd")
    dact = jnp.concatenate([dxs, dbm, dcm], axis=1)
    dxbc, g_conv_w, g_conv_b = _ssd_conv_bwd(s["proj"], p["conv_w"], p["conv_b"], dact, "ssd_conv_bwd")
    dproj = jnp.concatenate([dz, dxbc, draw], axis=1)
    dhn = _matmul(dproj, p["w_in"], "nn", F32, "ssd_in_dgrad", tm=TOKEN_ROWS, tn=1024, tk=896)
    g_w_in = _matmul(dproj, s["hn"], "tn", MXU_DTYPE, "ssd_in_wgrad", tm=896, tn=1024, tk=TOKEN_ROWS)
    dx_new, g_mix = _rmsnorm_bwd(s["x"], p["mix_norm"], dhn, dx, "rmsnorm_bwd")
    grads = dict(w_in=g_w_in[:SSD_IN_DIM], w_out=g_w_out, conv_w=g_conv_w, conv_b=g_conv_b.reshape(-1),
                 dt_bias=g_bias[0, :SSD_HEADS], a_log=(g_a * a_neg)[0, :SSD_HEADS],
                 d=g_dexp.reshape(SSD_HEADS, SSD_HEAD_DIM).sum(axis=1), norm=g_norm.reshape(-1), mix_norm=g_mix.reshape(-1))
    return dx_new, grads, landed


def _sb_fwd(x, p, carried=None):
    hn = _rmsnorm(x, p["mix_norm"], "rmsnorm_fwd")
    qkv = _matmul(hn, p["w_qkv"], "nt", MXU_DTYPE, "sb_qkv_fwd", tm=TOKEN_ROWS, tn=1024, tk=1024)
    if carried is None:
        o, landed = _sb_attention_fwd(qkv, "sb_attn_fwd"), None
    else:
        o, landed = _sb_attention_fwd(qkv, "sb_attn_fwd_carrying_gather", carried)
    x_new = _matmul(o, p["w_out"], "nn", F32, "sb_out_fwd", add=x, tm=TOKEN_ROWS, tn=1024, tk=1024)
    return x_new, dict(x=x, hn=hn, qkv=qkv, o=o), landed


def _sb_bwd(dx, p, s, carried_of=None):
    do = _matmul(dx, p["w_out"], "nt", MXU_DTYPE, "sb_out_dgrad", tm=TOKEN_ROWS, tn=1024, tk=1024)
    g_w_out = _matmul(s["o"], dx, "tn", MXU_DTYPE, "sb_out_wgrad", tm=1024, tn=1024, tk=TOKEN_ROWS)
    if carried_of is None:
        (dq, dk, dv), landed = _sb_attention_bwd(s["qkv"], do, "sb_attn_bwd"), None
    else:
        dq, dk, dv, landed = _sb_attention_bwd(s["qkv"], do, "sb_attn_bwd_carrying_grads", carried_of(g_w_out))
    dqkv = jnp.concatenate([dq, dk, dv], axis=1)
    dhn = _matmul(dqkv, p["w_qkv"], "nn", F32, "sb_qkv_dgrad", tm=TOKEN_ROWS, tn=1024, tk=1024)
    g_w_qkv = _matmul(dqkv, s["hn"], "tn", MXU_DTYPE, "sb_qkv_wgrad", tm=1024, tn=1024, tk=TOKEN_ROWS)
    dx_new, g_mix = _rmsnorm_bwd(s["x"], p["mix_norm"], dhn, dx, "rmsnorm_bwd")
    return dx_new, dict(w_qkv=g_w_qkv, w_out=g_w_out, mix_norm=g_mix.reshape(-1)), landed


def _ffn_fwd(x, p):
    hn = _rmsnorm(x, p["ffn_norm"], "rmsnorm_fwd")
    proj = _matmul(hn, p["w_in"], "nt", F32, "ffn_in_fwd", tm=TOKEN_ROWS, tn=1408, tk=1024)
    act = _ffn_conv_fwd(proj, p["conv_w"], p["conv_b"], "ffn_conv_fwd")
    x_new = _matmul(act, p["w_out"], "nn", F32, "ffn_out_fwd", add=x, tm=TOKEN_ROWS, tn=1024, tk=1408)
    return x_new, dict(x=x, hn=hn, proj=proj, act=act)


def _ffn_bwd(dx, p, s):
    dact = _matmul(dx, p["w_out"], "nt", F32, "ffn_out_dgrad", tm=TOKEN_ROWS, tn=1408, tk=1024)
    g_w_out = _matmul(s["act"], dx, "tn", MXU_DTYPE, "ffn_out_wgrad", tm=1408, tn=1024, tk=TOKEN_ROWS)
    dpg, dpu, dwg, dwu, dbg, dbu = _ffn_conv_bwd(s["proj"], p["conv_w"], p["conv_b"], dact, "ffn_conv_bwd")
    dproj = jnp.concatenate([dpg, dpu], axis=1)
    dhn = _matmul(dproj, p["w_in"], "nn", F32, "ffn_in_dgrad", tm=TOKEN_ROWS, tn=1024, tk=1408)
    g_w_in = _matmul(dproj, s["hn"], "tn", MXU_DTYPE, "ffn_in_wgrad", tm=1408, tn=1024, tk=TOKEN_ROWS)
    dx_new, g_norm = _rmsnorm_bwd(s["x"], p["ffn_norm"], dhn, dx, "rmsnorm_bwd")
    grads = dict(w_in=g_w_in, w_out=g_w_out, conv_w=jnp.concatenate([dwg, dwu], axis=1),
                 conv_b=jnp.concatenate([dbg, dbu], axis=1).reshape(-1), ffn_norm=g_norm.reshape(-1))
    return dx_new, grads


ADD_ROWS = 256
BIG = ["ssd_w_in", "sb_w_qkv", "ffn_w_in", "ssd_w_out", "sb_w_out", "ffn_w_out"]
LAYER_PIECES = [[("ssd_w_in", 0), ("ssd_w_out", 0), ("ffn_w_in", 0), ("ffn_w_out", 0)],
                [("sb_w_qkv", 0), ("sb_w_out", 0), ("ffn_w_in", 1), ("ffn_w_out", 1)],
                [("ssd_w_in", 1), ("ssd_w_out", 1), ("ffn_w_in", 2), ("ffn_w_out", 2)],
                [("sb_w_qkv", 1), ("sb_w_out", 1), ("ffn_w_in", 3), ("ffn_w_out", 3)]]
GRAD_SETS = {3: [("ffn_w_in", 3), ("ffn_w_out", 3), ("sb_w_out", 1)],
             1: [("sb_w_qkv", 1), ("ssd_w_out", 1), ("ffn_w_in", 2), ("ffn_w_out", 2), ("ffn_w_in", 1), ("ffn_w_out", 1),
                 ("sb_w_out", 0), ("ssd_w_in", 1)],
             0: [("sb_w_qkv", 0), ("ffn_w_in", 0), ("ffn_w_out", 0), ("ssd_w_out", 0)],
             "end": [("ssd_w_in", 0)]}
GATHER_SETS = {"early": [("ssd_w_out", 0), ("ssd_w_in", 0)],
               0: [("ffn_w_in", 0), ("ffn_w_out", 0), ("sb_w_qkv", 0), ("sb_w_out", 0)],
               1: [("ffn_w_in", 1), ("ffn_w_out", 1), ("ssd_w_out", 1), ("ffn_w_in", 2), ("ffn_w_out", 2)] + LAYER_PIECES[3]
                  + [("ssd_w_in", 1)]}
COL_SHARDED = {"ssd_w_in": 2, "sb_w_qkv": 2, "ffn_w_in": 4}
CONV = ["ssd_conv_w", "ffn_conv_w"]
SMALL = ["mix_norm", "ffn_norm", "final_norm", "ssd_conv_b", "ssd_dt_bias", "ssd_a_log", "ssd_d", "ssd_norm", "ffn_conv_b"]
WEIGHTS = ["mix_norm", "ffn_norm", "final_norm", "ssd_w_in", "ssd_conv_w", "ssd_conv_b", "ssd_dt_bias", "ssd_a_log", "ssd_d",
           "ssd_norm", "ssd_w_out", "sb_w_qkv", "sb_w_out", "ffn_w_in", "ffn_conv_w", "ffn_conv_b", "ffn_w_out"]


def _step(x, loss_target, w, m, v):
    x = x.reshape(x.shape[-2], x.shape[-1])
    target = loss_target.reshape(x.shape)
    dev = 4 * lax.axis_index("x") + 2 * lax.axis_index("y") + lax.axis_index("c")
    core = lax.axis_index("c")

    shard_rows = {n: _shard_as_rows(n, w[n].astype(MXU_DTYPE)) for n in BIG}
    per_shard = {n: shard_rows[n].shape[0] // w[n].shape[0] for n in BIG}

    def layout(pieces):
        where, off = {}, 0
        for n, l in pieces:
            where[(n, l)] = (off, per_shard[n])
            off += per_shard[n]
        return where

    def pack_pieces(pieces, rows_of):
        return jnp.concatenate([rows_of(piece) for piece in pieces], axis=-2)

    def shard_piece(piece):
        n, l = piece
        return shard_rows[n][l * per_shard[n]:(l + 1) * per_shard[n]]

    full = {}

    def unpack_weights(gathered, where):
        for (n, l), (off, rows) in where.items():
            mat = gathered[:, off:off + rows].reshape(N_DEV * rows, PACK_W)
            if n == "ssd_w_in":
                mat = jnp.pad(mat, ((0, SSD_IN_PAD - SSD_IN_DIM), (0, 0)))
            full[(n, l)] = mat

    unpack_weights(_all_gather(pack_pieces(GATHER_SETS["early"], shard_piece), "gather_weights_early"), layout(GATHER_SETS["early"]))
    conv_pack, conv_layout = _pack([w[n] for n in CONV], 8)
    conv_all = _all_gather(conv_pack, "gather_conv_taps")
    for n, (off, nrows, shape) in zip(CONV, conv_layout):
        parts = [_unpack(conv_all[j], conv_layout)[CONV.index(n)] for j in range(N_DEV)]
        full[n] = jnp.concatenate(parts, axis=-1)

    def ssd_params(j):
        return dict(mix_norm=w["mix_norm"][2 * j], w_in=full[("ssd_w_in", j)], conv_w=full["ssd_conv_w"][j],
                    conv_b=w["ssd_conv_b"][j], dt_bias=w["ssd_dt_bias"][j], a_log=w["ssd_a_log"][j], d=w["ssd_d"][j],
                    norm=w["ssd_norm"][j], w_out=full[("ssd_w_out", j)])

    def sb_params(j):
        return dict(mix_norm=w["mix_norm"][2 * j + 1], w_qkv=full[("sb_w_qkv", j)], w_out=full[("sb_w_out", j)])

    def ffn_params(i):
        return dict(ffn_norm=w["ffn_norm"][i], w_in=full[("ffn_w_in", i)], conv_w=full["ffn_conv_w"][i],
                    conv_b=w["ffn_conv_b"][i], w_out=full[("ffn_w_out", i)])

    saved = []
    for i in range(DEPTH):
        mixer_fwd, params = (_ssd_fwd, ssd_params) if i % 2 == 0 else (_sb_fwd, sb_params)
        if i in GATHER_SETS:
            x, s_mix, arrived = mixer_fwd(x, params(i // 2), carried=(pack_pieces(GATHER_SETS[i], shard_piece), False))
            unpack_weights(arrived, layout(GATHER_SETS[i]))
        else:
            x, s_mix, _ = mixer_fwd(x, params(i // 2))
        x, s_ffn = _ffn_fwd(x, ffn_params(i))
        saved.append((s_mix, s_ffn))
    dx, g_final, loss_part = _final_norm_loss(x, w["final_norm"], target, "final_norm_loss")

    piece_grad = {}

    def grad_piece(piece):
        g = piece_grad[piece]
        return g.reshape(N_DEV, g.shape[0] // N_DEV, PACK_W)

    def carried_set(pieces, own_piece):
        def make(g_w_out):
            piece_grad[own_piece] = g_w_out
            return pack_pieces(pieces, grad_piece), True
        return make

    g_mix, g_ffn, g_ssd, g_sb = [None] * DEPTH, [None] * DEPTH, [None] * 2, [None] * 2
    landed = {}
    for i in reversed(range(DEPTH)):
        s_mix, s_ffn = saved[i]
        dx, g_ffn[i] = _ffn_bwd(dx, ffn_params(i), s_ffn)
        piece_grad[("ffn_w_in", i)], piece_grad[("ffn_w_out", i)] = g_ffn[i]["w_in"], g_ffn[i]["w_out"]
        j = i // 2
        if i % 2 == 0:
            carried_of = carried_set(GRAD_SETS[i], ("ssd_w_out", j)) if i in GRAD_SETS else None
            dx, g_ssd[j], landed[i] = _ssd_bwd(dx, ssd_params(j), s_mix, carried_of)
            g_mix[i] = g_ssd[j]["mix_norm"]
            piece_grad[("ssd_w_in", j)], piece_grad[("ssd_w_out", j)] = g_ssd[j]["w_in"], g_ssd[j]["w_out"]
        else:
            dx, g_sb[j], landed[i] = _sb_bwd(dx, sb_params(j), s_mix, carried_set(GRAD_SETS[i], ("sb_w_out", j)))
            g_mix[i] = g_sb[j]["mix_norm"]
            piece_grad[("sb_w_qkv", j)] = g_sb[j]["w_qkv"]
    grad_x = dx.reshape(1, *dx.shape)

    g8 = pack_pieces(GRAD_SETS["end"], grad_piece)
    g8 = jnp.pad(g8, ((0, 0), (0, (-g8.shape[1]) % ADD_ROWS), (0, 0)))
    g8 = g8.reshape(4, 2, *g8.shape[1:])
    keep = lax.dynamic_index_in_dim(g8, core, axis=1, keepdims=False)
    give = lax.dynamic_index_in_dim(g8, 1 - core, axis=1, keepdims=False)
    got = _swap_with_sibling(give, "grads_to_sibling")
    chip_part = _add_pair(keep, got, "grads_add_sibling")
    landed["end"] = _exchange_chips(chip_part, "grads_across_chips")

    summed = {}
    for key, pieces in GRAD_SETS.items():
        total = _sum_slots(landed[key], "grads_sum_landed")
        for piece, (off, rows) in layout(pieces).items():
            summed[piece] = total[off:off + rows]
    big_res = [dict() for _ in range(4)]
    for n in BIG:
        lead, rows, cols = w[n].shape
        g_rows = jnp.concatenate([summed[(n, l)] for l in range(lead)], axis=0)
        g_nat = _rows_as_shard(n, g_rows, w[n].shape).reshape(1, lead * rows, cols)
        two_d = (lead * rows, cols)
        outs = _adamw(g_nat, w[n].reshape(two_d), m[n].reshape(two_d), v[n].reshape(two_d), "adamw_" + n)
        for kind in range(4):
            big_res[kind][n] = outs[kind].reshape(w[n].shape)

    small_g = {
        "mix_norm": jnp.stack(g_mix), "ffn_norm": jnp.stack([g["ffn_norm"] for g in g_ffn]), "final_norm": g_final.reshape(-1),
        "ssd_conv_b": jnp.stack([g["conv_b"] for g in g_ssd]), "ssd_dt_bias": jnp.stack([g["dt_bias"] for g in g_ssd]),
        "ssd_a_log": jnp.stack([g["a_log"] for g in g_ssd]), "ssd_d": jnp.stack([g["d"] for g in g_ssd]),
        "ssd_norm": jnp.stack([g["norm"] for g in g_ssd]), "ffn_conv_b": jnp.stack([g["conv_b"] for g in g_ffn]),
    }
    conv_g = {"ssd_conv_w": jnp.stack([g["conv_w"] for g in g_ssd]), "ffn_conv_w": jnp.stack([g["conv_w"] for g in g_ffn])}
    extra = [conv_g[n] for n in CONV] + [loss_part]
    small_pack, small_layout = _pack([small_g[n] for n in SMALL] + extra, 8)
    small_all = _all_gather(small_pack, "gather_small_grads")
    zeros_like = [jnp.zeros(a.shape, F32) for a in extra]
    sw, _ = _pack([w[n] for n in SMALL] + zeros_like, 8)
    sm, _ = _pack([m[n] for n in SMALL] + zeros_like, 8)
    sv, _ = _pack([v[n] for n in SMALL] + [jnp.ones(a.shape, F32) for a in extra], 8)
    small_out = _adamw(small_all, sw, sm, sv, "adamw_replicated")
    small_res = [_unpack(o, small_layout) for o in small_out]
    summed = small_res[0]
    loss = summed[-1][0, 0]
    conv_shard_g = []
    for n, gsum in zip(CONV, summed[len(SMALL):len(SMALL) + len(CONV)]):
        ns = w[n].shape[-1]
        conv_shard_g.append(lax.dynamic_slice_in_dim(gsum, dev * ns, ns, axis=2))
    cg, conv_sh_layout = _pack(conv_shard_g, 8)
    cw, _ = _pack([w[n] for n in CONV], 8)
    cm_, _ = _pack([m[n] for n in CONV], 8)
    cv, _ = _pack([v[n] for n in CONV], 8)
    conv_out = _adamw(cg.reshape(1, *cg.shape), cw, cm_, cv, "adamw_conv_taps")
    conv_res = [dict(zip(CONV, _unpack(o, conv_sh_layout))) for o in conv_out]

    def pick(kind, n):
        if n in BIG:
            return big_res[kind][n]
        if n in CONV:
            return conv_res[kind][n]
        return small_res[kind][SMALL.index(n)]

    outs = [loss, grad_x]
    for kind in range(4):
        outs += [pick(kind, n) for n in WEIGHTS]
    return tuple(outs)


def kernel(x, mix_norm, ffn_norm, final_norm, ssd_w_in, ssd_conv_w, ssd_conv_b, ssd_dt_bias, ssd_a_log, ssd_d, ssd_norm, ssd_w_out, sb_w_qkv, sb_w_out, ffn_w_in, ffn_conv_w, ffn_conv_b, ffn_w_out, loss_target, m_mix_norm, m_ffn_norm, m_final_norm, m_ssd_w_in, m_ssd_conv_w, m_ssd_conv_b, m_ssd_dt_bias, m_ssd_a_log, m_ssd_d, m_ssd_norm, m_ssd_w_out, m_sb_w_qkv, m_sb_w_out, m_ffn_w_in, m_ffn_conv_w, m_ffn_conv_b, m_ffn_w_out, v_mix_norm, v_ffn_norm, v_final_norm, v_ssd_w_in, v_ssd_conv_w, v_ssd_conv_b, v_ssd_dt_bias, v_ssd_a_log, v_ssd_d, v_ssd_norm, v_ssd_w_out, v_sb_w_qkv, v_sb_w_out, v_ffn_w_in, v_ffn_conv_w, v_ffn_conv_b, v_ffn_w_out):
    w = dict(mix_norm=mix_norm, ffn_norm=ffn_norm, final_norm=final_norm, ssd_w_in=ssd_w_in, ssd_conv_w=ssd_conv_w,
             ssd_conv_b=ssd_conv_b, ssd_dt_bias=ssd_dt_bias, ssd_a_log=ssd_a_log, ssd_d=ssd_d, ssd_norm=ssd_norm,
             ssd_w_out=ssd_w_out, sb_w_qkv=sb_w_qkv, sb_w_out=sb_w_out, ffn_w_in=ffn_w_in, ffn_conv_w=ffn_conv_w,
             ffn_conv_b=ffn_conv_b, ffn_w_out=ffn_w_out)
    m = dict(mix_norm=m_mix_norm, ffn_norm=m_ffn_norm, final_norm=m_final_norm, ssd_w_in=m_ssd_w_in, ssd_conv_w=m_ssd_conv_w,
             ssd_conv_b=m_ssd_conv_b, ssd_dt_bias=m_ssd_dt_bias, ssd_a_log=m_ssd_a_log, ssd_d=m_ssd_d, ssd_norm=m_ssd_norm,
             ssd_w_out=m_ssd_w_out, sb_w_qkv=m_sb_w_qkv, sb_w_out=m_sb_w_out, ffn_w_in=m_ffn_w_in, ffn_conv_w=m_ffn_conv_w,
             ffn_conv_b=m_ffn_conv_b, ffn_w_out=m_ffn_w_out)
    v = dict(mix_norm=v_mix_norm, ffn_norm=v_ffn_norm, final_norm=v_final_norm, ssd_w_in=v_ssd_w_in, ssd_conv_w=v_ssd_conv_w,
             ssd_conv_b=v_ssd_conv_b, ssd_dt_bias=v_ssd_dt_bias, ssd_a_log=v_ssd_a_log, ssd_d=v_ssd_d, ssd_norm=v_ssd_norm,
             ssd_w_out=v_ssd_w_out, sb_w_qkv=v_sb_w_qkv, sb_w_out=v_sb_w_out, ffn_w_in=v_ffn_w_in, ffn_conv_w=v_ffn_conv_w,
             ffn_conv_b=v_ffn_conv_b, ffn_w_out=v_ffn_w_out)
    return _step(x, loss_target, w, m, v)
```

```python
import jax
import jax.numpy as jnp
import numpy as np
from jax import lax
from jax.experimental import pallas as pl
from jax.experimental.pallas import tpu as pltpu

F32 = jnp.float32
MXU_DTYPE = jnp.bfloat16
MESH_ID = pl.DeviceIdType.MESH
N_DEV = 8

NORM_EPS = 1e-6
D_MODEL = 1024
DEPTH = 4
SSD_D_INNER = 2048
SSD_HEADS = 32
SSD_HEAD_DIM = 64
SSD_GROUPS = 8
SSD_STATE = 128
SSD_CONV = 4
SSD_CHUNK = 128
SSD_CONV_DIM = SSD_D_INNER + 2 * SSD_GROUPS * SSD_STATE
SSD_IN_DIM = SSD_D_INNER + SSD_CONV_DIM + SSD_HEADS
LANES = 128
SSD_IN_PAD = SSD_D_INNER + SSD_CONV_DIM + LANES
SB_HEADS = 16
SB_HEAD_DIM = 64
SB_TILE = 256
SB_SCALE = SB_HEAD_DIM ** -0.5
FFN_D_FF = 2816
FFN_CONV = 3
PACK_W = 1024

ADAM_LR = 0.001
ADAM_B1 = 0.9
ADAM_B2 = 0.999
ADAM_EPS = 1e-08
ADAM_WD = 0.01
ADAM_STEP = 10

VMEM_LIMIT_BYTES = 56 * 1024 * 1024


def _cp(*sem):
    return pltpu.CompilerParams(dimension_semantics=sem, vmem_limit_bytes=VMEM_LIMIT_BYTES)


def _iota(shape, dim):
    return lax.broadcasted_iota(jnp.int32, shape, dim)


def _sigmoid(x):
    return 1.0 / (1.0 + jnp.exp(-x))


def _mm(a, b):
    return lax.dot_general(a, b, (((1,), (0,)), ((), ())), preferred_element_type=F32)


def _mm_nt(a, b):
    return lax.dot_general(a, b, (((1,), (1,)), ((), ())), preferred_element_type=F32)


def _mm_tn(a, b):
    return lax.dot_general(a, b, (((0,), (0,)), ((), ())), preferred_element_type=F32)


def _split(x):
    hi = x.astype(MXU_DTYPE)
    lo = (x - hi.astype(F32)).astype(MXU_DTYPE)
    return hi, lo


def _mm_exact_rhs(x, m):
    hi, lo = _split(x)
    return _mm(jnp.concatenate([hi, lo], axis=1), jnp.concatenate([m, m], axis=0))


def _mm_exact_lhs(m, x):
    hi, lo = _split(x)
    return _mm(jnp.concatenate([m, m], axis=1), jnp.concatenate([hi, lo], axis=0))


def _my_place():
    return lax.axis_index("x"), lax.axis_index("y"), lax.axis_index("c")


def _gather_phase(phase, x_ref, out_ref, send_sems, recv_sems, local_sem):
    x, y, c = _my_place()
    me, sibling = (x, y, c), (x, y, 1 - c)
    chips = [(1 - x, y), (x, 1 - y), (1 - x, 1 - y)]

    def slot(px, py, pc):
        return out_ref.at[4 * px + 2 * py + pc]

    def copy(k, block, to, src=None):
        return pltpu.make_async_remote_copy(
            src_ref=slot(*block) if src is None else src, dst_ref=slot(*block),
            send_sem=send_sems.at[k], recv_sem=recv_sems.at[k], device_id=to, device_id_type=MESH_ID)

    if phase == "send":
        pltpu.make_async_copy(x_ref, slot(*me), local_sem).start()
        copy(0, me, sibling, src=x_ref).start()
        for j, chip in enumerate(chips):
            copy(1 + j, me, (*chip, c), src=x_ref).start()
    elif phase == "pass_on":
        for j, chip in enumerate(chips):
            copy(1 + j, (*chip, c), me).wait_recv()
            copy(4 + j, (*chip, c), sibling).start()
    else:
        copy(0, sibling, me).wait_recv()
        for j, chip in enumerate(chips):
            copy(4 + j, (*chip, 1 - c), me).wait_recv()
        copy(0, me, sibling, src=x_ref).wait_send()
        for j, chip in enumerate(chips):
            copy(1 + j, me, (*chip, c), src=x_ref).wait_send()
            copy(4 + j, (*chip, c), sibling).wait_send()
        pltpu.make_async_copy(x_ref, slot(*me), local_sem).wait()


def _all_gather(shard, name):
    r, c_ = shard.shape

    def body(*refs):
        _gather_phase("send", *refs)
        _gather_phase("pass_on", *refs)
        _gather_phase("finish", *refs)

    return pl.pallas_call(
        body, name=name,
        out_shape=jax.ShapeDtypeStruct((N_DEV, r, c_), shard.dtype),
        in_specs=[pl.BlockSpec(memory_space=pl.ANY)],
        out_specs=pl.BlockSpec(memory_space=pl.ANY),
        scratch_shapes=[pltpu.SemaphoreType.DMA((7,)), pltpu.SemaphoreType.DMA((7,)), pltpu.SemaphoreType.DMA(())],
    )(shard)


def _swap_with_sibling(buf, name):
    def body(x_ref, out_ref, send_sem, recv_sem):
        x, y, c = _my_place()
        cp = pltpu.make_async_remote_copy(src_ref=x_ref, dst_ref=out_ref, send_sem=send_sem, recv_sem=recv_sem,
                                          device_id=(x, y, 1 - c), device_id_type=MESH_ID)
        cp.start()
        cp.wait()

    return pl.pallas_call(
        body, name=name, out_shape=jax.ShapeDtypeStruct(buf.shape, buf.dtype),
        in_specs=[pl.BlockSpec(memory_space=pl.ANY)], out_specs=pl.BlockSpec(memory_space=pl.ANY),
        scratch_shapes=[pltpu.SemaphoreType.DMA(()), pltpu.SemaphoreType.DMA(())],
    )(buf)


def _exchange_chips(parts, name):
    def body(p_ref, out_ref, send_sems, recv_sems, local_sem):
        x, y, c = _my_place()
        my_q = 2 * x + y
        chips = [(1 - x, y), (x, 1 - y), (1 - x, 1 - y)]
        local = pltpu.make_async_copy(p_ref.at[my_q], out_ref.at[my_q], local_sem)
        local.start()

        def copy(k, px, py):
            return pltpu.make_async_remote_copy(
                src_ref=p_ref.at[2 * px + py], dst_ref=out_ref.at[my_q],
                send_sem=send_sems.at[k], recv_sem=recv_sems.at[k], device_id=(px, py, c), device_id_type=MESH_ID)

        def landing(k, px, py):
            return pltpu.make_async_remote_copy(
                src_ref=p_ref.at[my_q], dst_ref=out_ref.at[2 * px + py],
                send_sem=send_sems.at[k], recv_sem=recv_sems.at[k], device_id=(px, py, c), device_id_type=MESH_ID)

        sends = [copy(k, px, py) for k, (px, py) in enumerate(chips)]
        for cp in sends:
            cp.start()
        for k, (px, py) in enumerate(chips):
            landing(k, px, py).wait_recv()
        for cp in sends:
            cp.wait_send()
        local.wait()

    return pl.pallas_call(
        body, name=name, out_shape=jax.ShapeDtypeStruct(parts.shape, parts.dtype),
        in_specs=[pl.BlockSpec(memory_space=pl.ANY)], out_specs=pl.BlockSpec(memory_space=pl.ANY),
        scratch_shapes=[pltpu.SemaphoreType.DMA((3,)), pltpu.SemaphoreType.DMA((3,)), pltpu.SemaphoreType.DMA(())],
    )(parts)


RELATIONS = [(0, 0, 1), (1, 0, 0), (0, 1, 0), (1, 1, 0), (1, 0, 1), (0, 1, 1), (1, 1, 1)]
EXCHANGE_SCRATCH = [pltpu.SemaphoreType.DMA((len(RELATIONS),)), pltpu.SemaphoreType.DMA((len(RELATIONS),)),
                    pltpu.SemaphoreType.DMA(())]


def _exchange_copies(src_ref, land_ref, send_sems, recv_sems, local_sem, per_peer, incoming=True):
    x, y, c = _my_place()
    me = 4 * x + 2 * y + c

    def src(j):
        return src_ref.at[j] if per_peer else src_ref

    local = pltpu.make_async_copy(src(me), land_ref.at[me], local_sem)
    pairs = []
    for k, (dx, dy, dc) in enumerate(RELATIONS):
        peer = (1 - x if dx else x, 1 - y if dy else y, 1 - c if dc else c)
        j = 4 * peer[0] + 2 * peer[1] + peer[2]
        sems = dict(send_sem=send_sems.at[k], recv_sem=recv_sems.at[k], device_id=peer, device_id_type=MESH_ID)
        pairs.append((pltpu.make_async_remote_copy(src_ref=src(j), dst_ref=land_ref.at[me], **sems),
                      pltpu.make_async_remote_copy(src_ref=src(me), dst_ref=land_ref.at[j], **sems) if incoming else None))
    return local, pairs


def _exchange_start(*refs, per_peer):
    local, pairs = _exchange_copies(*refs, per_peer, incoming=False)
    local.start()
    for outgoing, _ in pairs:
        outgoing.start()


def _exchange_finish(*refs, per_peer):
    local, pairs = _exchange_copies(*refs, per_peer)
    for _, incoming in pairs:
        incoming.wait_recv()
    for outgoing, _ in pairs:
        outgoing.wait_send()
    local.wait()


TOKEN_ROWS = 1024


def _matmul(a, b, mode, out_dtype, name, add=None, tm=512, tn=512, tk=512):
    if mode == "nn":
        (m, k), (k2, n) = a.shape, b.shape
    elif mode == "nt":
        (m, k), (n, k2) = a.shape, b.shape
    else:
        (k, m), (k2, n) = a.shape, b.shape
    assert k == k2, (a.shape, b.shape, mode)
    tm, tn, tk = min(tm, m), min(tn, n), min(tk, k)
    assert m % tm == 0 and n % tn == 0 and k % tk == 0, (m, n, k, tm, tn, tk)
    nk = k // tk
    mm = {"nn": _mm, "nt": _mm_nt, "tn": _mm_tn}[mode]

    def body(*refs):
        if add is None:
            a_ref, b_ref, o_ref, acc_ref = refs
        else:
            a_ref, b_ref, add_ref, o_ref, acc_ref = refs
        kk = pl.program_id(2)

        @pl.when(kk == 0)
        def _():
            acc_ref[...] = jnp.zeros_like(acc_ref)

        acc_ref[...] += mm(a_ref[...].astype(MXU_DTYPE), b_ref[...].astype(MXU_DTYPE))

        @pl.when(kk == nk - 1)
        def _():
            res = acc_ref[...]
            if add is not None:
                res = res + add_ref[...]
            o_ref[...] = res.astype(o_ref.dtype)

    a_spec = {"nn": pl.BlockSpec((tm, tk), lambda i, j, kk: (i, kk)),
              "nt": pl.BlockSpec((tm, tk), lambda i, j, kk: (i, kk)),
              "tn": pl.BlockSpec((tk, tm), lambda i, j, kk: (kk, i))}[mode]
    b_spec = {"nn": pl.BlockSpec((tk, tn), lambda i, j, kk: (kk, j)),
              "nt": pl.BlockSpec((tn, tk), lambda i, j, kk: (j, kk)),
              "tn": pl.BlockSpec((tk, tn), lambda i, j, kk: (kk, j))}[mode]
    o_spec = pl.BlockSpec((tm, tn), lambda i, j, kk: (i, j))
    in_specs, args = [a_spec, b_spec], [a, b]
    if add is not None:
        in_specs.append(o_spec)
        args.append(add)
    return pl.pallas_call(
        body, name=name, grid=(m // tm, n // tn, nk), in_specs=in_specs, out_specs=o_spec,
        out_shape=jax.ShapeDtypeStruct((m, n), out_dtype),
        scratch_shapes=[pltpu.VMEM((tm, tn), F32)],
        compiler_params=_cp("parallel", "parallel", "arbitrary"),
    )(*args)


def _rmsnorm(x, g, name):
    t, d = x.shape
    tm = min(512, t)

    def body(x_ref, g_ref, o_ref):
        xv = x_ref[...]
        r = lax.rsqrt(jnp.mean(xv * xv, axis=-1, keepdims=True) + NORM_EPS)
        o_ref[...] = (xv * r * g_ref[...]).astype(o_ref.dtype)

    return pl.pallas_call(
        body, name=name, grid=(t // tm,),
        in_specs=[pl.BlockSpec((tm, d), lambda i: (i, 0)), pl.BlockSpec((1, d), lambda i: (0, 0))],
        out_specs=pl.BlockSpec((tm, d), lambda i: (i, 0)),
        out_shape=jax.ShapeDtypeStruct((t, d), MXU_DTYPE), compiler_params=_cp("parallel"),
    )(x, g.reshape(1, d))


def _rmsnorm_bwd(x, g, dh, dres, name):
    t, d = x.shape
    tm = min(512, t)

    def body(x_ref, g_ref, dh_ref, dres_ref, dx_ref, dg_ref):
        @pl.when(pl.program_id(0) == 0)
        def _():
            dg_ref[...] = jnp.zeros_like(dg_ref)

        xv = x_ref[...]
        r = lax.rsqrt(jnp.mean(xv * xv, axis=-1, keepdims=True) + NORM_EPS)
        xn = xv * r
        dhv = dh_ref[...]
        u = dhv * g_ref[...]
        dx_ref[...] = dres_ref[...] + r * (u - xn * jnp.mean(u * xn, axis=-1, keepdims=True))
        dg_ref[...] += jnp.sum(dhv * xn, axis=0, keepdims=True)

    row = pl.BlockSpec((tm, d), lambda i: (i, 0))
    vec = pl.BlockSpec((1, d), lambda i: (0, 0))
    return pl.pallas_call(
        body, name=name, grid=(t // tm,), in_specs=[row, vec, row, row], out_specs=[row, vec],
        out_shape=[jax.ShapeDtypeStruct((t, d), F32), jax.ShapeDtypeStruct((1, d), F32)],
        compiler_params=_cp("arbitrary"),
    )(x, g.reshape(1, d), dh, dres)


def _final_norm_loss(x, g, target, name):
    t, d = x.shape
    tm = min(512, t)

    def body(x_ref, g_ref, t_ref, dx_ref, dg_ref, loss_ref):
        @pl.when(pl.program_id(0) == 0)
        def _():
            dg_ref[...] = jnp.zeros_like(dg_ref)
            loss_ref[...] = jnp.zeros_like(loss_ref)

        xv = x_ref[...]
        gv = g_ref[...]
        r = lax.rsqrt(jnp.mean(xv * xv, axis=-1, keepdims=True) + NORM_EPS)
        xn = xv * r
        err = xn * gv - t_ref[...]
        per_tok = jnp.mean(err * err, axis=-1, keepdims=True)
        loss_ref[...] += jnp.broadcast_to(0.5 * jnp.sum(per_tok, axis=0, keepdims=True), loss_ref.shape)
        dy = err * (1.0 / d)
        u = dy * gv
        dx_ref[...] = r * (u - xn * jnp.mean(u * xn, axis=-1, keepdims=True))
        dg_ref[...] += jnp.sum(dy * xn, axis=0, keepdims=True)

    row = pl.BlockSpec((tm, d), lambda i: (i, 0))
    vec = pl.BlockSpec((1, d), lambda i: (0, 0))
    return pl.pallas_call(
        body, name=name, grid=(t // tm,), in_specs=[row, vec, row],
        out_specs=[row, vec, pl.BlockSpec((1, LANES), lambda i: (0, 0))],
        out_shape=[jax.ShapeDtypeStruct((t, d), F32), jax.ShapeDtypeStruct((1, d), F32),
                   jax.ShapeDtypeStruct((1, LANES), F32)],
        compiler_params=_cp("arbitrary"),
    )(x, g.reshape(1, d), target)


CONV_COLS = 128


def _shifts_down(p, width):
    row = _iota(p.shape, 0)
    return [jnp.where(row >= s, pltpu.roll(p, s, axis=0), 0.0) for s in range(1, width)]


def _shifts_up(p, width):
    n = p.shape[0]
    row = _iota(p.shape, 0)
    return [jnp.where(row < n - s, pltpu.roll(p, n - s, axis=0), 0.0) for s in range(1, width)]


def _conv_pre(p, shifted, w_ref, b_ref):
    width = w_ref.shape[0]
    u = b_ref[...] + w_ref[width - 1:width, :] * p
    for s in range(1, width):
        u = u + w_ref[width - 1 - s:width - s, :] * shifted[s - 1]
    return u


def _conv_transpose(du, w_ref):
    width = w_ref.shape[0]
    shifted = _shifts_up(du, width)
    dp = w_ref[width - 1:width, :] * du
    for s in range(1, width):
        dp = dp + w_ref[width - 1 - s:width - s, :] * shifted[s - 1]
    return dp


def _conv_wgrad(du, p, shifted, dw_ref, db_ref):
    width = dw_ref.shape[0]
    db_ref[...] = jnp.sum(du, axis=0, keepdims=True)
    dw_ref[width - 1:width, :] = jnp.sum(du * p, axis=0, keepdims=True)
    for s in range(1, width):
        dw_ref[width - 1 - s:width - s, :] = jnp.sum(du * shifted[s - 1], axis=0, keepdims=True)


def _ssd_conv_fwd(proj, w, b, name):
    t = proj.shape[0]
    cb = CONV_COLS
    off = SSD_D_INNER // cb

    def body(p_ref, w_ref, b_ref, o_ref):
        p = p_ref[...]
        u = _conv_pre(p, _shifts_down(p, SSD_CONV), w_ref, b_ref)
        o_ref[...] = u * _sigmoid(u)

    return pl.pallas_call(
        body, name=name, grid=(SSD_CONV_DIM // cb,),
        in_specs=[pl.BlockSpec((t, cb), lambda j: (0, j + off)), pl.BlockSpec((SSD_CONV, cb), lambda j: (0, j)),
                  pl.BlockSpec((1, cb), lambda j: (0, j))],
        out_specs=pl.BlockSpec((t, cb), lambda j: (0, j)),
        out_shape=jax.ShapeDtypeStruct((t, SSD_CONV_DIM), F32), compiler_params=_cp("parallel"),
    )(proj, w, b.reshape(1, -1))


def _ssd_conv_bwd(proj, w, b, dact, name):
    t = proj.shape[0]
    cb = CONV_COLS
    off = SSD_D_INNER // cb

    def body(p_ref, w_ref, b_ref, da_ref, dp_ref, dw_ref, db_ref):
        p = p_ref[...]
        shifted = _shifts_down(p, SSD_CONV)
        u = _conv_pre(p, shifted, w_ref, b_ref)
        sg = _sigmoid(u)
        du = da_ref[...] * (sg * (1.0 + u * (1.0 - sg)))
        dp_ref[...] = _conv_transpose(du, w_ref).astype(dp_ref.dtype)
        _conv_wgrad(du, p, shifted, dw_ref, db_ref)

    col = pl.BlockSpec((t, cb), lambda j: (0, j))
    wspec = pl.BlockSpec((SSD_CONV, cb), lambda j: (0, j))
    bspec = pl.BlockSpec((1, cb), lambda j: (0, j))
    return pl.pallas_call(
        body, name=name, grid=(SSD_CONV_DIM // cb,),
        in_specs=[pl.BlockSpec((t, cb), lambda j: (0, j + off)), wspec, bspec, col],
        out_specs=[col, wspec, bspec],
        out_shape=[jax.ShapeDtypeStruct((t, SSD_CONV_DIM), MXU_DTYPE), jax.ShapeDtypeStruct((SSD_CONV, SSD_CONV_DIM), F32),
                   jax.ShapeDtypeStruct((1, SSD_CONV_DIM), F32)],
        compiler_params=_cp("parallel"),
    )(proj, w, b.reshape(1, -1), dact)


def _ffn_conv_fwd(proj, w, b, name):
    t = proj.shape[0]
    cb = CONV_COLS
    nb = FFN_D_FF // cb

    def body(pg_ref, pu_ref, wg_ref, wu_ref, bg_ref, bu_ref, o_ref):
        pg, pu = pg_ref[...], pu_ref[...]
        ug = _conv_pre(pg, _shifts_down(pg, FFN_CONV), wg_ref, bg_ref)
        uu = _conv_pre(pu, _shifts_down(pu, FFN_CONV), wu_ref, bu_ref)
        o_ref[...] = (ug * _sigmoid(ug) * uu).astype(o_ref.dtype)

    gcol = pl.BlockSpec((t, cb), lambda j: (0, j))
    ucol = pl.BlockSpec((t, cb), lambda j: (0, j + nb))
    b2 = b.reshape(1, -1)
    return pl.pallas_call(
        body, name=name, grid=(nb,),
        in_specs=[gcol, ucol, pl.BlockSpec((FFN_CONV, cb), lambda j: (0, j)), pl.BlockSpec((FFN_CONV, cb), lambda j: (0, j + nb)),
                  pl.BlockSpec((1, cb), lambda j: (0, j)), pl.BlockSpec((1, cb), lambda j: (0, j + nb))],
        out_specs=gcol, out_shape=jax.ShapeDtypeStruct((t, FFN_D_FF), MXU_DTYPE), compiler_params=_cp("parallel"),
    )(proj, proj, w, w, b2, b2)


def _ffn_conv_bwd(proj, w, b, dact, name):
    t = proj.shape[0]
    cb = CONV_COLS
    nb = FFN_D_FF // cb

    def body(pg_ref, pu_ref, wg_ref, wu_ref, bg_ref, bu_ref, da_ref,
             dpg_ref, dpu_ref, dwg_ref, dwu_ref, dbg_ref, dbu_ref):
        pg, pu = pg_ref[...], pu_ref[...]
        pg_shifted, pu_shifted = _shifts_down(pg, FFN_CONV), _shifts_down(pu, FFN_CONV)
        ug = _conv_pre(pg, pg_shifted, wg_ref, bg_ref)
        uu = _conv_pre(pu, pu_shifted, wu_ref, bu_ref)
        sg = _sigmoid(ug)
        da = da_ref[...]
        dug = da * uu * (sg * (1.0 + ug * (1.0 - sg)))
        duu = da * (ug * sg)
        dpg_ref[...] = _conv_transpose(dug, wg_ref).astype(dpg_ref.dtype)
        dpu_ref[...] = _conv_transpose(duu, wu_ref).astype(dpu_ref.dtype)
        _conv_wgrad(dug, pg, pg_shifted, dwg_ref, dbg_ref)
        _conv_wgrad(duu, pu, pu_shifted, dwu_ref, dbu_ref)

    gcol = pl.BlockSpec((t, cb), lambda j: (0, j))
    ucol = pl.BlockSpec((t, cb), lambda j: (0, j + nb))
    wg = pl.BlockSpec((FFN_CONV, cb), lambda j: (0, j))
    wu = pl.BlockSpec((FFN_CONV, cb), lambda j: (0, j + nb))
    bg = pl.BlockSpec((1, cb), lambda j: (0, j))
    bu = pl.BlockSpec((1, cb), lambda j: (0, j + nb))
    b2 = b.reshape(1, -1)
    half = jax.ShapeDtypeStruct((t, FFN_D_FF), MXU_DTYPE)
    return pl.pallas_call(
        body, name=name, grid=(nb,),
        in_specs=[gcol, ucol, wg, wu, bg, bu, gcol],
        out_specs=[gcol, gcol, wg, wg, bg, bg],
        out_shape=[half, half, jax.ShapeDtypeStruct((FFN_CONV, FFN_D_FF), F32), jax.ShapeDtypeStruct((FFN_CONV, FFN_D_FF), F32),
                   jax.ShapeDtypeStruct((1, FFN_D_FF), F32), jax.ShapeDtypeStruct((1, FFN_D_FF), F32)],
        compiler_params=_cp("parallel"),
    )(proj, proj, w, w, b2, b2, dact)


SSD_ROWS = 128
DT_COL = (SSD_D_INNER + SSD_CONV_DIM) // LANES


def _head_expand():
    h = np.arange(LANES)[:, None]
    col = np.arange(SSD_D_INNER)[None, :]
    return jnp.asarray((col // SSD_HEAD_DIM == h), MXU_DTYPE)


def _chunk_tri(n, lower):
    t = _iota((n, n), 0)
    s = _iota((n, n), 1)
    shift = SSD_CHUNK.bit_length() - 1
    same = jnp.right_shift(t, shift) == jnp.right_shift(s, shift)
    tri = (s <= t) if lower else (s >= t)
    return jnp.where(same & tri, 1.0, 0.0).astype(MXU_DTYPE)


def _softplus(x):
    return jnp.maximum(x, 0.0) + jnp.log(1.0 + jnp.exp(-jnp.abs(x)))


def _ssd_dt_fwd(proj, act, dt_bias, a_neg, expand, name):
    t = proj.shape[0]
    tm = min(SSD_ROWS, t)

    def body(raw_ref, xs_ref, bias_ref, a_ref, e_ref, xdt_ref, dt_ref, acum_ref):
        lane = _iota((tm, LANES), 1)
        dt = jnp.where(lane < SSD_HEADS, _softplus(raw_ref[...] + bias_ref[...]), 0.0)
        dt_ref[...] = dt
        xdt_ref[...] = xs_ref[...] * _mm_exact_rhs(dt, e_ref[...])
        acum_ref[...] = _mm_exact_lhs(_chunk_tri(tm, True), a_ref[...] * dt)

    vec = pl.BlockSpec((1, LANES), lambda i: (0, 0))
    return pl.pallas_call(
        body, name=name, grid=(t // tm,),
        in_specs=[pl.BlockSpec((tm, LANES), lambda i: (i, DT_COL)), pl.BlockSpec((tm, SSD_D_INNER), lambda i: (i, 0)),
                  vec, vec, pl.BlockSpec((LANES, SSD_D_INNER), lambda i: (0, 0))],
        out_specs=[pl.BlockSpec((tm, SSD_D_INNER), lambda i: (i, 0)), pl.BlockSpec((tm, LANES), lambda i: (i, 0)),
                   pl.BlockSpec((tm, LANES), lambda i: (i, 0))],
        out_shape=[jax.ShapeDtypeStruct((t, SSD_D_INNER), F32), jax.ShapeDtypeStruct((t, LANES), F32),
                   jax.ShapeDtypeStruct((t, LANES), F32)],
        compiler_params=_cp("parallel"),
    )(proj, act, dt_bias, a_neg, expand)


def _ssd_dt_bwd(proj, act, dt, dxdt, dyy, dacum, dt_bias, a_neg, d_exp, expand, expand_t, name):
    t = proj.shape[0]
    tm = min(SSD_ROWS, t)

    def body(raw_ref, xs_ref, dt_ref, dxdt_ref, dyy_ref, dac_ref, bias_ref, a_ref, dsk_ref, e_ref, et_ref,
             dxs_ref, draw_ref, da_ref, dbias_ref, dd_ref):
        @pl.when(pl.program_id(0) == 0)
        def _():
            da_ref[...] = jnp.zeros_like(da_ref)
            dbias_ref[...] = jnp.zeros_like(dbias_ref)
            dd_ref[...] = jnp.zeros_like(dd_ref)

        lane = _iota((tm, LANES), 1)
        xs, dt, dxdt, dyy = xs_ref[...], dt_ref[...], dxdt_ref[...], dyy_ref[...]
        dxs_ref[...] = dxdt * _mm_exact_rhs(dt, e_ref[...]) + dsk_ref[...] * dyy
        dd_ref[...] += jnp.sum(dyy * xs, axis=0, keepdims=True)
        ddt = _mm_exact_rhs(dxdt * xs, et_ref[...])
        da = _mm_exact_lhs(_chunk_tri(tm, False), dac_ref[...])
        ddt = ddt + da * a_ref[...]
        da_ref[...] += jnp.sum(da * dt, axis=0, keepdims=True)
        draw = jnp.where(lane < SSD_HEADS, ddt * _sigmoid(raw_ref[...] + bias_ref[...]), 0.0)
        dbias_ref[...] += jnp.sum(draw, axis=0, keepdims=True)
        draw_ref[...] = draw.astype(draw_ref.dtype)

    wide = pl.BlockSpec((tm, SSD_D_INNER), lambda i: (i, 0))
    thin = pl.BlockSpec((tm, LANES), lambda i: (i, 0))
    vec = pl.BlockSpec((1, LANES), lambda i: (0, 0))
    wvec = pl.BlockSpec((1, SSD_D_INNER), lambda i: (0, 0))
    return pl.pallas_call(
        body, name=name, grid=(t // tm,),
        in_specs=[pl.BlockSpec((tm, LANES), lambda i: (i, DT_COL)), wide, thin, wide, wide, thin, vec, vec, wvec,
                  pl.BlockSpec((LANES, SSD_D_INNER), lambda i: (0, 0)), pl.BlockSpec((SSD_D_INNER, LANES), lambda i: (0, 0))],
        out_specs=[wide, thin, vec, vec, wvec],
        out_shape=[jax.ShapeDtypeStruct((t, SSD_D_INNER), F32), jax.ShapeDtypeStruct((t, LANES), MXU_DTYPE),
                   jax.ShapeDtypeStruct((1, LANES), F32), jax.ShapeDtypeStruct((1, LANES), F32),
                   jax.ShapeDtypeStruct((1, SSD_D_INNER), F32)],
        compiler_params=_cp("arbitrary"),
    )(proj, act, dt, dxdt, dyy, dacum, dt_bias, a_neg, d_exp, expand, expand_t)


SSD_PAIR = 2 * SSD_HEAD_DIM
HEADS_PER_GROUP = SSD_HEADS // SSD_GROUPS
GROUP_COLS = HEADS_PER_GROUP * SSD_HEAD_DIM
B_COL0 = SSD_D_INNER // SSD_STATE
C_COL0 = (SSD_D_INNER + SSD_GROUPS * SSD_STATE) // SSD_STATE


def _pair_cols(vals, h0, lo_mask):
    return jnp.where(lo_mask, vals[:, h0:h0 + 1], vals[:, h0 + 1:h0 + 2])


SCAN_GROUPS = 2
SCAN_STEPS = SSD_GROUPS // SCAN_GROUPS


def _scan_step_is(g, c):
    return (pl.program_id(0) == g) & (pl.program_id(1) == c)


def _ssd_scan_fwd(xdt, act, acum_g, acum_gt, name, carried=None):
    t = xdt.shape[0]
    nc = t // SSD_CHUNK
    ln = SSD_CHUNK
    c_in_specs, c_in, c_out_specs, c_out = _carried_specs(carried)

    def body(*refs):
        if carried is None:
            x_ref, b_ref, c_ref, ac_ref, act_ref, y_ref, sst_ref, state = refs
            comm_refs = None
        else:
            x_ref, b_ref, c_ref, ac_ref, act_ref, src_ref, y_ref, sst_ref, land_ref, state, send_sems, recv_sems, local_sem = refs
            comm_refs = (src_ref, land_ref, send_sems, recv_sems, local_sem)
        finish = _carried_hooks(carried, comm_refs, _scan_step_is(0, 0), _scan_step_is(SCAN_STEPS - 1, 0),
                                _scan_step_is(SCAN_STEPS - 1, nc - 1))

        @pl.when(pl.program_id(1) == 0)
        def _():
            state[...] = jnp.zeros_like(state)

        causal = _iota((ln, ln), 1) <= _iota((ln, ln), 0)
        lo_mask = _iota((ln, SSD_PAIR), 1) < SSD_HEAD_DIM
        lo_rows = _iota((SSD_PAIR, SSD_STATE), 0) < SSD_HEAD_DIM
        for gg in range(SCAN_GROUPS):
            sst_ref[0, gg] = state[gg * GROUP_COLS:(gg + 1) * GROUP_COLS, :]
            bm = b_ref[:, gg * SSD_STATE:(gg + 1) * SSD_STATE].astype(MXU_DTYPE)
            cm = c_ref[:, gg * SSD_STATE:(gg + 1) * SSD_STATE].astype(MXU_DTYPE)
            cb = _mm_nt(cm, bm)
            ac, act_ = ac_ref[gg], act_ref[gg]
            last = ac[ln - 1:ln, :]
            e_ac = jnp.exp(ac)
            w_all = jnp.exp(last - ac)
            e_last = jnp.exp(last)
            for pr in range(2):
                first = gg * GROUP_COLS + pr * SSD_PAIR
                cols = slice(first, first + SSD_PAIR)
                xp = x_ref[:, cols]
                sp = state[cols, :]
                ydiag = jnp.zeros((ln, SSD_PAIR), F32)
                for hh in range(2):
                    h = 2 * pr + hh
                    seg = ac[:, h:h + 1] - act_[h:h + 1, :]
                    dec = jnp.exp(jnp.where(causal, seg, -1e30))
                    mask = lo_mask if hh == 0 else jnp.logical_not(lo_mask)
                    ydiag = ydiag + _mm((cb * dec).astype(MXU_DTYPE), jnp.where(mask, xp, 0.0).astype(MXU_DTYPE))
                yoff = _mm_nt(cm, sp.astype(MXU_DTYPE)) * _pair_cols(e_ac, 2 * pr, lo_mask)
                y_ref[:, cols] = ydiag + yoff
                xw = (xp * _pair_cols(w_all, 2 * pr, lo_mask)).astype(MXU_DTYPE)
                el = jnp.where(lo_rows, e_last[:, 2 * pr:2 * pr + 1], e_last[:, 2 * pr + 1:2 * pr + 2])
                state[cols, :] = sp * el + _mm_tn(xw, bm)
        finish()

    sg = SCAN_GROUPS
    return pl.pallas_call(
        body, name=name, grid=(SCAN_STEPS, nc),
        in_specs=[pl.BlockSpec((ln, sg * GROUP_COLS), lambda g, c: (c, g)),
                  pl.BlockSpec((ln, sg * SSD_STATE), lambda g, c: (c, B_COL0 // sg + g)),
                  pl.BlockSpec((ln, sg * SSD_STATE), lambda g, c: (c, C_COL0 // sg + g)),
                  pl.BlockSpec((sg, ln, HEADS_PER_GROUP), lambda g, c: (g, c, 0)),
                  pl.BlockSpec((sg, HEADS_PER_GROUP, ln), lambda g, c: (g, 0, c))] + c_in_specs,
        out_specs=[pl.BlockSpec((ln, sg * GROUP_COLS), lambda g, c: (c, g)),
                   pl.BlockSpec((1, sg, GROUP_COLS, SSD_STATE), lambda g, c: (c, g, 0, 0))] + c_out_specs,
        out_shape=[jax.ShapeDtypeStruct((t, SSD_D_INNER), F32),
                   jax.ShapeDtypeStruct((nc, SSD_GROUPS, GROUP_COLS, SSD_STATE), F32)] + c_out,
        scratch_shapes=[pltpu.VMEM((sg * GROUP_COLS, SSD_STATE), F32)] + (EXCHANGE_SCRATCH if carried else []),
        compiler_params=_cp("arbitrary", "arbitrary"),
    )(xdt, act, act, acum_g, acum_gt, *c_in)


def _ssd_scan_bwd(xdt, act, acum_g, acum_gt, states, dy, name, carried=None):
    t = xdt.shape[0]
    nc = t // SSD_CHUNK
    ln = SSD_CHUNK
    c_in_specs, c_in, c_out_specs, c_out = _carried_specs(carried)

    def body(*refs):
        if carried is None:
            x_ref, b_ref, c_ref, ac_ref, act_ref, sst_ref, dy_ref, dx_ref, db_ref, dc_ref, dacol_ref, darow_ref, dstate = refs
            comm_refs = None
        else:
            (x_ref, b_ref, c_ref, ac_ref, act_ref, sst_ref, dy_ref, src_ref, dx_ref, db_ref, dc_ref, dacol_ref, darow_ref,
             land_ref, dstate, send_sems, recv_sems, local_sem) = refs
            comm_refs = (src_ref, land_ref, send_sems, recv_sems, local_sem)
        finish = _carried_hooks(carried, comm_refs, _scan_step_is(0, 0), _scan_step_is(SCAN_STEPS - 1, 0),
                                _scan_step_is(SCAN_STEPS - 1, nc - 1))

        @pl.when(pl.program_id(1) == 0)
        def _():
            dstate[...] = jnp.zeros_like(dstate)

        for gg in range(SCAN_GROUPS):
            group_bwd(gg, x_ref, b_ref, c_ref, ac_ref, act_ref, sst_ref, dy_ref, dx_ref, db_ref, dc_ref, dacol_ref, darow_ref, dstate)
        finish()

    def group_bwd(gg, x_ref, b_ref, c_ref, ac_ref, act_ref, sst_ref, dy_ref, dx_ref, db_ref, dc_ref, dacol_ref, darow_ref, dstate):
        bc_cols = slice(gg * SSD_STATE, (gg + 1) * SSD_STATE)
        bm = b_ref[:, bc_cols].astype(MXU_DTYPE)
        cm = c_ref[:, bc_cols].astype(MXU_DTYPE)
        cb = _mm_nt(cm, bm)
        ac, act_ = ac_ref[gg], act_ref[gg]
        causal = _iota((ln, ln), 1) <= _iota((ln, ln), 0)
        lo_mask = _iota((ln, SSD_PAIR), 1) < SSD_HEAD_DIM
        lo_rows = _iota((SSD_PAIR, SSD_STATE), 0) < SSD_HEAD_DIM
        lane4 = _iota((ln, HEADS_PER_GROUP), 1)
        sub4 = _iota((HEADS_PER_GROUP, ln), 0)
        is_last = _iota((ln, 1), 0) == ln - 1
        last = ac[ln - 1:ln, :]
        e_ac = jnp.exp(ac)
        w_all = jnp.exp(last - ac)
        e_last = jnp.exp(last)
        dcb = jnp.zeros((ln, ln), F32)
        dc_acc = jnp.zeros((ln, SSD_STATE), F32)
        db_acc = jnp.zeros((ln, SSD_STATE), F32)
        dacol = jnp.zeros((ln, HEADS_PER_GROUP), F32)
        darow = jnp.zeros((HEADS_PER_GROUP, ln), F32)
        for pr in range(2):
            in_group = slice(pr * SSD_PAIR, (pr + 1) * SSD_PAIR)
            cols = slice(gg * GROUP_COLS + pr * SSD_PAIR, gg * GROUP_COLS + (pr + 1) * SSD_PAIR)
            xp = x_ref[:, cols]
            dyp = dy_ref[:, cols]
            sp = sst_ref[0, gg, in_group, :]
            dsp = dstate[cols, :]
            ea = _pair_cols(e_ac, 2 * pr, lo_mask)
            w = _pair_cols(w_all, 2 * pr, lo_mask)
            dye = (dyp * ea).astype(MXU_DTYPE)
            dx_state = w * _mm_nt(bm, dsp.astype(MXU_DTYPE))
            yoff = _mm_nt(cm, sp.astype(MXU_DTYPE)) * ea
            dxp = dx_state
            for hh in range(2):
                h = 2 * pr + hh
                mask = lo_mask if hh == 0 else jnp.logical_not(lo_mask)
                rmask = lo_rows if hh == 0 else jnp.logical_not(lo_rows)
                seg = ac[:, h:h + 1] - act_[h:h + 1, :]
                dec = jnp.exp(jnp.where(causal, seg, -1e30))
                m = cb * dec
                dym = jnp.where(mask, dyp, 0.0).astype(MXU_DTYPE)
                xm = jnp.where(mask, xp, 0.0).astype(MXU_DTYPE)
                g = _mm_nt(dym, xm)
                dxp = dxp + _mm_tn(m.astype(MXU_DTYPE), dym)
                dcb = dcb + dec * g
                mg = m * g
                rs = jnp.sum(mg, axis=1, keepdims=True)
                cs = jnp.sum(mg, axis=0, keepdims=True)
                t_off = jnp.sum(jnp.where(mask, dyp * yoff, 0.0), axis=1, keepdims=True)
                q = jnp.sum(jnp.where(mask, xp * dx_state, 0.0), axis=1, keepdims=True)
                qsum = jnp.sum(q, axis=0, keepdims=True)
                ds_s = jnp.sum(jnp.sum(jnp.where(rmask, dsp * sp, 0.0), axis=1, keepdims=True), axis=0, keepdims=True)
                extra = qsum + e_last[:, h:h + 1] * ds_s
                col = rs + t_off - q + jnp.where(is_last, extra, 0.0)
                dacol = jnp.where(lane4 == h, col, dacol)
                darow = jnp.where(sub4 == h, -cs, darow)
            dx_ref[:, cols] = dxp
            dc_acc = dc_acc + _mm(dye, sp.astype(MXU_DTYPE))
            db_acc = db_acc + _mm((xp * w).astype(MXU_DTYPE), dsp.astype(MXU_DTYPE))
            el = jnp.where(lo_rows, e_last[:, 2 * pr:2 * pr + 1], e_last[:, 2 * pr + 1:2 * pr + 2])
            dstate[cols, :] = dsp * el + _mm_tn(dye, cm)
        dcbm = dcb.astype(MXU_DTYPE)
        dc_ref[:, bc_cols] = _mm(dcbm, bm) + dc_acc
        db_ref[:, bc_cols] = _mm_tn(dcbm, cm) + db_acc
        dacol_ref[gg] = dacol
        darow_ref[gg] = darow

    def rev(c):
        return nc - 1 - c

    sg = SCAN_GROUPS
    grp = pl.BlockSpec((ln, sg * GROUP_COLS), lambda g, c: (rev(c), g))
    return pl.pallas_call(
        body, name=name, grid=(SCAN_STEPS, nc),
        in_specs=[grp,
                  pl.BlockSpec((ln, sg * SSD_STATE), lambda g, c: (rev(c), B_COL0 // sg + g)),
                  pl.BlockSpec((ln, sg * SSD_STATE), lambda g, c: (rev(c), C_COL0 // sg + g)),
                  pl.BlockSpec((sg, ln, HEADS_PER_GROUP), lambda g, c: (g, rev(c), 0)),
                  pl.BlockSpec((sg, HEADS_PER_GROUP, ln), lambda g, c: (g, 0, rev(c))),
                  pl.BlockSpec((1, sg, GROUP_COLS, SSD_STATE), lambda g, c: (rev(c), g, 0, 0)),
                  grp] + c_in_specs,
        out_specs=[grp,
                   pl.BlockSpec((ln, sg * SSD_STATE), lambda g, c: (rev(c), g)),
                   pl.BlockSpec((ln, sg * SSD_STATE), lambda g, c: (rev(c), g)),
                   pl.BlockSpec((sg, ln, HEADS_PER_GROUP), lambda g, c: (g, rev(c), 0)),
                   pl.BlockSpec((sg, HEADS_PER_GROUP, ln), lambda g, c: (g, 0, rev(c)))] + c_out_specs,
        out_shape=[jax.ShapeDtypeStruct((t, SSD_D_INNER), F32),
                   jax.ShapeDtypeStruct((t, SSD_GROUPS * SSD_STATE), F32),
                   jax.ShapeDtypeStruct((t, SSD_GROUPS * SSD_STATE), F32),
                   jax.ShapeDtypeStruct((SSD_GROUPS, t, HEADS_PER_GROUP), F32),
                   jax.ShapeDtypeStruct((SSD_GROUPS, HEADS_PER_GROUP, t), F32)] + c_out,
        scratch_shapes=[pltpu.VMEM((sg * GROUP_COLS, SSD_STATE), F32)] + (EXCHANGE_SCRATCH if carried else []),
        compiler_params=_cp("arbitrary", "arbitrary"),
    )(xdt, act, act, acum_g, acum_gt, states, dy, *c_in)


GN_ROWS = 128


def _gated_norm_parts(y_ref, xs_ref, z_ref, dsk_ref):
    yy = y_ref[...] + dsk_ref[...] * xs_ref[...]
    z = z_ref[...]
    sz = _sigmoid(z)
    silu = z * sz
    u = yy * silu
    r = lax.rsqrt(jnp.mean(u * u, axis=-1, keepdims=True) + NORM_EPS)
    return yy, z, sz, silu, u, r


def _gated_norm_fwd(y, act, proj, d_exp, g, name):
    t = y.shape[0]
    tm = min(GN_ROWS, t)

    def body(y_ref, xs_ref, z_ref, dsk_ref, g_ref, o_ref):
        _, _, _, _, u, r = _gated_norm_parts(y_ref, xs_ref, z_ref, dsk_ref)
        o_ref[...] = (u * r * g_ref[...]).astype(o_ref.dtype)

    wide = pl.BlockSpec((tm, SSD_D_INNER), lambda i: (i, 0))
    wvec = pl.BlockSpec((1, SSD_D_INNER), lambda i: (0, 0))
    return pl.pallas_call(
        body, name=name, grid=(t // tm,), in_specs=[wide, wide, wide, wvec, wvec], out_specs=wide,
        out_shape=jax.ShapeDtypeStruct((t, SSD_D_INNER), MXU_DTYPE), compiler_params=_cp("parallel"),
    )(y, act, proj, d_exp, g.reshape(1, -1))


def _gated_norm_bwd(y, act, proj, d_exp, g, dn, name):
    t = y.shape[0]
    tm = min(GN_ROWS, t)

    def body(y_ref, xs_ref, z_ref, dsk_ref, g_ref, dn_ref, dyy_ref, dz_ref, dg_ref):
        @pl.when(pl.program_id(0) == 0)
        def _():
            dg_ref[...] = jnp.zeros_like(dg_ref)

        yy, z, sz, silu, u, r = _gated_norm_parts(y_ref, xs_ref, z_ref, dsk_ref)
        un = u * r
        dn = dn_ref[...]
        v = dn * g_ref[...]
        du = r * (v - un * jnp.mean(v * un, axis=-1, keepdims=True))
        dg_ref[...] += jnp.sum(dn * un, axis=0, keepdims=True)
        dyy_ref[...] = du * silu
        dz_ref[...] = (du * yy * (sz * (1.0 + z * (1.0 - sz)))).astype(dz_ref.dtype)

    wide = pl.BlockSpec((tm, SSD_D_INNER), lambda i: (i, 0))
    wvec = pl.BlockSpec((1, SSD_D_INNER), lambda i: (0, 0))
    return pl.pallas_call(
        body, name=name, grid=(t // tm,), in_specs=[wide, wide, wide, wvec, wvec, wide], out_specs=[wide, wide, wvec],
        out_shape=[jax.ShapeDtypeStruct((t, SSD_D_INNER), F32), jax.ShapeDtypeStruct((t, SSD_D_INNER), MXU_DTYPE),
                   jax.ShapeDtypeStruct((1, SSD_D_INNER), F32)],
        compiler_params=_cp("arbitrary"),
    )(y, act, proj, d_exp, g.reshape(1, -1), dn)


SB_PAIRS = SB_HEADS // 2


def _kv_rows(j, bt, nt=1):
    return pl.ds(pl.multiple_of(j * bt, bt), nt * bt)


def _sb_tile_masks(bt):
    lane = _iota((bt, bt), 1)
    rowi = _iota((bt, bt), 0)
    return lane < rowi, (rowi >= lane).astype(MXU_DTYPE), (rowi <= lane).astype(MXU_DTYPE)


def _sb_scaled_heads(pair, scale):
    lane = _iota(pair.shape, 1)
    val = pair.astype(F32) * scale
    return [jnp.where(lane < SB_HEAD_DIM, val, 0.0).astype(pair.dtype), jnp.where(lane >= SB_HEAD_DIM, val, 0.0).astype(pair.dtype)]


def _sb_logits(qs, kb, bt, strict):
    nt = kb.shape[0] // bt
    full = [_mm_nt(q_head, kb) for q_head in qs]
    xs, nlfs = [], []
    for x in full:
        nlf = jnp.maximum(x, 0.0) + jnp.log(1.0 + jnp.exp(-jnp.abs(x)))
        xs.append([x[:, tt * bt:(tt + 1) * bt] for tt in range(nt)])
        tiles = [nlf[:, tt * bt:(tt + 1) * bt] for tt in range(nt)]
        if strict is not None:
            tiles[-1] = jnp.where(strict, tiles[-1], 0.0)
        nlfs.append(tiles)
    return xs, nlfs


def _sb_tails(nlf_tiles, from_j):
    tails, run = [None] * len(nlf_tiles), None
    for tt in reversed(range(len(nlf_tiles))):
        tail = _mm_exact_rhs(nlf_tiles[tt], from_j)
        tails[tt] = tail if run is None else tail + run
        run = tails[tt][:, 0:1]
    return tails


def _sb_heads(e_tiles, upto_j, pre):
    sums, run = [], pre
    for e in e_tiles:
        sums.append(_mm_exact_rhs(e, upto_j) + run)
        run = sums[-1][:, e.shape[1] - 1:e.shape[1]]
    return sums


def _carried_specs(carried):
    if carried is None:
        return [], [], [], []
    src, per_peer = carried
    rows = src.shape[1:] if per_peer else src.shape
    anywhere = pl.BlockSpec(memory_space=pl.ANY)
    return [anywhere], [src], [anywhere], [jax.ShapeDtypeStruct((N_DEV, *rows), src.dtype)]


def _carried_hooks(carried, comm_refs, first, pass_on, last):
    if carried is None:
        return lambda: None
    per_peer = carried[1]

    @pl.when(first)
    def _():
        if per_peer:
            _exchange_start(*comm_refs, per_peer=True)
        else:
            _gather_phase("send", *comm_refs)

    if not per_peer:
        @pl.when(pass_on)
        def _():
            _gather_phase("pass_on", *comm_refs)

    def finish():
        @pl.when(last)
        def _():
            if per_peer:
                _exchange_finish(*comm_refs, per_peer=True)
            else:
                _gather_phase("finish", *comm_refs)

    return finish


def _sb_attention_fwd(qkv, name, carried=None):
    t = qkv.shape[0]
    bt = min(SB_TILE, t)
    nq = t // bt
    c_in_specs, c_in, c_out_specs, c_out = _carried_specs(carried)

    def body(*refs):
        if carried is None:
            q_ref, k_ref, v_ref, o_ref, acc_ref = refs
            comm_refs = None
        else:
            q_ref, k_ref, v_ref, src_ref, o_ref, land_ref, acc_ref, send_sems, recv_sems, local_sem = refs
            comm_refs = (src_ref, land_ref, send_sems, recv_sems, local_sem)
        i = pl.program_id(1)
        last_pair = pl.program_id(0) == SB_PAIRS - 1
        finish = _carried_hooks(carried, comm_refs, (pl.program_id(0) == 0) & (i == 0), last_pair & (i == 0),
                                last_pair & (i == nq - 1))
        strict, from_j, _ = _sb_tile_masks(bt)
        qs = _sb_scaled_heads(q_ref[...], SB_SCALE)
        acc_ref[...] = jnp.zeros_like(acc_ref)

        def block(j, nt, carries, diag):
            rows = _kv_rows(j, bt, nt)
            kb, vb = k_ref[rows, :], v_ref[rows, :]
            xs, nlfs = _sb_logits(qs, kb, bt, strict if diag else None)
            tails = [_sb_tails(nlfs[hh], from_j) for hh in range(2)]
            for hh in range(2):
                ws = [jnp.exp(xs[hh][tt] - tails[hh][tt] - carries[hh]) for tt in range(nt)]
                if diag:
                    ws[-1] = jnp.where(strict, ws[-1], 0.0)
                acc_ref[hh] += _mm(jnp.concatenate([w.astype(MXU_DTYPE) for w in ws], axis=1), vb)
            return tuple(carries[hh] + tails[hh][0][:, 0:1] for hh in range(2))

        zero = jnp.zeros((bt, 1), F32)
        carries = block(i, 1, (zero, zero), True)
        carries = lax.fori_loop(0, i // 2, lambda it, cr: block(i - 2 - 2 * it, 2, cr, False), carries)

        @pl.when(i % 2 == 1)
        def _():
            block(0, 1, carries, False)

        low = _iota((bt, 2 * SB_HEAD_DIM), 1) < SB_HEAD_DIM
        o_ref[...] = jnp.where(low, acc_ref[0], acc_ref[1]).astype(o_ref.dtype)
        finish()

    lanes = 2 * SB_HEAD_DIM
    res = pl.pallas_call(
        body, name=name, grid=(SB_PAIRS, nq),
        in_specs=[pl.BlockSpec((bt, lanes), lambda p, i: (i, p)),
                  pl.BlockSpec((t, lanes), lambda p, i: (0, SB_PAIRS + p)),
                  pl.BlockSpec((t, lanes), lambda p, i: (0, 2 * SB_PAIRS + p))] + c_in_specs,
        out_specs=[pl.BlockSpec((bt, lanes), lambda p, i: (i, p))] + c_out_specs,
        out_shape=[jax.ShapeDtypeStruct((t, D_MODEL), MXU_DTYPE)] + c_out,
        scratch_shapes=[pltpu.VMEM((2, bt, lanes), F32)] + (EXCHANGE_SCRATCH if carried else []),
        compiler_params=_cp("arbitrary", "arbitrary"),
    )(qkv, qkv, qkv, *c_in)
    return res[0] if carried is None else res


def _sb_attention_bwd(qkv, do, name, carried=None):
    t = qkv.shape[0]
    bt = min(SB_TILE, t)
    nq = t // bt
    lanes = 2 * SB_HEAD_DIM
    c_in_specs, c_in, c_out_specs, c_out = _carried_specs(carried)

    def body(*refs):
        if carried is None:
            q_ref, k_ref, v_ref, do_ref, dq_ref, dk_ref, dv_ref, sbuf, ebuf, dq_acc, dk_acc, dv_acc = refs
            comm_refs = None
        else:
            (q_ref, k_ref, v_ref, do_ref, src_ref, dq_ref, dk_ref, dv_ref, land_ref,
             sbuf, ebuf, dq_acc, dk_acc, dv_acc, send_sems, recv_sems, local_sem) = refs
            comm_refs = (src_ref, land_ref, send_sems, recv_sems, local_sem)
        i = pl.program_id(1)
        last_pair = pl.program_id(0) == SB_PAIRS - 1
        finish = _carried_hooks(carried, comm_refs, (pl.program_id(0) == 0) & (i == 0), last_pair & (i == 0),
                                last_pair & (i == nq - 1))

        @pl.when(i == 0)
        def _():
            dk_acc[...] = jnp.zeros_like(dk_acc)
            dv_acc[...] = jnp.zeros_like(dv_acc)

        strict, from_j, upto_j = _sb_tile_masks(bt)
        qs = _sb_scaled_heads(q_ref[...], SB_SCALE)
        dos = _sb_scaled_heads(do_ref[...], 1.0)
        q_both = jnp.concatenate(qs, axis=0)
        do_both = jnp.concatenate(dos, axis=0)
        dq_acc[...] = jnp.zeros_like(dq_acc)

        def pass1(j, nt, carries, diag):
            rows = _kv_rows(j, bt, nt)
            kb, vb = k_ref[rows, :], v_ref[rows, :]
            xs, nlfs = _sb_logits(qs, kb, bt, strict if diag else None)
            dws = [_mm_nt(dos[hh], vb) for hh in range(2)]
            tails = [_sb_tails(nlfs[hh], from_j) for hh in range(2)]
            wcat = []
            for hh in range(2):
                ws = [jnp.exp(xs[hh][tt] - tails[hh][tt] - carries[hh]) for tt in range(nt)]
                if diag:
                    ws[-1] = jnp.where(strict, ws[-1], 0.0)
                w_all = jnp.concatenate(ws, axis=1)
                sbuf[hh, :, rows] = jnp.exp(jnp.concatenate([xs[hh][tt] - nlfs[hh][tt] for tt in range(nt)], axis=1))
                ebuf[hh, :, rows] = w_all * dws[hh]
                wcat.append(w_all.astype(MXU_DTYPE))
            dv_acc[rows, :] += _mm_tn(jnp.concatenate(wcat, axis=0), do_both)
            return tuple(carries[hh] + tails[hh][0][:, 0:1] for hh in range(2))

        zero = jnp.zeros((bt, 1), F32)
        carries = pass1(i, 1, (zero, zero), True)
        carries = lax.fori_loop(0, i // 2, lambda it, cr: pass1(i - 2 - 2 * it, 2, cr, False), carries)

        @pl.when(i % 2 == 1)
        def _():
            pass1(0, 1, carries, False)

        def pass2(j, nt, pres, diag):
            rows = _kv_rows(j, bt, nt)
            kb = k_ref[rows, :]
            sums = [_sb_heads([ebuf[hh, :, _kv_rows(j + tt, bt)] for tt in range(nt)], upto_j, pres[hh]) for hh in range(2)]
            dxm = []
            for hh in range(2):
                dxs = [ebuf[hh, :, _kv_rows(j + tt, bt)] - sbuf[hh, :, _kv_rows(j + tt, bt)] * sums[hh][tt] for tt in range(nt)]
                if diag:
                    dxs[-1] = jnp.where(strict, dxs[-1], 0.0)
                dxm.append(jnp.concatenate(dxs, axis=1).astype(MXU_DTYPE))
                dq_acc[hh] += _mm(dxm[hh], kb)
            dk_acc[rows, :] += _mm_tn(jnp.concatenate(dxm, axis=0), q_both)
            return tuple(sums[hh][-1][:, bt - 1:bt] for hh in range(2))

        pres = lax.fori_loop(0, i // 2, lambda it, pr: pass2(2 * it, 2, pr, False), (zero, zero))

        @pl.when(i % 2 == 0)
        def _():
            pass2(i, 1, pres, True)

        @pl.when(i % 2 == 1)
        def _():
            pass2(i - 1, 2, pres, True)

        low = _iota((bt, lanes), 1) < SB_HEAD_DIM
        dq_ref[...] = (jnp.where(low, dq_acc[0], dq_acc[1]) * SB_SCALE).astype(dq_ref.dtype)

        @pl.when(i == nq - 1)
        def _():
            dk_ref[...] = dk_acc[...].astype(dk_ref.dtype)
            dv_ref[...] = dv_acc[...].astype(dv_ref.dtype)

        finish()

    blk = pl.BlockSpec((bt, lanes), lambda p, i: (i, p))
    whole = pl.BlockSpec((t, lanes), lambda p, i: (0, p))
    out = jax.ShapeDtypeStruct((t, D_MODEL), MXU_DTYPE)
    return pl.pallas_call(
        body, name=name, grid=(SB_PAIRS, nq),
        in_specs=[blk, pl.BlockSpec((t, lanes), lambda p, i: (0, SB_PAIRS + p)),
                  pl.BlockSpec((t, lanes), lambda p, i: (0, 2 * SB_PAIRS + p)), blk] + c_in_specs,
        out_specs=[blk, whole, whole] + c_out_specs, out_shape=[out, out, out] + c_out,
        scratch_shapes=[pltpu.VMEM((2, bt, t), F32), pltpu.VMEM((2, bt, t), F32), pltpu.VMEM((2, bt, lanes), F32),
                        pltpu.VMEM((t, lanes), F32), pltpu.VMEM((t, lanes), F32)] + (EXCHANGE_SCRATCH if carried else []),
        compiler_params=_cp("arbitrary", "arbitrary"),
    )(qkv, qkv, qkv, do, *c_in)


def _add_pair(a, b, name):
    s, r, c = a.shape
    tm = _row_tile(r)

    def body(a_ref, b_ref, o_ref):
        o_ref[...] = (a_ref[...].astype(F32) + b_ref[...].astype(F32)).astype(o_ref.dtype)

    blk = pl.BlockSpec((1, tm, c), lambda q, i: (q, i, 0))
    return pl.pallas_call(body, name=name, grid=(s, r // tm), in_specs=[blk, blk], out_specs=blk,
                          out_shape=jax.ShapeDtypeStruct(a.shape, a.dtype), compiler_params=_cp("parallel", "parallel"))(a, b)


def _sum_slots(gslots, name):
    s, r, c = gslots.shape

    def body(g_ref, o_ref):
        g = g_ref[0].astype(F32)
        for q in range(1, s):
            g = g + g_ref[q].astype(F32)
        o_ref[...] = g

    return pl.pallas_call(
        body, name=name, grid=(c // LANES,),
        in_specs=[pl.BlockSpec((s, r, LANES), lambda j: (0, 0, j))], out_specs=pl.BlockSpec((r, LANES), lambda j: (0, j)),
        out_shape=jax.ShapeDtypeStruct((r, c), F32), compiler_params=_cp("parallel"),
    )(gslots)


def _adamw(gslots, w, m, v, name):
    s, r, c = gslots.shape
    tm = _row_tile(r)
    assert w.shape == (r, c), (w.shape, gslots.shape)
    c1 = 1.0 - ADAM_B1 ** ADAM_STEP
    c2 = 1.0 - ADAM_B2 ** ADAM_STEP

    def body(g_ref, w_ref, m_ref, v_ref, go_ref, d_ref, mo_ref, vo_ref):
        g = g_ref[0].astype(F32)
        for q in range(1, s):
            g = g + g_ref[q].astype(F32)
        mn = ADAM_B1 * m_ref[...] + (1.0 - ADAM_B1) * g
        vn = ADAM_B2 * v_ref[...] + (1.0 - ADAM_B2) * (g * g)
        go_ref[...] = g
        mo_ref[...] = mn
        vo_ref[...] = vn
        d_ref[...] = -ADAM_LR * ((mn / c1) / (jnp.sqrt(vn / c2) + ADAM_EPS) + ADAM_WD * w_ref[...])

    row = pl.BlockSpec((tm, c), lambda i: (i, 0))
    out = jax.ShapeDtypeStruct((r, c), F32)
    return pl.pallas_call(
        body, name=name, grid=(r // tm,),
        in_specs=[pl.BlockSpec((s, tm, c), lambda i: (0, i, 0)), row, row, row],
        out_specs=[row, row, row, row], out_shape=[out, out, out, out], compiler_params=_cp("parallel"),
    )(gslots, w, m, v)


def _rows(a):
    flat = a.reshape(-1)
    pad = (-flat.shape[0]) % PACK_W
    if pad:
        flat = jnp.concatenate([flat, jnp.zeros((pad,), flat.dtype)])
    return flat.reshape(-1, PACK_W)


def _pack(arrays, row_multiple):
    parts, layout, off = [], [], 0
    for a in arrays:
        rw = _rows(a)
        parts.append(rw)
        layout.append((off, rw.shape[0], a.shape))
        off += rw.shape[0]
    pad = (-off) % row_multiple
    if pad:
        parts.append(jnp.zeros((pad, PACK_W), parts[0].dtype))
    return jnp.concatenate(parts, axis=0), layout


def _unpack(packed, layout):
    out = []
    for off, nrows, shape in layout:
        n = int(np.prod(shape))
        out.append(packed[off:off + nrows].reshape(-1)[:n].reshape(shape))
    return out


def _shard_as_rows(name, shard):
    if name in COL_SHARDED:
        shard = shard.transpose(0, 2, 1)
    return shard.reshape(-1, PACK_W)


def _rows_as_shard(name, rows, shape):
    if name in COL_SHARDED:
        lead, k, ns = shape
        return rows.reshape(lead, ns, k).transpose(0, 2, 1)
    return rows.reshape(shape)


def _row_tile(r):
    return next(tm for tm in (256, 128, 64, 32, 16, 8) if r % tm == 0)


def _ssd_consts(dt_bias, a_log, d_skip):
    pad = LANES - SSD_HEADS
    bias = jnp.pad(dt_bias, (0, pad)).reshape(1, LANES)
    a_neg = jnp.pad(-jnp.exp(a_log), (0, pad)).reshape(1, LANES)
    d_exp = jnp.repeat(d_skip, SSD_HEAD_DIM).reshape(1, SSD_D_INNER)
    return bias, a_neg, d_exp


def _group_layouts(acum):
    t = acum.shape[0]
    a = acum[:, :SSD_HEADS].reshape(t, SSD_GROUPS, HEADS_PER_GROUP)
    return a.transpose(1, 0, 2), a.transpose(1, 2, 0)


def _ssd_fwd(x, p, carried=None):
    hn = _rmsnorm(x, p["mix_norm"], "rmsnorm_fwd")
    proj = _matmul(hn, p["w_in"], "nt", F32, "ssd_in_fwd", tm=TOKEN_ROWS, tn=896, tk=1024)
    act = _ssd_conv_fwd(proj, p["conv_w"], p["conv_b"], "ssd_conv_fwd")
    bias, a_neg, d_exp = _ssd_consts(p["dt_bias"], p["a_log"], p["d"])
    expand = _head_expand()
    xdt, dt, acum = _ssd_dt_fwd(proj, act, bias, a_neg, expand, "ssd_dt_fwd")
    acum_g, acum_gt = _group_layouts(acum)
    if carried is None:
        (y, states), landed = _ssd_scan_fwd(xdt, act, acum_g, acum_gt, "ssd_scan_fwd"), None
    else:
        y, states, landed = _ssd_scan_fwd(xdt, act, acum_g, acum_gt, "ssd_scan_fwd_carrying_gather", carried)
    yn = _gated_norm_fwd(y, act, proj, d_exp, p["norm"], "ssd_gnorm_fwd")
    x_new = _matmul(yn, p["w_out"], "nn", F32, "ssd_out_fwd", add=x, tm=TOKEN_ROWS, tn=1024, tk=2048)
    saved = dict(x=x, hn=hn, proj=proj, act=act, xdt=xdt, dt=dt, acum_g=acum_g, acum_gt=acum_gt, y=y, states=states, yn=yn)
    return x_new, saved, landed


def _ssd_bwd(dx, p, s, carried_of=None):
    bias, a_neg, d_exp = _ssd_consts(p["dt_bias"], p["a_log"], p["d"])
    expand = _head_expand()
    dyn = _matmul(dx, p["w_out"], "nt", F32, "ssd_out_dgrad", tm=TOKEN_ROWS, tn=1024, tk=1024)
    g_w_out = _matmul(s["yn"], dx, "tn", MXU_DTYPE, "ssd_out_wgrad", tm=1024, tn=1024, tk=TOKEN_ROWS)
    dyy, dz, g_norm = _gated_norm_bwd(s["y"], s["act"], s["proj"], d_exp, p["norm"], dyn, "ssd_gnorm_bwd")
    scan_args = (s["xdt"], s["act"], s["acum_g"], s["acum_gt"], s["states"], dyy)
    if carried_of is None:
        (dxdt, dbm, dcm, dacol, darow), landed = _ssd_scan_bwd(*scan_args, "ssd_scan_bwd"), None
    else:
        dxdt, dbm, dcm, dacol, darow, landed = _ssd_scan_bwd(*scan_args, "ssd_scan_bwd_carrying_grads", carried_of(g_w_out))
    t = dx.shape[0]
    dacum = dacol.transpose(1, 0, 2).reshape(t, SSD_HEADS) + darow.transpose(2, 0, 1).reshape(t, SSD_HEADS)
    dacum = jnp.pad(dacum, ((0, 0), (0, LANES - SSD_HEADS)))
    dxs, draw, g_a, g_bias, g_dexp = _ssd_dt_bwd(s["proj"], s["act"], s["dt"], dxdt, dyy, dacum, bias, a_neg, d_exp,
                                                  expand, expand.T, "ssd_dt_bwd")
    dact = jnp.concatenate([dxs, dbm, dcm], axis=1)
    dxbc, g_conv_w, g_conv_b = _ssd_conv_bwd(s["proj"], p["conv_w"], p["conv_b"], dact, "ssd_conv_bwd")
    dproj = jnp.concatenate([dz, dxbc, draw], axis=1)
    dhn = _matmul(dproj, p["w_in"], "nn", F32, "ssd_in_dgrad", tm=TOKEN_ROWS, tn=1024, tk=896)
    g_w_in = _matmul(dproj, s["hn"], "tn", MXU_DTYPE, "ssd_in_wgrad", tm=896, tn=1024, tk=TOKEN_ROWS)
    dx_new, g_mix = _rmsnorm_bwd(s["x"], p["mix_norm"], dhn, dx, "rmsnorm_bwd")
    grads = dict(w_in=g_w_in[:SSD_IN_DIM], w_out=g_w_out, conv_w=g_conv_w, conv_b=g_conv_b.reshape(-1),
                 dt_bias=g_bias[0, :SSD_HEADS], a_log=(g_a * a_neg)[0, :SSD_HEADS],
                 d=g_dexp.reshape(SSD_HEADS, SSD_HEAD_DIM).sum(axis=1), norm=g_norm.reshape(-1), mix_norm=g_mix.reshape(-1))
    return dx_new, grads, landed


def _sb_fwd(x, p, carried=None):
    hn = _rmsnorm(x, p["mix_norm"], "rmsnorm_fwd")
    qkv = _matmul(hn, p["w_qkv"], "nt", MXU_DTYPE, "sb_qkv_fwd", tm=TOKEN_ROWS, tn=1024, tk=1024)
    if carried is None:
        o, landed = _sb_attention_fwd(qkv, "sb_attn_fwd"), None
    else:
        o, landed = _sb_attention_fwd(qkv, "sb_attn_fwd_carrying_gather", carried)
    x_new = _matmul(o, p["w_out"], "nn", F32, "sb_out_fwd", add=x, tm=TOKEN_ROWS, tn=1024, tk=1024)
    return x_new, dict(x=x, hn=hn, qkv=qkv, o=o), landed


def _sb_bwd(dx, p, s, carried_of=None):
    do = _matmul(dx, p["w_out"], "nt", MXU_DTYPE, "sb_out_dgrad", tm=TOKEN_ROWS, tn=1024, tk=1024)
    g_w_out = _matmul(s["o"], dx, "tn", MXU_DTYPE, "sb_out_wgrad", tm=1024, tn=1024, tk=TOKEN_ROWS)
    if carried_of is None:
        (dq, dk, dv), landed = _sb_attention_bwd(s["qkv"], do, "sb_attn_bwd"), None
    else:
        dq, dk, dv, landed = _sb_attention_bwd(s["qkv"], do, "sb_attn_bwd_carrying_grads", carried_of(g_w_out))
    dqkv = jnp.concatenate([dq, dk, dv], axis=1)
    dhn = _matmul(dqkv, p["w_qkv"], "nn", F32, "sb_qkv_dgrad", tm=TOKEN_ROWS, tn=1024, tk=1024)
    g_w_qkv = _matmul(dqkv, s["hn"], "tn", MXU_DTYPE, "sb_qkv_wgrad", tm=1024, tn=1024, tk=TOKEN_ROWS)
    dx_new, g_mix = _rmsnorm_bwd(s["x"], p["mix_norm"], dhn, dx, "rmsnorm_bwd")
    return dx_new, dict(w_qkv=g_w_qkv, w_out=g_w_out, mix_norm=g_mix.reshape(-1)), landed


def _ffn_fwd(x, p):
    hn = _rmsnorm(x, p["ffn_norm"], "rmsnorm_fwd")
    proj = _matmul(hn, p["w_in"], "nt", F32, "ffn_in_fwd", tm=TOKEN_ROWS, tn=1408, tk=1024)
    act = _ffn_conv_fwd(proj, p["conv_w"], p["conv_b"], "ffn_conv_fwd")
    x_new = _matmul(act, p["w_out"], "nn", F32, "ffn_out_fwd", add=x, tm=TOKEN_ROWS, tn=1024, tk=1408)
    return x_new, dict(x=x, hn=hn, proj=proj, act=act)


def _ffn_bwd(dx, p, s):
    dact = _matmul(dx, p["w_out"], "nt", F32, "ffn_out_dgrad", tm=TOKEN_ROWS, tn=1408, tk=1024)
    g_w_out = _matmul(s["act"], dx, "tn", MXU_DTYPE, "ffn_out_wgrad", tm=1408, tn=1024, tk=TOKEN_ROWS)
    dpg, dpu, dwg, dwu, dbg, dbu = _ffn_conv_bwd(s["proj"], p["conv_w"], p["conv_b"], dact, "ffn_conv_bwd")
    dproj = jnp.concatenate([dpg, dpu], axis=1)
    dhn = _matmul(dproj, p["w_in"], "nn", F32, "ffn_in_dgrad", tm=TOKEN_ROWS, tn=1024, tk=1408)
    g_w_in = _matmul(dproj, s["hn"], "tn", MXU_DTYPE, "ffn_in_wgrad", tm=1408, tn=1024, tk=TOKEN_ROWS)
    dx_new, g_norm = _rmsnorm_bwd(s["x"], p["ffn_norm"], dhn, dx, "rmsnorm_bwd")
    grads = dict(w_in=g_w_in, w_out=g_w_out, conv_w=jnp.concatenate([dwg, dwu], axis=1),
                 conv_b=jnp.concatenate([dbg, dbu], axis=1).reshape(-1), ffn_norm=g_norm.reshape(-1))
    return dx_new, grads


ADD_ROWS = 256
BIG = ["ssd_w_in", "sb_w_qkv", "ffn_w_in", "ssd_w_out", "sb_w_out", "ffn_w_out"]
LAYER_PIECES = [[("ssd_w_in", 0), ("ssd_w_out", 0), ("ffn_w_in", 0), ("ffn_w_out", 0)],
                [("sb_w_qkv", 0), ("sb_w_out", 0), ("ffn_w_in", 1), ("ffn_w_out", 1)],
                [("ssd_w_in", 1), ("ssd_w_out", 1), ("ffn_w_in", 2), ("ffn_w_out", 2)],
                [("sb_w_qkv", 1), ("sb_w_out", 1), ("ffn_w_in", 3), ("ffn_w_out", 3)]]
GRAD_SETS = {3: [("ffn_w_in", 3), ("ffn_w_out", 3), ("sb_w_out", 1)],
             1: [("sb_w_qkv", 1), ("ssd_w_out", 1), ("ffn_w_in", 2), ("ffn_w_out", 2), ("ffn_w_in", 1), ("ffn_w_out", 1),
                 ("sb_w_out", 0), ("ssd_w_in", 1)],
             0: [("sb_w_qkv", 0), ("ffn_w_in", 0), ("ffn_w_out", 0), ("ssd_w_out", 0)],
             "end": [("ssd_w_in", 0)]}
GATHER_SETS = {"early": [("ssd_w_out", 0), ("ssd_w_in", 0)],
               0: [("ffn_w_in", 0), ("ffn_w_out", 0), ("sb_w_qkv", 0), ("sb_w_out", 0)],
               1: [("ffn_w_in", 1), ("ffn_w_out", 1), ("ssd_w_out", 1), ("ffn_w_in", 2), ("ffn_w_out", 2)] + LAYER_PIECES[3]
                  + [("ssd_w_in", 1)]}
COL_SHARDED = {"ssd_w_in": 2, "sb_w_qkv": 2, "ffn_w_in": 4}
CONV = ["ssd_conv_w", "ffn_conv_w"]
SMALL = ["mix_norm", "ffn_norm", "final_norm", "ssd_conv_b", "ssd_dt_bias", "ssd_a_log", "ssd_d", "ssd_norm", "ffn_conv_b"]
WEIGHTS = ["mix_norm", "ffn_norm", "final_norm", "ssd_w_in", "ssd_conv_w", "ssd_conv_b", "ssd_dt_bias", "ssd_a_log", "ssd_d",
           "ssd_norm", "ssd_w_out", "sb_w_qkv", "sb_w_out", "ffn_w_in", "ffn_conv_w", "ffn_conv_b", "ffn_w_out"]


def _step(x, loss_target, w, m, v):
    x = x.reshape(x.shape[-2], x.shape[-1])
    target = loss_target.reshape(x.shape)
    dev = 4 * lax.axis_index("x") + 2 * lax.axis_index("y") + lax.axis_index("c")
    core = lax.axis_index("c")

    shard_rows = {n: _shard_as_rows(n, w[n].astype(MXU_DTYPE)) for n in BIG}
    per_shard = {n: shard_rows[n].shape[0] // w[n].shape[0] for n in BIG}

    def layout(pieces):
        where, off = {}, 0
        for n, l in pieces:
            where[(n, l)] = (off, per_shard[n])
            off += per_shard[n]
        return where

    def pack_pieces(pieces, rows_of):
        return jnp.concatenate([rows_of(piece) for piece in pieces], axis=-2)

    def shard_piece(piece):
        n, l = piece
        return shard_rows[n][l * per_shard[n]:(l + 1) * per_shard[n]]

    full = {}

    def unpack_weights(gathered, where):
        for (n, l), (off, rows) in where.items():
            mat = gathered[:, off:off + rows].reshape(N_DEV * rows, PACK_W)
            if n == "ssd_w_in":
                mat = jnp.pad(mat, ((0, SSD_IN_PAD - SSD_IN_DIM), (0, 0)))
            full[(n, l)] = mat

    unpack_weights(_all_gather(pack_pieces(GATHER_SETS["early"], shard_piece), "gather_weights_early"), layout(GATHER_SETS["early"]))
    conv_pack, conv_layout = _pack([w[n] for n in CONV], 8)
    conv_all = _all_gather(conv_pack, "gather_conv_taps")
    for n, (off, nrows, shape) in zip(CONV, conv_layout):
        parts = [_unpack(conv_all[j], conv_layout)[CONV.index(n)] for j in range(N_DEV)]
        full[n] = jnp.concatenate(parts, axis=-1)

    def ssd_params(j):
        return dict(mix_norm=w["mix_norm"][2 * j], w_in=full[("ssd_w_in", j)], conv_w=full["ssd_conv_w"][j],
                    conv_b=w["ssd_conv_b"][j], dt_bias=w["ssd_dt_bias"][j], a_log=w["ssd_a_log"][j], d=w["ssd_d"][j],
                    norm=w["ssd_norm"][j], w_out=full[("ssd_w_out", j)])

    def sb_params(j):
        return dict(mix_norm=w["mix_norm"][2 * j + 1], w_qkv=full[("sb_w_qkv", j)], w_out=full[("sb_w_out", j)])

    def ffn_params(i):
        return dict(ffn_norm=w["ffn_norm"][i], w_in=full[("ffn_w_in", i)], conv_w=full["ffn_conv_w"][i],
                    conv_b=w["ffn_conv_b"][i], w_out=full[("ffn_w_out", i)])

    saved = []
    for i in range(DEPTH):
        mixer_fwd, params = (_ssd_fwd, ssd_params) if i % 2 == 0 else (_sb_fwd, sb_params)
        if i in GATHER_SETS:
            x, s_mix, arrived = mixer_fwd(x, params(i // 2), carried=(pack_pieces(GATHER_SETS[i], shard_piece), False))
            unpack_weights(arrived, layout(GATHER_SETS[i]))
        else:
            x, s_mix, _ = mixer_fwd(x, params(i // 2))
        x, s_ffn = _ffn_fwd(x, ffn_params(i))
        saved.append((s_mix, s_ffn))
    dx, g_final, loss_part = _final_norm_loss(x, w["final_norm"], target, "final_norm_loss")

    piece_grad = {}

    def grad_piece(piece):
        g = piece_grad[piece]
        return g.reshape(N_DEV, g.shape[0] // N_DEV, PACK_W)

    def carried_set(pieces, own_piece):
        def make(g_w_out):
            piece_grad[own_piece] = g_w_out
            return pack_pieces(pieces, grad_piece), True
        return make

    g_mix, g_ffn, g_ssd, g_sb = [None] * DEPTH, [None] * DEPTH, [None] * 2, [None] * 2
    landed = {}
    for i in reversed(range(DEPTH)):
        s_mix, s_ffn = saved[i]
        dx, g_ffn[i] = _ffn_bwd(dx, ffn_params(i), s_ffn)
        piece_grad[("ffn_w_in", i)], piece_grad[("ffn_w_out", i)] = g_ffn[i]["w_in"], g_ffn[i]["w_out"]
        j = i // 2
        if i % 2 == 0:
            carried_of = carried_set(GRAD_SETS[i], ("ssd_w_out", j)) if i in GRAD_SETS else None
            dx, g_ssd[j], landed[i] = _ssd_bwd(dx, ssd_params(j), s_mix, carried_of)
            g_mix[i] = g_ssd[j]["mix_norm"]
            piece_grad[("ssd_w_in", j)], piece_grad[("ssd_w_out", j)] = g_ssd[j]["w_in"], g_ssd[j]["w_out"]
        else:
            dx, g_sb[j], landed[i] = _sb_bwd(dx, sb_params(j), s_mix, carried_set(GRAD_SETS[i], ("sb_w_out", j)))
            g_mix[i] = g_sb[j]["mix_norm"]
            piece_grad[("sb_w_qkv", j)] = g_sb[j]["w_qkv"]
    grad_x = dx.reshape(1, *dx.shape)

    g8 = pack_pieces(GRAD_SETS["end"], grad_piece)
    g8 = jnp.pad(g8, ((0, 0), (0, (-g8.shape[1]) % ADD_ROWS), (0, 0)))
    g8 = g8.reshape(4, 2, *g8.shape[1:])
    keep = lax.dynamic_index_in_dim(g8, core, axis=1, keepdims=False)
    give = lax.dynamic_index_in_dim(g8, 1 - core, axis=1, keepdims=False)
    got = _swap_with_sibling(give, "grads_to_sibling")
    chip_part = _add_pair(keep, got, "grads_add_sibling")
    landed["end"] = _exchange_chips(chip_part, "grads_across_chips")

    summed = {}
    for key, pieces in GRAD_SETS.items():
        total = _sum_slots(landed[key], "grads_sum_landed")
        for piece, (off, rows) in layout(pieces).items():
            summed[piece] = total[off:off + rows]
    big_res = [dict() for _ in range(4)]
    for n in BIG:
        lead, rows, cols = w[n].shape
        g_rows = jnp.concatenate([summed[(n, l)] for l in range(lead)], axis=0)
        g_nat = _rows_as_shard(n, g_rows, w[n].shape).reshape(1, lead * rows, cols)
        two_d = (lead * rows, cols)
        outs = _adamw(g_nat, w[n].reshape(two_d), m[n].reshape(two_d), v[n].reshape(two_d), "adamw_" + n)
        for kind in range(4):
            big_res[kind][n] = outs[kind].reshape(w[n].shape)

    small_g = {
        "mix_norm": jnp.stack(g_mix), "ffn_norm": jnp.stack([g["ffn_norm"] for g in g_ffn]), "final_norm": g_final.reshape(-1),
        "ssd_conv_b": jnp.stack([g["conv_b"] for g in g_ssd]), "ssd_dt_bias": jnp.stack([g["dt_bias"] for g in g_ssd]),
        "ssd_a_log": jnp.stack([g["a_log"] for g in g_ssd]), "ssd_d": jnp.stack([g["d"] for g in g_ssd]),
        "ssd_norm": jnp.stack([g["norm"] for g in g_ssd]), "ffn_conv_b": jnp.stack([g["conv_b"] for g in g_ffn]),
    }
    conv_g = {"ssd_conv_w": jnp.stack([g["conv_w"] for g in g_ssd]), "ffn_conv_w": jnp.stack([g["conv_w"] for g in g_ffn])}
    extra = [conv_g[n] for n in CONV] + [loss_part]
    small_pack, small_layout = _pack([small_g[n] for n in SMALL] + extra, 8)
    small_all = _all_gather(small_pack, "gather_small_grads")
    zeros_like = [jnp.zeros(a.shape, F32) for a in extra]
    sw, _ = _pack([w[n] for n in SMALL] + zeros_like, 8)
    sm, _ = _pack([m[n] for n in SMALL] + zeros_like, 8)
    sv, _ = _pack([v[n] for n in SMALL] + [jnp.ones(a.shape, F32) for a in extra], 8)
    small_out = _adamw(small_all, sw, sm, sv, "adamw_replicated")
    small_res = [_unpack(o, small_layout) for o in small_out]
    summed = small_res[0]
    loss = summed[-1][0, 0]
    conv_shard_g = []
    for n, gsum in zip(CONV, summed[len(SMALL):len(SMALL) + len(CONV)]):
        ns = w[n].shape[-1]
        conv_shard_g.append(lax.dynamic_slice_in_dim(gsum, dev * ns, ns, axis=2))
    cg, conv_sh_layout = _pack(conv_shard_g, 8)
    cw, _ = _pack([w[n] for n in CONV], 8)
    cm_, _ = _pack([m[n] for n in CONV], 8)
    cv, _ = _pack([v[n] for n in CONV], 8)
    conv_out = _adamw(cg.reshape(1, *cg.shape), cw, cm_, cv, "adamw_conv_taps")
    conv_res = [dict(zip(CONV, _unpack(o, conv_sh_layout))) for o in conv_out]

    def pick(kind, n):
        if n in BIG:
            return big_res[kind][n]
        if n in CONV:
            return conv_res[kind][n]
        return small_res[kind][SMALL.index(n)]

    outs = [loss, grad_x]
    for kind in range(4):
        outs += [pick(kind, n) for n in WEIGHTS]
    return tuple(outs)


def kernel(x, mix_norm, ffn_norm, final_norm, ssd_w_in, ssd_conv_w, ssd_conv_b, ssd_dt_bias, ssd_a_log, ssd_d, ssd_norm, ssd_w_out, sb_w_qkv, sb_w_out, ffn_w_in, ffn_conv_w, ffn_conv_b, ffn_w_out, loss_target, m_mix_norm, m_ffn_norm, m_final_norm, m_ssd_w_in, m_ssd_conv_w, m_ssd_conv_b, m_ssd_dt_bias, m_ssd_a_log, m_ssd_d, m_ssd_norm, m_ssd_w_out, m_sb_w_qkv, m_sb_w_out, m_ffn_w_in, m_ffn_conv_w, m_ffn_conv_b, m_ffn_w_out, v_mix_norm, v_ffn_norm, v_final_norm, v_ssd_w_in, v_ssd_conv_w, v_ssd_conv_b, v_ssd_dt_bias, v_ssd_a_log, v_ssd_d, v_ssd_norm, v_ssd_w_out, v_sb_w_qkv, v_sb_w_out, v_ffn_w_in, v_ffn_conv_w, v_ffn_conv_b, v_ffn_w_out):
    w = dict(mix_norm=mix_norm, ffn_norm=ffn_norm, final_norm=final_norm, ssd_w_in=ssd_w_in, ssd_conv_w=ssd_conv_w,
             ssd_conv_b=ssd_conv_b, ssd_dt_bias=ssd_dt_bias, ssd_a_log=ssd_a_log, ssd_d=ssd_d, ssd_norm=ssd_norm,
             ssd_w_out=ssd_w_out, sb_w_qkv=sb_w_qkv, sb_w_out=sb_w_out, ffn_w_in=ffn_w_in, ffn_conv_w=ffn_conv_w,
             ffn_conv_b=ffn_conv_b, ffn_w_out=ffn_w_out)
    m = dict(mix_norm=m_mix_norm, ffn_norm=m_ffn_norm, final_norm=m_final_norm, ssd_w_in=m_ssd_w_in, ssd_conv_w=m_ssd_conv_w,
             ssd_conv_b=m_ssd_conv_b, ssd_dt_bias=m_ssd_dt_bias, ssd_a_log=m_ssd_a_log, ssd_d=m_ssd_d, ssd_norm=m_ssd_norm,
             ssd_w_out=m_ssd_w_out, sb_w_qkv=m_sb_w_qkv, sb_w_out=m_sb_w_out, ffn_w_in=m_ffn_w_in, ffn_conv_w=m_ffn_conv_w,
             ffn_conv_b=m_ffn_conv_b, ffn_w_out=m_ffn_w_out)
    v = dict(mix_norm=v_mix_norm, ffn_norm=v_ffn_norm, final_norm=v_final_norm, ssd_w_in=v_ssd_w_in, ssd_conv_w=v_ssd_conv_w,
             ssd_conv_b=v_ssd_conv_b, ssd_dt_bias=v_ssd_dt_bias, ssd_a_log=v_ssd_a_log, ssd_d=v_ssd_d, ssd_norm=v_ssd_norm,
             ssd_w_out=v_ssd_w_out, sb_w_qkv=v_sb_w_qkv, sb_w_out=v_sb_w_out, ffn_w_in=v_ffn_w_in, ffn_conv_w=v_ffn_conv_w,
             ffn_conv_b=v_ffn_conv_b, ffn_w_out=v_ffn_w_out)
    return _step(x, loss_target, w, m, v)
```

```python
import jax
import jax.numpy as jnp
import numpy as np
from jax import lax
from jax.experimental import pallas as pl
from jax.experimental.pallas import tpu as pltpu

F32 = jnp.float32
MXU_DTYPE = jnp.bfloat16
MESH_ID = pl.DeviceIdType.MESH
N_DEV = 8

NORM_EPS = 1e-6
D_MODEL = 1024
DEPTH = 4
SSD_D_INNER = 2048
SSD_HEADS = 32
SSD_HEAD_DIM = 64
SSD_GROUPS = 8
SSD_STATE = 128
SSD_CONV = 4
SSD_CHUNK = 128
SSD_CONV_DIM = SSD_D_INNER + 2 * SSD_GROUPS * SSD_STATE
SSD_IN_DIM = SSD_D_INNER + SSD_CONV_DIM + SSD_HEADS
LANES = 128
SSD_IN_PAD = SSD_D_INNER + SSD_CONV_DIM + LANES
SB_HEADS = 16
SB_HEAD_DIM = 64
SB_TILE = 256
SB_SCALE = SB_HEAD_DIM ** -0.5
FFN_D_FF = 2816
FFN_CONV = 3
PACK_W = 1024

ADAM_LR = 0.001
ADAM_B1 = 0.9
ADAM_B2 = 0.999
ADAM_EPS = 1e-08
ADAM_WD = 0.01
ADAM_STEP = 10

VMEM_LIMIT_BYTES = 56 * 1024 * 1024


def _cp(*sem):
    return pltpu.CompilerParams(dimension_semantics=sem, vmem_limit_bytes=VMEM_LIMIT_BYTES)


def _iota(shape, dim):
    return lax.broadcasted_iota(jnp.int32, shape, dim)


def _sigmoid(x):
    return 1.0 / (1.0 + jnp.exp(-x))


def _mm(a, b):
    return lax.dot_general(a, b, (((1,), (0,)), ((), ())), preferred_element_type=F32)


def _mm_nt(a, b):
    return lax.dot_general(a, b, (((1,), (1,)), ((), ())), preferred_element_type=F32)


def _mm_tn(a, b):
    return lax.dot_general(a, b, (((0,), (0,)), ((), ())), preferred_element_type=F32)


def _split(x):
    hi = x.astype(MXU_DTYPE)
    lo = (x - hi.astype(F32)).astype(MXU_DTYPE)
    return hi, lo


def _mm_exact_rhs(x, m):
    hi, lo = _split(x)
    return _mm(jnp.concatenate([hi, lo], axis=1), jnp.concatenate([m, m], axis=0))


def _mm_exact_lhs(m, x):
    hi, lo = _split(x)
    return _mm(jnp.concatenate([m, m], axis=1), jnp.concatenate([hi, lo], axis=0))


def _my_place():
    return lax.axis_index("x"), lax.axis_index("y"), lax.axis_index("c")


def _gather_phase(phase, x_ref, out_ref, send_sems, recv_sems, local_sem):
    x, y, c = _my_place()
    me, sibling = (x, y, c), (x, y, 1 - c)
    chips = [(1 - x, y), (x, 1 - y), (1 - x, 1 - y)]

    def slot(px, py, pc):
        return out_ref.at[4 * px + 2 * py + pc]

    def copy(k, block, to, src=None):
        return pltpu.make_async_remote_copy(
            src_ref=slot(*block) if src is None else src, dst_ref=slot(*block),
            send_sem=send_sems.at[k], recv_sem=recv_sems.at[k], device_id=to, device_id_type=MESH_ID)

    if phase == "send":
        pltpu.make_async_copy(x_ref, slot(*me), local_sem).start()
        copy(0, me, sibling, src=x_ref).start()
        for j, chip in enumerate(chips):
            copy(1 + j, me, (*chip, c), src=x_ref).start()
    elif phase == "pass_on":
        for j, chip in enumerate(chips):
            copy(1 + j, (*chip, c), me).wait_recv()
            copy(4 + j, (*chip, c), sibling).start()
    else:
        copy(0, sibling, me).wait_recv()
        for j, chip in enumerate(chips):
            copy(4 + j, (*chip, 1 - c), me).wait_recv()
        copy(0, me, sibling, src=x_ref).wait_send()
        for j, chip in enumerate(chips):
            copy(1 + j, me, (*chip, c), src=x_ref).wait_send()
            copy(4 + j, (*chip, c), sibling).wait_send()
        pltpu.make_async_copy(x_ref, slot(*me), local_sem).wait()


def _all_gather(shard, name):
    r, c_ = shard.shape

    def body(*refs):
        _gather_phase("send", *refs)
        _gather_phase("pass_on", *refs)
        _gather_phase("finish", *refs)

    return pl.pallas_call(
        body, name=name,
        out_shape=jax.ShapeDtypeStruct((N_DEV, r, c_), shard.dtype),
        in_specs=[pl.BlockSpec(memory_space=pl.ANY)],
        out_specs=pl.BlockSpec(memory_space=pl.ANY),
        scratch_shapes=[pltpu.SemaphoreType.DMA((7,)), pltpu.SemaphoreType.DMA((7,)), pltpu.SemaphoreType.DMA(())],
    )(shard)


def _swap_with_sibling(buf, name):
    def body(x_ref, out_ref, send_sem, recv_sem):
        x, y, c = _my_place()
        cp = pltpu.make_async_remote_copy(src_ref=x_ref, dst_ref=out_ref, send_sem=send_sem, recv_sem=recv_sem,
                                          device_id=(x, y, 1 - c), device_id_type=MESH_ID)
        cp.start()
        cp.wait()

    return pl.pallas_call(
        body, name=name, out_shape=jax.ShapeDtypeStruct(buf.shape, buf.dtype),
        in_specs=[pl.BlockSpec(memory_space=pl.ANY)], out_specs=pl.BlockSpec(memory_space=pl.ANY),
        scratch_shapes=[pltpu.SemaphoreType.DMA(()), pltpu.SemaphoreType.DMA(())],
    )(buf)


def _exchange_chips(parts, name):
    def body(p_ref, out_ref, send_sems, recv_sems, local_sem):
        x, y, c = _my_place()
        my_q = 2 * x + y
        chips = [(1 - x, y), (x, 1 - y), (1 - x, 1 - y)]
        local = pltpu.make_async_copy(p_ref.at[my_q], out_ref.at[my_q], local_sem)
        local.start()

        def copy(k, px, py):
            return pltpu.make_async_remote_copy(
                src_ref=p_ref.at[2 * px + py], dst_ref=out_ref.at[my_q],
                send_sem=send_sems.at[k], recv_sem=recv_sems.at[k], device_id=(px, py, c), device_id_type=MESH_ID)

        def landing(k, px, py):
            return pltpu.make_async_remote_copy(
                src_ref=p_ref.at[my_q], dst_ref=out_ref.at[2 * px + py],
                send_sem=send_sems.at[k], recv_sem=recv_sems.at[k], device_id=(px, py, c), device_id_type=MESH_ID)

        sends = [copy(k, px, py) for k, (px, py) in enumerate(chips)]
        for cp in sends:
            cp.start()
        for k, (px, py) in enumerate(chips):
            landing(k, px, py).wait_recv()
        for cp in sends:
            cp.wait_send()
        local.wait()

    return pl.pallas_call(
        body, name=name, out_shape=jax.ShapeDtypeStruct(parts.shape, parts.dtype),
        in_specs=[pl.BlockSpec(memory_space=pl.ANY)], out_specs=pl.BlockSpec(memory_space=pl.ANY),
        scratch_shapes=[pltpu.SemaphoreType.DMA((3,)), pltpu.SemaphoreType.DMA((3,)), pltpu.SemaphoreType.DMA(())],
    )(parts)


RELATIONS = [(0, 0, 1), (1, 0, 0), (0, 1, 0), (1, 1, 0), (1, 0, 1), (0, 1, 1), (1, 1, 1)]
EXCHANGE_SCRATCH = [pltpu.SemaphoreType.DMA((len(RELATIONS),)), pltpu.SemaphoreType.DMA((len(RELATIONS),)),
                    pltpu.SemaphoreType.DMA(())]


def _exchange_copies(src_ref, land_ref, send_sems, recv_sems, local_sem, per_peer, incoming=True):
    x, y, c = _my_place()
    me = 4 * x + 2 * y + c

    def src(j):
        return src_ref.at[j] if per_peer else src_ref

    local = pltpu.make_async_copy(src(me), land_ref.at[me], local_sem)
    pairs = []
    for k, (dx, dy, dc) in enumerate(RELATIONS):
        peer = (1 - x if dx else x, 1 - y if dy else y, 1 - c if dc else c)
        j = 4 * peer[0] + 2 * peer[1] + peer[2]
        sems = dict(send_sem=send_sems.at[k], recv_sem=recv_sems.at[k], device_id=peer, device_id_type=MESH_ID)
        pairs.append((pltpu.make_async_remote_copy(src_ref=src(j), dst_ref=land_ref.at[me], **sems),
                      pltpu.make_async_remote_copy(src_ref=src(me), dst_ref=land_ref.at[j], **sems) if incoming else None))
    return local, pairs


def _exchange_start(*refs, per_peer):
    local, pairs = _exchange_copies(*refs, per_peer, incoming=False)
    local.start()
    for outgoing, _ in pairs:
        outgoing.start()


def _exchange_finish(*refs, per_peer):
    local, pairs = _exchange_copies(*refs, per_peer)
    for _, incoming in pairs:
        incoming.wait_recv()
    for outgoing, _ in pairs:
        outgoing.wait_send()
    local.wait()


TOKEN_ROWS = 1024


def _matmul(a, b, mode, out_dtype, name, add=None, tm=512, tn=512, tk=512):
    if mode == "nn":
        (m, k), (k2, n) = a.shape, b.shape
    elif mode == "nt":
        (m, k), (n, k2) = a.shape, b.shape
    else:
        (k, m), (k2, n) = a.shape, b.shape
    assert k == k2, (a.shape, b.shape, mode)
    tm, tn, tk = min(tm, m), min(tn, n), min(tk, k)
    assert m % tm == 0 and n % tn == 0 and k % tk == 0, (m, n, k, tm, tn, tk)
    nk = k // tk
    mm = {"nn": _mm, "nt": _mm_nt, "tn": _mm_tn}[mode]

    def body(*refs):
        if add is None:
            a_ref, b_ref, o_ref, acc_ref = refs
        else:
            a_ref, b_ref, add_ref, o_ref, acc_ref = refs
        kk = pl.program_id(2)

        @pl.when(kk == 0)
        def _():
            acc_ref[...] = jnp.zeros_like(acc_ref)

        acc_ref[...] += mm(a_ref[...].astype(MXU_DTYPE), b_ref[...].astype(MXU_DTYPE))

        @pl.when(kk == nk - 1)
        def _():
            res = acc_ref[...]
            if add is not None:
                res = res + add_ref[...]
            o_ref[...] = res.astype(o_ref.dtype)

    a_spec = {"nn": pl.BlockSpec((tm, tk), lambda i, j, kk: (i, kk)),
              "nt": pl.BlockSpec((tm, tk), lambda i, j, kk: (i, kk)),
              "tn": pl.BlockSpec((tk, tm), lambda i, j, kk: (kk, i))}[mode]
    b_spec = {"nn": pl.BlockSpec((tk, tn), lambda i, j, kk: (kk, j)),
              "nt": pl.BlockSpec((tn, tk), lambda i, j, kk: (j, kk)),
              "tn": pl.BlockSpec((tk, tn), lambda i, j, kk: (kk, j))}[mode]
    o_spec = pl.BlockSpec((tm, tn), lambda i, j, kk: (i, j))
    in_specs, args = [a_spec, b_spec], [a, b]
    if add is not None:
        in_specs.append(o_spec)
        args.append(add)
    return pl.pallas_call(
        body, name=name, grid=(m // tm, n // tn, nk), in_specs=in_specs, out_specs=o_spec,
        out_shape=jax.ShapeDtypeStruct((m, n), out_dtype),
        scratch_shapes=[pltpu.VMEM((tm, tn), F32)],
        compiler_params=_cp("parallel", "parallel", "arbitrary"),
    )(*args)


def _rmsnorm(x, g, name):
    t, d = x.shape
    tm = min(512, t)

    def body(x_ref, g_ref, o_ref):
        xv = x_ref[...]
        r = lax.rsqrt(jnp.mean(xv * xv, axis=-1, keepdims=True) + NORM_EPS)
        o_ref[...] = (xv * r * g_ref[...]).astype(o_ref.dtype)

    return pl.pallas_call(
        body, name=name, grid=(t // tm,),
        in_specs=[pl.BlockSpec((tm, d), lambda i: (i, 0)), pl.BlockSpec((1, d), lambda i: (0, 0))],
        out_specs=pl.BlockSpec((tm, d), lambda i: (i, 0)),
        out_shape=jax.ShapeDtypeStruct((t, d), MXU_DTYPE), compiler_params=_cp("parallel"),
    )(x, g.reshape(1, d))


def _rmsnorm_bwd(x, g, dh, dres, name):
    t, d = x.shape
    tm = min(512, t)

    def body(x_ref, g_ref, dh_ref, dres_ref, dx_ref, dg_ref):
        @pl.when(pl.program_id(0) == 0)
        def _():
            dg_ref[...] = jnp.zeros_like(dg_ref)

        xv = x_ref[...]
        r = lax.rsqrt(jnp.mean(xv * xv, axis=-1, keepdims=True) + NORM_EPS)
        xn = xv * r
        dhv = dh_ref[...]
        u = dhv * g_ref[...]
        dx_ref[...] = dres_ref[...] + r * (u - xn * jnp.mean(u * xn, axis=-1, keepdims=True))
        dg_ref[...] += jnp.sum(dhv * xn, axis=0, keepdims=True)

    row = pl.BlockSpec((tm, d), lambda i: (i, 0))
    vec = pl.BlockSpec((1, d), lambda i: (0, 0))
    return pl.pallas_call(
        body, name=name, grid=(t // tm,), in_specs=[row, vec, row, row], out_specs=[row, vec],
        out_shape=[jax.ShapeDtypeStruct((t, d), F32), jax.ShapeDtypeStruct((1, d), F32)],
        compiler_params=_cp("arbitrary"),
    )(x, g.reshape(1, d), dh, dres)


def _final_norm_loss(x, g, target, name):
    t, d = x.shape
    tm = min(512, t)

    def body(x_ref, g_ref, t_ref, dx_ref, dg_ref, loss_ref):
        @pl.when(pl.program_id(0) == 0)
        def _():
            dg_ref[...] = jnp.zeros_like(dg_ref)
            loss_ref[...] = jnp.zeros_like(loss_ref)

        xv = x_ref[...]
        gv = g_ref[...]
        r = lax.rsqrt(jnp.mean(xv * xv, axis=-1, keepdims=True) + NORM_EPS)
        xn = xv * r
        err = xn * gv - t_ref[...]
        per_tok = jnp.mean(err * err, axis=-1, keepdims=True)
        loss_ref[...] += jnp.broadcast_to(0.5 * jnp.sum(per_tok, axis=0, keepdims=True), loss_ref.shape)
        dy = err * (1.0 / d)
        u = dy * gv
        dx_ref[...] = r * (u - xn * jnp.mean(u * xn, axis=-1, keepdims=True))
        dg_ref[...] += jnp.sum(dy * xn, axis=0, keepdims=True)

    row = pl.BlockSpec((tm, d), lambda i: (i, 0))
    vec = pl.BlockSpec((1, d), lambda i: (0, 0))
    return pl.pallas_call(
        body, name=name, grid=(t // tm,), in_specs=[row, vec, row],
        out_specs=[row, vec, pl.BlockSpec((1, LANES), lambda i: (0, 0))],
        out_shape=[jax.ShapeDtypeStruct((t, d), F32), jax.ShapeDtypeStruct((1, d), F32),
                   jax.ShapeDtypeStruct((1, LANES), F32)],
        compiler_params=_cp("arbitrary"),
    )(x, g.reshape(1, d), target)


CONV_COLS = 128


def _shifts_down(p, width):
    row = _iota(p.shape, 0)
    return [jnp.where(row >= s, pltpu.roll(p, s, axis=0), 0.0) for s in range(1, width)]


def _shifts_up(p, width):
    n = p.shape[0]
    row = _iota(p.shape, 0)
    return [jnp.where(row < n - s, pltpu.roll(p, n - s, axis=0), 0.0) for s in range(1, width)]


def _conv_pre(p, shifted, w_ref, b_ref):
    width = w_ref.shape[0]
    u = b_ref[...] + w_ref[width - 1:width, :] * p
    for s in range(1, width):
        u = u + w_ref[width - 1 - s:width - s, :] * shifted[s - 1]
    return u


def _conv_transpose(du, w_ref):
    width = w_ref.shape[0]
    shifted = _shifts_up(du, width)
    dp = w_ref[width - 1:width, :] * du
    for s in range(1, width):
        dp = dp + w_ref[width - 1 - s:width - s, :] * shifted[s - 1]
    return dp


def _conv_wgrad(du, p, shifted, dw_ref, db_ref):
    width = dw_ref.shape[0]
    db_ref[...] = jnp.sum(du, axis=0, keepdims=True)
    dw_ref[width - 1:width, :] = jnp.sum(du * p, axis=0, keepdims=True)
    for s in range(1, width):
        dw_ref[width - 1 - s:width - s, :] = jnp.sum(du * shifted[s - 1], axis=0, keepdims=True)


def _ssd_conv_fwd(proj, w, b, name):
    t = proj.shape[0]
    cb = CONV_COLS
    off = SSD_D_INNER // cb

    def body(p_ref, w_ref, b_ref, o_ref):
        p = p_ref[...]
        u = _conv_pre(p, _shifts_down(p, SSD_CONV), w_ref, b_ref)
        o_ref[...] = u * _sigmoid(u)

    return pl.pallas_call(
        body, name=name, grid=(SSD_CONV_DIM // cb,),
        in_specs=[pl.BlockSpec((t, cb), lambda j: (0, j + off)), pl.BlockSpec((SSD_CONV, cb), lambda j: (0, j)),
                  pl.BlockSpec((1, cb), lambda j: (0, j))],
        out_specs=pl.BlockSpec((t, cb), lambda j: (0, j)),
        out_shape=jax.ShapeDtypeStruct((t, SSD_CONV_DIM), F32), compiler_params=_cp("parallel"),
    )(proj, w, b.reshape(1, -1))


def _ssd_conv_bwd(proj, w, b, dact, name):
    t = proj.shape[0]
    cb = CONV_COLS
    off = SSD_D_INNER // cb

    def body(p_ref, w_ref, b_ref, da_ref, dp_ref, dw_ref, db_ref):
        p = p_ref[...]
        shifted = _shifts_down(p, SSD_CONV)
        u = _conv_pre(p, shifted, w_ref, b_ref)
        sg = _sigmoid(u)
        du = da_ref[...] * (sg * (1.0 + u * (1.0 - sg)))
        dp_ref[...] = _conv_transpose(du, w_ref).astype(dp_ref.dtype)
        _conv_wgrad(du, p, shifted, dw_ref, db_ref)

    col = pl.BlockSpec((t, cb), lambda j: (0, j))
    wspec = pl.BlockSpec((SSD_CONV, cb), lambda j: (0, j))
    bspec = pl.BlockSpec((1, cb), lambda j: (0, j))
    return pl.pallas_call(
        body, name=name, grid=(SSD_CONV_DIM // cb,),
        in_specs=[pl.BlockSpec((t, cb), lambda j: (0, j + off)), wspec, bspec, col],
        out_specs=[col, wspec, bspec],
        out_shape=[jax.ShapeDtypeStruct((t, SSD_CONV_DIM), MXU_DTYPE), jax.ShapeDtypeStruct((SSD_CONV, SSD_CONV_DIM), F32),
                   jax.ShapeDtypeStruct((1, SSD_CONV_DIM), F32)],
        compiler_params=_cp("parallel"),
    )(proj, w, b.reshape(1, -1), dact)


def _ffn_conv_fwd(proj, w, b, name):
    t = proj.shape[0]
    cb = CONV_COLS
    nb = FFN_D_FF // cb

    def body(pg_ref, pu_ref, wg_ref, wu_ref, bg_ref, bu_ref, o_ref):
        pg, pu = pg_ref[...], pu_ref[...]
        ug = _conv_pre(pg, _shifts_down(pg, FFN_CONV), wg_ref, bg_ref)
        uu = _conv_pre(pu, _shifts_down(pu, FFN_CONV), wu_ref, bu_ref)
        o_ref[...] = (ug * _sigmoid(ug) * uu).astype(o_ref.dtype)

    gcol = pl.BlockSpec((t, cb), lambda j: (0, j))
    ucol = pl.BlockSpec((t, cb), lambda j: (0, j + nb))
    b2 = b.reshape(1, -1)
    return pl.pallas_call(
        body, name=name, grid=(nb,),
        in_specs=[gcol, ucol, pl.BlockSpec((FFN_CONV, cb), lambda j: (0, j)), pl.BlockSpec((FFN_CONV, cb), lambda j: (0, j + nb)),
                  pl.BlockSpec((1, cb), lambda j: (0, j)), pl.BlockSpec((1, cb), lambda j: (0, j + nb))],
        out_specs=gcol, out_shape=jax.ShapeDtypeStruct((t, FFN_D_FF), MXU_DTYPE), compiler_params=_cp("parallel"),
    )(proj, proj, w, w, b2, b2)


def _ffn_conv_bwd(proj, w, b, dact, name):
    t = proj.shape[0]
    cb = CONV_COLS
    nb = FFN_D_FF // cb

    def body(pg_ref, pu_ref, wg_ref, wu_ref, bg_ref, bu_ref, da_ref,
             dpg_ref, dpu_ref, dwg_ref, dwu_ref, dbg_ref, dbu_ref):
        pg, pu = pg_ref[...], pu_ref[...]
        pg_shifted, pu_shifted = _shifts_down(pg, FFN_CONV), _shifts_down(pu, FFN_CONV)
        ug = _conv_pre(pg, pg_shifted, wg_ref, bg_ref)
        uu = _conv_pre(pu, pu_shifted, wu_ref, bu_ref)
        sg = _sigmoid(ug)
        da = da_ref[...]
        dug = da * uu * (sg * (1.0 + ug * (1.0 - sg)))
        duu = da * (ug * sg)
        dpg_ref[...] = _conv_transpose(dug, wg_ref).astype(dpg_ref.dtype)
        dpu_ref[...] = _conv_transpose(duu, wu_ref).astype(dpu_ref.dtype)
        _conv_wgrad(dug, pg, pg_shifted, dwg_ref, dbg_ref)
        _conv_wgrad(duu, pu, pu_shifted, dwu_ref, dbu_ref)

    gcol = pl.BlockSpec((t, cb), lambda j: (0, j))
    ucol = pl.BlockSpec((t, cb), lambda j: (0, j + nb))
    wg = pl.BlockSpec((FFN_CONV, cb), lambda j: (0, j))
    wu = pl.BlockSpec((FFN_CONV, cb), lambda j: (0, j + nb))
    bg = pl.BlockSpec((1, cb), lambda j: (0, j))
    bu = pl.BlockSpec((1, cb), lambda j: (0, j + nb))
    b2 = b.reshape(1, -1)
    half = jax.ShapeDtypeStruct((t, FFN_D_FF), MXU_DTYPE)
    return pl.pallas_call(
        body, name=name, grid=(nb,),
        in_specs=[gcol, ucol, wg, wu, bg, bu, gcol],
        out_specs=[gcol, gcol, wg, wg, bg, bg],
        out_shape=[half, half, jax.ShapeDtypeStruct((FFN_CONV, FFN_D_FF), F32), jax.ShapeDtypeStruct((FFN_CONV, FFN_D_FF), F32),
                   jax.ShapeDtypeStruct((1, FFN_D_FF), F32), jax.ShapeDtypeStruct((1, FFN_D_FF), F32)],
        compiler_params=_cp("parallel"),
    )(proj, proj, w, w, b2, b2, dact)


SSD_ROWS = 128
DT_COL = (SSD_D_INNER + SSD_CONV_DIM) // LANES


def _head_expand():
    h = np.arange(LANES)[:, None]
    col = np.arange(SSD_D_INNER)[None, :]
    return jnp.asarray((col // SSD_HEAD_DIM == h), MXU_DTYPE)


def _chunk_tri(n, lower):
    t = _iota((n, n), 0)
    s = _iota((n, n), 1)
    shift = SSD_CHUNK.bit_length() - 1
    same = jnp.right_shift(t, shift) == jnp.right_shift(s, shift)
    tri = (s <= t) if lower else (s >= t)
    return jnp.where(same & tri, 1.0, 0.0).astype(MXU_DTYPE)


def _softplus(x):
    return jnp.maximum(x, 0.0) + jnp.log(1.0 + jnp.exp(-jnp.abs(x)))


def _ssd_dt_fwd(proj, act, dt_bias, a_neg, expand, name):
    t = proj.shape[0]
    tm = min(SSD_ROWS, t)

    def body(raw_ref, xs_ref, bias_ref, a_ref, e_ref, xdt_ref, dt_ref, acum_ref):
        lane = _iota((tm, LANES), 1)
        dt = jnp.where(lane < SSD_HEADS, _softplus(raw_ref[...] + bias_ref[...]), 0.0)
        dt_ref[...] = dt
        xdt_ref[...] = xs_ref[...] * _mm_exact_rhs(dt, e_ref[...])
        acum_ref[...] = _mm_exact_lhs(_chunk_tri(tm, True), a_ref[...] * dt)

    vec = pl.BlockSpec((1, LANES), lambda i: (0, 0))
    return pl.pallas_call(
        body, name=name, grid=(t // tm,),
        in_specs=[pl.BlockSpec((tm, LANES), lambda i: (i, DT_COL)), pl.BlockSpec((tm, SSD_D_INNER), lambda i: (i, 0)),
                  vec, vec, pl.BlockSpec((LANES, SSD_D_INNER), lambda i: (0, 0))],
        out_specs=[pl.BlockSpec((tm, SSD_D_INNER), lambda i: (i, 0)), pl.BlockSpec((tm, LANES), lambda i: (i, 0)),
                   pl.BlockSpec((tm, LANES), lambda i: (i, 0))],
        out_shape=[jax.ShapeDtypeStruct((t, SSD_D_INNER), F32), jax.ShapeDtypeStruct((t, LANES), F32),
                   jax.ShapeDtypeStruct((t, LANES), F32)],
        compiler_params=_cp("parallel"),
    )(proj, act, dt_bias, a_neg, expand)


def _ssd_dt_bwd(proj, act, dt, dxdt, dyy, dacum, dt_bias, a_neg, d_exp, expand, expand_t, name):
    t = proj.shape[0]
    tm = min(SSD_ROWS, t)

    def body(raw_ref, xs_ref, dt_ref, dxdt_ref, dyy_ref, dac_ref, bias_ref, a_ref, dsk_ref, e_ref, et_ref,
             dxs_ref, draw_ref, da_ref, dbias_ref, dd_ref):
        @pl.when(pl.program_id(0) == 0)
        def _():
            da_ref[...] = jnp.zeros_like(da_ref)
            dbias_ref[...] = jnp.zeros_like(dbias_ref)
            dd_ref[...] = jnp.zeros_like(dd_ref)

        lane = _iota((tm, LANES), 1)
        xs, dt, dxdt, dyy = xs_ref[...], dt_ref[...], dxdt_ref[...], dyy_ref[...]
        dxs_ref[...] = dxdt * _mm_exact_rhs(dt, e_ref[...]) + dsk_ref[...] * dyy
        dd_ref[...] += jnp.sum(dyy * xs, axis=0, keepdims=True)
        ddt = _mm_exact_rhs(dxdt * xs, et_ref[...])
        da = _mm_exact_lhs(_chunk_tri(tm, False), dac_ref[...])
        ddt = ddt + da * a_ref[...]
        da_ref[...] += jnp.sum(da * dt, axis=0, keepdims=True)
        draw = jnp.where(lane < SSD_HEADS, ddt * _sigmoid(raw_ref[...] + bias_ref[...]), 0.0)
        dbias_ref[...] += jnp.sum(draw, axis=0, keepdims=True)
        draw_ref[...] = draw.astype(draw_ref.dtype)

    wide = pl.BlockSpec((tm, SSD_D_INNER), lambda i: (i, 0))
    thin = pl.BlockSpec((tm, LANES), lambda i: (i, 0))
    vec = pl.BlockSpec((1, LANES), lambda i: (0, 0))
    wvec = pl.BlockSpec((1, SSD_D_INNER), lambda i: (0, 0))
    return pl.pallas_call(
        body, name=name, grid=(t // tm,),
        in_specs=[pl.BlockSpec((tm, LANES), lambda i: (i, DT_COL)), wide, thin, wide, wide, thin, vec, vec, wvec,
                  pl.BlockSpec((LANES, SSD_D_INNER), lambda i: (0, 0)), pl.BlockSpec((SSD_D_INNER, LANES), lambda i: (0, 0))],
        out_specs=[wide, thin, vec, vec, wvec],
        out_shape=[jax.ShapeDtypeStruct((t, SSD_D_INNER), F32), jax.ShapeDtypeStruct((t, LANES), MXU_DTYPE),
                   jax.ShapeDtypeStruct((1, LANES), F32), jax.ShapeDtypeStruct((1, LANES), F32),
                   jax.ShapeDtypeStruct((1, SSD_D_INNER), F32)],
        compiler_params=_cp("arbitrary"),
    )(proj, act, dt, dxdt, dyy, dacum, dt_bias, a_neg, d_exp, expand, expand_t)


SSD_PAIR = 2 * SSD_HEAD_DIM
HEADS_PER_GROUP = SSD_HEADS // SSD_GROUPS
GROUP_COLS = HEADS_PER_GROUP * SSD_HEAD_DIM
B_COL0 = SSD_D_INNER // SSD_STATE
C_COL0 = (SSD_D_INNER + SSD_GROUPS * SSD_STATE) // SSD_STATE


SCAN_GROUPS = 4
SCAN_STEPS = SSD_GROUPS // SCAN_GROUPS


def _scan_step_is(g, c):
    return (pl.program_id(0) == g) & (pl.program_id(1) == c)


def _ssd_scan_fwd(xdt, act, acum_g, acum_gt, name, carried=None):
    t = xdt.shape[0]
    nc = t // SSD_CHUNK
    ln = SSD_CHUNK
    c_in_specs, c_in, c_out_specs, c_out = _carried_specs(carried)

    def body(*refs):
        if carried is None:
            x_ref, b_ref, c_ref, ac_ref, act_ref, y_ref, sst_ref, state = refs
            comm_refs = None
        else:
            x_ref, b_ref, c_ref, ac_ref, act_ref, src_ref, y_ref, sst_ref, land_ref, state, send_sems, recv_sems, local_sem = refs
            comm_refs = (src_ref, land_ref, send_sems, recv_sems, local_sem)
        finish = _carried_hooks(carried, comm_refs, _scan_step_is(0, 0), _scan_step_is(SCAN_STEPS - 1, 0),
                                _scan_step_is(SCAN_STEPS - 1, nc - 1))

        @pl.when(pl.program_id(1) == 0)
        def _():
            state[...] = jnp.zeros_like(state)

        causal = _iota((ln, ln), 1) <= _iota((ln, ln), 0)
        lo_mask = _iota((ln, SSD_PAIR), 1) < SSD_HEAD_DIM
        lo_rows = _iota((SSD_PAIR, SSD_STATE), 0) < SSD_HEAD_DIM
        for gg in range(SCAN_GROUPS):
            sst_ref[0, gg] = state[gg * GROUP_COLS:(gg + 1) * GROUP_COLS, :]
            bm = b_ref[:, gg * SSD_STATE:(gg + 1) * SSD_STATE].astype(MXU_DTYPE)
            cm = c_ref[:, gg * SSD_STATE:(gg + 1) * SSD_STATE].astype(MXU_DTYPE)
            cb = _mm_nt(cm, bm)
            ac, act_ = ac_ref[gg], act_ref[gg]
            e_last = jnp.exp(ac[ln - 1:ln, :])
            ac_rows = [jnp.broadcast_to(ac[:, h:h + 1], (ln, SSD_PAIR)) for h in range(HEADS_PER_GROUP)]
            for pr in range(2):
                first = gg * GROUP_COLS + pr * SSD_PAIR
                cols = slice(first, first + SSD_PAIR)
                xp = x_ref[:, cols]
                sp = state[cols, :]
                pair_ac = jnp.where(lo_mask, ac_rows[2 * pr], ac_rows[2 * pr + 1])
                ydiag = jnp.zeros((ln, SSD_PAIR), F32)
                for hh in range(2):
                    h = 2 * pr + hh
                    seg = ac_rows[h] - act_[h:h + 1, :]
                    dec = jnp.exp(jnp.where(causal, seg, -1e30))
                    mask = lo_mask if hh == 0 else jnp.logical_not(lo_mask)
                    ydiag = ydiag + _mm((cb * dec).astype(MXU_DTYPE), jnp.where(mask, xp, 0.0).astype(MXU_DTYPE))
                yoff = _mm_nt(cm, sp.astype(MXU_DTYPE)) * jnp.exp(pair_ac)
                y_ref[:, cols] = ydiag + yoff
                xw = (xp * jnp.exp(pair_ac[ln - 1:ln, :] - pair_ac)).astype(MXU_DTYPE)
                el = jnp.where(lo_rows, e_last[:, 2 * pr:2 * pr + 1], e_last[:, 2 * pr + 1:2 * pr + 2])
                state[cols, :] = sp * el + _mm_tn(xw, bm)
        finish()

    sg = SCAN_GROUPS
    return pl.pallas_call(
        body, name=name, grid=(SCAN_STEPS, nc),
        in_specs=[pl.BlockSpec((ln, sg * GROUP_COLS), lambda g, c: (c, g)),
                  pl.BlockSpec((ln, sg * SSD_STATE), lambda g, c: (c, B_COL0 // sg + g)),
                  pl.BlockSpec((ln, sg * SSD_STATE), lambda g, c: (c, C_COL0 // sg + g)),
                  pl.BlockSpec((sg, ln, HEADS_PER_GROUP), lambda g, c: (g, c, 0)),
                  pl.BlockSpec((sg, HEADS_PER_GROUP, ln), lambda g, c: (g, 0, c))] + c_in_specs,
        out_specs=[pl.BlockSpec((ln, sg * GROUP_COLS), lambda g, c: (c, g)),
                   pl.BlockSpec((1, sg, GROUP_COLS, SSD_STATE), lambda g, c: (c, g, 0, 0))] + c_out_specs,
        out_shape=[jax.ShapeDtypeStruct((t, SSD_D_INNER), F32),
                   jax.ShapeDtypeStruct((nc, SSD_GROUPS, GROUP_COLS, SSD_STATE), F32)] + c_out,
        scratch_shapes=[pltpu.VMEM((sg * GROUP_COLS, SSD_STATE), F32)] + (EXCHANGE_SCRATCH if carried else []),
        compiler_params=_cp("arbitrary", "arbitrary"),
    )(xdt, act, act, acum_g, acum_gt, *c_in)


def _ssd_scan_bwd(xdt, act, acum_g, acum_gt, states, dy, name, carried=None):
    t = xdt.shape[0]
    nc = t // SSD_CHUNK
    ln = SSD_CHUNK
    c_in_specs, c_in, c_out_specs, c_out = _carried_specs(carried)

    def body(*refs):
        if carried is None:
            x_ref, b_ref, c_ref, ac_ref, act_ref, sst_ref, dy_ref, dx_ref, db_ref, dc_ref, dacol_ref, darow_ref, dstate = refs
            comm_refs = None
        else:
            (x_ref, b_ref, c_ref, ac_ref, act_ref, sst_ref, dy_ref, src_ref, dx_ref, db_ref, dc_ref, dacol_ref, darow_ref,
             land_ref, dstate, send_sems, recv_sems, local_sem) = refs
            comm_refs = (src_ref, land_ref, send_sems, recv_sems, local_sem)
        finish = _carried_hooks(carried, comm_refs, _scan_step_is(0, 0), _scan_step_is(SCAN_STEPS - 1, 0),
                                _scan_step_is(SCAN_STEPS - 1, nc - 1))

        @pl.when(pl.program_id(1) == 0)
        def _():
            dstate[...] = jnp.zeros_like(dstate)

        for gg in range(SCAN_GROUPS):
            group_bwd(gg, x_ref, b_ref, c_ref, ac_ref, act_ref, sst_ref, dy_ref, dx_ref, db_ref, dc_ref, dacol_ref, darow_ref, dstate)
        finish()

    def group_bwd(gg, x_ref, b_ref, c_ref, ac_ref, act_ref, sst_ref, dy_ref, dx_ref, db_ref, dc_ref, dacol_ref, darow_ref, dstate):
        bc_cols = slice(gg * SSD_STATE, (gg + 1) * SSD_STATE)
        bm = b_ref[:, bc_cols].astype(MXU_DTYPE)
        cm = c_ref[:, bc_cols].astype(MXU_DTYPE)
        cb = _mm_nt(cm, bm)
        ac, act_ = ac_ref[gg], act_ref[gg]
        causal = _iota((ln, ln), 1) <= _iota((ln, ln), 0)
        lo_mask = _iota((ln, SSD_PAIR), 1) < SSD_HEAD_DIM
        lo_rows = _iota((SSD_PAIR, SSD_STATE), 0) < SSD_HEAD_DIM
        lane4 = _iota((ln, HEADS_PER_GROUP), 1)
        sub4 = _iota((HEADS_PER_GROUP, ln), 0)
        is_last = _iota((ln, 1), 0) == ln - 1
        e_last = jnp.exp(ac[ln - 1:ln, :])
        ac_rows = [jnp.broadcast_to(ac[:, h:h + 1], (ln, SSD_PAIR)) for h in range(HEADS_PER_GROUP)]
        dcb = jnp.zeros((ln, ln), F32)
        dc_acc = jnp.zeros((ln, SSD_STATE), F32)
        db_acc = jnp.zeros((ln, SSD_STATE), F32)
        dacol = jnp.zeros((ln, HEADS_PER_GROUP), F32)
        darow = jnp.zeros((HEADS_PER_GROUP, ln), F32)
        for pr in range(2):
            in_group = slice(pr * SSD_PAIR, (pr + 1) * SSD_PAIR)
            cols = slice(gg * GROUP_COLS + pr * SSD_PAIR, gg * GROUP_COLS + (pr + 1) * SSD_PAIR)
            xp = x_ref[:, cols]
            dyp = dy_ref[:, cols]
            sp = sst_ref[0, gg, in_group, :]
            dsp = dstate[cols, :]
            pair_ac = jnp.where(lo_mask, ac_rows[2 * pr], ac_rows[2 * pr + 1])
            ea = jnp.exp(pair_ac)
            w = jnp.exp(pair_ac[ln - 1:ln, :] - pair_ac)
            dye = (dyp * ea).astype(MXU_DTYPE)
            dx_state = w * _mm_nt(bm, dsp.astype(MXU_DTYPE))
            yoff = _mm_nt(cm, sp.astype(MXU_DTYPE)) * ea
            dxp = dx_state
            for hh in range(2):
                h = 2 * pr + hh
                mask = lo_mask if hh == 0 else jnp.logical_not(lo_mask)
                rmask = lo_rows if hh == 0 else jnp.logical_not(lo_rows)
                seg = ac_rows[h] - act_[h:h + 1, :]
                dec = jnp.exp(jnp.where(causal, seg, -1e30))
                m = cb * dec
                dym = jnp.where(mask, dyp, 0.0).astype(MXU_DTYPE)
                xm = jnp.where(mask, xp, 0.0).astype(MXU_DTYPE)
                g = _mm_nt(dym, xm)
                dxp = dxp + _mm_tn(m.astype(MXU_DTYPE), dym)
                dcb = dcb + dec * g
                mg = m * g
                rs = jnp.sum(mg, axis=1, keepdims=True)
                cs = jnp.sum(mg, axis=0, keepdims=True)
                t_off = jnp.sum(jnp.where(mask, dyp * yoff, 0.0), axis=1, keepdims=True)
                q = jnp.sum(jnp.where(mask, xp * dx_state, 0.0), axis=1, keepdims=True)
                qsum = jnp.sum(q, axis=0, keepdims=True)
                ds_s = jnp.sum(jnp.sum(jnp.where(rmask, dsp * sp, 0.0), axis=1, keepdims=True), axis=0, keepdims=True)
                extra = qsum + e_last[:, h:h + 1] * ds_s
                col = rs + t_off - q + jnp.where(is_last, extra, 0.0)
                dacol = jnp.where(lane4 == h, col, dacol)
                darow = jnp.where(sub4 == h, -cs, darow)
            dx_ref[:, cols] = dxp
            dc_acc = dc_acc + _mm(dye, sp.astype(MXU_DTYPE))
            db_acc = db_acc + _mm((xp * w).astype(MXU_DTYPE), dsp.astype(MXU_DTYPE))
            el = jnp.where(lo_rows, e_last[:, 2 * pr:2 * pr + 1], e_last[:, 2 * pr + 1:2 * pr + 2])
            dstate[cols, :] = dsp * el + _mm_tn(dye, cm)
        dcbm = dcb.astype(MXU_DTYPE)
        dc_ref[:, bc_cols] = _mm(dcbm, bm) + dc_acc
        db_ref[:, bc_cols] = _mm_tn(dcbm, cm) + db_acc
        dacol_ref[gg] = dacol
        darow_ref[gg] = darow

    def rev(c):
        return nc - 1 - c

    sg = SCAN_GROUPS
    grp = pl.BlockSpec((ln, sg * GROUP_COLS), lambda g, c: (rev(c), g))
    return pl.pallas_call(
        body, name=name, grid=(SCAN_STEPS, nc),
        in_specs=[grp,
                  pl.BlockSpec((ln, sg * SSD_STATE), lambda g, c: (rev(c), B_COL0 // sg + g)),
                  pl.BlockSpec((ln, sg * SSD_STATE), lambda g, c: (rev(c), C_COL0 // sg + g)),
                  pl.BlockSpec((sg, ln, HEADS_PER_GROUP), lambda g, c: (g, rev(c), 0)),
                  pl.BlockSpec((sg, HEADS_PER_GROUP, ln), lambda g, c: (g, 0, rev(c))),
                  pl.BlockSpec((1, sg, GROUP_COLS, SSD_STATE), lambda g, c: (rev(c), g, 0, 0)),
                  grp] + c_in_specs,
        out_specs=[grp,
                   pl.BlockSpec((ln, sg * SSD_STATE), lambda g, c: (rev(c), g)),
                   pl.BlockSpec((ln, sg * SSD_STATE), lambda g, c: (rev(c), g)),
                   pl.BlockSpec((sg, ln, HEADS_PER_GROUP), lambda g, c: (g, rev(c), 0)),
                   pl.BlockSpec((sg, HEADS_PER_GROUP, ln), lambda g, c: (g, 0, rev(c)))] + c_out_specs,
        out_shape=[jax.ShapeDtypeStruct((t, SSD_D_INNER), F32),
                   jax.ShapeDtypeStruct((t, SSD_GROUPS * SSD_STATE), F32),
                   jax.ShapeDtypeStruct((t, SSD_GROUPS * SSD_STATE), F32),
                   jax.ShapeDtypeStruct((SSD_GROUPS, t, HEADS_PER_GROUP), F32),
                   jax.ShapeDtypeStruct((SSD_GROUPS, HEADS_PER_GROUP, t), F32)] + c_out,
        scratch_shapes=[pltpu.VMEM((sg * GROUP_COLS, SSD_STATE), F32)] + (EXCHANGE_SCRATCH if carried else []),
        compiler_params=_cp("arbitrary", "arbitrary"),
    )(xdt, act, act, acum_g, acum_gt, states, dy, *c_in)


GN_ROWS = 128


def _gated_norm_parts(y_ref, xs_ref, z_ref, dsk_ref):
    yy = y_ref[...] + dsk_ref[...] * xs_ref[...]
    z = z_ref[...]
    sz = _sigmoid(z)
    silu = z * sz
    u = yy * silu
    r = lax.rsqrt(jnp.mean(u * u, axis=-1, keepdims=True) + NORM_EPS)
    return yy, z, sz, silu, u, r


def _gated_norm_fwd(y, act, proj, d_exp, g, name):
    t = y.shape[0]
    tm = min(GN_ROWS, t)

    def body(y_ref, xs_ref, z_ref, dsk_ref, g_ref, o_ref):
        _, _, _, _, u, r = _gated_norm_parts(y_ref, xs_ref, z_ref, dsk_ref)
        o_ref[...] = (u * r * g_ref[...]).astype(o_ref.dtype)

    wide = pl.BlockSpec((tm, SSD_D_INNER), lambda i: (i, 0))
    wvec = pl.BlockSpec((1, SSD_D_INNER), lambda i: (0, 0))
    return pl.pallas_call(
        body, name=name, grid=(t // tm,), in_specs=[wide, wide, wide, wvec, wvec], out_specs=wide,
        out_shape=jax.ShapeDtypeStruct((t, SSD_D_INNER), MXU_DTYPE), compiler_params=_cp("parallel"),
    )(y, act, proj, d_exp, g.reshape(1, -1))


def _gated_norm_bwd(y, act, proj, d_exp, g, dn, name):
    t = y.shape[0]
    tm = min(GN_ROWS, t)

    def body(y_ref, xs_ref, z_ref, dsk_ref, g_ref, dn_ref, dyy_ref, dz_ref, dg_ref):
        @pl.when(pl.program_id(0) == 0)
        def _():
            dg_ref[...] = jnp.zeros_like(dg_ref)

        yy, z, sz, silu, u, r = _gated_norm_parts(y_ref, xs_ref, z_ref, dsk_ref)
        un = u * r
        dn = dn_ref[...]
        v = dn * g_ref[...]
        du = r * (v - un * jnp.mean(v * un, axis=-1, keepdims=True))
        dg_ref[...] += jnp.sum(dn * un, axis=0, keepdims=True)
        dyy_ref[...] = du * silu
        dz_ref[...] = (du * yy * (sz * (1.0 + z * (1.0 - sz)))).astype(dz_ref.dtype)

    wide = pl.BlockSpec((tm, SSD_D_INNER), lambda i: (i, 0))
    wvec = pl.BlockSpec((1, SSD_D_INNER), lambda i: (0, 0))
    return pl.pallas_call(
        body, name=name, grid=(t // tm,), in_specs=[wide, wide, wide, wvec, wvec, wide], out_specs=[wide, wide, wvec],
        out_shape=[jax.ShapeDtypeStruct((t, SSD_D_INNER), F32), jax.ShapeDtypeStruct((t, SSD_D_INNER), MXU_DTYPE),
                   jax.ShapeDtypeStruct((1, SSD_D_INNER), F32)],
        compiler_params=_cp("arbitrary"),
    )(y, act, proj, d_exp, g.reshape(1, -1), dn)


SB_PAIRS = SB_HEADS // 2


def _kv_rows(j, bt, nt=1):
    return pl.ds(pl.multiple_of(j * bt, bt), nt * bt)


def _sb_tile_masks(bt):
    lane = _iota((bt, bt), 1)
    rowi = _iota((bt, bt), 0)
    return lane < rowi, (rowi >= lane).astype(MXU_DTYPE), (rowi <= lane).astype(MXU_DTYPE)


def _sb_scaled_heads(pair, scale):
    lane = _iota(pair.shape, 1)
    val = pair.astype(F32) * scale
    return [jnp.where(lane < SB_HEAD_DIM, val, 0.0).astype(pair.dtype), jnp.where(lane >= SB_HEAD_DIM, val, 0.0).astype(pair.dtype)]


def _sb_logits(qs, kb, bt, strict):
    nt = kb.shape[0] // bt
    full = [_mm_nt(q_head, kb) for q_head in qs]
    xs, nlfs = [], []
    for x in full:
        nlf = jnp.maximum(x, 0.0) + jnp.log(1.0 + jnp.exp(-jnp.abs(x)))
        xs.append([x[:, tt * bt:(tt + 1) * bt] for tt in range(nt)])
        tiles = [nlf[:, tt * bt:(tt + 1) * bt] for tt in range(nt)]
        if strict is not None:
            tiles[-1] = jnp.where(strict, tiles[-1], 0.0)
        nlfs.append(tiles)
    return xs, nlfs


def _sb_tails(nlf_tiles, from_j):
    tails, run = [None] * len(nlf_tiles), None
    for tt in reversed(range(len(nlf_tiles))):
        tail = _mm_exact_rhs(nlf_tiles[tt], from_j)
        tails[tt] = tail if run is None else tail + run
        run = tails[tt][:, 0:1]
    return tails


def _sb_heads(e_tiles, upto_j, pre):
    sums, run = [], pre
    for e in e_tiles:
        sums.append(_mm_exact_rhs(e, upto_j) + run)
        run = sums[-1][:, e.shape[1] - 1:e.shape[1]]
    return sums


def _carried_specs(carried):
    if carried is None:
        return [], [], [], []
    src, per_peer = carried
    rows = src.shape[1:] if per_peer else src.shape
    anywhere = pl.BlockSpec(memory_space=pl.ANY)
    return [anywhere], [src], [anywhere], [jax.ShapeDtypeStruct((N_DEV, *rows), src.dtype)]


def _carried_hooks(carried, comm_refs, first, pass_on, last):
    if carried is None:
        return lambda: None
    per_peer = carried[1]

    @pl.when(first)
    def _():
        if per_peer:
            _exchange_start(*comm_refs, per_peer=True)
        else:
            _gather_phase("send", *comm_refs)

    if not per_peer:
        @pl.when(pass_on)
        def _():
            _gather_phase("pass_on", *comm_refs)

    def finish():
        @pl.when(last)
        def _():
            if per_peer:
                _exchange_finish(*comm_refs, per_peer=True)
            else:
                _gather_phase("finish", *comm_refs)

    return finish


def _sb_attention_fwd(qkv, name, carried=None):
    t = qkv.shape[0]
    bt = min(SB_TILE, t)
    nq = t // bt
    c_in_specs, c_in, c_out_specs, c_out = _carried_specs(carried)

    def body(*refs):
        if carried is None:
            q_ref, k_ref, v_ref, o_ref, acc_ref = refs
            comm_refs = None
        else:
            q_ref, k_ref, v_ref, src_ref, o_ref, land_ref, acc_ref, send_sems, recv_sems, local_sem = refs
            comm_refs = (src_ref, land_ref, send_sems, recv_sems, local_sem)
        i = pl.program_id(1)
        last_pair = pl.program_id(0) == SB_PAIRS - 1
        finish = _carried_hooks(carried, comm_refs, (pl.program_id(0) == 0) & (i == 0), last_pair & (i == 0),
                                last_pair & (i == nq - 1))
        strict, from_j, _ = _sb_tile_masks(bt)
        qs = _sb_scaled_heads(q_ref[...], SB_SCALE)
        acc_ref[...] = jnp.zeros_like(acc_ref)

        def block(j, nt, carries, diag):
            rows = _kv_rows(j, bt, nt)
            kb, vb = k_ref[rows, :], v_ref[rows, :]
            xs, nlfs = _sb_logits(qs, kb, bt, strict if diag else None)
            tails = [_sb_tails(nlfs[hh], from_j) for hh in range(2)]
            for hh in range(2):
                ws = [jnp.exp(xs[hh][tt] - tails[hh][tt] - carries[hh]) for tt in range(nt)]
                if diag:
                    ws[-1] = jnp.where(strict, ws[-1], 0.0)
                acc_ref[hh] += _mm(jnp.concatenate([w.astype(MXU_DTYPE) for w in ws], axis=1), vb)
            return tuple(carries[hh] + tails[hh][0][:, 0:1] for hh in range(2))

        zero = jnp.zeros((bt, 1), F32)
        carries = block(i, 1, (zero, zero), True)
        carries = lax.fori_loop(0, i // 2, lambda it, cr: block(i - 2 - 2 * it, 2, cr, False), carries)

        @pl.when(i % 2 == 1)
        def _():
            block(0, 1, carries, False)

        low = _iota((bt, 2 * SB_HEAD_DIM), 1) < SB_HEAD_DIM
        o_ref[...] = jnp.where(low, acc_ref[0], acc_ref[1]).astype(o_ref.dtype)
        finish()

    lanes = 2 * SB_HEAD_DIM
    res = pl.pallas_call(
        body, name=name, grid=(SB_PAIRS, nq),
        in_specs=[pl.BlockSpec((bt, lanes), lambda p, i: (i, p)),
                  pl.BlockSpec((t, lanes), lambda p, i: (0, SB_PAIRS + p)),
                  pl.BlockSpec((t, lanes), lambda p, i: (0, 2 * SB_PAIRS + p))] + c_in_specs,
        out_specs=[pl.BlockSpec((bt, lanes), lambda p, i: (i, p))] + c_out_specs,
        out_shape=[jax.ShapeDtypeStruct((t, D_MODEL), MXU_DTYPE)] + c_out,
        scratch_shapes=[pltpu.VMEM((2, bt, lanes), F32)] + (EXCHANGE_SCRATCH if carried else []),
        compiler_params=_cp("arbitrary", "arbitrary"),
    )(qkv, qkv, qkv, *c_in)
    return res[0] if carried is None else res


def _sb_attention_bwd(qkv, do, name, carried=None):
    t = qkv.shape[0]
    bt = min(SB_TILE, t)
    nq = t // bt
    lanes = 2 * SB_HEAD_DIM
    c_in_specs, c_in, c_out_specs, c_out = _carried_specs(carried)

    def body(*refs):
        if carried is None:
            q_ref, k_ref, v_ref, do_ref, dq_ref, dk_ref, dv_ref, sbuf, ebuf, dq_acc, dk_acc, dv_acc = refs
            comm_refs = None
        else:
            (q_ref, k_ref, v_ref, do_ref, src_ref, dq_ref, dk_ref, dv_ref, land_ref,
             sbuf, ebuf, dq_acc, dk_acc, dv_acc, send_sems, recv_sems, local_sem) = refs
            comm_refs = (src_ref, land_ref, send_sems, recv_sems, local_sem)
        i = pl.program_id(1)
        last_pair = pl.program_id(0) == SB_PAIRS - 1
        finish = _carried_hooks(carried, comm_refs, (pl.program_id(0) == 0) & (i == 0), last_pair & (i == 0),
                                last_pair & (i == nq - 1))

        @pl.when(i == 0)
        def _():
            dk_acc[...] = jnp.zeros_like(dk_acc)
            dv_acc[...] = jnp.zeros_like(dv_acc)

        strict, from_j, upto_j = _sb_tile_masks(bt)
        qs = _sb_scaled_heads(q_ref[...], SB_SCALE)
        dos = _sb_scaled_heads(do_ref[...], 1.0)
        q_both = jnp.concatenate(qs, axis=0)
        do_both = jnp.concatenate(dos, axis=0)
        dq_acc[...] = jnp.zeros_like(dq_acc)

        def pass1(j, nt, carries, diag):
            rows = _kv_rows(j, bt, nt)
            kb, vb = k_ref[rows, :], v_ref[rows, :]
            xs, nlfs = _sb_logits(qs, kb, bt, strict if diag else None)
            dws = [_mm_nt(dos[hh], vb) for hh in range(2)]
            tails = [_sb_tails(nlfs[hh], from_j) for hh in range(2)]
            wcat = []
            for hh in range(2):
                ws = [jnp.exp(xs[hh][tt] - tails[hh][tt] - carries[hh]) for tt in range(nt)]
                if diag:
                    ws[-1] = jnp.where(strict, ws[-1], 0.0)
                w_all = jnp.concatenate(ws, axis=1)
                sbuf[hh, :, rows] = jnp.exp(jnp.concatenate([xs[hh][tt] - nlfs[hh][tt] for tt in range(nt)], axis=1))
                ebuf[hh, :, rows] = w_all * dws[hh]
                wcat.append(w_all.astype(MXU_DTYPE))
            dv_acc[rows, :] += _mm_tn(jnp.concatenate(wcat, axis=0), do_both)
            return tuple(carries[hh] + tails[hh][0][:, 0:1] for hh in range(2))

        zero = jnp.zeros((bt, 1), F32)
        carries = pass1(i, 1, (zero, zero), True)
        carries = lax.fori_loop(0, i // 2, lambda it, cr: pass1(i - 2 - 2 * it, 2, cr, False), carries)

        @pl.when(i % 2 == 1)
        def _():
            pass1(0, 1, carries, False)

        def pass2(j, nt, pres, diag):
            rows = _kv_rows(j, bt, nt)
            kb = k_ref[rows, :]
            sums = [_sb_heads([ebuf[hh, :, _kv_rows(j + tt, bt)] for tt in range(nt)], upto_j, pres[hh]) for hh in range(2)]
            dxm = []
            for hh in range(2):
                dxs = [ebuf[hh, :, _kv_rows(j + tt, bt)] - sbuf[hh, :, _kv_rows(j + tt, bt)] * sums[hh][tt] for tt in range(nt)]
                if diag:
                    dxs[-1] = jnp.where(strict, dxs[-1], 0.0)
                dxm.append(jnp.concatenate(dxs, axis=1).astype(MXU_DTYPE))
                dq_acc[hh] += _mm(dxm[hh], kb)
            dk_acc[rows, :] += _mm_tn(jnp.concatenate(dxm, axis=0), q_both)
            return tuple(sums[hh][-1][:, bt - 1:bt] for hh in range(2))

        pres = lax.fori_loop(0, i // 2, lambda it, pr: pass2(2 * it, 2, pr, False), (zero, zero))

        @pl.when(i % 2 == 0)
        def _():
            pass2(i, 1, pres, True)

        @pl.when(i % 2 == 1)
        def _():
            pass2(i - 1, 2, pres, True)

        low = _iota((bt, lanes), 1) < SB_HEAD_DIM
        dq_ref[...] = (jnp.where(low, dq_acc[0], dq_acc[1]) * SB_SCALE).astype(dq_ref.dtype)

        @pl.when(i == nq - 1)
        def _():
            dk_ref[...] = dk_acc[...].astype(dk_ref.dtype)
            dv_ref[...] = dv_acc[...].astype(dv_ref.dtype)

        finish()

    blk = pl.BlockSpec((bt, lanes), lambda p, i: (i, p))
    whole = pl.BlockSpec((t, lanes), lambda p, i: (0, p))
    out = jax.ShapeDtypeStruct((t, D_MODEL), MXU_DTYPE)
    return pl.pallas_call(
        body, name=name, grid=(SB_PAIRS, nq),
        in_specs=[blk, pl.BlockSpec((t, lanes), lambda p, i: (0, SB_PAIRS + p)),
                  pl.BlockSpec((t, lanes), lambda p, i: (0, 2 * SB_PAIRS + p)), blk] + c_in_specs,
        out_specs=[blk, whole, whole] + c_out_specs, out_shape=[out, out, out] + c_out,
        scratch_shapes=[pltpu.VMEM((2, bt, t), F32), pltpu.VMEM((2, bt, t), F32), pltpu.VMEM((2, bt, lanes), F32),
                        pltpu.VMEM((t, lanes), F32), pltpu.VMEM((t, lanes), F32)] + (EXCHANGE_SCRATCH if carried else []),
        compiler_params=_cp("arbitrary", "arbitrary"),
    )(qkv, qkv, qkv, do, *c_in)


def _add_pair(a, b, name):
    s, r, c = a.shape
    tm = _row_tile(r)

    def body(a_ref, b_ref, o_ref):
        o_ref[...] = (a_ref[...].astype(F32) + b_ref[...].astype(F32)).astype(o_ref.dtype)

    blk = pl.BlockSpec((1, tm, c), lambda q, i: (q, i, 0))
    return pl.pallas_call(body, name=name, grid=(s, r // tm), in_specs=[blk, blk], out_specs=blk,
                          out_shape=jax.ShapeDtypeStruct(a.shape, a.dtype), compiler_params=_cp("parallel", "parallel"))(a, b)


def _sum_slots(gslots, name):
    s, r, c = gslots.shape

    def body(g_ref, o_ref):
        g = g_ref[0].astype(F32)
        for q in range(1, s):
            g = g + g_ref[q].astype(F32)
        o_ref[...] = g

    return pl.pallas_call(
        body, name=name, grid=(c // LANES,),
        in_specs=[pl.BlockSpec((s, r, LANES), lambda j: (0, 0, j))], out_specs=pl.BlockSpec((r, LANES), lambda j: (0, j)),
        out_shape=jax.ShapeDtypeStruct((r, c), F32), compiler_params=_cp("parallel"),
    )(gslots)


def _adamw(gslots, w, m, v, name):
    s, r, c = gslots.shape
    tm = _row_tile(r)
    assert w.shape == (r, c), (w.shape, gslots.shape)
    c1 = 1.0 - ADAM_B1 ** ADAM_STEP
    c2 = 1.0 - ADAM_B2 ** ADAM_STEP

    def body(g_ref, w_ref, m_ref, v_ref, go_ref, d_ref, mo_ref, vo_ref):
        g = g_ref[0].astype(F32)
        for q in range(1, s):
            g = g + g_ref[q].astype(F32)
        mn = ADAM_B1 * m_ref[...] + (1.0 - ADAM_B1) * g
        vn = ADAM_B2 * v_ref[...] + (1.0 - ADAM_B2) * (g * g)
        go_ref[...] = g
        mo_ref[...] = mn
        vo_ref[...] = vn
        d_ref[...] = -ADAM_LR * ((mn / c1) / (jnp.sqrt(vn / c2) + ADAM_EPS) + ADAM_WD * w_ref[...])

    row = pl.BlockSpec((tm, c), lambda i: (i, 0))
    out = jax.ShapeDtypeStruct((r, c), F32)
    return pl.pallas_call(
        body, name=name, grid=(r // tm,),
        in_specs=[pl.BlockSpec((s, tm, c), lambda i: (0, i, 0)), row, row, row],
        out_specs=[row, row, row, row], out_shape=[out, out, out, out], compiler_params=_cp("parallel"),
    )(gslots, w, m, v)


def _rows(a):
    flat = a.reshape(-1)
    pad = (-flat.shape[0]) % PACK_W
    if pad:
        flat = jnp.concatenate([flat, jnp.zeros((pad,), flat.dtype)])
    return flat.reshape(-1, PACK_W)


def _pack(arrays, row_multiple):
    parts, layout, off = [], [], 0
    for a in arrays:
        rw = _rows(a)
        parts.append(rw)
        layout.append((off, rw.shape[0], a.shape))
        off += rw.shape[0]
    pad = (-off) % row_multiple
    if pad:
        parts.append(jnp.zeros((pad, PACK_W), parts[0].dtype))
    return jnp.concatenate(parts, axis=0), layout


def _unpack(packed, layout):
    out = []
    for off, nrows, shape in layout:
        n = int(np.prod(shape))
        out.append(packed[off:off + nrows].reshape(-1)[:n].reshape(shape))
    return out


def _shard_as_rows(name, shard):
    if name in COL_SHARDED:
        shard = shard.transpose(0, 2, 1)
    return shard.reshape(-1, PACK_W)


def _rows_as_shard(name, rows, shape):
    if name in COL_SHARDED:
        lead, k, ns = shape
        return rows.reshape(lead, ns, k).transpose(0, 2, 1)
    return rows.reshape(shape)


def _row_tile(r):
    return next(tm for tm in (256, 128, 64, 32, 16, 8) if r % tm == 0)


def _ssd_consts(dt_bias, a_log, d_skip):
    pad = LANES - SSD_HEADS
    bias = jnp.pad(dt_bias, (0, pad)).reshape(1, LANES)
    a_neg = jnp.pad(-jnp.exp(a_log), (0, pad)).reshape(1, LANES)
    d_exp = jnp.repeat(d_skip, SSD_HEAD_DIM).reshape(1, SSD_D_INNER)
    return bias, a_neg, d_exp


def _group_layouts(acum):
    t = acum.shape[0]
    a = acum[:, :SSD_HEADS].reshape(t, SSD_GROUPS, HEADS_PER_GROUP)
    return a.transpose(1, 0, 2), a.transpose(1, 2, 0)


def _ssd_fwd(x, p, carried=None):
    hn = _rmsnorm(x, p["mix_norm"], "rmsnorm_fwd")
    proj = _matmul(hn, p["w_in"], "nt", F32, "ssd_in_fwd", tm=TOKEN_ROWS, tn=896, tk=1024)
    act = _ssd_conv_fwd(proj, p["conv_w"], p["conv_b"], "ssd_conv_fwd")
    bias, a_neg, d_exp = _ssd_consts(p["dt_bias"], p["a_log"], p["d"])
    expand = _head_expand()
    xdt, dt, acum = _ssd_dt_fwd(proj, act, bias, a_neg, expand, "ssd_dt_fwd")
    acum_g, acum_gt = _group_layouts(acum)
    if carried is None:
        (y, states), landed = _ssd_scan_fwd(xdt, act, acum_g, acum_gt, "ssd_scan_fwd"), None
    else:
        y, states, landed = _ssd_scan_fwd(xdt, act, acum_g, acum_gt, "ssd_scan_fwd_carrying_gather", carried)
    yn = _gated_norm_fwd(y, act, proj, d_exp, p["norm"], "ssd_gnorm_fwd")
    x_new = _matmul(yn, p["w_out"], "nn", F32, "ssd_out_fwd", add=x, tm=TOKEN_ROWS, tn=1024, tk=2048)
    saved = dict(x=x, hn=hn, proj=proj, act=act, xdt=xdt, dt=dt, acum_g=acum_g, acum_gt=acum_gt, y=y, states=states, yn=yn)
    return x_new, saved, landed


def _ssd_bwd(dx, p, s, carried_of=None):
    bias, a_neg, d_exp = _ssd_consts(p["dt_bias"], p["a_log"], p["d"])
    expand = _head_expand()
    dyn = _matmul(dx, p["w_out"], "nt", F32, "ssd_out_dgrad", tm=TOKEN_ROWS, tn=1024, tk=1024)
    g_w_out = _matmul(s["yn"], dx, "tn", MXU_DTYPE, "ssd_out_wgrad", tm=1024, tn=1024, tk=TOKEN_ROWS)
    dyy, dz, g_norm = _gated_norm_bwd(s["y"], s["act"], s["proj"], d_exp, p["norm"], dyn, "ssd_gnorm_bwd")
    scan_args = (s["xdt"], s["act"], s["acum_g"], s["acum_gt"], s["states"], dyy)
    if carried_of is None:
        (dxdt, dbm, dcm, dacol, darow), landed = _ssd_scan_bwd(*scan_args, "ssd_scan_bwd"), None
    else:
        dxdt, dbm, dcm, dacol, darow, landed = _ssd_scan_bwd(*scan_args, "ssd_scan_bwd_carrying_grads", carried_of(g_w_out))
    t = dx.shape[0]
    dacum = dacol.transpose(1, 0, 2).reshape(t, SSD_HEADS) + darow.transpose(2, 0, 1).reshape(t, SSD_HEADS)
    dacum = jnp.pad(dacum, ((0, 0), (0, LANES - SSD_HEADS)))
    dxs, draw, g_a, g_bias, g_dexp = _ssd_dt_bwd(s["proj"], s["act"], s["dt"], dxdt, dyy, dacum, bias, a_neg, d_exp,
                                                  expand, expand.T, "ssd_dt_bwd")
    dact = jnp.concatenate([dxs, dbm, dcm], axis=1)
    dxbc, g_conv_w, g_conv_b = _ssd_conv_bwd(s["proj"], p["conv_w"], p["conv_b"], dact, "ssd_conv_bwd")
    dproj = jnp.concatenate([dz, dxbc, draw], axis=1)
    dhn = _matmul(dproj, p["w_in"], "nn", F32, "ssd_in_dgrad", tm=TOKEN_ROWS, tn=1024, tk=896)
    g_w_in = _matmul(dproj, s["hn"], "tn", MXU_DTYPE, "ssd_in_wgrad", tm=896, tn=1024, tk=TOKEN_ROWS)
    dx_new, g_mix = _rmsnorm_bwd(s["x"], p["mix_norm"], dhn, dx, "rmsnorm_bwd")
    grads = dict(w_in=g_w_in[:SSD_IN_DIM], w_out=g_w_out, conv_w=g_conv_w, conv_b=g_conv_b.reshape(-1),
                 dt_bias=g_bias[0, :SSD_HEADS], a_log=(g_a * a_neg)[0, :SSD_HEADS],
                 d=g_dexp.reshape(SSD_HEADS, SSD_HEAD_DIM).sum(axis=1), norm=g_norm.reshape(-1), mix_norm=g_mix.reshape(-1))
    return dx_new, grads, landed


def _sb_fwd(x, p, carried=None):
    hn = _rmsnorm(x, p["mix_norm"], "rmsnorm_fwd")
    qkv = _matmul(hn, p["w_qkv"], "nt", MXU_DTYPE, "sb_qkv_fwd", tm=TOKEN_ROWS, tn=1024, tk=1024)
    if carried is None:
        o, landed = _sb_attention_fwd(qkv, "sb_attn_fwd"), None
    else:
        o, landed = _sb_attention_fwd(qkv, "sb_attn_fwd_carrying_gather", carried)
    x_new = _matmul(o, p["w_out"], "nn", F32, "sb_out_fwd", add=x, tm=TOKEN_ROWS, tn=1024, tk=1024)
    return x_new, dict(x=x, hn=hn, qkv=qkv, o=o), landed


def _sb_bwd(dx, p, s, carried_of=None):
    do = _matmul(dx, p["w_out"], "nt", MXU_DTYPE, "sb_out_dgrad", tm=TOKEN_ROWS, tn=1024, tk=1024)
    g_w_out = _matmul(s["o"], dx, "tn", MXU_DTYPE, "sb_out_wgrad", tm=1024, tn=1024, tk=TOKEN_ROWS)
    if carried_of is None:
        (dq, dk, dv), landed = _sb_attention_bwd(s["qkv"], do, "sb_attn_bwd"), None
    else:
        dq, dk, dv, landed = _sb_attention_bwd(s["qkv"], do, "sb_attn_bwd_carrying_grads", carried_of(g_w_out))
    dqkv = jnp.concatenate([dq, dk, dv], axis=1)
    dhn = _matmul(dqkv, p["w_qkv"], "nn", F32, "sb_qkv_dgrad", tm=TOKEN_ROWS, tn=1024, tk=1024)
    g_w_qkv = _matmul(dqkv, s["hn"], "tn", MXU_DTYPE, "sb_qkv_wgrad", tm=1024, tn=1024, tk=TOKEN_ROWS)
    dx_new, g_mix = _rmsnorm_bwd(s["x"], p["mix_norm"], dhn, dx, "rmsnorm_bwd")
    return dx_new, dict(w_qkv=g_w_qkv, w_out=g_w_out, mix_norm=g_mix.reshape(-1)), landed


def _ffn_fwd(x, p):
    hn = _rmsnorm(x, p["ffn_norm"], "rmsnorm_fwd")
    proj = _matmul(hn, p["w_in"], "nt", F32, "ffn_in_fwd", tm=TOKEN_ROWS, tn=1408, tk=1024)
    act = _ffn_conv_fwd(proj, p["conv_w"], p["conv_b"], "ffn_conv_fwd")
    x_new = _matmul(act, p["w_out"], "nn", F32, "ffn_out_fwd", add=x, tm=TOKEN_ROWS, tn=1024, tk=1408)
    return x_new, dict(x=x, hn=hn, proj=proj, act=act)


def _ffn_bwd(dx, p, s):
    dact = _matmul(dx, p["w_out"], "nt", F32, "ffn_out_dgrad", tm=TOKEN_ROWS, tn=1408, tk=1024)
    g_w_out = _matmul(s["act"], dx, "tn", MXU_DTYPE, "ffn_out_wgrad", tm=1408, tn=1024, tk=TOKEN_ROWS)
    dpg, dpu, dwg, dwu, dbg, dbu = _ffn_conv_bwd(s["proj"], p["conv_w"], p["conv_b"], dact, "ffn_conv_bwd")
    dproj = jnp.concatenate([dpg, dpu], axis=1)
    dhn = _matmul(dproj, p["w_in"], "nn", F32, "ffn_in_dgrad", tm=TOKEN_ROWS, tn=1024, tk=1408)
    g_w_in = _matmul(dproj, s["hn"], "tn", MXU_DTYPE, "ffn_in_wgrad", tm=1408, tn=1024, tk=TOKEN_ROWS)
    dx_new, g_norm = _rmsnorm_bwd(s["x"], p["ffn_norm"], dhn, dx, "rmsnorm_bwd")
    grads = dict(w_in=g_w_in, w_out=g_w_out, conv_w=jnp.concatenate([dwg, dwu], axis=1),
                 conv_b=jnp.concatenate([dbg, dbu], axis=1).reshape(-1), ffn_norm=g_norm.reshape(-1))
    return dx_new, grads


ADD_ROWS = 256
BIG = ["ssd_w_in", "sb_w_qkv", "ffn_w_in", "ssd_w_out", "sb_w_out", "ffn_w_out"]
LAYER_PIECES = [[("ssd_w_in", 0), ("ssd_w_out", 0), ("ffn_w_in", 0), ("ffn_w_out", 0)],
                [("sb_w_qkv", 0), ("sb_w_out", 0), ("ffn_w_in", 1), ("ffn_w_out", 1)],
                [("ssd_w_in", 1), ("ssd_w_out", 1), ("ffn_w_in", 2), ("ffn_w_out", 2)],
                [("sb_w_qkv", 1), ("sb_w_out", 1), ("ffn_w_in", 3), ("ffn_w_out", 3)]]
GRAD_SETS = {3: [("ffn_w_in", 3), ("ffn_w_out", 3), ("sb_w_out", 1)],
             1: [("sb_w_qkv", 1), ("ssd_w_out", 1), ("ffn_w_in", 2), ("ffn_w_out", 2), ("ffn_w_in", 1), ("ffn_w_out", 1),
                 ("sb_w_out", 0), ("ssd_w_in", 1)],
             0: [("sb_w_qkv", 0), ("ffn_w_in", 0), ("ffn_w_out", 0), ("ssd_w_out", 0)],
             "end": [("ssd_w_in", 0)]}
GATHER_SETS = {"early": [("ssd_w_out", 0), ("ssd_w_in", 0)],
               0: [("ffn_w_in", 0), ("ffn_w_out", 0), ("sb_w_qkv", 0), ("sb_w_out", 0)],
               1: [("ffn_w_in", 1), ("ffn_w_out", 1), ("ssd_w_out", 1), ("ffn_w_in", 2), ("ffn_w_out", 2)] + LAYER_PIECES[3]
                  + [("ssd_w_in", 1)]}
COL_SHARDED = {"ssd_w_in": 2, "sb_w_qkv": 2, "ffn_w_in": 4}
CONV = ["ssd_conv_w", "ffn_conv_w"]
SMALL = ["mix_norm", "ffn_norm", "final_norm", "ssd_conv_b", "ssd_dt_bias", "ssd_a_log", "ssd_d", "ssd_norm", "ffn_conv_b"]
WEIGHTS = ["mix_norm", "ffn_norm", "final_norm", "ssd_w_in", "ssd_conv_w", "ssd_conv_b", "ssd_dt_bias", "ssd_a_log", "ssd_d",
           "ssd_norm", "ssd_w_out", "sb_w_qkv", "sb_w_out", "ffn_w_in", "ffn_conv_w", "ffn_conv_b", "ffn_w_out"]


def _step(x, loss_target, w, m, v):
    x = x.reshape(x.shape[-2], x.shape[-1])
    target = loss_target.reshape(x.shape)
    dev = 4 * lax.axis_index("x") + 2 * lax.axis_index("y") + lax.axis_index("c")
    core = lax.axis_index("c")

    shard_rows = {n: _shard_as_rows(n, w[n].astype(MXU_DTYPE)) for n in BIG}
    per_shard = {n: shard_rows[n].shape[0] // w[n].shape[0] for n in BIG}

    def layout(pieces):
        where, off = {}, 0
        for n, l in pieces:
            where[(n, l)] = (off, per_shard[n])
            off += per_shard[n]
        return where

    def pack_pieces(pieces, rows_of):
        return jnp.concatenate([rows_of(piece) for piece in pieces], axis=-2)

    def shard_piece(piece):
        n, l = piece
        return shard_rows[n][l * per_shard[n]:(l + 1) * per_shard[n]]

    full = {}

    def unpack_weights(gathered, where):
        for (n, l), (off, rows) in where.items():
            mat = gathered[:, off:off + rows].reshape(N_DEV * rows, PACK_W)
            if n == "ssd_w_in":
                mat = jnp.pad(mat, ((0, SSD_IN_PAD - SSD_IN_DIM), (0, 0)))
            full[(n, l)] = mat

    unpack_weights(_all_gather(pack_pieces(GATHER_SETS["early"], shard_piece), "gather_weights_early"), layout(GATHER_SETS["early"]))
    conv_pack, conv_layout = _pack([w[n] for n in CONV], 8)
    conv_all = _all_gather(conv_pack, "gather_conv_taps")
    for n, (off, nrows, shape) in zip(CONV, conv_layout):
        parts = [_unpack(conv_all[j], conv_layout)[CONV.index(n)] for j in range(N_DEV)]
        full[n] = jnp.concatenate(parts, axis=-1)

    def ssd_params(j):
        return dict(mix_norm=w["mix_norm"][2 * j], w_in=full[("ssd_w_in", j)], conv_w=full["ssd_conv_w"][j],
                    conv_b=w["ssd_conv_b"][j], dt_bias=w["ssd_dt_bias"][j], a_log=w["ssd_a_log"][j], d=w["ssd_d"][j],
                    norm=w["ssd_norm"][j], w_out=full[("ssd_w_out", j)])

    def sb_params(j):
        return dict(mix_norm=w["mix_norm"][2 * j + 1], w_qkv=full[("sb_w_qkv", j)], w_out=full[("sb_w_out", j)])

    def ffn_params(i):
        return dict(ffn_norm=w["ffn_norm"][i], w_in=full[("ffn_w_in", i)], conv_w=full["ffn_conv_w"][i],
                    conv_b=w["ffn_conv_b"][i], w_out=full[("ffn_w_out", i)])

    saved = []
    for i in range(DEPTH):
        mixer_fwd, params = (_ssd_fwd, ssd_params) if i % 2 == 0 else (_sb_fwd, sb_params)
        if i in GATHER_SETS:
            x, s_mix, arrived = mixer_fwd(x, params(i // 2), carried=(pack_pieces(GATHER_SETS[i], shard_piece), False))
            unpack_weights(arrived, layout(GATHER_SETS[i]))
        else:
            x, s_mix, _ = mixer_fwd(x, params(i // 2))
        x, s_ffn = _ffn_fwd(x, ffn_params(i))
        saved.append((s_mix, s_ffn))
    dx, g_final, loss_part = _final_norm_loss(x, w["final_norm"], target, "final_norm_loss")

    piece_grad = {}

    def grad_piece(piece):
        g = piece_grad[piece]
        return g.reshape(N_DEV, g.shape[0] // N_DEV, PACK_W)

    def carried_set(pieces, own_piece):
        def make(g_w_out):
            piece_grad[own_piece] = g_w_out
            return pack_pieces(pieces, grad_piece), True
        return make

    g_mix, g_ffn, g_ssd, g_sb = [None] * DEPTH, [None] * DEPTH, [None] * 2, [None] * 2
    landed = {}
    for i in reversed(range(DEPTH)):
        s_mix, s_ffn = saved[i]
        dx, g_ffn[i] = _ffn_bwd(dx, ffn_params(i), s_ffn)
        piece_grad[("ffn_w_in", i)], piece_grad[("ffn_w_out", i)] = g_ffn[i]["w_in"], g_ffn[i]["w_out"]
        j = i // 2
        if i % 2 == 0:
            carried_of = carried_set(GRAD_SETS[i], ("ssd_w_out", j)) if i in GRAD_SETS else None
            dx, g_ssd[j], landed[i] = _ssd_bwd(dx, ssd_params(j), s_mix, carried_of)
            g_mix[i] = g_ssd[j]["mix_norm"]
            piece_grad[("ssd_w_in", j)], piece_grad[("ssd_w_out", j)] = g_ssd[j]["w_in"], g_ssd[j]["w_out"]
        else:
            dx, g_sb[j], landed[i] = _sb_bwd(dx, sb_params(j), s_mix, carried_set(GRAD_SETS[i], ("sb_w_out", j)))
            g_mix[i] = g_sb[j]["mix_norm"]
            piece_grad[("sb_w_qkv", j)] = g_sb[j]["w_qkv"]
    grad_x = dx.reshape(1, *dx.shape)

    g8 = pack_pieces(GRAD_SETS["end"], grad_piece)
    g8 = jnp.pad(g8, ((0, 0), (0, (-g8.shape[1]) % ADD_ROWS), (0, 0)))
    g8 = g8.reshape(4, 2, *g8.shape[1:])
    keep = lax.dynamic_index_in_dim(g8, core, axis=1, keepdims=False)
    give = lax.dynamic_index_in_dim(g8, 1 - core, axis=1, keepdims=False)
    got = _swap_with_sibling(give, "grads_to_sibling")
    chip_part = _add_pair(keep, got, "grads_add_sibling")
    landed["end"] = _exchange_chips(chip_part, "grads_across_chips")

    summed = {}
    for key, pieces in GRAD_SETS.items():
        total = _sum_slots(landed[key], "grads_sum_landed")
        for piece, (off, rows) in layout(pieces).items():
            summed[piece] = total[off:off + rows]
    big_res = [dict() for _ in range(4)]
    for n in BIG:
        lead, rows, cols = w[n].shape
        g_rows = jnp.concatenate([summed[(n, l)] for l in range(lead)], axis=0)
        g_nat = _rows_as_shard(n, g_rows, w[n].shape).reshape(1, lead * rows, cols)
        two_d = (lead * rows, cols)
        outs = _adamw(g_nat, w[n].reshape(two_d), m[n].reshape(two_d), v[n].reshape(two_d), "adamw_" + n)
        for kind in range(4):
            big_res[kind][n] = outs[kind].reshape(w[n].shape)

    small_g = {
        "mix_norm": jnp.stack(g_mix), "ffn_norm": jnp.stack([g["ffn_norm"] for g in g_ffn]), "final_norm": g_final.reshape(-1),
        "ssd_conv_b": jnp.stack([g["conv_b"] for g in g_ssd]), "ssd_dt_bias": jnp.stack([g["dt_bias"] for g in g_ssd]),
        "ssd_a_log": jnp.stack([g["a_log"] for g in g_ssd]), "ssd_d": jnp.stack([g["d"] for g in g_ssd]),
        "ssd_norm": jnp.stack([g["norm"] for g in g_ssd]), "ffn_conv_b": jnp.stack([g["conv_b"] for g in g_ffn]),
    }
    conv_g = {"ssd_conv_w": jnp.stack([g["conv_w"] for g in g_ssd]), "ffn_conv_w": jnp.stack([g["conv_w"] for g in g_ffn])}
    extra = [conv_g[n] for n in CONV] + [loss_part]
    small_pack, small_layout = _pack([small_g[n] for n in SMALL] + extra, 8)
    small_all = _all_gather(small_pack, "gather_small_grads")
    zeros_like = [jnp.zeros(a.shape, F32) for a in extra]
    sw, _ = _pack([w[n] for n in SMALL] + zeros_like, 8)
    sm, _ = _pack([m[n] for n in SMALL] + zeros_like, 8)
    sv, _ = _pack([v[n] for n in SMALL] + [jnp.ones(a.shape, F32) for a in extra], 8)
    small_out = _adamw(small_all, sw, sm, sv, "adamw_replicated")
    small_res = [_unpack(o, small_layout) for o in small_out]
    summed = small_res[0]
    loss = summed[-1][0, 0]
    conv_shard_g = []
    for n, gsum in zip(CONV, summed[len(SMALL):len(SMALL) + len(CONV)]):
        ns = w[n].shape[-1]
        conv_shard_g.append(lax.dynamic_slice_in_dim(gsum, dev * ns, ns, axis=2))
    cg, conv_sh_layout = _pack(conv_shard_g, 8)
    cw, _ = _pack([w[n] for n in CONV], 8)
    cm_, _ = _pack([m[n] for n in CONV], 8)
    cv, _ = _pack([v[n] for n in CONV], 8)
    conv_out = _adamw(cg.reshape(1, *cg.shape), cw, cm_, cv, "adamw_conv_taps")
    conv_res = [dict(zip(CONV, _unpack(o, conv_sh_layout))) for o in conv_out]

    def pick(kind, n):
        if n in BIG:
            return big_res[kind][n]
        if n in CONV:
            return conv_res[kind][n]
        return small_res[kind][SMALL.index(n)]

    outs = [loss, grad_x]
    for kind in range(4):
        outs += [pick(kind, n) for n in WEIGHTS]
    return tuple(outs)


def kernel(x, mix_norm, ffn_norm, final_norm, ssd_w_in, ssd_conv_w, ssd_conv_b, ssd_dt_bias, ssd_a_log, ssd_d, ssd_norm, ssd_w_out, sb_w_qkv, sb_w_out, ffn_w_in, ffn_conv_w, ffn_conv_b, ffn_w_out, loss_target, m_mix_norm, m_ffn_norm, m_final_norm, m_ssd_w_in, m_ssd_conv_w, m_ssd_conv_b, m_ssd_dt_bias, m_ssd_a_log, m_ssd_d, m_ssd_norm, m_ssd_w_out, m_sb_w_qkv, m_sb_w_out, m_ffn_w_in, m_ffn_conv_w, m_ffn_conv_b, m_ffn_w_out, v_mix_norm, v_ffn_norm, v_final_norm, v_ssd_w_in, v_ssd_conv_w, v_ssd_conv_b, v_ssd_dt_bias, v_ssd_a_log, v_ssd_d, v_ssd_norm, v_ssd_w_out, v_sb_w_qkv, v_sb_w_out, v_ffn_w_in, v_ffn_conv_w, v_ffn_conv_b, v_ffn_w_out):
    w = dict(mix_norm=mix_norm, ffn_norm=ffn_norm, final_norm=final_norm, ssd_w_in=ssd_w_in, ssd_conv_w=ssd_conv_w,
             ssd_conv_b=ssd_conv_b, ssd_dt_bias=ssd_dt_bias, ssd_a_log=ssd_a_log, ssd_d=ssd_d, ssd_norm=ssd_norm,
             ssd_w_out=ssd_w_out, sb_w_qkv=sb_w_qkv, sb_w_out=sb_w_out, ffn_w_in=ffn_w_in, ffn_conv_w=ffn_conv_w,
             ffn_conv_b=ffn_conv_b, ffn_w_out=ffn_w_out)
    m = dict(mix_norm=m_mix_norm, ffn_norm=m_ffn_norm, final_norm=m_final_norm, ssd_w_in=m_ssd_w_in, ssd_conv_w=m_ssd_conv_w,
             ssd_conv_b=m_ssd_conv_b, ssd_dt_bias=m_ssd_dt_bias, ssd_a_log=m_ssd_a_log, ssd_d=m_ssd_d, ssd_norm=m_ssd_norm,
             ssd_w_out=m_ssd_w_out, sb_w_qkv=m_sb_w_qkv, sb_w_out=m_sb_w_out, ffn_w_in=m_ffn_w_in, ffn_conv_w=m_ffn_conv_w,
             ffn_conv_b=m_ffn_conv_b, ffn_w_out=m_ffn_w_out)
    v = dict(mix_norm=v_mix_norm, ffn_norm=v_ffn_norm, final_norm=v_final_norm, ssd_w_in=v_ssd_w_in, ssd_conv_w=v_ssd_conv_w,
             ssd_conv_b=v_ssd_conv_b, ssd_dt_bias=v_ssd_dt_bias, ssd_a_log=v_ssd_a_log, ssd_d=v_ssd_d, ssd_norm=v_ssd_norm,
             ssd_w_out=v_ssd_w_out, sb_w_qkv=v_sb_w_qkv, sb_w_out=v_sb_w_out, ffn_w_in=v_ffn_w_in, ffn_conv_w=v_ffn_conv_w,
             ffn_conv_b=v_ffn_conv_b, ffn_w_out=v_ffn_w_out)
    return _step(x, loss_target, w, m, v)
```

```python
import jax
import jax.numpy as jnp
import numpy as np
from jax import lax
from jax.experimental import pallas as pl
from jax.experimental.pallas import tpu as pltpu

F32 = jnp.float32
MXU_DTYPE = jnp.bfloat16
MESH_ID = pl.DeviceIdType.MESH
N_DEV = 8

NORM_EPS = 1e-6
D_MODEL = 1024
DEPTH = 4
SSD_D_INNER = 2048
SSD_HEADS = 32
SSD_HEAD_DIM = 64
SSD_GROUPS = 8
SSD_STATE = 128
SSD_CONV = 4
SSD_CHUNK = 128
SSD_CONV_DIM = SSD_D_INNER + 2 * SSD_GROUPS * SSD_STATE
SSD_IN_DIM = SSD_D_INNER + SSD_CONV_DIM + SSD_HEADS
LANES = 128
SSD_IN_PAD = SSD_D_INNER + SSD_CONV_DIM + LANES
SB_HEADS = 16
SB_HEAD_DIM = 64
SB_TILE = 256
SB_SCALE = SB_HEAD_DIM ** -0.5
FFN_D_FF = 2816
FFN_CONV = 3
PACK_W = 1024

ADAM_LR = 0.001
ADAM_B1 = 0.9
ADAM_B2 = 0.999
ADAM_EPS = 1e-08
ADAM_WD = 0.01
ADAM_STEP = 10

VMEM_LIMIT_BYTES = 56 * 1024 * 1024


def _cp(*sem):
    return pltpu.CompilerParams(dimension_semantics=sem, vmem_limit_bytes=VMEM_LIMIT_BYTES)


def _iota(shape, dim):
    return lax.broadcasted_iota(jnp.int32, shape, dim)


def _sigmoid(x):
    return 1.0 / (1.0 + jnp.exp(-x))


def _mm(a, b):
    return lax.dot_general(a, b, (((1,), (0,)), ((), ())), preferred_element_type=F32)


def _mm_nt(a, b):
    return lax.dot_general(a, b, (((1,), (1,)), ((), ())), preferred_element_type=F32)


def _mm_tn(a, b):
    return lax.dot_general(a, b, (((0,), (0,)), ((), ())), preferred_element_type=F32)


def _split(x):
    hi = x.astype(MXU_DTYPE)
    lo = (x - hi.astype(F32)).astype(MXU_DTYPE)
    return hi, lo


def _mm_exact_rhs(x, m):
    hi, lo = _split(x)
    return _mm(jnp.concatenate([hi, lo], axis=1), jnp.concatenate([m, m], axis=0))


def _mm_exact_lhs(m, x):
    hi, lo = _split(x)
    return _mm(jnp.concatenate([m, m], axis=1), jnp.concatenate([hi, lo], axis=0))


def _my_place():
    return lax.axis_index("x"), lax.axis_index("y"), lax.axis_index("c")


def _gather_phase(phase, x_ref, out_ref, send_sems, recv_sems, local_sem):
    x, y, c = _my_place()
    me, sibling = (x, y, c), (x, y, 1 - c)
    chips = [(1 - x, y), (x, 1 - y), (1 - x, 1 - y)]

    def slot(px, py, pc):
        return out_ref.at[4 * px + 2 * py + pc]

    def copy(k, block, to, src=None):
        return pltpu.make_async_remote_copy(
            src_ref=slot(*block) if src is None else src, dst_ref=slot(*block),
            send_sem=send_sems.at[k], recv_sem=recv_sems.at[k], device_id=to, device_id_type=MESH_ID)

    if phase == "send":
        pltpu.make_async_copy(x_ref, slot(*me), local_sem).start()
        copy(0, me, sibling, src=x_ref).start()
        for j, chip in enumerate(chips):
            copy(1 + j, me, (*chip, c), src=x_ref).start()
    elif phase == "pass_on":
        for j, chip in enumerate(chips):
            copy(1 + j, (*chip, c), me).wait_recv()
            copy(4 + j, (*chip, c), sibling).start()
    else:
        copy(0, sibling, me).wait_recv()
        for j, chip in enumerate(chips):
            copy(4 + j, (*chip, 1 - c), me).wait_recv()
        copy(0, me, sibling, src=x_ref).wait_send()
        for j, chip in enumerate(chips):
            copy(1 + j, me, (*chip, c), src=x_ref).wait_send()
            copy(4 + j, (*chip, c), sibling).wait_send()
        pltpu.make_async_copy(x_ref, slot(*me), local_sem).wait()


def _all_gather(shard, name):
    r, c_ = shard.shape

    def body(*refs):
        _gather_phase("send", *refs)
        _gather_phase("pass_on", *refs)
        _gather_phase("finish", *refs)

    return pl.pallas_call(
        body, name=name,
        out_shape=jax.ShapeDtypeStruct((N_DEV, r, c_), shard.dtype),
        in_specs=[pl.BlockSpec(memory_space=pl.ANY)],
        out_specs=pl.BlockSpec(memory_space=pl.ANY),
        scratch_shapes=[pltpu.SemaphoreType.DMA((7,)), pltpu.SemaphoreType.DMA((7,)), pltpu.SemaphoreType.DMA(())],
    )(shard)


def _swap_with_sibling(buf, name):
    def body(x_ref, out_ref, send_sem, recv_sem):
        x, y, c = _my_place()
        cp = pltpu.make_async_remote_copy(src_ref=x_ref, dst_ref=out_ref, send_sem=send_sem, recv_sem=recv_sem,
                                          device_id=(x, y, 1 - c), device_id_type=MESH_ID)
        cp.start()
        cp.wait()

    return pl.pallas_call(
        body, name=name, out_shape=jax.ShapeDtypeStruct(buf.shape, buf.dtype),
        in_specs=[pl.BlockSpec(memory_space=pl.ANY)], out_specs=pl.BlockSpec(memory_space=pl.ANY),
        scratch_shapes=[pltpu.SemaphoreType.DMA(()), pltpu.SemaphoreType.DMA(())],
    )(buf)


def _exchange_chips(parts, name):
    def body(p_ref, out_ref, send_sems, recv_sems, local_sem):
        x, y, c = _my_place()
        my_q = 2 * x + y
        chips = [(1 - x, y), (x, 1 - y), (1 - x, 1 - y)]
        local = pltpu.make_async_copy(p_ref.at[my_q], out_ref.at[my_q], local_sem)
        local.start()

        def copy(k, px, py):
            return pltpu.make_async_remote_copy(
                src_ref=p_ref.at[2 * px + py], dst_ref=out_ref.at[my_q],
                send_sem=send_sems.at[k], recv_sem=recv_sems.at[k], device_id=(px, py, c), device_id_type=MESH_ID)

        def landing(k, px, py):
            return pltpu.make_async_remote_copy(
                src_ref=p_ref.at[my_q], dst_ref=out_ref.at[2 * px + py],
                send_sem=send_sems.at[k], recv_sem=recv_sems.at[k], device_id=(px, py, c), device_id_type=MESH_ID)

        sends = [copy(k, px, py) for k, (px, py) in enumerate(chips)]
        for cp in sends:
            cp.start()
        for k, (px, py) in enumerate(chips):
            landing(k, px, py).wait_recv()
        for cp in sends:
            cp.wait_send()
        local.wait()

    return pl.pallas_call(
        body, name=name, out_shape=jax.ShapeDtypeStruct(parts.shape, parts.dtype),
        in_specs=[pl.BlockSpec(memory_space=pl.ANY)], out_specs=pl.BlockSpec(memory_space=pl.ANY),
        scratch_shapes=[pltpu.SemaphoreType.DMA((3,)), pltpu.SemaphoreType.DMA((3,)), pltpu.SemaphoreType.DMA(())],
    )(parts)


RELATIONS = [(0, 0, 1), (1, 0, 0), (0, 1, 0), (1, 1, 0), (1, 0, 1), (0, 1, 1), (1, 1, 1)]
EXCHANGE_SCRATCH = [pltpu.SemaphoreType.DMA((len(RELATIONS),)), pltpu.SemaphoreType.DMA((len(RELATIONS),)),
                    pltpu.SemaphoreType.DMA(())]


def _exchange_copies(src_ref, land_ref, send_sems, recv_sems, local_sem, per_peer, incoming=True):
    x, y, c = _my_place()
    me = 4 * x + 2 * y + c

    def src(j):
        return src_ref.at[j] if per_peer else src_ref

    local = pltpu.make_async_copy(src(me), land_ref.at[me], local_sem)
    pairs = []
    for k, (dx, dy, dc) in enumerate(RELATIONS):
        peer = (1 - x if dx else x, 1 - y if dy else y, 1 - c if dc else c)
        j = 4 * peer[0] + 2 * peer[1] + peer[2]
        sems = dict(send_sem=send_sems.at[k], recv_sem=recv_sems.at[k], device_id=peer, device_id_type=MESH_ID)
        pairs.append((pltpu.make_async_remote_copy(src_ref=src(j), dst_ref=land_ref.at[me], **sems),
                      pltpu.make_async_remote_copy(src_ref=src(me), dst_ref=land_ref.at[j], **sems) if incoming else None))
    return local, pairs


def _exchange_start(*refs, per_peer):
    local, pairs = _exchange_copies(*refs, per_peer, incoming=False)
    local.start()
    for outgoing, _ in pairs:
        outgoing.start()


def _exchange_finish(*refs, per_peer):
    local, pairs = _exchange_copies(*refs, per_peer)
    for _, incoming in pairs:
        incoming.wait_recv()
    for outgoing, _ in pairs:
        outgoing.wait_send()
    local.wait()


TOKEN_ROWS = 1024


def _matmul(a, b, mode, out_dtype, name, add=None, tm=512, tn=512, tk=512):
    if mode == "nn":
        (m, k), (k2, n) = a.shape, b.shape
    elif mode == "nt":
        (m, k), (n, k2) = a.shape, b.shape
    else:
        (k, m), (k2, n) = a.shape, b.shape
    assert k == k2, (a.shape, b.shape, mode)
    tm, tn, tk = min(tm, m), min(tn, n), min(tk, k)
    assert m % tm == 0 and n % tn == 0 and k % tk == 0, (m, n, k, tm, tn, tk)
    nk = k // tk
    mm = {"nn": _mm, "nt": _mm_nt, "tn": _mm_tn}[mode]

    def body(*refs):
        if add is None:
            a_ref, b_ref, o_ref, acc_ref = refs
        else:
            a_ref, b_ref, add_ref, o_ref, acc_ref = refs
        kk = pl.program_id(2)

        @pl.when(kk == 0)
        def _():
            acc_ref[...] = jnp.zeros_like(acc_ref)

        acc_ref[...] += mm(a_ref[...].astype(MXU_DTYPE), b_ref[...].astype(MXU_DTYPE))

        @pl.when(kk == nk - 1)
        def _():
            res = acc_ref[...]
            if add is not None:
                res = res + add_ref[...]
            o_ref[...] = res.astype(o_ref.dtype)

    a_spec = {"nn": pl.BlockSpec((tm, tk), lambda i, j, kk: (i, kk)),
              "nt": pl.BlockSpec((tm, tk), lambda i, j, kk: (i, kk)),
              "tn": pl.BlockSpec((tk, tm), lambda i, j, kk: (kk, i))}[mode]
    b_spec = {"nn": pl.BlockSpec((tk, tn), lambda i, j, kk: (kk, j)),
              "nt": pl.BlockSpec((tn, tk), lambda i, j, kk: (j, kk)),
              "tn": pl.BlockSpec((tk, tn), lambda i, j, kk: (kk, j))}[mode]
    o_spec = pl.BlockSpec((tm, tn), lambda i, j, kk: (i, j))
    in_specs, args = [a_spec, b_spec], [a, b]
    if add is not None:
        in_specs.append(o_spec)
        args.append(add)
    return pl.pallas_call(
        body, name=name, grid=(m // tm, n // tn, nk), in_specs=in_specs, out_specs=o_spec,
        out_shape=jax.ShapeDtypeStruct((m, n), out_dtype),
        scratch_shapes=[pltpu.VMEM((tm, tn), F32)],
        compiler_params=_cp("parallel", "parallel", "arbitrary"),
    )(*args)


def _rmsnorm(x, g, name):
    t, d = x.shape
    tm = min(512, t)

    def body(x_ref, g_ref, o_ref):
        xv = x_ref[...]
        r = lax.rsqrt(jnp.mean(xv * xv, axis=-1, keepdims=True) + NORM_EPS)
        o_ref[...] = (xv * r * g_ref[...]).astype(o_ref.dtype)

    return pl.pallas_call(
        body, name=name, grid=(t // tm,),
        in_specs=[pl.BlockSpec((tm, d), lambda i: (i, 0)), pl.BlockSpec((1, d), lambda i: (0, 0))],
        out_specs=pl.BlockSpec((tm, d), lambda i: (i, 0)),
        out_shape=jax.ShapeDtypeStruct((t, d), MXU_DTYPE), compiler_params=_cp("parallel"),
    )(x, g.reshape(1, d))


def _rmsnorm_bwd(x, g, dh, dres, name):
    t, d = x.shape
    tm = min(512, t)

    def body(x_ref, g_ref, dh_ref, dres_ref, dx_ref, dg_ref):
        @pl.when(pl.program_id(0) == 0)
        def _():
            dg_ref[...] = jnp.zeros_like(dg_ref)

        xv = x_ref[...]
        r = lax.rsqrt(jnp.mean(xv * xv, axis=-1, keepdims=True) + NORM_EPS)
        xn = xv * r
        dhv = dh_ref[...]
        u = dhv * g_ref[...]
        dx_ref[...] = dres_ref[...] + r * (u - xn * jnp.mean(u * xn, axis=-1, keepdims=True))
        dg_ref[...] += jnp.sum(dhv * xn, axis=0, keepdims=True)

    row = pl.BlockSpec((tm, d), lambda i: (i, 0))
    vec = pl.BlockSpec((1, d), lambda i: (0, 0))
    return pl.pallas_call(
        body, name=name, grid=(t // tm,), in_specs=[row, vec, row, row], out_specs=[row, vec],
        out_shape=[jax.ShapeDtypeStruct((t, d), F32), jax.ShapeDtypeStruct((1, d), F32)],
        compiler_params=_cp("arbitrary"),
    )(x, g.reshape(1, d), dh, dres)


def _final_norm_loss(x, g, target, name):
    t, d = x.shape
    tm = min(512, t)

    def body(x_ref, g_ref, t_ref, dx_ref, dg_ref, loss_ref):
        @pl.when(pl.program_id(0) == 0)
        def _():
            dg_ref[...] = jnp.zeros_like(dg_ref)
            loss_ref[...] = jnp.zeros_like(loss_ref)

        xv = x_ref[...]
        gv = g_ref[...]
        r = lax.rsqrt(jnp.mean(xv * xv, axis=-1, keepdims=True) + NORM_EPS)
        xn = xv * r
        err = xn * gv - t_ref[...]
        per_tok = jnp.mean(err * err, axis=-1, keepdims=True)
        loss_ref[...] += jnp.broadcast_to(0.5 * jnp.sum(per_tok, axis=0, keepdims=True), loss_ref.shape)
        dy = err * (1.0 / d)
        u = dy * gv
        dx_ref[...] = r * (u - xn * jnp.mean(u * xn, axis=-1, keepdims=True))
        dg_ref[...] += jnp.sum(dy * xn, axis=0, keepdims=True)

    row = pl.BlockSpec((tm, d), lambda i: (i, 0))
    vec = pl.BlockSpec((1, d), lambda i: (0, 0))
    return pl.pallas_call(
        body, name=name, grid=(t // tm,), in_specs=[row, vec, row],
        out_specs=[row, vec, pl.BlockSpec((1, LANES), lambda i: (0, 0))],
        out_shape=[jax.ShapeDtypeStruct((t, d), F32), jax.ShapeDtypeStruct((1, d), F32),
                   jax.ShapeDtypeStruct((1, LANES), F32)],
        compiler_params=_cp("arbitrary"),
    )(x, g.reshape(1, d), target)


CONV_COLS = 128


def _shifts_down(p, width):
    row = _iota(p.shape, 0)
    return [jnp.where(row >= s, pltpu.roll(p, s, axis=0), 0.0) for s in range(1, width)]


def _shifts_up(p, width):
    n = p.shape[0]
    row = _iota(p.shape, 0)
    return [jnp.where(row < n - s, pltpu.roll(p, n - s, axis=0), 0.0) for s in range(1, width)]


def _conv_pre(p, shifted, w_ref, b_ref):
    width = w_ref.shape[0]
    u = b_ref[...] + w_ref[width - 1:width, :] * p
    for s in range(1, width):
        u = u + w_ref[width - 1 - s:width - s, :] * shifted[s - 1]
    return u


def _conv_transpose(du, w_ref):
    width = w_ref.shape[0]
    shifted = _shifts_up(du, width)
    dp = w_ref[width - 1:width, :] * du
    for s in range(1, width):
        dp = dp + w_ref[width - 1 - s:width - s, :] * shifted[s - 1]
    return dp


def _conv_wgrad(du, p, shifted, dw_ref, db_ref):
    width = dw_ref.shape[0]
    db_ref[...] = jnp.sum(du, axis=0, keepdims=True)
    dw_ref[width - 1:width, :] = jnp.sum(du * p, axis=0, keepdims=True)
    for s in range(1, width):
        dw_ref[width - 1 - s:width - s, :] = jnp.sum(du * shifted[s - 1], axis=0, keepdims=True)


def _ssd_conv_fwd(proj, w, b, name):
    t = proj.shape[0]
    cb = CONV_COLS
    off = SSD_D_INNER // cb

    def body(p_ref, w_ref, b_ref, o_ref):
        p = p_ref[...]
        u = _conv_pre(p, _shifts_down(p, SSD_CONV), w_ref, b_ref)
        o_ref[...] = u * _sigmoid(u)

    return pl.pallas_call(
        body, name=name, grid=(SSD_CONV_DIM // cb,),
        in_specs=[pl.BlockSpec((t, cb), lambda j: (0, j + off)), pl.BlockSpec((SSD_CONV, cb), lambda j: (0, j)),
                  pl.BlockSpec((1, cb), lambda j: (0, j))],
        out_specs=pl.BlockSpec((t, cb), lambda j: (0, j)),
        out_shape=jax.ShapeDtypeStruct((t, SSD_CONV_DIM), F32), compiler_params=_cp("parallel"),
    )(proj, w, b.reshape(1, -1))


def _ssd_conv_bwd(proj, w, b, dact, name):
    t = proj.shape[0]
    cb = CONV_COLS
    off = SSD_D_INNER // cb

    def body(p_ref, w_ref, b_ref, da_ref, dp_ref, dw_ref, db_ref):
        p = p_ref[...]
        shifted = _shifts_down(p, SSD_CONV)
        u = _conv_pre(p, shifted, w_ref, b_ref)
        sg = _sigmoid(u)
        du = da_ref[...] * (sg * (1.0 + u * (1.0 - sg)))
        dp_ref[...] = _conv_transpose(du, w_ref).astype(dp_ref.dtype)
        _conv_wgrad(du, p, shifted, dw_ref, db_ref)

    col = pl.BlockSpec((t, cb), lambda j: (0, j))
    wspec = pl.BlockSpec((SSD_CONV, cb), lambda j: (0, j))
    bspec = pl.BlockSpec((1, cb), lambda j: (0, j))
    return pl.pallas_call(
        body, name=name, grid=(SSD_CONV_DIM // cb,),
        in_specs=[pl.BlockSpec((t, cb), lambda j: (0, j + off)), wspec, bspec, col],
        out_specs=[col, wspec, bspec],
        out_shape=[jax.ShapeDtypeStruct((t, SSD_CONV_DIM), MXU_DTYPE), jax.ShapeDtypeStruct((SSD_CONV, SSD_CONV_DIM), F32),
                   jax.ShapeDtypeStruct((1, SSD_CONV_DIM), F32)],
        compiler_params=_cp("parallel"),
    )(proj, w, b.reshape(1, -1), dact)


def _ffn_conv_fwd(proj, w, b, name):
    t = proj.shape[0]
    cb = CONV_COLS
    nb = FFN_D_FF // cb

    def body(pg_ref, pu_ref, wg_ref, wu_ref, bg_ref, bu_ref, o_ref):
        pg, pu = pg_ref[...], pu_ref[...]
        ug = _conv_pre(pg, _shifts_down(pg, FFN_CONV), wg_ref, bg_ref)
        uu = _conv_pre(pu, _shifts_down(pu, FFN_CONV), wu_ref, bu_ref)
        o_ref[...] = (ug * _sigmoid(ug) * uu).astype(o_ref.dtype)

    gcol = pl.BlockSpec((t, cb), lambda j: (0, j))
    ucol = pl.BlockSpec((t, cb), lambda j: (0, j + nb))
    b2 = b.reshape(1, -1)
    return pl.pallas_call(
        body, name=name, grid=(nb,),
        in_specs=[gcol, ucol, pl.BlockSpec((FFN_CONV, cb), lambda j: (0, j)), pl.BlockSpec((FFN_CONV, cb), lambda j: (0, j + nb)),
                  pl.BlockSpec((1, cb), lambda j: (0, j)), pl.BlockSpec((1, cb), lambda j: (0, j + nb))],
        out_specs=gcol, out_shape=jax.ShapeDtypeStruct((t, FFN_D_FF), MXU_DTYPE), compiler_params=_cp("parallel"),
    )(proj, proj, w, w, b2, b2)


def _ffn_conv_bwd(proj, w, b, dact, name):
    t = proj.shape[0]
    cb = CONV_COLS
    nb = FFN_D_FF // cb

    def body(pg_ref, pu_ref, wg_ref, wu_ref, bg_ref, bu_ref, da_ref,
             dpg_ref, dpu_ref, dwg_ref, dwu_ref, dbg_ref, dbu_ref):
        pg, pu = pg_ref[...], pu_ref[...]
        pg_shifted, pu_shifted = _shifts_down(pg, FFN_CONV), _shifts_down(pu, FFN_CONV)
        ug = _conv_pre(pg, pg_shifted, wg_ref, bg_ref)
        uu = _conv_pre(pu, pu_shifted, wu_ref, bu_ref)
        sg = _sigmoid(ug)
        da = da_ref[...]
        dug = da * uu * (sg * (1.0 + ug * (1.0 - sg)))
        duu = da * (ug * sg)
        dpg_ref[...] = _conv_transpose(dug, wg_ref).astype(dpg_ref.dtype)
        dpu_ref[...] = _conv_transpose(duu, wu_ref).astype(dpu_ref.dtype)
        _conv_wgrad(dug, pg, pg_shifted, dwg_ref, dbg_ref)
        _conv_wgrad(duu, pu, pu_shifted, dwu_ref, dbu_ref)

    gcol = pl.BlockSpec((t, cb), lambda j: (0, j))
    ucol = pl.BlockSpec((t, cb), lambda j: (0, j + nb))
    wg = pl.BlockSpec((FFN_CONV, cb), lambda j: (0, j))
    wu = pl.BlockSpec((FFN_CONV, cb), lambda j: (0, j + nb))
    bg = pl.BlockSpec((1, cb), lambda j: (0, j))
    bu = pl.BlockSpec((1, cb), lambda j: (0, j + nb))
    b2 = b.reshape(1, -1)
    half = jax.ShapeDtypeStruct((t, FFN_D_FF), MXU_DTYPE)
    return pl.pallas_call(
        body, name=name, grid=(nb,),
        in_specs=[gcol, ucol, wg, wu, bg, bu, gcol],
        out_specs=[gcol, gcol, wg, wg, bg, bg],
        out_shape=[half, half, jax.ShapeDtypeStruct((FFN_CONV, FFN_D_FF), F32), jax.ShapeDtypeStruct((FFN_CONV, FFN_D_FF), F32),
                   jax.ShapeDtypeStruct((1, FFN_D_FF), F32), jax.ShapeDtypeStruct((1, FFN_D_FF), F32)],
        compiler_params=_cp("parallel"),
    )(proj, proj, w, w, b2, b2, dact)


SSD_ROWS = 256
DT_COL = (SSD_D_INNER + SSD_CONV_DIM) // LANES


def _head_expand():
    h = np.arange(LANES)[:, None]
    col = np.arange(SSD_D_INNER)[None, :]
    return jnp.asarray((col // SSD_HEAD_DIM == h), MXU_DTYPE)


def _chunk_tri(n, lower):
    t = _iota((n, n), 0)
    s = _iota((n, n), 1)
    shift = SSD_CHUNK.bit_length() - 1
    same = jnp.right_shift(t, shift) == jnp.right_shift(s, shift)
    tri = (s <= t) if lower else (s >= t)
    return jnp.where(same & tri, 1.0, 0.0).astype(MXU_DTYPE)


def _softplus(x):
    return jnp.maximum(x, 0.0) + jnp.log(1.0 + jnp.exp(-jnp.abs(x)))


def _ssd_dt_fwd(proj, act, dt_bias, a_neg, expand, name):
    t = proj.shape[0]
    tm = min(SSD_ROWS, t)

    def body(raw_ref, xs_ref, bias_ref, a_ref, e_ref, xdt_ref, dt_ref, acum_ref):
        lane = _iota((tm, LANES), 1)
        dt = jnp.where(lane < SSD_HEADS, _softplus(raw_ref[...] + bias_ref[...]), 0.0)
        dt_ref[...] = dt
        xdt_ref[...] = xs_ref[...] * _mm_exact_rhs(dt, e_ref[...])
        acum_ref[...] = _mm_exact_lhs(_chunk_tri(tm, True), a_ref[...] * dt)

    vec = pl.BlockSpec((1, LANES), lambda i: (0, 0))
    return pl.pallas_call(
        body, name=name, grid=(t // tm,),
        in_specs=[pl.BlockSpec((tm, LANES), lambda i: (i, DT_COL)), pl.BlockSpec((tm, SSD_D_INNER), lambda i: (i, 0)),
                  vec, vec, pl.BlockSpec((LANES, SSD_D_INNER), lambda i: (0, 0))],
        out_specs=[pl.BlockSpec((tm, SSD_D_INNER), lambda i: (i, 0)), pl.BlockSpec((tm, LANES), lambda i: (i, 0)),
                   pl.BlockSpec((tm, LANES), lambda i: (i, 0))],
        out_shape=[jax.ShapeDtypeStruct((t, SSD_D_INNER), F32), jax.ShapeDtypeStruct((t, LANES), F32),
                   jax.ShapeDtypeStruct((t, LANES), F32)],
        compiler_params=_cp("parallel"),
    )(proj, act, dt_bias, a_neg, expand)


def _ssd_dt_bwd(proj, act, dt, dxdt, dyy, dacum, dt_bias, a_neg, d_exp, expand, expand_t, name):
    t = proj.shape[0]
    tm = min(SSD_ROWS, t)

    def body(raw_ref, xs_ref, dt_ref, dxdt_ref, dyy_ref, dac_ref, bias_ref, a_ref, dsk_ref, e_ref, et_ref,
             dxs_ref, draw_ref, da_ref, dbias_ref, dd_ref):
        @pl.when(pl.program_id(0) == 0)
        def _():
            da_ref[...] = jnp.zeros_like(da_ref)
            dbias_ref[...] = jnp.zeros_like(dbias_ref)
            dd_ref[...] = jnp.zeros_like(dd_ref)

        lane = _iota((tm, LANES), 1)
        xs, dt, dxdt, dyy = xs_ref[...], dt_ref[...], dxdt_ref[...], dyy_ref[...]
        dxs_ref[...] = dxdt * _mm_exact_rhs(dt, e_ref[...]) + dsk_ref[...] * dyy
        dd_ref[...] += jnp.sum(dyy * xs, axis=0, keepdims=True)
        ddt = _mm_exact_rhs(dxdt * xs, et_ref[...])
        da = _mm_exact_lhs(_chunk_tri(tm, False), dac_ref[...])
        ddt = ddt + da * a_ref[...]
        da_ref[...] += jnp.sum(da * dt, axis=0, keepdims=True)
        draw = jnp.where(lane < SSD_HEADS, ddt * _sigmoid(raw_ref[...] + bias_ref[...]), 0.0)
        dbias_ref[...] += jnp.sum(draw, axis=0, keepdims=True)
        draw_ref[...] = draw.astype(draw_ref.dtype)

    wide = pl.BlockSpec((tm, SSD_D_INNER), lambda i: (i, 0))
    thin = pl.BlockSpec((tm, LANES), lambda i: (i, 0))
    vec = pl.BlockSpec((1, LANES), lambda i: (0, 0))
    wvec = pl.BlockSpec((1, SSD_D_INNER), lambda i: (0, 0))
    return pl.pallas_call(
        body, name=name, grid=(t // tm,),
        in_specs=[pl.BlockSpec((tm, LANES), lambda i: (i, DT_COL)), wide, thin, wide, wide, thin, vec, vec, wvec,
                  pl.BlockSpec((LANES, SSD_D_INNER), lambda i: (0, 0)), pl.BlockSpec((SSD_D_INNER, LANES), lambda i: (0, 0))],
        out_specs=[wide, thin, vec, vec, wvec],
        out_shape=[jax.ShapeDtypeStruct((t, SSD_D_INNER), F32), jax.ShapeDtypeStruct((t, LANES), MXU_DTYPE),
                   jax.ShapeDtypeStruct((1, LANES), F32), jax.ShapeDtypeStruct((1, LANES), F32),
                   jax.ShapeDtypeStruct((1, SSD_D_INNER), F32)],
        compiler_params=_cp("arbitrary"),
    )(proj, act, dt, dxdt, dyy, dacum, dt_bias, a_neg, d_exp, expand, expand_t)


SSD_PAIR = 2 * SSD_HEAD_DIM
HEADS_PER_GROUP = SSD_HEADS // SSD_GROUPS
GROUP_COLS = HEADS_PER_GROUP * SSD_HEAD_DIM
B_COL0 = SSD_D_INNER // SSD_STATE
C_COL0 = (SSD_D_INNER + SSD_GROUPS * SSD_STATE) // SSD_STATE


SCAN_GROUPS = 4
SCAN_STEPS = SSD_GROUPS // SCAN_GROUPS


def _scan_step_is(g, c):
    return (pl.program_id(0) == g) & (pl.program_id(1) == c)


def _ssd_scan_fwd(xdt, act, acum_g, acum_gt, name, carried=None):
    t = xdt.shape[0]
    nc = t // SSD_CHUNK
    ln = SSD_CHUNK
    c_in_specs, c_in, c_out_specs, c_out = _carried_specs(carried)

    def body(*refs):
        if carried is None:
            x_ref, b_ref, c_ref, ac_ref, act_ref, y_ref, sst_ref, state = refs
            comm_refs = None
        else:
            x_ref, b_ref, c_ref, ac_ref, act_ref, src_ref, y_ref, sst_ref, land_ref, state, send_sems, recv_sems, local_sem = refs
            comm_refs = (src_ref, land_ref, send_sems, recv_sems, local_sem)
        finish = _carried_hooks(carried, comm_refs, _scan_step_is(0, 0), _scan_step_is(SCAN_STEPS - 1, (3 * nc) // 4),
                                _scan_step_is(SCAN_STEPS - 1, nc - 1))

        @pl.when(pl.program_id(1) == 0)
        def _():
            state[...] = jnp.zeros_like(state)

        causal = _iota((ln, ln), 1) <= _iota((ln, ln), 0)
        lo_mask = _iota((ln, SSD_PAIR), 1) < SSD_HEAD_DIM
        lo_rows = _iota((SSD_PAIR, SSD_STATE), 0) < SSD_HEAD_DIM
        for gg in range(SCAN_GROUPS):
            sst_ref[0, gg] = state[gg * GROUP_COLS:(gg + 1) * GROUP_COLS, :]
            bm = b_ref[:, gg * SSD_STATE:(gg + 1) * SSD_STATE].astype(MXU_DTYPE)
            cm = c_ref[:, gg * SSD_STATE:(gg + 1) * SSD_STATE].astype(MXU_DTYPE)
            cb = _mm_nt(cm, bm)
            ac, act_ = ac_ref[gg], act_ref[gg]
            e_last = jnp.exp(ac[ln - 1:ln, :])
            ac_rows = [jnp.broadcast_to(ac[:, h:h + 1], (ln, SSD_PAIR)) for h in range(HEADS_PER_GROUP)]
            for pr in range(2):
                first = gg * GROUP_COLS + pr * SSD_PAIR
                cols = slice(first, first + SSD_PAIR)
                xp = x_ref[:, cols]
                sp = state[cols, :]
                pair_ac = jnp.where(lo_mask, ac_rows[2 * pr], ac_rows[2 * pr + 1])
                ydiag = jnp.zeros((ln, SSD_PAIR), F32)
                for hh in range(2):
                    h = 2 * pr + hh
                    seg = ac_rows[h] - act_[h:h + 1, :]
                    dec = jnp.exp(jnp.where(causal, seg, -1e30))
                    mask = lo_mask if hh == 0 else jnp.logical_not(lo_mask)
                    ydiag = ydiag + _mm((cb * dec).astype(MXU_DTYPE), jnp.where(mask, xp, 0.0).astype(MXU_DTYPE))
                yoff = _mm_nt(cm, sp.astype(MXU_DTYPE)) * jnp.exp(pair_ac)
                y_ref[:, cols] = ydiag + yoff
                xw = (xp * jnp.exp(pair_ac[ln - 1:ln, :] - pair_ac)).astype(MXU_DTYPE)
                el = jnp.where(lo_rows, e_last[:, 2 * pr:2 * pr + 1], e_last[:, 2 * pr + 1:2 * pr + 2])
                state[cols, :] = sp * el + _mm_tn(xw, bm)
        finish()

    sg = SCAN_GROUPS
    return pl.pallas_call(
        body, name=name, grid=(SCAN_STEPS, nc),
        in_specs=[pl.BlockSpec((ln, sg * GROUP_COLS), lambda g, c: (c, g)),
                  pl.BlockSpec((ln, sg * SSD_STATE), lambda g, c: (c, B_COL0 // sg + g)),
                  pl.BlockSpec((ln, sg * SSD_STATE), lambda g, c: (c, C_COL0 // sg + g)),
                  pl.BlockSpec((sg, ln, HEADS_PER_GROUP), lambda g, c: (g, c, 0)),
                  pl.BlockSpec((sg, HEADS_PER_GROUP, ln), lambda g, c: (g, 0, c))] + c_in_specs,
        out_specs=[pl.BlockSpec((ln, sg * GROUP_COLS), lambda g, c: (c, g)),
                   pl.BlockSpec((1, sg, GROUP_COLS, SSD_STATE), lambda g, c: (c, g, 0, 0))] + c_out_specs,
        out_shape=[jax.ShapeDtypeStruct((t, SSD_D_INNER), F32),
                   jax.ShapeDtypeStruct((nc, SSD_GROUPS, GROUP_COLS, SSD_STATE), F32)] + c_out,
        scratch_shapes=[pltpu.VMEM((sg * GROUP_COLS, SSD_STATE), F32)] + (EXCHANGE_SCRATCH if carried else []),
        compiler_params=_cp("arbitrary", "arbitrary"),
    )(xdt, act, act, acum_g, acum_gt, *c_in)


def _ssd_scan_bwd(xdt, act, acum_g, acum_gt, states, dy, name, carried=None):
    t = xdt.shape[0]
    nc = t // SSD_CHUNK
    ln = SSD_CHUNK
    c_in_specs, c_in, c_out_specs, c_out = _carried_specs(carried)

    def body(*refs):
        if carried is None:
            x_ref, b_ref, c_ref, ac_ref, act_ref, sst_ref, dy_ref, dx_ref, db_ref, dc_ref, dacol_ref, darow_ref, dstate = refs
            comm_refs = None
        else:
            (x_ref, b_ref, c_ref, ac_ref, act_ref, sst_ref, dy_ref, src_ref, dx_ref, db_ref, dc_ref, dacol_ref, darow_ref,
             land_ref, dstate, send_sems, recv_sems, local_sem) = refs
            comm_refs = (src_ref, land_ref, send_sems, recv_sems, local_sem)
        finish = _carried_hooks(carried, comm_refs, _scan_step_is(0, 0), _scan_step_is(SCAN_STEPS - 1, (3 * nc) // 4),
                                _scan_step_is(SCAN_STEPS - 1, nc - 1))

        @pl.when(pl.program_id(1) == 0)
        def _():
            dstate[...] = jnp.zeros_like(dstate)

        for gg in range(SCAN_GROUPS):
            group_bwd(gg, x_ref, b_ref, c_ref, ac_ref, act_ref, sst_ref, dy_ref, dx_ref, db_ref, dc_ref, dacol_ref, darow_ref, dstate)
        finish()

    def group_bwd(gg, x_ref, b_ref, c_ref, ac_ref, act_ref, sst_ref, dy_ref, dx_ref, db_ref, dc_ref, dacol_ref, darow_ref, dstate):
        bc_cols = slice(gg * SSD_STATE, (gg + 1) * SSD_STATE)
        bm = b_ref[:, bc_cols].astype(MXU_DTYPE)
        cm = c_ref[:, bc_cols].astype(MXU_DTYPE)
        cb = _mm_nt(cm, bm)
        ac, act_ = ac_ref[gg], act_ref[gg]
        causal = _iota((ln, ln), 1) <= _iota((ln, ln), 0)
        lo_mask = _iota((ln, SSD_PAIR), 1) < SSD_HEAD_DIM
        lo_rows = _iota((SSD_PAIR, SSD_STATE), 0) < SSD_HEAD_DIM
        lane4 = _iota((ln, HEADS_PER_GROUP), 1)
        sub4 = _iota((HEADS_PER_GROUP, ln), 0)
        is_last = _iota((ln, 1), 0) == ln - 1
        e_last = jnp.exp(ac[ln - 1:ln, :])
        ac_rows = [jnp.broadcast_to(ac[:, h:h + 1], (ln, SSD_PAIR)) for h in range(HEADS_PER_GROUP)]
        dcb = jnp.zeros((ln, ln), F32)
        dc_acc = jnp.zeros((ln, SSD_STATE), F32)
        db_acc = jnp.zeros((ln, SSD_STATE), F32)
        dacol = jnp.zeros((ln, HEADS_PER_GROUP), F32)
        darow = jnp.zeros((HEADS_PER_GROUP, ln), F32)
        for pr in range(2):
            in_group = slice(pr * SSD_PAIR, (pr + 1) * SSD_PAIR)
            cols = slice(gg * GROUP_COLS + pr * SSD_PAIR, gg * GROUP_COLS + (pr + 1) * SSD_PAIR)
            xp = x_ref[:, cols]
            dyp = dy_ref[:, cols]
            sp = sst_ref[0, gg, in_group, :]
            dsp = dstate[cols, :]
            pair_ac = jnp.where(lo_mask, ac_rows[2 * pr], ac_rows[2 * pr + 1])
            ea = jnp.exp(pair_ac)
            w = jnp.exp(pair_ac[ln - 1:ln, :] - pair_ac)
            dye = (dyp * ea).astype(MXU_DTYPE)
            dx_state = w * _mm_nt(bm, dsp.astype(MXU_DTYPE))
            yoff = _mm_nt(cm, sp.astype(MXU_DTYPE)) * ea
            dxp = dx_state
            for hh in range(2):
                h = 2 * pr + hh
                mask = lo_mask if hh == 0 else jnp.logical_not(lo_mask)
                rmask = lo_rows if hh == 0 else jnp.logical_not(lo_rows)
                seg = ac_rows[h] - act_[h:h + 1, :]
                dec = jnp.exp(jnp.where(causal, seg, -1e30))
                m = cb * dec
                dym = jnp.where(mask, dyp, 0.0).astype(MXU_DTYPE)
                xm = jnp.where(mask, xp, 0.0).astype(MXU_DTYPE)
                g = _mm_nt(dym, xm)
                dxp = dxp + _mm_tn(m.astype(MXU_DTYPE), dym)
                dcb = dcb + dec * g
                mg = m * g
                rs = jnp.sum(mg, axis=1, keepdims=True)
                cs = jnp.sum(mg, axis=0, keepdims=True)
                t_off = jnp.sum(jnp.where(mask, dyp * yoff, 0.0), axis=1, keepdims=True)
                q = jnp.sum(jnp.where(mask, xp * dx_state, 0.0), axis=1, keepdims=True)
                qsum = jnp.sum(q, axis=0, keepdims=True)
                ds_s = jnp.sum(jnp.sum(jnp.where(rmask, dsp * sp, 0.0), axis=1, keepdims=True), axis=0, keepdims=True)
                extra = qsum + e_last[:, h:h + 1] * ds_s
                col = rs + t_off - q + jnp.where(is_last, extra, 0.0)
                dacol = jnp.where(lane4 == h, col, dacol)
                darow = jnp.where(sub4 == h, -cs, darow)
            dx_ref[:, cols] = dxp
            dc_acc = dc_acc + _mm(dye, sp.astype(MXU_DTYPE))
            db_acc = db_acc + _mm((xp * w).astype(MXU_DTYPE), dsp.astype(MXU_DTYPE))
            el = jnp.where(lo_rows, e_last[:, 2 * pr:2 * pr + 1], e_last[:, 2 * pr + 1:2 * pr + 2])
            dstate[cols, :] = dsp * el + _mm_tn(dye, cm)
        dcbm = dcb.astype(MXU_DTYPE)
        dc_ref[:, bc_cols] = _mm(dcbm, bm) + dc_acc
        db_ref[:, bc_cols] = _mm_tn(dcbm, cm) + db_acc
        dacol_ref[gg] = dacol
        darow_ref[gg] = darow

    def rev(c):
        return nc - 1 - c

    sg = SCAN_GROUPS
    grp = pl.BlockSpec((ln, sg * GROUP_COLS), lambda g, c: (rev(c), g))
    return pl.pallas_call(
        body, name=name, grid=(SCAN_STEPS, nc),
        in_specs=[grp,
                  pl.BlockSpec((ln, sg * SSD_STATE), lambda g, c: (rev(c), B_COL0 // sg + g)),
                  pl.BlockSpec((ln, sg * SSD_STATE), lambda g, c: (rev(c), C_COL0 // sg + g)),
                  pl.BlockSpec((sg, ln, HEADS_PER_GROUP), lambda g, c: (g, rev(c), 0)),
                  pl.BlockSpec((sg, HEADS_PER_GROUP, ln), lambda g, c: (g, 0, rev(c))),
                  pl.BlockSpec((1, sg, GROUP_COLS, SSD_STATE), lambda g, c: (rev(c), g, 0, 0)),
                  grp] + c_in_specs,
        out_specs=[grp,
                   pl.BlockSpec((ln, sg * SSD_STATE), lambda g, c: (rev(c), g)),
                   pl.BlockSpec((ln, sg * SSD_STATE), lambda g, c: (rev(c), g)),
                   pl.BlockSpec((sg, ln, HEADS_PER_GROUP), lambda g, c: (g, rev(c), 0)),
                   pl.BlockSpec((sg, HEADS_PER_GROUP, ln), lambda g, c: (g, 0, rev(c)))] + c_out_specs,
        out_shape=[jax.ShapeDtypeStruct((t, SSD_D_INNER), F32),
                   jax.ShapeDtypeStruct((t, SSD_GROUPS * SSD_STATE), F32),
                   jax.ShapeDtypeStruct((t, SSD_GROUPS * SSD_STATE), F32),
                   jax.ShapeDtypeStruct((SSD_GROUPS, t, HEADS_PER_GROUP), F32),
                   jax.ShapeDtypeStruct((SSD_GROUPS, HEADS_PER_GROUP, t), F32)] + c_out,
        scratch_shapes=[pltpu.VMEM((sg * GROUP_COLS, SSD_STATE), F32)] + (EXCHANGE_SCRATCH if carried else []),
        compiler_params=_cp("arbitrary", "arbitrary"),
    )(xdt, act, act, acum_g, acum_gt, states, dy, *c_in)


GN_ROWS = 256


def _gated_norm_parts(y_ref, xs_ref, z_ref, dsk_ref):
    yy = y_ref[...] + dsk_ref[...] * xs_ref[...]
    z = z_ref[...]
    sz = _sigmoid(z)
    silu = z * sz
    u = yy * silu
    r = lax.rsqrt(jnp.mean(u * u, axis=-1, keepdims=True) + NORM_EPS)
    return yy, z, sz, silu, u, r


def _gated_norm_fwd(y, act, proj, d_exp, g, name):
    t = y.shape[0]
    tm = min(GN_ROWS, t)

    def body(y_ref, xs_ref, z_ref, dsk_ref, g_ref, o_ref):
        _, _, _, _, u, r = _gated_norm_parts(y_ref, xs_ref, z_ref, dsk_ref)
        o_ref[...] = (u * r * g_ref[...]).astype(o_ref.dtype)

    wide = pl.BlockSpec((tm, SSD_D_INNER), lambda i: (i, 0))
    wvec = pl.BlockSpec((1, SSD_D_INNER), lambda i: (0, 0))
    return pl.pallas_call(
        body, name=name, grid=(t // tm,), in_specs=[wide, wide, wide, wvec, wvec], out_specs=wide,
        out_shape=jax.ShapeDtypeStruct((t, SSD_D_INNER), MXU_DTYPE), compiler_params=_cp("parallel"),
    )(y, act, proj, d_exp, g.reshape(1, -1))


def _gated_norm_bwd(y, act, proj, d_exp, g, dn, name):
    t = y.shape[0]
    tm = min(GN_ROWS, t)

    def body(y_ref, xs_ref, z_ref, dsk_ref, g_ref, dn_ref, dyy_ref, dz_ref, dg_ref):
        @pl.when(pl.program_id(0) == 0)
        def _():
            dg_ref[...] = jnp.zeros_like(dg_ref)

        yy, z, sz, silu, u, r = _gated_norm_parts(y_ref, xs_ref, z_ref, dsk_ref)
        un = u * r
        dn = dn_ref[...]
        v = dn * g_ref[...]
        du = r * (v - un * jnp.mean(v * un, axis=-1, keepdims=True))
        dg_ref[...] += jnp.sum(dn * un, axis=0, keepdims=True)
        dyy_ref[...] = du * silu
        dz_ref[...] = (du * yy * (sz * (1.0 + z * (1.0 - sz)))).astype(dz_ref.dtype)

    wide = pl.BlockSpec((tm, SSD_D_INNER), lambda i: (i, 0))
    wvec = pl.BlockSpec((1, SSD_D_INNER), lambda i: (0, 0))
    return pl.pallas_call(
        body, name=name, grid=(t // tm,), in_specs=[wide, wide, wide, wvec, wvec, wide], out_specs=[wide, wide, wvec],
        out_shape=[jax.ShapeDtypeStruct((t, SSD_D_INNER), F32), jax.ShapeDtypeStruct((t, SSD_D_INNER), MXU_DTYPE),
                   jax.ShapeDtypeStruct((1, SSD_D_INNER), F32)],
        compiler_params=_cp("arbitrary"),
    )(y, act, proj, d_exp, g.reshape(1, -1), dn)


SB_PAIRS = SB_HEADS // 2


def _kv_rows(j, bt, nt=1):
    return pl.ds(pl.multiple_of(j * bt, bt), nt * bt)


def _sb_tile_masks(bt):
    lane = _iota((bt, bt), 1)
    rowi = _iota((bt, bt), 0)
    return lane < rowi, (rowi >= lane).astype(MXU_DTYPE), (rowi <= lane).astype(MXU_DTYPE)


def _sb_scaled_heads(pair, scale):
    lane = _iota(pair.shape, 1)
    val = pair.astype(F32) * scale
    return [jnp.where(lane < SB_HEAD_DIM, val, 0.0).astype(pair.dtype), jnp.where(lane >= SB_HEAD_DIM, val, 0.0).astype(pair.dtype)]


def _sb_logits(qs, kb, bt, strict):
    nt = kb.shape[0] // bt
    full = [_mm_nt(q_head, kb) for q_head in qs]
    xs, nlfs = [], []
    for x in full:
        nlf = jnp.maximum(x, 0.0) + jnp.log(1.0 + jnp.exp(-jnp.abs(x)))
        xs.append([x[:, tt * bt:(tt + 1) * bt] for tt in range(nt)])
        tiles = [nlf[:, tt * bt:(tt + 1) * bt] for tt in range(nt)]
        if strict is not None:
            tiles[-1] = jnp.where(strict, tiles[-1], 0.0)
        nlfs.append(tiles)
    return xs, nlfs


def _sb_tails(nlf_tiles, from_j):
    tails, run = [None] * len(nlf_tiles), None
    for tt in reversed(range(len(nlf_tiles))):
        tail = _mm_exact_rhs(nlf_tiles[tt], from_j)
        tails[tt] = tail if run is None else tail + run
        run = tails[tt][:, 0:1]
    return tails


def _sb_heads(e_tiles, upto_j, pre):
    sums, run = [], pre
    for e in e_tiles:
        sums.append(_mm_exact_rhs(e, upto_j) + run)
        run = sums[-1][:, e.shape[1] - 1:e.shape[1]]
    return sums


def _carried_specs(carried):
    if carried is None:
        return [], [], [], []
    src, per_peer = carried
    rows = src.shape[1:] if per_peer else src.shape
    anywhere = pl.BlockSpec(memory_space=pl.ANY)
    return [anywhere], [src], [anywhere], [jax.ShapeDtypeStruct((N_DEV, *rows), src.dtype)]


def _carried_hooks(carried, comm_refs, first, pass_on, last):
    if carried is None:
        return lambda: None
    per_peer = carried[1]

    @pl.when(first)
    def _():
        if per_peer:
            _exchange_start(*comm_refs, per_peer=True)
        else:
            _gather_phase("send", *comm_refs)

    if not per_peer:
        @pl.when(pass_on)
        def _():
            _gather_phase("pass_on", *comm_refs)

    def finish():
        @pl.when(last)
        def _():
            if per_peer:
                _exchange_finish(*comm_refs, per_peer=True)
            else:
                _gather_phase("finish", *comm_refs)

    return finish


def _sb_attention_fwd(qkv, name, carried=None):
    t = qkv.shape[0]
    bt = min(SB_TILE, t)
    nq = t // bt
    c_in_specs, c_in, c_out_specs, c_out = _carried_specs(carried)

    def body(*refs):
        if carried is None:
            q_ref, k_ref, v_ref, o_ref, acc_ref = refs
            comm_refs = None
        else:
            q_ref, k_ref, v_ref, src_ref, o_ref, land_ref, acc_ref, send_sems, recv_sems, local_sem = refs
            comm_refs = (src_ref, land_ref, send_sems, recv_sems, local_sem)
        i = pl.program_id(1)
        last_pair = pl.program_id(0) == SB_PAIRS - 1
        finish = _carried_hooks(carried, comm_refs, (pl.program_id(0) == 0) & (i == 0), last_pair & (i == 0),
                                last_pair & (i == nq - 1))
        strict, from_j, _ = _sb_tile_masks(bt)
        qs = _sb_scaled_heads(q_ref[...], SB_SCALE)
        acc_ref[...] = jnp.zeros_like(acc_ref)

        def block(j, nt, carries, diag):
            rows = _kv_rows(j, bt, nt)
            kb, vb = k_ref[rows, :], v_ref[rows, :]
            xs, nlfs = _sb_logits(qs, kb, bt, strict if diag else None)
            tails = [_sb_tails(nlfs[hh], from_j) for hh in range(2)]
            for hh in range(2):
                ws = [jnp.exp(xs[hh][tt] - tails[hh][tt] - carries[hh]) for tt in range(nt)]
                if diag:
                    ws[-1] = jnp.where(strict, ws[-1], 0.0)
                acc_ref[hh] += _mm(jnp.concatenate([w.astype(MXU_DTYPE) for w in ws], axis=1), vb)
            return tuple(carries[hh] + tails[hh][0][:, 0:1] for hh in range(2))

        zero = jnp.zeros((bt, 1), F32)
        carries = block(i, 1, (zero, zero), True)
        carries = lax.fori_loop(0, i // 2, lambda it, cr: block(i - 2 - 2 * it, 2, cr, False), carries)

        @pl.when(i % 2 == 1)
        def _():
            block(0, 1, carries, False)

        low = _iota((bt, 2 * SB_HEAD_DIM), 1) < SB_HEAD_DIM
        o_ref[...] = jnp.where(low, acc_ref[0], acc_ref[1]).astype(o_ref.dtype)
        finish()

    lanes = 2 * SB_HEAD_DIM
    res = pl.pallas_call(
        body, name=name, grid=(SB_PAIRS, nq),
        in_specs=[pl.BlockSpec((bt, lanes), lambda p, i: (i, p)),
                  pl.BlockSpec((t, lanes), lambda p, i: (0, SB_PAIRS + p)),
                  pl.BlockSpec((t, lanes), lambda p, i: (0, 2 * SB_PAIRS + p))] + c_in_specs,
        out_specs=[pl.BlockSpec((bt, lanes), lambda p, i: (i, p))] + c_out_specs,
        out_shape=[jax.ShapeDtypeStruct((t, D_MODEL), MXU_DTYPE)] + c_out,
        scratch_shapes=[pltpu.VMEM((2, bt, lanes), F32)] + (EXCHANGE_SCRATCH if carried else []),
        compiler_params=_cp("arbitrary", "arbitrary"),
    )(qkv, qkv, qkv, *c_in)
    return res[0] if carried is None else res


def _sb_attention_bwd(qkv, do, name, carried=None):
    t = qkv.shape[0]
    bt = min(SB_TILE, t)
    nq = t // bt
    lanes = 2 * SB_HEAD_DIM
    c_in_specs, c_in, c_out_specs, c_out = _carried_specs(carried)

    def body(*refs):
        if carried is None:
            q_ref, k_ref, v_ref, do_ref, dq_ref, dk_ref, dv_ref, sbuf, ebuf, dq_acc, dk_acc, dv_acc = refs
            comm_refs = None
        else:
            (q_ref, k_ref, v_ref, do_ref, src_ref, dq_ref, dk_ref, dv_ref, land_ref,
             sbuf, ebuf, dq_acc, dk_acc, dv_acc, send_sems, recv_sems, local_sem) = refs
            comm_refs = (src_ref, land_ref, send_sems, recv_sems, local_sem)
        i = pl.program_id(1)
        last_pair = pl.program_id(0) == SB_PAIRS - 1
        finish = _carried_hooks(carried, comm_refs, (pl.program_id(0) == 0) & (i == 0), last_pair & (i == 0),
                                last_pair & (i == nq - 1))

        @pl.when(i == 0)
        def _():
            dk_acc[...] = jnp.zeros_like(dk_acc)
            dv_acc[...] = jnp.zeros_like(dv_acc)

        strict, from_j, upto_j = _sb_tile_masks(bt)
        qs = _sb_scaled_heads(q_ref[...], SB_SCALE)
        dos = _sb_scaled_heads(do_ref[...], 1.0)
        q_both = jnp.concatenate(qs, axis=0)
        do_both = jnp.concatenate(dos, axis=0)
        dq_acc[...] = jnp.zeros_like(dq_acc)

        def pass1(j, nt, carries, diag):
            rows = _kv_rows(j, bt, nt)
            kb, vb = k_ref[rows, :], v_ref[rows, :]
            xs, nlfs = _sb_logits(qs, kb, bt, strict if diag else None)
            dws = [_mm_nt(dos[hh], vb) for hh in range(2)]
            tails = [_sb_tails(nlfs[hh], from_j) for hh in range(2)]
            wcat = []
            for hh in range(2):
                ws = [jnp.exp(xs[hh][tt] - tails[hh][tt] - carries[hh]) for tt in range(nt)]
                if diag:
                    ws[-1] = jnp.where(strict, ws[-1], 0.0)
                w_all = jnp.concatenate(ws, axis=1)
                sbuf[hh, :, rows] = jnp.exp(jnp.concatenate([xs[hh][tt] - nlfs[hh][tt] for tt in range(nt)], axis=1))
                ebuf[hh, :, rows] = w_all * dws[hh]
                wcat.append(w_all.astype(MXU_DTYPE))
            dv_acc[rows, :] += _mm_tn(jnp.concatenate(wcat, axis=0), do_both)
            return tuple(carries[hh] + tails[hh][0][:, 0:1] for hh in range(2))

        zero = jnp.zeros((bt, 1), F32)
        carries = pass1(i, 1, (zero, zero), True)
        carries = lax.fori_loop(0, i // 2, lambda it, cr: pass1(i - 2 - 2 * it, 2, cr, False), carries)

        @pl.when(i % 2 == 1)
        def _():
            pass1(0, 1, carries, False)

        def pass2(j, nt, pres, diag):
            rows = _kv_rows(j, bt, nt)
            kb = k_ref[rows, :]
            sums = [_sb_heads([ebuf[hh, :, _kv_rows(j + tt, bt)] for tt in range(nt)], upto_j, pres[hh]) for hh in range(2)]
            dxm = []
            for hh in range(2):
                dxs = [ebuf[hh, :, _kv_rows(j + tt, bt)] - sbuf[hh, :, _kv_rows(j + tt, bt)] * sums[hh][tt] for tt in range(nt)]
                if diag:
                    dxs[-1] = jnp.where(strict, dxs[-1], 0.0)
                dxm.append(jnp.concatenate(dxs, axis=1).astype(MXU_DTYPE))
                dq_acc[hh] += _mm(dxm[hh], kb)
            dk_acc[rows, :] += _mm_tn(jnp.concatenate(dxm, axis=0), q_both)
            return tuple(sums[hh][-1][:, bt - 1:bt] for hh in range(2))

        pres = lax.fori_loop(0, i // 2, lambda it, pr: pass2(2 * it, 2, pr, False), (zero, zero))

        @pl.when(i % 2 == 0)
        def _():
            pass2(i, 1, pres, True)

        @pl.when(i % 2 == 1)
        def _():
            pass2(i - 1, 2, pres, True)

        low = _iota((bt, lanes), 1) < SB_HEAD_DIM
        dq_ref[...] = (jnp.where(low, dq_acc[0], dq_acc[1]) * SB_SCALE).astype(dq_ref.dtype)

        @pl.when(i == nq - 1)
        def _():
            dk_ref[...] = dk_acc[...].astype(dk_ref.dtype)
            dv_ref[...] = dv_acc[...].astype(dv_ref.dtype)

        finish()

    blk = pl.BlockSpec((bt, lanes), lambda p, i: (i, p))
    whole = pl.BlockSpec((t, lanes), lambda p, i: (0, p))
    out = jax.ShapeDtypeStruct((t, D_MODEL), MXU_DTYPE)
    return pl.pallas_call(
        body, name=name, grid=(SB_PAIRS, nq),
        in_specs=[blk, pl.BlockSpec((t, lanes), lambda p, i: (0, SB_PAIRS + p)),
                  pl.BlockSpec((t, lanes), lambda p, i: (0, 2 * SB_PAIRS + p)), blk] + c_in_specs,
        out_specs=[blk, whole, whole] + c_out_specs, out_shape=[out, out, out] + c_out,
        scratch_shapes=[pltpu.VMEM((2, bt, t), F32), pltpu.VMEM((2, bt, t), F32), pltpu.VMEM((2, bt, lanes), F32),
                        pltpu.VMEM((t, lanes), F32), pltpu.VMEM((t, lanes), F32)] + (EXCHANGE_SCRATCH if carried else []),
        compiler_params=_cp("arbitrary", "arbitrary"),
    )(qkv, qkv, qkv, do, *c_in)


def _add_pair(a, b, name):
    s, r, c = a.shape
    tm = _row_tile(r)

    def body(a_ref, b_ref, o_ref):
        o_ref[...] = (a_ref[...].astype(F32) + b_ref[...].astype(F32)).astype(o_ref.dtype)

    blk = pl.BlockSpec((1, tm, c), lambda q, i: (q, i, 0))
    return pl.pallas_call(body, name=name, grid=(s, r // tm), in_specs=[blk, blk], out_specs=blk,
                          out_shape=jax.ShapeDtypeStruct(a.shape, a.dtype), compiler_params=_cp("parallel", "parallel"))(a, b)


def _sum_slots(gslots, name):
    s, r, c = gslots.shape

    def body(g_ref, o_ref):
        g = g_ref[0].astype(F32)
        for q in range(1, s):
            g = g + g_ref[q].astype(F32)
        o_ref[...] = g

    return pl.pallas_call(
        body, name=name, grid=(c // LANES,),
        in_specs=[pl.BlockSpec((s, r, LANES), lambda j: (0, 0, j))], out_specs=pl.BlockSpec((r, LANES), lambda j: (0, j)),
        out_shape=jax.ShapeDtypeStruct((r, c), F32), compiler_params=_cp("parallel"),
    )(gslots)


def _adamw(gslots, w, m, v, name):
    s, r, c = gslots.shape
    tm = _row_tile(r)
    assert w.shape == (r, c), (w.shape, gslots.shape)
    c1 = 1.0 - ADAM_B1 ** ADAM_STEP
    c2 = 1.0 - ADAM_B2 ** ADAM_STEP

    def body(g_ref, w_ref, m_ref, v_ref, go_ref, d_ref, mo_ref, vo_ref):
        g = g_ref[0].astype(F32)
        for q in range(1, s):
            g = g + g_ref[q].astype(F32)
        mn = ADAM_B1 * m_ref[...] + (1.0 - ADAM_B1) * g
        vn = ADAM_B2 * v_ref[...] + (1.0 - ADAM_B2) * (g * g)
        go_ref[...] = g
        mo_ref[...] = mn
        vo_ref[...] = vn
        d_ref[...] = -ADAM_LR * ((mn / c1) / (jnp.sqrt(vn / c2) + ADAM_EPS) + ADAM_WD * w_ref[...])

    row = pl.BlockSpec((tm, c), lambda i: (i, 0))
    out = jax.ShapeDtypeStruct((r, c), F32)
    return pl.pallas_call(
        body, name=name, grid=(r // tm,),
        in_specs=[pl.BlockSpec((s, tm, c), lambda i: (0, i, 0)), row, row, row],
        out_specs=[row, row, row, row], out_shape=[out, out, out, out], compiler_params=_cp("parallel"),
    )(gslots, w, m, v)


def _rows(a):
    flat = a.reshape(-1)
    pad = (-flat.shape[0]) % PACK_W
    if pad:
        flat = jnp.concatenate([flat, jnp.zeros((pad,), flat.dtype)])
    return flat.reshape(-1, PACK_W)


def _pack(arrays, row_multiple):
    parts, layout, off = [], [], 0
    for a in arrays:
        rw = _rows(a)
        parts.append(rw)
        layout.append((off, rw.shape[0], a.shape))
        off += rw.shape[0]
    pad = (-off) % row_multiple
    if pad:
        parts.append(jnp.zeros((pad, PACK_W), parts[0].dtype))
    return jnp.concatenate(parts, axis=0), layout


def _unpack(packed, layout):
    out = []
    for off, nrows, shape in layout:
        n = int(np.prod(shape))
        out.append(packed[off:off + nrows].reshape(-1)[:n].reshape(shape))
    return out


def _shard_as_rows(name, shard):
    if name in COL_SHARDED:
        shard = shard.transpose(0, 2, 1)
    return shard.reshape(-1, PACK_W)


def _rows_as_shard(name, rows, shape):
    if name in COL_SHARDED:
        lead, k, ns = shape
        return rows.reshape(lead, ns, k).transpose(0, 2, 1)
    return rows.reshape(shape)


def _row_tile(r):
    return next(tm for tm in (256, 128, 64, 32, 16, 8) if r % tm == 0)


def _ssd_consts(dt_bias, a_log, d_skip):
    pad = LANES - SSD_HEADS
    bias = jnp.pad(dt_bias, (0, pad)).reshape(1, LANES)
    a_neg = jnp.pad(-jnp.exp(a_log), (0, pad)).reshape(1, LANES)
    d_exp = jnp.repeat(d_skip, SSD_HEAD_DIM).reshape(1, SSD_D_INNER)
    return bias, a_neg, d_exp


def _group_layouts(acum):
    t = acum.shape[0]
    a = acum[:, :SSD_HEADS].reshape(t, SSD_GROUPS, HEADS_PER_GROUP)
    return a.transpose(1, 0, 2), a.transpose(1, 2, 0)


def _ssd_fwd(x, p, carried=None):
    hn = _rmsnorm(x, p["mix_norm"], "rmsnorm_fwd")
    proj = _matmul(hn, p["w_in"], "nt", F32, "ssd_in_fwd", tm=TOKEN_ROWS, tn=896, tk=1024)
    act = _ssd_conv_fwd(proj, p["conv_w"], p["conv_b"], "ssd_conv_fwd")
    bias, a_neg, d_exp = _ssd_consts(p["dt_bias"], p["a_log"], p["d"])
    expand = _head_expand()
    xdt, dt, acum = _ssd_dt_fwd(proj, act, bias, a_neg, expand, "ssd_dt_fwd")
    acum_g, acum_gt = _group_layouts(acum)
    if carried is None:
        (y, states), landed = _ssd_scan_fwd(xdt, act, acum_g, acum_gt, "ssd_scan_fwd"), None
    else:
        y, states, landed = _ssd_scan_fwd(xdt, act, acum_g, acum_gt, "ssd_scan_fwd_carrying_gather", carried)
    yn = _gated_norm_fwd(y, act, proj, d_exp, p["norm"], "ssd_gnorm_fwd")
    x_new = _matmul(yn, p["w_out"], "nn", F32, "ssd_out_fwd", add=x, tm=TOKEN_ROWS, tn=1024, tk=2048)
    saved = dict(x=x, hn=hn, proj=proj, act=act, xdt=xdt, dt=dt, acum_g=acum_g, acum_gt=acum_gt, y=y, states=states, yn=yn)
    return x_new, saved, landed


def _ssd_bwd(dx, p, s, carried_of=None):
    bias, a_neg, d_exp = _ssd_consts(p["dt_bias"], p["a_log"], p["d"])
    expand = _head_expand()
    dyn = _matmul(dx, p["w_out"], "nt", F32, "ssd_out_dgrad", tm=TOKEN_ROWS, tn=1024, tk=1024)
    g_w_out = _matmul(s["yn"], dx, "tn", MXU_DTYPE, "ssd_out_wgrad", tm=1024, tn=1024, tk=TOKEN_ROWS)
    dyy, dz, g_norm = _gated_norm_bwd(s["y"], s["act"], s["proj"], d_exp, p["norm"], dyn, "ssd_gnorm_bwd")
    scan_args = (s["xdt"], s["act"], s["acum_g"], s["acum_gt"], s["states"], dyy)
    if carried_of is None:
        (dxdt, dbm, dcm, dacol, darow), landed = _ssd_scan_bwd(*scan_args, "ssd_scan_bwd"), None
    else:
        dxdt, dbm, dcm, dacol, darow, landed = _ssd_scan_bwd(*scan_args, "ssd_scan_bwd_carrying_grads", carried_of(g_w_out))
    t = dx.shape[0]
    dacum = dacol.transpose(1, 0, 2).reshape(t, SSD_HEADS) + darow.transpose(2, 0, 1).reshape(t, SSD_HEADS)
    dacum = jnp.pad(dacum, ((0, 0), (0, LANES - SSD_HEADS)))
    dxs, draw, g_a, g_bias, g_dexp = _ssd_dt_bwd(s["proj"], s["act"], s["dt"], dxdt, dyy, dacum, bias, a_neg, d_exp,
                                                  expand, expand.T, "ssd_dt_bwd")
    dact = jnp.concatenate([dxs, dbm, dcm], axis=1)
    dxbc, g_conv_w, g_conv_b = _ssd_conv_bwd(s["proj"], p["conv_w"], p["conv_b"], dact, "ssd_conv_bwd")
    dproj = jnp.concatenate([dz, dxbc, draw], axis=1)
    dhn = _matmul(dproj, p["w_in"], "nn", F32, "ssd_in_dgrad", tm=TOKEN_ROWS, tn=1024, tk=896)
    g_w_in = _matmul(dproj, s["hn"], "tn", MXU_DTYPE, "ssd_in_wgrad", tm=896, tn=1024, tk=TOKEN_ROWS)
    dx_new, g_mix = _rmsnorm_bwd(s["x"], p["mix_norm"], dhn, dx, "rmsnorm_bwd")
    grads = dict(w_in=g_w_in[:SSD_IN_DIM], w_out=g_w_out, conv_w=g_conv_w, conv_b=g_conv_b.reshape(-1),
                 dt_bias=g_bias[0, :SSD_HEADS], a_log=(g_a * a_neg)[0, :SSD_HEADS],
                 d=g_dexp.reshape(SSD_HEADS, SSD_HEAD_DIM).sum(axis=1), norm=g_norm.reshape(-1), mix_norm=g_mix.reshape(-1))
    return dx_new, grads, landed


def _sb_fwd(x, p, carried=None):
    hn = _rmsnorm(x, p["mix_norm"], "rmsnorm_fwd")
    qkv = _matmul(hn, p["w_qkv"], "nt", MXU_DTYPE, "sb_qkv_fwd", tm=TOKEN_ROWS, tn=1024, tk=1024)
    if carried is None:
        o, landed = _sb_attention_fwd(qkv, "sb_attn_fwd"), None
    else:
        o, landed = _sb_attention_fwd(qkv, "sb_attn_fwd_carrying_gather", carried)
    x_new = _matmul(o, p["w_out"], "nn", F32, "sb_out_fwd", add=x, tm=TOKEN_ROWS, tn=1024, tk=1024)
    return x_new, dict(x=x, hn=hn, qkv=qkv, o=o), landed


def _sb_bwd(dx, p, s, carried_of=None):
    do = _matmul(dx, p["w_out"], "nt", MXU_DTYPE, "sb_out_dgrad", tm=TOKEN_ROWS, tn=1024, tk=1024)
    g_w_out = _matmul(s["o"], dx, "tn", MXU_DTYPE, "sb_out_wgrad", tm=1024, tn=1024, tk=TOKEN_ROWS)
    if carried_of is None:
        (dq, dk, dv), landed = _sb_attention_bwd(s["qkv"], do, "sb_attn_bwd"), None
    else:
        dq, dk, dv, landed = _sb_attention_bwd(s["qkv"], do, "sb_attn_bwd_carrying_grads", carried_of(g_w_out))
    dqkv = jnp.concatenate([dq, dk, dv], axis=1)
    dhn = _matmul(dqkv, p["w_qkv"], "nn", F32, "sb_qkv_dgrad", tm=TOKEN_ROWS, tn=1024, tk=1024)
    g_w_qkv = _matmul(dqkv, s["hn"], "tn", MXU_DTYPE, "sb_qkv_wgrad", tm=1024, tn=1024, tk=TOKEN_ROWS)
    dx_new, g_mix = _rmsnorm_bwd(s["x"], p["mix_norm"], dhn, dx, "rmsnorm_bwd")
    return dx_new, dict(w_qkv=g_w_qkv, w_out=g_w_out, mix_norm=g_mix.reshape(-1)), landed


def _ffn_fwd(x, p):
    hn = _rmsnorm(x, p["ffn_norm"], "rmsnorm_fwd")
    proj = _matmul(hn, p["w_in"], "nt", F32, "ffn_in_fwd", tm=TOKEN_ROWS, tn=1408, tk=1024)
    act = _ffn_conv_fwd(proj, p["conv_w"], p["conv_b"], "ffn_conv_fwd")
    x_new = _matmul(act, p["w_out"], "nn", F32, "ffn_out_fwd", add=x, tm=TOKEN_ROWS, tn=1024, tk=1408)
    return x_new, dict(x=x, hn=hn, proj=proj, act=act)


def _ffn_bwd(dx, p, s):
    dact = _matmul(dx, p["w_out"], "nt", F32, "ffn_out_dgrad", tm=TOKEN_ROWS, tn=1408, tk=1024)
    g_w_out = _matmul(s["act"], dx, "tn", MXU_DTYPE, "ffn_out_wgrad", tm=1408, tn=1024, tk=TOKEN_ROWS)
    dpg, dpu, dwg, dwu, dbg, dbu = _ffn_conv_bwd(s["proj"], p["conv_w"], p["conv_b"], dact, "ffn_conv_bwd")
    dproj = jnp.concatenate([dpg, dpu], axis=1)
    dhn = _matmul(dproj, p["w_in"], "nn", F32, "ffn_in_dgrad", tm=TOKEN_ROWS, tn=1024, tk=1408)
    g_w_in = _matmul(dproj, s["hn"], "tn", MXU_DTYPE, "ffn_in_wgrad", tm=1408, tn=1024, tk=TOKEN_ROWS)
    dx_new, g_norm = _rmsnorm_bwd(s["x"], p["ffn_norm"], dhn, dx, "rmsnorm_bwd")
    grads = dict(w_in=g_w_in, w_out=g_w_out, conv_w=jnp.concatenate([dwg, dwu], axis=1),
                 conv_b=jnp.concatenate([dbg, dbu], axis=1).reshape(-1), ffn_norm=g_norm.reshape(-1))
    return dx_new, grads


ADD_ROWS = 256
BIG = ["ssd_w_in", "sb_w_qkv", "ffn_w_in", "ssd_w_out", "sb_w_out", "ffn_w_out"]
LAYER_PIECES = [[("ssd_w_in", 0), ("ssd_w_out", 0), ("ffn_w_in", 0), ("ffn_w_out", 0)],
                [("sb_w_qkv", 0), ("sb_w_out", 0), ("ffn_w_in", 1), ("ffn_w_out", 1)],
                [("ssd_w_in", 1), ("ssd_w_out", 1), ("ffn_w_in", 2), ("ffn_w_out", 2)],
                [("sb_w_qkv", 1), ("sb_w_out", 1), ("ffn_w_in", 3), ("ffn_w_out", 3)]]
GRAD_SETS = {3: [("ffn_w_in", 3), ("ffn_w_out", 3), ("sb_w_out", 1)],
             1: [("sb_w_qkv", 1), ("ssd_w_out", 1), ("ffn_w_in", 2), ("ffn_w_out", 2), ("ffn_w_in", 1), ("ffn_w_out", 1),
                 ("sb_w_out", 0), ("ssd_w_in", 1)],
             0: [("sb_w_qkv", 0), ("ffn_w_in", 0), ("ffn_w_out", 0), ("ssd_w_out", 0)],
             "end": [("ssd_w_in", 0)]}
GATHER_SETS = {"early": [("ssd_w_out", 0), ("ssd_w_in", 0)],
               0: [("ffn_w_in", 0), ("ffn_w_out", 0), ("sb_w_qkv", 0), ("sb_w_out", 0)],
               1: [("ffn_w_in", 1), ("ffn_w_out", 1), ("ssd_w_out", 1), ("ffn_w_in", 2), ("ffn_w_out", 2)] + LAYER_PIECES[3]
                  + [("ssd_w_in", 1)]}
COL_SHARDED = {"ssd_w_in": 2, "sb_w_qkv": 2, "ffn_w_in": 4}
CONV = ["ssd_conv_w", "ffn_conv_w"]
SMALL = ["mix_norm", "ffn_norm", "final_norm", "ssd_conv_b", "ssd_dt_bias", "ssd_a_log", "ssd_d", "ssd_norm", "ffn_conv_b"]
WEIGHTS = ["mix_norm", "ffn_norm", "final_norm", "ssd_w_in", "ssd_conv_w", "ssd_conv_b", "ssd_dt_bias", "ssd_a_log", "ssd_d",
           "ssd_norm", "ssd_w_out", "sb_w_qkv", "sb_w_out", "ffn_w_in", "ffn_conv_w", "ffn_conv_b", "ffn_w_out"]


def _step(x, loss_target, w, m, v):
    x = x.reshape(x.shape[-2], x.shape[-1])
    target = loss_target.reshape(x.shape)
    dev = 4 * lax.axis_index("x") + 2 * lax.axis_index("y") + lax.axis_index("c")
    core = lax.axis_index("c")

    shard_rows = {n: _shard_as_rows(n, w[n].astype(MXU_DTYPE)) for n in BIG}
    per_shard = {n: shard_rows[n].shape[0] // w[n].shape[0] for n in BIG}

    def layout(pieces):
        where, off = {}, 0
        for n, l in pieces:
            where[(n, l)] = (off, per_shard[n])
            off += per_shard[n]
        return where

    def pack_pieces(pieces, rows_of):
        return jnp.concatenate([rows_of(piece) for piece in pieces], axis=-2)

    def shard_piece(piece):
        n, l = piece
        return shard_rows[n][l * per_shard[n]:(l + 1) * per_shard[n]]

    full = {}

    def unpack_weights(gathered, where):
        for (n, l), (off, rows) in where.items():
            mat = gathered[:, off:off + rows].reshape(N_DEV * rows, PACK_W)
            if n == "ssd_w_in":
                mat = jnp.pad(mat, ((0, SSD_IN_PAD - SSD_IN_DIM), (0, 0)))
            full[(n, l)] = mat

    unpack_weights(_all_gather(pack_pieces(GATHER_SETS["early"], shard_piece), "gather_weights_early"), layout(GATHER_SETS["early"]))
    conv_pack, conv_layout = _pack([w[n] for n in CONV], 8)
    conv_all = _all_gather(conv_pack, "gather_conv_taps")
    for n, (off, nrows, shape) in zip(CONV, conv_layout):
        parts = [_unpack(conv_all[j], conv_layout)[CONV.index(n)] for j in range(N_DEV)]
        full[n] = jnp.concatenate(parts, axis=-1)

    def ssd_params(j):
        return dict(mix_norm=w["mix_norm"][2 * j], w_in=full[("ssd_w_in", j)], conv_w=full["ssd_conv_w"][j],
                    conv_b=w["ssd_conv_b"][j], dt_bias=w["ssd_dt_bias"][j], a_log=w["ssd_a_log"][j], d=w["ssd_d"][j],
                    norm=w["ssd_norm"][j], w_out=full[("ssd_w_out", j)])

    def sb_params(j):
        return dict(mix_norm=w["mix_norm"][2 * j + 1], w_qkv=full[("sb_w_qkv", j)], w_out=full[("sb_w_out", j)])

    def ffn_params(i):
        return dict(ffn_norm=w["ffn_norm"][i], w_in=full[("ffn_w_in", i)], conv_w=full["ffn_conv_w"][i],
                    conv_b=w["ffn_conv_b"][i], w_out=full[("ffn_w_out", i)])

    saved = []
    for i in range(DEPTH):
        mixer_fwd, params = (_ssd_fwd, ssd_params) if i % 2 == 0 else (_sb_fwd, sb_params)
        if i in GATHER_SETS:
            x, s_mix, arrived = mixer_fwd(x, params(i // 2), carried=(pack_pieces(GATHER_SETS[i], shard_piece), False))
            unpack_weights(arrived, layout(GATHER_SETS[i]))
        else:
            x, s_mix, _ = mixer_fwd(x, params(i // 2))
        x, s_ffn = _ffn_fwd(x, ffn_params(i))
        saved.append((s_mix, s_ffn))
    dx, g_final, loss_part = _final_norm_loss(x, w["final_norm"], target, "final_norm_loss")

    piece_grad = {}

    def grad_piece(piece):
        g = piece_grad[piece]
        return g.reshape(N_DEV, g.shape[0] // N_DEV, PACK_W)

    def carried_set(pieces, own_piece):
        def make(g_w_out):
            piece_grad[own_piece] = g_w_out
            return pack_pieces(pieces, grad_piece), True
        return make

    g_mix, g_ffn, g_ssd, g_sb = [None] * DEPTH, [None] * DEPTH, [None] * 2, [None] * 2
    landed = {}
    for i in reversed(range(DEPTH)):
        s_mix, s_ffn = saved[i]
        dx, g_ffn[i] = _ffn_bwd(dx, ffn_params(i), s_ffn)
        piece_grad[("ffn_w_in", i)], piece_grad[("ffn_w_out", i)] = g_ffn[i]["w_in"], g_ffn[i]["w_out"]
        j = i // 2
        if i % 2 == 0:
            carried_of = carried_set(GRAD_SETS[i], ("ssd_w_out", j)) if i in GRAD_SETS else None
            dx, g_ssd[j], landed[i] = _ssd_bwd(dx, ssd_params(j), s_mix, carried_of)
            g_mix[i] = g_ssd[j]["mix_norm"]
            piece_grad[("ssd_w_in", j)], piece_grad[("ssd_w_out", j)] = g_ssd[j]["w_in"], g_ssd[j]["w_out"]
        else:
            dx, g_sb[j], landed[i] = _sb_bwd(dx, sb_params(j), s_mix, carried_set(GRAD_SETS[i], ("sb_w_out", j)))
            g_mix[i] = g_sb[j]["mix_norm"]
            piece_grad[("sb_w_qkv", j)] = g_sb[j]["w_qkv"]
    grad_x = dx.reshape(1, *dx.shape)

    g8 = pack_pieces(GRAD_SETS["end"], grad_piece)
    g8 = jnp.pad(g8, ((0, 0), (0, (-g8.shape[1]) % ADD_ROWS), (0, 0)))
    g8 = g8.reshape(4, 2, *g8.shape[1:])
    keep = lax.dynamic_index_in_dim(g8, core, axis=1, keepdims=False)
    give = lax.dynamic_index_in_dim(g8, 1 - core, axis=1, keepdims=False)
    got = _swap_with_sibling(give, "grads_to_sibling")
    chip_part = _add_pair(keep, got, "grads_add_sibling")
    landed["end"] = _exchange_chips(chip_part, "grads_across_chips")

    summed = {}
    for key, pieces in GRAD_SETS.items():
        total = _sum_slots(landed[key], "grads_sum_landed")
        for piece, (off, rows) in layout(pieces).items():
            summed[piece] = total[off:off + rows]
    big_res = [dict() for _ in range(4)]
    for n in BIG:
        lead, rows, cols = w[n].shape
        g_rows = jnp.concatenate([summed[(n, l)] for l in range(lead)], axis=0)
        g_nat = _rows_as_shard(n, g_rows, w[n].shape).reshape(1, lead * rows, cols)
        two_d = (lead * rows, cols)
        outs = _adamw(g_nat, w[n].reshape(two_d), m[n].reshape(two_d), v[n].reshape(two_d), "adamw_" + n)
        for kind in range(4):
            big_res[kind][n] = outs[kind].reshape(w[n].shape)

    small_g = {
        "mix_norm": jnp.stack(g_mix), "ffn_norm": jnp.stack([g["ffn_norm"] for g in g_ffn]), "final_norm": g_final.reshape(-1),
        "ssd_conv_b": jnp.stack([g["conv_b"] for g in g_ssd]), "ssd_dt_bias": jnp.stack([g["dt_bias"] for g in g_ssd]),
        "ssd_a_log": jnp.stack([g["a_log"] for g in g_ssd]), "ssd_d": jnp.stack([g["d"] for g in g_ssd]),
        "ssd_norm": jnp.stack([g["norm"] for g in g_ssd]), "ffn_conv_b": jnp.stack([g["conv_b"] for g in g_ffn]),
    }
    conv_g = {"ssd_conv_w": jnp.stack([g["conv_w"] for g in g_ssd]), "ffn_conv_w": jnp.stack([g["conv_w"] for g in g_ffn])}
    extra = [conv_g[n] for n in CONV] + [loss_part]
    small_pack, small_layout = _pack([small_g[n] for n in SMALL] + extra, 8)
    small_all = _all_gather(small_pack, "gather_small_grads")
    zeros_like = [jnp.zeros(a.shape, F32) for a in extra]
    sw, _ = _pack([w[n] for n in SMALL] + zeros_like, 8)
    sm, _ = _pack([m[n] for n in SMALL] + zeros_like, 8)
    sv, _ = _pack([v[n] for n in SMALL] + [jnp.ones(a.shape, F32) for a in extra], 8)
    small_out = _adamw(small_all, sw, sm, sv, "adamw_replicated")
    small_res = [_unpack(o, small_layout) for o in small_out]
    summed = small_res[0]
    loss = summed[-1][0, 0]
    conv_shard_g = []
    for n, gsum in zip(CONV, summed[len(SMALL):len(SMALL) + len(CONV)]):
        ns = w[n].shape[-1]
        conv_shard_g.append(lax.dynamic_slice_in_dim(gsum, dev * ns, ns, axis=2))
    cg, conv_sh_layout = _pack(conv_shard_g, 8)
    cw, _ = _pack([w[n] for n in CONV], 8)
    cm_, _ = _pack([m[n] for n in CONV], 8)
    cv, _ = _pack([v[n] for n in CONV], 8)
    conv_out = _adamw(cg.reshape(1, *cg.shape), cw, cm_, cv, "adamw_conv_taps")
    conv_res = [dict(zip(CONV, _unpack(o, conv_sh_layout))) for o in conv_out]

    def pick(kind, n):
        if n in BIG:
            return big_res[kind][n]
        if n in CONV:
            return conv_res[kind][n]
        return small_res[kind][SMALL.index(n)]

    outs = [loss, grad_x]
    for kind in range(4):
        outs += [pick(kind, n) for n in WEIGHTS]
    return tuple(outs)


def kernel(x, mix_norm, ffn_norm, final_norm, ssd_w_in, ssd_conv_w, ssd_conv_b, ssd_dt_bias, ssd_a_log, ssd_d, ssd_norm, ssd_w_out, sb_w_qkv, sb_w_out, ffn_w_in, ffn_conv_w, ffn_conv_b, ffn_w_out, loss_target, m_mix_norm, m_ffn_norm, m_final_norm, m_ssd_w_in, m_ssd_conv_w, m_ssd_conv_b, m_ssd_dt_bias, m_ssd_a_log, m_ssd_d, m_ssd_norm, m_ssd_w_out, m_sb_w_qkv, m_sb_w_out, m_ffn_w_in, m_ffn_conv_w, m_ffn_conv_b, m_ffn_w_out, v_mix_norm, v_ffn_norm, v_final_norm, v_ssd_w_in, v_ssd_conv_w, v_ssd_conv_b, v_ssd_dt_bias, v_ssd_a_log, v_ssd_d, v_ssd_norm, v_ssd_w_out, v_sb_w_qkv, v_sb_w_out, v_ffn_w_in, v_ffn_conv_w, v_ffn_conv_b, v_ffn_w_out):
    w = dict(mix_norm=mix_norm, ffn_norm=ffn_norm, final_norm=final_norm, ssd_w_in=ssd_w_in, ssd_conv_w=ssd_conv_w,
             ssd_conv_b=ssd_conv_b, ssd_dt_bias=ssd_dt_bias, ssd_a_log=ssd_a_log, ssd_d=ssd_d, ssd_norm=ssd_norm,
             ssd_w_out=ssd_w_out, sb_w_qkv=sb_w_qkv, sb_w_out=sb_w_out, ffn_w_in=ffn_w_in, ffn_conv_w=ffn_conv_w,
             ffn_conv_b=ffn_conv_b, ffn_w_out=ffn_w_out)
    m = dict(mix_norm=m_mix_norm, ffn_norm=m_ffn_norm, final_norm=m_final_norm, ssd_w_in=m_ssd_w_in, ssd_conv_w=m_ssd_conv_w,
             ssd_conv_b=m_ssd_conv_b, ssd_dt_bias=m_ssd_dt_bias, ssd_a_log=m_ssd_a_log, ssd_d=m_ssd_d, ssd_norm=m_ssd_norm,
             ssd_w_out=m_ssd_w_out, sb_w_qkv=m_sb_w_qkv, sb_w_out=m_sb_w_out, ffn_w_in=m_ffn_w_in, ffn_conv_w=m_ffn_conv_w,
             ffn_conv_b=m_ffn_conv_b, ffn_w_out=m_ffn_w_out)
    v = dict(mix_norm=v_mix_norm, ffn_norm=v_ffn_norm, final_norm=v_final_norm, ssd_w_in=v_ssd_w_in, ssd_conv_w=v_ssd_conv_w,
             ssd_conv_b=v_ssd_conv_b, ssd_dt_bias=v_ssd_dt_bias, ssd_a_log=v_ssd_a_log, ssd_d=v_ssd_d, ssd_norm=v_ssd_norm,
             ssd_w_out=v_ssd_w_out, sb_w_qkv=v_sb_w_qkv, sb_w_out=v_sb_w_out, ffn_w_in=v_ffn_w_in, ffn_conv_w=v_ffn_conv_w,
             ffn_conv_b=v_ffn_conv_b, ffn_w_out=v_ffn_w_out)
    return _step(x, loss_target, w, m, v)
```

```python
import jax
import jax.numpy as jnp
import numpy as np
from jax import lax
from jax.experimental import pallas as pl
from jax.experimental.pallas import tpu as pltpu

F32 = jnp.float32
MXU_DTYPE = jnp.bfloat16
MESH_ID = pl.DeviceIdType.MESH
N_DEV = 8

NORM_EPS = 1e-6
D_MODEL = 1024
DEPTH = 4
SSD_D_INNER = 2048
SSD_HEADS = 32
SSD_HEAD_DIM = 64
SSD_GROUPS = 8
SSD_STATE = 128
SSD_CONV = 4
SSD_CHUNK = 128
SSD_CONV_DIM = SSD_D_INNER + 2 * SSD_GROUPS * SSD_STATE
SSD_IN_DIM = SSD_D_INNER + SSD_CONV_DIM + SSD_HEADS
LANES = 128
SSD_IN_PAD = SSD_D_INNER + SSD_CONV_DIM + LANES
SB_HEADS = 16
SB_HEAD_DIM = 64
SB_TILE = 256
SB_SCALE = SB_HEAD_DIM ** -0.5
FFN_D_FF = 2816
FFN_CONV = 3
PACK_W = 1024

ADAM_LR = 0.001
ADAM_B1 = 0.9
ADAM_B2 = 0.999
ADAM_EPS = 1e-08
ADAM_WD = 0.01
ADAM_STEP = 10

VMEM_LIMIT_BYTES = 56 * 1024 * 1024


def _cp(*sem):
    return pltpu.CompilerParams(dimension_semantics=sem, vmem_limit_bytes=VMEM_LIMIT_BYTES)


def _iota(shape, dim):
    return lax.broadcasted_iota(jnp.int32, shape, dim)


def _sigmoid(x):
    return 1.0 / (1.0 + jnp.exp(-x))


def _mm(a, b):
    return lax.dot_general(a, b, (((1,), (0,)), ((), ())), preferred_element_type=F32)


def _mm_nt(a, b):
    return lax.dot_general(a, b, (((1,), (1,)), ((), ())), preferred_element_type=F32)


def _mm_tn(a, b):
    return lax.dot_general(a, b, (((0,), (0,)), ((), ())), preferred_element_type=F32)


def _split(x):
    hi = x.astype(MXU_DTYPE)
    lo = (x - hi.astype(F32)).astype(MXU_DTYPE)
    return hi, lo


def _mm_exact_rhs(x, m):
    hi, lo = _split(x)
    return _mm(jnp.concatenate([hi, lo], axis=1), jnp.concatenate([m, m], axis=0))


def _mm_exact_lhs(m, x):
    hi, lo = _split(x)
    return _mm(jnp.concatenate([m, m], axis=1), jnp.concatenate([hi, lo], axis=0))


def _my_place():
    return lax.axis_index("x"), lax.axis_index("y"), lax.axis_index("c")


def _gather_phase(phase, x_ref, out_ref, send_sems, recv_sems, local_sem):
    x, y, c = _my_place()
    me, sibling = (x, y, c), (x, y, 1 - c)
    chips = [(1 - x, y), (x, 1 - y), (1 - x, 1 - y)]

    def slot(px, py, pc):
        return out_ref.at[4 * px + 2 * py + pc]

    def copy(k, block, to, src=None):
        return pltpu.make_async_remote_copy(
            src_ref=slot(*block) if src is None else src, dst_ref=slot(*block),
            send_sem=send_sems.at[k], recv_sem=recv_sems.at[k], device_id=to, device_id_type=MESH_ID)

    if phase == "send":
        pltpu.make_async_copy(x_ref, slot(*me), local_sem).start()
        copy(0, me, sibling, src=x_ref).start()
        for j, chip in enumerate(chips):
            copy(1 + j, me, (*chip, c), src=x_ref).start()
    elif phase == "pass_on":
        for j, chip in enumerate(chips):
            copy(1 + j, (*chip, c), me).wait_recv()
            copy(4 + j, (*chip, c), sibling).start()
    else:
        copy(0, sibling, me).wait_recv()
        for j, chip in enumerate(chips):
            copy(4 + j, (*chip, 1 - c), me).wait_recv()
        copy(0, me, sibling, src=x_ref).wait_send()
        for j, chip in enumerate(chips):
            copy(1 + j, me, (*chip, c), src=x_ref).wait_send()
            copy(4 + j, (*chip, c), sibling).wait_send()
        pltpu.make_async_copy(x_ref, slot(*me), local_sem).wait()


def _all_gather(shard, name):
    r, c_ = shard.shape

    def body(*refs):
        _gather_phase("send", *refs)
        _gather_phase("pass_on", *refs)
        _gather_phase("finish", *refs)

    return pl.pallas_call(
        body, name=name,
        out_shape=jax.ShapeDtypeStruct((N_DEV, r, c_), shard.dtype),
        in_specs=[pl.BlockSpec(memory_space=pl.ANY)],
        out_specs=pl.BlockSpec(memory_space=pl.ANY),
        scratch_shapes=[pltpu.SemaphoreType.DMA((7,)), pltpu.SemaphoreType.DMA((7,)), pltpu.SemaphoreType.DMA(())],
    )(shard)


def _swap_with_sibling(buf, name):
    def body(x_ref, out_ref, send_sem, recv_sem):
        x, y, c = _my_place()
        cp = pltpu.make_async_remote_copy(src_ref=x_ref, dst_ref=out_ref, send_sem=send_sem, recv_sem=recv_sem,
                                          device_id=(x, y, 1 - c), device_id_type=MESH_ID)
        cp.start()
        cp.wait()

    return pl.pallas_call(
        body, name=name, out_shape=jax.ShapeDtypeStruct(buf.shape, buf.dtype),
        in_specs=[pl.BlockSpec(memory_space=pl.ANY)], out_specs=pl.BlockSpec(memory_space=pl.ANY),
        scratch_shapes=[pltpu.SemaphoreType.DMA(()), pltpu.SemaphoreType.DMA(())],
    )(buf)


def _exchange_chips(parts, name):
    def body(p_ref, out_ref, send_sems, recv_sems, local_sem):
        x, y, c = _my_place()
        my_q = 2 * x + y
        chips = [(1 - x, y), (x, 1 - y), (1 - x, 1 - y)]
        local = pltpu.make_async_copy(p_ref.at[my_q], out_ref.at[my_q], local_sem)
        local.start()

        def copy(k, px, py):
            return pltpu.make_async_remote_copy(
                src_ref=p_ref.at[2 * px + py], dst_ref=out_ref.at[my_q],
                send_sem=send_sems.at[k], recv_sem=recv_sems.at[k], device_id=(px, py, c), device_id_type=MESH_ID)

        def landing(k, px, py):
            return pltpu.make_async_remote_copy(
                src_ref=p_ref.at[my_q], dst_ref=out_ref.at[2 * px + py],
                send_sem=send_sems.at[k], recv_sem=recv_sems.at[k], device_id=(px, py, c), device_id_type=MESH_ID)

        sends = [copy(k, px, py) for k, (px, py) in enumerate(chips)]
        for cp in sends:
            cp.start()
        for k, (px, py) in enumerate(chips):
            landing(k, px, py).wait_recv()
        for cp in sends:
            cp.wait_send()
        local.wait()

    return pl.pallas_call(
        body, name=name, out_shape=jax.ShapeDtypeStruct(parts.shape, parts.dtype),
        in_specs=[pl.BlockSpec(memory_space=pl.ANY)], out_specs=pl.BlockSpec(memory_space=pl.ANY),
        scratch_shapes=[pltpu.SemaphoreType.DMA((3,)), pltpu.SemaphoreType.DMA((3,)), pltpu.SemaphoreType.DMA(())],
    )(parts)


RELATIONS = [(0, 0, 1), (1, 0, 0), (0, 1, 0), (1, 1, 0), (1, 0, 1), (0, 1, 1), (1, 1, 1)]
EXCHANGE_SCRATCH = [pltpu.SemaphoreType.DMA((len(RELATIONS),)), pltpu.SemaphoreType.DMA((len(RELATIONS),)),
                    pltpu.SemaphoreType.DMA(())]


def _exchange_copies(src_ref, land_ref, send_sems, recv_sems, local_sem, per_peer, incoming=True):
    x, y, c = _my_place()
    me = 4 * x + 2 * y + c

    def src(j):
        return src_ref.at[j] if per_peer else src_ref

    local = pltpu.make_async_copy(src(me), land_ref.at[me], local_sem)
    pairs = []
    for k, (dx, dy, dc) in enumerate(RELATIONS):
        peer = (1 - x if dx else x, 1 - y if dy else y, 1 - c if dc else c)
        j = 4 * peer[0] + 2 * peer[1] + peer[2]
        sems = dict(send_sem=send_sems.at[k], recv_sem=recv_sems.at[k], device_id=peer, device_id_type=MESH_ID)
        pairs.append((pltpu.make_async_remote_copy(src_ref=src(j), dst_ref=land_ref.at[me], **sems),
                      pltpu.make_async_remote_copy(src_ref=src(me), dst_ref=land_ref.at[j], **sems) if incoming else None))
    return local, pairs


def _exchange_start(*refs, per_peer):
    local, pairs = _exchange_copies(*refs, per_peer, incoming=False)
    local.start()
    for outgoing, _ in pairs:
        outgoing.start()


def _exchange_finish(*refs, per_peer):
    local, pairs = _exchange_copies(*refs, per_peer)
    for _, incoming in pairs:
        incoming.wait_recv()
    for outgoing, _ in pairs:
        outgoing.wait_send()
    local.wait()


TOKEN_ROWS = 1024


def _matmul(a, b, mode, out_dtype, name, add=None, tm=512, tn=512, tk=512):
    if mode == "nn":
        (m, k), (k2, n) = a.shape, b.shape
    elif mode == "nt":
        (m, k), (n, k2) = a.shape, b.shape
    else:
        (k, m), (k2, n) = a.shape, b.shape
    assert k == k2, (a.shape, b.shape, mode)
    tm, tn, tk = min(tm, m), min(tn, n), min(tk, k)
    assert m % tm == 0 and n % tn == 0 and k % tk == 0, (m, n, k, tm, tn, tk)
    nk = k // tk
    mm = {"nn": _mm, "nt": _mm_nt, "tn": _mm_tn}[mode]

    def body(*refs):
        if add is None:
            a_ref, b_ref, o_ref, acc_ref = refs
        else:
            a_ref, b_ref, add_ref, o_ref, acc_ref = refs
        kk = pl.program_id(2)

        @pl.when(kk == 0)
        def _():
            acc_ref[...] = jnp.zeros_like(acc_ref)

        acc_ref[...] += mm(a_ref[...].astype(MXU_DTYPE), b_ref[...].astype(MXU_DTYPE))

        @pl.when(kk == nk - 1)
        def _():
            res = acc_ref[...]
            if add is not None:
                res = res + add_ref[...]
            o_ref[...] = res.astype(o_ref.dtype)

    a_spec = {"nn": pl.BlockSpec((tm, tk), lambda i, j, kk: (i, kk)),
              "nt": pl.BlockSpec((tm, tk), lambda i, j, kk: (i, kk)),
              "tn": pl.BlockSpec((tk, tm), lambda i, j, kk: (kk, i))}[mode]
    b_spec = {"nn": pl.BlockSpec((tk, tn), lambda i, j, kk: (kk, j)),
              "nt": pl.BlockSpec((tn, tk), lambda i, j, kk: (j, kk)),
              "tn": pl.BlockSpec((tk, tn), lambda i, j, kk: (kk, j))}[mode]
    o_spec = pl.BlockSpec((tm, tn), lambda i, j, kk: (i, j))
    in_specs, args = [a_spec, b_spec], [a, b]
    if add is not None:
        in_specs.append(o_spec)
        args.append(add)
    return pl.pallas_call(
        body, name=name, grid=(m // tm, n // tn, nk), in_specs=in_specs, out_specs=o_spec,
        out_shape=jax.ShapeDtypeStruct((m, n), out_dtype),
        scratch_shapes=[pltpu.VMEM((tm, tn), F32)],
        compiler_params=_cp("parallel", "parallel", "arbitrary"),
    )(*args)


def _rmsnorm(x, g, name):
    t, d = x.shape
    tm = min(TOKEN_ROWS, t)

    def body(x_ref, g_ref, o_ref):
        xv = x_ref[...]
        r = lax.rsqrt(jnp.mean(xv * xv, axis=-1, keepdims=True) + NORM_EPS)
        o_ref[...] = (xv * r * g_ref[...]).astype(o_ref.dtype)

    return pl.pallas_call(
        body, name=name, grid=(t // tm,),
        in_specs=[pl.BlockSpec((tm, d), lambda i: (i, 0)), pl.BlockSpec((1, d), lambda i: (0, 0))],
        out_specs=pl.BlockSpec((tm, d), lambda i: (i, 0)),
        out_shape=jax.ShapeDtypeStruct((t, d), MXU_DTYPE), compiler_params=_cp("parallel"),
    )(x, g.reshape(1, d))


def _rmsnorm_bwd(x, g, dh, dres, name):
    t, d = x.shape
    tm = min(TOKEN_ROWS, t)

    def body(x_ref, g_ref, dh_ref, dres_ref, dx_ref, dg_ref):
        @pl.when(pl.program_id(0) == 0)
        def _():
            dg_ref[...] = jnp.zeros_like(dg_ref)

        xv = x_ref[...]
        r = lax.rsqrt(jnp.mean(xv * xv, axis=-1, keepdims=True) + NORM_EPS)
        xn = xv * r
        dhv = dh_ref[...]
        u = dhv * g_ref[...]
        dx_ref[...] = dres_ref[...] + r * (u - xn * jnp.mean(u * xn, axis=-1, keepdims=True))
        dg_ref[...] += jnp.sum(dhv * xn, axis=0, keepdims=True)

    row = pl.BlockSpec((tm, d), lambda i: (i, 0))
    vec = pl.BlockSpec((1, d), lambda i: (0, 0))
    return pl.pallas_call(
        body, name=name, grid=(t // tm,), in_specs=[row, vec, row, row], out_specs=[row, vec],
        out_shape=[jax.ShapeDtypeStruct((t, d), F32), jax.ShapeDtypeStruct((1, d), F32)],
        compiler_params=_cp("arbitrary"),
    )(x, g.reshape(1, d), dh, dres)


def _final_norm_loss(x, g, target, name):
    t, d = x.shape
    tm = min(TOKEN_ROWS, t)

    def body(x_ref, g_ref, t_ref, dx_ref, dg_ref, loss_ref):
        @pl.when(pl.program_id(0) == 0)
        def _():
            dg_ref[...] = jnp.zeros_like(dg_ref)
            loss_ref[...] = jnp.zeros_like(loss_ref)

        xv = x_ref[...]
        gv = g_ref[...]
        r = lax.rsqrt(jnp.mean(xv * xv, axis=-1, keepdims=True) + NORM_EPS)
        xn = xv * r
        err = xn * gv - t_ref[...]
        per_tok = jnp.mean(err * err, axis=-1, keepdims=True)
        loss_ref[...] += jnp.broadcast_to(0.5 * jnp.sum(per_tok, axis=0, keepdims=True), loss_ref.shape)
        dy = err * (1.0 / d)
        u = dy * gv
        dx_ref[...] = r * (u - xn * jnp.mean(u * xn, axis=-1, keepdims=True))
        dg_ref[...] += jnp.sum(dy * xn, axis=0, keepdims=True)

    row = pl.BlockSpec((tm, d), lambda i: (i, 0))
    vec = pl.BlockSpec((1, d), lambda i: (0, 0))
    return pl.pallas_call(
        body, name=name, grid=(t // tm,), in_specs=[row, vec, row],
        out_specs=[row, vec, pl.BlockSpec((1, LANES), lambda i: (0, 0))],
        out_shape=[jax.ShapeDtypeStruct((t, d), F32), jax.ShapeDtypeStruct((1, d), F32),
                   jax.ShapeDtypeStruct((1, LANES), F32)],
        compiler_params=_cp("arbitrary"),
    )(x, g.reshape(1, d), target)


CONV_COLS = 128


def _shifts_down(p, width):
    row = _iota(p.shape, 0)
    return [jnp.where(row >= s, pltpu.roll(p, s, axis=0), 0.0) for s in range(1, width)]


def _shifts_up(p, width):
    n = p.shape[0]
    row = _iota(p.shape, 0)
    return [jnp.where(row < n - s, pltpu.roll(p, n - s, axis=0), 0.0) for s in range(1, width)]


def _conv_pre(p, shifted, w_ref, b_ref):
    width = w_ref.shape[0]
    u = b_ref[...] + w_ref[width - 1:width, :] * p
    for s in range(1, width):
        u = u + w_ref[width - 1 - s:width - s, :] * shifted[s - 1]
    return u


def _conv_transpose(du, w_ref):
    width = w_ref.shape[0]
    shifted = _shifts_up(du, width)
    dp = w_ref[width - 1:width, :] * du
    for s in range(1, width):
        dp = dp + w_ref[width - 1 - s:width - s, :] * shifted[s - 1]
    return dp


def _conv_wgrad(du, p, shifted, dw_ref, db_ref):
    width = dw_ref.shape[0]
    db_ref[...] = jnp.sum(du, axis=0, keepdims=True)
    dw_ref[width - 1:width, :] = jnp.sum(du * p, axis=0, keepdims=True)
    for s in range(1, width):
        dw_ref[width - 1 - s:width - s, :] = jnp.sum(du * shifted[s - 1], axis=0, keepdims=True)


def _ssd_conv_fwd(proj, w, b, name):
    t = proj.shape[0]
    cb = CONV_COLS
    off = SSD_D_INNER // cb

    def body(p_ref, w_ref, b_ref, o_ref):
        p = p_ref[...]
        u = _conv_pre(p, _shifts_down(p, SSD_CONV), w_ref, b_ref)
        o_ref[...] = u * _sigmoid(u)

    return pl.pallas_call(
        body, name=name, grid=(SSD_CONV_DIM // cb,),
        in_specs=[pl.BlockSpec((t, cb), lambda j: (0, j + off)), pl.BlockSpec((SSD_CONV, cb), lambda j: (0, j)),
                  pl.BlockSpec((1, cb), lambda j: (0, j))],
        out_specs=pl.BlockSpec((t, cb), lambda j: (0, j)),
        out_shape=jax.ShapeDtypeStruct((t, SSD_CONV_DIM), F32), compiler_params=_cp("parallel"),
    )(proj, w, b.reshape(1, -1))


def _ssd_conv_bwd(proj, w, b, dact, name):
    t = proj.shape[0]
    cb = CONV_COLS
    off = SSD_D_INNER // cb

    def body(p_ref, w_ref, b_ref, da_ref, dp_ref, dw_ref, db_ref):
        p = p_ref[...]
        shifted = _shifts_down(p, SSD_CONV)
        u = _conv_pre(p, shifted, w_ref, b_ref)
        sg = _sigmoid(u)
        du = da_ref[...] * (sg * (1.0 + u * (1.0 - sg)))
        dp_ref[...] = _conv_transpose(du, w_ref).astype(dp_ref.dtype)
        _conv_wgrad(du, p, shifted, dw_ref, db_ref)

    col = pl.BlockSpec((t, cb), lambda j: (0, j))
    wspec = pl.BlockSpec((SSD_CONV, cb), lambda j: (0, j))
    bspec = pl.BlockSpec((1, cb), lambda j: (0, j))
    return pl.pallas_call(
        body, name=name, grid=(SSD_CONV_DIM // cb,),
        in_specs=[pl.BlockSpec((t, cb), lambda j: (0, j + off)), wspec, bspec, col],
        out_specs=[col, wspec, bspec],
        out_shape=[jax.ShapeDtypeStruct((t, SSD_CONV_DIM), MXU_DTYPE), jax.ShapeDtypeStruct((SSD_CONV, SSD_CONV_DIM), F32),
                   jax.ShapeDtypeStruct((1, SSD_CONV_DIM), F32)],
        compiler_params=_cp("parallel"),
    )(proj, w, b.reshape(1, -1), dact)


def _ffn_conv_fwd(proj, w, b, name):
    t = proj.shape[0]
    cb = CONV_COLS
    nb = FFN_D_FF // cb

    def body(pg_ref, pu_ref, wg_ref, wu_ref, bg_ref, bu_ref, o_ref):
        pg, pu = pg_ref[...], pu_ref[...]
        ug = _conv_pre(pg, _shifts_down(pg, FFN_CONV), wg_ref, bg_ref)
        uu = _conv_pre(pu, _shifts_down(pu, FFN_CONV), wu_ref, bu_ref)
        o_ref[...] = (ug * _sigmoid(ug) * uu).astype(o_ref.dtype)

    gcol = pl.BlockSpec((t, cb), lambda j: (0, j))
    ucol = pl.BlockSpec((t, cb), lambda j: (0, j + nb))
    b2 = b.reshape(1, -1)
    return pl.pallas_call(
        body, name=name, grid=(nb,),
        in_specs=[gcol, ucol, pl.BlockSpec((FFN_CONV, cb), lambda j: (0, j)), pl.BlockSpec((FFN_CONV, cb), lambda j: (0, j + nb)),
                  pl.BlockSpec((1, cb), lambda j: (0, j)), pl.BlockSpec((1, cb), lambda j: (0, j + nb))],
        out_specs=gcol, out_shape=jax.ShapeDtypeStruct((t, FFN_D_FF), MXU_DTYPE), compiler_params=_cp("parallel"),
    )(proj, proj, w, w, b2, b2)


def _ffn_conv_bwd(proj, w, b, dact, name):
    t = proj.shape[0]
    cb = CONV_COLS
    nb = FFN_D_FF // cb

    def body(pg_ref, pu_ref, wg_ref, wu_ref, bg_ref, bu_ref, da_ref,
             dpg_ref, dpu_ref, dwg_ref, dwu_ref, dbg_ref, dbu_ref):
        pg, pu = pg_ref[...], pu_ref[...]
        pg_shifted, pu_shifted = _shifts_down(pg, FFN_CONV), _shifts_down(pu, FFN_CONV)
        ug = _conv_pre(pg, pg_shifted, wg_ref, bg_ref)
        uu = _conv_pre(pu, pu_shifted, wu_ref, bu_ref)
        sg = _sigmoid(ug)
        da = da_ref[...]
        dug = da * uu * (sg * (1.0 + ug * (1.0 - sg)))
        duu = da * (ug * sg)
        dpg_ref[...] = _conv_transpose(dug, wg_ref).astype(dpg_ref.dtype)
        dpu_ref[...] = _conv_transpose(duu, wu_ref).astype(dpu_ref.dtype)
        _conv_wgrad(dug, pg, pg_shifted, dwg_ref, dbg_ref)
        _conv_wgrad(duu, pu, pu_shifted, dwu_ref, dbu_ref)

    gcol = pl.BlockSpec((t, cb), lambda j: (0, j))
    ucol = pl.BlockSpec((t, cb), lambda j: (0, j + nb))
    wg = pl.BlockSpec((FFN_CONV, cb), lambda j: (0, j))
    wu = pl.BlockSpec((FFN_CONV, cb), lambda j: (0, j + nb))
    bg = pl.BlockSpec((1, cb), lambda j: (0, j))
    bu = pl.BlockSpec((1, cb), lambda j: (0, j + nb))
    b2 = b.reshape(1, -1)
    half = jax.ShapeDtypeStruct((t, FFN_D_FF), MXU_DTYPE)
    return pl.pallas_call(
        body, name=name, grid=(nb,),
        in_specs=[gcol, ucol, wg, wu, bg, bu, gcol],
        out_specs=[gcol, gcol, wg, wg, bg, bg],
        out_shape=[half, half, jax.ShapeDtypeStruct((FFN_CONV, FFN_D_FF), F32), jax.ShapeDtypeStruct((FFN_CONV, FFN_D_FF), F32),
                   jax.ShapeDtypeStruct((1, FFN_D_FF), F32), jax.ShapeDtypeStruct((1, FFN_D_FF), F32)],
        compiler_params=_cp("parallel"),
    )(proj, proj, w, w, b2, b2, dact)


SSD_ROWS = 256
DT_COL = (SSD_D_INNER + SSD_CONV_DIM) // LANES


def _head_expand():
    h = np.arange(LANES)[:, None]
    col = np.arange(SSD_D_INNER)[None, :]
    return jnp.asarray((col // SSD_HEAD_DIM == h), MXU_DTYPE)


def _chunk_tri(n, lower):
    t = _iota((n, n), 0)
    s = _iota((n, n), 1)
    shift = SSD_CHUNK.bit_length() - 1
    same = jnp.right_shift(t, shift) == jnp.right_shift(s, shift)
    tri = (s <= t) if lower else (s >= t)
    return jnp.where(same & tri, 1.0, 0.0).astype(MXU_DTYPE)


def _softplus(x):
    return jnp.maximum(x, 0.0) + jnp.log(1.0 + jnp.exp(-jnp.abs(x)))


def _ssd_dt_fwd(proj, act, dt_bias, a_neg, expand, name):
    t = proj.shape[0]
    tm = min(SSD_ROWS, t)

    def body(raw_ref, xs_ref, bias_ref, a_ref, e_ref, xdt_ref, dt_ref, acum_ref):
        lane = _iota((tm, LANES), 1)
        dt = jnp.where(lane < SSD_HEADS, _softplus(raw_ref[...] + bias_ref[...]), 0.0)
        dt_ref[...] = dt
        xdt_ref[...] = xs_ref[...] * _mm_exact_rhs(dt, e_ref[...])
        acum_ref[...] = _mm_exact_lhs(_chunk_tri(tm, True), a_ref[...] * dt)

    vec = pl.BlockSpec((1, LANES), lambda i: (0, 0))
    return pl.pallas_call(
        body, name=name, grid=(t // tm,),
        in_specs=[pl.BlockSpec((tm, LANES), lambda i: (i, DT_COL)), pl.BlockSpec((tm, SSD_D_INNER), lambda i: (i, 0)),
                  vec, vec, pl.BlockSpec((LANES, SSD_D_INNER), lambda i: (0, 0))],
        out_specs=[pl.BlockSpec((tm, SSD_D_INNER), lambda i: (i, 0)), pl.BlockSpec((tm, LANES), lambda i: (i, 0)),
                   pl.BlockSpec((tm, LANES), lambda i: (i, 0))],
        out_shape=[jax.ShapeDtypeStruct((t, SSD_D_INNER), F32), jax.ShapeDtypeStruct((t, LANES), F32),
                   jax.ShapeDtypeStruct((t, LANES), F32)],
        compiler_params=_cp("parallel"),
    )(proj, act, dt_bias, a_neg, expand)


def _ssd_dt_bwd(proj, act, dt, dxdt, dyy, dacum, dt_bias, a_neg, d_exp, expand, expand_t, name):
    t = proj.shape[0]
    tm = min(SSD_ROWS, t)

    def body(raw_ref, xs_ref, dt_ref, dxdt_ref, dyy_ref, dac_ref, bias_ref, a_ref, dsk_ref, e_ref, et_ref,
             dxs_ref, draw_ref, da_ref, dbias_ref, dd_ref):
        @pl.when(pl.program_id(0) == 0)
        def _():
            da_ref[...] = jnp.zeros_like(da_ref)
            dbias_ref[...] = jnp.zeros_like(dbias_ref)
            dd_ref[...] = jnp.zeros_like(dd_ref)

        lane = _iota((tm, LANES), 1)
        xs, dt, dxdt, dyy = xs_ref[...], dt_ref[...], dxdt_ref[...], dyy_ref[...]
        dxs_ref[...] = dxdt * _mm_exact_rhs(dt, e_ref[...]) + dsk_ref[...] * dyy
        dd_ref[...] += jnp.sum(dyy * xs, axis=0, keepdims=True)
        ddt = _mm_exact_rhs(dxdt * xs, et_ref[...])
        da = _mm_exact_lhs(_chunk_tri(tm, False), dac_ref[...])
        ddt = ddt + da * a_ref[...]
        da_ref[...] += jnp.sum(da * dt, axis=0, keepdims=True)
        draw = jnp.where(lane < SSD_HEADS, ddt * _sigmoid(raw_ref[...] + bias_ref[...]), 0.0)
        dbias_ref[...] += jnp.sum(draw, axis=0, keepdims=True)
        draw_ref[...] = draw.astype(draw_ref.dtype)

    wide = pl.BlockSpec((tm, SSD_D_INNER), lambda i: (i, 0))
    thin = pl.BlockSpec((tm, LANES), lambda i: (i, 0))
    vec = pl.BlockSpec((1, LANES), lambda i: (0, 0))
    wvec = pl.BlockSpec((1, SSD_D_INNER), lambda i: (0, 0))
    return pl.pallas_call(
        body, name=name, grid=(t // tm,),
        in_specs=[pl.BlockSpec((tm, LANES), lambda i: (i, DT_COL)), wide, thin, wide, wide, thin, vec, vec, wvec,
                  pl.BlockSpec((LANES, SSD_D_INNER), lambda i: (0, 0)), pl.BlockSpec((SSD_D_INNER, LANES), lambda i: (0, 0))],
        out_specs=[wide, thin, vec, vec, wvec],
        out_shape=[jax.ShapeDtypeStruct((t, SSD_D_INNER), F32), jax.ShapeDtypeStruct((t, LANES), MXU_DTYPE),
                   jax.ShapeDtypeStruct((1, LANES), F32), jax.ShapeDtypeStruct((1, LANES), F32),
                   jax.ShapeDtypeStruct((1, SSD_D_INNER), F32)],
        compiler_params=_cp("arbitrary"),
    )(proj, act, dt, dxdt, dyy, dacum, dt_bias, a_neg, d_exp, expand, expand_t)


SSD_PAIR = 2 * SSD_HEAD_DIM
HEADS_PER_GROUP = SSD_HEADS // SSD_GROUPS
GROUP_COLS = HEADS_PER_GROUP * SSD_HEAD_DIM
B_COL0 = SSD_D_INNER // SSD_STATE
C_COL0 = (SSD_D_INNER + SSD_GROUPS * SSD_STATE) // SSD_STATE


MASKED_LOG = -1e30
SCAN_GROUPS = 4
SCAN_STEPS = SSD_GROUPS // SCAN_GROUPS


def _scan_step_is(g, c):
    return (pl.program_id(0) == g) & (pl.program_id(1) == c)


def _ssd_scan_fwd(xdt, act, acum_g, acum_gt, name, carried=None):
    t = xdt.shape[0]
    nc = t // SSD_CHUNK
    ln = SSD_CHUNK
    c_in_specs, c_in, c_out_specs, c_out = _carried_specs(carried)

    def body(*refs):
        if carried is None:
            x_ref, b_ref, c_ref, ac_ref, act_ref, y_ref, sst_ref, state = refs
            comm_refs = None
        else:
            x_ref, b_ref, c_ref, ac_ref, act_ref, src_ref, y_ref, sst_ref, land_ref, state, send_sems, recv_sems, local_sem = refs
            comm_refs = (src_ref, land_ref, send_sems, recv_sems, local_sem)
        finish = _carried_hooks(carried, comm_refs, _scan_step_is(0, 0), _scan_step_is(SCAN_STEPS - 1, (3 * nc) // 4),
                                _scan_step_is(SCAN_STEPS - 1, nc - 1))

        @pl.when(pl.program_id(1) == 0)
        def _():
            state[...] = jnp.zeros_like(state)

        causal = _iota((ln, ln), 1) <= _iota((ln, ln), 0)
        lo_mask = _iota((ln, SSD_PAIR), 1) < SSD_HEAD_DIM
        lo_rows = _iota((SSD_PAIR, SSD_STATE), 0) < SSD_HEAD_DIM
        for gg in range(SCAN_GROUPS):
            sst_ref[0, gg] = state[gg * GROUP_COLS:(gg + 1) * GROUP_COLS, :]
            bm = b_ref[:, gg * SSD_STATE:(gg + 1) * SSD_STATE].astype(MXU_DTYPE)
            cm = c_ref[:, gg * SSD_STATE:(gg + 1) * SSD_STATE].astype(MXU_DTYPE)
            cb = _mm_nt(cm, bm)
            ac, act_ = ac_ref[gg], act_ref[gg]
            e_last = jnp.exp(ac[ln - 1:ln, :])
            ac_rows = [jnp.broadcast_to(ac[:, h:h + 1], (ln, SSD_PAIR)) for h in range(HEADS_PER_GROUP)]
            for pr in range(2):
                first = gg * GROUP_COLS + pr * SSD_PAIR
                cols = slice(first, first + SSD_PAIR)
                xp = x_ref[:, cols]
                sp = state[cols, :]
                pair_ac = jnp.where(lo_mask, ac_rows[2 * pr], ac_rows[2 * pr + 1])
                ydiag = jnp.zeros((ln, SSD_PAIR), F32)
                for hh in range(2):
                    h = 2 * pr + hh
                    seg = ac_rows[h] - act_[h:h + 1, :]
                    dec = jnp.exp(jnp.where(causal, seg, MASKED_LOG))
                    mask = lo_mask if hh == 0 else jnp.logical_not(lo_mask)
                    ydiag = ydiag + _mm((cb * dec).astype(MXU_DTYPE), jnp.where(mask, xp, 0.0).astype(MXU_DTYPE))
                yoff = _mm_nt(cm, sp.astype(MXU_DTYPE)) * jnp.exp(pair_ac)
                y_ref[:, cols] = ydiag + yoff
                xw = (xp * jnp.exp(pair_ac[ln - 1:ln, :] - pair_ac)).astype(MXU_DTYPE)
                el = jnp.where(lo_rows, e_last[:, 2 * pr:2 * pr + 1], e_last[:, 2 * pr + 1:2 * pr + 2])
                state[cols, :] = sp * el + _mm_tn(xw, bm)
        finish()

    sg = SCAN_GROUPS
    return pl.pallas_call(
        body, name=name, grid=(SCAN_STEPS, nc),
        in_specs=[pl.BlockSpec((ln, sg * GROUP_COLS), lambda g, c: (c, g)),
                  pl.BlockSpec((ln, sg * SSD_STATE), lambda g, c: (c, B_COL0 // sg + g)),
                  pl.BlockSpec((ln, sg * SSD_STATE), lambda g, c: (c, C_COL0 // sg + g)),
                  pl.BlockSpec((sg, ln, HEADS_PER_GROUP), lambda g, c: (g, c, 0)),
                  pl.BlockSpec((sg, HEADS_PER_GROUP, ln), lambda g, c: (g, 0, c))] + c_in_specs,
        out_specs=[pl.BlockSpec((ln, sg * GROUP_COLS), lambda g, c: (c, g)),
                   pl.BlockSpec((1, sg, GROUP_COLS, SSD_STATE), lambda g, c: (c, g, 0, 0))] + c_out_specs,
        out_shape=[jax.ShapeDtypeStruct((t, SSD_D_INNER), F32),
                   jax.ShapeDtypeStruct((nc, SSD_GROUPS, GROUP_COLS, SSD_STATE), F32)] + c_out,
        scratch_shapes=[pltpu.VMEM((sg * GROUP_COLS, SSD_STATE), F32)] + (EXCHANGE_SCRATCH if carried else []),
        compiler_params=_cp("arbitrary", "arbitrary"),
    )(xdt, act, act, acum_g, acum_gt, *c_in)


def _ssd_scan_bwd(xdt, act, acum_g, acum_gt, states, dy, name, carried=None):
    t = xdt.shape[0]
    nc = t // SSD_CHUNK
    ln = SSD_CHUNK
    c_in_specs, c_in, c_out_specs, c_out = _carried_specs(carried)

    def body(*refs):
        if carried is None:
            x_ref, b_ref, c_ref, ac_ref, act_ref, sst_ref, dy_ref, dx_ref, db_ref, dc_ref, dacol_ref, darow_ref, dstate = refs
            comm_refs = None
        else:
            (x_ref, b_ref, c_ref, ac_ref, act_ref, sst_ref, dy_ref, src_ref, dx_ref, db_ref, dc_ref, dacol_ref, darow_ref,
             land_ref, dstate, send_sems, recv_sems, local_sem) = refs
            comm_refs = (src_ref, land_ref, send_sems, recv_sems, local_sem)
        finish = _carried_hooks(carried, comm_refs, _scan_step_is(0, 0), _scan_step_is(SCAN_STEPS - 1, (3 * nc) // 4),
                                _scan_step_is(SCAN_STEPS - 1, nc - 1))

        @pl.when(pl.program_id(1) == 0)
        def _():
            dstate[...] = jnp.zeros_like(dstate)

        for gg in range(SCAN_GROUPS):
            group_bwd(gg, x_ref, b_ref, c_ref, ac_ref, act_ref, sst_ref, dy_ref, dx_ref, db_ref, dc_ref, dacol_ref, darow_ref, dstate)
        finish()

    def group_bwd(gg, x_ref, b_ref, c_ref, ac_ref, act_ref, sst_ref, dy_ref, dx_ref, db_ref, dc_ref, dacol_ref, darow_ref, dstate):
        bc_cols = slice(gg * SSD_STATE, (gg + 1) * SSD_STATE)
        bm = b_ref[:, bc_cols].astype(MXU_DTYPE)
        cm = c_ref[:, bc_cols].astype(MXU_DTYPE)
        cb = _mm_nt(cm, bm)
        ac, act_ = ac_ref[gg], act_ref[gg]
        causal = _iota((ln, ln), 1) <= _iota((ln, ln), 0)
        lo_mask = _iota((ln, SSD_PAIR), 1) < SSD_HEAD_DIM
        lo_rows = _iota((SSD_PAIR, SSD_STATE), 0) < SSD_HEAD_DIM
        lane4 = _iota((ln, HEADS_PER_GROUP), 1)
        sub4 = _iota((HEADS_PER_GROUP, ln), 0)
        is_last = _iota((ln, 1), 0) == ln - 1
        e_last = jnp.exp(ac[ln - 1:ln, :])
        ac_rows = [jnp.broadcast_to(ac[:, h:h + 1], (ln, SSD_PAIR)) for h in range(HEADS_PER_GROUP)]
        dcb = jnp.zeros((ln, ln), F32)
        dc_acc = jnp.zeros((ln, SSD_STATE), F32)
        db_acc = jnp.zeros((ln, SSD_STATE), F32)
        dacol = jnp.zeros((ln, HEADS_PER_GROUP), F32)
        darow = jnp.zeros((HEADS_PER_GROUP, ln), F32)
        for pr in range(2):
            in_group = slice(pr * SSD_PAIR, (pr + 1) * SSD_PAIR)
            cols = slice(gg * GROUP_COLS + pr * SSD_PAIR, gg * GROUP_COLS + (pr + 1) * SSD_PAIR)
            xp = x_ref[:, cols]
            dyp = dy_ref[:, cols]
            sp = sst_ref[0, gg, in_group, :]
            dsp = dstate[cols, :]
            pair_ac = jnp.where(lo_mask, ac_rows[2 * pr], ac_rows[2 * pr + 1])
            ea = jnp.exp(pair_ac)
            w = jnp.exp(pair_ac[ln - 1:ln, :] - pair_ac)
            dye = (dyp * ea).astype(MXU_DTYPE)
            dx_state = w * _mm_nt(bm, dsp.astype(MXU_DTYPE))
            yoff = _mm_nt(cm, sp.astype(MXU_DTYPE)) * ea
            dxp = dx_state
            for hh in range(2):
                h = 2 * pr + hh
                mask = lo_mask if hh == 0 else jnp.logical_not(lo_mask)
                rmask = lo_rows if hh == 0 else jnp.logical_not(lo_rows)
                seg = ac_rows[h] - act_[h:h + 1, :]
                dec = jnp.exp(jnp.where(causal, seg, MASKED_LOG))
                m = cb * dec
                dym = jnp.where(mask, dyp, 0.0).astype(MXU_DTYPE)
                xm = jnp.where(mask, xp, 0.0).astype(MXU_DTYPE)
                g = _mm_nt(dym, xm)
                dxp = dxp + _mm_tn(m.astype(MXU_DTYPE), dym)
                dcb = dcb + dec * g
                mg = m * g
                rs = jnp.sum(mg, axis=1, keepdims=True)
                cs = jnp.sum(mg, axis=0, keepdims=True)
                t_off = jnp.sum(jnp.where(mask, dyp * yoff, 0.0), axis=1, keepdims=True)
                q = jnp.sum(jnp.where(mask, xp * dx_state, 0.0), axis=1, keepdims=True)
                qsum = jnp.sum(q, axis=0, keepdims=True)
                ds_s = jnp.sum(jnp.sum(jnp.where(rmask, dsp * sp, 0.0), axis=1, keepdims=True), axis=0, keepdims=True)
                extra = qsum + e_last[:, h:h + 1] * ds_s
                col = rs + t_off - q + jnp.where(is_last, extra, 0.0)
                dacol = jnp.where(lane4 == h, col, dacol)
                darow = jnp.where(sub4 == h, -cs, darow)
            dx_ref[:, cols] = dxp
            dc_acc = dc_acc + _mm(dye, sp.astype(MXU_DTYPE))
            db_acc = db_acc + _mm((xp * w).astype(MXU_DTYPE), dsp.astype(MXU_DTYPE))
            el = jnp.where(lo_rows, e_last[:, 2 * pr:2 * pr + 1], e_last[:, 2 * pr + 1:2 * pr + 2])
            dstate[cols, :] = dsp * el + _mm_tn(dye, cm)
        dcbm = dcb.astype(MXU_DTYPE)
        dc_ref[:, bc_cols] = _mm(dcbm, bm) + dc_acc
        db_ref[:, bc_cols] = _mm_tn(dcbm, cm) + db_acc
        dacol_ref[gg] = dacol
        darow_ref[gg] = darow

    def rev(c):
        return nc - 1 - c

    sg = SCAN_GROUPS
    grp = pl.BlockSpec((ln, sg * GROUP_COLS), lambda g, c: (rev(c), g))
    return pl.pallas_call(
        body, name=name, grid=(SCAN_STEPS, nc),
        in_specs=[grp,
                  pl.BlockSpec((ln, sg * SSD_STATE), lambda g, c: (rev(c), B_COL0 // sg + g)),
                  pl.BlockSpec((ln, sg * SSD_STATE), lambda g, c: (rev(c), C_COL0 // sg + g)),
                  pl.BlockSpec((sg, ln, HEADS_PER_GROUP), lambda g, c: (g, rev(c), 0)),
                  pl.BlockSpec((sg, HEADS_PER_GROUP, ln), lambda g, c: (g, 0, rev(c))),
                  pl.BlockSpec((1, sg, GROUP_COLS, SSD_STATE), lambda g, c: (rev(c), g, 0, 0)),
                  grp] + c_in_specs,
        out_specs=[grp,
                   pl.BlockSpec((ln, sg * SSD_STATE), lambda g, c: (rev(c), g)),
                   pl.BlockSpec((ln, sg * SSD_STATE), lambda g, c: (rev(c), g)),
                   pl.BlockSpec((sg, ln, HEADS_PER_GROUP), lambda g, c: (g, rev(c), 0)),
                   pl.BlockSpec((sg, HEADS_PER_GROUP, ln), lambda g, c: (g, 0, rev(c)))] + c_out_specs,
        out_shape=[jax.ShapeDtypeStruct((t, SSD_D_INNER), F32),
                   jax.ShapeDtypeStruct((t, SSD_GROUPS * SSD_STATE), F32),
                   jax.ShapeDtypeStruct((t, SSD_GROUPS * SSD_STATE), F32),
                   jax.ShapeDtypeStruct((SSD_GROUPS, t, HEADS_PER_GROUP), F32),
                   jax.ShapeDtypeStruct((SSD_GROUPS, HEADS_PER_GROUP, t), F32)] + c_out,
        scratch_shapes=[pltpu.VMEM((sg * GROUP_COLS, SSD_STATE), F32)] + (EXCHANGE_SCRATCH if carried else []),
        compiler_params=_cp("arbitrary", "arbitrary"),
    )(xdt, act, act, acum_g, acum_gt, states, dy, *c_in)


GN_ROWS = 256


def _gated_norm_parts(y_ref, xs_ref, z_ref, dsk_ref):
    yy = y_ref[...] + dsk_ref[...] * xs_ref[...]
    z = z_ref[...]
    sz = _sigmoid(z)
    silu = z * sz
    u = yy * silu
    r = lax.rsqrt(jnp.mean(u * u, axis=-1, keepdims=True) + NORM_EPS)
    return yy, z, sz, silu, u, r


def _gated_norm_fwd(y, act, proj, d_exp, g, name):
    t = y.shape[0]
    tm = min(GN_ROWS, t)

    def body(y_ref, xs_ref, z_ref, dsk_ref, g_ref, o_ref):
        _, _, _, _, u, r = _gated_norm_parts(y_ref, xs_ref, z_ref, dsk_ref)
        o_ref[...] = (u * r * g_ref[...]).astype(o_ref.dtype)

    wide = pl.BlockSpec((tm, SSD_D_INNER), lambda i: (i, 0))
    wvec = pl.BlockSpec((1, SSD_D_INNER), lambda i: (0, 0))
    return pl.pallas_call(
        body, name=name, grid=(t // tm,), in_specs=[wide, wide, wide, wvec, wvec], out_specs=wide,
        out_shape=jax.ShapeDtypeStruct((t, SSD_D_INNER), MXU_DTYPE), compiler_params=_cp("parallel"),
    )(y, act, proj, d_exp, g.reshape(1, -1))


def _gated_norm_bwd(y, act, proj, d_exp, g, dn, name):
    t = y.shape[0]
    tm = min(GN_ROWS, t)

    def body(y_ref, xs_ref, z_ref, dsk_ref, g_ref, dn_ref, dyy_ref, dz_ref, dg_ref):
        @pl.when(pl.program_id(0) == 0)
        def _():
            dg_ref[...] = jnp.zeros_like(dg_ref)

        yy, z, sz, silu, u, r = _gated_norm_parts(y_ref, xs_ref, z_ref, dsk_ref)
        un = u * r
        dn = dn_ref[...]
        v = dn * g_ref[...]
        du = r * (v - un * jnp.mean(v * un, axis=-1, keepdims=True))
        dg_ref[...] += jnp.sum(dn * un, axis=0, keepdims=True)
        dyy_ref[...] = du * silu
        dz_ref[...] = (du * yy * (sz * (1.0 + z * (1.0 - sz)))).astype(dz_ref.dtype)

    wide = pl.BlockSpec((tm, SSD_D_INNER), lambda i: (i, 0))
    wvec = pl.BlockSpec((1, SSD_D_INNER), lambda i: (0, 0))
    return pl.pallas_call(
        body, name=name, grid=(t // tm,), in_specs=[wide, wide, wide, wvec, wvec, wide], out_specs=[wide, wide, wvec],
        out_shape=[jax.ShapeDtypeStruct((t, SSD_D_INNER), F32), jax.ShapeDtypeStruct((t, SSD_D_INNER), MXU_DTYPE),
                   jax.ShapeDtypeStruct((1, SSD_D_INNER), F32)],
        compiler_params=_cp("arbitrary"),
    )(y, act, proj, d_exp, g.reshape(1, -1), dn)


SB_PAIRS = SB_HEADS // 2


def _kv_rows(j, bt, nt=1):
    return pl.ds(pl.multiple_of(j * bt, bt), nt * bt)


def _sb_tile_masks(bt):
    lane = _iota((bt, bt), 1)
    rowi = _iota((bt, bt), 0)
    return lane < rowi, (rowi >= lane).astype(MXU_DTYPE), (rowi <= lane).astype(MXU_DTYPE)


def _sb_scaled_heads(pair, scale):
    lane = _iota(pair.shape, 1)
    val = pair.astype(F32) * scale
    return [jnp.where(lane < SB_HEAD_DIM, val, 0.0).astype(pair.dtype), jnp.where(lane >= SB_HEAD_DIM, val, 0.0).astype(pair.dtype)]


def _sb_logits(qs, kb, bt, strict):
    nt = kb.shape[0] // bt
    full = [_mm_nt(q_head, kb) for q_head in qs]
    xs, nlfs = [], []
    for x in full:
        nlf = jnp.maximum(x, 0.0) + jnp.log(1.0 + jnp.exp(-jnp.abs(x)))
        xs.append([x[:, tt * bt:(tt + 1) * bt] for tt in range(nt)])
        tiles = [nlf[:, tt * bt:(tt + 1) * bt] for tt in range(nt)]
        if strict is not None:
            tiles[-1] = jnp.where(strict, tiles[-1], 0.0)
        nlfs.append(tiles)
    return xs, nlfs


def _sb_tails(nlf_tiles, from_j):
    tails, run = [None] * len(nlf_tiles), None
    for tt in reversed(range(len(nlf_tiles))):
        tail = _mm_exact_rhs(nlf_tiles[tt], from_j)
        tails[tt] = tail if run is None else tail + run
        run = tails[tt][:, 0:1]
    return tails


def _sb_heads(e_tiles, upto_j, pre):
    sums, run = [], pre
    for e in e_tiles:
        sums.append(_mm_exact_rhs(e, upto_j) + run)
        run = sums[-1][:, e.shape[1] - 1:e.shape[1]]
    return sums


def _carried_specs(carried):
    if carried is None:
        return [], [], [], []
    src, per_peer = carried
    rows = src.shape[1:] if per_peer else src.shape
    anywhere = pl.BlockSpec(memory_space=pl.ANY)
    return [anywhere], [src], [anywhere], [jax.ShapeDtypeStruct((N_DEV, *rows), src.dtype)]


def _carried_hooks(carried, comm_refs, first, pass_on, last):
    if carried is None:
        return lambda: None
    per_peer = carried[1]

    @pl.when(first)
    def _():
        if per_peer:
            _exchange_start(*comm_refs, per_peer=True)
        else:
            _gather_phase("send", *comm_refs)

    if not per_peer:
        @pl.when(pass_on)
        def _():
            _gather_phase("pass_on", *comm_refs)

    def finish():
        @pl.when(last)
        def _():
            if per_peer:
                _exchange_finish(*comm_refs, per_peer=True)
            else:
                _gather_phase("finish", *comm_refs)

    return finish


def _sb_attention_fwd(qkv, name, carried=None):
    t = qkv.shape[0]
    bt = min(SB_TILE, t)
    nq = t // bt
    c_in_specs, c_in, c_out_specs, c_out = _carried_specs(carried)

    def body(*refs):
        if carried is None:
            q_ref, k_ref, v_ref, o_ref, acc_ref = refs
            comm_refs = None
        else:
            q_ref, k_ref, v_ref, src_ref, o_ref, land_ref, acc_ref, send_sems, recv_sems, local_sem = refs
            comm_refs = (src_ref, land_ref, send_sems, recv_sems, local_sem)
        i = pl.program_id(1)
        last_pair = pl.program_id(0) == SB_PAIRS - 1
        finish = _carried_hooks(carried, comm_refs, (pl.program_id(0) == 0) & (i == 0), last_pair & (i == 0),
                                last_pair & (i == nq - 1))
        strict, from_j, _ = _sb_tile_masks(bt)
        qs = _sb_scaled_heads(q_ref[...], SB_SCALE)
        acc_ref[...] = jnp.zeros_like(acc_ref)

        def block(j, nt, carries, diag):
            rows = _kv_rows(j, bt, nt)
            kb, vb = k_ref[rows, :], v_ref[rows, :]
            xs, nlfs = _sb_logits(qs, kb, bt, strict if diag else None)
            tails = [_sb_tails(nlfs[hh], from_j) for hh in range(2)]
            for hh in range(2):
                ws = [jnp.exp(xs[hh][tt] - tails[hh][tt] - carries[hh]) for tt in range(nt)]
                if diag:
                    ws[-1] = jnp.where(strict, ws[-1], 0.0)
                acc_ref[hh] += _mm(jnp.concatenate([w.astype(MXU_DTYPE) for w in ws], axis=1), vb)
            return tuple(carries[hh] + tails[hh][0][:, 0:1] for hh in range(2))

        zero = jnp.zeros((bt, 1), F32)
        carries = block(i, 1, (zero, zero), True)
        carries = lax.fori_loop(0, i // 2, lambda it, cr: block(i - 2 - 2 * it, 2, cr, False), carries)

        @pl.when(i % 2 == 1)
        def _():
            block(0, 1, carries, False)

        low = _iota((bt, 2 * SB_HEAD_DIM), 1) < SB_HEAD_DIM
        o_ref[...] = jnp.where(low, acc_ref[0], acc_ref[1]).astype(o_ref.dtype)
        finish()

    lanes = 2 * SB_HEAD_DIM
    res = pl.pallas_call(
        body, name=name, grid=(SB_PAIRS, nq),
        in_specs=[pl.BlockSpec((bt, lanes), lambda p, i: (i, p)),
                  pl.BlockSpec((t, lanes), lambda p, i: (0, SB_PAIRS + p)),
                  pl.BlockSpec((t, lanes), lambda p, i: (0, 2 * SB_PAIRS + p))] + c_in_specs,
        out_specs=[pl.BlockSpec((bt, lanes), lambda p, i: (i, p))] + c_out_specs,
        out_shape=[jax.ShapeDtypeStruct((t, D_MODEL), MXU_DTYPE)] + c_out,
        scratch_shapes=[pltpu.VMEM((2, bt, lanes), F32)] + (EXCHANGE_SCRATCH if carried else []),
        compiler_params=_cp("arbitrary", "arbitrary"),
    )(qkv, qkv, qkv, *c_in)
    return res[0] if carried is None else res


def _sb_attention_bwd(qkv, do, name, carried=None):
    t = qkv.shape[0]
    bt = min(SB_TILE, t)
    nq = t // bt
    lanes = 2 * SB_HEAD_DIM
    c_in_specs, c_in, c_out_specs, c_out = _carried_specs(carried)

    def body(*refs):
        if carried is None:
            q_ref, k_ref, v_ref, do_ref, dq_ref, dk_ref, dv_ref, sbuf, ebuf, dq_acc, dk_acc, dv_acc = refs
            comm_refs = None
        else:
            (q_ref, k_ref, v_ref, do_ref, src_ref, dq_ref, dk_ref, dv_ref, land_ref,
             sbuf, ebuf, dq_acc, dk_acc, dv_acc, send_sems, recv_sems, local_sem) = refs
            comm_refs = (src_ref, land_ref, send_sems, recv_sems, local_sem)
        i = pl.program_id(1)
        last_pair = pl.program_id(0) == SB_PAIRS - 1
        finish = _carried_hooks(carried, comm_refs, (pl.program_id(0) == 0) & (i == 0), last_pair & (i == 0),
                                last_pair & (i == nq - 1))

        @pl.when(i == 0)
        def _():
            dk_acc[...] = jnp.zeros_like(dk_acc)
            dv_acc[...] = jnp.zeros_like(dv_acc)

        strict, from_j, upto_j = _sb_tile_masks(bt)
        qs = _sb_scaled_heads(q_ref[...], SB_SCALE)
        dos = _sb_scaled_heads(do_ref[...], 1.0)
        q_both = jnp.concatenate(qs, axis=0)
        do_both = jnp.concatenate(dos, axis=0)
        dq_acc[...] = jnp.zeros_like(dq_acc)

        def pass1(j, nt, carries, diag):
            rows = _kv_rows(j, bt, nt)
            kb, vb = k_ref[rows, :], v_ref[rows, :]
            xs, nlfs = _sb_logits(qs, kb, bt, strict if diag else None)
            dws = [_mm_nt(dos[hh], vb) for hh in range(2)]
            tails = [_sb_tails(nlfs[hh], from_j) for hh in range(2)]
            wcat = []
            for hh in range(2):
                ws = [jnp.exp(xs[hh][tt] - tails[hh][tt] - carries[hh]) for tt in range(nt)]
                if diag:
                    ws[-1] = jnp.where(strict, ws[-1], 0.0)
                w_all = jnp.concatenate(ws, axis=1)
                sbuf[hh, :, rows] = jnp.exp(jnp.concatenate([xs[hh][tt] - nlfs[hh][tt] for tt in range(nt)], axis=1))
                ebuf[hh, :, rows] = w_all * dws[hh]
                wcat.append(w_all.astype(MXU_DTYPE))
            dv_acc[rows, :] += _mm_tn(jnp.concatenate(wcat, axis=0), do_both)
            return tuple(carries[hh] + tails[hh][0][:, 0:1] for hh in range(2))

        zero = jnp.zeros((bt, 1), F32)
        carries = pass1(i, 1, (zero, zero), True)
        carries = lax.fori_loop(0, i // 2, lambda it, cr: pass1(i - 2 - 2 * it, 2, cr, False), carries)

        @pl.when(i % 2 == 1)
        def _():
            pass1(0, 1, carries, False)

        def pass2(j, nt, pres, diag):
            rows = _kv_rows(j, bt, nt)
            kb = k_ref[rows, :]
            sums = [_sb_heads([ebuf[hh, :, _kv_rows(j + tt, bt)] for tt in range(nt)], upto_j, pres[hh]) for hh in range(2)]
            dxm = []
            for hh in range(2):
                dxs = [ebuf[hh, :, _kv_rows(j + tt, bt)] - sbuf[hh, :, _kv_rows(j + tt, bt)] * sums[hh][tt] for tt in range(nt)]
                if diag:
                    dxs[-1] = jnp.where(strict, dxs[-1], 0.0)
                dxm.append(jnp.concatenate(dxs, axis=1).astype(MXU_DTYPE))
                dq_acc[hh] += _mm(dxm[hh], kb)
            dk_acc[rows, :] += _mm_tn(jnp.concatenate(dxm, axis=0), q_both)
            return tuple(sums[hh][-1][:, bt - 1:bt] for hh in range(2))

        pres = lax.fori_loop(0, i // 2, lambda it, pr: pass2(2 * it, 2, pr, False), (zero, zero))

        @pl.when(i % 2 == 0)
        def _():
            pass2(i, 1, pres, True)

        @pl.when(i % 2 == 1)
        def _():
            pass2(i - 1, 2, pres, True)

        low = _iota((bt, lanes), 1) < SB_HEAD_DIM
        dq_ref[...] = (jnp.where(low, dq_acc[0], dq_acc[1]) * SB_SCALE).astype(dq_ref.dtype)

        @pl.when(i == nq - 1)
        def _():
            dk_ref[...] = dk_acc[...].astype(dk_ref.dtype)
            dv_ref[...] = dv_acc[...].astype(dv_ref.dtype)

        finish()

    blk = pl.BlockSpec((bt, lanes), lambda p, i: (i, p))
    whole = pl.BlockSpec((t, lanes), lambda p, i: (0, p))
    out = jax.ShapeDtypeStruct((t, D_MODEL), MXU_DTYPE)
    return pl.pallas_call(
        body, name=name, grid=(SB_PAIRS, nq),
        in_specs=[blk, pl.BlockSpec((t, lanes), lambda p, i: (0, SB_PAIRS + p)),
                  pl.BlockSpec((t, lanes), lambda p, i: (0, 2 * SB_PAIRS + p)), blk] + c_in_specs,
        out_specs=[blk, whole, whole] + c_out_specs, out_shape=[out, out, out] + c_out,
        scratch_shapes=[pltpu.VMEM((2, bt, t), F32), pltpu.VMEM((2, bt, t), F32), pltpu.VMEM((2, bt, lanes), F32),
                        pltpu.VMEM((t, lanes), F32), pltpu.VMEM((t, lanes), F32)] + (EXCHANGE_SCRATCH if carried else []),
        compiler_params=_cp("arbitrary", "arbitrary"),
    )(qkv, qkv, qkv, do, *c_in)


def _add_pair(a, b, name):
    s, r, c = a.shape
    tm = _row_tile(r)

    def body(a_ref, b_ref, o_ref):
        o_ref[...] = (a_ref[...].astype(F32) + b_ref[...].astype(F32)).astype(o_ref.dtype)

    blk = pl.BlockSpec((1, tm, c), lambda q, i: (q, i, 0))
    return pl.pallas_call(body, name=name, grid=(s, r // tm), in_specs=[blk, blk], out_specs=blk,
                          out_shape=jax.ShapeDtypeStruct(a.shape, a.dtype), compiler_params=_cp("parallel", "parallel"))(a, b)


def _sum_slots(gslots, name):
    s, r, c = gslots.shape

    def body(g_ref, o_ref):
        g = g_ref[0].astype(F32)
        for q in range(1, s):
            g = g + g_ref[q].astype(F32)
        o_ref[...] = g

    return pl.pallas_call(
        body, name=name, grid=(c // LANES,),
        in_specs=[pl.BlockSpec((s, r, LANES), lambda j: (0, 0, j))], out_specs=pl.BlockSpec((r, LANES), lambda j: (0, j)),
        out_shape=jax.ShapeDtypeStruct((r, c), F32), compiler_params=_cp("parallel"),
    )(gslots)


def _adamw(gslots, w, m, v, name):
    s, r, c = gslots.shape
    tm = _row_tile(r)
    assert w.shape == (r, c), (w.shape, gslots.shape)
    c1 = 1.0 - ADAM_B1 ** ADAM_STEP
    c2 = 1.0 - ADAM_B2 ** ADAM_STEP

    def body(g_ref, w_ref, m_ref, v_ref, go_ref, d_ref, mo_ref, vo_ref):
        g = g_ref[0].astype(F32)
        for q in range(1, s):
            g = g + g_ref[q].astype(F32)
        mn = ADAM_B1 * m_ref[...] + (1.0 - ADAM_B1) * g
        vn = ADAM_B2 * v_ref[...] + (1.0 - ADAM_B2) * (g * g)
        go_ref[...] = g
        mo_ref[...] = mn
        vo_ref[...] = vn
        d_ref[...] = -ADAM_LR * ((mn / c1) / (jnp.sqrt(vn / c2) + ADAM_EPS) + ADAM_WD * w_ref[...])

    row = pl.BlockSpec((tm, c), lambda i: (i, 0))
    out = jax.ShapeDtypeStruct((r, c), F32)
    return pl.pallas_call(
        body, name=name, grid=(r // tm,),
        in_specs=[pl.BlockSpec((s, tm, c), lambda i: (0, i, 0)), row, row, row],
        out_specs=[row, row, row, row], out_shape=[out, out, out, out], compiler_params=_cp("parallel"),
    )(gslots, w, m, v)


def _rows(a):
    flat = a.reshape(-1)
    pad = (-flat.shape[0]) % PACK_W
    if pad:
        flat = jnp.concatenate([flat, jnp.zeros((pad,), flat.dtype)])
    return flat.reshape(-1, PACK_W)


def _pack(arrays, row_multiple):
    parts, layout, off = [], [], 0
    for a in arrays:
        rw = _rows(a)
        parts.append(rw)
        layout.append((off, rw.shape[0], a.shape))
        off += rw.shape[0]
    pad = (-off) % row_multiple
    if pad:
        parts.append(jnp.zeros((pad, PACK_W), parts[0].dtype))
    return jnp.concatenate(parts, axis=0), layout


def _unpack(packed, layout):
    out = []
    for off, nrows, shape in layout:
        n = int(np.prod(shape))
        out.append(packed[off:off + nrows].reshape(-1)[:n].reshape(shape))
    return out


def _shard_as_rows(name, shard):
    if name in COL_SHARDED:
        shard = shard.transpose(0, 2, 1)
    return shard.reshape(-1, PACK_W)


def _rows_as_shard(name, rows, shape):
    if name in COL_SHARDED:
        lead, k, ns = shape
        return rows.reshape(lead, ns, k).transpose(0, 2, 1)
    return rows.reshape(shape)


def _row_tile(r):
    return next(tm for tm in (512, 256, 128, 64, 32, 16, 8) if r % tm == 0)


def _ssd_consts(dt_bias, a_log, d_skip):
    pad = LANES - SSD_HEADS
    bias = jnp.pad(dt_bias, (0, pad)).reshape(1, LANES)
    a_neg = jnp.pad(-jnp.exp(a_log), (0, pad)).reshape(1, LANES)
    d_exp = jnp.repeat(d_skip, SSD_HEAD_DIM).reshape(1, SSD_D_INNER)
    return bias, a_neg, d_exp


def _group_layouts(acum):
    t = acum.shape[0]
    a = acum[:, :SSD_HEADS].reshape(t, SSD_GROUPS, HEADS_PER_GROUP)
    return a.transpose(1, 0, 2), a.transpose(1, 2, 0)


def _ssd_fwd(x, p, carried=None):
    hn = _rmsnorm(x, p["mix_norm"], "rmsnorm_fwd")
    proj = _matmul(hn, p["w_in"], "nt", F32, "ssd_in_fwd", tm=TOKEN_ROWS, tn=896, tk=1024)
    act = _ssd_conv_fwd(proj, p["conv_w"], p["conv_b"], "ssd_conv_fwd")
    bias, a_neg, d_exp = _ssd_consts(p["dt_bias"], p["a_log"], p["d"])
    expand = _head_expand()
    xdt, dt, acum = _ssd_dt_fwd(proj, act, bias, a_neg, expand, "ssd_dt_fwd")
    acum_g, acum_gt = _group_layouts(acum)
    if carried is None:
        (y, states), landed = _ssd_scan_fwd(xdt, act, acum_g, acum_gt, "ssd_scan_fwd"), None
    else:
        y, states, landed = _ssd_scan_fwd(xdt, act, acum_g, acum_gt, "ssd_scan_fwd_carrying_gather", carried)
    yn = _gated_norm_fwd(y, act, proj, d_exp, p["norm"], "ssd_gnorm_fwd")
    x_new = _matmul(yn, p["w_out"], "nn", F32, "ssd_out_fwd", add=x, tm=TOKEN_ROWS, tn=1024, tk=2048)
    saved = dict(x=x, hn=hn, proj=proj, act=act, xdt=xdt, dt=dt, acum_g=acum_g, acum_gt=acum_gt, y=y, states=states, yn=yn)
    return x_new, saved, landed


def _ssd_bwd(dx, p, s, carried_of=None):
    bias, a_neg, d_exp = _ssd_consts(p["dt_bias"], p["a_log"], p["d"])
    expand = _head_expand()
    dyn = _matmul(dx, p["w_out"], "nt", F32, "ssd_out_dgrad", tm=TOKEN_ROWS, tn=1024, tk=1024)
    g_w_out = _matmul(s["yn"], dx, "tn", MXU_DTYPE, "ssd_out_wgrad", tm=1024, tn=1024, tk=TOKEN_ROWS)
    dyy, dz, g_norm = _gated_norm_bwd(s["y"], s["act"], s["proj"], d_exp, p["norm"], dyn, "ssd_gnorm_bwd")
    scan_args = (s["xdt"], s["act"], s["acum_g"], s["acum_gt"], s["states"], dyy)
    if carried_of is None:
        (dxdt, dbm, dcm, dacol, darow), landed = _ssd_scan_bwd(*scan_args, "ssd_scan_bwd"), None
    else:
        dxdt, dbm, dcm, dacol, darow, landed = _ssd_scan_bwd(*scan_args, "ssd_scan_bwd_carrying_grads", carried_of(g_w_out))
    t = dx.shape[0]
    dacum = dacol.transpose(1, 0, 2).reshape(t, SSD_HEADS) + darow.transpose(2, 0, 1).reshape(t, SSD_HEADS)
    dacum = jnp.pad(dacum, ((0, 0), (0, LANES - SSD_HEADS)))
    dxs, draw, g_a, g_bias, g_dexp = _ssd_dt_bwd(s["proj"], s["act"], s["dt"], dxdt, dyy, dacum, bias, a_neg, d_exp,
                                                  expand, expand.T, "ssd_dt_bwd")
    dact = jnp.concatenate([dxs, dbm, dcm], axis=1)
    dxbc, g_conv_w, g_conv_b = _ssd_conv_bwd(s["proj"], p["conv_w"], p["conv_b"], dact, "ssd_conv_bwd")
    dproj = jnp.concatenate([dz, dxbc, draw], axis=1)
    dhn = _matmul(dproj, p["w_in"], "nn", F32, "ssd_in_dgrad", tm=TOKEN_ROWS, tn=1024, tk=896)
    g_w_in = _matmul(dproj, s["hn"], "tn", MXU_DTYPE, "ssd_in_wgrad", tm=896, tn=1024, tk=TOKEN_ROWS)
    dx_new, g_mix = _rmsnorm_bwd(s["x"], p["mix_norm"], dhn, dx, "rmsnorm_bwd")
    grads = dict(w_in=g_w_in[:SSD_IN_DIM], w_out=g_w_out, conv_w=g_conv_w, conv_b=g_conv_b.reshape(-1),
                 dt_bias=g_bias[0, :SSD_HEADS], a_log=(g_a * a_neg)[0, :SSD_HEADS],
                 d=g_dexp.reshape(SSD_HEADS, SSD_HEAD_DIM).sum(axis=1), norm=g_norm.reshape(-1), mix_norm=g_mix.reshape(-1))
    return dx_new, grads, landed


def _sb_fwd(x, p, carried=None):
    hn = _rmsnorm(x, p["mix_norm"], "rmsnorm_fwd")
    qkv = _matmul(hn, p["w_qkv"], "nt", MXU_DTYPE, "sb_qkv_fwd", tm=TOKEN_ROWS, tn=1024, tk=1024)
    if carried is None:
        o, landed = _sb_attention_fwd(qkv, "sb_attn_fwd"), None
    else:
        o, landed = _sb_attention_fwd(qkv, "sb_attn_fwd_carrying_gather", carried)
    x_new = _matmul(o, p["w_out"], "nn", F32, "sb_out_fwd", add=x, tm=TOKEN_ROWS, tn=1024, tk=1024)
    return x_new, dict(x=x, hn=hn, qkv=qkv, o=o), landed


def _sb_bwd(dx, p, s, carried_of=None):
    do = _matmul(dx, p["w_out"], "nt", MXU_DTYPE, "sb_out_dgrad", tm=TOKEN_ROWS, tn=1024, tk=1024)
    g_w_out = _matmul(s["o"], dx, "tn", MXU_DTYPE, "sb_out_wgrad", tm=1024, tn=1024, tk=TOKEN_ROWS)
    if carried_of is None:
        (dq, dk, dv), landed = _sb_attention_bwd(s["qkv"], do, "sb_attn_bwd"), None
    else:
        dq, dk, dv, landed = _sb_attention_bwd(s["qkv"], do, "sb_attn_bwd_carrying_grads", carried_of(g_w_out))
    dqkv = jnp.concatenate([dq, dk, dv], axis=1)
    dhn = _matmul(dqkv, p["w_qkv"], "nn", F32, "sb_qkv_dgrad", tm=TOKEN_ROWS, tn=1024, tk=1024)
    g_w_qkv = _matmul(dqkv, s["hn"], "tn", MXU_DTYPE, "sb_qkv_wgrad", tm=1024, tn=1024, tk=TOKEN_ROWS)
    dx_new, g_mix = _rmsnorm_bwd(s["x"], p["mix_norm"], dhn, dx, "rmsnorm_bwd")
    return dx_new, dict(w_qkv=g_w_qkv, w_out=g_w_out, mix_norm=g_mix.reshape(-1)), landed


def _ffn_fwd(x, p):
    hn = _rmsnorm(x, p["ffn_norm"], "rmsnorm_fwd")
    proj = _matmul(hn, p["w_in"], "nt", F32, "ffn_in_fwd", tm=TOKEN_ROWS, tn=1408, tk=1024)
    act = _ffn_conv_fwd(proj, p["conv_w"], p["conv_b"], "ffn_conv_fwd")
    x_new = _matmul(act, p["w_out"], "nn", F32, "ffn_out_fwd", add=x, tm=TOKEN_ROWS, tn=1024, tk=1408)
    return x_new, dict(x=x, hn=hn, proj=proj, act=act)


def _ffn_bwd(dx, p, s):
    dact = _matmul(dx, p["w_out"], "nt", F32, "ffn_out_dgrad", tm=TOKEN_ROWS, tn=1408, tk=1024)
    g_w_out = _matmul(s["act"], dx, "tn", MXU_DTYPE, "ffn_out_wgrad", tm=1408, tn=1024, tk=TOKEN_ROWS)
    dpg, dpu, dwg, dwu, dbg, dbu = _ffn_conv_bwd(s["proj"], p["conv_w"], p["conv_b"], dact, "ffn_conv_bwd")
    dproj = jnp.concatenate([dpg, dpu], axis=1)
    dhn = _matmul(dproj, p["w_in"], "nn", F32, "ffn_in_dgrad", tm=TOKEN_ROWS, tn=1024, tk=1408)
    g_w_in = _matmul(dproj, s["hn"], "tn", MXU_DTYPE, "ffn_in_wgrad", tm=1408, tn=1024, tk=TOKEN_ROWS)
    dx_new, g_norm = _rmsnorm_bwd(s["x"], p["ffn_norm"], dhn, dx, "rmsnorm_bwd")
    grads = dict(w_in=g_w_in, w_out=g_w_out, conv_w=jnp.concatenate([dwg, dwu], axis=1),
                 conv_b=jnp.concatenate([dbg, dbu], axis=1).reshape(-1), ffn_norm=g_norm.reshape(-1))
    return dx_new, grads


ADD_ROWS = 256
BIG = ["ssd_w_in", "sb_w_qkv", "ffn_w_in", "ssd_w_out", "sb_w_out", "ffn_w_out"]
LAYER_PIECES = [[("ssd_w_in", 0), ("ssd_w_out", 0), ("ffn_w_in", 0), ("ffn_w_out", 0)],
                [("sb_w_qkv", 0), ("sb_w_out", 0), ("ffn_w_in", 1), ("ffn_w_out", 1)],
                [("ssd_w_in", 1), ("ssd_w_out", 1), ("ffn_w_in", 2), ("ffn_w_out", 2)],
                [("sb_w_qkv", 1), ("sb_w_out", 1), ("ffn_w_in", 3), ("ffn_w_out", 3)]]
GRAD_SETS = {3: [("ffn_w_in", 3), ("ffn_w_out", 3), ("sb_w_out", 1)],
             1: [("sb_w_qkv", 1), ("ssd_w_out", 1), ("ffn_w_in", 2), ("ffn_w_out", 2), ("ffn_w_in", 1), ("ffn_w_out", 1),
                 ("sb_w_out", 0), ("ssd_w_in", 1)],
             0: [("sb_w_qkv", 0), ("ffn_w_in", 0), ("ffn_w_out", 0), ("ssd_w_out", 0)],
             "end": [("ssd_w_in", 0)]}
GATHER_SETS = {"early": [("ssd_w_out", 0), ("ssd_w_in", 0)],
               0: [("ffn_w_in", 0), ("ffn_w_out", 0), ("sb_w_qkv", 0), ("sb_w_out", 0)],
               1: [("ffn_w_in", 1), ("ffn_w_out", 1), ("ssd_w_out", 1), ("ffn_w_in", 2), ("ffn_w_out", 2)] + LAYER_PIECES[3]
                  + [("ssd_w_in", 1)]}
COL_SHARDED = {"ssd_w_in": 2, "sb_w_qkv": 2, "ffn_w_in": 4}
CONV = ["ssd_conv_w", "ffn_conv_w"]
SMALL = ["mix_norm", "ffn_norm", "final_norm", "ssd_conv_b", "ssd_dt_bias", "ssd_a_log", "ssd_d", "ssd_norm", "ffn_conv_b"]
WEIGHTS = ["mix_norm", "ffn_norm", "final_norm", "ssd_w_in", "ssd_conv_w", "ssd_conv_b", "ssd_dt_bias", "ssd_a_log", "ssd_d",
           "ssd_norm", "ssd_w_out", "sb_w_qkv", "sb_w_out", "ffn_w_in", "ffn_conv_w", "ffn_conv_b", "ffn_w_out"]


def _step(x, loss_target, w, m, v):
    x = x.reshape(x.shape[-2], x.shape[-1])
    target = loss_target.reshape(x.shape)
    dev = 4 * lax.axis_index("x") + 2 * lax.axis_index("y") + lax.axis_index("c")
    core = lax.axis_index("c")

    shard_rows = {n: _shard_as_rows(n, w[n].astype(MXU_DTYPE)) for n in BIG}
    per_shard = {n: shard_rows[n].shape[0] // w[n].shape[0] for n in BIG}

    def layout(pieces):
        where, off = {}, 0
        for n, l in pieces:
            where[(n, l)] = (off, per_shard[n])
            off += per_shard[n]
        return where

    def pack_pieces(pieces, rows_of):
        return jnp.concatenate([rows_of(piece) for piece in pieces], axis=-2)

    def shard_piece(piece):
        n, l = piece
        return shard_rows[n][l * per_shard[n]:(l + 1) * per_shard[n]]

    full = {}

    def unpack_weights(gathered, where):
        for (n, l), (off, rows) in where.items():
            mat = gathered[:, off:off + rows].reshape(N_DEV * rows, PACK_W)
            if n == "ssd_w_in":
                mat = jnp.pad(mat, ((0, SSD_IN_PAD - SSD_IN_DIM), (0, 0)))
            full[(n, l)] = mat

    unpack_weights(_all_gather(pack_pieces(GATHER_SETS["early"], shard_piece), "gather_weights_early"), layout(GATHER_SETS["early"]))
    conv_pack, conv_layout = _pack([w[n] for n in CONV], 8)
    conv_all = _all_gather(conv_pack, "gather_conv_taps")
    for n, (off, nrows, shape) in zip(CONV, conv_layout):
        parts = [_unpack(conv_all[j], conv_layout)[CONV.index(n)] for j in range(N_DEV)]
        full[n] = jnp.concatenate(parts, axis=-1)

    def ssd_params(j):
        return dict(mix_norm=w["mix_norm"][2 * j], w_in=full[("ssd_w_in", j)], conv_w=full["ssd_conv_w"][j],
                    conv_b=w["ssd_conv_b"][j], dt_bias=w["ssd_dt_bias"][j], a_log=w["ssd_a_log"][j], d=w["ssd_d"][j],
                    norm=w["ssd_norm"][j], w_out=full[("ssd_w_out", j)])

    def sb_params(j):
        return dict(mix_norm=w["mix_norm"][2 * j + 1], w_qkv=full[("sb_w_qkv", j)], w_out=full[("sb_w_out", j)])

    def ffn_params(i):
        return dict(ffn_norm=w["ffn_norm"][i], w_in=full[("ffn_w_in", i)], conv_w=full["ffn_conv_w"][i],
                    conv_b=w["ffn_conv_b"][i], w_out=full[("ffn_w_out", i)])

    saved = []
    for i in range(DEPTH):
        mixer_fwd, params = (_ssd_fwd, ssd_params) if i % 2 == 0 else (_sb_fwd, sb_params)
        if i in GATHER_SETS:
            x, s_mix, arrived = mixer_fwd(x, params(i // 2), carried=(pack_pieces(GATHER_SETS[i], shard_piece), False))
            unpack_weights(arrived, layout(GATHER_SETS[i]))
        else:
            x, s_mix, _ = mixer_fwd(x, params(i // 2))
        x, s_ffn = _ffn_fwd(x, ffn_params(i))
        saved.append((s_mix, s_ffn))
    dx, g_final, loss_part = _final_norm_loss(x, w["final_norm"], target, "final_norm_loss")

    piece_grad = {}

    def grad_piece(piece):
        g = piece_grad[piece]
        return g.reshape(N_DEV, g.shape[0] // N_DEV, PACK_W)

    def carried_set(pieces, own_piece):
        def make(g_w_out):
            piece_grad[own_piece] = g_w_out
            return pack_pieces(pieces, grad_piece), True
        return make

    g_mix, g_ffn, g_ssd, g_sb = [None] * DEPTH, [None] * DEPTH, [None] * 2, [None] * 2
    landed = {}
    for i in reversed(range(DEPTH)):
        s_mix, s_ffn = saved[i]
        dx, g_ffn[i] = _ffn_bwd(dx, ffn_params(i), s_ffn)
        piece_grad[("ffn_w_in", i)], piece_grad[("ffn_w_out", i)] = g_ffn[i]["w_in"], g_ffn[i]["w_out"]
        j = i // 2
        if i % 2 == 0:
            carried_of = carried_set(GRAD_SETS[i], ("ssd_w_out", j)) if i in GRAD_SETS else None
            dx, g_ssd[j], landed[i] = _ssd_bwd(dx, ssd_params(j), s_mix, carried_of)
            g_mix[i] = g_ssd[j]["mix_norm"]
            piece_grad[("ssd_w_in", j)], piece_grad[("ssd_w_out", j)] = g_ssd[j]["w_in"], g_ssd[j]["w_out"]
        else:
            dx, g_sb[j], landed[i] = _sb_bwd(dx, sb_params(j), s_mix, carried_set(GRAD_SETS[i], ("sb_w_out", j)))
            g_mix[i] = g_sb[j]["mix_norm"]
            piece_grad[("sb_w_qkv", j)] = g_sb[j]["w_qkv"]
    grad_x = dx.reshape(1, *dx.shape)

    g8 = pack_pieces(GRAD_SETS["end"], grad_piece)
    g8 = jnp.pad(g8, ((0, 0), (0, (-g8.shape[1]) % ADD_ROWS), (0, 0)))
    g8 = g8.reshape(4, 2, *g8.shape[1:])
    keep = lax.dynamic_index_in_dim(g8, core, axis=1, keepdims=False)
    give = lax.dynamic_index_in_dim(g8, 1 - core, axis=1, keepdims=False)
    got = _swap_with_sibling(give, "grads_to_sibling")
    chip_part = _add_pair(keep, got, "grads_add_sibling")
    landed["end"] = _exchange_chips(chip_part, "grads_across_chips")

    summed = {}
    for key, pieces in GRAD_SETS.items():
        total = _sum_slots(landed[key], "grads_sum_landed")
        for piece, (off, rows) in layout(pieces).items():
            summed[piece] = total[off:off + rows]
    big_res = [dict() for _ in range(4)]
    for n in BIG:
        lead, rows, cols = w[n].shape
        g_rows = jnp.concatenate([summed[(n, l)] for l in range(lead)], axis=0)
        g_nat = _rows_as_shard(n, g_rows, w[n].shape).reshape(1, lead * rows, cols)
        two_d = (lead * rows, cols)
        outs = _adamw(g_nat, w[n].reshape(two_d), m[n].reshape(two_d), v[n].reshape(two_d), "adamw_" + n)
        for kind in range(4):
            big_res[kind][n] = outs[kind].reshape(w[n].shape)

    small_g = {
        "mix_norm": jnp.stack(g_mix), "ffn_norm": jnp.stack([g["ffn_norm"] for g in g_ffn]), "final_norm": g_final.reshape(-1),
        "ssd_conv_b": jnp.stack([g["conv_b"] for g in g_ssd]), "ssd_dt_bias": jnp.stack([g["dt_bias"] for g in g_ssd]),
        "ssd_a_log": jnp.stack([g["a_log"] for g in g_ssd]), "ssd_d": jnp.stack([g["d"] for g in g_ssd]),
        "ssd_norm": jnp.stack([g["norm"] for g in g_ssd]), "ffn_conv_b": jnp.stack([g["conv_b"] for g in g_ffn]),
    }
    conv_g = {"ssd_conv_w": jnp.stack([g["conv_w"] for g in g_ssd]), "ffn_conv_w": jnp.stack([g["conv_w"] for g in g_ffn])}
    extra = [conv_g[n] for n in CONV] + [loss_part]
    small_pack, small_layout = _pack([small_g[n] for n in SMALL] + extra, 8)
    small_all = _all_gather(small_pack, "gather_small_grads")
    zeros_like = [jnp.zeros(a.shape, F32) for a in extra]
    sw, _ = _pack([w[n] for n in SMALL] + zeros_like, 8)
    sm, _ = _pack([m[n] for n in SMALL] + zeros_like, 8)
    sv, _ = _pack([v[n] for n in SMALL] + [jnp.ones(a.shape, F32) for a in extra], 8)
    small_out = _adamw(small_all, sw, sm, sv, "adamw_replicated")
    small_res = [_unpack(o, small_layout) for o in small_out]
    summed = small_res[0]
    loss = summed[-1][0, 0]
    conv_shard_g = []
    for n, gsum in zip(CONV, summed[len(SMALL):len(SMALL) + len(CONV)]):
        ns = w[n].shape[-1]
        conv_shard_g.append(lax.dynamic_slice_in_dim(gsum, dev * ns, ns, axis=2))
    cg, conv_sh_layout = _pack(conv_shard_g, 8)
    cw, _ = _pack([w[n] for n in CONV], 8)
    cm_, _ = _pack([m[n] for n in CONV], 8)
    cv, _ = _pack([v[n] for n in CONV], 8)
    conv_out = _adamw(cg.reshape(1, *cg.shape), cw, cm_, cv, "adamw_conv_taps")
    conv_res = [dict(zip(CONV, _unpack(o, conv_sh_layout))) for o in conv_out]

    def pick(kind, n):
        if n in BIG:
            return big_res[kind][n]
        if n in CONV:
            return conv_res[kind][n]
        return small_res[kind][SMALL.index(n)]

    outs = [loss, grad_x]
    for kind in range(4):
        outs += [pick(kind, n) for n in WEIGHTS]
    return tuple(outs)


def kernel(x, mix_norm, ffn_norm, final_norm, ssd_w_in, ssd_conv_w, ssd_conv_b, ssd_dt_bias, ssd_a_log, ssd_d, ssd_norm, ssd_w_out, sb_w_qkv, sb_w_out, ffn_w_in, ffn_conv_w, ffn_conv_b, ffn_w_out, loss_target, m_mix_norm, m_ffn_norm, m_final_norm, m_ssd_w_in, m_ssd_conv_w, m_ssd_conv_b, m_ssd_dt_bias, m_ssd_a_log, m_ssd_d, m_ssd_norm, m_ssd_w_out, m_sb_w_qkv, m_sb_w_out, m_ffn_w_in, m_ffn_conv_w, m_ffn_conv_b, m_ffn_w_out, v_mix_norm, v_ffn_norm, v_final_norm, v_ssd_w_in, v_ssd_conv_w, v_ssd_conv_b, v_ssd_dt_bias, v_ssd_a_log, v_ssd_d, v_ssd_norm, v_ssd_w_out, v_sb_w_qkv, v_sb_w_out, v_ffn_w_in, v_ffn_conv_w, v_ffn_conv_b, v_ffn_w_out):
    w = dict(mix_norm=mix_norm, ffn_norm=ffn_norm, final_norm=final_norm, ssd_w_in=ssd_w_in, ssd_conv_w=ssd_conv_w,
             ssd_conv_b=ssd_conv_b, ssd_dt_bias=ssd_dt_bias, ssd_a_log=ssd_a_log, ssd_d=ssd_d, ssd_norm=ssd_norm,
             ssd_w_out=ssd_w_out, sb_w_qkv=sb_w_qkv, sb_w_out=sb_w_out, ffn_w_in=ffn_w_in, ffn_conv_w=ffn_conv_w,
             ffn_conv_b=ffn_conv_b, ffn_w_out=ffn_w_out)
    m = dict(mix_norm=m_mix_norm, ffn_norm=m_ffn_norm, final_norm=m_final_norm, ssd_w_in=m_ssd_w_in, ssd_conv_w=m_ssd_conv_w,
             ssd_conv_b=m_ssd_conv_b, ssd_dt_bias=m_ssd_dt_bias, ssd_a_log=m_ssd_a_log, ssd_d=m_ssd_d, ssd_norm=m_ssd_norm,
             ssd_w_out=m_ssd_w_out, sb_w_qkv=m_sb_w_qkv, sb_w_out=m_sb_w_out, ffn_w_in=m_ffn_w_in, ffn_conv_w=m_ffn_conv_w,
             ffn_conv_b=m_ffn_conv_b, ffn_w_out=m_ffn_w_out)
    v = dict(mix_norm=v_mix_norm, ffn_norm=v_ffn_norm, final_norm=v_final_norm, ssd_w_in=v_ssd_w_in, ssd_conv_w=v_ssd_conv_w,
             ssd_conv_b=v_ssd_conv_b, ssd_dt_bias=v_ssd_dt_bias, ssd_a_log=v_ssd_a_log, ssd_d=v_ssd_d, ssd_norm=v_ssd_norm,
             ssd_w_out=v_ssd_w_out, sb_w_qkv=v_sb_w_qkv, sb_w_out=v_sb_w_out, ffn_w_in=v_ffn_w_in, ffn_conv_w=v_ffn_conv_w,
             ffn_conv_b=v_ffn_conv_b, ffn_w_out=v_ffn_w_out)
    return _step(x, loss_target, w, m, v)
```

```python
import jax
import jax.numpy as jnp
import numpy as np
from jax import lax
from jax.experimental import pallas as pl
from jax.experimental.pallas import tpu as pltpu

F32 = jnp.float32
MXU_DTYPE = jnp.bfloat16
MESH_ID = pl.DeviceIdType.MESH
N_DEV = 8

NORM_EPS = 1e-6
D_MODEL = 1024
DEPTH = 4
SSD_D_INNER = 2048
SSD_HEADS = 32
SSD_HEAD_DIM = 64
SSD_GROUPS = 8
SSD_STATE = 128
SSD_CONV = 4
SSD_CHUNK = 128
SSD_CONV_DIM = SSD_D_INNER + 2 * SSD_GROUPS * SSD_STATE
SSD_IN_DIM = SSD_D_INNER + SSD_CONV_DIM + SSD_HEADS
LANES = 128
SSD_IN_PAD = SSD_D_INNER + SSD_CONV_DIM + LANES
SB_HEADS = 16
SB_HEAD_DIM = 64
SB_TILE = 256
SB_SCALE = SB_HEAD_DIM ** -0.5
FFN_D_FF = 2816
FFN_CONV = 3
PACK_W = 1024

ADAM_LR = 0.001
ADAM_B1 = 0.9
ADAM_B2 = 0.999
ADAM_EPS = 1e-08
ADAM_WD = 0.01
ADAM_STEP = 10

VMEM_LIMIT_BYTES = 56 * 1024 * 1024


def _cp(*sem):
    return pltpu.CompilerParams(dimension_semantics=sem, vmem_limit_bytes=VMEM_LIMIT_BYTES)


def _iota(shape, dim):
    return lax.broadcasted_iota(jnp.int32, shape, dim)


def _sigmoid(x):
    return 1.0 / (1.0 + jnp.exp(-x))


def _mm(a, b):
    return lax.dot_general(a, b, (((1,), (0,)), ((), ())), preferred_element_type=F32)


def _mm_nt(a, b):
    return lax.dot_general(a, b, (((1,), (1,)), ((), ())), preferred_element_type=F32)


def _mm_tn(a, b):
    return lax.dot_general(a, b, (((0,), (0,)), ((), ())), preferred_element_type=F32)


def _split(x):
    hi = x.astype(MXU_DTYPE)
    lo = (x - hi.astype(F32)).astype(MXU_DTYPE)
    return hi, lo


def _mm_exact_rhs(x, m):
    hi, lo = _split(x)
    return _mm(jnp.concatenate([hi, lo], axis=1), jnp.concatenate([m, m], axis=0))


def _mm_exact_lhs(m, x):
    hi, lo = _split(x)
    return _mm(jnp.concatenate([m, m], axis=1), jnp.concatenate([hi, lo], axis=0))


def _my_place():
    return lax.axis_index("x"), lax.axis_index("y"), lax.axis_index("c")


def _gather_phase(phase, x_ref, out_ref, send_sems, recv_sems, local_sem):
    x, y, c = _my_place()
    me, sibling = (x, y, c), (x, y, 1 - c)
    chips = [(1 - x, y), (x, 1 - y), (1 - x, 1 - y)]

    def slot(px, py, pc):
        return out_ref.at[4 * px + 2 * py + pc]

    def copy(k, block, to, src=None):
        return pltpu.make_async_remote_copy(
            src_ref=slot(*block) if src is None else src, dst_ref=slot(*block),
            send_sem=send_sems.at[k], recv_sem=recv_sems.at[k], device_id=to, device_id_type=MESH_ID)

    if phase == "send":
        pltpu.make_async_copy(x_ref, slot(*me), local_sem).start()
        copy(0, me, sibling, src=x_ref).start()
        for j, chip in enumerate(chips):
            copy(1 + j, me, (*chip, c), src=x_ref).start()
    elif phase == "pass_on":
        for j, chip in enumerate(chips):
            copy(1 + j, (*chip, c), me).wait_recv()
            copy(4 + j, (*chip, c), sibling).start()
    else:
        copy(0, sibling, me).wait_recv()
        for j, chip in enumerate(chips):
            copy(4 + j, (*chip, 1 - c), me).wait_recv()
        copy(0, me, sibling, src=x_ref).wait_send()
        for j, chip in enumerate(chips):
            copy(1 + j, me, (*chip, c), src=x_ref).wait_send()
            copy(4 + j, (*chip, c), sibling).wait_send()
        pltpu.make_async_copy(x_ref, slot(*me), local_sem).wait()


def _all_gather(shard, name):
    r, c_ = shard.shape

    def body(*refs):
        _gather_phase("send", *refs)
        _gather_phase("pass_on", *refs)
        _gather_phase("finish", *refs)

    return pl.pallas_call(
        body, name=name,
        out_shape=jax.ShapeDtypeStruct((N_DEV, r, c_), shard.dtype),
        in_specs=[pl.BlockSpec(memory_space=pl.ANY)],
        out_specs=pl.BlockSpec(memory_space=pl.ANY),
        scratch_shapes=[pltpu.SemaphoreType.DMA((7,)), pltpu.SemaphoreType.DMA((7,)), pltpu.SemaphoreType.DMA(())],
    )(shard)


def _swap_with_sibling(buf, name):
    def body(x_ref, out_ref, send_sem, recv_sem):
        x, y, c = _my_place()
        cp = pltpu.make_async_remote_copy(src_ref=x_ref, dst_ref=out_ref, send_sem=send_sem, recv_sem=recv_sem,
                                          device_id=(x, y, 1 - c), device_id_type=MESH_ID)
        cp.start()
        cp.wait()

    return pl.pallas_call(
        body, name=name, out_shape=jax.ShapeDtypeStruct(buf.shape, buf.dtype),
        in_specs=[pl.BlockSpec(memory_space=pl.ANY)], out_specs=pl.BlockSpec(memory_space=pl.ANY),
        scratch_shapes=[pltpu.SemaphoreType.DMA(()), pltpu.SemaphoreType.DMA(())],
    )(buf)


def _exchange_chips(parts, name):
    def body(p_ref, out_ref, send_sems, recv_sems, local_sem):
        x, y, c = _my_place()
        my_q = 2 * x + y
        chips = [(1 - x, y), (x, 1 - y), (1 - x, 1 - y)]
        local = pltpu.make_async_copy(p_ref.at[my_q], out_ref.at[my_q], local_sem)
        local.start()

        def copy(k, px, py):
            return pltpu.make_async_remote_copy(
                src_ref=p_ref.at[2 * px + py], dst_ref=out_ref.at[my_q],
                send_sem=send_sems.at[k], recv_sem=recv_sems.at[k], device_id=(px, py, c), device_id_type=MESH_ID)

        def landing(k, px, py):
            return pltpu.make_async_remote_copy(
                src_ref=p_ref.at[my_q], dst_ref=out_ref.at[2 * px + py],
                send_sem=send_sems.at[k], recv_sem=recv_sems.at[k], device_id=(px, py, c), device_id_type=MESH_ID)

        sends = [copy(k, px, py) for k, (px, py) in enumerate(chips)]
        for cp in sends:
            cp.start()
        for k, (px, py) in enumerate(chips):
            landing(k, px, py).wait_recv()
        for cp in sends:
            cp.wait_send()
        local.wait()

    return pl.pallas_call(
        body, name=name, out_shape=jax.ShapeDtypeStruct(parts.shape, parts.dtype),
        in_specs=[pl.BlockSpec(memory_space=pl.ANY)], out_specs=pl.BlockSpec(memory_space=pl.ANY),
        scratch_shapes=[pltpu.SemaphoreType.DMA((3,)), pltpu.SemaphoreType.DMA((3,)), pltpu.SemaphoreType.DMA(())],
    )(parts)


RELATIONS = [(0, 0, 1), (1, 0, 0), (0, 1, 0), (1, 1, 0), (1, 0, 1), (0, 1, 1), (1, 1, 1)]
EXCHANGE_SCRATCH = [pltpu.SemaphoreType.DMA((len(RELATIONS),)), pltpu.SemaphoreType.DMA((len(RELATIONS),)),
                    pltpu.SemaphoreType.DMA(())]


def _exchange_copies(src_ref, land_ref, send_sems, recv_sems, local_sem, per_peer, incoming=True):
    x, y, c = _my_place()
    me = 4 * x + 2 * y + c

    def src(j):
        return src_ref.at[j] if per_peer else src_ref

    local = pltpu.make_async_copy(src(me), land_ref.at[me], local_sem)
    pairs = []
    for k, (dx, dy, dc) in enumerate(RELATIONS):
        peer = (1 - x if dx else x, 1 - y if dy else y, 1 - c if dc else c)
        j = 4 * peer[0] + 2 * peer[1] + peer[2]
        sems = dict(send_sem=send_sems.at[k], recv_sem=recv_sems.at[k], device_id=peer, device_id_type=MESH_ID)
        pairs.append((pltpu.make_async_remote_copy(src_ref=src(j), dst_ref=land_ref.at[me], **sems),
                      pltpu.make_async_remote_copy(src_ref=src(me), dst_ref=land_ref.at[j], **sems) if incoming else None))
    return local, pairs


def _exchange_start(*refs, per_peer):
    local, pairs = _exchange_copies(*refs, per_peer, incoming=False)
    local.start()
    for outgoing, _ in pairs:
        outgoing.start()


def _exchange_finish(*refs, per_peer):
    local, pairs = _exchange_copies(*refs, per_peer)
    for _, incoming in pairs:
        incoming.wait_recv()
    for outgoing, _ in pairs:
        outgoing.wait_send()
    local.wait()


TOKEN_ROWS = 1024
WIDE_ROWS = 2048


def _matmul(a, b, mode, out_dtype, name, add=None, tm=512, tn=512, tk=512):
    if mode == "nn":
        (m, k), (k2, n) = a.shape, b.shape
    elif mode == "nt":
        (m, k), (n, k2) = a.shape, b.shape
    else:
        (k, m), (k2, n) = a.shape, b.shape
    assert k == k2, (a.shape, b.shape, mode)
    tm, tn, tk = min(tm, m), min(tn, n), min(tk, k)
    assert m % tm == 0 and n % tn == 0 and k % tk == 0, (m, n, k, tm, tn, tk)
    nk = k // tk
    mm = {"nn": _mm, "nt": _mm_nt, "tn": _mm_tn}[mode]

    def body(*refs):
        if add is None:
            a_ref, b_ref, o_ref, acc_ref = refs
        else:
            a_ref, b_ref, add_ref, o_ref, acc_ref = refs
        kk = pl.program_id(2)

        @pl.when(kk == 0)
        def _():
            acc_ref[...] = jnp.zeros_like(acc_ref)

        acc_ref[...] += mm(a_ref[...].astype(MXU_DTYPE), b_ref[...].astype(MXU_DTYPE))

        @pl.when(kk == nk - 1)
        def _():
            res = acc_ref[...]
            if add is not None:
                res = res + add_ref[...]
            o_ref[...] = res.astype(o_ref.dtype)

    a_spec = {"nn": pl.BlockSpec((tm, tk), lambda i, j, kk: (i, kk)),
              "nt": pl.BlockSpec((tm, tk), lambda i, j, kk: (i, kk)),
              "tn": pl.BlockSpec((tk, tm), lambda i, j, kk: (kk, i))}[mode]
    b_spec = {"nn": pl.BlockSpec((tk, tn), lambda i, j, kk: (kk, j)),
              "nt": pl.BlockSpec((tn, tk), lambda i, j, kk: (j, kk)),
              "tn": pl.BlockSpec((tk, tn), lambda i, j, kk: (kk, j))}[mode]
    o_spec = pl.BlockSpec((tm, tn), lambda i, j, kk: (i, j))
    in_specs, args = [a_spec, b_spec], [a, b]
    if add is not None:
        in_specs.append(o_spec)
        args.append(add)
    return pl.pallas_call(
        body, name=name, grid=(m // tm, n // tn, nk), in_specs=in_specs, out_specs=o_spec,
        out_shape=jax.ShapeDtypeStruct((m, n), out_dtype),
        scratch_shapes=[pltpu.VMEM((tm, tn), F32)],
        compiler_params=_cp("parallel", "parallel", "arbitrary"),
    )(*args)


def _rmsnorm(x, g, name):
    t, d = x.shape
    tm = min(TOKEN_ROWS, t)

    def body(x_ref, g_ref, o_ref):
        xv = x_ref[...]
        r = lax.rsqrt(jnp.mean(xv * xv, axis=-1, keepdims=True) + NORM_EPS)
        o_ref[...] = (xv * r * g_ref[...]).astype(o_ref.dtype)

    return pl.pallas_call(
        body, name=name, grid=(t // tm,),
        in_specs=[pl.BlockSpec((tm, d), lambda i: (i, 0)), pl.BlockSpec((1, d), lambda i: (0, 0))],
        out_specs=pl.BlockSpec((tm, d), lambda i: (i, 0)),
        out_shape=jax.ShapeDtypeStruct((t, d), MXU_DTYPE), compiler_params=_cp("parallel"),
    )(x, g.reshape(1, d))


def _rmsnorm_bwd(x, g, dh, dres, name):
    t, d = x.shape
    tm = min(TOKEN_ROWS, t)

    def body(x_ref, g_ref, dh_ref, dres_ref, dx_ref, dg_ref):
        @pl.when(pl.program_id(0) == 0)
        def _():
            dg_ref[...] = jnp.zeros_like(dg_ref)

        xv = x_ref[...]
        r = lax.rsqrt(jnp.mean(xv * xv, axis=-1, keepdims=True) + NORM_EPS)
        xn = xv * r
        dhv = dh_ref[...]
        u = dhv * g_ref[...]
        dx_ref[...] = dres_ref[...] + r * (u - xn * jnp.mean(u * xn, axis=-1, keepdims=True))
        dg_ref[...] += jnp.sum(dhv * xn, axis=0, keepdims=True)

    row = pl.BlockSpec((tm, d), lambda i: (i, 0))
    vec = pl.BlockSpec((1, d), lambda i: (0, 0))
    return pl.pallas_call(
        body, name=name, grid=(t // tm,), in_specs=[row, vec, row, row], out_specs=[row, vec],
        out_shape=[jax.ShapeDtypeStruct((t, d), F32), jax.ShapeDtypeStruct((1, d), F32)],
        compiler_params=_cp("arbitrary"),
    )(x, g.reshape(1, d), dh, dres)


def _final_norm_loss(x, g, target, name):
    t, d = x.shape
    tm = min(TOKEN_ROWS, t)

    def body(x_ref, g_ref, t_ref, dx_ref, dg_ref, loss_ref):
        @pl.when(pl.program_id(0) == 0)
        def _():
            dg_ref[...] = jnp.zeros_like(dg_ref)
            loss_ref[...] = jnp.zeros_like(loss_ref)

        xv = x_ref[...]
        gv = g_ref[...]
        r = lax.rsqrt(jnp.mean(xv * xv, axis=-1, keepdims=True) + NORM_EPS)
        xn = xv * r
        err = xn * gv - t_ref[...]
        per_tok = jnp.mean(err * err, axis=-1, keepdims=True)
        loss_ref[...] += jnp.broadcast_to(0.5 * jnp.sum(per_tok, axis=0, keepdims=True), loss_ref.shape)
        dy = err * (1.0 / d)
        u = dy * gv
        dx_ref[...] = r * (u - xn * jnp.mean(u * xn, axis=-1, keepdims=True))
        dg_ref[...] += jnp.sum(dy * xn, axis=0, keepdims=True)

    row = pl.BlockSpec((tm, d), lambda i: (i, 0))
    vec = pl.BlockSpec((1, d), lambda i: (0, 0))
    return pl.pallas_call(
        body, name=name, grid=(t // tm,), in_specs=[row, vec, row],
        out_specs=[row, vec, pl.BlockSpec((1, LANES), lambda i: (0, 0))],
        out_shape=[jax.ShapeDtypeStruct((t, d), F32), jax.ShapeDtypeStruct((1, d), F32),
                   jax.ShapeDtypeStruct((1, LANES), F32)],
        compiler_params=_cp("arbitrary"),
    )(x, g.reshape(1, d), target)


CONV_COLS = 128


def _shifts_down(p, width):
    row = _iota(p.shape, 0)
    return [jnp.where(row >= s, pltpu.roll(p, s, axis=0), 0.0) for s in range(1, width)]


def _shifts_up(p, width):
    n = p.shape[0]
    row = _iota(p.shape, 0)
    return [jnp.where(row < n - s, pltpu.roll(p, n - s, axis=0), 0.0) for s in range(1, width)]


def _conv_pre(p, shifted, w_ref, b_ref):
    width = w_ref.shape[0]
    u = b_ref[...] + w_ref[width - 1:width, :] * p
    for s in range(1, width):
        u = u + w_ref[width - 1 - s:width - s, :] * shifted[s - 1]
    return u


def _conv_transpose(du, w_ref):
    width = w_ref.shape[0]
    shifted = _shifts_up(du, width)
    dp = w_ref[width - 1:width, :] * du
    for s in range(1, width):
        dp = dp + w_ref[width - 1 - s:width - s, :] * shifted[s - 1]
    return dp


def _conv_wgrad(du, p, shifted, dw_ref, db_ref):
    width = dw_ref.shape[0]
    db_ref[...] = jnp.sum(du, axis=0, keepdims=True)
    dw_ref[width - 1:width, :] = jnp.sum(du * p, axis=0, keepdims=True)
    for s in range(1, width):
        dw_ref[width - 1 - s:width - s, :] = jnp.sum(du * shifted[s - 1], axis=0, keepdims=True)


def _ssd_conv_fwd(proj, w, b, name):
    t = proj.shape[0]
    cb = CONV_COLS
    off = SSD_D_INNER // cb

    def body(p_ref, w_ref, b_ref, o_ref):
        p = p_ref[...]
        u = _conv_pre(p, _shifts_down(p, SSD_CONV), w_ref, b_ref)
        o_ref[...] = u * _sigmoid(u)

    return pl.pallas_call(
        body, name=name, grid=(SSD_CONV_DIM // cb,),
        in_specs=[pl.BlockSpec((t, cb), lambda j: (0, j + off)), pl.BlockSpec((SSD_CONV, cb), lambda j: (0, j)),
                  pl.BlockSpec((1, cb), lambda j: (0, j))],
        out_specs=pl.BlockSpec((t, cb), lambda j: (0, j)),
        out_shape=jax.ShapeDtypeStruct((t, SSD_CONV_DIM), F32), compiler_params=_cp("parallel"),
    )(proj, w, b.reshape(1, -1))


def _ssd_conv_bwd(proj, w, b, dact, name):
    t = proj.shape[0]
    cb = CONV_COLS
    off = SSD_D_INNER // cb

    def body(p_ref, w_ref, b_ref, da_ref, dp_ref, dw_ref, db_ref):
        p = p_ref[...]
        shifted = _shifts_down(p, SSD_CONV)
        u = _conv_pre(p, shifted, w_ref, b_ref)
        sg = _sigmoid(u)
        du = da_ref[...] * (sg * (1.0 + u * (1.0 - sg)))
        dp_ref[...] = _conv_transpose(du, w_ref).astype(dp_ref.dtype)
        _conv_wgrad(du, p, shifted, dw_ref, db_ref)

    col = pl.BlockSpec((t, cb), lambda j: (0, j))
    wspec = pl.BlockSpec((SSD_CONV, cb), lambda j: (0, j))
    bspec = pl.BlockSpec((1, cb), lambda j: (0, j))
    return pl.pallas_call(
        body, name=name, grid=(SSD_CONV_DIM // cb,),
        in_specs=[pl.BlockSpec((t, cb), lambda j: (0, j + off)), wspec, bspec, col],
        out_specs=[col, wspec, bspec],
        out_shape=[jax.ShapeDtypeStruct((t, SSD_CONV_DIM), MXU_DTYPE), jax.ShapeDtypeStruct((SSD_CONV, SSD_CONV_DIM), F32),
                   jax.ShapeDtypeStruct((1, SSD_CONV_DIM), F32)],
        compiler_params=_cp("parallel"),
    )(proj, w, b.reshape(1, -1), dact)


def _ffn_conv_fwd(proj, w, b, name):
    t = proj.shape[0]
    cb = CONV_COLS
    nb = FFN_D_FF // cb

    def body(pg_ref, pu_ref, wg_ref, wu_ref, bg_ref, bu_ref, o_ref):
        pg, pu = pg_ref[...], pu_ref[...]
        ug = _conv_pre(pg, _shifts_down(pg, FFN_CONV), wg_ref, bg_ref)
        uu = _conv_pre(pu, _shifts_down(pu, FFN_CONV), wu_ref, bu_ref)
        o_ref[...] = (ug * _sigmoid(ug) * uu).astype(o_ref.dtype)

    gcol = pl.BlockSpec((t, cb), lambda j: (0, j))
    ucol = pl.BlockSpec((t, cb), lambda j: (0, j + nb))
    b2 = b.reshape(1, -1)
    return pl.pallas_call(
        body, name=name, grid=(nb,),
        in_specs=[gcol, ucol, pl.BlockSpec((FFN_CONV, cb), lambda j: (0, j)), pl.BlockSpec((FFN_CONV, cb), lambda j: (0, j + nb)),
                  pl.BlockSpec((1, cb), lambda j: (0, j)), pl.BlockSpec((1, cb), lambda j: (0, j + nb))],
        out_specs=gcol, out_shape=jax.ShapeDtypeStruct((t, FFN_D_FF), MXU_DTYPE), compiler_params=_cp("parallel"),
    )(proj, proj, w, w, b2, b2)


def _ffn_conv_bwd(proj, w, b, dact, name):
    t = proj.shape[0]
    cb = CONV_COLS
    nb = FFN_D_FF // cb

    def body(pg_ref, pu_ref, wg_ref, wu_ref, bg_ref, bu_ref, da_ref,
             dpg_ref, dpu_ref, dwg_ref, dwu_ref, dbg_ref, dbu_ref):
        pg, pu = pg_ref[...], pu_ref[...]
        pg_shifted, pu_shifted = _shifts_down(pg, FFN_CONV), _shifts_down(pu, FFN_CONV)
        ug = _conv_pre(pg, pg_shifted, wg_ref, bg_ref)
        uu = _conv_pre(pu, pu_shifted, wu_ref, bu_ref)
        sg = _sigmoid(ug)
        da = da_ref[...]
        dug = da * uu * (sg * (1.0 + ug * (1.0 - sg)))
        duu = da * (ug * sg)
        dpg_ref[...] = _conv_transpose(dug, wg_ref).astype(dpg_ref.dtype)
        dpu_ref[...] = _conv_transpose(duu, wu_ref).astype(dpu_ref.dtype)
        _conv_wgrad(dug, pg, pg_shifted, dwg_ref, dbg_ref)
        _conv_wgrad(duu, pu, pu_shifted, dwu_ref, dbu_ref)

    gcol = pl.BlockSpec((t, cb), lambda j: (0, j))
    ucol = pl.BlockSpec((t, cb), lambda j: (0, j + nb))
    wg = pl.BlockSpec((FFN_CONV, cb), lambda j: (0, j))
    wu = pl.BlockSpec((FFN_CONV, cb), lambda j: (0, j + nb))
    bg = pl.BlockSpec((1, cb), lambda j: (0, j))
    bu = pl.BlockSpec((1, cb), lambda j: (0, j + nb))
    b2 = b.reshape(1, -1)
    half = jax.ShapeDtypeStruct((t, FFN_D_FF), MXU_DTYPE)
    return pl.pallas_call(
        body, name=name, grid=(nb,),
        in_specs=[gcol, ucol, wg, wu, bg, bu, gcol],
        out_specs=[gcol, gcol, wg, wg, bg, bg],
        out_shape=[half, half, jax.ShapeDtypeStruct((FFN_CONV, FFN_D_FF), F32), jax.ShapeDtypeStruct((FFN_CONV, FFN_D_FF), F32),
                   jax.ShapeDtypeStruct((1, FFN_D_FF), F32), jax.ShapeDtypeStruct((1, FFN_D_FF), F32)],
        compiler_params=_cp("parallel"),
    )(proj, proj, w, w, b2, b2, dact)


SSD_ROWS = 256
DT_COL = (SSD_D_INNER + SSD_CONV_DIM) // LANES


def _head_expand():
    h = np.arange(LANES)[:, None]
    col = np.arange(SSD_D_INNER)[None, :]
    return jnp.asarray((col // SSD_HEAD_DIM == h), MXU_DTYPE)


def _chunk_tri(n, lower):
    t = _iota((n, n), 0)
    s = _iota((n, n), 1)
    shift = SSD_CHUNK.bit_length() - 1
    same = jnp.right_shift(t, shift) == jnp.right_shift(s, shift)
    tri = (s <= t) if lower else (s >= t)
    return jnp.where(same & tri, 1.0, 0.0).astype(MXU_DTYPE)


def _softplus(x):
    return jnp.maximum(x, 0.0) + jnp.log(1.0 + jnp.exp(-jnp.abs(x)))


def _ssd_dt_fwd(proj, act, dt_bias, a_neg, expand, name):
    t = proj.shape[0]
    tm = min(SSD_ROWS, t)

    def body(raw_ref, xs_ref, bias_ref, a_ref, e_ref, xdt_ref, dt_ref, acum_ref):
        lane = _iota((tm, LANES), 1)
        dt = jnp.where(lane < SSD_HEADS, _softplus(raw_ref[...] + bias_ref[...]), 0.0)
        dt_ref[...] = dt
        xdt_ref[...] = xs_ref[...] * _mm_exact_rhs(dt, e_ref[...])
        acum_ref[...] = _mm_exact_lhs(_chunk_tri(tm, True), a_ref[...] * dt)

    vec = pl.BlockSpec((1, LANES), lambda i: (0, 0))
    return pl.pallas_call(
        body, name=name, grid=(t // tm,),
        in_specs=[pl.BlockSpec((tm, LANES), lambda i: (i, DT_COL)), pl.BlockSpec((tm, SSD_D_INNER), lambda i: (i, 0)),
                  vec, vec, pl.BlockSpec((LANES, SSD_D_INNER), lambda i: (0, 0))],
        out_specs=[pl.BlockSpec((tm, SSD_D_INNER), lambda i: (i, 0)), pl.BlockSpec((tm, LANES), lambda i: (i, 0)),
                   pl.BlockSpec((tm, LANES), lambda i: (i, 0))],
        out_shape=[jax.ShapeDtypeStruct((t, SSD_D_INNER), F32), jax.ShapeDtypeStruct((t, LANES), F32),
                   jax.ShapeDtypeStruct((t, LANES), F32)],
        compiler_params=_cp("parallel"),
    )(proj, act, dt_bias, a_neg, expand)


def _ssd_dt_bwd(proj, act, dt, dxdt, dyy, dacum, dt_bias, a_neg, d_exp, expand, expand_t, name):
    t = proj.shape[0]
    tm = min(SSD_ROWS, t)

    def body(raw_ref, xs_ref, dt_ref, dxdt_ref, dyy_ref, dac_ref, bias_ref, a_ref, dsk_ref, e_ref, et_ref,
             dxs_ref, draw_ref, da_ref, dbias_ref, dd_ref):
        @pl.when(pl.program_id(0) == 0)
        def _():
            da_ref[...] = jnp.zeros_like(da_ref)
            dbias_ref[...] = jnp.zeros_like(dbias_ref)
            dd_ref[...] = jnp.zeros_like(dd_ref)

        lane = _iota((tm, LANES), 1)
        xs, dt, dxdt, dyy = xs_ref[...], dt_ref[...], dxdt_ref[...], dyy_ref[...]
        dxs_ref[...] = dxdt * _mm_exact_rhs(dt, e_ref[...]) + dsk_ref[...] * dyy
        dd_ref[...] += jnp.sum(dyy * xs, axis=0, keepdims=True)
        ddt = _mm_exact_rhs(dxdt * xs, et_ref[...])
        da = _mm_exact_lhs(_chunk_tri(tm, False), dac_ref[...])
        ddt = ddt + da * a_ref[...]
        da_ref[...] += jnp.sum(da * dt, axis=0, keepdims=True)
        draw = jnp.where(lane < SSD_HEADS, ddt * _sigmoid(raw_ref[...] + bias_ref[...]), 0.0)
        dbias_ref[...] += jnp.sum(draw, axis=0, keepdims=True)
        draw_ref[...] = draw.astype(draw_ref.dtype)

    wide = pl.BlockSpec((tm, SSD_D_INNER), lambda i: (i, 0))
    thin = pl.BlockSpec((tm, LANES), lambda i: (i, 0))
    vec = pl.BlockSpec((1, LANES), lambda i: (0, 0))
    wvec = pl.BlockSpec((1, SSD_D_INNER), lambda i: (0, 0))
    return pl.pallas_call(
        body, name=name, grid=(t // tm,),
        in_specs=[pl.BlockSpec((tm, LANES), lambda i: (i, DT_COL)), wide, thin, wide, wide, thin, vec, vec, wvec,
                  pl.BlockSpec((LANES, SSD_D_INNER), lambda i: (0, 0)), pl.BlockSpec((SSD_D_INNER, LANES), lambda i: (0, 0))],
        out_specs=[wide, thin, vec, vec, wvec],
        out_shape=[jax.ShapeDtypeStruct((t, SSD_D_INNER), F32), jax.ShapeDtypeStruct((t, LANES), MXU_DTYPE),
                   jax.ShapeDtypeStruct((1, LANES), F32), jax.ShapeDtypeStruct((1, LANES), F32),
                   jax.ShapeDtypeStruct((1, SSD_D_INNER), F32)],
        compiler_params=_cp("arbitrary"),
    )(proj, act, dt, dxdt, dyy, dacum, dt_bias, a_neg, d_exp, expand, expand_t)


SSD_PAIR = 2 * SSD_HEAD_DIM
HEADS_PER_GROUP = SSD_HEADS // SSD_GROUPS
GROUP_COLS = HEADS_PER_GROUP * SSD_HEAD_DIM
B_COL0 = SSD_D_INNER // SSD_STATE
C_COL0 = (SSD_D_INNER + SSD_GROUPS * SSD_STATE) // SSD_STATE


MASKED_LOG = -1e30
SCAN_GROUPS = 4
SCAN_STEPS = SSD_GROUPS // SCAN_GROUPS


def _scan_step_is(g, c):
    return (pl.program_id(0) == g) & (pl.program_id(1) == c)


def _ssd_scan_fwd(xdt, act, acum_g, acum_gt, name, carried=None):
    t = xdt.shape[0]
    nc = t // SSD_CHUNK
    ln = SSD_CHUNK
    c_in_specs, c_in, c_out_specs, c_out = _carried_specs(carried)

    def body(*refs):
        if carried is None:
            x_ref, b_ref, c_ref, ac_ref, act_ref, y_ref, sst_ref, state = refs
            comm_refs = None
        else:
            x_ref, b_ref, c_ref, ac_ref, act_ref, src_ref, y_ref, sst_ref, land_ref, state, send_sems, recv_sems, local_sem = refs
            comm_refs = (src_ref, land_ref, send_sems, recv_sems, local_sem)
        finish = _carried_hooks(carried, comm_refs, _scan_step_is(0, 0), _scan_step_is(SCAN_STEPS - 1, (3 * nc) // 4),
                                _scan_step_is(SCAN_STEPS - 1, nc - 1))

        @pl.when(pl.program_id(1) == 0)
        def _():
            state[...] = jnp.zeros_like(state)

        causal = _iota((ln, ln), 1) <= _iota((ln, ln), 0)
        lo_mask = _iota((ln, SSD_PAIR), 1) < SSD_HEAD_DIM
        lo_rows = _iota((SSD_PAIR, SSD_STATE), 0) < SSD_HEAD_DIM
        for gg in range(SCAN_GROUPS):
            sst_ref[0, gg] = state[gg * GROUP_COLS:(gg + 1) * GROUP_COLS, :]
            bm = b_ref[:, gg * SSD_STATE:(gg + 1) * SSD_STATE].astype(MXU_DTYPE)
            cm = c_ref[:, gg * SSD_STATE:(gg + 1) * SSD_STATE].astype(MXU_DTYPE)
            cb = _mm_nt(cm, bm)
            ac, act_ = ac_ref[gg], act_ref[gg]
            e_last = jnp.exp(ac[ln - 1:ln, :])
            ac_rows = [jnp.broadcast_to(ac[:, h:h + 1], (ln, SSD_PAIR)) for h in range(HEADS_PER_GROUP)]
            for pr in range(2):
                first = gg * GROUP_COLS + pr * SSD_PAIR
                cols = slice(first, first + SSD_PAIR)
                xp = x_ref[:, cols]
                sp = state[cols, :]
                pair_ac = jnp.where(lo_mask, ac_rows[2 * pr], ac_rows[2 * pr + 1])
                ydiag = jnp.zeros((ln, SSD_PAIR), F32)
                for hh in range(2):
                    h = 2 * pr + hh
                    seg = ac_rows[h] - act_[h:h + 1, :]
                    dec = jnp.exp(jnp.where(causal, seg, MASKED_LOG))
                    mask = lo_mask if hh == 0 else jnp.logical_not(lo_mask)
                    ydiag = ydiag + _mm((cb * dec).astype(MXU_DTYPE), jnp.where(mask, xp, 0.0).astype(MXU_DTYPE))
                yoff = _mm_nt(cm, sp.astype(MXU_DTYPE)) * jnp.exp(pair_ac)
                y_ref[:, cols] = ydiag + yoff
                xw = (xp * jnp.exp(pair_ac[ln - 1:ln, :] - pair_ac)).astype(MXU_DTYPE)
                el = jnp.where(lo_rows, e_last[:, 2 * pr:2 * pr + 1], e_last[:, 2 * pr + 1:2 * pr + 2])
                state[cols, :] = sp * el + _mm_tn(xw, bm)
        finish()

    sg = SCAN_GROUPS
    return pl.pallas_call(
        body, name=name, grid=(SCAN_STEPS, nc),
        in_specs=[pl.BlockSpec((ln, sg * GROUP_COLS), lambda g, c: (c, g)),
                  pl.BlockSpec((ln, sg * SSD_STATE), lambda g, c: (c, B_COL0 // sg + g)),
                  pl.BlockSpec((ln, sg * SSD_STATE), lambda g, c: (c, C_COL0 // sg + g)),
                  pl.BlockSpec((sg, ln, HEADS_PER_GROUP), lambda g, c: (g, c, 0)),
                  pl.BlockSpec((sg, HEADS_PER_GROUP, ln), lambda g, c: (g, 0, c))] + c_in_specs,
        out_specs=[pl.BlockSpec((ln, sg * GROUP_COLS), lambda g, c: (c, g)),
                   pl.BlockSpec((1, sg, GROUP_COLS, SSD_STATE), lambda g, c: (c, g, 0, 0))] + c_out_specs,
        out_shape=[jax.ShapeDtypeStruct((t, SSD_D_INNER), F32),
                   jax.ShapeDtypeStruct((nc, SSD_GROUPS, GROUP_COLS, SSD_STATE), F32)] + c_out,
        scratch_shapes=[pltpu.VMEM((sg * GROUP_COLS, SSD_STATE), F32)] + (EXCHANGE_SCRATCH if carried else []),
        compiler_params=_cp("arbitrary", "arbitrary"),
    )(xdt, act, act, acum_g, acum_gt, *c_in)


def _ssd_scan_bwd(xdt, act, acum_g, acum_gt, states, dy, name, carried=None):
    t = xdt.shape[0]
    nc = t // SSD_CHUNK
    ln = SSD_CHUNK
    c_in_specs, c_in, c_out_specs, c_out = _carried_specs(carried)

    def body(*refs):
        if carried is None:
            x_ref, b_ref, c_ref, ac_ref, act_ref, sst_ref, dy_ref, dx_ref, db_ref, dc_ref, dacol_ref, darow_ref, dstate = refs
            comm_refs = None
        else:
            (x_ref, b_ref, c_ref, ac_ref, act_ref, sst_ref, dy_ref, src_ref, dx_ref, db_ref, dc_ref, dacol_ref, darow_ref,
             land_ref, dstate, send_sems, recv_sems, local_sem) = refs
            comm_refs = (src_ref, land_ref, send_sems, recv_sems, local_sem)
        finish = _carried_hooks(carried, comm_refs, _scan_step_is(0, 0), _scan_step_is(SCAN_STEPS - 1, (3 * nc) // 4),
                                _scan_step_is(SCAN_STEPS - 1, nc - 1))

        @pl.when(pl.program_id(1) == 0)
        def _():
            dstate[...] = jnp.zeros_like(dstate)

        for gg in range(SCAN_GROUPS):
            group_bwd(gg, x_ref, b_ref, c_ref, ac_ref, act_ref, sst_ref, dy_ref, dx_ref, db_ref, dc_ref, dacol_ref, darow_ref, dstate)
        finish()

    def group_bwd(gg, x_ref, b_ref, c_ref, ac_ref, act_ref, sst_ref, dy_ref, dx_ref, db_ref, dc_ref, dacol_ref, darow_ref, dstate):
        bc_cols = slice(gg * SSD_STATE, (gg + 1) * SSD_STATE)
        bm = b_ref[:, bc_cols].astype(MXU_DTYPE)
        cm = c_ref[:, bc_cols].astype(MXU_DTYPE)
        cb = _mm_nt(cm, bm)
        ac, act_ = ac_ref[gg], act_ref[gg]
        causal = _iota((ln, ln), 1) <= _iota((ln, ln), 0)
        lo_mask = _iota((ln, SSD_PAIR), 1) < SSD_HEAD_DIM
        lo_rows = _iota((SSD_PAIR, SSD_STATE), 0) < SSD_HEAD_DIM
        lane4 = _iota((ln, HEADS_PER_GROUP), 1)
        sub4 = _iota((HEADS_PER_GROUP, ln), 0)
        is_last = _iota((ln, 1), 0) == ln - 1
        e_last = jnp.exp(ac[ln - 1:ln, :])
        ac_rows = [jnp.broadcast_to(ac[:, h:h + 1], (ln, SSD_PAIR)) for h in range(HEADS_PER_GROUP)]
        dcb = jnp.zeros((ln, ln), F32)
        dc_acc = jnp.zeros((ln, SSD_STATE), F32)
        db_acc = jnp.zeros((ln, SSD_STATE), F32)
        dacol = jnp.zeros((ln, HEADS_PER_GROUP), F32)
        darow = jnp.zeros((HEADS_PER_GROUP, ln), F32)
        for pr in range(2):
            in_group = slice(pr * SSD_PAIR, (pr + 1) * SSD_PAIR)
            cols = slice(gg * GROUP_COLS + pr * SSD_PAIR, gg * GROUP_COLS + (pr + 1) * SSD_PAIR)
            xp = x_ref[:, cols]
            dyp = dy_ref[:, cols]
            sp = sst_ref[0, gg, in_group, :]
            dsp = dstate[cols, :]
            pair_ac = jnp.where(lo_mask, ac_rows[2 * pr], ac_rows[2 * pr + 1])
            ea = jnp.exp(pair_ac)
            w = jnp.exp(pair_ac[ln - 1:ln, :] - pair_ac)
            dye = (dyp * ea).astype(MXU_DTYPE)
            dx_state = w * _mm_nt(bm, dsp.astype(MXU_DTYPE))
            yoff = _mm_nt(cm, sp.astype(MXU_DTYPE)) * ea
            dxp = dx_state
            for hh in range(2):
                h = 2 * pr + hh
                mask = lo_mask if hh == 0 else jnp.logical_not(lo_mask)
                rmask = lo_rows if hh == 0 else jnp.logical_not(lo_rows)
                seg = ac_rows[h] - act_[h:h + 1, :]
                dec = jnp.exp(jnp.where(causal, seg, MASKED_LOG))
                m = cb * dec
                dym = jnp.where(mask, dyp, 0.0).astype(MXU_DTYPE)
                xm = jnp.where(mask, xp, 0.0).astype(MXU_DTYPE)
                g = _mm_nt(dym, xm)
                dxp = dxp + _mm_tn(m.astype(MXU_DTYPE), dym)
                dcb = dcb + dec * g
                mg = m * g
                rs = jnp.sum(mg, axis=1, keepdims=True)
                cs = jnp.sum(mg, axis=0, keepdims=True)
                t_off = jnp.sum(jnp.where(mask, dyp * yoff, 0.0), axis=1, keepdims=True)
                q = jnp.sum(jnp.where(mask, xp * dx_state, 0.0), axis=1, keepdims=True)
                qsum = jnp.sum(q, axis=0, keepdims=True)
                ds_s = jnp.sum(jnp.sum(jnp.where(rmask, dsp * sp, 0.0), axis=1, keepdims=True), axis=0, keepdims=True)
                extra = qsum + e_last[:, h:h + 1] * ds_s
                col = rs + t_off - q + jnp.where(is_last, extra, 0.0)
                dacol = jnp.where(lane4 == h, col, dacol)
                darow = jnp.where(sub4 == h, -cs, darow)
            dx_ref[:, cols] = dxp
            dc_acc = dc_acc + _mm(dye, sp.astype(MXU_DTYPE))
            db_acc = db_acc + _mm((xp * w).astype(MXU_DTYPE), dsp.astype(MXU_DTYPE))
            el = jnp.where(lo_rows, e_last[:, 2 * pr:2 * pr + 1], e_last[:, 2 * pr + 1:2 * pr + 2])
            dstate[cols, :] = dsp * el + _mm_tn(dye, cm)
        dcbm = dcb.astype(MXU_DTYPE)
        dc_ref[:, bc_cols] = _mm(dcbm, bm) + dc_acc
        db_ref[:, bc_cols] = _mm_tn(dcbm, cm) + db_acc
        dacol_ref[gg] = dacol
        darow_ref[gg] = darow

    def rev(c):
        return nc - 1 - c

    sg = SCAN_GROUPS
    grp = pl.BlockSpec((ln, sg * GROUP_COLS), lambda g, c: (rev(c), g))
    return pl.pallas_call(
        body, name=name, grid=(SCAN_STEPS, nc),
        in_specs=[grp,
                  pl.BlockSpec((ln, sg * SSD_STATE), lambda g, c: (rev(c), B_COL0 // sg + g)),
                  pl.BlockSpec((ln, sg * SSD_STATE), lambda g, c: (rev(c), C_COL0 // sg + g)),
                  pl.BlockSpec((sg, ln, HEADS_PER_GROUP), lambda g, c: (g, rev(c), 0)),
                  pl.BlockSpec((sg, HEADS_PER_GROUP, ln), lambda g, c: (g, 0, rev(c))),
                  pl.BlockSpec((1, sg, GROUP_COLS, SSD_STATE), lambda g, c: (rev(c), g, 0, 0)),
                  grp] + c_in_specs,
        out_specs=[grp,
                   pl.BlockSpec((ln, sg * SSD_STATE), lambda g, c: (rev(c), g)),
                   pl.BlockSpec((ln, sg * SSD_STATE), lambda g, c: (rev(c), g)),
                   pl.BlockSpec((sg, ln, HEADS_PER_GROUP), lambda g, c: (g, rev(c), 0)),
                   pl.BlockSpec((sg, HEADS_PER_GROUP, ln), lambda g, c: (g, 0, rev(c)))] + c_out_specs,
        out_shape=[jax.ShapeDtypeStruct((t, SSD_D_INNER), F32),
                   jax.ShapeDtypeStruct((t, SSD_GROUPS * SSD_STATE), F32),
                   jax.ShapeDtypeStruct((t, SSD_GROUPS * SSD_STATE), F32),
                   jax.ShapeDtypeStruct((SSD_GROUPS, t, HEADS_PER_GROUP), F32),
                   jax.ShapeDtypeStruct((SSD_GROUPS, HEADS_PER_GROUP, t), F32)] + c_out,
        scratch_shapes=[pltpu.VMEM((sg * GROUP_COLS, SSD_STATE), F32)] + (EXCHANGE_SCRATCH if carried else []),
        compiler_params=_cp("arbitrary", "arbitrary"),
    )(xdt, act, act, acum_g, acum_gt, states, dy, *c_in)


GN_ROWS = 256


def _gated_norm_parts(y_ref, xs_ref, z_ref, dsk_ref):
    yy = y_ref[...] + dsk_ref[...] * xs_ref[...]
    z = z_ref[...]
    sz = _sigmoid(z)
    silu = z * sz
    u = yy * silu
    r = lax.rsqrt(jnp.mean(u * u, axis=-1, keepdims=True) + NORM_EPS)
    return yy, z, sz, silu, u, r


def _gated_norm_fwd(y, act, proj, d_exp, g, name):
    t = y.shape[0]
    tm = min(GN_ROWS, t)

    def body(y_ref, xs_ref, z_ref, dsk_ref, g_ref, o_ref):
        _, _, _, _, u, r = _gated_norm_parts(y_ref, xs_ref, z_ref, dsk_ref)
        o_ref[...] = (u * r * g_ref[...]).astype(o_ref.dtype)

    wide = pl.BlockSpec((tm, SSD_D_INNER), lambda i: (i, 0))
    wvec = pl.BlockSpec((1, SSD_D_INNER), lambda i: (0, 0))
    return pl.pallas_call(
        body, name=name, grid=(t // tm,), in_specs=[wide, wide, wide, wvec, wvec], out_specs=wide,
        out_shape=jax.ShapeDtypeStruct((t, SSD_D_INNER), MXU_DTYPE), compiler_params=_cp("parallel"),
    )(y, act, proj, d_exp, g.reshape(1, -1))


def _gated_norm_bwd(y, act, proj, d_exp, g, dn, name):
    t = y.shape[0]
    tm = min(GN_ROWS, t)

    def body(y_ref, xs_ref, z_ref, dsk_ref, g_ref, dn_ref, dyy_ref, dz_ref, dg_ref):
        @pl.when(pl.program_id(0) == 0)
        def _():
            dg_ref[...] = jnp.zeros_like(dg_ref)

        yy, z, sz, silu, u, r = _gated_norm_parts(y_ref, xs_ref, z_ref, dsk_ref)
        un = u * r
        dn = dn_ref[...]
        v = dn * g_ref[...]
        du = r * (v - un * jnp.mean(v * un, axis=-1, keepdims=True))
        dg_ref[...] += jnp.sum(dn * un, axis=0, keepdims=True)
        dyy_ref[...] = du * silu
        dz_ref[...] = (du * yy * (sz * (1.0 + z * (1.0 - sz)))).astype(dz_ref.dtype)

    wide = pl.BlockSpec((tm, SSD_D_INNER), lambda i: (i, 0))
    wvec = pl.BlockSpec((1, SSD_D_INNER), lambda i: (0, 0))
    return pl.pallas_call(
        body, name=name, grid=(t // tm,), in_specs=[wide, wide, wide, wvec, wvec, wide], out_specs=[wide, wide, wvec],
        out_shape=[jax.ShapeDtypeStruct((t, SSD_D_INNER), F32), jax.ShapeDtypeStruct((t, SSD_D_INNER), MXU_DTYPE),
                   jax.ShapeDtypeStruct((1, SSD_D_INNER), F32)],
        compiler_params=_cp("arbitrary"),
    )(y, act, proj, d_exp, g.reshape(1, -1), dn)


SB_PAIRS = SB_HEADS // 2


def _kv_rows(j, bt, nt=1):
    return pl.ds(pl.multiple_of(j * bt, bt), nt * bt)


def _sb_tile_masks(bt):
    lane = _iota((bt, bt), 1)
    rowi = _iota((bt, bt), 0)
    return lane < rowi, (rowi >= lane).astype(MXU_DTYPE), (rowi <= lane).astype(MXU_DTYPE)


def _sb_scaled_heads(pair, scale):
    lane = _iota(pair.shape, 1)
    val = pair.astype(F32) * scale
    return [jnp.where(lane < SB_HEAD_DIM, val, 0.0).astype(pair.dtype), jnp.where(lane >= SB_HEAD_DIM, val, 0.0).astype(pair.dtype)]


def _sb_logits(qs, kb, bt, strict):
    nt = kb.shape[0] // bt
    full = [_mm_nt(q_head, kb) for q_head in qs]
    xs, nlfs = [], []
    for x in full:
        nlf = jnp.maximum(x, 0.0) + jnp.log(1.0 + jnp.exp(-jnp.abs(x)))
        xs.append([x[:, tt * bt:(tt + 1) * bt] for tt in range(nt)])
        tiles = [nlf[:, tt * bt:(tt + 1) * bt] for tt in range(nt)]
        if strict is not None:
            tiles[-1] = jnp.where(strict, tiles[-1], 0.0)
        nlfs.append(tiles)
    return xs, nlfs


def _sb_tails(nlf_tiles, from_j):
    tails, run = [None] * len(nlf_tiles), None
    for tt in reversed(range(len(nlf_tiles))):
        tail = _mm_exact_rhs(nlf_tiles[tt], from_j)
        tails[tt] = tail if run is None else tail + run
        run = tails[tt][:, 0:1]
    return tails


def _sb_heads(e_tiles, upto_j, pre):
    sums, run = [], pre
    for e in e_tiles:
        sums.append(_mm_exact_rhs(e, upto_j) + run)
        run = sums[-1][:, e.shape[1] - 1:e.shape[1]]
    return sums


def _carried_specs(carried):
    if carried is None:
        return [], [], [], []
    src, per_peer = carried
    rows = src.shape[1:] if per_peer else src.shape
    anywhere = pl.BlockSpec(memory_space=pl.ANY)
    return [anywhere], [src], [anywhere], [jax.ShapeDtypeStruct((N_DEV, *rows), src.dtype)]


def _carried_hooks(carried, comm_refs, first, pass_on, last):
    if carried is None:
        return lambda: None
    per_peer = carried[1]

    @pl.when(first)
    def _():
        if per_peer:
            _exchange_start(*comm_refs, per_peer=True)
        else:
            _gather_phase("send", *comm_refs)

    if not per_peer:
        @pl.when(pass_on)
        def _():
            _gather_phase("pass_on", *comm_refs)

    def finish():
        @pl.when(last)
        def _():
            if per_peer:
                _exchange_finish(*comm_refs, per_peer=True)
            else:
                _gather_phase("finish", *comm_refs)

    return finish


def _sb_attention_fwd(qkv, name, carried=None):
    t = qkv.shape[0]
    bt = min(SB_TILE, t)
    nq = t // bt
    c_in_specs, c_in, c_out_specs, c_out = _carried_specs(carried)

    def body(*refs):
        if carried is None:
            q_ref, k_ref, v_ref, o_ref, acc_ref = refs
            comm_refs = None
        else:
            q_ref, k_ref, v_ref, src_ref, o_ref, land_ref, acc_ref, send_sems, recv_sems, local_sem = refs
            comm_refs = (src_ref, land_ref, send_sems, recv_sems, local_sem)
        i = pl.program_id(1)
        last_pair = pl.program_id(0) == SB_PAIRS - 1
        finish = _carried_hooks(carried, comm_refs, (pl.program_id(0) == 0) & (i == 0), last_pair & (i == 0),
                                last_pair & (i == nq - 1))
        strict, from_j, _ = _sb_tile_masks(bt)
        qs = _sb_scaled_heads(q_ref[...], SB_SCALE)
        acc_ref[...] = jnp.zeros_like(acc_ref)

        def block(j, nt, carries, diag):
            rows = _kv_rows(j, bt, nt)
            kb, vb = k_ref[rows, :], v_ref[rows, :]
            xs, nlfs = _sb_logits(qs, kb, bt, strict if diag else None)
            tails = [_sb_tails(nlfs[hh], from_j) for hh in range(2)]
            for hh in range(2):
                ws = [jnp.exp(xs[hh][tt] - tails[hh][tt] - carries[hh]) for tt in range(nt)]
                if diag:
                    ws[-1] = jnp.where(strict, ws[-1], 0.0)
                acc_ref[hh] += _mm(jnp.concatenate([w.astype(MXU_DTYPE) for w in ws], axis=1), vb)
            return tuple(carries[hh] + tails[hh][0][:, 0:1] for hh in range(2))

        zero = jnp.zeros((bt, 1), F32)
        carries = block(i, 1, (zero, zero), True)
        carries = lax.fori_loop(0, i // 2, lambda it, cr: block(i - 2 - 2 * it, 2, cr, False), carries)

        @pl.when(i % 2 == 1)
        def _():
            block(0, 1, carries, False)

        low = _iota((bt, 2 * SB_HEAD_DIM), 1) < SB_HEAD_DIM
        o_ref[...] = jnp.where(low, acc_ref[0], acc_ref[1]).astype(o_ref.dtype)
        finish()

    lanes = 2 * SB_HEAD_DIM
    res = pl.pallas_call(
        body, name=name, grid=(SB_PAIRS, nq),
        in_specs=[pl.BlockSpec((bt, lanes), lambda p, i: (i, p)),
                  pl.BlockSpec((t, lanes), lambda p, i: (0, SB_PAIRS + p)),
                  pl.BlockSpec((t, lanes), lambda p, i: (0, 2 * SB_PAIRS + p))] + c_in_specs,
        out_specs=[pl.BlockSpec((bt, lanes), lambda p, i: (i, p))] + c_out_specs,
        out_shape=[jax.ShapeDtypeStruct((t, D_MODEL), MXU_DTYPE)] + c_out,
        scratch_shapes=[pltpu.VMEM((2, bt, lanes), F32)] + (EXCHANGE_SCRATCH if carried else []),
        compiler_params=_cp("arbitrary", "arbitrary"),
    )(qkv, qkv, qkv, *c_in)
    return res[0] if carried is None else res


def _sb_attention_bwd(qkv, do, name, carried=None):
    t = qkv.shape[0]
    bt = min(SB_TILE, t)
    nq = t // bt
    lanes = 2 * SB_HEAD_DIM
    c_in_specs, c_in, c_out_specs, c_out = _carried_specs(carried)

    def body(*refs):
        if carried is None:
            q_ref, k_ref, v_ref, do_ref, dq_ref, dk_ref, dv_ref, sbuf, ebuf, dq_acc, dk_acc, dv_acc = refs
            comm_refs = None
        else:
            (q_ref, k_ref, v_ref, do_ref, src_ref, dq_ref, dk_ref, dv_ref, land_ref,
             sbuf, ebuf, dq_acc, dk_acc, dv_acc, send_sems, recv_sems, local_sem) = refs
            comm_refs = (src_ref, land_ref, send_sems, recv_sems, local_sem)
        i = pl.program_id(1)
        last_pair = pl.program_id(0) == SB_PAIRS - 1
        finish = _carried_hooks(carried, comm_refs, (pl.program_id(0) == 0) & (i == 0), last_pair & (i == 0),
                                last_pair & (i == nq - 1))

        @pl.when(i == 0)
        def _():
            dk_acc[...] = jnp.zeros_like(dk_acc)
            dv_acc[...] = jnp.zeros_like(dv_acc)

        strict, from_j, upto_j = _sb_tile_masks(bt)
        qs = _sb_scaled_heads(q_ref[...], SB_SCALE)
        dos = _sb_scaled_heads(do_ref[...], 1.0)
        q_both = jnp.concatenate(qs, axis=0)
        do_both = jnp.concatenate(dos, axis=0)
        dq_acc[...] = jnp.zeros_like(dq_acc)

        def pass1(j, nt, carries, diag):
            rows = _kv_rows(j, bt, nt)
            kb, vb = k_ref[rows, :], v_ref[rows, :]
            xs, nlfs = _sb_logits(qs, kb, bt, strict if diag else None)
            dws = [_mm_nt(dos[hh], vb) for hh in range(2)]
            tails = [_sb_tails(nlfs[hh], from_j) for hh in range(2)]
            wcat = []
            for hh in range(2):
                ws = [jnp.exp(xs[hh][tt] - tails[hh][tt] - carries[hh]) for tt in range(nt)]
                if diag:
                    ws[-1] = jnp.where(strict, ws[-1], 0.0)
                w_all = jnp.concatenate(ws, axis=1)
                sbuf[hh, :, rows] = jnp.exp(jnp.concatenate([xs[hh][tt] - nlfs[hh][tt] for tt in range(nt)], axis=1))
                ebuf[hh, :, rows] = w_all * dws[hh]
                wcat.append(w_all.astype(MXU_DTYPE))
            dv_acc[rows, :] += _mm_tn(jnp.concatenate(wcat, axis=0), do_both)
            return tuple(carries[hh] + tails[hh][0][:, 0:1] for hh in range(2))

        zero = jnp.zeros((bt, 1), F32)
        carries = pass1(i, 1, (zero, zero), True)
        carries = lax.fori_loop(0, i // 2, lambda it, cr: pass1(i - 2 - 2 * it, 2, cr, False), carries)

        @pl.when(i % 2 == 1)
        def _():
            pass1(0, 1, carries, False)

        def pass2(j, nt, pres, diag):
            rows = _kv_rows(j, bt, nt)
            kb = k_ref[rows, :]
            sums = [_sb_heads([ebuf[hh, :, _kv_rows(j + tt, bt)] for tt in range(nt)], upto_j, pres[hh]) for hh in range(2)]
            dxm = []
            for hh in range(2):
                dxs = [ebuf[hh, :, _kv_rows(j + tt, bt)] - sbuf[hh, :, _kv_rows(j + tt, bt)] * sums[hh][tt] for tt in range(nt)]
                if diag:
                    dxs[-1] = jnp.where(strict, dxs[-1], 0.0)
                dxm.append(jnp.concatenate(dxs, axis=1).astype(MXU_DTYPE))
                dq_acc[hh] += _mm(dxm[hh], kb)
            dk_acc[rows, :] += _mm_tn(jnp.concatenate(dxm, axis=0), q_both)
            return tuple(sums[hh][-1][:, bt - 1:bt] for hh in range(2))

        pres = lax.fori_loop(0, i // 2, lambda it, pr: pass2(2 * it, 2, pr, False), (zero, zero))

        @pl.when(i % 2 == 0)
        def _():
            pass2(i, 1, pres, True)

        @pl.when(i % 2 == 1)
        def _():
            pass2(i - 1, 2, pres, True)

        low = _iota((bt, lanes), 1) < SB_HEAD_DIM
        dq_ref[...] = (jnp.where(low, dq_acc[0], dq_acc[1]) * SB_SCALE).astype(dq_ref.dtype)

        @pl.when(i == nq - 1)
        def _():
            dk_ref[...] = dk_acc[...].astype(dk_ref.dtype)
            dv_ref[...] = dv_acc[...].astype(dv_ref.dtype)

        finish()

    blk = pl.BlockSpec((bt, lanes), lambda p, i: (i, p))
    whole = pl.BlockSpec((t, lanes), lambda p, i: (0, p))
    out = jax.ShapeDtypeStruct((t, D_MODEL), MXU_DTYPE)
    return pl.pallas_call(
        body, name=name, grid=(SB_PAIRS, nq),
        in_specs=[blk, pl.BlockSpec((t, lanes), lambda p, i: (0, SB_PAIRS + p)),
                  pl.BlockSpec((t, lanes), lambda p, i: (0, 2 * SB_PAIRS + p)), blk] + c_in_specs,
        out_specs=[blk, whole, whole] + c_out_specs, out_shape=[out, out, out] + c_out,
        scratch_shapes=[pltpu.VMEM((2, bt, t), F32), pltpu.VMEM((2, bt, t), F32), pltpu.VMEM((2, bt, lanes), F32),
                        pltpu.VMEM((t, lanes), F32), pltpu.VMEM((t, lanes), F32)] + (EXCHANGE_SCRATCH if carried else []),
        compiler_params=_cp("arbitrary", "arbitrary"),
    )(qkv, qkv, qkv, do, *c_in)


def _add_pair(a, b, name):
    s, r, c = a.shape
    tm = _row_tile(r)

    def body(a_ref, b_ref, o_ref):
        o_ref[...] = (a_ref[...].astype(F32) + b_ref[...].astype(F32)).astype(o_ref.dtype)

    blk = pl.BlockSpec((1, tm, c), lambda q, i: (q, i, 0))
    return pl.pallas_call(body, name=name, grid=(s, r // tm), in_specs=[blk, blk], out_specs=blk,
                          out_shape=jax.ShapeDtypeStruct(a.shape, a.dtype), compiler_params=_cp("parallel", "parallel"))(a, b)


def _sum_slots(gslots, name):
    s, r, c = gslots.shape

    def body(g_ref, o_ref):
        g = g_ref[0].astype(F32)
        for q in range(1, s):
            g = g + g_ref[q].astype(F32)
        o_ref[...] = g

    return pl.pallas_call(
        body, name=name, grid=(c // LANES,),
        in_specs=[pl.BlockSpec((s, r, LANES), lambda j: (0, 0, j))], out_specs=pl.BlockSpec((r, LANES), lambda j: (0, j)),
        out_shape=jax.ShapeDtypeStruct((r, c), F32), compiler_params=_cp("parallel"),
    )(gslots)


def _adamw(gslots, w, m, v, name):
    s, r, c = gslots.shape
    tm = _row_tile(r)
    assert w.shape == (r, c), (w.shape, gslots.shape)
    c1 = 1.0 - ADAM_B1 ** ADAM_STEP
    c2 = 1.0 - ADAM_B2 ** ADAM_STEP

    def body(g_ref, w_ref, m_ref, v_ref, go_ref, d_ref, mo_ref, vo_ref):
        g = g_ref[0].astype(F32)
        for q in range(1, s):
            g = g + g_ref[q].astype(F32)
        mn = ADAM_B1 * m_ref[...] + (1.0 - ADAM_B1) * g
        vn = ADAM_B2 * v_ref[...] + (1.0 - ADAM_B2) * (g * g)
        go_ref[...] = g
        mo_ref[...] = mn
        vo_ref[...] = vn
        d_ref[...] = -ADAM_LR * ((mn / c1) / (jnp.sqrt(vn / c2) + ADAM_EPS) + ADAM_WD * w_ref[...])

    row = pl.BlockSpec((tm, c), lambda i: (i, 0))
    out = jax.ShapeDtypeStruct((r, c), F32)
    return pl.pallas_call(
        body, name=name, grid=(r // tm,),
        in_specs=[pl.BlockSpec((s, tm, c), lambda i: (0, i, 0)), row, row, row],
        out_specs=[row, row, row, row], out_shape=[out, out, out, out], compiler_params=_cp("parallel"),
    )(gslots, w, m, v)


def _rows(a):
    flat = a.reshape(-1)
    pad = (-flat.shape[0]) % PACK_W
    if pad:
        flat = jnp.concatenate([flat, jnp.zeros((pad,), flat.dtype)])
    return flat.reshape(-1, PACK_W)


def _pack(arrays, row_multiple):
    parts, layout, off = [], [], 0
    for a in arrays:
        rw = _rows(a)
        parts.append(rw)
        layout.append((off, rw.shape[0], a.shape))
        off += rw.shape[0]
    pad = (-off) % row_multiple
    if pad:
        parts.append(jnp.zeros((pad, PACK_W), parts[0].dtype))
    return jnp.concatenate(parts, axis=0), layout


def _unpack(packed, layout):
    out = []
    for off, nrows, shape in layout:
        n = int(np.prod(shape))
        out.append(packed[off:off + nrows].reshape(-1)[:n].reshape(shape))
    return out


def _shard_as_rows(name, shard):
    if name in COL_SHARDED:
        shard = shard.transpose(0, 2, 1)
    return shard.reshape(-1, PACK_W)


def _rows_as_shard(name, rows, shape):
    if name in COL_SHARDED:
        lead, k, ns = shape
        return rows.reshape(lead, ns, k).transpose(0, 2, 1)
    return rows.reshape(shape)


def _row_tile(r):
    return next(tm for tm in (512, 256, 128, 64, 32, 16, 8) if r % tm == 0)


def _ssd_consts(dt_bias, a_log, d_skip):
    pad = LANES - SSD_HEADS
    bias = jnp.pad(dt_bias, (0, pad)).reshape(1, LANES)
    a_neg = jnp.pad(-jnp.exp(a_log), (0, pad)).reshape(1, LANES)
    d_exp = jnp.repeat(d_skip, SSD_HEAD_DIM).reshape(1, SSD_D_INNER)
    return bias, a_neg, d_exp


def _group_layouts(acum):
    t = acum.shape[0]
    a = acum[:, :SSD_HEADS].reshape(t, SSD_GROUPS, HEADS_PER_GROUP)
    return a.transpose(1, 0, 2), a.transpose(1, 2, 0)


def _ssd_fwd(x, p, carried=None):
    hn = _rmsnorm(x, p["mix_norm"], "rmsnorm_fwd")
    proj = _matmul(hn, p["w_in"], "nt", F32, "ssd_in_fwd", tm=WIDE_ROWS, tn=896, tk=1024)
    act = _ssd_conv_fwd(proj, p["conv_w"], p["conv_b"], "ssd_conv_fwd")
    bias, a_neg, d_exp = _ssd_consts(p["dt_bias"], p["a_log"], p["d"])
    expand = _head_expand()
    xdt, dt, acum = _ssd_dt_fwd(proj, act, bias, a_neg, expand, "ssd_dt_fwd")
    acum_g, acum_gt = _group_layouts(acum)
    if carried is None:
        (y, states), landed = _ssd_scan_fwd(xdt, act, acum_g, acum_gt, "ssd_scan_fwd"), None
    else:
        y, states, landed = _ssd_scan_fwd(xdt, act, acum_g, acum_gt, "ssd_scan_fwd_carrying_gather", carried)
    yn = _gated_norm_fwd(y, act, proj, d_exp, p["norm"], "ssd_gnorm_fwd")
    x_new = _matmul(yn, p["w_out"], "nn", F32, "ssd_out_fwd", add=x, tm=TOKEN_ROWS, tn=1024, tk=2048)
    saved = dict(x=x, hn=hn, proj=proj, act=act, xdt=xdt, dt=dt, acum_g=acum_g, acum_gt=acum_gt, y=y, states=states, yn=yn)
    return x_new, saved, landed


def _ssd_bwd(dx, p, s, carried_of=None):
    bias, a_neg, d_exp = _ssd_consts(p["dt_bias"], p["a_log"], p["d"])
    expand = _head_expand()
    dyn = _matmul(dx, p["w_out"], "nt", F32, "ssd_out_dgrad", tm=WIDE_ROWS, tn=1024, tk=1024)
    g_w_out = _matmul(s["yn"], dx, "tn", MXU_DTYPE, "ssd_out_wgrad", tm=1024, tn=1024, tk=TOKEN_ROWS)
    dyy, dz, g_norm = _gated_norm_bwd(s["y"], s["act"], s["proj"], d_exp, p["norm"], dyn, "ssd_gnorm_bwd")
    scan_args = (s["xdt"], s["act"], s["acum_g"], s["acum_gt"], s["states"], dyy)
    if carried_of is None:
        (dxdt, dbm, dcm, dacol, darow), landed = _ssd_scan_bwd(*scan_args, "ssd_scan_bwd"), None
    else:
        dxdt, dbm, dcm, dacol, darow, landed = _ssd_scan_bwd(*scan_args, "ssd_scan_bwd_carrying_grads", carried_of(g_w_out))
    t = dx.shape[0]
    dacum = dacol.transpose(1, 0, 2).reshape(t, SSD_HEADS) + darow.transpose(2, 0, 1).reshape(t, SSD_HEADS)
    dacum = jnp.pad(dacum, ((0, 0), (0, LANES - SSD_HEADS)))
    dxs, draw, g_a, g_bias, g_dexp = _ssd_dt_bwd(s["proj"], s["act"], s["dt"], dxdt, dyy, dacum, bias, a_neg, d_exp,
                                                  expand, expand.T, "ssd_dt_bwd")
    dact = jnp.concatenate([dxs, dbm, dcm], axis=1)
    dxbc, g_conv_w, g_conv_b = _ssd_conv_bwd(s["proj"], p["conv_w"], p["conv_b"], dact, "ssd_conv_bwd")
    dproj = jnp.concatenate([dz, dxbc, draw], axis=1)
    dhn = _matmul(dproj, p["w_in"], "nn", F32, "ssd_in_dgrad", tm=WIDE_ROWS, tn=1024, tk=896)
    g_w_in = _matmul(dproj, s["hn"], "tn", MXU_DTYPE, "ssd_in_wgrad", tm=896, tn=1024, tk=TOKEN_ROWS)
    dx_new, g_mix = _rmsnorm_bwd(s["x"], p["mix_norm"], dhn, dx, "rmsnorm_bwd")
    grads = dict(w_in=g_w_in[:SSD_IN_DIM], w_out=g_w_out, conv_w=g_conv_w, conv_b=g_conv_b.reshape(-1),
                 dt_bias=g_bias[0, :SSD_HEADS], a_log=(g_a * a_neg)[0, :SSD_HEADS],
                 d=g_dexp.reshape(SSD_HEADS, SSD_HEAD_DIM).sum(axis=1), norm=g_norm.reshape(-1), mix_norm=g_mix.reshape(-1))
    return dx_new, grads, landed


def _sb_fwd(x, p, carried=None):
    hn = _rmsnorm(x, p["mix_norm"], "rmsnorm_fwd")
    qkv = _matmul(hn, p["w_qkv"], "nt", MXU_DTYPE, "sb_qkv_fwd", tm=WIDE_ROWS, tn=1024, tk=1024)
    if carried is None:
        o, landed = _sb_attention_fwd(qkv, "sb_attn_fwd"), None
    else:
        o, landed = _sb_attention_fwd(qkv, "sb_attn_fwd_carrying_gather", carried)
    x_new = _matmul(o, p["w_out"], "nn", F32, "sb_out_fwd", add=x, tm=TOKEN_ROWS, tn=1024, tk=1024)
    return x_new, dict(x=x, hn=hn, qkv=qkv, o=o), landed


def _sb_bwd(dx, p, s, carried_of=None):
    do = _matmul(dx, p["w_out"], "nt", MXU_DTYPE, "sb_out_dgrad", tm=WIDE_ROWS, tn=1024, tk=1024)
    g_w_out = _matmul(s["o"], dx, "tn", MXU_DTYPE, "sb_out_wgrad", tm=1024, tn=1024, tk=TOKEN_ROWS)
    if carried_of is None:
        (dq, dk, dv), landed = _sb_attention_bwd(s["qkv"], do, "sb_attn_bwd"), None
    else:
        dq, dk, dv, landed = _sb_attention_bwd(s["qkv"], do, "sb_attn_bwd_carrying_grads", carried_of(g_w_out))
    dqkv = jnp.concatenate([dq, dk, dv], axis=1)
    dhn = _matmul(dqkv, p["w_qkv"], "nn", F32, "sb_qkv_dgrad", tm=WIDE_ROWS, tn=1024, tk=1024)
    g_w_qkv = _matmul(dqkv, s["hn"], "tn", MXU_DTYPE, "sb_qkv_wgrad", tm=1024, tn=1024, tk=TOKEN_ROWS)
    dx_new, g_mix = _rmsnorm_bwd(s["x"], p["mix_norm"], dhn, dx, "rmsnorm_bwd")
    return dx_new, dict(w_qkv=g_w_qkv, w_out=g_w_out, mix_norm=g_mix.reshape(-1)), landed


def _ffn_fwd(x, p):
    hn = _rmsnorm(x, p["ffn_norm"], "rmsnorm_fwd")
    proj = _matmul(hn, p["w_in"], "nt", F32, "ffn_in_fwd", tm=WIDE_ROWS, tn=1408, tk=1024)
    act = _ffn_conv_fwd(proj, p["conv_w"], p["conv_b"], "ffn_conv_fwd")
    x_new = _matmul(act, p["w_out"], "nn", F32, "ffn_out_fwd", add=x, tm=TOKEN_ROWS, tn=1024, tk=1408)
    return x_new, dict(x=x, hn=hn, proj=proj, act=act)


def _ffn_bwd(dx, p, s):
    dact = _matmul(dx, p["w_out"], "nt", F32, "ffn_out_dgrad", tm=TOKEN_ROWS, tn=1408, tk=1024)
    g_w_out = _matmul(s["act"], dx, "tn", MXU_DTYPE, "ffn_out_wgrad", tm=1408, tn=1024, tk=TOKEN_ROWS)
    dpg, dpu, dwg, dwu, dbg, dbu = _ffn_conv_bwd(s["proj"], p["conv_w"], p["conv_b"], dact, "ffn_conv_bwd")
    dproj = jnp.concatenate([dpg, dpu], axis=1)
    dhn = _matmul(dproj, p["w_in"], "nn", F32, "ffn_in_dgrad", tm=WIDE_ROWS, tn=1024, tk=1408)
    g_w_in = _matmul(dproj, s["hn"], "tn", MXU_DTYPE, "ffn_in_wgrad", tm=1408, tn=1024, tk=TOKEN_ROWS)
    dx_new, g_norm = _rmsnorm_bwd(s["x"], p["ffn_norm"], dhn, dx, "rmsnorm_bwd")
    grads = dict(w_in=g_w_in, w_out=g_w_out, conv_w=jnp.concatenate([dwg, dwu], axis=1),
                 conv_b=jnp.concatenate([dbg, dbu], axis=1).reshape(-1), ffn_norm=g_norm.reshape(-1))
    return dx_new, grads


ADD_ROWS = 256
BIG = ["ssd_w_in", "sb_w_qkv", "ffn_w_in", "ssd_w_out", "sb_w_out", "ffn_w_out"]
LAYER_PIECES = [[("ssd_w_in", 0), ("ssd_w_out", 0), ("ffn_w_in", 0), ("ffn_w_out", 0)],
                [("sb_w_qkv", 0), ("sb_w_out", 0), ("ffn_w_in", 1), ("ffn_w_out", 1)],
                [("ssd_w_in", 1), ("ssd_w_out", 1), ("ffn_w_in", 2), ("ffn_w_out", 2)],
                [("sb_w_qkv", 1), ("sb_w_out", 1), ("ffn_w_in", 3), ("ffn_w_out", 3)]]
GRAD_SETS = {3: [("ffn_w_in", 3), ("ffn_w_out", 3), ("sb_w_out", 1)],
             1: [("sb_w_qkv", 1), ("ssd_w_out", 1), ("ffn_w_in", 2), ("ffn_w_out", 2), ("ffn_w_in", 1), ("ffn_w_out", 1),
                 ("sb_w_out", 0), ("ssd_w_in", 1)],
             0: [("sb_w_qkv", 0), ("ffn_w_in", 0), ("ffn_w_out", 0), ("ssd_w_out", 0)],
             "end": [("ssd_w_in", 0)]}
GATHER_SETS = {"early": [("ssd_w_out", 0), ("ssd_w_in", 0)],
               0: [("ffn_w_in", 0), ("ffn_w_out", 0), ("sb_w_qkv", 0), ("sb_w_out", 0)],
               1: [("ffn_w_in", 1), ("ffn_w_out", 1), ("ssd_w_out", 1), ("ffn_w_in", 2), ("ffn_w_out", 2)] + LAYER_PIECES[3]
                  + [("ssd_w_in", 1)]}
COL_SHARDED = {"ssd_w_in": 2, "sb_w_qkv": 2, "ffn_w_in": 4}
CONV = ["ssd_conv_w", "ffn_conv_w"]
SMALL = ["mix_norm", "ffn_norm", "final_norm", "ssd_conv_b", "ssd_dt_bias", "ssd_a_log", "ssd_d", "ssd_norm", "ffn_conv_b"]
WEIGHTS = ["mix_norm", "ffn_norm", "final_norm", "ssd_w_in", "ssd_conv_w", "ssd_conv_b", "ssd_dt_bias", "ssd_a_log", "ssd_d",
           "ssd_norm", "ssd_w_out", "sb_w_qkv", "sb_w_out", "ffn_w_in", "ffn_conv_w", "ffn_conv_b", "ffn_w_out"]


def _step(x, loss_target, w, m, v):
    x = x.reshape(x.shape[-2], x.shape[-1])
    target = loss_target.reshape(x.shape)
    dev = 4 * lax.axis_index("x") + 2 * lax.axis_index("y") + lax.axis_index("c")
    core = lax.axis_index("c")

    shard_rows = {n: _shard_as_rows(n, w[n].astype(MXU_DTYPE)) for n in BIG}
    per_shard = {n: shard_rows[n].shape[0] // w[n].shape[0] for n in BIG}

    def layout(pieces):
        where, off = {}, 0
        for n, l in pieces:
            where[(n, l)] = (off, per_shard[n])
            off += per_shard[n]
        return where

    def pack_pieces(pieces, rows_of):
        return jnp.concatenate([rows_of(piece) for piece in pieces], axis=-2)

    def shard_piece(piece):
        n, l = piece
        return shard_rows[n][l * per_shard[n]:(l + 1) * per_shard[n]]

    full = {}

    def unpack_weights(gathered, where):
        for (n, l), (off, rows) in where.items():
            mat = gathered[:, off:off + rows].reshape(N_DEV * rows, PACK_W)
            if n == "ssd_w_in":
                mat = jnp.pad(mat, ((0, SSD_IN_PAD - SSD_IN_DIM), (0, 0)))
            full[(n, l)] = mat

    unpack_weights(_all_gather(pack_pieces(GATHER_SETS["early"], shard_piece), "gather_weights_early"), layout(GATHER_SETS["early"]))
    conv_pack, conv_layout = _pack([w[n] for n in CONV], 8)
    conv_all = _all_gather(conv_pack, "gather_conv_taps")
    for n, (off, nrows, shape) in zip(CONV, conv_layout):
        parts = [_unpack(conv_all[j], conv_layout)[CONV.index(n)] for j in range(N_DEV)]
        full[n] = jnp.concatenate(parts, axis=-1)

    def ssd_params(j):
        return dict(mix_norm=w["mix_norm"][2 * j], w_in=full[("ssd_w_in", j)], conv_w=full["ssd_conv_w"][j],
                    conv_b=w["ssd_conv_b"][j], dt_bias=w["ssd_dt_bias"][j], a_log=w["ssd_a_log"][j], d=w["ssd_d"][j],
                    norm=w["ssd_norm"][j], w_out=full[("ssd_w_out", j)])

    def sb_params(j):
        return dict(mix_norm=w["mix_norm"][2 * j + 1], w_qkv=full[("sb_w_qkv", j)], w_out=full[("sb_w_out", j)])

    def ffn_params(i):
        return dict(ffn_norm=w["ffn_norm"][i], w_in=full[("ffn_w_in", i)], conv_w=full["ffn_conv_w"][i],
                    conv_b=w["ffn_conv_b"][i], w_out=full[("ffn_w_out", i)])

    saved = []
    for i in range(DEPTH):
        mixer_fwd, params = (_ssd_fwd, ssd_params) if i % 2 == 0 else (_sb_fwd, sb_params)
        if i in GATHER_SETS:
            x, s_mix, arrived = mixer_fwd(x, params(i // 2), carried=(pack_pieces(GATHER_SETS[i], shard_piece), False))
            unpack_weights(arrived, layout(GATHER_SETS[i]))
        else:
            x, s_mix, _ = mixer_fwd(x, params(i // 2))
        x, s_ffn = _ffn_fwd(x, ffn_params(i))
        saved.append((s_mix, s_ffn))
    dx, g_final, loss_part = _final_norm_loss(x, w["final_norm"], target, "final_norm_loss")

    piece_grad = {}

    def grad_piece(piece):
        g = piece_grad[piece]
        return g.reshape(N_DEV, g.shape[0] // N_DEV, PACK_W)

    def carried_set(pieces, own_piece):
        def make(g_w_out):
            piece_grad[own_piece] = g_w_out
            return pack_pieces(pieces, grad_piece), True
        return make

    g_mix, g_ffn, g_ssd, g_sb = [None] * DEPTH, [None] * DEPTH, [None] * 2, [None] * 2
    landed = {}
    for i in reversed(range(DEPTH)):
        s_mix, s_ffn = saved[i]
        dx, g_ffn[i] = _ffn_bwd(dx, ffn_params(i), s_ffn)
        piece_grad[("ffn_w_in", i)], piece_grad[("ffn_w_out", i)] = g_ffn[i]["w_in"], g_ffn[i]["w_out"]
        j = i // 2
        if i % 2 == 0:
            carried_of = carried_set(GRAD_SETS[i], ("ssd_w_out", j)) if i in GRAD_SETS else None
            dx, g_ssd[j], landed[i] = _ssd_bwd(dx, ssd_params(j), s_mix, carried_of)
            g_mix[i] = g_ssd[j]["mix_norm"]
            piece_grad[("ssd_w_in", j)], piece_grad[("ssd_w_out", j)] = g_ssd[j]["w_in"], g_ssd[j]["w_out"]
        else:
            dx, g_sb[j], landed[i] = _sb_bwd(dx, sb_params(j), s_mix, carried_set(GRAD_SETS[i], ("sb_w_out", j)))
            g_mix[i] = g_sb[j]["mix_norm"]
            piece_grad[("sb_w_qkv", j)] = g_sb[j]["w_qkv"]
    grad_x = dx.reshape(1, *dx.shape)

    g8 = pack_pieces(GRAD_SETS["end"], grad_piece)
    g8 = jnp.pad(g8, ((0, 0), (0, (-g8.shape[1]) % ADD_ROWS), (0, 0)))
    g8 = g8.reshape(4, 2, *g8.shape[1:])
    keep = lax.dynamic_index_in_dim(g8, core, axis=1, keepdims=False)
    give = lax.dynamic_index_in_dim(g8, 1 - core, axis=1, keepdims=False)
    got = _swap_with_sibling(give, "grads_to_sibling")
    chip_part = _add_pair(keep, got, "grads_add_sibling")
    landed["end"] = _exchange_chips(chip_part, "grads_across_chips")

    summed = {}
    for key, pieces in GRAD_SETS.items():
        total = _sum_slots(landed[key], "grads_sum_landed")
        for piece, (off, rows) in layout(pieces).items():
            summed[piece] = total[off:off + rows]
    big_res = [dict() for _ in range(4)]
    for n in BIG:
        lead, rows, cols = w[n].shape
        g_rows = jnp.concatenate([summed[(n, l)] for l in range(lead)], axis=0)
        g_nat = _rows_as_shard(n, g_rows, w[n].shape).reshape(1, lead * rows, cols)
        two_d = (lead * rows, cols)
        outs = _adamw(g_nat, w[n].reshape(two_d), m[n].reshape(two_d), v[n].reshape(two_d), "adamw_" + n)
        for kind in range(4):
            big_res[kind][n] = outs[kind].reshape(w[n].shape)

    small_g = {
        "mix_norm": jnp.stack(g_mix), "ffn_norm": jnp.stack([g["ffn_norm"] for g in g_ffn]), "final_norm": g_final.reshape(-1),
        "ssd_conv_b": jnp.stack([g["conv_b"] for g in g_ssd]), "ssd_dt_bias": jnp.stack([g["dt_bias"] for g in g_ssd]),
        "ssd_a_log": jnp.stack([g["a_log"] for g in g_ssd]), "ssd_d": jnp.stack([g["d"] for g in g_ssd]),
        "ssd_norm": jnp.stack([g["norm"] for g in g_ssd]), "ffn_conv_b": jnp.stack([g["conv_b"] for g in g_ffn]),
    }
    conv_g = {"ssd_conv_w": jnp.stack([g["conv_w"] for g in g_ssd]), "ffn_conv_w": jnp.stack([g["conv_w"] for g in g_ffn])}
    extra = [conv_g[n] for n in CONV] + [loss_part]
    small_pack, small_layout = _pack([small_g[n] for n in SMALL] + extra, 8)
    small_all = _all_gather(small_pack, "gather_small_grads")
    zeros_like = [jnp.zeros(a.shape, F32) for a in extra]
    sw, _ = _pack([w[n] for n in SMALL] + zeros_like, 8)
    sm, _ = _pack([m[n] for n in SMALL] + zeros_like, 8)
    sv, _ = _pack([v[n] for n in SMALL] + [jnp.ones(a.shape, F32) for a in extra], 8)
    small_out = _adamw(small_all, sw, sm, sv, "adamw_replicated")
    small_res = [_unpack(o, small_layout) for o in small_out]
    summed = small_res[0]
    loss = summed[-1][0, 0]
    conv_shard_g = []
    for n, gsum in zip(CONV, summed[len(SMALL):len(SMALL) + len(CONV)]):
        ns = w[n].shape[-1]
        conv_shard_g.append(lax.dynamic_slice_in_dim(gsum, dev * ns, ns, axis=2))
    cg, conv_sh_layout = _pack(conv_shard_g, 8)
    cw, _ = _pack([w[n] for n in CONV], 8)
    cm_, _ = _pack([m[n] for n in CONV], 8)
    cv, _ = _pack([v[n] for n in CONV], 8)
    conv_out = _adamw(cg.reshape(1, *cg.shape), cw, cm_, cv, "adamw_conv_taps")
    conv_res = [dict(zip(CONV, _unpack(o, conv_sh_layout))) for o in conv_out]

    def pick(kind, n):
        if n in BIG:
            return big_res[kind][n]
        if n in CONV:
            return conv_res[kind][n]
        return small_res[kind][SMALL.index(n)]

    outs = [loss, grad_x]
    for kind in range(4):
        outs += [pick(kind, n) for n in WEIGHTS]
    return tuple(outs)


def kernel(x, mix_norm, ffn_norm, final_norm, ssd_w_in, ssd_conv_w, ssd_conv_b, ssd_dt_bias, ssd_a_log, ssd_d, ssd_norm, ssd_w_out, sb_w_qkv, sb_w_out, ffn_w_in, ffn_conv_w, ffn_conv_b, ffn_w_out, loss_target, m_mix_norm, m_ffn_norm, m_final_norm, m_ssd_w_in, m_ssd_conv_w, m_ssd_conv_b, m_ssd_dt_bias, m_ssd_a_log, m_ssd_d, m_ssd_norm, m_ssd_w_out, m_sb_w_qkv, m_sb_w_out, m_ffn_w_in, m_ffn_conv_w, m_ffn_conv_b, m_ffn_w_out, v_mix_norm, v_ffn_norm, v_final_norm, v_ssd_w_in, v_ssd_conv_w, v_ssd_conv_b, v_ssd_dt_bias, v_ssd_a_log, v_ssd_d, v_ssd_norm, v_ssd_w_out, v_sb_w_qkv, v_sb_w_out, v_ffn_w_in, v_ffn_conv_w, v_ffn_conv_b, v_ffn_w_out):
    w = dict(mix_norm=mix_norm, ffn_norm=ffn_norm, final_norm=final_norm, ssd_w_in=ssd_w_in, ssd_conv_w=ssd_conv_w,
             ssd_conv_b=ssd_conv_b, ssd_dt_bias=ssd_dt_bias, ssd_a_log=ssd_a_log, ssd_d=ssd_d, ssd_norm=ssd_norm,
             ssd_w_out=ssd_w_out, sb_w_qkv=sb_w_qkv, sb_w_out=sb_w_out, ffn_w_in=ffn_w_in, ffn_conv_w=ffn_conv_w,
             ffn_conv_b=ffn_conv_b, ffn_w_out=ffn_w_out)
    m = dict(mix_norm=m_mix_norm, ffn_norm=m_ffn_norm, final_norm=m_final_norm, ssd_w_in=m_ssd_w_in, ssd_conv_w=m_ssd_conv_w,
             ssd_conv_b=m_ssd_conv_b, ssd_dt_bias=m_ssd_dt_bias, ssd_a_log=m_ssd_a_log, ssd_d=m_ssd_d, ssd_norm=m_ssd_norm,
             ssd_w_out=m_ssd_w_out, sb_w_qkv=m_sb_w_qkv, sb_w_out=m_sb_w_out, ffn_w_in=m_ffn_w_in, ffn_conv_w=m_ffn_conv_w,
             ffn_conv_b=m_ffn_conv_b, ffn_w_out=m_ffn_w_out)
    v = dict(mix_norm=v_mix_norm, ffn_norm=v_ffn_norm, final_norm=v_final_norm, ssd_w_in=v_ssd_w_in, ssd_conv_w=v_ssd_conv_w,
             ssd_conv_b=v_ssd_conv_b, ssd_dt_bias=v_ssd_dt_bias, ssd_a_log=v_ssd_a_log, ssd_d=v_ssd_d, ssd_norm=v_ssd_norm,
             ssd_w_out=v_ssd_w_out, sb_w_qkv=v_sb_w_qkv, sb_w_out=v_sb_w_out, ffn_w_in=v_ffn_w_in, ffn_conv_w=v_ffn_conv_w,
             ffn_conv_b=v_ffn_conv_b, ffn_w_out=v_ffn_w_out)
    return _step(x, loss_target, w, m, v)
```

```python
import jax
import jax.numpy as jnp
import numpy as np
from jax import lax
from jax.experimental import pallas as pl
from jax.experimental.pallas import tpu as pltpu

F32 = jnp.float32
MXU_DTYPE = jnp.bfloat16
MESH_ID = pl.DeviceIdType.MESH
N_DEV = 8

NORM_EPS = 1e-6
D_MODEL = 1024
DEPTH = 4
SSD_D_INNER = 2048
SSD_HEADS = 32
SSD_HEAD_DIM = 64
SSD_GROUPS = 8
SSD_STATE = 128
SSD_CONV = 4
SSD_CHUNK = 128
SSD_CONV_DIM = SSD_D_INNER + 2 * SSD_GROUPS * SSD_STATE
SSD_IN_DIM = SSD_D_INNER + SSD_CONV_DIM + SSD_HEADS
LANES = 128
SSD_IN_PAD = SSD_D_INNER + SSD_CONV_DIM + LANES
SB_HEADS = 16
SB_HEAD_DIM = 64
SB_TILE = 256
SB_SCALE = SB_HEAD_DIM ** -0.5
FFN_D_FF = 2816
FFN_CONV = 3
PACK_W = 1024

ADAM_LR = 0.001
ADAM_B1 = 0.9
ADAM_B2 = 0.999
ADAM_EPS = 1e-08
ADAM_WD = 0.01
ADAM_STEP = 10

VMEM_LIMIT_BYTES = 56 * 1024 * 1024


def _cp(*sem):
    return pltpu.CompilerParams(dimension_semantics=sem, vmem_limit_bytes=VMEM_LIMIT_BYTES)


def _iota(shape, dim):
    return lax.broadcasted_iota(jnp.int32, shape, dim)


def _sigmoid(x):
    return 1.0 / (1.0 + jnp.exp(-x))


def _mm(a, b):
    return lax.dot_general(a, b, (((1,), (0,)), ((), ())), preferred_element_type=F32)


def _mm_nt(a, b):
    return lax.dot_general(a, b, (((1,), (1,)), ((), ())), preferred_element_type=F32)


def _mm_tn(a, b):
    return lax.dot_general(a, b, (((0,), (0,)), ((), ())), preferred_element_type=F32)


def _split(x):
    hi = x.astype(MXU_DTYPE)
    lo = (x - hi.astype(F32)).astype(MXU_DTYPE)
    return hi, lo


def _mm_exact_rhs(x, m):
    hi, lo = _split(x)
    return _mm(jnp.concatenate([hi, lo], axis=1), jnp.concatenate([m, m], axis=0))


def _mm_exact_lhs(m, x):
    hi, lo = _split(x)
    return _mm(jnp.concatenate([m, m], axis=1), jnp.concatenate([hi, lo], axis=0))


def _my_place():
    return lax.axis_index("x"), lax.axis_index("y"), lax.axis_index("c")


def _gather_phase(phase, x_ref, out_ref, send_sems, recv_sems, local_sem):
    x, y, c = _my_place()
    me, sibling = (x, y, c), (x, y, 1 - c)
    chips = [(1 - x, y), (x, 1 - y), (1 - x, 1 - y)]

    def slot(px, py, pc):
        return out_ref.at[4 * px + 2 * py + pc]

    def copy(k, block, to, src=None):
        return pltpu.make_async_remote_copy(
            src_ref=slot(*block) if src is None else src, dst_ref=slot(*block),
            send_sem=send_sems.at[k], recv_sem=recv_sems.at[k], device_id=to, device_id_type=MESH_ID)

    if phase == "send":
        pltpu.make_async_copy(x_ref, slot(*me), local_sem).start()
        copy(0, me, sibling, src=x_ref).start()
        for j, chip in enumerate(chips):
            copy(1 + j, me, (*chip, c), src=x_ref).start()
    elif phase == "pass_on":
        for j, chip in enumerate(chips):
            copy(1 + j, (*chip, c), me).wait_recv()
            copy(4 + j, (*chip, c), sibling).start()
    else:
        copy(0, sibling, me).wait_recv()
        for j, chip in enumerate(chips):
            copy(4 + j, (*chip, 1 - c), me).wait_recv()
        copy(0, me, sibling, src=x_ref).wait_send()
        for j, chip in enumerate(chips):
            copy(1 + j, me, (*chip, c), src=x_ref).wait_send()
            copy(4 + j, (*chip, c), sibling).wait_send()
        pltpu.make_async_copy(x_ref, slot(*me), local_sem).wait()


def _all_gather(shard, name):
    r, c_ = shard.shape

    def body(*refs):
        _gather_phase("send", *refs)
        _gather_phase("pass_on", *refs)
        _gather_phase("finish", *refs)

    return pl.pallas_call(
        body, name=name,
        out_shape=jax.ShapeDtypeStruct((N_DEV, r, c_), shard.dtype),
        in_specs=[pl.BlockSpec(memory_space=pl.ANY)],
        out_specs=pl.BlockSpec(memory_space=pl.ANY),
        scratch_shapes=[pltpu.SemaphoreType.DMA((7,)), pltpu.SemaphoreType.DMA((7,)), pltpu.SemaphoreType.DMA(())],
    )(shard)


def _swap_with_sibling(buf, name):
    def body(x_ref, out_ref, send_sem, recv_sem):
        x, y, c = _my_place()
        cp = pltpu.make_async_remote_copy(src_ref=x_ref, dst_ref=out_ref, send_sem=send_sem, recv_sem=recv_sem,
                                          device_id=(x, y, 1 - c), device_id_type=MESH_ID)
        cp.start()
        cp.wait()

    return pl.pallas_call(
        body, name=name, out_shape=jax.ShapeDtypeStruct(buf.shape, buf.dtype),
        in_specs=[pl.BlockSpec(memory_space=pl.ANY)], out_specs=pl.BlockSpec(memory_space=pl.ANY),
        scratch_shapes=[pltpu.SemaphoreType.DMA(()), pltpu.SemaphoreType.DMA(())],
    )(buf)


def _exchange_chips(parts, name):
    def body(p_ref, out_ref, send_sems, recv_sems, local_sem):
        x, y, c = _my_place()
        my_q = 2 * x + y
        chips = [(1 - x, y), (x, 1 - y), (1 - x, 1 - y)]
        local = pltpu.make_async_copy(p_ref.at[my_q], out_ref.at[my_q], local_sem)
        local.start()

        def copy(k, px, py):
            return pltpu.make_async_remote_copy(
                src_ref=p_ref.at[2 * px + py], dst_ref=out_ref.at[my_q],
                send_sem=send_sems.at[k], recv_sem=recv_sems.at[k], device_id=(px, py, c), device_id_type=MESH_ID)

        def landing(k, px, py):
            return pltpu.make_async_remote_copy(
                src_ref=p_ref.at[my_q], dst_ref=out_ref.at[2 * px + py],
                send_sem=send_sems.at[k], recv_sem=recv_sems.at[k], device_id=(px, py, c), device_id_type=MESH_ID)

        sends = [copy(k, px, py) for k, (px, py) in enumerate(chips)]
        for cp in sends:
            cp.start()
        for k, (px, py) in enumerate(chips):
            landing(k, px, py).wait_recv()
        for cp in sends:
            cp.wait_send()
        local.wait()

    return pl.pallas_call(
        body, name=name, out_shape=jax.ShapeDtypeStruct(parts.shape, parts.dtype),
        in_specs=[pl.BlockSpec(memory_space=pl.ANY)], out_specs=pl.BlockSpec(memory_space=pl.ANY),
        scratch_shapes=[pltpu.SemaphoreType.DMA((3,)), pltpu.SemaphoreType.DMA((3,)), pltpu.SemaphoreType.DMA(())],
    )(parts)


RELATIONS = [(0, 0, 1), (1, 0, 0), (0, 1, 0), (1, 1, 0), (1, 0, 1), (0, 1, 1), (1, 1, 1)]
EXCHANGE_SCRATCH = [pltpu.SemaphoreType.DMA((len(RELATIONS),)), pltpu.SemaphoreType.DMA((len(RELATIONS),)),
                    pltpu.SemaphoreType.DMA(())]


def _exchange_copies(src_ref, land_ref, send_sems, recv_sems, local_sem, per_peer, incoming=True):
    x, y, c = _my_place()
    me = 4 * x + 2 * y + c

    def src(j):
        return src_ref.at[j] if per_peer else src_ref

    local = pltpu.make_async_copy(src(me), land_ref.at[me], local_sem)
    pairs = []
    for k, (dx, dy, dc) in enumerate(RELATIONS):
        peer = (1 - x if dx else x, 1 - y if dy else y, 1 - c if dc else c)
        j = 4 * peer[0] + 2 * peer[1] + peer[2]
        sems = dict(send_sem=send_sems.at[k], recv_sem=recv_sems.at[k], device_id=peer, device_id_type=MESH_ID)
        pairs.append((pltpu.make_async_remote_copy(src_ref=src(j), dst_ref=land_ref.at[me], **sems),
                      pltpu.make_async_remote_copy(src_ref=src(me), dst_ref=land_ref.at[j], **sems) if incoming else None))
    return local, pairs


def _exchange_start(*refs, per_peer):
    local, pairs = _exchange_copies(*refs, per_peer, incoming=False)
    local.start()
    for outgoing, _ in pairs:
        outgoing.start()


def _exchange_finish(*refs, per_peer):
    local, pairs = _exchange_copies(*refs, per_peer)
    for _, incoming in pairs:
        incoming.wait_recv()
    for outgoing, _ in pairs:
        outgoing.wait_send()
    local.wait()


TOKEN_ROWS = 1024


def _matmul(a, b, mode, out_dtype, name, add=None, tm=512, tn=512, tk=512, carried=None):
    c_in_specs, c_in, c_out_specs, c_out = _carried_specs(carried)
    if mode == "nn":
        (m, k), (k2, n) = a.shape, b.shape
    elif mode == "nt":
        (m, k), (n, k2) = a.shape, b.shape
    else:
        (k, m), (k2, n) = a.shape, b.shape
    assert k == k2, (a.shape, b.shape, mode)
    tm, tn, tk = min(tm, m), min(tn, n), min(tk, k)
    assert m % tm == 0 and n % tn == 0 and k % tk == 0, (m, n, k, tm, tn, tk)
    nk = k // tk
    mm = {"nn": _mm, "nt": _mm_nt, "tn": _mm_tn}[mode]

    def body(*refs):
        n_in = 2 if add is None else 3
        a_ref, b_ref = refs[0], refs[1]
        add_ref = None if add is None else refs[2]
        if carried is None:
            o_ref, acc_ref = refs[n_in:]
            comm_refs = None
        else:
            src_ref, o_ref, land_ref, acc_ref, send_sems, recv_sems, local_sem = refs[n_in:]
            comm_refs = (src_ref, land_ref, send_sems, recv_sems, local_sem)
        kk = pl.program_id(2)
        first = (pl.program_id(0) == 0) & (pl.program_id(1) == 0) & (kk == 0)
        last = (pl.program_id(0) == m // tm - 1) & (pl.program_id(1) == n // tn - 1) & (kk == nk - 1)
        finish = _carried_hooks(carried, comm_refs, first, last, last)

        @pl.when(kk == 0)
        def _():
            acc_ref[...] = jnp.zeros_like(acc_ref)

        acc_ref[...] += mm(a_ref[...].astype(MXU_DTYPE), b_ref[...].astype(MXU_DTYPE))

        @pl.when(kk == nk - 1)
        def _():
            res = acc_ref[...]
            if add is not None:
                res = res + add_ref[...]
            o_ref[...] = res.astype(o_ref.dtype)

        finish()

    a_spec = {"nn": pl.BlockSpec((tm, tk), lambda i, j, kk: (i, kk)),
              "nt": pl.BlockSpec((tm, tk), lambda i, j, kk: (i, kk)),
              "tn": pl.BlockSpec((tk, tm), lambda i, j, kk: (kk, i))}[mode]
    b_spec = {"nn": pl.BlockSpec((tk, tn), lambda i, j, kk: (kk, j)),
              "nt": pl.BlockSpec((tn, tk), lambda i, j, kk: (j, kk)),
              "tn": pl.BlockSpec((tk, tn), lambda i, j, kk: (kk, j))}[mode]
    o_spec = pl.BlockSpec((tm, tn), lambda i, j, kk: (i, j))
    in_specs, args = [a_spec, b_spec], [a, b]
    if add is not None:
        in_specs.append(o_spec)
        args.append(add)
    res = pl.pallas_call(
        body, name=name, grid=(m // tm, n // tn, nk), in_specs=in_specs + c_in_specs, out_specs=[o_spec] + c_out_specs,
        out_shape=[jax.ShapeDtypeStruct((m, n), out_dtype)] + c_out,
        scratch_shapes=[pltpu.VMEM((tm, tn), F32)] + (EXCHANGE_SCRATCH if carried else []),
        compiler_params=_cp("arbitrary", "arbitrary", "arbitrary") if carried else _cp("parallel", "parallel", "arbitrary"),
    )(*args, *c_in)
    return res[0] if carried is None else res


def _rmsnorm(x, g, name):
    t, d = x.shape
    tm = min(TOKEN_ROWS, t)

    def body(x_ref, g_ref, o_ref):
        xv = x_ref[...]
        r = lax.rsqrt(jnp.mean(xv * xv, axis=-1, keepdims=True) + NORM_EPS)
        o_ref[...] = (xv * r * g_ref[...]).astype(o_ref.dtype)

    return pl.pallas_call(
        body, name=name, grid=(t // tm,),
        in_specs=[pl.BlockSpec((tm, d), lambda i: (i, 0)), pl.BlockSpec((1, d), lambda i: (0, 0))],
        out_specs=pl.BlockSpec((tm, d), lambda i: (i, 0)),
        out_shape=jax.ShapeDtypeStruct((t, d), MXU_DTYPE), compiler_params=_cp("parallel"),
    )(x, g.reshape(1, d))


def _rmsnorm_bwd(x, g, dh, dres, name):
    t, d = x.shape
    tm = min(TOKEN_ROWS, t)

    def body(x_ref, g_ref, dh_ref, dres_ref, dx_ref, dg_ref):
        @pl.when(pl.program_id(0) == 0)
        def _():
            dg_ref[...] = jnp.zeros_like(dg_ref)

        xv = x_ref[...]
        r = lax.rsqrt(jnp.mean(xv * xv, axis=-1, keepdims=True) + NORM_EPS)
        xn = xv * r
        dhv = dh_ref[...]
        u = dhv * g_ref[...]
        dx_ref[...] = dres_ref[...] + r * (u - xn * jnp.mean(u * xn, axis=-1, keepdims=True))
        dg_ref[...] += jnp.sum(dhv * xn, axis=0, keepdims=True)

    row = pl.BlockSpec((tm, d), lambda i: (i, 0))
    vec = pl.BlockSpec((1, d), lambda i: (0, 0))
    return pl.pallas_call(
        body, name=name, grid=(t // tm,), in_specs=[row, vec, row, row], out_specs=[row, vec],
        out_shape=[jax.ShapeDtypeStruct((t, d), F32), jax.ShapeDtypeStruct((1, d), F32)],
        compiler_params=_cp("arbitrary"),
    )(x, g.reshape(1, d), dh, dres)


def _final_norm_loss(x, g, target, name):
    t, d = x.shape
    tm = min(TOKEN_ROWS, t)

    def body(x_ref, g_ref, t_ref, dx_ref, dg_ref, loss_ref):
        @pl.when(pl.program_id(0) == 0)
        def _():
            dg_ref[...] = jnp.zeros_like(dg_ref)
            loss_ref[...] = jnp.zeros_like(loss_ref)

        xv = x_ref[...]
        gv = g_ref[...]
        r = lax.rsqrt(jnp.mean(xv * xv, axis=-1, keepdims=True) + NORM_EPS)
        xn = xv * r
        err = xn * gv - t_ref[...]
        per_tok = jnp.mean(err * err, axis=-1, keepdims=True)
        loss_ref[...] += jnp.broadcast_to(0.5 * jnp.sum(per_tok, axis=0, keepdims=True), loss_ref.shape)
        dy = err * (1.0 / d)
        u = dy * gv
        dx_ref[...] = r * (u - xn * jnp.mean(u * xn, axis=-1, keepdims=True))
        dg_ref[...] += jnp.sum(dy * xn, axis=0, keepdims=True)

    row = pl.BlockSpec((tm, d), lambda i: (i, 0))
    vec = pl.BlockSpec((1, d), lambda i: (0, 0))
    return pl.pallas_call(
        body, name=name, grid=(t // tm,), in_specs=[row, vec, row],
        out_specs=[row, vec, pl.BlockSpec((1, LANES), lambda i: (0, 0))],
        out_shape=[jax.ShapeDtypeStruct((t, d), F32), jax.ShapeDtypeStruct((1, d), F32),
                   jax.ShapeDtypeStruct((1, LANES), F32)],
        compiler_params=_cp("arbitrary"),
    )(x, g.reshape(1, d), target)


CONV_COLS = 128


def _shifts_down(p, width):
    row = _iota(p.shape, 0)
    return [jnp.where(row >= s, pltpu.roll(p, s, axis=0), 0.0) for s in range(1, width)]


def _shifts_up(p, width):
    n = p.shape[0]
    row = _iota(p.shape, 0)
    return [jnp.where(row < n - s, pltpu.roll(p, n - s, axis=0), 0.0) for s in range(1, width)]


def _conv_pre(p, shifted, w_ref, b_ref):
    width = w_ref.shape[0]
    u = b_ref[...] + w_ref[width - 1:width, :] * p
    for s in range(1, width):
        u = u + w_ref[width - 1 - s:width - s, :] * shifted[s - 1]
    return u


def _conv_transpose(du, w_ref):
    width = w_ref.shape[0]
    shifted = _shifts_up(du, width)
    dp = w_ref[width - 1:width, :] * du
    for s in range(1, width):
        dp = dp + w_ref[width - 1 - s:width - s, :] * shifted[s - 1]
    return dp


def _conv_wgrad(du, p, shifted, dw_ref, db_ref):
    width = dw_ref.shape[0]
    db_ref[...] = jnp.sum(du, axis=0, keepdims=True)
    dw_ref[width - 1:width, :] = jnp.sum(du * p, axis=0, keepdims=True)
    for s in range(1, width):
        dw_ref[width - 1 - s:width - s, :] = jnp.sum(du * shifted[s - 1], axis=0, keepdims=True)


def _ssd_conv_fwd(proj, w, b, name):
    t = proj.shape[0]
    cb = CONV_COLS
    off = SSD_D_INNER // cb

    def body(p_ref, w_ref, b_ref, o_ref):
        p = p_ref[...]
        u = _conv_pre(p, _shifts_down(p, SSD_CONV), w_ref, b_ref)
        o_ref[...] = u * _sigmoid(u)

    return pl.pallas_call(
        body, name=name, grid=(SSD_CONV_DIM // cb,),
        in_specs=[pl.BlockSpec((t, cb), lambda j: (0, j + off)), pl.BlockSpec((SSD_CONV, cb), lambda j: (0, j)),
                  pl.BlockSpec((1, cb), lambda j: (0, j))],
        out_specs=pl.BlockSpec((t, cb), lambda j: (0, j)),
        out_shape=jax.ShapeDtypeStruct((t, SSD_CONV_DIM), F32), compiler_params=_cp("parallel"),
    )(proj, w, b.reshape(1, -1))


def _ssd_conv_bwd(proj, w, b, dact, name):
    t = proj.shape[0]
    cb = CONV_COLS
    off = SSD_D_INNER // cb

    def body(p_ref, w_ref, b_ref, da_ref, dp_ref, dw_ref, db_ref):
        p = p_ref[...]
        shifted = _shifts_down(p, SSD_CONV)
        u = _conv_pre(p, shifted, w_ref, b_ref)
        sg = _sigmoid(u)
        du = da_ref[...] * (sg * (1.0 + u * (1.0 - sg)))
        dp_ref[...] = _conv_transpose(du, w_ref).astype(dp_ref.dtype)
        _conv_wgrad(du, p, shifted, dw_ref, db_ref)

    col = pl.BlockSpec((t, cb), lambda j: (0, j))
    wspec = pl.BlockSpec((SSD_CONV, cb), lambda j: (0, j))
    bspec = pl.BlockSpec((1, cb), lambda j: (0, j))
    return pl.pallas_call(
        body, name=name, grid=(SSD_CONV_DIM // cb,),
        in_specs=[pl.BlockSpec((t, cb), lambda j: (0, j + off)), wspec, bspec, col],
        out_specs=[col, wspec, bspec],
        out_shape=[jax.ShapeDtypeStruct((t, SSD_CONV_DIM), MXU_DTYPE), jax.ShapeDtypeStruct((SSD_CONV, SSD_CONV_DIM), F32),
                   jax.ShapeDtypeStruct((1, SSD_CONV_DIM), F32)],
        compiler_params=_cp("parallel"),
    )(proj, w, b.reshape(1, -1), dact)


def _ffn_conv_fwd(proj, w, b, name):
    t = proj.shape[0]
    cb = CONV_COLS
    nb = FFN_D_FF // cb

    def body(pg_ref, pu_ref, wg_ref, wu_ref, bg_ref, bu_ref, o_ref):
        pg, pu = pg_ref[...], pu_ref[...]
        ug = _conv_pre(pg, _shifts_down(pg, FFN_CONV), wg_ref, bg_ref)
        uu = _conv_pre(pu, _shifts_down(pu, FFN_CONV), wu_ref, bu_ref)
        o_ref[...] = (ug * _sigmoid(ug) * uu).astype(o_ref.dtype)

    gcol = pl.BlockSpec((t, cb), lambda j: (0, j))
    ucol = pl.BlockSpec((t, cb), lambda j: (0, j + nb))
    b2 = b.reshape(1, -1)
    return pl.pallas_call(
        body, name=name, grid=(nb,),
        in_specs=[gcol, ucol, pl.BlockSpec((FFN_CONV, cb), lambda j: (0, j)), pl.BlockSpec((FFN_CONV, cb), lambda j: (0, j + nb)),
                  pl.BlockSpec((1, cb), lambda j: (0, j)), pl.BlockSpec((1, cb), lambda j: (0, j + nb))],
        out_specs=gcol, out_shape=jax.ShapeDtypeStruct((t, FFN_D_FF), MXU_DTYPE), compiler_params=_cp("parallel"),
    )(proj, proj, w, w, b2, b2)


def _ffn_conv_bwd(proj, w, b, dact, name):
    t = proj.shape[0]
    cb = CONV_COLS
    nb = FFN_D_FF // cb

    def body(pg_ref, pu_ref, wg_ref, wu_ref, bg_ref, bu_ref, da_ref,
             dpg_ref, dpu_ref, dwg_ref, dwu_ref, dbg_ref, dbu_ref):
        pg, pu = pg_ref[...], pu_ref[...]
        pg_shifted, pu_shifted = _shifts_down(pg, FFN_CONV), _shifts_down(pu, FFN_CONV)
        ug = _conv_pre(pg, pg_shifted, wg_ref, bg_ref)
        uu = _conv_pre(pu, pu_shifted, wu_ref, bu_ref)
        sg = _sigmoid(ug)
        da = da_ref[...]
        dug = da * uu * (sg * (1.0 + ug * (1.0 - sg)))
        duu = da * (ug * sg)
        dpg_ref[...] = _conv_transpose(dug, wg_ref).astype(dpg_ref.dtype)
        dpu_ref[...] = _conv_transpose(duu, wu_ref).astype(dpu_ref.dtype)
        _conv_wgrad(dug, pg, pg_shifted, dwg_ref, dbg_ref)
        _conv_wgrad(duu, pu, pu_shifted, dwu_ref, dbu_ref)

    gcol = pl.BlockSpec((t, cb), lambda j: (0, j))
    ucol = pl.BlockSpec((t, cb), lambda j: (0, j + nb))
    wg = pl.BlockSpec((FFN_CONV, cb), lambda j: (0, j))
    wu = pl.BlockSpec((FFN_CONV, cb), lambda j: (0, j + nb))
    bg = pl.BlockSpec((1, cb), lambda j: (0, j))
    bu = pl.BlockSpec((1, cb), lambda j: (0, j + nb))
    b2 = b.reshape(1, -1)
    half = jax.ShapeDtypeStruct((t, FFN_D_FF), MXU_DTYPE)
    return pl.pallas_call(
        body, name=name, grid=(nb,),
        in_specs=[gcol, ucol, wg, wu, bg, bu, gcol],
        out_specs=[gcol, gcol, wg, wg, bg, bg],
        out_shape=[half, half, jax.ShapeDtypeStruct((FFN_CONV, FFN_D_FF), F32), jax.ShapeDtypeStruct((FFN_CONV, FFN_D_FF), F32),
                   jax.ShapeDtypeStruct((1, FFN_D_FF), F32), jax.ShapeDtypeStruct((1, FFN_D_FF), F32)],
        compiler_params=_cp("parallel"),
    )(proj, proj, w, w, b2, b2, dact)


SSD_ROWS = 256
DT_COL = (SSD_D_INNER + SSD_CONV_DIM) // LANES


def _head_expand():
    h = np.arange(LANES)[:, None]
    col = np.arange(SSD_D_INNER)[None, :]
    return jnp.asarray((col // SSD_HEAD_DIM == h), MXU_DTYPE)


def _chunk_tri(n, lower):
    t = _iota((n, n), 0)
    s = _iota((n, n), 1)
    shift = SSD_CHUNK.bit_length() - 1
    same = jnp.right_shift(t, shift) == jnp.right_shift(s, shift)
    tri = (s <= t) if lower else (s >= t)
    return jnp.where(same & tri, 1.0, 0.0).astype(MXU_DTYPE)


def _softplus(x):
    return jnp.maximum(x, 0.0) + jnp.log(1.0 + jnp.exp(-jnp.abs(x)))


def _ssd_dt_fwd(proj, act, dt_bias, a_neg, expand, name):
    t = proj.shape[0]
    tm = min(SSD_ROWS, t)

    def body(raw_ref, xs_ref, bias_ref, a_ref, e_ref, xdt_ref, dt_ref, acum_ref):
        lane = _iota((tm, LANES), 1)
        dt = jnp.where(lane < SSD_HEADS, _softplus(raw_ref[...] + bias_ref[...]), 0.0)
        dt_ref[...] = dt
        xdt_ref[...] = xs_ref[...] * _mm_exact_rhs(dt, e_ref[...])
        acum_ref[...] = _mm_exact_lhs(_chunk_tri(tm, True), a_ref[...] * dt)

    vec = pl.BlockSpec((1, LANES), lambda i: (0, 0))
    return pl.pallas_call(
        body, name=name, grid=(t // tm,),
        in_specs=[pl.BlockSpec((tm, LANES), lambda i: (i, DT_COL)), pl.BlockSpec((tm, SSD_D_INNER), lambda i: (i, 0)),
                  vec, vec, pl.BlockSpec((LANES, SSD_D_INNER), lambda i: (0, 0))],
        out_specs=[pl.BlockSpec((tm, SSD_D_INNER), lambda i: (i, 0)), pl.BlockSpec((tm, LANES), lambda i: (i, 0)),
                   pl.BlockSpec((tm, LANES), lambda i: (i, 0))],
        out_shape=[jax.ShapeDtypeStruct((t, SSD_D_INNER), F32), jax.ShapeDtypeStruct((t, LANES), F32),
                   jax.ShapeDtypeStruct((t, LANES), F32)],
        compiler_params=_cp("parallel"),
    )(proj, act, dt_bias, a_neg, expand)


def _ssd_dt_bwd(proj, act, dt, dxdt, dyy, dacum, dt_bias, a_neg, d_exp, expand, expand_t, name):
    t = proj.shape[0]
    tm = min(SSD_ROWS, t)

    def body(raw_ref, xs_ref, dt_ref, dxdt_ref, dyy_ref, dac_ref, bias_ref, a_ref, dsk_ref, e_ref, et_ref,
             dxs_ref, draw_ref, da_ref, dbias_ref, dd_ref):
        @pl.when(pl.program_id(0) == 0)
        def _():
            da_ref[...] = jnp.zeros_like(da_ref)
            dbias_ref[...] = jnp.zeros_like(dbias_ref)
            dd_ref[...] = jnp.zeros_like(dd_ref)

        lane = _iota((tm, LANES), 1)
        xs, dt, dxdt, dyy = xs_ref[...], dt_ref[...], dxdt_ref[...], dyy_ref[...]
        dxs_ref[...] = dxdt * _mm_exact_rhs(dt, e_ref[...]) + dsk_ref[...] * dyy
        dd_ref[...] += jnp.sum(dyy * xs, axis=0, keepdims=True)
        ddt = _mm_exact_rhs(dxdt * xs, et_ref[...])
        da = _mm_exact_lhs(_chunk_tri(tm, False), dac_ref[...])
        ddt = ddt + da * a_ref[...]
        da_ref[...] += jnp.sum(da * dt, axis=0, keepdims=True)
        draw = jnp.where(lane < SSD_HEADS, ddt * _sigmoid(raw_ref[...] + bias_ref[...]), 0.0)
        dbias_ref[...] += jnp.sum(draw, axis=0, keepdims=True)
        draw_ref[...] = draw.astype(draw_ref.dtype)

    wide = pl.BlockSpec((tm, SSD_D_INNER), lambda i: (i, 0))
    thin = pl.BlockSpec((tm, LANES), lambda i: (i, 0))
    vec = pl.BlockSpec((1, LANES), lambda i: (0, 0))
    wvec = pl.BlockSpec((1, SSD_D_INNER), lambda i: (0, 0))
    return pl.pallas_call(
        body, name=name, grid=(t // tm,),
        in_specs=[pl.BlockSpec((tm, LANES), lambda i: (i, DT_COL)), wide, thin, wide, wide, thin, vec, vec, wvec,
                  pl.BlockSpec((LANES, SSD_D_INNER), lambda i: (0, 0)), pl.BlockSpec((SSD_D_INNER, LANES), lambda i: (0, 0))],
        out_specs=[wide, thin, vec, vec, wvec],
        out_shape=[jax.ShapeDtypeStruct((t, SSD_D_INNER), F32), jax.ShapeDtypeStruct((t, LANES), MXU_DTYPE),
                   jax.ShapeDtypeStruct((1, LANES), F32), jax.ShapeDtypeStruct((1, LANES), F32),
                   jax.ShapeDtypeStruct((1, SSD_D_INNER), F32)],
        compiler_params=_cp("arbitrary"),
    )(proj, act, dt, dxdt, dyy, dacum, dt_bias, a_neg, d_exp, expand, expand_t)


SSD_PAIR = 2 * SSD_HEAD_DIM
HEADS_PER_GROUP = SSD_HEADS // SSD_GROUPS
GROUP_COLS = HEADS_PER_GROUP * SSD_HEAD_DIM
B_COL0 = SSD_D_INNER // SSD_STATE
C_COL0 = (SSD_D_INNER + SSD_GROUPS * SSD_STATE) // SSD_STATE


MASKED_LOG = -1e30
SCAN_GROUPS = 4
SCAN_STEPS = SSD_GROUPS // SCAN_GROUPS


def _scan_step_is(g, c):
    return (pl.program_id(0) == g) & (pl.program_id(1) == c)


def _ssd_scan_fwd(xdt, act, acum_g, acum_gt, name, carried=None):
    t = xdt.shape[0]
    nc = t // SSD_CHUNK
    ln = SSD_CHUNK
    c_in_specs, c_in, c_out_specs, c_out = _carried_specs(carried)

    def body(*refs):
        if carried is None:
            x_ref, b_ref, c_ref, ac_ref, act_ref, y_ref, sst_ref, state = refs
            comm_refs = None
        else:
            x_ref, b_ref, c_ref, ac_ref, act_ref, src_ref, y_ref, sst_ref, land_ref, state, send_sems, recv_sems, local_sem = refs
            comm_refs = (src_ref, land_ref, send_sems, recv_sems, local_sem)
        finish = _carried_hooks(carried, comm_refs, _scan_step_is(0, 0), _scan_step_is(SCAN_STEPS - 1, (3 * nc) // 4),
                                _scan_step_is(SCAN_STEPS - 1, nc - 1))

        @pl.when(pl.program_id(1) == 0)
        def _():
            state[...] = jnp.zeros_like(state)

        causal = _iota((ln, ln), 1) <= _iota((ln, ln), 0)
        lo_mask = _iota((ln, SSD_PAIR), 1) < SSD_HEAD_DIM
        lo_rows = _iota((SSD_PAIR, SSD_STATE), 0) < SSD_HEAD_DIM
        for gg in range(SCAN_GROUPS):
            sst_ref[0, gg] = state[gg * GROUP_COLS:(gg + 1) * GROUP_COLS, :]
            bm = b_ref[:, gg * SSD_STATE:(gg + 1) * SSD_STATE].astype(MXU_DTYPE)
            cm = c_ref[:, gg * SSD_STATE:(gg + 1) * SSD_STATE].astype(MXU_DTYPE)
            cb = _mm_nt(cm, bm)
            ac, act_ = ac_ref[gg], act_ref[gg]
            e_last = jnp.exp(ac[ln - 1:ln, :])
            ac_rows = [jnp.broadcast_to(ac[:, h:h + 1], (ln, SSD_PAIR)) for h in range(HEADS_PER_GROUP)]
            for pr in range(2):
                first = gg * GROUP_COLS + pr * SSD_PAIR
                cols = slice(first, first + SSD_PAIR)
                xp = x_ref[:, cols]
                sp = state[cols, :]
                pair_ac = jnp.where(lo_mask, ac_rows[2 * pr], ac_rows[2 * pr + 1])
                ydiag = jnp.zeros((ln, SSD_PAIR), F32)
                for hh in range(2):
                    h = 2 * pr + hh
                    seg = ac_rows[h] - act_[h:h + 1, :]
                    dec = jnp.exp(jnp.where(causal, seg, MASKED_LOG))
                    mask = lo_mask if hh == 0 else jnp.logical_not(lo_mask)
                    ydiag = ydiag + _mm((cb * dec).astype(MXU_DTYPE), jnp.where(mask, xp, 0.0).astype(MXU_DTYPE))
                yoff = _mm_nt(cm, sp.astype(MXU_DTYPE)) * jnp.exp(pair_ac)
                y_ref[:, cols] = ydiag + yoff
                xw = (xp * jnp.exp(pair_ac[ln - 1:ln, :] - pair_ac)).astype(MXU_DTYPE)
                el = jnp.where(lo_rows, e_last[:, 2 * pr:2 * pr + 1], e_last[:, 2 * pr + 1:2 * pr + 2])
                state[cols, :] = sp * el + _mm_tn(xw, bm)
        finish()

    sg = SCAN_GROUPS
    return pl.pallas_call(
        body, name=name, grid=(SCAN_STEPS, nc),
        in_specs=[pl.BlockSpec((ln, sg * GROUP_COLS), lambda g, c: (c, g)),
                  pl.BlockSpec((ln, sg * SSD_STATE), lambda g, c: (c, B_COL0 // sg + g)),
                  pl.BlockSpec((ln, sg * SSD_STATE), lambda g, c: (c, C_COL0 // sg + g)),
                  pl.BlockSpec((sg, ln, HEADS_PER_GROUP), lambda g, c: (g, c, 0)),
                  pl.BlockSpec((sg, HEADS_PER_GROUP, ln), lambda g, c: (g, 0, c))] + c_in_specs,
        out_specs=[pl.BlockSpec((ln, sg * GROUP_COLS), lambda g, c: (c, g)),
                   pl.BlockSpec((1, sg, GROUP_COLS, SSD_STATE), lambda g, c: (c, g, 0, 0))] + c_out_specs,
        out_shape=[jax.ShapeDtypeStruct((t, SSD_D_INNER), F32),
                   jax.ShapeDtypeStruct((nc, SSD_GROUPS, GROUP_COLS, SSD_STATE), F32)] + c_out,
        scratch_shapes=[pltpu.VMEM((sg * GROUP_COLS, SSD_STATE), F32)] + (EXCHANGE_SCRATCH if carried else []),
        compiler_params=_cp("arbitrary", "arbitrary"),
    )(xdt, act, act, acum_g, acum_gt, *c_in)


def _ssd_scan_bwd(xdt, act, acum_g, acum_gt, states, dy, name, carried=None):
    t = xdt.shape[0]
    nc = t // SSD_CHUNK
    ln = SSD_CHUNK
    c_in_specs, c_in, c_out_specs, c_out = _carried_specs(carried)

    def body(*refs):
        if carried is None:
            x_ref, b_ref, c_ref, ac_ref, act_ref, sst_ref, dy_ref, dx_ref, db_ref, dc_ref, dacol_ref, darow_ref, dstate = refs
            comm_refs = None
        else:
            (x_ref, b_ref, c_ref, ac_ref, act_ref, sst_ref, dy_ref, src_ref, dx_ref, db_ref, dc_ref, dacol_ref, darow_ref,
             land_ref, dstate, send_sems, recv_sems, local_sem) = refs
            comm_refs = (src_ref, land_ref, send_sems, recv_sems, local_sem)
        finish = _carried_hooks(carried, comm_refs, _scan_step_is(0, 0), _scan_step_is(SCAN_STEPS - 1, (3 * nc) // 4),
                                _scan_step_is(SCAN_STEPS - 1, nc - 1))

        @pl.when(pl.program_id(1) == 0)
        def _():
            dstate[...] = jnp.zeros_like(dstate)

        for gg in range(SCAN_GROUPS):
            group_bwd(gg, x_ref, b_ref, c_ref, ac_ref, act_ref, sst_ref, dy_ref, dx_ref, db_ref, dc_ref, dacol_ref, darow_ref, dstate)
        finish()

    def group_bwd(gg, x_ref, b_ref, c_ref, ac_ref, act_ref, sst_ref, dy_ref, dx_ref, db_ref, dc_ref, dacol_ref, darow_ref, dstate):
        bc_cols = slice(gg * SSD_STATE, (gg + 1) * SSD_STATE)
        bm = b_ref[:, bc_cols].astype(MXU_DTYPE)
        cm = c_ref[:, bc_cols].astype(MXU_DTYPE)
        cb = _mm_nt(cm, bm)
        ac, act_ = ac_ref[gg], act_ref[gg]
        causal = _iota((ln, ln), 1) <= _iota((ln, ln), 0)
        lo_mask = _iota((ln, SSD_PAIR), 1) < SSD_HEAD_DIM
        lo_rows = _iota((SSD_PAIR, SSD_STATE), 0) < SSD_HEAD_DIM
        lane4 = _iota((ln, HEADS_PER_GROUP), 1)
        sub4 = _iota((HEADS_PER_GROUP, ln), 0)
        is_last = _iota((ln, 1), 0) == ln - 1
        e_last = jnp.exp(ac[ln - 1:ln, :])
        ac_rows = [jnp.broadcast_to(ac[:, h:h + 1], (ln, SSD_PAIR)) for h in range(HEADS_PER_GROUP)]
        dcb = jnp.zeros((ln, ln), F32)
        dc_acc = jnp.zeros((ln, SSD_STATE), F32)
        db_acc = jnp.zeros((ln, SSD_STATE), F32)
        dacol = jnp.zeros((ln, HEADS_PER_GROUP), F32)
        darow = jnp.zeros((HEADS_PER_GROUP, ln), F32)
        for pr in range(2):
            in_group = slice(pr * SSD_PAIR, (pr + 1) * SSD_PAIR)
            cols = slice(gg * GROUP_COLS + pr * SSD_PAIR, gg * GROUP_COLS + (pr + 1) * SSD_PAIR)
            xp = x_ref[:, cols]
            dyp = dy_ref[:, cols]
            sp = sst_ref[0, gg, in_group, :]
            dsp = dstate[cols, :]
            pair_ac = jnp.where(lo_mask, ac_rows[2 * pr], ac_rows[2 * pr + 1])
            ea = jnp.exp(pair_ac)
            w = jnp.exp(pair_ac[ln - 1:ln, :] - pair_ac)
            dye = (dyp * ea).astype(MXU_DTYPE)
            dx_state = w * _mm_nt(bm, dsp.astype(MXU_DTYPE))
            yoff = _mm_nt(cm, sp.astype(MXU_DTYPE)) * ea
            dxp = dx_state
            for hh in range(2):
                h = 2 * pr + hh
                mask = lo_mask if hh == 0 else jnp.logical_not(lo_mask)
                rmask = lo_rows if hh == 0 else jnp.logical_not(lo_rows)
                seg = ac_rows[h] - act_[h:h + 1, :]
                dec = jnp.exp(jnp.where(causal, seg, MASKED_LOG))
                m = cb * dec
                dym = jnp.where(mask, dyp, 0.0).astype(MXU_DTYPE)
                xm = jnp.where(mask, xp, 0.0).astype(MXU_DTYPE)
                g = _mm_nt(dym, xm)
                dxp = dxp + _mm_tn(m.astype(MXU_DTYPE), dym)
                dcb = dcb + dec * g
                mg = m * g
                rs = jnp.sum(mg, axis=1, keepdims=True)
                cs = jnp.sum(mg, axis=0, keepdims=True)
                t_off = jnp.sum(jnp.where(mask, dyp * yoff, 0.0), axis=1, keepdims=True)
                q = jnp.sum(jnp.where(mask, xp * dx_state, 0.0), axis=1, keepdims=True)
                qsum = jnp.sum(q, axis=0, keepdims=True)
                ds_s = jnp.sum(jnp.sum(jnp.where(rmask, dsp * sp, 0.0), axis=1, keepdims=True), axis=0, keepdims=True)
                extra = qsum + e_last[:, h:h + 1] * ds_s
                col = rs + t_off - q + jnp.where(is_last, extra, 0.0)
                dacol = jnp.where(lane4 == h, col, dacol)
                darow = jnp.where(sub4 == h, -cs, darow)
            dx_ref[:, cols] = dxp
            dc_acc = dc_acc + _mm(dye, sp.astype(MXU_DTYPE))
            db_acc = db_acc + _mm((xp * w).astype(MXU_DTYPE), dsp.astype(MXU_DTYPE))
            el = jnp.where(lo_rows, e_last[:, 2 * pr:2 * pr + 1], e_last[:, 2 * pr + 1:2 * pr + 2])
            dstate[cols, :] = dsp * el + _mm_tn(dye, cm)
        dcbm = dcb.astype(MXU_DTYPE)
        dc_ref[:, bc_cols] = _mm(dcbm, bm) + dc_acc
        db_ref[:, bc_cols] = _mm_tn(dcbm, cm) + db_acc
        dacol_ref[gg] = dacol
        darow_ref[gg] = darow

    def rev(c):
        return nc - 1 - c

    sg = SCAN_GROUPS
    grp = pl.BlockSpec((ln, sg * GROUP_COLS), lambda g, c: (rev(c), g))
    return pl.pallas_call(
        body, name=name, grid=(SCAN_STEPS, nc),
        in_specs=[grp,
                  pl.BlockSpec((ln, sg * SSD_STATE), lambda g, c: (rev(c), B_COL0 // sg + g)),
                  pl.BlockSpec((ln, sg * SSD_STATE), lambda g, c: (rev(c), C_COL0 // sg + g)),
                  pl.BlockSpec((sg, ln, HEADS_PER_GROUP), lambda g, c: (g, rev(c), 0)),
                  pl.BlockSpec((sg, HEADS_PER_GROUP, ln), lambda g, c: (g, 0, rev(c))),
                  pl.BlockSpec((1, sg, GROUP_COLS, SSD_STATE), lambda g, c: (rev(c), g, 0, 0)),
                  grp] + c_in_specs,
        out_specs=[grp,
                   pl.BlockSpec((ln, sg * SSD_STATE), lambda g, c: (rev(c), g)),
                   pl.BlockSpec((ln, sg * SSD_STATE), lambda g, c: (rev(c), g)),
                   pl.BlockSpec((sg, ln, HEADS_PER_GROUP), lambda g, c: (g, rev(c), 0)),
                   pl.BlockSpec((sg, HEADS_PER_GROUP, ln), lambda g, c: (g, 0, rev(c)))] + c_out_specs,
        out_shape=[jax.ShapeDtypeStruct((t, SSD_D_INNER), F32),
                   jax.ShapeDtypeStruct((t, SSD_GROUPS * SSD_STATE), F32),
                   jax.ShapeDtypeStruct((t, SSD_GROUPS * SSD_STATE), F32),
                   jax.ShapeDtypeStruct((SSD_GROUPS, t, HEADS_PER_GROUP), F32),
                   jax.ShapeDtypeStruct((SSD_GROUPS, HEADS_PER_GROUP, t), F32)] + c_out,
        scratch_shapes=[pltpu.VMEM((sg * GROUP_COLS, SSD_STATE), F32)] + (EXCHANGE_SCRATCH if carried else []),
        compiler_params=_cp("arbitrary", "arbitrary"),
    )(xdt, act, act, acum_g, acum_gt, states, dy, *c_in)


GN_ROWS = 256


def _gated_norm_parts(y_ref, xs_ref, z_ref, dsk_ref):
    yy = y_ref[...] + dsk_ref[...] * xs_ref[...]
    z = z_ref[...]
    sz = _sigmoid(z)
    silu = z * sz
    u = yy * silu
    r = lax.rsqrt(jnp.mean(u * u, axis=-1, keepdims=True) + NORM_EPS)
    return yy, z, sz, silu, u, r


def _gated_norm_fwd(y, act, proj, d_exp, g, name):
    t = y.shape[0]
    tm = min(GN_ROWS, t)

    def body(y_ref, xs_ref, z_ref, dsk_ref, g_ref, o_ref):
        _, _, _, _, u, r = _gated_norm_parts(y_ref, xs_ref, z_ref, dsk_ref)
        o_ref[...] = (u * r * g_ref[...]).astype(o_ref.dtype)

    wide = pl.BlockSpec((tm, SSD_D_INNER), lambda i: (i, 0))
    wvec = pl.BlockSpec((1, SSD_D_INNER), lambda i: (0, 0))
    return pl.pallas_call(
        body, name=name, grid=(t // tm,), in_specs=[wide, wide, wide, wvec, wvec], out_specs=wide,
        out_shape=jax.ShapeDtypeStruct((t, SSD_D_INNER), MXU_DTYPE), compiler_params=_cp("parallel"),
    )(y, act, proj, d_exp, g.reshape(1, -1))


def _gated_norm_bwd(y, act, proj, d_exp, g, dn, name):
    t = y.shape[0]
    tm = min(GN_ROWS, t)

    def body(y_ref, xs_ref, z_ref, dsk_ref, g_ref, dn_ref, dyy_ref, dz_ref, dg_ref):
        @pl.when(pl.program_id(0) == 0)
        def _():
            dg_ref[...] = jnp.zeros_like(dg_ref)

        yy, z, sz, silu, u, r = _gated_norm_parts(y_ref, xs_ref, z_ref, dsk_ref)
        un = u * r
        dn = dn_ref[...]
        v = dn * g_ref[...]
        du = r * (v - un * jnp.mean(v * un, axis=-1, keepdims=True))
        dg_ref[...] += jnp.sum(dn * un, axis=0, keepdims=True)
        dyy_ref[...] = du * silu
        dz_ref[...] = (du * yy * (sz * (1.0 + z * (1.0 - sz)))).astype(dz_ref.dtype)

    wide = pl.BlockSpec((tm, SSD_D_INNER), lambda i: (i, 0))
    wvec = pl.BlockSpec((1, SSD_D_INNER), lambda i: (0, 0))
    return pl.pallas_call(
        body, name=name, grid=(t // tm,), in_specs=[wide, wide, wide, wvec, wvec, wide], out_specs=[wide, wide, wvec],
        out_shape=[jax.ShapeDtypeStruct((t, SSD_D_INNER), F32), jax.ShapeDtypeStruct((t, SSD_D_INNER), MXU_DTYPE),
                   jax.ShapeDtypeStruct((1, SSD_D_INNER), F32)],
        compiler_params=_cp("arbitrary"),
    )(y, act, proj, d_exp, g.reshape(1, -1), dn)


SB_PAIRS = SB_HEADS // 2


def _kv_rows(j, bt, nt=1):
    return pl.ds(pl.multiple_of(j * bt, bt), nt * bt)


def _sb_tile_masks(bt):
    lane = _iota((bt, bt), 1)
    rowi = _iota((bt, bt), 0)
    return lane < rowi, (rowi >= lane).astype(MXU_DTYPE), (rowi <= lane).astype(MXU_DTYPE)


def _sb_scaled_heads(pair, scale):
    lane = _iota(pair.shape, 1)
    val = pair.astype(F32) * scale
    return [jnp.where(lane < SB_HEAD_DIM, val, 0.0).astype(pair.dtype), jnp.where(lane >= SB_HEAD_DIM, val, 0.0).astype(pair.dtype)]


def _sb_logits(qs, kb, bt, strict):
    nt = kb.shape[0] // bt
    full = [_mm_nt(q_head, kb) for q_head in qs]
    xs, nlfs = [], []
    for x in full:
        nlf = jnp.maximum(x, 0.0) + jnp.log(1.0 + jnp.exp(-jnp.abs(x)))
        xs.append([x[:, tt * bt:(tt + 1) * bt] for tt in range(nt)])
        tiles = [nlf[:, tt * bt:(tt + 1) * bt] for tt in range(nt)]
        if strict is not None:
            tiles[-1] = jnp.where(strict, tiles[-1], 0.0)
        nlfs.append(tiles)
    return xs, nlfs


def _sb_tails(nlf_tiles, from_j):
    tails, run = [None] * len(nlf_tiles), None
    for tt in reversed(range(len(nlf_tiles))):
        tail = _mm_exact_rhs(nlf_tiles[tt], from_j)
        tails[tt] = tail if run is None else tail + run
        run = tails[tt][:, 0:1]
    return tails


def _sb_heads(e_tiles, upto_j, pre):
    sums, run = [], pre
    for e in e_tiles:
        sums.append(_mm_exact_rhs(e, upto_j) + run)
        run = sums[-1][:, e.shape[1] - 1:e.shape[1]]
    return sums


def _carried_specs(carried):
    if carried is None:
        return [], [], [], []
    src, per_peer = carried
    rows = src.shape[1:] if per_peer else src.shape
    anywhere = pl.BlockSpec(memory_space=pl.ANY)
    return [anywhere], [src], [anywhere], [jax.ShapeDtypeStruct((N_DEV, *rows), src.dtype)]


def _carried_hooks(carried, comm_refs, first, pass_on, last):
    if carried is None:
        return lambda: None
    per_peer = carried[1]

    @pl.when(first)
    def _():
        if per_peer:
            _exchange_start(*comm_refs, per_peer=True)
        else:
            _gather_phase("send", *comm_refs)

    if not per_peer:
        @pl.when(pass_on)
        def _():
            _gather_phase("pass_on", *comm_refs)

    def finish():
        @pl.when(last)
        def _():
            if per_peer:
                _exchange_finish(*comm_refs, per_peer=True)
            else:
                _gather_phase("finish", *comm_refs)

    return finish


def _sb_attention_fwd(qkv, name, carried=None):
    t = qkv.shape[0]
    bt = min(SB_TILE, t)
    nq = t // bt
    c_in_specs, c_in, c_out_specs, c_out = _carried_specs(carried)

    def body(*refs):
        if carried is None:
            q_ref, k_ref, v_ref, o_ref, acc_ref = refs
            comm_refs = None
        else:
            q_ref, k_ref, v_ref, src_ref, o_ref, land_ref, acc_ref, send_sems, recv_sems, local_sem = refs
            comm_refs = (src_ref, land_ref, send_sems, recv_sems, local_sem)
        i = pl.program_id(1)
        last_pair = pl.program_id(0) == SB_PAIRS - 1
        finish = _carried_hooks(carried, comm_refs, (pl.program_id(0) == 0) & (i == 0), last_pair & (i == 0),
                                last_pair & (i == nq - 1))
        strict, from_j, _ = _sb_tile_masks(bt)
        qs = _sb_scaled_heads(q_ref[...], SB_SCALE)
        acc_ref[...] = jnp.zeros_like(acc_ref)

        def block(j, nt, carries, diag):
            rows = _kv_rows(j, bt, nt)
            kb, vb = k_ref[rows, :], v_ref[rows, :]
            xs, nlfs = _sb_logits(qs, kb, bt, strict if diag else None)
            tails = [_sb_tails(nlfs[hh], from_j) for hh in range(2)]
            for hh in range(2):
                ws = [jnp.exp(xs[hh][tt] - tails[hh][tt] - carries[hh]) for tt in range(nt)]
                if diag:
                    ws[-1] = jnp.where(strict, ws[-1], 0.0)
                acc_ref[hh] += _mm(jnp.concatenate([w.astype(MXU_DTYPE) for w in ws], axis=1), vb)
            return tuple(carries[hh] + tails[hh][0][:, 0:1] for hh in range(2))

        zero = jnp.zeros((bt, 1), F32)
        carries = block(i, 1, (zero, zero), True)
        carries = lax.fori_loop(0, i // 2, lambda it, cr: block(i - 2 - 2 * it, 2, cr, False), carries)

        @pl.when(i % 2 == 1)
        def _():
            block(0, 1, carries, False)

        low = _iota((bt, 2 * SB_HEAD_DIM), 1) < SB_HEAD_DIM
        o_ref[...] = jnp.where(low, acc_ref[0], acc_ref[1]).astype(o_ref.dtype)
        finish()

    lanes = 2 * SB_HEAD_DIM
    res = pl.pallas_call(
        body, name=name, grid=(SB_PAIRS, nq),
        in_specs=[pl.BlockSpec((bt, lanes), lambda p, i: (i, p)),
                  pl.BlockSpec((t, lanes), lambda p, i: (0, SB_PAIRS + p)),
                  pl.BlockSpec((t, lanes), lambda p, i: (0, 2 * SB_PAIRS + p))] + c_in_specs,
        out_specs=[pl.BlockSpec((bt, lanes), lambda p, i: (i, p))] + c_out_specs,
        out_shape=[jax.ShapeDtypeStruct((t, D_MODEL), MXU_DTYPE)] + c_out,
        scratch_shapes=[pltpu.VMEM((2, bt, lanes), F32)] + (EXCHANGE_SCRATCH if carried else []),
        compiler_params=_cp("arbitrary", "arbitrary"),
    )(qkv, qkv, qkv, *c_in)
    return res[0] if carried is None else res


def _sb_attention_bwd(qkv, do, name, carried=None):
    t = qkv.shape[0]
    bt = min(SB_TILE, t)
    nq = t // bt
    lanes = 2 * SB_HEAD_DIM
    c_in_specs, c_in, c_out_specs, c_out = _carried_specs(carried)

    def body(*refs):
        if carried is None:
            q_ref, k_ref, v_ref, do_ref, dq_ref, dk_ref, dv_ref, sbuf, ebuf, dq_acc, dk_acc, dv_acc = refs
            comm_refs = None
        else:
            (q_ref, k_ref, v_ref, do_ref, src_ref, dq_ref, dk_ref, dv_ref, land_ref,
             sbuf, ebuf, dq_acc, dk_acc, dv_acc, send_sems, recv_sems, local_sem) = refs
            comm_refs = (src_ref, land_ref, send_sems, recv_sems, local_sem)
        i = pl.program_id(1)
        last_pair = pl.program_id(0) == SB_PAIRS - 1
        finish = _carried_hooks(carried, comm_refs, (pl.program_id(0) == 0) & (i == 0), last_pair & (i == 0),
                                last_pair & (i == nq - 1))

        @pl.when(i == 0)
        def _():
            dk_acc[...] = jnp.zeros_like(dk_acc)
            dv_acc[...] = jnp.zeros_like(dv_acc)

        strict, from_j, upto_j = _sb_tile_masks(bt)
        qs = _sb_scaled_heads(q_ref[...], SB_SCALE)
        dos = _sb_scaled_heads(do_ref[...], 1.0)
        q_both = jnp.concatenate(qs, axis=0)
        do_both = jnp.concatenate(dos, axis=0)
        dq_acc[...] = jnp.zeros_like(dq_acc)

        def pass1(j, nt, carries, diag):
            rows = _kv_rows(j, bt, nt)
            kb, vb = k_ref[rows, :], v_ref[rows, :]
            xs, nlfs = _sb_logits(qs, kb, bt, strict if diag else None)
            dws = [_mm_nt(dos[hh], vb) for hh in range(2)]
            tails = [_sb_tails(nlfs[hh], from_j) for hh in range(2)]
            wcat = []
            for hh in range(2):
                ws = [jnp.exp(xs[hh][tt] - tails[hh][tt] - carries[hh]) for tt in range(nt)]
                if diag:
                    ws[-1] = jnp.where(strict, ws[-1], 0.0)
                w_all = jnp.concatenate(ws, axis=1)
                sbuf[hh, :, rows] = jnp.exp(jnp.concatenate([xs[hh][tt] - nlfs[hh][tt] for tt in range(nt)], axis=1))
                ebuf[hh, :, rows] = w_all * dws[hh]
                wcat.append(w_all.astype(MXU_DTYPE))
            dv_acc[rows, :] += _mm_tn(jnp.concatenate(wcat, axis=0), do_both)
            return tuple(carries[hh] + tails[hh][0][:, 0:1] for hh in range(2))

        zero = jnp.zeros((bt, 1), F32)
        carries = pass1(i, 1, (zero, zero), True)
        carries = lax.fori_loop(0, i // 2, lambda it, cr: pass1(i - 2 - 2 * it, 2, cr, False), carries)

        @pl.when(i % 2 == 1)
        def _():
            pass1(0, 1, carries, False)

        def pass2(j, nt, pres, diag):
            rows = _kv_rows(j, bt, nt)
            kb = k_ref[rows, :]
            sums = [_sb_heads([ebuf[hh, :, _kv_rows(j + tt, bt)] for tt in range(nt)], upto_j, pres[hh]) for hh in range(2)]
            dxm = []
            for hh in range(2):
                dxs = [ebuf[hh, :, _kv_rows(j + tt, bt)] - sbuf[hh, :, _kv_rows(j + tt, bt)] * sums[hh][tt] for tt in range(nt)]
                if diag:
                    dxs[-1] = jnp.where(strict, dxs[-1], 0.0)
                dxm.append(jnp.concatenate(dxs, axis=1).astype(MXU_DTYPE))
                dq_acc[hh] += _mm(dxm[hh], kb)
            dk_acc[rows, :] += _mm_tn(jnp.concatenate(dxm, axis=0), q_both)
            return tuple(sums[hh][-1][:, bt - 1:bt] for hh in range(2))

        pres = lax.fori_loop(0, i // 2, lambda it, pr: pass2(2 * it, 2, pr, False), (zero, zero))

        @pl.when(i % 2 == 0)
        def _():
            pass2(i, 1, pres, True)

        @pl.when(i % 2 == 1)
        def _():
            pass2(i - 1, 2, pres, True)

        low = _iota((bt, lanes), 1) < SB_HEAD_DIM
        dq_ref[...] = (jnp.where(low, dq_acc[0], dq_acc[1]) * SB_SCALE).astype(dq_ref.dtype)

        @pl.when(i == nq - 1)
        def _():
            dk_ref[...] = dk_acc[...].astype(dk_ref.dtype)
            dv_ref[...] = dv_acc[...].astype(dv_ref.dtype)

        finish()

    blk = pl.BlockSpec((bt, lanes), lambda p, i: (i, p))
    whole = pl.BlockSpec((t, lanes), lambda p, i: (0, p))
    out = jax.ShapeDtypeStruct((t, D_MODEL), MXU_DTYPE)
    return pl.pallas_call(
        body, name=name, grid=(SB_PAIRS, nq),
        in_specs=[blk, pl.BlockSpec((t, lanes), lambda p, i: (0, SB_PAIRS + p)),
                  pl.BlockSpec((t, lanes), lambda p, i: (0, 2 * SB_PAIRS + p)), blk] + c_in_specs,
        out_specs=[blk, whole, whole] + c_out_specs, out_shape=[out, out, out] + c_out,
        scratch_shapes=[pltpu.VMEM((2, bt, t), F32), pltpu.VMEM((2, bt, t), F32), pltpu.VMEM((2, bt, lanes), F32),
                        pltpu.VMEM((t, lanes), F32), pltpu.VMEM((t, lanes), F32)] + (EXCHANGE_SCRATCH if carried else []),
        compiler_params=_cp("arbitrary", "arbitrary"),
    )(qkv, qkv, qkv, do, *c_in)


def _add_pair(a, b, name):
    s, r, c = a.shape
    tm = _row_tile(r)

    def body(a_ref, b_ref, o_ref):
        o_ref[...] = (a_ref[...].astype(F32) + b_ref[...].astype(F32)).astype(o_ref.dtype)

    blk = pl.BlockSpec((1, tm, c), lambda q, i: (q, i, 0))
    return pl.pallas_call(body, name=name, grid=(s, r // tm), in_specs=[blk, blk], out_specs=blk,
                          out_shape=jax.ShapeDtypeStruct(a.shape, a.dtype), compiler_params=_cp("parallel", "parallel"))(a, b)


def _sum_slots(gslots, name):
    s, r, c = gslots.shape

    def body(g_ref, o_ref):
        g = g_ref[0].astype(F32)
        for q in range(1, s):
            g = g + g_ref[q].astype(F32)
        o_ref[...] = g

    return pl.pallas_call(
        body, name=name, grid=(c // LANES,),
        in_specs=[pl.BlockSpec((s, r, LANES), lambda j: (0, 0, j))], out_specs=pl.BlockSpec((r, LANES), lambda j: (0, j)),
        out_shape=jax.ShapeDtypeStruct((r, c), F32), compiler_params=_cp("parallel"),
    )(gslots)


def _adamw(gslots, w, m, v, name):
    s, r, c = gslots.shape
    tm = _row_tile(r)
    assert w.shape == (r, c), (w.shape, gslots.shape)
    c1 = 1.0 - ADAM_B1 ** ADAM_STEP
    c2 = 1.0 - ADAM_B2 ** ADAM_STEP

    def body(g_ref, w_ref, m_ref, v_ref, go_ref, d_ref, mo_ref, vo_ref):
        g = g_ref[0].astype(F32)
        for q in range(1, s):
            g = g + g_ref[q].astype(F32)
        mn = ADAM_B1 * m_ref[...] + (1.0 - ADAM_B1) * g
        vn = ADAM_B2 * v_ref[...] + (1.0 - ADAM_B2) * (g * g)
        go_ref[...] = g
        mo_ref[...] = mn
        vo_ref[...] = vn
        d_ref[...] = -ADAM_LR * ((mn / c1) / (jnp.sqrt(vn / c2) + ADAM_EPS) + ADAM_WD * w_ref[...])

    row = pl.BlockSpec((tm, c), lambda i: (i, 0))
    out = jax.ShapeDtypeStruct((r, c), F32)
    return pl.pallas_call(
        body, name=name, grid=(r // tm,),
        in_specs=[pl.BlockSpec((s, tm, c), lambda i: (0, i, 0)), row, row, row],
        out_specs=[row, row, row, row], out_shape=[out, out, out, out], compiler_params=_cp("parallel"),
    )(gslots, w, m, v)


def _rows(a):
    flat = a.reshape(-1)
    pad = (-flat.shape[0]) % PACK_W
    if pad:
        flat = jnp.concatenate([flat, jnp.zeros((pad,), flat.dtype)])
    return flat.reshape(-1, PACK_W)


def _pack(arrays, row_multiple):
    parts, layout, off = [], [], 0
    for a in arrays:
        rw = _rows(a)
        parts.append(rw)
        layout.append((off, rw.shape[0], a.shape))
        off += rw.shape[0]
    pad = (-off) % row_multiple
    if pad:
        parts.append(jnp.zeros((pad, PACK_W), parts[0].dtype))
    return jnp.concatenate(parts, axis=0), layout


def _unpack(packed, layout):
    out = []
    for off, nrows, shape in layout:
        n = int(np.prod(shape))
        out.append(packed[off:off + nrows].reshape(-1)[:n].reshape(shape))
    return out


def _shard_as_rows(name, shard):
    if name in COL_SHARDED:
        shard = shard.transpose(0, 2, 1)
    return shard.reshape(-1, PACK_W)


def _rows_as_shard(name, rows, shape):
    if name in COL_SHARDED:
        lead, k, ns = shape
        return rows.reshape(lead, ns, k).transpose(0, 2, 1)
    return rows.reshape(shape)


def _row_tile(r):
    return next(tm for tm in (512, 256, 128, 64, 32, 16, 8) if r % tm == 0)


def _ssd_consts(dt_bias, a_log, d_skip):
    pad = LANES - SSD_HEADS
    bias = jnp.pad(dt_bias, (0, pad)).reshape(1, LANES)
    a_neg = jnp.pad(-jnp.exp(a_log), (0, pad)).reshape(1, LANES)
    d_exp = jnp.repeat(d_skip, SSD_HEAD_DIM).reshape(1, SSD_D_INNER)
    return bias, a_neg, d_exp


def _group_layouts(acum):
    t = acum.shape[0]
    a = acum[:, :SSD_HEADS].reshape(t, SSD_GROUPS, HEADS_PER_GROUP)
    return a.transpose(1, 0, 2), a.transpose(1, 2, 0)


def _ssd_fwd(x, p, carried=None):
    hn = _rmsnorm(x, p["mix_norm"], "rmsnorm_fwd")
    proj = _matmul(hn, p["w_in"], "nt", F32, "ssd_in_fwd", tm=TOKEN_ROWS, tn=896, tk=1024)
    act = _ssd_conv_fwd(proj, p["conv_w"], p["conv_b"], "ssd_conv_fwd")
    bias, a_neg, d_exp = _ssd_consts(p["dt_bias"], p["a_log"], p["d"])
    expand = _head_expand()
    xdt, dt, acum = _ssd_dt_fwd(proj, act, bias, a_neg, expand, "ssd_dt_fwd")
    acum_g, acum_gt = _group_layouts(acum)
    if carried is None:
        (y, states), landed = _ssd_scan_fwd(xdt, act, acum_g, acum_gt, "ssd_scan_fwd"), None
    else:
        y, states, landed = _ssd_scan_fwd(xdt, act, acum_g, acum_gt, "ssd_scan_fwd_carrying_gather", carried)
    yn = _gated_norm_fwd(y, act, proj, d_exp, p["norm"], "ssd_gnorm_fwd")
    x_new = _matmul(yn, p["w_out"], "nn", F32, "ssd_out_fwd", add=x, tm=TOKEN_ROWS, tn=1024, tk=2048)
    saved = dict(x=x, hn=hn, proj=proj, act=act, xdt=xdt, dt=dt, acum_g=acum_g, acum_gt=acum_gt, y=y, states=states, yn=yn)
    return x_new, saved, landed


def _ssd_bwd(dx, p, s, carried_of=None, last_of=None):
    bias, a_neg, d_exp = _ssd_consts(p["dt_bias"], p["a_log"], p["d"])
    expand = _head_expand()
    dyn = _matmul(dx, p["w_out"], "nt", F32, "ssd_out_dgrad", tm=TOKEN_ROWS, tn=1024, tk=1024)
    g_w_out = _matmul(s["yn"], dx, "tn", MXU_DTYPE, "ssd_out_wgrad", tm=1024, tn=1024, tk=TOKEN_ROWS)
    dyy, dz, g_norm = _gated_norm_bwd(s["y"], s["act"], s["proj"], d_exp, p["norm"], dyn, "ssd_gnorm_bwd")
    scan_args = (s["xdt"], s["act"], s["acum_g"], s["acum_gt"], s["states"], dyy)
    if carried_of is None:
        (dxdt, dbm, dcm, dacol, darow), landed = _ssd_scan_bwd(*scan_args, "ssd_scan_bwd"), None
    else:
        dxdt, dbm, dcm, dacol, darow, landed = _ssd_scan_bwd(*scan_args, "ssd_scan_bwd_carrying_grads", carried_of(g_w_out))
    t = dx.shape[0]
    dacum = dacol.transpose(1, 0, 2).reshape(t, SSD_HEADS) + darow.transpose(2, 0, 1).reshape(t, SSD_HEADS)
    dacum = jnp.pad(dacum, ((0, 0), (0, LANES - SSD_HEADS)))
    dxs, draw, g_a, g_bias, g_dexp = _ssd_dt_bwd(s["proj"], s["act"], s["dt"], dxdt, dyy, dacum, bias, a_neg, d_exp,
                                                  expand, expand.T, "ssd_dt_bwd")
    dact = jnp.concatenate([dxs, dbm, dcm], axis=1)
    dxbc, g_conv_w, g_conv_b = _ssd_conv_bwd(s["proj"], p["conv_w"], p["conv_b"], dact, "ssd_conv_bwd")
    dproj = jnp.concatenate([dz, dxbc, draw], axis=1)
    g_w_in = _matmul(dproj, s["hn"], "tn", MXU_DTYPE, "ssd_in_wgrad", tm=896, tn=1024, tk=TOKEN_ROWS)
    if last_of is None:
        dhn, landed_last = _matmul(dproj, p["w_in"], "nn", F32, "ssd_in_dgrad", tm=TOKEN_ROWS, tn=1024, tk=896), None
    else:
        dhn, landed_last = _matmul(dproj, p["w_in"], "nn", F32, "ssd_in_dgrad_carrying_grads", tm=TOKEN_ROWS, tn=1024, tk=896,
                                   carried=last_of(g_w_in[:SSD_IN_DIM]))
    dx_new, g_mix = _rmsnorm_bwd(s["x"], p["mix_norm"], dhn, dx, "rmsnorm_bwd")
    grads = dict(w_in=g_w_in[:SSD_IN_DIM], w_out=g_w_out, conv_w=g_conv_w, conv_b=g_conv_b.reshape(-1),
                 dt_bias=g_bias[0, :SSD_HEADS], a_log=(g_a * a_neg)[0, :SSD_HEADS],
                 d=g_dexp.reshape(SSD_HEADS, SSD_HEAD_DIM).sum(axis=1), norm=g_norm.reshape(-1), mix_norm=g_mix.reshape(-1))
    return dx_new, grads, landed, landed_last


def _sb_fwd(x, p, carried=None):
    hn = _rmsnorm(x, p["mix_norm"], "rmsnorm_fwd")
    qkv = _matmul(hn, p["w_qkv"], "nt", MXU_DTYPE, "sb_qkv_fwd", tm=TOKEN_ROWS, tn=1024, tk=1024)
    if carried is None:
        o, landed = _sb_attention_fwd(qkv, "sb_attn_fwd"), None
    else:
        o, landed = _sb_attention_fwd(qkv, "sb_attn_fwd_carrying_gather", carried)
    x_new = _matmul(o, p["w_out"], "nn", F32, "sb_out_fwd", add=x, tm=TOKEN_ROWS, tn=1024, tk=1024)
    return x_new, dict(x=x, hn=hn, qkv=qkv, o=o), landed


def _sb_bwd(dx, p, s, carried_of=None):
    do = _matmul(dx, p["w_out"], "nt", MXU_DTYPE, "sb_out_dgrad", tm=TOKEN_ROWS, tn=1024, tk=1024)
    g_w_out = _matmul(s["o"], dx, "tn", MXU_DTYPE, "sb_out_wgrad", tm=1024, tn=1024, tk=TOKEN_ROWS)
    if carried_of is None:
        (dq, dk, dv), landed = _sb_attention_bwd(s["qkv"], do, "sb_attn_bwd"), None
    else:
        dq, dk, dv, landed = _sb_attention_bwd(s["qkv"], do, "sb_attn_bwd_carrying_grads", carried_of(g_w_out))
    dqkv = jnp.concatenate([dq, dk, dv], axis=1)
    dhn = _matmul(dqkv, p["w_qkv"], "nn", F32, "sb_qkv_dgrad", tm=TOKEN_ROWS, tn=1024, tk=1024)
    g_w_qkv = _matmul(dqkv, s["hn"], "tn", MXU_DTYPE, "sb_qkv_wgrad", tm=1024, tn=1024, tk=TOKEN_ROWS)
    dx_new, g_mix = _rmsnorm_bwd(s["x"], p["mix_norm"], dhn, dx, "rmsnorm_bwd")
    return dx_new, dict(w_qkv=g_w_qkv, w_out=g_w_out, mix_norm=g_mix.reshape(-1)), landed


def _ffn_fwd(x, p):
    hn = _rmsnorm(x, p["ffn_norm"], "rmsnorm_fwd")
    proj = _matmul(hn, p["w_in"], "nt", F32, "ffn_in_fwd", tm=TOKEN_ROWS, tn=1408, tk=1024)
    act = _ffn_conv_fwd(proj, p["conv_w"], p["conv_b"], "ffn_conv_fwd")
    x_new = _matmul(act, p["w_out"], "nn", F32, "ffn_out_fwd", add=x, tm=TOKEN_ROWS, tn=1024, tk=1408)
    return x_new, dict(x=x, hn=hn, proj=proj, act=act)


def _ffn_bwd(dx, p, s):
    dact = _matmul(dx, p["w_out"], "nt", F32, "ffn_out_dgrad", tm=TOKEN_ROWS, tn=1408, tk=1024)
    g_w_out = _matmul(s["act"], dx, "tn", MXU_DTYPE, "ffn_out_wgrad", tm=1408, tn=1024, tk=TOKEN_ROWS)
    dpg, dpu, dwg, dwu, dbg, dbu = _ffn_conv_bwd(s["proj"], p["conv_w"], p["conv_b"], dact, "ffn_conv_bwd")
    dproj = jnp.concatenate([dpg, dpu], axis=1)
    dhn = _matmul(dproj, p["w_in"], "nn", F32, "ffn_in_dgrad", tm=TOKEN_ROWS, tn=1024, tk=1408)
    g_w_in = _matmul(dproj, s["hn"], "tn", MXU_DTYPE, "ffn_in_wgrad", tm=1408, tn=1024, tk=TOKEN_ROWS)
    dx_new, g_norm = _rmsnorm_bwd(s["x"], p["ffn_norm"], dhn, dx, "rmsnorm_bwd")
    grads = dict(w_in=g_w_in, w_out=g_w_out, conv_w=jnp.concatenate([dwg, dwu], axis=1),
                 conv_b=jnp.concatenate([dbg, dbu], axis=1).reshape(-1), ffn_norm=g_norm.reshape(-1))
    return dx_new, grads


ADD_ROWS = 256
BIG = ["ssd_w_in", "sb_w_qkv", "ffn_w_in", "ssd_w_out", "sb_w_out", "ffn_w_out"]
LAYER_PIECES = [[("ssd_w_in", 0), ("ssd_w_out", 0), ("ffn_w_in", 0), ("ffn_w_out", 0)],
                [("sb_w_qkv", 0), ("sb_w_out", 0), ("ffn_w_in", 1), ("ffn_w_out", 1)],
                [("ssd_w_in", 1), ("ssd_w_out", 1), ("ffn_w_in", 2), ("ffn_w_out", 2)],
                [("sb_w_qkv", 1), ("sb_w_out", 1), ("ffn_w_in", 3), ("ffn_w_out", 3)]]
GRAD_SETS = {3: [("ffn_w_in", 3), ("ffn_w_out", 3), ("sb_w_out", 1)],
             1: [("sb_w_qkv", 1), ("ssd_w_out", 1), ("ffn_w_in", 2), ("ffn_w_out", 2), ("ffn_w_in", 1), ("ffn_w_out", 1),
                 ("sb_w_out", 0), ("ssd_w_in", 1)],
             0: [("sb_w_qkv", 0), ("ffn_w_in", 0), ("ffn_w_out", 0), ("ssd_w_out", 0)],
             "end": [("ssd_w_in", 0)]}
GATHER_SETS = {"early": [("ssd_w_out", 0), ("ssd_w_in", 0)],
               0: [("ffn_w_in", 0), ("ffn_w_out", 0), ("sb_w_qkv", 0), ("sb_w_out", 0)],
               1: [("ffn_w_in", 1), ("ffn_w_out", 1), ("ssd_w_out", 1), ("ffn_w_in", 2), ("ffn_w_out", 2)] + LAYER_PIECES[3]
                  + [("ssd_w_in", 1)]}
COL_SHARDED = {"ssd_w_in": 2, "sb_w_qkv": 2, "ffn_w_in": 4}
CONV = ["ssd_conv_w", "ffn_conv_w"]
SMALL = ["mix_norm", "ffn_norm", "final_norm", "ssd_conv_b", "ssd_dt_bias", "ssd_a_log", "ssd_d", "ssd_norm", "ffn_conv_b"]
WEIGHTS = ["mix_norm", "ffn_norm", "final_norm", "ssd_w_in", "ssd_conv_w", "ssd_conv_b", "ssd_dt_bias", "ssd_a_log", "ssd_d",
           "ssd_norm", "ssd_w_out", "sb_w_qkv", "sb_w_out", "ffn_w_in", "ffn_conv_w", "ffn_conv_b", "ffn_w_out"]


def _step(x, loss_target, w, m, v):
    x = x.reshape(x.shape[-2], x.shape[-1])
    target = loss_target.reshape(x.shape)
    dev = 4 * lax.axis_index("x") + 2 * lax.axis_index("y") + lax.axis_index("c")
    core = lax.axis_index("c")

    shard_rows = {n: _shard_as_rows(n, w[n].astype(MXU_DTYPE)) for n in BIG}
    per_shard = {n: shard_rows[n].shape[0] // w[n].shape[0] for n in BIG}

    def layout(pieces):
        where, off = {}, 0
        for n, l in pieces:
            where[(n, l)] = (off, per_shard[n])
            off += per_shard[n]
        return where

    def pack_pieces(pieces, rows_of):
        return jnp.concatenate([rows_of(piece) for piece in pieces], axis=-2)

    def shard_piece(piece):
        n, l = piece
        return shard_rows[n][l * per_shard[n]:(l + 1) * per_shard[n]]

    full = {}

    def unpack_weights(gathered, where):
        for (n, l), (off, rows) in where.items():
            mat = gathered[:, off:off + rows].reshape(N_DEV * rows, PACK_W)
            if n == "ssd_w_in":
                mat = jnp.pad(mat, ((0, SSD_IN_PAD - SSD_IN_DIM), (0, 0)))
            full[(n, l)] = mat

    unpack_weights(_all_gather(pack_pieces(GATHER_SETS["early"], shard_piece), "gather_weights_early"), layout(GATHER_SETS["early"]))
    conv_pack, conv_layout = _pack([w[n] for n in CONV], 8)
    conv_all = _all_gather(conv_pack, "gather_conv_taps")
    for n, (off, nrows, shape) in zip(CONV, conv_layout):
        parts = [_unpack(conv_all[j], conv_layout)[CONV.index(n)] for j in range(N_DEV)]
        full[n] = jnp.concatenate(parts, axis=-1)

    def ssd_params(j):
        return dict(mix_norm=w["mix_norm"][2 * j], w_in=full[("ssd_w_in", j)], conv_w=full["ssd_conv_w"][j],
                    conv_b=w["ssd_conv_b"][j], dt_bias=w["ssd_dt_bias"][j], a_log=w["ssd_a_log"][j], d=w["ssd_d"][j],
                    norm=w["ssd_norm"][j], w_out=full[("ssd_w_out", j)])

    def sb_params(j):
        return dict(mix_norm=w["mix_norm"][2 * j + 1], w_qkv=full[("sb_w_qkv", j)], w_out=full[("sb_w_out", j)])

    def ffn_params(i):
        return dict(ffn_norm=w["ffn_norm"][i], w_in=full[("ffn_w_in", i)], conv_w=full["ffn_conv_w"][i],
                    conv_b=w["ffn_conv_b"][i], w_out=full[("ffn_w_out", i)])

    saved = []
    for i in range(DEPTH):
        mixer_fwd, params = (_ssd_fwd, ssd_params) if i % 2 == 0 else (_sb_fwd, sb_params)
        if i in GATHER_SETS:
            x, s_mix, arrived = mixer_fwd(x, params(i // 2), carried=(pack_pieces(GATHER_SETS[i], shard_piece), False))
            unpack_weights(arrived, layout(GATHER_SETS[i]))
        else:
            x, s_mix, _ = mixer_fwd(x, params(i // 2))
        x, s_ffn = _ffn_fwd(x, ffn_params(i))
        saved.append((s_mix, s_ffn))
    dx, g_final, loss_part = _final_norm_loss(x, w["final_norm"], target, "final_norm_loss")

    piece_grad = {}

    def grad_piece(piece):
        g = piece_grad[piece]
        return g.reshape(N_DEV, g.shape[0] // N_DEV, PACK_W)

    def carried_set(pieces, own_piece):
        def make(g_w_out):
            piece_grad[own_piece] = g_w_out
            return pack_pieces(pieces, grad_piece), True
        return make

    g_mix, g_ffn, g_ssd, g_sb = [None] * DEPTH, [None] * DEPTH, [None] * 2, [None] * 2
    landed = {}
    for i in reversed(range(DEPTH)):
        s_mix, s_ffn = saved[i]
        dx, g_ffn[i] = _ffn_bwd(dx, ffn_params(i), s_ffn)
        piece_grad[("ffn_w_in", i)], piece_grad[("ffn_w_out", i)] = g_ffn[i]["w_in"], g_ffn[i]["w_out"]
        j = i // 2
        if i % 2 == 0:
            carried_of = carried_set(GRAD_SETS[i], ("ssd_w_out", j)) if i in GRAD_SETS else None
            last_of = carried_set(GRAD_SETS["end"], ("ssd_w_in", j)) if i == 0 else None
            dx, g_ssd[j], landed[i], landed_last = _ssd_bwd(dx, ssd_params(j), s_mix, carried_of, last_of)
            if i == 0:
                landed["end"] = landed_last
            g_mix[i] = g_ssd[j]["mix_norm"]
            piece_grad[("ssd_w_in", j)], piece_grad[("ssd_w_out", j)] = g_ssd[j]["w_in"], g_ssd[j]["w_out"]
        else:
            dx, g_sb[j], landed[i] = _sb_bwd(dx, sb_params(j), s_mix, carried_set(GRAD_SETS[i], ("sb_w_out", j)))
            g_mix[i] = g_sb[j]["mix_norm"]
            piece_grad[("sb_w_qkv", j)] = g_sb[j]["w_qkv"]
    grad_x = dx.reshape(1, *dx.shape)

    summed = {}
    for key, pieces in GRAD_SETS.items():
        total = _sum_slots(landed[key], "grads_sum_landed")
        for piece, (off, rows) in layout(pieces).items():
            summed[piece] = total[off:off + rows]
    big_res = [dict() for _ in range(4)]
    for n in BIG:
        lead, rows, cols = w[n].shape
        g_rows = jnp.concatenate([summed[(n, l)] for l in range(lead)], axis=0)
        g_nat = _rows_as_shard(n, g_rows, w[n].shape).reshape(1, lead * rows, cols)
        two_d = (lead * rows, cols)
        outs = _adamw(g_nat, w[n].reshape(two_d), m[n].reshape(two_d), v[n].reshape(two_d), "adamw_" + n)
        for kind in range(4):
            big_res[kind][n] = outs[kind].reshape(w[n].shape)

    small_g = {
        "mix_norm": jnp.stack(g_mix), "ffn_norm": jnp.stack([g["ffn_norm"] for g in g_ffn]), "final_norm": g_final.reshape(-1),
        "ssd_conv_b": jnp.stack([g["conv_b"] for g in g_ssd]), "ssd_dt_bias": jnp.stack([g["dt_bias"] for g in g_ssd]),
        "ssd_a_log": jnp.stack([g["a_log"] for g in g_ssd]), "ssd_d": jnp.stack([g["d"] for g in g_ssd]),
        "ssd_norm": jnp.stack([g["norm"] for g in g_ssd]), "ffn_conv_b": jnp.stack([g["conv_b"] for g in g_ffn]),
    }
    conv_g = {"ssd_conv_w": jnp.stack([g["conv_w"] for g in g_ssd]), "ffn_conv_w": jnp.stack([g["conv_w"] for g in g_ffn])}
    extra = [conv_g[n] for n in CONV] + [loss_part]
    small_pack, small_layout = _pack([small_g[n] for n in SMALL] + extra, 8)
    small_all = _all_gather(small_pack, "gather_small_grads")
    zeros_like = [jnp.zeros(a.shape, F32) for a in extra]
    sw, _ = _pack([w[n] for n in SMALL] + zeros_like, 8)
    sm, _ = _pack([m[n] for n in SMALL] + zeros_like, 8)
    sv, _ = _pack([v[n] for n in SMALL] + [jnp.ones(a.shape, F32) for a in extra], 8)
    small_out = _adamw(small_all, sw, sm, sv, "adamw_replicated")
    small_res = [_unpack(o, small_layout) for o in small_out]
    summed = small_res[0]
    loss = summed[-1][0, 0]
    conv_shard_g = []
    for n, gsum in zip(CONV, summed[len(SMALL):len(SMALL) + len(CONV)]):
        ns = w[n].shape[-1]
        conv_shard_g.append(lax.dynamic_slice_in_dim(gsum, dev * ns, ns, axis=2))
    cg, conv_sh_layout = _pack(conv_shard_g, 8)
    cw, _ = _pack([w[n] for n in CONV], 8)
    cm_, _ = _pack([m[n] for n in CONV], 8)
    cv, _ = _pack([v[n] for n in CONV], 8)
    conv_out = _adamw(cg.reshape(1, *cg.shape), cw, cm_, cv, "adamw_conv_taps")
    conv_res = [dict(zip(CONV, _unpack(o, conv_sh_layout))) for o in conv_out]

    def pick(kind, n):
        if n in BIG:
            return big_res[kind][n]
        if n in CONV:
            return conv_res[kind][n]
        return small_res[kind][SMALL.index(n)]

    outs = [loss, grad_x]
    for kind in range(4):
        outs += [pick(kind, n) for n in WEIGHTS]
    return tuple(outs)


def kernel(x, mix_norm, ffn_norm, final_norm, ssd_w_in, ssd_conv_w, ssd_conv_b, ssd_dt_bias, ssd_a_log, ssd_d, ssd_norm, ssd_w_out, sb_w_qkv, sb_w_out, ffn_w_in, ffn_conv_w, ffn_conv_b, ffn_w_out, loss_target, m_mix_norm, m_ffn_norm, m_final_norm, m_ssd_w_in, m_ssd_conv_w, m_ssd_conv_b, m_ssd_dt_bias, m_ssd_a_log, m_ssd_d, m_ssd_norm, m_ssd_w_out, m_sb_w_qkv, m_sb_w_out, m_ffn_w_in, m_ffn_conv_w, m_ffn_conv_b, m_ffn_w_out, v_mix_norm, v_ffn_norm, v_final_norm, v_ssd_w_in, v_ssd_conv_w, v_ssd_conv_b, v_ssd_dt_bias, v_ssd_a_log, v_ssd_d, v_ssd_norm, v_ssd_w_out, v_sb_w_qkv, v_sb_w_out, v_ffn_w_in, v_ffn_conv_w, v_ffn_conv_b, v_ffn_w_out):
    w = dict(mix_norm=mix_norm, ffn_norm=ffn_norm, final_norm=final_norm, ssd_w_in=ssd_w_in, ssd_conv_w=ssd_conv_w,
             ssd_conv_b=ssd_conv_b, ssd_dt_bias=ssd_dt_bias, ssd_a_log=ssd_a_log, ssd_d=ssd_d, ssd_norm=ssd_norm,
             ssd_w_out=ssd_w_out, sb_w_qkv=sb_w_qkv, sb_w_out=sb_w_out, ffn_w_in=ffn_w_in, ffn_conv_w=ffn_conv_w,
             ffn_conv_b=ffn_conv_b, ffn_w_out=ffn_w_out)
    m = dict(mix_norm=m_mix_norm, ffn_norm=m_ffn_norm, final_norm=m_final_norm, ssd_w_in=m_ssd_w_in, ssd_conv_w=m_ssd_conv_w,
             ssd_conv_b=m_ssd_conv_b, ssd_dt_bias=m_ssd_dt_bias, ssd_a_log=m_ssd_a_log, ssd_d=m_ssd_d, ssd_norm=m_ssd_norm,
             ssd_w_out=m_ssd_w_out, sb_w_qkv=m_sb_w_qkv, sb_w_out=m_sb_w_out, ffn_w_in=m_ffn_w_in, ffn_conv_w=m_ffn_conv_w,
             ffn_conv_b=m_ffn_conv_b, ffn_w_out=m_ffn_w_out)
    v = dict(mix_norm=v_mix_norm, ffn_norm=v_ffn_norm, final_norm=v_final_norm, ssd_w_in=v_ssd_w_in, ssd_conv_w=v_ssd_conv_w,
             ssd_conv_b=v_ssd_conv_b, ssd_dt_bias=v_ssd_dt_bias, ssd_a_log=v_ssd_a_log, ssd_d=v_ssd_d, ssd_norm=v_ssd_norm,
             ssd_w_out=v_ssd_w_out, sb_w_qkv=v_sb_w_qkv, sb_w_out=v_sb_w_out, ffn_w_in=v_ffn_w_in, ffn_conv_w=v_ffn_conv_w,
             ffn_conv_b=v_ffn_conv_b, ffn_w_out=v_ffn_w_out)
    return _step(x, loss_target, w, m, v)
```
